```python
import math
import jax, jax.numpy as jnp
from jax import lax
import numpy as np

D_MODEL = 1024
BATCH = 4
SEQ = 8192
DEPTH = 2

HEAD_DIM = 64
ROT_DIM = HEAD_DIM // 4
ROPE_THETA = 500000.0
Q_BLOCK = 128
NEG = -1e30
LN_EPS = 1e-5

A_HEADS = 8
IDX_HEADS = 4
IDX_DIM = 64
DSA_TOPK = 256
B_HEADS = 8
B_KV_GROUPS = 2
CMP_LEN = 32
CMP_STRIDE = 16
CMP_HIDDEN = 128
SLC_LEN = 64
SLC_TOPN = 16
WINDOW = 512
FORCED_BOOST = 1e6
C_HEADS = 8
D_FF = 2816
N_EXPERTS = 8
TOP_K = 2
D_FF_EXPERT = 3584

N_EVEN = (DEPTH + 1) // 2
N_ODD = DEPTH // 2
DEEPNORM_ALPHA = (2 * DEPTH) ** 0.25
DEEPNORM_BETA = (8 * DEPTH) ** -0.25

AB_LAYOUT = (
    ("q_a", A_HEADS * HEAD_DIM), ("k_a", HEAD_DIM), ("v_a", HEAD_DIM),
    ("q_idx", IDX_HEADS * IDX_DIM), ("k_idx", IDX_DIM), ("w_idx", IDX_HEADS),
    ("q_b", B_HEADS * HEAD_DIM),
    ("k_cmp", B_KV_GROUPS * HEAD_DIM), ("v_cmp", B_KV_GROUPS * HEAD_DIM),
    ("k_slc", B_KV_GROUPS * HEAD_DIM), ("v_slc", B_KV_GROUPS * HEAD_DIM),
    ("k_win", B_KV_GROUPS * HEAD_DIM), ("v_win", B_KV_GROUPS * HEAD_DIM),
    ("gate_b", 3 * B_HEADS),
)
C_LAYOUT = (("q_c", C_HEADS * 2 * HEAD_DIM), ("k_c", C_HEADS * 2 * HEAD_DIM), ("v_c", C_HEADS * 2 * HEAD_DIM))

kernel_name = "hybrid_dsa_nsa_diffattn_moe_deepnorm"


def _split(h, layout):
    offs = np.cumsum([w for _, w in layout])[:-1].tolist()
    return jnp.split(h, offs, axis=-1)


def _col_scale(layout):
    return np.concatenate([np.full((w,), DEEPNORM_BETA if n.startswith("v_") else 1.0, np.float32) for n, w in layout])


def layer_norm(x, g, b):
    xf = x.astype(jnp.float32)
    mu = xf.mean(-1, keepdims=True)
    var = jnp.square(xf - mu).mean(-1, keepdims=True)
    return ((xf - mu) * lax.rsqrt(var + LN_EPS) * g + b).astype(x.dtype)


def rms_norm(x, g):
    xf = x.astype(jnp.float32)
    return (xf * lax.rsqrt(jnp.mean(xf * xf, -1, keepdims=True) + LN_EPS) * g).astype(x.dtype)


def rope_tables(positions):
    inv_freq = ROPE_THETA ** (-jnp.arange(0, ROT_DIM, 2, dtype=jnp.float32) / ROT_DIM)
    ang = positions.astype(jnp.float32)[..., None] * inv_freq
    return jnp.cos(ang), jnp.sin(ang)


def apply_rope(x, cos, sin):
    shape = cos.shape[:2] + (1,) * (x.ndim - 3) + cos.shape[-1:]
    c = cos.reshape(shape).astype(x.dtype)
    s = sin.reshape(shape).astype(x.dtype)
    half = ROT_DIM // 2
    x1, x2 = x[..., :half], x[..., half:ROT_DIM]
    return jnp.concatenate([x1 * c - x2 * s, x2 * c + x1 * s, x[..., ROT_DIM:]], axis=-1)


def masked_softmax(logits, mask):
    return jax.nn.softmax(jnp.where(mask, logits.astype(jnp.float32), NEG), axis=-1)


def sweep_blocks(fn, seq):
    out = lax.map(fn, jnp.arange(seq // Q_BLOCK))
    out = jnp.moveaxis(out, 0, 1)
    return out.reshape((out.shape[0], seq) + out.shape[3:])


def dsa_attention(q_a, k_a, v_a, q_idx, k_idx, w_idx):
    seq = q_a.shape[1]
    topk = min(DSA_TOPK, seq // 4)
    key_pos = jnp.arange(seq)
    gather = jax.vmap(lambda table, idx: table[idx])

    def block(i):
        q0 = i * Q_BLOCK
        t = q0 + jnp.arange(Q_BLOCK)
        qi = lax.dynamic_slice_in_dim(q_idx, q0, Q_BLOCK, axis=1)
        wi = lax.dynamic_slice_in_dim(w_idx, q0, Q_BLOCK, axis=1)
        qa = lax.dynamic_slice_in_dim(q_a, q0, Q_BLOCK, axis=1)
        rel = jax.nn.relu(jnp.einsum('bqhd,bsd->bqhs', qi, k_idx))
        score = jnp.einsum('bqh,bqhs->bqs', wi, rel).astype(jnp.float32)
        causal = key_pos[None, :] <= t[:, None]
        score = jnp.where(causal[None], score, NEG)
        _, sel = lax.top_k(score, topk)
        k_sel = gather(k_a, sel)
        v_sel = gather(v_a, sel)
        logits = jnp.einsum('bqhd,bqkd->bqhk', qa, k_sel) * HEAD_DIM ** -0.5
        valid = (sel <= t[None, :, None])[:, :, None, :]
        p = masked_softmax(logits, valid).astype(v_a.dtype)
        return jnp.einsum('bqhk,bqkd->bqhd', p, v_sel)

    return sweep_blocks(block, seq)


def nsa_compress(kv, pe, w1, w2):
    bsz, seq, groups, dim = kv.shape
    n_cmp = (seq - CMP_LEN) // CMP_STRIDE + 1
    idx = np.arange(n_cmp)[:, None] * CMP_STRIDE + np.arange(CMP_LEN)[None, :]
    blocks = kv[:, idx] + pe[None, None, :, None, :]
    blocks = jnp.moveaxis(blocks, 3, 2).reshape(bsz, n_cmp, groups, CMP_LEN * dim)
    return jax.nn.silu(blocks @ w1) @ w2


def nsa_attention(q_b, k_c, v_c, k_s, v_s, k_w, v_w, gates):
    bsz, seq = q_b.shape[:2]
    G = B_KV_GROUPS
    R = B_HEADS // G
    n_cmp = k_c.shape[1]
    n_slc = seq // SLC_LEN
    n_sel = min(SLC_TOPN, n_slc)
    scale = HEAD_DIM ** -0.5
    c_start = np.arange(n_cmp) * CMP_STRIDE
    cmp_end = c_start + CMP_LEN - 1
    s_start = np.arange(n_slc) * SLC_LEN
    overlap = jnp.asarray(((c_start[:, None] < s_start[None, :] + SLC_LEN)
                           & (c_start[:, None] + CMP_LEN > s_start[None, :])).astype(np.float32))
    ks_blk = k_s.reshape(bsz, n_slc, SLC_LEN, G, HEAD_DIM).transpose(0, 3, 1, 2, 4)
    vs_blk = v_s.reshape(bsz, n_slc, SLC_LEN, G, HEAD_DIM).transpose(0, 3, 1, 2, 4)
    gather2 = jax.vmap(jax.vmap(lambda table, idx: table[idx]))
    kw_pad = jnp.pad(k_w, ((0, 0), (WINDOW, 0), (0, 0), (0, 0)))
    vw_pad = jnp.pad(v_w, ((0, 0), (WINDOW, 0), (0, 0), (0, 0)))
    win_off = jnp.arange(WINDOW + Q_BLOCK) - WINDOW
    blk_ids = jnp.arange(n_slc)

    def block(i):
        q0 = i * Q_BLOCK
        t = q0 + jnp.arange(Q_BLOCK)
        q = lax.dynamic_slice_in_dim(q_b, q0, Q_BLOCK, 1).reshape(bsz, Q_BLOCK, G, R, HEAD_DIM)
        g = lax.dynamic_slice_in_dim(gates, q0, Q_BLOCK, 1)
        lc = jnp.einsum('bqgrd,bcgd->bqgrc', q, k_c) * scale
        mc = (cmp_end[None, :] <= t[:, None])[None, :, None, None, :]
        p_c = jnp.where(mc, masked_softmax(lc, mc), 0.0)
        o_c = jnp.einsum('bqgrc,bcgd->bqgrd', p_c.astype(v_c.dtype), v_c)
        imp = jnp.einsum('bqgrc,cn->bqgn', p_c, overlap)
        blk_t = t // SLC_LEN
        forced = (blk_ids[None, :] == 0) | (blk_ids[None, :] == blk_t[:, None]) | (blk_ids[None, :] == blk_t[:, None] - 1)
        admissible = s_start[None, :] <= t[:, None]
        imp = jnp.where(forced[None, :, None, :], FORCED_BOOST, imp)
        imp = jnp.where(admissible[None, :, None, :], imp, NEG)
        _, sel = lax.top_k(imp, n_sel)
        sel_t = sel.transpose(0, 2, 1, 3)
        k_sel = gather2(ks_blk, sel_t).reshape(bsz, G, Q_BLOCK, n_sel * SLC_LEN, HEAD_DIM)
        v_sel = gather2(vs_blk, sel_t).reshape(bsz, G, Q_BLOCK, n_sel * SLC_LEN, HEAD_DIM)
        tok = (sel_t[..., None] * SLC_LEN + jnp.arange(SLC_LEN)).reshape(bsz, G, Q_BLOCK, n_sel * SLC_LEN)
        ms = (tok <= t[None, None, :, None]).transpose(0, 2, 1, 3)[:, :, :, None, :]
        ls = jnp.einsum('bqgrd,bgqkd->bqgrk', q, k_sel) * scale
        p_s = masked_softmax(ls, ms).astype(v_s.dtype)
        o_s = jnp.einsum('bqgrk,bgqkd->bqgrd', p_s, v_sel)
        kw = lax.dynamic_slice_in_dim(kw_pad, q0, WINDOW + Q_BLOCK, 1)
        vw = lax.dynamic_slice_in_dim(vw_pad, q0, WINDOW + Q_BLOCK, 1)
        s_pos = q0 + win_off
        dist = t[:, None] - s_pos[None, :]
        mw = ((dist >= 0) & (dist < WINDOW) & (s_pos[None, :] >= 0))[None, :, None, None, :]
        lw = jnp.einsum('bqgrd,bkgd->bqgrk', q, kw) * scale
        p_w = masked_softmax(lw, mw).astype(v_w.dtype)
        o_w = jnp.einsum('bqgrk,bkgd->bqgrd', p_w, vw)
        o = jnp.stack([o_c, o_s, o_w], axis=-1).reshape(bsz, Q_BLOCK, B_HEADS, HEAD_DIM, 3)
        return jnp.einsum('bqhdn,bqhn->bqhd', o, g)

    return sweep_blocks(block, seq)


def ab_mixer(x, positions, cos, sin, w_in, pe_k, pe_v, ck1, ck2, cv1, cv2, w_out):
    bsz, seq, _ = x.shape
    (q_a, k_a, v_a, q_idx, k_idx, w_idx, q_b, k_cmp, v_cmp, k_slc, v_slc,
     k_win, v_win, g_b) = _split(x @ w_in, AB_LAYOUT)
    q_a = apply_rope(q_a.reshape(bsz, seq, A_HEADS, HEAD_DIM), cos, sin)
    k_a = apply_rope(k_a, cos, sin)
    q_idx = apply_rope(q_idx.reshape(bsz, seq, IDX_HEADS, IDX_DIM), cos, sin)
    k_idx = apply_rope(k_idx, cos, sin)
    w_idx = w_idx * (IDX_HEADS * IDX_DIM) ** -0.5
    o_a = dsa_attention(q_a, k_a, v_a, q_idx, k_idx, w_idx)
    kv_shape = (bsz, seq, B_KV_GROUPS, HEAD_DIM)
    q_b = apply_rope(q_b.reshape(bsz, seq, B_HEADS, HEAD_DIM), cos, sin)
    n_cmp = (seq - CMP_LEN) // CMP_STRIDE + 1
    cmp_end = np.arange(n_cmp) * CMP_STRIDE + CMP_LEN - 1
    cos_c, sin_c = rope_tables(positions[:, cmp_end])
    k_c = apply_rope(nsa_compress(k_cmp.reshape(kv_shape), pe_k, ck1, ck2), cos_c, sin_c)
    v_c = nsa_compress(v_cmp.reshape(kv_shape), pe_v, cv1, cv2)
    k_s = apply_rope(k_slc.reshape(kv_shape), cos, sin)
    k_w = apply_rope(k_win.reshape(kv_shape), cos, sin)
    gates = jax.nn.sigmoid(g_b).reshape(bsz, seq, B_HEADS, 3)
    o_b = nsa_attention(q_b, k_c, v_c, k_s, v_slc.reshape(kv_shape), k_w, v_win.reshape(kv_shape), gates)
    o = jnp.concatenate([o_a.reshape(bsz, seq, A_HEADS * HEAD_DIM), o_b.reshape(bsz, seq, B_HEADS * HEAD_DIM)], axis=-1)
    return o @ w_out


def diff_mixer(x, cos, sin, w_in, lq1, lk1, lq2, lk2, subln_g, w_out, lam_init):
    bsz, seq, _ = x.shape
    q, k, v = _split(x @ w_in, C_LAYOUT)
    q = apply_rope(q.reshape(bsz, seq, C_HEADS, 2, HEAD_DIM), cos, sin)
    k = apply_rope(k.reshape(bsz, seq, C_HEADS, 2, HEAD_DIM), cos, sin)
    v = v.reshape(bsz, seq, C_HEADS, 2 * HEAD_DIM)
    f32 = jnp.float32
    lam = (jnp.exp(jnp.sum(lq1.astype(f32) * lk1.astype(f32)))
           - jnp.exp(jnp.sum(lq2.astype(f32) * lk2.astype(f32))) + lam_init)
    key_pos = jnp.arange(seq)

    def block(i):
        q0 = i * Q_BLOCK
        t = q0 + jnp.arange(Q_BLOCK)
        qb = lax.dynamic_slice_in_dim(q, q0, Q_BLOCK, 1)
        logits = jnp.einsum('bqhmd,bshmd->bhmqs', qb, k) * HEAD_DIM ** -0.5
        p = masked_softmax(logits, key_pos[None, :] <= t[:, None])
        a = (p[:, :, 0] - lam * p[:, :, 1]).astype(v.dtype)
        return jnp.einsum('bhqs,bshe->bqhe', a, v)

    o = sweep_blocks(block, seq)
    o = rms_norm(o, subln_g) * (1.0 - lam_init)
    return o.reshape(bsz, seq, C_HEADS * 2 * HEAD_DIM) @ w_out


def swiglu(x, w1, w3, w2):
    return (jax.nn.silu(x @ w1) * (x @ w3)) @ w2


def moe_swiglu(x, router_w, w1, w3, w2):
    logits = (x @ router_w).astype(jnp.float32)
    top_val, top_idx = lax.top_k(logits, TOP_K)
    top_w = jax.nn.softmax(top_val, axis=-1)
    gate = jnp.sum(jax.nn.one_hot(top_idx, N_EXPERTS, dtype=jnp.float32) * top_w[..., None], axis=-2).astype(x.dtype)
    y = jnp.zeros_like(x)
    for e in range(N_EXPERTS):
        y = y + gate[..., e:e + 1] * swiglu(x, w1[e], w3[e], w2[e])
    return y


def setup_inputs(seed: int = 0) -> dict:
    key = jax.random.key(seed)
    ks = iter(jax.random.split(key, 40))

    def nrm(shape, fan_in, scale=1.0):
        return jax.random.normal(next(ks), shape, jnp.float32) * (scale * fan_in ** -0.5)

    def gain(shape):
        return 1.0 + 0.02 * jax.random.normal(next(ks), shape, jnp.float32)

    def small(shape, s=0.02):
        return s * jax.random.normal(next(ks), shape, jnp.float32)

    ab_width = sum(w for _, w in AB_LAYOUT)
    c_width = sum(w for _, w in C_LAYOUT)
    x = jax.random.normal(next(ks), (BATCH, SEQ, D_MODEL), jnp.float32)
    start = jax.random.randint(next(ks), (BATCH, 1), 0, 1024, dtype=jnp.int32)
    positions = start + jnp.arange(SEQ, dtype=jnp.int32)[None, :]
    return {
        "x": x,
        "positions": positions,
        "ab_w_in": nrm((N_EVEN, D_MODEL, ab_width), D_MODEL) * jnp.asarray(_col_scale(AB_LAYOUT)),
        "cmp_pe_k": small((N_EVEN, CMP_LEN, HEAD_DIM), 0.2),
        "cmp_pe_v": small((N_EVEN, CMP_LEN, HEAD_DIM), 0.2),
        "cmp_k_w1": nrm((N_EVEN, CMP_LEN * HEAD_DIM, CMP_HIDDEN), CMP_LEN * HEAD_DIM),
        "cmp_k_w2": nrm((N_EVEN, CMP_HIDDEN, HEAD_DIM), CMP_HIDDEN),
        "cmp_v_w1": nrm((N_EVEN, CMP_LEN * HEAD_DIM, CMP_HIDDEN), CMP_LEN * HEAD_DIM),
        "cmp_v_w2": nrm((N_EVEN, CMP_HIDDEN, HEAD_DIM), CMP_HIDDEN),
        "ab_w_out": nrm((N_EVEN, (A_HEADS + B_HEADS) * HEAD_DIM, D_MODEL), (A_HEADS + B_HEADS) * HEAD_DIM, DEEPNORM_BETA),
        "ln_ab_g": gain((N_EVEN, D_MODEL)),
        "ln_ab_b": small((N_EVEN, D_MODEL)),
        "ffn_w1": nrm((N_EVEN, D_MODEL, D_FF), D_MODEL, DEEPNORM_BETA),
        "ffn_w3": nrm((N_EVEN, D_MODEL, D_FF), D_MODEL, DEEPNORM_BETA),
        "ffn_w2": nrm((N_EVEN, D_FF, D_MODEL), D_FF, DEEPNORM_BETA),
        "ln_ffn_g": gain((N_EVEN, D_MODEL)),
        "ln_ffn_b": small((N_EVEN, D_MODEL)),
        "c_w_in": nrm((N_ODD, D_MODEL, c_width), D_MODEL) * jnp.asarray(_col_scale(C_LAYOUT)),
        "lambda_q1": small((N_ODD, HEAD_DIM), 0.1),
        "lambda_k1": small((N_ODD, HEAD_DIM), 0.1),
        "lambda_q2": small((N_ODD, HEAD_DIM), 0.1),
        "lambda_k2": small((N_ODD, HEAD_DIM), 0.1),
        "c_subln_g": gain((N_ODD, 2 * HEAD_DIM)),
        "c_w_out": nrm((N_ODD, C_HEADS * 2 * HEAD_DIM, D_MODEL), C_HEADS * 2 * HEAD_DIM, DEEPNORM_BETA),
        "ln_c_g": gain((N_ODD, D_MODEL)),
        "ln_c_b": small((N_ODD, D_MODEL)),
        "router_w": nrm((N_ODD, D_MODEL, N_EXPERTS), D_MODEL),
        "moe_w1": nrm((N_ODD, N_EXPERTS, D_MODEL, D_FF_EXPERT), D_MODEL, DEEPNORM_BETA),
        "moe_w3": nrm((N_ODD, N_EXPERTS, D_MODEL, D_FF_EXPERT), D_MODEL, DEEPNORM_BETA),
        "moe_w2": nrm((N_ODD, N_EXPERTS, D_FF_EXPERT, D_MODEL), D_FF_EXPERT, DEEPNORM_BETA),
        "ln_moe_g": gain((N_ODD, D_MODEL)),
        "ln_moe_b": small((N_ODD, D_MODEL)),
    }


def reference(x, positions, ab_w_in, cmp_pe_k, cmp_pe_v, cmp_k_w1, cmp_k_w2, cmp_v_w1, cmp_v_w2,
              ab_w_out, ln_ab_g, ln_ab_b, ffn_w1, ffn_w3, ffn_w2, ln_ffn_g, ln_ffn_b,
              c_w_in, lambda_q1, lambda_k1, lambda_q2, lambda_k2, c_subln_g, c_w_out, ln_c_g, ln_c_b,
              router_w, moe_w1, moe_w3, moe_w2, ln_moe_g, ln_moe_b):
    cos, sin = rope_tables(positions)
    for layer in range(DEPTH):
        i = layer // 2
        if layer % 2 == 0:
            h = ab_mixer(x, positions, cos, sin, ab_w_in[i], cmp_pe_k[i], cmp_pe_v[i], cmp_k_w1[i], cmp_k_w2[i],
                         cmp_v_w1[i], cmp_v_w2[i], ab_w_out[i])
            x = layer_norm(DEEPNORM_ALPHA * x + h, ln_ab_g[i], ln_ab_b[i])
            x = layer_norm(DEEPNORM_ALPHA * x + swiglu(x, ffn_w1[i], ffn_w3[i], ffn_w2[i]), ln_ffn_g[i], ln_ffn_b[i])
        else:
            lam_init = 0.8 - 0.6 * math.exp(-0.3 * layer)
            h = diff_mixer(x, cos, sin, c_w_in[i], lambda_q1[i], lambda_k1[i], lambda_q2[i], lambda_k2[i],
                           c_subln_g[i], c_w_out[i], lam_init)
            x = layer_norm(DEEPNORM_ALPHA * x + h, ln_c_g[i], ln_c_b[i])
            x = layer_norm(DEEPNORM_ALPHA * x + moe_swiglu(x, router_w[i], moe_w1[i], moe_w3[i], moe_w2[i]),
                           ln_moe_g[i], ln_moe_b[i])
    return x
```

```python
import functools

import numpy as np
import jax
import jax.numpy as jnp
from jax import lax
from jax.experimental import pallas as pl
from jax.experimental.pallas import tpu as pltpu

f32 = jnp.float32
i32 = jnp.int32
MXU_DTYPE = jnp.bfloat16
VMEM_LIMIT_BYTES = 56 * 1024 * 1024
LANES = 128

D_MODEL = 1024
DEPTH = 2
HEAD_DIM = 64
ROT_DIM = HEAD_DIM // 4
ROPE_THETA = 500000.0
Q_BLOCK = 128
NEG = -1e30
LN_EPS = 1e-5
A_HEADS = 8
IDX_HEADS = 4
IDX_DIM = 64
DSA_TOPK = 256
B_HEADS = 8
B_KV_GROUPS = 2
B_REP = B_HEADS // B_KV_GROUPS
CMP_LEN = 32
CMP_STRIDE = 16
CMP_HIDDEN = 128
SLC_LEN = 64
SLC_TOPN = 16
WINDOW = 512
FORCED_BOOST = 1e6
C_HEADS = 8
N_EXPERTS = 8
TOP_K = 2
DEEPNORM_ALPHA = (2 * DEPTH) ** 0.25
QK_SCALE = HEAD_DIM ** -0.5
INT_MIN = -(2 ** 31)

AB_LAYOUT = (
    ("q_a", A_HEADS * HEAD_DIM), ("k_a", HEAD_DIM), ("v_a", HEAD_DIM),
    ("q_idx", IDX_HEADS * IDX_DIM), ("k_idx", IDX_DIM), ("w_idx", IDX_HEADS),
    ("q_b", B_HEADS * HEAD_DIM),
    ("k_cmp", B_KV_GROUPS * HEAD_DIM), ("v_cmp", B_KV_GROUPS * HEAD_DIM),
    ("k_slc", B_KV_GROUPS * HEAD_DIM), ("v_slc", B_KV_GROUPS * HEAD_DIM),
    ("k_win", B_KV_GROUPS * HEAD_DIM), ("v_win", B_KV_GROUPS * HEAD_DIM),
    ("gate_b", 3 * B_HEADS),
)
C_LAYOUT = (("q_c", C_HEADS * 2 * HEAD_DIM), ("k_c", C_HEADS * 2 * HEAD_DIM), ("v_c", C_HEADS * 2 * HEAD_DIM))


def _params(*sem):
    return pltpu.CompilerParams(dimension_semantics=sem, vmem_limit_bytes=VMEM_LIMIT_BYTES)


def _mm(a, b):
    return jnp.dot(a, b, preferred_element_type=f32)


def _layer_norm_rows(y, g, b):
    mu = jnp.mean(y, axis=-1, keepdims=True)
    yc = y - mu
    var = jnp.mean(yc * yc, axis=-1, keepdims=True)
    return yc * lax.rsqrt(var + LN_EPS) * g + b


def _proj_kernel(a_ref, w_ref, o_ref):
    o_ref[...] = _mm(a_ref[...].astype(MXU_DTYPE), w_ref[...]).astype(o_ref.dtype)


def project(a, w, tm=512):
    m, k = a.shape
    n = w.shape[1]
    return pl.pallas_call(
        _proj_kernel,
        grid=(m // tm,),
        in_specs=[pl.BlockSpec((tm, k), lambda i: (i, 0)), pl.BlockSpec((k, n), lambda i: (0, 0))],
        out_specs=pl.BlockSpec((tm, n), lambda i: (i, 0)),
        out_shape=jax.ShapeDtypeStruct((m, n), f32),
        compiler_params=_params("parallel"),
        name="project",
    )(a, w)


def _proj_ln_kernel(a_ref, w_ref, res_ref, g_ref, b_ref, o_ref):
    h = _mm(a_ref[...].astype(MXU_DTYPE), w_ref[...])
    o_ref[...] = _layer_norm_rows(DEEPNORM_ALPHA * res_ref[...] + h, g_ref[...], b_ref[...])


def project_residual_ln(a, w, res, g, b, tm=512):
    m, k = a.shape
    n = w.shape[1]
    return pl.pallas_call(
        _proj_ln_kernel,
        grid=(m // tm,),
        in_specs=[pl.BlockSpec((tm, k), lambda i: (i, 0)), pl.BlockSpec((k, n), lambda i: (0, 0)),
                  pl.BlockSpec((tm, n), lambda i: (i, 0)),
                  pl.BlockSpec((1, n), lambda i: (0, 0)), pl.BlockSpec((1, n), lambda i: (0, 0))],
        out_specs=pl.BlockSpec((tm, n), lambda i: (i, 0)),
        out_shape=jax.ShapeDtypeStruct((m, n), f32),
        compiler_params=_params("parallel"),
        name="project_residual_ln",
    )(a, w, res, g.reshape(1, n), b.reshape(1, n))


def _experts_ln_kernel(x_ref, gate_ref, w1_ref, w3_ref, w2_ref, g_ref, b_ref, o_ref, xb_ref, acc_ref, *, n_experts):
    e = pl.program_id(1)
    f = pl.program_id(2)

    @pl.when((e == 0) & (f == 0))
    def _():
        xb_ref[...] = x_ref[...].astype(MXU_DTYPE)
        acc_ref[...] = jnp.zeros_like(acc_ref)

    xb = xb_ref[...]
    a = _mm(xb, w1_ref[...])
    h = (a * jax.nn.sigmoid(a)) * _mm(xb, w3_ref[...])
    y = _mm(h.astype(MXU_DTYPE), w2_ref[...])
    if n_experts > 1:
        lane = lax.broadcasted_iota(i32, gate_ref.shape, 1)
        gcol = jnp.sum(jnp.where(lane == e, gate_ref[...], 0.0), axis=1, keepdims=True)
        y = gcol * y
    acc_ref[...] += y

    @pl.when((e == pl.num_programs(1) - 1) & (f == pl.num_programs(2) - 1))
    def _():
        o_ref[...] = _layer_norm_rows(DEEPNORM_ALPHA * x_ref[...] + acc_ref[...], g_ref[...], b_ref[...])


def experts_residual_ln(x, gate, w1, w3, w2, g, b, tm, tf):
    m, d = x.shape
    n_experts, _, ff = w1.shape
    kern = functools.partial(_experts_ln_kernel, n_experts=n_experts)
    return pl.pallas_call(
        kern,
        grid=(m // tm, n_experts, ff // tf),
        in_specs=[pl.BlockSpec((tm, d), lambda i, e, f: (i, 0)),
                  pl.BlockSpec((tm, LANES), lambda i, e, f: (i, 0)),
                  pl.BlockSpec((None, d, tf), lambda i, e, f: (e, 0, f)),
                  pl.BlockSpec((None, d, tf), lambda i, e, f: (e, 0, f)),
                  pl.BlockSpec((None, tf, d), lambda i, e, f: (e, f, 0)),
                  pl.BlockSpec((1, d), lambda i, e, f: (0, 0)), pl.BlockSpec((1, d), lambda i, e, f: (0, 0))],
        out_specs=pl.BlockSpec((tm, d), lambda i, e, f: (i, 0)),
        out_shape=jax.ShapeDtypeStruct((m, d), f32),
        scratch_shapes=[pltpu.VMEM((tm, d), MXU_DTYPE), pltpu.VMEM((tm, d), f32)],
        compiler_params=_params("parallel", "arbitrary", "arbitrary"),
        name="experts_residual_ln",
    )(x, gate, w1, w3, w2, g.reshape(1, d), b.reshape(1, d))


def _router_kernel(x_ref, w_ref, gate_ref, *, n_experts):
    logits = jnp.dot(x_ref[...], w_ref[...], preferred_element_type=f32, precision=lax.Precision.HIGHEST)
    lane = lax.broadcasted_iota(i32, logits.shape, 1).astype(f32)
    logits = jnp.where(lane < n_experts, logits, -jnp.inf)
    v1 = jnp.max(logits, axis=1, keepdims=True)
    i1 = jnp.min(jnp.where(logits == v1, lane, float(LANES)), axis=1, keepdims=True)
    rest = jnp.where(lane == i1, -jnp.inf, logits)
    v2 = jnp.max(rest, axis=1, keepdims=True)
    i2 = jnp.min(jnp.where(rest == v2, lane, float(LANES)), axis=1, keepdims=True)
    e2 = jnp.exp(v2 - v1)
    den = 1.0 + e2
    gate_ref[...] = jnp.where(lane == i1, 1.0 / den, 0.0) + jnp.where(lane == i2, e2 / den, 0.0)


def router_gates(x, router_w, tm=512):
    m, d = x.shape
    n_experts = router_w.shape[1]
    w = jnp.zeros((d, LANES), f32).at[:, :n_experts].set(router_w)
    return pl.pallas_call(
        functools.partial(_router_kernel, n_experts=n_experts),
        grid=(m // tm,),
        in_specs=[pl.BlockSpec((tm, d), lambda i: (i, 0)), pl.BlockSpec((d, LANES), lambda i: (0, 0))],
        out_specs=pl.BlockSpec((tm, LANES), lambda i: (i, 0)),
        out_shape=jax.ShapeDtypeStruct((m, LANES), f32),
        compiler_params=_params("parallel"),
        name="router_gates",
    )(x, w)


def _flash_reset(m_ref, l_ref, acc_ref):
    m_ref[...] = jnp.full(m_ref.shape, NEG, f32)
    l_ref[...] = jnp.zeros(l_ref.shape, f32)
    acc_ref[...] = jnp.zeros(acc_ref.shape, f32)


def _flash_step(q, kt, v, mask, m_ref, l_ref, acc_ref):
    s = jnp.where(mask, _mm(q, kt), NEG)
    m_old = m_ref[...]
    m_new = jnp.maximum(m_old, jnp.max(s, axis=1, keepdims=True))
    alpha = jnp.exp(m_old - m_new)
    p = jnp.where(mask, jnp.exp(s - m_new), 0.0)
    l_ref[...] = alpha * l_ref[...] + jnp.sum(p, axis=1, keepdims=True)
    acc_ref[...] = alpha * acc_ref[...] + _mm(p.astype(v.dtype), v)
    m_ref[...] = m_new


def _tile_rows(x, reps):
    return jnp.concatenate([x] * reps, axis=0)


def _dsa_kernel(qi_ref, wi_ref, kit_ref, qa_ref, kat_ref, va_ref, o_ref,
                skey_ref, m_ref, l_ref, acc_ref, *, topk, kc):
    qb = Q_BLOCK
    i = pl.program_id(1)
    q0 = i * qb
    n_chunks = (q0 + qb + kc - 1) // kc
    t_row = q0 + lax.broadcasted_iota(i32, (qb, kc), 0)
    col = lax.broadcasted_iota(i32, (qb, kc), 1)
    wi = wi_ref[...]
    qi = qi_ref[...]

    def score_body(c, carry):
        off = pl.multiple_of(c * kc, kc)
        kt = kit_ref[:, pl.ds(off, kc)]
        s = jnp.zeros((qb, kc), f32)
        for h in range(IDX_HEADS):
            d = _mm(qi[:, h * IDX_DIM:(h + 1) * IDX_DIM], kt)
            s = s + wi[:, h:h + 1] * jnp.maximum(d, 0.0)
        s = jnp.where(col + off <= t_row, s, NEG)
        bits = pltpu.bitcast(s, i32)
        skey_ref[:, pl.ds(off, kc)] = jnp.where(bits < 0, bits ^ 0x7FFFFFFF, bits)
        return carry

    lax.fori_loop(0, n_chunks, score_body, 0)

    def count_ge(cand):
        def body(c, acc):
            off = pl.multiple_of(c * kc, kc)
            return acc + jnp.where(skey_ref[:, pl.ds(off, kc)] >= cand, 1.0, 0.0)
        acc = lax.fori_loop(0, n_chunks, body, jnp.zeros((qb, kc), f32))
        return jnp.sum(acc, axis=1, keepdims=True)

    thr = jnp.where(count_ge(jnp.zeros((qb, 1), i32)) >= topk, 0, INT_MIN).astype(i32)

    def bisect(b, thr):
        cand = thr + jnp.left_shift(jnp.int32(1), 30 - b)
        return jnp.where(count_ge(cand) >= topk, cand, thr)

    thr = lax.fori_loop(0, 31, bisect, thr)
    quota = topk - count_ge(thr + 1)

    qa = qa_ref[...]
    before = (lax.broadcasted_iota(i32, (kc, kc), 0) < lax.broadcasted_iota(i32, (kc, kc), 1))
    before = jnp.where(before, 1.0, 0.0).astype(MXU_DTYPE)
    _flash_reset(m_ref, l_ref, acc_ref)

    def attn_body(c, ties_seen):
        off = pl.multiple_of(c * kc, kc)
        key = skey_ref[:, pl.ds(off, kc)]
        eq = key == thr
        eqf = jnp.where(eq, 1.0, 0.0)
        rank = ties_seen + _mm(eqf.astype(MXU_DTYPE), before)
        sel = (key > thr) | (eq & (rank < quota))
        sel = sel & (col + off <= t_row)
        mask = _tile_rows(jnp.where(sel, 1.0, 0.0), A_HEADS) > 0.5
        _flash_step(qa, kat_ref[:, pl.ds(off, kc)], va_ref[pl.ds(off, kc), :], mask, m_ref, l_ref, acc_ref)
        return ties_seen + jnp.sum(eqf, axis=1, keepdims=True)

    lax.fori_loop(0, n_chunks, attn_body, jnp.zeros((qb, 1), f32))
    o_ref[...] = acc_ref[...] / l_ref[...]


def dsa_attention(qi, wi, kit, qa, kat, va, topk, kc=256):
    bsz, seq, _ = qi.shape
    nb = seq // Q_BLOCK
    rows = A_HEADS * Q_BLOCK
    kern = functools.partial(_dsa_kernel, topk=topk, kc=kc)
    return pl.pallas_call(
        kern,
        grid=(bsz, nb),
        in_specs=[pl.BlockSpec((None, Q_BLOCK, IDX_HEADS * IDX_DIM), lambda b, i: (b, i, 0)),
                  pl.BlockSpec((None, Q_BLOCK, IDX_HEADS), lambda b, i: (b, i, 0)),
                  pl.BlockSpec((None, IDX_DIM, seq), lambda b, i: (b, 0, 0)),
                  pl.BlockSpec((None, None, rows, HEAD_DIM), lambda b, i: (b, i, 0, 0)),
                  pl.BlockSpec((None, HEAD_DIM, seq), lambda b, i: (b, 0, 0)),
                  pl.BlockSpec((None, seq, HEAD_DIM), lambda b, i: (b, 0, 0))],
        out_specs=pl.BlockSpec((None, None, rows, HEAD_DIM), lambda b, i: (b, i, 0, 0)),
        out_shape=jax.ShapeDtypeStruct((bsz, nb, rows, HEAD_DIM), f32),
        scratch_shapes=[pltpu.VMEM((Q_BLOCK, seq), i32), pltpu.VMEM((rows, 1), f32),
                        pltpu.VMEM((rows, 1), f32), pltpu.VMEM((rows, HEAD_DIM), f32)],
        compiler_params=_params("parallel", "arbitrary"),
        name="dsa_attention",
    )(qi, wi, kit, qa, kat, va)


def _compress_kernel(x_ref, pe_ref, w1_ref, w2_ref, o_ref):
    half = (CMP_LEN // 2) * HEAD_DIM
    x = x_ref[...]
    first = _mm((x + pe_ref[0:1, :]).astype(MXU_DTYPE), w1_ref[0:half, :])
    second = _mm((x + pe_ref[1:2, :]).astype(MXU_DTYPE), w1_ref[half:2 * half, :])
    pre = first + pltpu.roll(second, shift=x.shape[0] - 1, axis=0)
    hid = pre * jax.nn.sigmoid(pre)
    o_ref[...] = _mm(hid.astype(MXU_DTYPE), w2_ref[...])


def nsa_compress(chunks, pe2, w1, w2):
    bsz, groups, nck, width = chunks.shape
    return pl.pallas_call(
        _compress_kernel,
        grid=(bsz, groups),
        in_specs=[pl.BlockSpec((None, None, nck, width), lambda b, g: (b, g, 0, 0)),
                  pl.BlockSpec((2, width), lambda b, g: (0, 0)),
                  pl.BlockSpec((2 * width, CMP_HIDDEN), lambda b, g: (0, 0)),
                  pl.BlockSpec((CMP_HIDDEN, HEAD_DIM), lambda b, g: (0, 0))],
        out_specs=pl.BlockSpec((None, None, nck, HEAD_DIM), lambda b, g: (b, g, 0, 0)),
        out_shape=jax.ShapeDtypeStruct((bsz, groups, nck, HEAD_DIM), f32),
        compiler_params=_params("parallel", "parallel"),
        name="nsa_compress",
    )(chunks, pe2, w1, w2)


def _nsa_kernel(q_ref, graw_ref, kct_ref, vc_ref, kst_ref, vs_ref, kwt_ref, vw_ref, ovl_ref, o_ref,
                m_ref, l_ref, acc_ref, *, n_slc, n_sel, kc):
    qb = Q_BLOCK
    rows = B_REP * qb
    i = pl.program_id(2)
    q0 = i * qb
    q = q_ref[...]
    n_cmp = kct_ref.shape[1]

    t_c = q0 + lax.broadcasted_iota(i32, (qb, n_cmp), 0)
    cmp_end = lax.broadcasted_iota(i32, (qb, n_cmp), 1) * CMP_STRIDE + (CMP_LEN - 1)
    vis = _tile_rows(jnp.where(cmp_end <= t_c, 1.0, 0.0), B_REP) > 0.5
    lc = jnp.where(vis, _mm(q, kct_ref[...]), NEG)
    ec = jnp.where(vis, jnp.exp(lc - jnp.max(lc, axis=1, keepdims=True)), 0.0)
    den = jnp.sum(ec, axis=1, keepdims=True)
    p_c = ec / jnp.where(den > 0.0, den, 1.0)
    o_c = _mm(p_c.astype(MXU_DTYPE), vc_ref[...])

    p_sum = p_c[0:qb]
    for r in range(1, B_REP):
        p_sum = p_sum + p_c[r * qb:(r + 1) * qb]
    imp = jnp.dot(p_sum, ovl_ref[...], preferred_element_type=f32, precision=lax.Precision.HIGHEST)
    t_q = q0 + lax.broadcasted_iota(i32, (qb, LANES), 0)
    blk = lax.broadcasted_iota(i32, (qb, LANES), 1)
    blk_t = t_q // SLC_LEN
    forced = (blk == 0) | (blk == blk_t) | (blk == blk_t - 1)
    imp = jnp.where(forced, FORCED_BOOST, imp)
    imp = jnp.where(blk * SLC_LEN <= t_q, imp, NEG)
    imp = jnp.where(blk < n_slc, imp, -jnp.inf)
    blk_f = blk.astype(f32)

    def pick(_, carry):
        imp, selm = carry
        best = jnp.max(imp, axis=1, keepdims=True)
        first = jnp.min(jnp.where(imp == best, blk_f, float(LANES)), axis=1, keepdims=True)
        hit = blk_f == first
        return jnp.where(hit, -jnp.inf, imp), jnp.where(hit, 1.0, selm)

    _, selm = lax.fori_loop(0, n_sel, pick, (imp, jnp.zeros((qb, LANES), f32)))
    selm = selm.astype(MXU_DTYPE)

    t_k = q0 + lax.broadcasted_iota(i32, (qb, kc), 0)
    col = lax.broadcasted_iota(i32, (qb, kc), 1)
    exp_row = lax.broadcasted_iota(i32, (LANES, kc), 0)
    exp_col = lax.broadcasted_iota(i32, (LANES, kc), 1)
    _flash_reset(m_ref, l_ref, acc_ref)

    def slc_body(c, carry):
        off = pl.multiple_of(c * kc, kc)
        expand = jnp.where(exp_row == (exp_col + off) // SLC_LEN, 1.0, 0.0).astype(MXU_DTYPE)
        sel = (_mm(selm, expand) > 0.5) & (col + off <= t_k)
        mask = _tile_rows(jnp.where(sel, 1.0, 0.0), B_REP) > 0.5
        _flash_step(q, kst_ref[:, pl.ds(off, kc)], vs_ref[pl.ds(off, kc), :], mask, m_ref, l_ref, acc_ref)
        return carry

    lax.fori_loop(0, (q0 + qb + kc - 1) // kc, slc_body, 0)
    o_s = acc_ref[...] / l_ref[...]

    t_w = q0 + lax.broadcasted_iota(i32, (qb, qb), 0)
    col_w = lax.broadcasted_iota(i32, (qb, qb), 1)
    _flash_reset(m_ref, l_ref, acc_ref)

    def win_body(c, carry):
        off = pl.multiple_of(c * qb, qb)
        dist = t_w - (col_w + off)
        sel = (dist >= 0) & (dist < WINDOW)
        mask = _tile_rows(jnp.where(sel, 1.0, 0.0), B_REP) > 0.5
        _flash_step(q, kwt_ref[:, pl.ds(off, qb)], vw_ref[pl.ds(off, qb), :], mask, m_ref, l_ref, acc_ref)
        return carry

    lax.fori_loop(jnp.maximum(i - WINDOW // qb, 0), i + 1, win_body, 0)
    o_w = acc_ref[...] / l_ref[...]

    gates = jax.nn.sigmoid(graw_ref[...])
    o_ref[...] = gates[:, 0:1] * o_c + gates[:, 1:2] * o_s + gates[:, 2:3] * o_w


def nsa_attention(q, graw, kct, vc, kst, vs, kwt, vw, ovl, n_slc, n_sel, kc=256):
    bsz, groups, nb, rows, _ = q.shape
    seq = kst.shape[-1]
    n_cmp = kct.shape[-1]
    kern = functools.partial(_nsa_kernel, n_slc=n_slc, n_sel=n_sel, kc=kc)
    per_bg_t = lambda b, g, i: (b, g, 0, 0)
    return pl.pallas_call(
        kern,
        grid=(bsz, groups, nb),
        in_specs=[pl.BlockSpec((None, None, None, rows, HEAD_DIM), lambda b, g, i: (b, g, i, 0, 0)),
                  pl.BlockSpec((None, None, None, rows, 3), lambda b, g, i: (b, g, i, 0, 0)),
                  pl.BlockSpec((None, None, HEAD_DIM, n_cmp), per_bg_t),
                  pl.BlockSpec((None, None, n_cmp, HEAD_DIM), per_bg_t),
                  pl.BlockSpec((None, None, HEAD_DIM, seq), per_bg_t),
                  pl.BlockSpec((None, None, seq, HEAD_DIM), per_bg_t),
                  pl.BlockSpec((None, None, HEAD_DIM, seq), per_bg_t),
                  pl.BlockSpec((None, None, seq, HEAD_DIM), per_bg_t),
                  pl.BlockSpec((n_cmp, LANES), lambda b, g, i: (0, 0))],
        out_specs=pl.BlockSpec((None, None, None, rows, HEAD_DIM), lambda b, g, i: (b, g, i, 0, 0)),
        out_shape=jax.ShapeDtypeStruct((bsz, groups, nb, rows, HEAD_DIM), f32),
        scratch_shapes=[pltpu.VMEM((rows, 1), f32), pltpu.VMEM((rows, 1), f32), pltpu.VMEM((rows, HEAD_DIM), f32)],
        compiler_params=_params("parallel", "parallel", "arbitrary"),
        name="nsa_attention",
    )(q, graw, kct, vc, kst, vs, kwt, vw, ovl)


def _diff_kernel(lam_ref, q_ref, kt_ref, v_ref, g_ref, o_ref, m_ref, l_ref, acc_ref, *, tq, kc, out_scale):
    i = pl.program_id(2)
    q0 = i * tq
    t_row = q0 + lax.broadcasted_iota(i32, (tq, kc), 0)
    col = lax.broadcasted_iota(i32, (tq, kc), 1)
    for half in range(2):
        _flash_reset(m_ref.at[half], l_ref.at[half], acc_ref.at[half])

    def body(c, carry):
        off = pl.multiple_of(c * kc, kc)
        mask = col + off <= t_row
        v = v_ref[pl.ds(off, kc), :]
        for half in range(2):
            _flash_step(q_ref[half], kt_ref[half, :, pl.ds(off, kc)], v, mask,
                        m_ref.at[half], l_ref.at[half], acc_ref.at[half])
        return carry

    lax.fori_loop(0, (q0 + tq + kc - 1) // kc, body, 0)
    o = acc_ref[0] / l_ref[0] - lam_ref[0] * (acc_ref[1] / l_ref[1])
    o = o * lax.rsqrt(jnp.mean(o * o, axis=-1, keepdims=True) + LN_EPS) * g_ref[...]
    o_ref[...] = o * out_scale


def diff_attention(lam, q, kt, v, subln_g, out_scale, tq=256, kc=512):
    bsz, heads, _, seq, _ = q.shape
    dv = v.shape[-1]
    tq = min(tq, seq)
    kc = min(kc, seq)
    kern = functools.partial(_diff_kernel, tq=tq, kc=kc, out_scale=out_scale)
    return pl.pallas_call(
        kern,
        grid=(bsz, heads, seq // tq),
        in_specs=[pl.BlockSpec(memory_space=pltpu.SMEM),
                  pl.BlockSpec((None, None, 2, tq, HEAD_DIM), lambda b, h, i: (b, h, 0, i, 0)),
                  pl.BlockSpec((None, None, 2, HEAD_DIM, seq), lambda b, h, i: (b, h, 0, 0, 0)),
                  pl.BlockSpec((None, None, seq, dv), lambda b, h, i: (b, h, 0, 0)),
                  pl.BlockSpec((1, dv), lambda b, h, i: (0, 0))],
        out_specs=pl.BlockSpec((None, tq, dv), lambda b, h, i: (b, i, h)),
        out_shape=jax.ShapeDtypeStruct((bsz, seq, heads * dv), f32),
        scratch_shapes=[pltpu.VMEM((2, tq, 1), f32), pltpu.VMEM((2, tq, 1), f32), pltpu.VMEM((2, tq, dv), f32)],
        compiler_params=_params("parallel", "parallel", "arbitrary"),
        name="diff_attention",
    )(lam, q, kt, v, subln_g.reshape(1, dv))


def _split(h, layout):
    offs = np.cumsum([w for _, w in layout])[:-1].tolist()
    return jnp.split(h, offs, axis=-1)


def _rope_tables(positions):
    inv_freq = ROPE_THETA ** (-jnp.arange(0, ROT_DIM, 2, dtype=f32) / ROT_DIM)
    ang = positions.astype(f32)[..., None] * inv_freq
    return jnp.cos(ang), jnp.sin(ang)


def _apply_rope(x, cos, sin):
    shape = cos.shape[:2] + (1,) * (x.ndim - 3) + cos.shape[-1:]
    c = cos.reshape(shape)
    s = sin.reshape(shape)
    half = ROT_DIM // 2
    x1, x2 = x[..., :half], x[..., half:ROT_DIM]
    return jnp.concatenate([x1 * c - x2 * s, x2 * c + x1 * s, x[..., ROT_DIM:]], axis=-1)


def _pad_cols(w, mult):
    n = w.shape[-1]
    pad = (-n) % mult
    return jnp.pad(w, ((0, 0), (0, pad))) if pad else w


def _overlap_matrix(n_cmp_rows, n_slc):
    c_start = np.arange(n_cmp_rows) * CMP_STRIDE
    s_start = np.arange(LANES) * SLC_LEN
    ovl = (c_start[:, None] < s_start[None, :] + SLC_LEN) & (c_start[:, None] + CMP_LEN > s_start[None, :])
    ovl = ovl & (np.arange(LANES)[None, :] < n_slc)
    return jnp.asarray(ovl.astype(np.float32))


def _ab_mixer(x2, bsz, seq, positions, cos, sin, w_in, pe_k, pe_v, ck1, ck2, cv1, cv2):
    nb = seq // Q_BLOCK
    width = sum(w for _, w in AB_LAYOUT)
    h = project(x2, _pad_cols(w_in, 2 * LANES).astype(MXU_DTYPE))[:, :width].reshape(bsz, seq, width)
    (q_a, k_a, v_a, q_idx, k_idx, w_idx, q_b, k_cmp, v_cmp, k_slc, v_slc, k_win, v_win, g_b) = _split(h, AB_LAYOUT)

    q_a = _apply_rope(q_a.reshape(bsz, seq, A_HEADS, HEAD_DIM), cos, sin) * QK_SCALE
    q_a = q_a.reshape(bsz, nb, Q_BLOCK, A_HEADS, HEAD_DIM).transpose(0, 1, 3, 2, 4)
    q_a = q_a.reshape(bsz, nb, A_HEADS * Q_BLOCK, HEAD_DIM).astype(MXU_DTYPE)
    k_a = _apply_rope(k_a, cos, sin).transpose(0, 2, 1).astype(MXU_DTYPE)
    q_idx = _apply_rope(q_idx.reshape(bsz, seq, IDX_HEADS, IDX_DIM), cos, sin)
    q_idx = q_idx.reshape(bsz, seq, IDX_HEADS * IDX_DIM).astype(MXU_DTYPE)
    k_idx = _apply_rope(k_idx, cos, sin).transpose(0, 2, 1).astype(MXU_DTYPE)
    w_idx = w_idx * (IDX_HEADS * IDX_DIM) ** -0.5
    o_a = dsa_attention(q_idx, w_idx, k_idx, q_a, k_a, v_a.astype(MXU_DTYPE), topk=min(DSA_TOPK, seq // 4))
    o_a = o_a.reshape(bsz, nb, A_HEADS, Q_BLOCK, HEAD_DIM).transpose(0, 1, 3, 2, 4)
    o_a = o_a.reshape(bsz * seq, A_HEADS * HEAD_DIM)

    groups = B_KV_GROUPS
    n_rows = seq // CMP_STRIDE
    n_cmp = (seq - CMP_LEN) // CMP_STRIDE + 1
    assert n_cmp == n_rows - 1
    n_slc = seq // SLC_LEN
    assert n_slc <= LANES

    def chunked(kv):
        kv = kv.reshape(bsz, n_rows, CMP_STRIDE, groups, HEAD_DIM).transpose(0, 3, 1, 2, 4)
        return kv.reshape(bsz, groups, n_rows, CMP_STRIDE * HEAD_DIM)

    def pe_halves(pe):
        return pe.reshape(2, CMP_STRIDE * HEAD_DIM)

    k_c = nsa_compress(chunked(k_cmp), pe_halves(pe_k), ck1.astype(MXU_DTYPE), ck2.astype(MXU_DTYPE))
    v_c = nsa_compress(chunked(v_cmp), pe_halves(pe_v), cv1.astype(MXU_DTYPE), cv2.astype(MXU_DTYPE))
    cmp_end = jnp.minimum(jnp.arange(n_rows) * CMP_STRIDE + CMP_LEN - 1, seq - 1)
    cos_c, sin_c = _rope_tables(positions[:, cmp_end])
    k_c = _apply_rope(k_c.transpose(0, 2, 1, 3), cos_c, sin_c)
    k_c = k_c.transpose(0, 2, 3, 1).astype(MXU_DTYPE)
    v_c = v_c.astype(MXU_DTYPE)

    def keys_t(k, rope):
        k = k.reshape(bsz, seq, groups, HEAD_DIM)
        if rope:
            k = _apply_rope(k, cos, sin)
        return k.transpose(0, 2, 3, 1).astype(MXU_DTYPE)

    def vals(v):
        return v.reshape(bsz, seq, groups, HEAD_DIM).transpose(0, 2, 1, 3).astype(MXU_DTYPE)

    def head_rows(a, last):
        a = a.reshape(bsz, nb, Q_BLOCK, groups, B_REP, last).transpose(0, 3, 1, 4, 2, 5)
        return a.reshape(bsz, groups, nb, B_REP * Q_BLOCK, last)

    q_b = _apply_rope(q_b.reshape(bsz, seq, B_HEADS, HEAD_DIM), cos, sin) * QK_SCALE
    q_b = head_rows(q_b.reshape(bsz, seq, B_HEADS * HEAD_DIM), HEAD_DIM).astype(MXU_DTYPE)
    o_b = nsa_attention(q_b, head_rows(g_b, 3), k_c, v_c, keys_t(k_slc, True), vals(v_slc),
                        keys_t(k_win, True), vals(v_win), _overlap_matrix(n_rows, n_slc),
                        n_slc=n_slc, n_sel=min(SLC_TOPN, n_slc))
    o_b = o_b.reshape(bsz, groups, nb, B_REP, Q_BLOCK, HEAD_DIM).transpose(0, 2, 4, 1, 3, 5)
    o_b = o_b.reshape(bsz * seq, B_HEADS * HEAD_DIM)
    return jnp.concatenate([o_a, o_b], axis=-1)


def _diff_mixer(x2, bsz, seq, cos, sin, w_in, lq1, lk1, lq2, lk2, subln_g, lam_init):
    h = project(x2, w_in.astype(MXU_DTYPE)).reshape(bsz, seq, -1)
    q, k, v = _split(h, C_LAYOUT)
    q = _apply_rope(q.reshape(bsz, seq, C_HEADS, 2, HEAD_DIM), cos, sin) * QK_SCALE
    k = _apply_rope(k.reshape(bsz, seq, C_HEADS, 2, HEAD_DIM), cos, sin)
    q = q.transpose(0, 2, 3, 1, 4).astype(MXU_DTYPE)
    kt = k.transpose(0, 2, 3, 4, 1).astype(MXU_DTYPE)
    v = v.reshape(bsz, seq, C_HEADS, 2 * HEAD_DIM).transpose(0, 2, 1, 3).astype(MXU_DTYPE)
    lam = (jnp.exp(jnp.sum(lq1 * lk1)) - jnp.exp(jnp.sum(lq2 * lk2)) + lam_init).reshape(1).astype(f32)
    o = diff_attention(lam, q, kt, v, subln_g, 1.0 - lam_init)
    return o.reshape(bsz * seq, C_HEADS * 2 * HEAD_DIM)


def kernel(x, positions, ab_w_in, cmp_pe_k, cmp_pe_v, cmp_k_w1, cmp_k_w2, cmp_v_w1, cmp_v_w2, ab_w_out, ln_ab_g, ln_ab_b, ffn_w1, ffn_w3, ffn_w2, ln_ffn_g, ln_ffn_b, c_w_in, lambda_q1, lambda_k1, lambda_q2, lambda_k2, c_subln_g, c_w_out, ln_c_g, ln_c_b, router_w, moe_w1, moe_w3, moe_w2, ln_moe_g, ln_moe_b):
    import math
    bsz, seq, d = x.shape
    assert seq % Q_BLOCK == 0 and d == D_MODEL
    cos, sin = _rope_tables(positions)
    x2 = x.reshape(bsz * seq, d)
    no_gate = jnp.zeros((bsz * seq, LANES), f32)
    for layer in range(DEPTH):
        i = layer // 2
        if layer % 2 == 0:
            o = _ab_mixer(x2, bsz, seq, positions, cos, sin, ab_w_in[i], cmp_pe_k[i], cmp_pe_v[i],
                          cmp_k_w1[i], cmp_k_w2[i], cmp_v_w1[i], cmp_v_w2[i])
            x2 = project_residual_ln(o, ab_w_out[i].astype(MXU_DTYPE), x2, ln_ab_g[i], ln_ab_b[i])
            x2 = experts_residual_ln(x2, no_gate, ffn_w1[i][None].astype(MXU_DTYPE), ffn_w3[i][None].astype(MXU_DTYPE),
                                     ffn_w2[i][None].astype(MXU_DTYPE), ln_ffn_g[i], ln_ffn_b[i], tm=512, tf=1408)
        else:
            lam_init = 0.8 - 0.6 * math.exp(-0.3 * layer)
            o = _diff_mixer(x2, bsz, seq, cos, sin, c_w_in[i], lambda_q1[i], lambda_k1[i], lambda_q2[i],
                            lambda_k2[i], c_subln_g[i], lam_init)
            x2 = project_residual_ln(o, c_w_out[i].astype(MXU_DTYPE), x2, ln_c_g[i], ln_c_b[i])
            gate = router_gates(x2, router_w[i])
            x2 = experts_residual_ln(x2, gate, moe_w1[i].astype(MXU_DTYPE), moe_w3[i].astype(MXU_DTYPE),
                                     moe_w2[i].astype(MXU_DTYPE), ln_moe_g[i], ln_moe_b[i], tm=512, tf=512)
    return x2.reshape(bsz, seq, d)
```

```python
import functools
import math

import numpy as np
import jax
import jax.numpy as jnp
from jax import lax
from jax.experimental import pallas as pl
from jax.experimental.pallas import tpu as pltpu

f32 = jnp.float32
i32 = jnp.int32
MXU_DTYPE = jnp.bfloat16
VMEM_LIMIT_BYTES = 56 * 1024 * 1024
LANES = 128

D_MODEL = 1024
DEPTH = 2
HEAD_DIM = 64
ROT_DIM = HEAD_DIM // 4
ROPE_THETA = 500000.0
Q_BLOCK = 128
NEG = -1e30
LN_EPS = 1e-5
A_HEADS = 8
IDX_HEADS = 4
IDX_DIM = 64
DSA_TOPK = 256
B_HEADS = 8
B_KV_GROUPS = 2
B_REP = B_HEADS // B_KV_GROUPS
CMP_LEN = 32
CMP_STRIDE = 16
CMP_HIDDEN = 128
SLC_LEN = 64
SLC_TOPN = 16
WINDOW = 512
FORCED_BOOST = 1e6
C_HEADS = 8
N_EXPERTS = 8
TOP_K = 2
DEEPNORM_ALPHA = (2 * DEPTH) ** 0.25
QK_SCALE = HEAD_DIM ** -0.5
INT_MIN = -(2 ** 31)
COUNT_STRIP = 512

AB_LAYOUT = (
    ("q_a", A_HEADS * HEAD_DIM), ("k_a", HEAD_DIM), ("v_a", HEAD_DIM),
    ("q_idx", IDX_HEADS * IDX_DIM), ("k_idx", IDX_DIM), ("w_idx", IDX_HEADS),
    ("q_b", B_HEADS * HEAD_DIM),
    ("k_cmp", B_KV_GROUPS * HEAD_DIM), ("v_cmp", B_KV_GROUPS * HEAD_DIM),
    ("k_slc", B_KV_GROUPS * HEAD_DIM), ("v_slc", B_KV_GROUPS * HEAD_DIM),
    ("k_win", B_KV_GROUPS * HEAD_DIM), ("v_win", B_KV_GROUPS * HEAD_DIM),
    ("gate_b", 3 * B_HEADS),
)
C_LAYOUT = (("q_c", C_HEADS * 2 * HEAD_DIM), ("k_c", C_HEADS * 2 * HEAD_DIM), ("v_c", C_HEADS * 2 * HEAD_DIM))


def _params(*sem):
    return pltpu.CompilerParams(dimension_semantics=sem, vmem_limit_bytes=VMEM_LIMIT_BYTES)


def _mm(a, b):
    return jnp.dot(a, b, preferred_element_type=f32)


def _layer_norm_rows(y, g, b):
    mu = jnp.mean(y, axis=-1, keepdims=True)
    yc = y - mu
    var = jnp.mean(yc * yc, axis=-1, keepdims=True)
    return yc * lax.rsqrt(var + LN_EPS) * g + b


def _proj_kernel(a_ref, w_ref, o_ref):
    o_ref[...] = _mm(a_ref[...].astype(MXU_DTYPE), w_ref[...]).astype(o_ref.dtype)


def project(a, w, tm=512):
    m, k = a.shape
    n = w.shape[1]
    return pl.pallas_call(
        _proj_kernel,
        grid=(m // tm,),
        in_specs=[pl.BlockSpec((tm, k), lambda i: (i, 0)), pl.BlockSpec((k, n), lambda i: (0, 0))],
        out_specs=pl.BlockSpec((tm, n), lambda i: (i, 0)),
        out_shape=jax.ShapeDtypeStruct((m, n), f32),
        compiler_params=_params("parallel"),
        name="project",
    )(a, w)


def _proj_ln_kernel(a_ref, w_ref, res_ref, g_ref, b_ref, o_ref):
    h = _mm(a_ref[...].astype(MXU_DTYPE), w_ref[...])
    o_ref[...] = _layer_norm_rows(DEEPNORM_ALPHA * res_ref[...] + h, g_ref[...], b_ref[...])


def project_residual_ln(a, w, res, g, b, tm=512):
    m, k = a.shape
    n = w.shape[1]
    return pl.pallas_call(
        _proj_ln_kernel,
        grid=(m // tm,),
        in_specs=[pl.BlockSpec((tm, k), lambda i: (i, 0)), pl.BlockSpec((k, n), lambda i: (0, 0)),
                  pl.BlockSpec((tm, n), lambda i: (i, 0)),
                  pl.BlockSpec((1, n), lambda i: (0, 0)), pl.BlockSpec((1, n), lambda i: (0, 0))],
        out_specs=pl.BlockSpec((tm, n), lambda i: (i, 0)),
        out_shape=jax.ShapeDtypeStruct((m, n), f32),
        compiler_params=_params("parallel"),
        name="project_residual_ln",
    )(a, w, res, g.reshape(1, n), b.reshape(1, n))


def _swiglu_tile(xb, w1_ref, w3_ref, w2_ref):
    a = _mm(xb, w1_ref[...])
    h = (a * jax.nn.sigmoid(a)) * _mm(xb, w3_ref[...])
    return _mm(h.astype(MXU_DTYPE), w2_ref[...])


def _ffn_ln_kernel(x_ref, w1_ref, w3_ref, w2_ref, g_ref, b_ref, o_ref, xb_ref, acc_ref):
    f = pl.program_id(1)

    @pl.when(f == 0)
    def _():
        xb_ref[...] = x_ref[...].astype(MXU_DTYPE)
        acc_ref[...] = jnp.zeros_like(acc_ref)

    acc_ref[...] += _swiglu_tile(xb_ref[...], w1_ref, w3_ref, w2_ref)

    @pl.when(f == pl.num_programs(1) - 1)
    def _():
        o_ref[...] = _layer_norm_rows(DEEPNORM_ALPHA * x_ref[...] + acc_ref[...], g_ref[...], b_ref[...])


def ffn_residual_ln(x, w1, w3, w2, g, b, tm, tf):
    m, d = x.shape
    ff = w1.shape[1]
    return pl.pallas_call(
        _ffn_ln_kernel,
        grid=(m // tm, ff // tf),
        in_specs=[pl.BlockSpec((tm, d), lambda i, f: (i, 0)),
                  pl.BlockSpec((d, tf), lambda i, f: (0, f)),
                  pl.BlockSpec((d, tf), lambda i, f: (0, f)),
                  pl.BlockSpec((tf, d), lambda i, f: (f, 0)),
                  pl.BlockSpec((1, d), lambda i, f: (0, 0)), pl.BlockSpec((1, d), lambda i, f: (0, 0))],
        out_specs=pl.BlockSpec((tm, d), lambda i, f: (i, 0)),
        out_shape=jax.ShapeDtypeStruct((m, d), f32),
        scratch_shapes=[pltpu.VMEM((tm, d), MXU_DTYPE), pltpu.VMEM((tm, d), f32)],
        compiler_params=_params("parallel", "arbitrary"),
        name="ffn_residual_ln",
    )(x, w1, w3, w2, g.reshape(1, d), b.reshape(1, d))


ROUTE_IDS = N_EXPERTS
ROUTE_GATES = N_EXPERTS + 2


def _router_kernel(x_ref, w_ref, o_ref, *, n_experts):
    logits = jnp.dot(x_ref[...], w_ref[...], preferred_element_type=f32, precision=lax.Precision.HIGHEST)
    lane = lax.broadcasted_iota(i32, logits.shape, 1).astype(f32)
    logits = jnp.where(lane < n_experts, logits, -jnp.inf)
    v1 = jnp.max(logits, axis=1, keepdims=True)
    i1 = jnp.min(jnp.where(logits == v1, lane, float(LANES)), axis=1, keepdims=True)
    rest = jnp.where(lane == i1, -jnp.inf, logits)
    v2 = jnp.max(rest, axis=1, keepdims=True)
    i2 = jnp.min(jnp.where(rest == v2, lane, float(LANES)), axis=1, keepdims=True)
    e2 = jnp.exp(v2 - v1)
    g1 = 1.0 / (1.0 + e2)
    g2 = e2 / (1.0 + e2)
    out = jnp.where(lane == ROUTE_IDS, i1, 0.0) + jnp.where(lane == ROUTE_IDS + 1, i2, 0.0)
    out = out + jnp.where(lane == ROUTE_GATES, g1, 0.0) + jnp.where(lane == ROUTE_GATES + 1, g2, 0.0)
    o_ref[...] = out


def route_top2(x, router_w, tm=512):
    m, d = x.shape
    n_experts = router_w.shape[1]
    w = jnp.zeros((d, LANES), f32).at[:, :n_experts].set(router_w)
    return pl.pallas_call(
        functools.partial(_router_kernel, n_experts=n_experts),
        grid=(m // tm,),
        in_specs=[pl.BlockSpec((tm, d), lambda i: (i, 0)), pl.BlockSpec((d, LANES), lambda i: (0, 0))],
        out_specs=pl.BlockSpec((tm, LANES), lambda i: (i, 0)),
        out_shape=jax.ShapeDtypeStruct((m, LANES), f32),
        compiler_params=_params("parallel"),
        name="route_top2",
    )(x, w)


def _moe_ln_kernel(tok_ref, gs_ref, off_ref, x_ref, w1_ref, w3_ref, w2_ref, g_ref, b_ref, o_ref,
                   xg_ref, xb_ref, y_ref, *, rt):
    c = pl.program_id(0)
    e = pl.program_id(1)
    f = pl.program_id(2)
    last_f = pl.num_programs(2) - 1
    start = off_ref[0, e]
    count = off_ref[0, e + 1] - start
    n_tiles = (count + rt - 1) // rt

    @pl.when((c == 0) & (e == 0) & (f == 0))
    def _():
        xg_ref[...] = jnp.zeros_like(xg_ref)

    @pl.when((e == 0) & (f == 0))
    def _():
        o_ref[...] = jnp.zeros_like(o_ref)

    @pl.when(f == 0)
    def _():
        def gather(r, carry):
            t = tok_ref[0, start + r]
            xg_ref[pl.ds(r, 1), :] = x_ref[pl.ds(t, 1), :]
            return carry
        lax.fori_loop(0, count, gather, 0)

        def cast(j, carry):
            rows = pl.ds(pl.multiple_of(j * rt, rt), rt)
            xb_ref[rows, :] = xg_ref[rows, :].astype(MXU_DTYPE)
            return carry
        lax.fori_loop(0, n_tiles, cast, 0)

    def tile(j, carry):
        rows = pl.ds(pl.multiple_of(j * rt, rt), rt)
        y = _swiglu_tile(xb_ref[rows, :], w1_ref, w3_ref, w2_ref)

        @pl.when(f == 0)
        def _():
            y_ref[rows, :] = y

        @pl.when(f != 0)
        def _():
            y_ref[rows, :] += y
        return carry
    lax.fori_loop(0, n_tiles, tile, 0)

    @pl.when(f == last_f)
    def _():
        def scatter(r, carry):
            t = tok_ref[0, start + r]
            o_ref[pl.ds(t, 1), :] += gs_ref[0, start + r] * y_ref[pl.ds(r, 1), :]
            return carry
        lax.fori_loop(0, count, scatter, 0)

    @pl.when((e == pl.num_programs(1) - 1) & (f == last_f))
    def _():
        o_ref[...] = _layer_norm_rows(DEEPNORM_ALPHA * x_ref[...] + o_ref[...], g_ref[...], b_ref[...])


def moe_residual_ln(x, routes, w1, w3, w2, g, b, tm, tf, rt=128):
    m, d = x.shape
    n_experts, _, ff = w1.shape
    n_chunks = m // tm
    ids = routes[:, ROUTE_IDS:ROUTE_IDS + TOP_K].astype(i32).reshape(n_chunks, tm * TOP_K)
    gts = routes[:, ROUTE_GATES:ROUTE_GATES + TOP_K].reshape(n_chunks, tm * TOP_K)
    order = jnp.argsort(ids, axis=1, stable=True).astype(i32)
    tok = order // TOP_K
    gs = jnp.take_along_axis(gts, order, axis=1)
    counts = jnp.sum(ids[:, :, None] == jnp.arange(n_experts, dtype=i32)[None, None, :], axis=1, dtype=i32)
    offs = jnp.concatenate([jnp.zeros((n_chunks, 1), i32), jnp.cumsum(counts, axis=1, dtype=i32)], axis=1)
    smem = lambda width: pl.BlockSpec((None, 1, width), lambda c, e, f: (c, 0, 0), memory_space=pltpu.SMEM)
    return pl.pallas_call(
        functools.partial(_moe_ln_kernel, rt=rt),
        grid=(n_chunks, n_experts, ff // tf),
        in_specs=[smem(tm * TOP_K), smem(tm * TOP_K), smem(n_experts + 1),
                  pl.BlockSpec((tm, d), lambda c, e, f: (c, 0)),
                  pl.BlockSpec((None, d, tf), lambda c, e, f: (e, 0, f)),
                  pl.BlockSpec((None, d, tf), lambda c, e, f: (e, 0, f)),
                  pl.BlockSpec((None, tf, d), lambda c, e, f: (e, f, 0)),
                  pl.BlockSpec((1, d), lambda c, e, f: (0, 0)), pl.BlockSpec((1, d), lambda c, e, f: (0, 0))],
        out_specs=pl.BlockSpec((tm, d), lambda c, e, f: (c, 0)),
        out_shape=jax.ShapeDtypeStruct((m, d), f32),
        scratch_shapes=[pltpu.VMEM((tm, d), f32), pltpu.VMEM((tm, d), MXU_DTYPE), pltpu.VMEM((tm, d), f32)],
        compiler_params=_params("arbitrary", "arbitrary", "arbitrary"),
        name="moe_residual_ln",
    )(tok[:, None, :], gs[:, None, :], offs[:, None, :], x, w1, w3, w2, g.reshape(1, d), b.reshape(1, d))


def _flash_reset(m_ref, acc_ref):
    m_ref[...] = jnp.full(m_ref.shape, NEG, f32)
    acc_ref[...] = jnp.zeros(acc_ref.shape, f32)


def _flash_rows(q, kt, bias, m_ref, alpha_ref, p_ref, rows):
    s = _mm(q, kt)
    if bias is not None:
        s = s + bias
    m_prev = m_ref[rows, :]
    m_next = jnp.maximum(m_prev, jnp.max(s, axis=1, keepdims=True))
    alpha_ref[rows, :] = jnp.exp(m_prev - m_next)
    m_ref[rows, :] = m_next
    p_ref[rows, :] = jnp.exp(s - pltpu.repeat(m_next, s.shape[1] // LANES, axis=1)).astype(p_ref.dtype)


def _flash_accumulate(v_aug, alpha_ref, p_ref, acc_ref):
    alpha = alpha_ref[...]
    reps = acc_ref.shape[-1] // LANES
    if reps > 1:
        alpha = pltpu.repeat(alpha, reps, axis=1)
    acc_ref[...] = alpha * acc_ref[...] + _mm(p_ref[...], v_aug)


def _with_ones(v):
    return jnp.concatenate([v, jnp.ones_like(v)], axis=-1)


def _row_slices(n_rows):
    return [slice(r * Q_BLOCK, (r + 1) * Q_BLOCK) for r in range(n_rows // Q_BLOCK)]


def _dsa_kernel(qi_ref, wi_ref, kit_ref, qa_ref, kat_ref, va_ref, o_ref,
                skey_ref, wrep_ref, m_ref, alpha_ref, acc_ref, p_ref, *, topk, kc):
    qb = Q_BLOCK
    i = pl.program_id(1)
    q0 = i * qb
    n_chunks = (q0 + qb + kc - 1) // kc
    t_row = q0 + lax.broadcasted_iota(i32, (qb, kc), 0)
    col = lax.broadcasted_iota(i32, (qb, kc), 1)
    wi = wi_ref[...]
    qi = qi_ref[...]
    qis = [qi[:, h * IDX_DIM:(h + 1) * IDX_DIM] for h in range(IDX_HEADS)]
    for h in range(IDX_HEADS):
        wrep_ref[h] = jnp.broadcast_to(wi[:, h:h + 1], (qb, kc))

    def score_body(c, carry):
        off = pl.multiple_of(c * kc, kc)
        kt = kit_ref[:, pl.ds(off, kc)]
        s = jnp.zeros((qb, kc), f32)
        for h in range(IDX_HEADS):
            s = s + wrep_ref[h] * jnp.maximum(_mm(qis[h], kt), 0.0)
        s = jnp.where(col + off <= t_row, s, NEG)
        bits = pltpu.bitcast(s, i32)
        skey_ref[:, pl.ds(off, kc)] = jnp.where(bits < 0, bits ^ 0x7FFFFFFF, bits)
        return carry

    n_strips = (q0 + qb + COUNT_STRIP - 1) // COUNT_STRIP
    lax.fori_loop(0, n_strips * (COUNT_STRIP // kc), score_body, 0)

    def count_ge(cand):
        def body(c, acc):
            off = pl.multiple_of(c * COUNT_STRIP, COUNT_STRIP)
            for j in range(COUNT_STRIP // LANES):
                acc = acc + jnp.where(skey_ref[:, pl.ds(off + j * LANES, LANES)] >= cand, 1.0, 0.0)
            return acc
        acc = lax.fori_loop(0, n_strips, body, jnp.zeros((qb, LANES), f32))
        return jnp.sum(acc, axis=1, keepdims=True)

    thr = jnp.where(count_ge(jnp.zeros((qb, 1), i32)) >= topk, 0, INT_MIN).astype(i32)

    def bisect(b, thr):
        cand = thr + jnp.left_shift(jnp.int32(1), 30 - b)
        return jnp.where(count_ge(cand) >= topk, cand, thr)

    thr = lax.fori_loop(0, 31, bisect, thr)
    quota = topk - count_ge(thr + 1)

    before = (lax.broadcasted_iota(i32, (kc, kc), 0) < lax.broadcasted_iota(i32, (kc, kc), 1))
    before = jnp.where(before, 1.0, 0.0).astype(MXU_DTYPE)
    _flash_reset(m_ref, acc_ref)
    heads = _row_slices(A_HEADS * qb)

    def attn_body(c, ties_seen):
        off = pl.multiple_of(c * kc, kc)
        key = skey_ref[:, pl.ds(off, kc)]
        eq = key == thr
        eqf = jnp.where(eq, 1.0, 0.0)
        rank = ties_seen + _mm(eqf.astype(MXU_DTYPE), before)
        sel = ((key > thr) | (eq & (rank < quota))) & (col + off <= t_row)
        bias = jnp.where(sel, 0.0, NEG)
        kt = kat_ref[:, pl.ds(off, kc)]
        for rows in heads:
            _flash_rows(qa_ref[rows, :], kt, bias, m_ref, alpha_ref, p_ref, rows)
        _flash_accumulate(va_ref[pl.ds(off, kc), :], alpha_ref, p_ref, acc_ref)
        return ties_seen + jnp.sum(eqf, axis=1, keepdims=True)

    lax.fori_loop(0, n_chunks, attn_body, jnp.zeros((qb, 1), f32))
    acc = acc_ref[...]
    o_ref[...] = acc[:, :HEAD_DIM] / acc[:, HEAD_DIM:]


def dsa_attention(qi, wi, kit, qa, kat, va, topk, kc=256):
    bsz, seq, _ = qi.shape
    nb = seq // Q_BLOCK
    rows = A_HEADS * Q_BLOCK
    kern = functools.partial(_dsa_kernel, topk=topk, kc=kc)
    return pl.pallas_call(
        kern,
        grid=(bsz, nb),
        in_specs=[pl.BlockSpec((None, Q_BLOCK, IDX_HEADS * IDX_DIM), lambda b, i: (b, i, 0)),
                  pl.BlockSpec((None, Q_BLOCK, IDX_HEADS), lambda b, i: (b, i, 0)),
                  pl.BlockSpec((None, IDX_DIM, seq), lambda b, i: (b, 0, 0)),
                  pl.BlockSpec((None, None, rows, HEAD_DIM), lambda b, i: (b, i, 0, 0)),
                  pl.BlockSpec((None, HEAD_DIM, seq), lambda b, i: (b, 0, 0)),
                  pl.BlockSpec((None, seq, 2 * HEAD_DIM), lambda b, i: (b, 0, 0))],
        out_specs=pl.BlockSpec((None, None, rows, HEAD_DIM), lambda b, i: (b, i, 0, 0)),
        out_shape=jax.ShapeDtypeStruct((bsz, nb, rows, HEAD_DIM), f32),
        scratch_shapes=[pltpu.VMEM((Q_BLOCK, seq), i32), pltpu.VMEM((IDX_HEADS, Q_BLOCK, kc), f32),
                        pltpu.VMEM((rows, LANES), f32), pltpu.VMEM((rows, LANES), f32),
                        pltpu.VMEM((rows, 2 * HEAD_DIM), f32), pltpu.VMEM((rows, kc), MXU_DTYPE)],
        compiler_params=_params("parallel", "arbitrary"),
        name="dsa_attention",
    )(qi, wi, kit, qa, kat, va)


def _compress_kernel(x_ref, pe_ref, w1_ref, w2_ref, o_ref):
    half = (CMP_LEN // 2) * HEAD_DIM
    x = x_ref[...]
    first = _mm((x + pe_ref[0:1, :]).astype(MXU_DTYPE), w1_ref[0:half, :])
    second = _mm((x + pe_ref[1:2, :]).astype(MXU_DTYPE), w1_ref[half:2 * half, :])
    pre = first + pltpu.roll(second, shift=x.shape[0] - 1, axis=0)
    hid = pre * jax.nn.sigmoid(pre)
    o_ref[...] = _mm(hid.astype(MXU_DTYPE), w2_ref[...])


def nsa_compress(chunks, pe2, w1, w2):
    bsz, groups, nck, width = chunks.shape
    return pl.pallas_call(
        _compress_kernel,
        grid=(bsz, groups),
        in_specs=[pl.BlockSpec((None, None, nck, width), lambda b, g: (b, g, 0, 0)),
                  pl.BlockSpec((2, width), lambda b, g: (0, 0)),
                  pl.BlockSpec((2 * width, CMP_HIDDEN), lambda b, g: (0, 0)),
                  pl.BlockSpec((CMP_HIDDEN, HEAD_DIM), lambda b, g: (0, 0))],
        out_specs=pl.BlockSpec((None, None, nck, HEAD_DIM), lambda b, g: (b, g, 0, 0)),
        out_shape=jax.ShapeDtypeStruct((bsz, groups, nck, HEAD_DIM), f32),
        compiler_params=_params("parallel", "parallel"),
        name="nsa_compress",
    )(chunks, pe2, w1, w2)


def _nsa_kernel(q_ref, graw_ref, kct_ref, vc_ref, kst_ref, vs_ref, kwt_ref, vw_ref, ovl_ref, o_ref,
                m_ref, alpha_ref, acc_ref, p_ref, pw_ref, *, n_slc, n_sel, kc):
    qb = Q_BLOCK
    i = pl.program_id(2)
    q0 = i * qb
    n_cmp = kct_ref.shape[1]
    heads = _row_slices(B_REP * qb)

    t_c = q0 + lax.broadcasted_iota(i32, (qb, n_cmp), 0)
    cmp_end = lax.broadcasted_iota(i32, (qb, n_cmp), 1) * CMP_STRIDE + (CMP_LEN - 1)
    vis = cmp_end <= t_c
    kct = kct_ref[...]
    p_sum = jnp.zeros((qb, n_cmp), f32)
    for rows in heads:
        lc = jnp.where(vis, _mm(q_ref[rows, :], kct), NEG)
        ec = jnp.where(vis, jnp.exp(lc - jnp.max(lc, axis=1, keepdims=True)), 0.0)
        den = jnp.sum(ec, axis=1, keepdims=True)
        p_c = ec / jnp.where(den > 0.0, den, 1.0)
        p_sum = p_sum + p_c
        acc_ref[rows, :HEAD_DIM] = _mm(p_c.astype(MXU_DTYPE), vc_ref[...])
    o_c = acc_ref[:, :HEAD_DIM]

    imp = jnp.dot(p_sum, ovl_ref[...], preferred_element_type=f32, precision=lax.Precision.HIGHEST).T
    t_q = q0 + lax.broadcasted_iota(i32, (LANES, qb), 1)
    blk = lax.broadcasted_iota(i32, (LANES, qb), 0)
    blk_t = t_q // SLC_LEN
    forced = (blk == 0) | (blk == blk_t) | (blk == blk_t - 1)
    imp = jnp.where(forced, FORCED_BOOST, imp)
    imp = jnp.where(blk * SLC_LEN <= t_q, imp, NEG)
    imp = jnp.where(blk < n_slc, imp, -jnp.inf)
    blk_f = blk.astype(f32)

    def pick(_, carry):
        imp, selm = carry
        best = jnp.max(imp, axis=0, keepdims=True)
        first = jnp.min(jnp.where(imp == best, blk_f, float(LANES)), axis=0, keepdims=True)
        hit = blk_f == first
        return jnp.where(hit, -jnp.inf, imp), jnp.where(hit, 1.0, selm)

    _, selm = lax.fori_loop(0, n_sel, pick, (imp, jnp.zeros((LANES, qb), f32)))
    selm = selm.T.astype(MXU_DTYPE)

    t_k = q0 + lax.broadcasted_iota(i32, (qb, kc), 0)
    col = lax.broadcasted_iota(i32, (qb, kc), 1)
    exp_row = lax.broadcasted_iota(i32, (LANES, kc), 0)
    exp_col = lax.broadcasted_iota(i32, (LANES, kc), 1)
    _flash_reset(m_ref, acc_ref)

    def slc_body(c, carry):
        off = pl.multiple_of(c * kc, kc)
        expand = jnp.where(exp_row == (exp_col + off) // SLC_LEN, 1.0, 0.0).astype(MXU_DTYPE)
        sel = (_mm(selm, expand) > 0.5) & (col + off <= t_k)
        bias = jnp.where(sel, 0.0, NEG)
        kt = kst_ref[:, pl.ds(off, kc)]
        for rows in heads:
            _flash_rows(q_ref[rows, :], kt, bias, m_ref, alpha_ref, p_ref, rows)
        _flash_accumulate(vs_ref[pl.ds(off, kc), :], alpha_ref, p_ref, acc_ref)
        return carry

    lax.fori_loop(0, (q0 + qb + kc - 1) // kc, slc_body, 0)
    acc = acc_ref[...]
    o_s = acc[:, :HEAD_DIM] / acc[:, HEAD_DIM:]

    slab = WINDOW + qb
    w0 = pl.multiple_of(jnp.maximum(q0 - WINDOW, 0), qb)
    dist = (q0 + lax.broadcasted_iota(i32, (qb, slab), 0)) - (w0 + lax.broadcasted_iota(i32, (qb, slab), 1))
    bias = jnp.where((dist >= 0) & (dist < WINDOW), 0.0, NEG)
    kt = kwt_ref[:, pl.ds(w0, slab)]
    for rows in heads:
        s = _mm(q_ref[rows, :], kt) + bias
        pw_ref[rows, :] = jnp.exp(s - jnp.max(s, axis=1, keepdims=True)).astype(pw_ref.dtype)
    acc = _mm(pw_ref[...], vw_ref[pl.ds(w0, slab), :])
    o_w = acc[:, :HEAD_DIM] / acc[:, HEAD_DIM:]

    gates = jax.nn.sigmoid(graw_ref[...])
    o_ref[...] = gates[:, 0:1] * o_c + gates[:, 1:2] * o_s + gates[:, 2:3] * o_w


def nsa_attention(q, graw, kct, vc, kst, vs, kwt, vw, ovl, n_slc, n_sel, kc=256):
    bsz, groups, nb, rows, _ = q.shape
    seq = kst.shape[-1]
    n_cmp = kct.shape[-1]
    kern = functools.partial(_nsa_kernel, n_slc=n_slc, n_sel=n_sel, kc=kc)
    per_bg = lambda b, g, i: (b, g, 0, 0)
    return pl.pallas_call(
        kern,
        grid=(bsz, groups, nb),
        in_specs=[pl.BlockSpec((None, None, None, rows, HEAD_DIM), lambda b, g, i: (b, g, i, 0, 0)),
                  pl.BlockSpec((None, None, None, rows, 3), lambda b, g, i: (b, g, i, 0, 0)),
                  pl.BlockSpec((None, None, HEAD_DIM, n_cmp), per_bg),
                  pl.BlockSpec((None, None, n_cmp, HEAD_DIM), per_bg),
                  pl.BlockSpec((None, None, HEAD_DIM, seq), per_bg),
                  pl.BlockSpec((None, None, seq, 2 * HEAD_DIM), per_bg),
                  pl.BlockSpec((None, None, HEAD_DIM, seq), per_bg),
                  pl.BlockSpec((None, None, seq, 2 * HEAD_DIM), per_bg),
                  pl.BlockSpec((n_cmp, LANES), lambda b, g, i: (0, 0))],
        out_specs=pl.BlockSpec((None, None, None, rows, HEAD_DIM), lambda b, g, i: (b, g, i, 0, 0)),
        out_shape=jax.ShapeDtypeStruct((bsz, groups, nb, rows, HEAD_DIM), f32),
        scratch_shapes=[pltpu.VMEM((rows, LANES), f32), pltpu.VMEM((rows, LANES), f32),
                        pltpu.VMEM((rows, 2 * HEAD_DIM), f32), pltpu.VMEM((rows, kc), MXU_DTYPE),
                        pltpu.VMEM((rows, WINDOW + Q_BLOCK), MXU_DTYPE)],
        compiler_params=_params("parallel", "parallel", "arbitrary"),
        name="nsa_attention",
    )(q, graw, kct, vc, kst, vs, kwt, vw, ovl)


def _diff_kernel(lam_ref, q_ref, kt_ref, v_ref, g_ref, o_ref, m_ref, alpha_ref, acc_ref, p_ref,
                 *, tq, kc, out_scale):
    i = pl.program_id(2)
    q0 = i * tq
    dv = v_ref.shape[-1] // 2
    groups = _row_slices(tq)
    col = lax.broadcasted_iota(i32, (Q_BLOCK, kc), 1)
    t_row = [q0 + r * Q_BLOCK + lax.broadcasted_iota(i32, (Q_BLOCK, kc), 0) for r in range(len(groups))]
    for half in range(2):
        _flash_reset(m_ref.at[half], acc_ref.at[half])

    def chunk(c, masked):
        off = pl.multiple_of(c * kc, kc)
        v = v_ref[pl.ds(off, kc), :]
        for half in range(2):
            kt = kt_ref[half, :, pl.ds(off, kc)]
            for r, rows in enumerate(groups):
                bias = jnp.where(col + off <= t_row[r], 0.0, NEG) if masked else None
                _flash_rows(q_ref[half, rows, :], kt, bias, m_ref.at[half], alpha_ref.at[half], p_ref.at[half], rows)
            _flash_accumulate(v, alpha_ref.at[half], p_ref.at[half], acc_ref.at[half])

    def full_body(c, carry):
        chunk(c, False)
        return carry

    def diag_body(c, carry):
        chunk(c, True)
        return carry

    n_full = q0 // kc
    lax.fori_loop(0, n_full, full_body, 0)
    lax.fori_loop(n_full, (q0 + tq + kc - 1) // kc, diag_body, 0)
    a1 = acc_ref[0]
    a2 = acc_ref[1]
    o = a1[:, :dv] / a1[:, dv:] - lam_ref[0] * (a2[:, :dv] / a2[:, dv:])
    o = o * lax.rsqrt(jnp.mean(o * o, axis=-1, keepdims=True) + LN_EPS) * g_ref[...]
    o_ref[...] = o * out_scale


def diff_attention(lam, q, kt, v, subln_g, out_scale, tq=256, kc=512):
    bsz, heads, _, seq, _ = q.shape
    dv = v.shape[-1] // 2
    tq = min(tq, seq)
    kc = min(kc, seq)
    kern = functools.partial(_diff_kernel, tq=tq, kc=kc, out_scale=out_scale)
    return pl.pallas_call(
        kern,
        grid=(bsz, heads, seq // tq),
        in_specs=[pl.BlockSpec(memory_space=pltpu.SMEM),
                  pl.BlockSpec((None, None, 2, tq, HEAD_DIM), lambda b, h, i: (b, h, 0, i, 0)),
                  pl.BlockSpec((None, None, 2, HEAD_DIM, seq), lambda b, h, i: (b, h, 0, 0, 0)),
                  pl.BlockSpec((None, None, seq, 2 * dv), lambda b, h, i: (b, h, 0, 0)),
                  pl.BlockSpec((1, dv), lambda b, h, i: (0, 0))],
        out_specs=pl.BlockSpec((None, tq, dv), lambda b, h, i: (b, i, h)),
        out_shape=jax.ShapeDtypeStruct((bsz, seq, heads * dv), f32),
        scratch_shapes=[pltpu.VMEM((2, tq, LANES), f32), pltpu.VMEM((2, tq, LANES), f32),
                        pltpu.VMEM((2, tq, 2 * dv), f32), pltpu.VMEM((2, tq, kc), MXU_DTYPE)],
        compiler_params=_params("parallel", "parallel", "arbitrary"),
        name="diff_attention",
    )(lam, q, kt, v, subln_g.reshape(1, dv))


def _split(h, layout):
    offs = np.cumsum([w for _, w in layout])[:-1].tolist()
    return jnp.split(h, offs, axis=-1)


def _rope_tables(positions):
    inv_freq = ROPE_THETA ** (-jnp.arange(0, ROT_DIM, 2, dtype=f32) / ROT_DIM)
    ang = positions.astype(f32)[..., None] * inv_freq
    return jnp.cos(ang), jnp.sin(ang)


def _apply_rope(x, cos, sin):
    shape = cos.shape[:2] + (1,) * (x.ndim - 3) + cos.shape[-1:]
    c = cos.reshape(shape)
    s = sin.reshape(shape)
    half = ROT_DIM // 2
    x1, x2 = x[..., :half], x[..., half:ROT_DIM]
    return jnp.concatenate([x1 * c - x2 * s, x2 * c + x1 * s, x[..., ROT_DIM:]], axis=-1)


def _pad_cols(w, mult):
    n = w.shape[-1]
    pad = (-n) % mult
    return jnp.pad(w, ((0, 0), (0, pad))) if pad else w


def _overlap_matrix(n_cmp_rows, n_slc):
    c_start = np.arange(n_cmp_rows) * CMP_STRIDE
    s_start = np.arange(LANES) * SLC_LEN
    ovl = (c_start[:, None] < s_start[None, :] + SLC_LEN) & (c_start[:, None] + CMP_LEN > s_start[None, :])
    ovl = ovl & (np.arange(LANES)[None, :] < n_slc)
    return jnp.asarray(ovl.astype(np.float32))


def _ab_mixer(x2, bsz, seq, positions, cos, sin, w_in, pe_k, pe_v, ck1, ck2, cv1, cv2):
    nb = seq // Q_BLOCK
    width = sum(w for _, w in AB_LAYOUT)
    h = project(x2, _pad_cols(w_in, 2 * LANES).astype(MXU_DTYPE))[:, :width].reshape(bsz, seq, width)
    (q_a, k_a, v_a, q_idx, k_idx, w_idx, q_b, k_cmp, v_cmp, k_slc, v_slc, k_win, v_win, g_b) = _split(h, AB_LAYOUT)

    q_a = _apply_rope(q_a.reshape(bsz, seq, A_HEADS, HEAD_DIM), cos, sin) * QK_SCALE
    q_a = q_a.reshape(bsz, nb, Q_BLOCK, A_HEADS, HEAD_DIM).transpose(0, 1, 3, 2, 4)
    q_a = q_a.reshape(bsz, nb, A_HEADS * Q_BLOCK, HEAD_DIM).astype(MXU_DTYPE)
    k_a = _apply_rope(k_a, cos, sin).transpose(0, 2, 1).astype(MXU_DTYPE)
    q_idx = _apply_rope(q_idx.reshape(bsz, seq, IDX_HEADS, IDX_DIM), cos, sin)
    q_idx = q_idx.reshape(bsz, seq, IDX_HEADS * IDX_DIM).astype(MXU_DTYPE)
    k_idx = _apply_rope(k_idx, cos, sin).transpose(0, 2, 1).astype(MXU_DTYPE)
    w_idx = w_idx * (IDX_HEADS * IDX_DIM) ** -0.5
    o_a = dsa_attention(q_idx, w_idx, k_idx, q_a, k_a, _with_ones(v_a.astype(MXU_DTYPE)),
                        topk=min(DSA_TOPK, seq // 4))
    o_a = o_a.reshape(bsz, nb, A_HEADS, Q_BLOCK, HEAD_DIM).transpose(0, 1, 3, 2, 4)
    o_a = o_a.reshape(bsz * seq, A_HEADS * HEAD_DIM)

    groups = B_KV_GROUPS
    n_rows = seq // CMP_STRIDE
    n_cmp = (seq - CMP_LEN) // CMP_STRIDE + 1
    assert n_cmp == n_rows - 1
    n_slc = seq // SLC_LEN
    assert n_slc <= LANES

    def chunked(kv):
        kv = kv.reshape(bsz, n_rows, CMP_STRIDE, groups, HEAD_DIM).transpose(0, 3, 1, 2, 4)
        return kv.reshape(bsz, groups, n_rows, CMP_STRIDE * HEAD_DIM)

    def pe_halves(pe):
        return pe.reshape(2, CMP_STRIDE * HEAD_DIM)

    k_c = nsa_compress(chunked(k_cmp), pe_halves(pe_k), ck1.astype(MXU_DTYPE), ck2.astype(MXU_DTYPE))
    v_c = nsa_compress(chunked(v_cmp), pe_halves(pe_v), cv1.astype(MXU_DTYPE), cv2.astype(MXU_DTYPE))
    cmp_end = jnp.minimum(jnp.arange(n_rows) * CMP_STRIDE + CMP_LEN - 1, seq - 1)
    cos_c, sin_c = _rope_tables(positions[:, cmp_end])
    k_c = _apply_rope(k_c.transpose(0, 2, 1, 3), cos_c, sin_c)
    k_c = k_c.transpose(0, 2, 3, 1).astype(MXU_DTYPE)
    v_c = v_c.astype(MXU_DTYPE)

    def keys_t(k, rope):
        k = k.reshape(bsz, seq, groups, HEAD_DIM)
        if rope:
            k = _apply_rope(k, cos, sin)
        return k.transpose(0, 2, 3, 1).astype(MXU_DTYPE)

    def vals(v):
        return _with_ones(v.reshape(bsz, seq, groups, HEAD_DIM).transpose(0, 2, 1, 3).astype(MXU_DTYPE))

    def head_rows(a, last):
        a = a.reshape(bsz, nb, Q_BLOCK, groups, B_REP, last).transpose(0, 3, 1, 4, 2, 5)
        return a.reshape(bsz, groups, nb, B_REP * Q_BLOCK, last)

    q_b = _apply_rope(q_b.reshape(bsz, seq, B_HEADS, HEAD_DIM), cos, sin) * QK_SCALE
    q_b = head_rows(q_b.reshape(bsz, seq, B_HEADS * HEAD_DIM), HEAD_DIM).astype(MXU_DTYPE)
    o_b = nsa_attention(q_b, head_rows(g_b, 3), k_c, v_c, keys_t(k_slc, True), vals(v_slc),
                        keys_t(k_win, True), vals(v_win), _overlap_matrix(n_rows, n_slc),
                        n_slc=n_slc, n_sel=min(SLC_TOPN, n_slc))
    o_b = o_b.reshape(bsz, groups, nb, B_REP, Q_BLOCK, HEAD_DIM).transpose(0, 2, 4, 1, 3, 5)
    o_b = o_b.reshape(bsz * seq, B_HEADS * HEAD_DIM)
    return jnp.concatenate([o_a, o_b], axis=-1)


def _diff_mixer(x2, bsz, seq, cos, sin, w_in, lq1, lk1, lq2, lk2, subln_g, lam_init):
    h = project(x2, w_in.astype(MXU_DTYPE)).reshape(bsz, seq, -1)
    q, k, v = _split(h, C_LAYOUT)
    q = _apply_rope(q.reshape(bsz, seq, C_HEADS, 2, HEAD_DIM), cos, sin) * QK_SCALE
    k = _apply_rope(k.reshape(bsz, seq, C_HEADS, 2, HEAD_DIM), cos, sin)
    q = q.transpose(0, 2, 3, 1, 4).astype(MXU_DTYPE)
    kt = k.transpose(0, 2, 3, 4, 1).astype(MXU_DTYPE)
    v = v.reshape(bsz, seq, C_HEADS, 2 * HEAD_DIM).transpose(0, 2, 1, 3).astype(MXU_DTYPE)
    lam = (jnp.exp(jnp.sum(lq1 * lk1)) - jnp.exp(jnp.sum(lq2 * lk2)) + lam_init).reshape(1).astype(f32)
    o = diff_attention(lam, q, kt, _with_ones(v), subln_g, 1.0 - lam_init)
    return o.reshape(bsz * seq, C_HEADS * 2 * HEAD_DIM)


def kernel(x, positions, ab_w_in, cmp_pe_k, cmp_pe_v, cmp_k_w1, cmp_k_w2, cmp_v_w1, cmp_v_w2, ab_w_out, ln_ab_g, ln_ab_b, ffn_w1, ffn_w3, ffn_w2, ln_ffn_g, ln_ffn_b, c_w_in, lambda_q1, lambda_k1, lambda_q2, lambda_k2, c_subln_g, c_w_out, ln_c_g, ln_c_b, router_w, moe_w1, moe_w3, moe_w2, ln_moe_g, ln_moe_b):
    bsz, seq, d = x.shape
    assert seq % COUNT_STRIP == 0 and seq >= WINDOW + Q_BLOCK and d == D_MODEL
    cos, sin = _rope_tables(positions)
    x2 = x.reshape(bsz * seq, d)
    for layer in range(DEPTH):
        i = layer // 2
        if layer % 2 == 0:
            o = _ab_mixer(x2, bsz, seq, positions, cos, sin, ab_w_in[i], cmp_pe_k[i], cmp_pe_v[i],
                          cmp_k_w1[i], cmp_k_w2[i], cmp_v_w1[i], cmp_v_w2[i])
            x2 = project_residual_ln(o, ab_w_out[i].astype(MXU_DTYPE), x2, ln_ab_g[i], ln_ab_b[i])
            x2 = ffn_residual_ln(x2, ffn_w1[i].astype(MXU_DTYPE), ffn_w3[i].astype(MXU_DTYPE),
                                 ffn_w2[i].astype(MXU_DTYPE), ln_ffn_g[i], ln_ffn_b[i], tm=512, tf=1408)
        else:
            lam_init = 0.8 - 0.6 * math.exp(-0.3 * layer)
            o = _diff_mixer(x2, bsz, seq, cos, sin, c_w_in[i], lambda_q1[i], lambda_k1[i], lambda_q2[i],
                            lambda_k2[i], c_subln_g[i], lam_init)
            x2 = project_residual_ln(o, c_w_out[i].astype(MXU_DTYPE), x2, ln_c_g[i], ln_c_b[i])
            routes = route_top2(x2, router_w[i])
            x2 = moe_residual_ln(x2, routes, moe_w1[i].astype(MXU_DTYPE), moe_w3[i].astype(MXU_DTYPE),
                                 moe_w2[i].astype(MXU_DTYPE), ln_moe_g[i], ln_moe_b[i], tm=1024, tf=896)
    return x2.reshape(bsz, seq, d)
```

```python
import functools
import math

import numpy as np
import jax
import jax.numpy as jnp
from jax import lax
from jax.experimental import pallas as pl
from jax.experimental.pallas import tpu as pltpu

f32 = jnp.float32
i32 = jnp.int32
MXU_DTYPE = jnp.bfloat16
VMEM_LIMIT_BYTES = 56 * 1024 * 1024
LANES = 128

D_MODEL = 1024
DEPTH = 2
HEAD_DIM = 64
ROT_DIM = HEAD_DIM // 4
ROPE_THETA = 500000.0
Q_BLOCK = 128
NEG = -1e30
LN_EPS = 1e-5
A_HEADS = 8
IDX_HEADS = 4
IDX_DIM = 64
DSA_TOPK = 256
B_HEADS = 8
B_KV_GROUPS = 2
B_REP = B_HEADS // B_KV_GROUPS
CMP_LEN = 32
CMP_STRIDE = 16
CMP_HIDDEN = 128
SLC_LEN = 64
SLC_TOPN = 16
WINDOW = 512
FORCED_BOOST = 1e6
C_HEADS = 8
N_EXPERTS = 8
TOP_K = 2
DEEPNORM_ALPHA = (2 * DEPTH) ** 0.25
QK_SCALE = HEAD_DIM ** -0.5
INT_MIN = -(2 ** 31)
COUNT_STRIP = 512

AB_LAYOUT = (
    ("q_a", A_HEADS * HEAD_DIM), ("k_a", HEAD_DIM), ("v_a", HEAD_DIM),
    ("q_idx", IDX_HEADS * IDX_DIM), ("k_idx", IDX_DIM), ("w_idx", IDX_HEADS),
    ("q_b", B_HEADS * HEAD_DIM),
    ("k_cmp", B_KV_GROUPS * HEAD_DIM), ("v_cmp", B_KV_GROUPS * HEAD_DIM),
    ("k_slc", B_KV_GROUPS * HEAD_DIM), ("v_slc", B_KV_GROUPS * HEAD_DIM),
    ("k_win", B_KV_GROUPS * HEAD_DIM), ("v_win", B_KV_GROUPS * HEAD_DIM),
    ("gate_b", 3 * B_HEADS),
)
C_LAYOUT = (("q_c", C_HEADS * 2 * HEAD_DIM), ("k_c", C_HEADS * 2 * HEAD_DIM), ("v_c", C_HEADS * 2 * HEAD_DIM))


def _params(*sem):
    return pltpu.CompilerParams(dimension_semantics=sem, vmem_limit_bytes=VMEM_LIMIT_BYTES)


def _mm(a, b):
    return jnp.dot(a, b, preferred_element_type=f32)


def _layer_norm_rows(y, g, b):
    mu = jnp.mean(y, axis=-1, keepdims=True)
    yc = y - mu
    var = jnp.mean(yc * yc, axis=-1, keepdims=True)
    return yc * lax.rsqrt(var + LN_EPS) * g + b


def _proj_kernel(a_ref, w_ref, o_ref):
    o_ref[...] = _mm(a_ref[...].astype(MXU_DTYPE), w_ref[...]).astype(o_ref.dtype)


def project(a, w, tm=512):
    m, k = a.shape
    n = w.shape[1]
    return pl.pallas_call(
        _proj_kernel,
        grid=(m // tm,),
        in_specs=[pl.BlockSpec((tm, k), lambda i: (i, 0)), pl.BlockSpec((k, n), lambda i: (0, 0))],
        out_specs=pl.BlockSpec((tm, n), lambda i: (i, 0)),
        out_shape=jax.ShapeDtypeStruct((m, n), f32),
        compiler_params=_params("parallel"),
        name="project",
    )(a, w)


def _proj_ln_kernel(a_ref, w_ref, res_ref, g_ref, b_ref, o_ref):
    h = _mm(a_ref[...].astype(MXU_DTYPE), w_ref[...])
    o_ref[...] = _layer_norm_rows(DEEPNORM_ALPHA * res_ref[...] + h, g_ref[...], b_ref[...])


def project_residual_ln(a, w, res, g, b, tm=512):
    m, k = a.shape
    n = w.shape[1]
    return pl.pallas_call(
        _proj_ln_kernel,
        grid=(m // tm,),
        in_specs=[pl.BlockSpec((tm, k), lambda i: (i, 0)), pl.BlockSpec((k, n), lambda i: (0, 0)),
                  pl.BlockSpec((tm, n), lambda i: (i, 0)),
                  pl.BlockSpec((1, n), lambda i: (0, 0)), pl.BlockSpec((1, n), lambda i: (0, 0))],
        out_specs=pl.BlockSpec((tm, n), lambda i: (i, 0)),
        out_shape=jax.ShapeDtypeStruct((m, n), f32),
        compiler_params=_params("parallel"),
        name="project_residual_ln",
    )(a, w, res, g.reshape(1, n), b.reshape(1, n))


def _swiglu_tile(xb, w1_ref, w3_ref, w2_ref):
    a = _mm(xb, w1_ref[...])
    h = (a * jax.nn.sigmoid(a)) * _mm(xb, w3_ref[...])
    return _mm(h.astype(MXU_DTYPE), w2_ref[...])


def _ffn_ln_kernel(x_ref, w1_ref, w3_ref, w2_ref, g_ref, b_ref, o_ref, xb_ref, acc_ref):
    f = pl.program_id(1)

    @pl.when(f == 0)
    def _():
        xb_ref[...] = x_ref[...].astype(MXU_DTYPE)
        acc_ref[...] = jnp.zeros_like(acc_ref)

    acc_ref[...] += _swiglu_tile(xb_ref[...], w1_ref, w3_ref, w2_ref)

    @pl.when(f == pl.num_programs(1) - 1)
    def _():
        o_ref[...] = _layer_norm_rows(DEEPNORM_ALPHA * x_ref[...] + acc_ref[...], g_ref[...], b_ref[...])


def ffn_residual_ln(x, w1, w3, w2, g, b, tm, tf):
    m, d = x.shape
    ff = w1.shape[1]
    return pl.pallas_call(
        _ffn_ln_kernel,
        grid=(m // tm, ff // tf),
        in_specs=[pl.BlockSpec((tm, d), lambda i, f: (i, 0)),
                  pl.BlockSpec((d, tf), lambda i, f: (0, f)),
                  pl.BlockSpec((d, tf), lambda i, f: (0, f)),
                  pl.BlockSpec((tf, d), lambda i, f: (f, 0)),
                  pl.BlockSpec((1, d), lambda i, f: (0, 0)), pl.BlockSpec((1, d), lambda i, f: (0, 0))],
        out_specs=pl.BlockSpec((tm, d), lambda i, f: (i, 0)),
        out_shape=jax.ShapeDtypeStruct((m, d), f32),
        scratch_shapes=[pltpu.VMEM((tm, d), MXU_DTYPE), pltpu.VMEM((tm, d), f32)],
        compiler_params=_params("parallel", "arbitrary"),
        name="ffn_residual_ln",
    )(x, w1, w3, w2, g.reshape(1, d), b.reshape(1, d))


ROUTE_IDS = N_EXPERTS
ROUTE_GATES = N_EXPERTS + 2


def _router_kernel(x_ref, w_ref, o_ref, *, n_experts):
    logits = jnp.dot(x_ref[...], w_ref[...], preferred_element_type=f32, precision=lax.Precision.HIGHEST)
    lane = lax.broadcasted_iota(i32, logits.shape, 1).astype(f32)
    logits = jnp.where(lane < n_experts, logits, -jnp.inf)
    v1 = jnp.max(logits, axis=1, keepdims=True)
    i1 = jnp.min(jnp.where(logits == v1, lane, float(LANES)), axis=1, keepdims=True)
    rest = jnp.where(lane == i1, -jnp.inf, logits)
    v2 = jnp.max(rest, axis=1, keepdims=True)
    i2 = jnp.min(jnp.where(rest == v2, lane, float(LANES)), axis=1, keepdims=True)
    e2 = jnp.exp(v2 - v1)
    g1 = 1.0 / (1.0 + e2)
    g2 = e2 / (1.0 + e2)
    out = jnp.where(lane == ROUTE_IDS, i1, 0.0) + jnp.where(lane == ROUTE_IDS + 1, i2, 0.0)
    out = out + jnp.where(lane == ROUTE_GATES, g1, 0.0) + jnp.where(lane == ROUTE_GATES + 1, g2, 0.0)
    o_ref[...] = out


def route_top2(x, router_w, tm=512):
    m, d = x.shape
    n_experts = router_w.shape[1]
    w = jnp.zeros((d, LANES), f32).at[:, :n_experts].set(router_w)
    return pl.pallas_call(
        functools.partial(_router_kernel, n_experts=n_experts),
        grid=(m // tm,),
        in_specs=[pl.BlockSpec((tm, d), lambda i: (i, 0)), pl.BlockSpec((d, LANES), lambda i: (0, 0))],
        out_specs=pl.BlockSpec((tm, LANES), lambda i: (i, 0)),
        out_shape=jax.ShapeDtypeStruct((m, LANES), f32),
        compiler_params=_params("parallel"),
        name="route_top2",
    )(x, w)


def _moe_ln_kernel(tok_ref, gs_ref, off_ref, x_ref, w1_ref, w3_ref, w2_ref, g_ref, b_ref, o_ref,
                   xg_ref, xb_ref, y_ref, *, rt):
    c = pl.program_id(0)
    e = pl.program_id(1)
    f = pl.program_id(2)
    last_f = pl.num_programs(2) - 1
    start = off_ref[0, e]
    count = off_ref[0, e + 1] - start
    n_tiles = (count + rt - 1) // rt

    @pl.when((c == 0) & (e == 0) & (f == 0))
    def _():
        xg_ref[...] = jnp.zeros_like(xg_ref)

    @pl.when((e == 0) & (f == 0))
    def _():
        o_ref[...] = jnp.zeros_like(o_ref)

    @pl.when(f == 0)
    def _():
        def gather(r, carry):
            t = tok_ref[0, start + r]
            xg_ref[pl.ds(r, 1), :] = x_ref[pl.ds(t, 1), :]
            return carry
        lax.fori_loop(0, count, gather, 0)

        def cast(j, carry):
            rows = pl.ds(pl.multiple_of(j * rt, rt), rt)
            xb_ref[rows, :] = xg_ref[rows, :].astype(MXU_DTYPE)
            return carry
        lax.fori_loop(0, n_tiles, cast, 0)

    def tile(j, carry):
        rows = pl.ds(pl.multiple_of(j * rt, rt), rt)
        y = _swiglu_tile(xb_ref[rows, :], w1_ref, w3_ref, w2_ref)

        @pl.when(f == 0)
        def _():
            y_ref[rows, :] = y

        @pl.when(f != 0)
        def _():
            y_ref[rows, :] += y
        return carry
    lax.fori_loop(0, n_tiles, tile, 0)

    @pl.when(f == last_f)
    def _():
        def scatter(r, carry):
            t = tok_ref[0, start + r]
            o_ref[pl.ds(t, 1), :] += gs_ref[0, start + r] * y_ref[pl.ds(r, 1), :]
            return carry
        lax.fori_loop(0, count, scatter, 0)

    @pl.when((e == pl.num_programs(1) - 1) & (f == last_f))
    def _():
        o_ref[...] = _layer_norm_rows(DEEPNORM_ALPHA * x_ref[...] + o_ref[...], g_ref[...], b_ref[...])


def moe_residual_ln(x, routes, w1, w3, w2, g, b, tm, tf, rt=128):
    m, d = x.shape
    n_experts, _, ff = w1.shape
    n_chunks = m // tm
    ids = routes[:, ROUTE_IDS:ROUTE_IDS + TOP_K].astype(i32).reshape(n_chunks, tm * TOP_K)
    gts = routes[:, ROUTE_GATES:ROUTE_GATES + TOP_K].reshape(n_chunks, tm * TOP_K)
    order = jnp.argsort(ids, axis=1, stable=True).astype(i32)
    tok = order // TOP_K
    gs = jnp.take_along_axis(gts, order, axis=1)
    counts = jnp.sum(ids[:, :, None] == jnp.arange(n_experts, dtype=i32)[None, None, :], axis=1, dtype=i32)
    offs = jnp.concatenate([jnp.zeros((n_chunks, 1), i32), jnp.cumsum(counts, axis=1, dtype=i32)], axis=1)
    smem = lambda width: pl.BlockSpec((None, 1, width), lambda c, e, f: (c, 0, 0), memory_space=pltpu.SMEM)
    return pl.pallas_call(
        functools.partial(_moe_ln_kernel, rt=rt),
        grid=(n_chunks, n_experts, ff // tf),
        in_specs=[smem(tm * TOP_K), smem(tm * TOP_K), smem(n_experts + 1),
                  pl.BlockSpec((tm, d), lambda c, e, f: (c, 0)),
                  pl.BlockSpec((None, d, tf), lambda c, e, f: (e, 0, f)),
                  pl.BlockSpec((None, d, tf), lambda c, e, f: (e, 0, f)),
                  pl.BlockSpec((None, tf, d), lambda c, e, f: (e, f, 0)),
                  pl.BlockSpec((1, d), lambda c, e, f: (0, 0)), pl.BlockSpec((1, d), lambda c, e, f: (0, 0))],
        out_specs=pl.BlockSpec((tm, d), lambda c, e, f: (c, 0)),
        out_shape=jax.ShapeDtypeStruct((m, d), f32),
        scratch_shapes=[pltpu.VMEM((tm, d), f32), pltpu.VMEM((tm, d), MXU_DTYPE), pltpu.VMEM((tm, d), f32)],
        compiler_params=_params("arbitrary", "arbitrary", "arbitrary"),
        name="moe_residual_ln",
    )(tok[:, None, :], gs[:, None, :], offs[:, None, :], x, w1, w3, w2, g.reshape(1, d), b.reshape(1, d))


def _flash_reset(m_ref, acc_ref):
    m_ref[...] = jnp.full(m_ref.shape, NEG, f32)
    acc_ref[...] = jnp.zeros(acc_ref.shape, f32)


def _flash_rows(q, kt, bias, m_ref, alpha_ref, p_ref, rows):
    s = _mm(q, kt)
    if bias is not None:
        s = s + bias
    m_prev = m_ref[rows, :]
    m_next = jnp.maximum(m_prev, jnp.max(s, axis=1, keepdims=True))
    alpha_ref[rows, :] = jnp.exp(m_prev - m_next)
    m_ref[rows, :] = m_next
    p_ref[rows, :] = jnp.exp(s - _lane_tile(m_next, s.shape[1] // LANES)).astype(p_ref.dtype)


def _flash_accumulate(v_aug, alpha_ref, p_ref, acc_ref):
    alpha = _lane_tile(alpha_ref[...], acc_ref.shape[-1] // LANES)
    acc_ref[...] = alpha * acc_ref[...] + _mm(p_ref[...], v_aug)


def _skewed_chunks(n, score, fold, carry):
    def pair(j, carry):
        c = 2 * j
        fold(c - 1, 1)
        carry = score(c, 0, carry)
        fold(c, 0)
        return score(c + 1, 1, carry)

    def single(c, carry):
        fold(c - 1, 1)
        return score(c, 0, carry)

    carry = lax.fori_loop(0, n // 2, pair, carry)
    return lax.fori_loop(2 * (n // 2), n, single, carry)


def _fold_last(n, fold):
    for buf in range(2):
        pl.when((n - 1) % 2 == buf)(functools.partial(fold, n - 1, buf))


def _lane_tile(x, reps):
    return x if reps == 1 else jnp.concatenate([x] * reps, axis=1)


def _with_ones(v):
    return jnp.concatenate([v, jnp.ones_like(v)], axis=-1)


def _row_slices(n_rows):
    return [slice(r * Q_BLOCK, (r + 1) * Q_BLOCK) for r in range(n_rows // Q_BLOCK)]


def _dsa_kernel(qi_ref, wi_ref, kit_ref, qa_ref, kat_ref, va_ref, o_ref,
                skey_ref, wrep_ref, m_ref, alpha_ref, acc_ref, p_ref, *, topk, kc):
    qb = Q_BLOCK
    i = pl.program_id(1)
    q0 = i * qb
    n_chunks = (q0 + qb + kc - 1) // kc
    t_row = q0 + lax.broadcasted_iota(i32, (qb, kc), 0)
    col = lax.broadcasted_iota(i32, (qb, kc), 1)
    wi = wi_ref[...]
    qi = qi_ref[...]
    qis = [qi[:, h * IDX_DIM:(h + 1) * IDX_DIM] for h in range(IDX_HEADS)]
    for h in range(IDX_HEADS):
        wrep_ref[h] = jnp.broadcast_to(wi[:, h:h + 1], (qb, kc))

    def score_body(c, carry):
        off = pl.multiple_of(c * kc, kc)
        kt = kit_ref[:, pl.ds(off, kc)]
        s = jnp.zeros((qb, kc), f32)
        for h in range(IDX_HEADS):
            s = s + wrep_ref[h] * jnp.maximum(_mm(qis[h], kt), 0.0)
        s = jnp.where(col + off <= t_row, s, NEG)
        bits = pltpu.bitcast(s, i32)
        skey_ref[:, pl.ds(off, kc)] = jnp.where(bits < 0, bits ^ 0x7FFFFFFF, bits)
        return carry

    n_strips = (q0 + qb + COUNT_STRIP - 1) // COUNT_STRIP
    lax.fori_loop(0, n_strips * (COUNT_STRIP // kc), score_body, 0)

    def count_ge(cand):
        def body(c, acc):
            off = pl.multiple_of(c * COUNT_STRIP, COUNT_STRIP)
            for j in range(COUNT_STRIP // LANES):
                acc = acc + jnp.where(skey_ref[:, pl.ds(off + j * LANES, LANES)] >= cand, 1.0, 0.0)
            return acc
        acc = lax.fori_loop(0, n_strips, body, jnp.zeros((qb, LANES), f32))
        return jnp.sum(acc, axis=1, keepdims=True)

    thr = jnp.where(count_ge(jnp.zeros((qb, 1), i32)) >= topk, 0, INT_MIN).astype(i32)

    def bisect(b, thr):
        cand = thr + jnp.left_shift(jnp.int32(1), 30 - b)
        return jnp.where(count_ge(cand) >= topk, cand, thr)

    thr = lax.fori_loop(0, 31, bisect, thr)
    quota = topk - count_ge(thr + 1)

    before = (lax.broadcasted_iota(i32, (kc, kc), 0) < lax.broadcasted_iota(i32, (kc, kc), 1))
    before = jnp.where(before, 1.0, 0.0).astype(MXU_DTYPE)
    _flash_reset(m_ref, acc_ref)
    alpha_ref[...] = jnp.zeros(alpha_ref.shape, f32)
    p_ref[...] = jnp.zeros(p_ref.shape, p_ref.dtype)
    heads = _row_slices(A_HEADS * qb)

    def fold(c, buf):
        off = pl.multiple_of(jnp.maximum(c, 0) * kc, kc)
        _flash_accumulate(va_ref[pl.ds(off, kc), :], alpha_ref.at[buf], p_ref.at[buf], acc_ref)

    def score(c, buf, ties_seen):
        off = pl.multiple_of(c * kc, kc)
        key = skey_ref[:, pl.ds(off, kc)]
        eq = key == thr
        eqf = jnp.where(eq, 1.0, 0.0)
        rank = ties_seen + _mm(eqf.astype(MXU_DTYPE), before)
        sel = ((key > thr) | (eq & (rank < quota))) & (col + off <= t_row)
        bias = jnp.where(sel, 0.0, NEG)
        kt = kat_ref[:, pl.ds(off, kc)]
        for rows in heads:
            _flash_rows(qa_ref[rows, :], kt, bias, m_ref, alpha_ref.at[buf], p_ref.at[buf], rows)
        return ties_seen + jnp.sum(eqf, axis=1, keepdims=True)

    _skewed_chunks(n_chunks, score, fold, jnp.zeros((qb, 1), f32))
    _fold_last(n_chunks, fold)
    acc = acc_ref[...]
    o_ref[...] = acc[:, :HEAD_DIM] / acc[:, HEAD_DIM:]


def dsa_attention(qi, wi, kit, qa, kat, va, topk, kc=256):
    bsz, seq, _ = qi.shape
    nb = seq // Q_BLOCK
    rows = A_HEADS * Q_BLOCK
    kern = functools.partial(_dsa_kernel, topk=topk, kc=kc)
    return pl.pallas_call(
        kern,
        grid=(bsz, nb),
        in_specs=[pl.BlockSpec((None, Q_BLOCK, IDX_HEADS * IDX_DIM), lambda b, i: (b, i, 0)),
                  pl.BlockSpec((None, Q_BLOCK, IDX_HEADS), lambda b, i: (b, i, 0)),
                  pl.BlockSpec((None, IDX_DIM, seq), lambda b, i: (b, 0, 0)),
                  pl.BlockSpec((None, None, rows, HEAD_DIM), lambda b, i: (b, i, 0, 0)),
                  pl.BlockSpec((None, HEAD_DIM, seq), lambda b, i: (b, 0, 0)),
                  pl.BlockSpec((None, seq, 2 * HEAD_DIM), lambda b, i: (b, 0, 0))],
        out_specs=pl.BlockSpec((None, None, rows, HEAD_DIM), lambda b, i: (b, i, 0, 0)),
        out_shape=jax.ShapeDtypeStruct((bsz, nb, rows, HEAD_DIM), f32),
        scratch_shapes=[pltpu.VMEM((Q_BLOCK, seq), i32), pltpu.VMEM((IDX_HEADS, Q_BLOCK, kc), f32),
                        pltpu.VMEM((rows, LANES), f32), pltpu.VMEM((2, rows, LANES), f32),
                        pltpu.VMEM((rows, 2 * HEAD_DIM), f32), pltpu.VMEM((2, rows, kc), MXU_DTYPE)],
        compiler_params=_params("parallel", "arbitrary"),
        name="dsa_attention",
    )(qi, wi, kit, qa, kat, va)


def _compress_kernel(x_ref, pe_ref, w1_ref, w2_ref, o_ref):
    half = (CMP_LEN // 2) * HEAD_DIM
    x = x_ref[...]
    first = _mm((x + pe_ref[0:1, :]).astype(MXU_DTYPE), w1_ref[0:half, :])
    second = _mm((x + pe_ref[1:2, :]).astype(MXU_DTYPE), w1_ref[half:2 * half, :])
    pre = first + pltpu.roll(second, shift=x.shape[0] - 1, axis=0)
    hid = pre * jax.nn.sigmoid(pre)
    o_ref[...] = _mm(hid.astype(MXU_DTYPE), w2_ref[...])


def nsa_compress(chunks, pe2, w1, w2):
    bsz, groups, nck, width = chunks.shape
    return pl.pallas_call(
        _compress_kernel,
        grid=(bsz, groups),
        in_specs=[pl.BlockSpec((None, None, nck, width), lambda b, g: (b, g, 0, 0)),
                  pl.BlockSpec((2, width), lambda b, g: (0, 0)),
                  pl.BlockSpec((2 * width, CMP_HIDDEN), lambda b, g: (0, 0)),
                  pl.BlockSpec((CMP_HIDDEN, HEAD_DIM), lambda b, g: (0, 0))],
        out_specs=pl.BlockSpec((None, None, nck, HEAD_DIM), lambda b, g: (b, g, 0, 0)),
        out_shape=jax.ShapeDtypeStruct((bsz, groups, nck, HEAD_DIM), f32),
        compiler_params=_params("parallel", "parallel"),
        name="nsa_compress",
    )(chunks, pe2, w1, w2)


def _nsa_kernel(q_ref, graw_ref, kct_ref, vc_ref, kst_ref, vs_ref, kwt_ref, vw_ref, ovl_ref, o_ref,
                m_ref, alpha_ref, acc_ref, p_ref, pw_ref, *, n_slc, n_sel, kc):
    qb = Q_BLOCK
    i = pl.program_id(2)
    q0 = i * qb
    n_cmp = kct_ref.shape[1]
    heads = _row_slices(B_REP * qb)

    t_c = q0 + lax.broadcasted_iota(i32, (qb, n_cmp), 0)
    cmp_end = lax.broadcasted_iota(i32, (qb, n_cmp), 1) * CMP_STRIDE + (CMP_LEN - 1)
    vis = cmp_end <= t_c
    kct = kct_ref[...]
    p_sum = jnp.zeros((qb, n_cmp), f32)
    for rows in heads:
        lc = jnp.where(vis, _mm(q_ref[rows, :], kct), NEG)
        ec = jnp.where(vis, jnp.exp(lc - jnp.max(lc, axis=1, keepdims=True)), 0.0)
        den = jnp.sum(ec, axis=1, keepdims=True)
        p_c = ec / jnp.where(den > 0.0, den, 1.0)
        p_sum = p_sum + p_c
        acc_ref[rows, :HEAD_DIM] = _mm(p_c.astype(MXU_DTYPE), vc_ref[...])
    o_c = acc_ref[:, :HEAD_DIM]

    imp = jnp.dot(p_sum, ovl_ref[...], preferred_element_type=f32, precision=lax.Precision.HIGHEST).T
    t_q = q0 + lax.broadcasted_iota(i32, (LANES, qb), 1)
    blk = lax.broadcasted_iota(i32, (LANES, qb), 0)
    blk_t = t_q // SLC_LEN
    forced = (blk == 0) | (blk == blk_t) | (blk == blk_t - 1)
    imp = jnp.where(forced, FORCED_BOOST, imp)
    imp = jnp.where(blk * SLC_LEN <= t_q, imp, NEG)
    imp = jnp.where(blk < n_slc, imp, -jnp.inf)
    blk_f = blk.astype(f32)

    def pick(_, carry):
        imp, selm = carry
        best = jnp.max(imp, axis=0, keepdims=True)
        first = jnp.min(jnp.where(imp == best, blk_f, float(LANES)), axis=0, keepdims=True)
        hit = blk_f == first
        return jnp.where(hit, -jnp.inf, imp), jnp.where(hit, 1.0, selm)

    _, selm = lax.fori_loop(0, n_sel, pick, (imp, jnp.zeros((LANES, qb), f32)))
    selm = selm.T.astype(MXU_DTYPE)

    t_k = q0 + lax.broadcasted_iota(i32, (qb, kc), 0)
    col = lax.broadcasted_iota(i32, (qb, kc), 1)
    exp_row = lax.broadcasted_iota(i32, (LANES, kc), 0)
    exp_col = lax.broadcasted_iota(i32, (LANES, kc), 1)
    _flash_reset(m_ref, acc_ref)
    alpha_ref[...] = jnp.zeros(alpha_ref.shape, f32)
    p_ref[...] = jnp.zeros(p_ref.shape, p_ref.dtype)

    def fold(c, buf):
        off = pl.multiple_of(jnp.maximum(c, 0) * kc, kc)
        _flash_accumulate(vs_ref[pl.ds(off, kc), :], alpha_ref.at[buf], p_ref.at[buf], acc_ref)

    def score(c, buf, carry):
        off = pl.multiple_of(c * kc, kc)
        expand = jnp.where(exp_row == (exp_col + off) // SLC_LEN, 1.0, 0.0).astype(MXU_DTYPE)
        sel = (_mm(selm, expand) > 0.5) & (col + off <= t_k)
        bias = jnp.where(sel, 0.0, NEG)
        kt = kst_ref[:, pl.ds(off, kc)]
        for rows in heads:
            _flash_rows(q_ref[rows, :], kt, bias, m_ref, alpha_ref.at[buf], p_ref.at[buf], rows)
        return carry

    n_slc_chunks = (q0 + qb + kc - 1) // kc
    _skewed_chunks(n_slc_chunks, score, fold, 0)
    _fold_last(n_slc_chunks, fold)
    acc = acc_ref[...]
    o_s = acc[:, :HEAD_DIM] / acc[:, HEAD_DIM:]

    slab = WINDOW + qb
    w0 = pl.multiple_of(jnp.maximum(q0 - WINDOW, 0), qb)
    dist = (q0 + lax.broadcasted_iota(i32, (qb, slab), 0)) - (w0 + lax.broadcasted_iota(i32, (qb, slab), 1))
    bias = jnp.where((dist >= 0) & (dist < WINDOW), 0.0, NEG)
    kt = kwt_ref[:, pl.ds(w0, slab)]
    for rows in heads:
        s = _mm(q_ref[rows, :], kt) + bias
        pw_ref[rows, :] = jnp.exp(s - jnp.max(s, axis=1, keepdims=True)).astype(pw_ref.dtype)
    acc = _mm(pw_ref[...], vw_ref[pl.ds(w0, slab), :])
    o_w = acc[:, :HEAD_DIM] / acc[:, HEAD_DIM:]

    gates = jax.nn.sigmoid(graw_ref[...])
    o_ref[...] = gates[:, 0:1] * o_c + gates[:, 1:2] * o_s + gates[:, 2:3] * o_w


def nsa_attention(q, graw, kct, vc, kst, vs, kwt, vw, ovl, n_slc, n_sel, kc=256):
    bsz, groups, nb, rows, _ = q.shape
    seq = kst.shape[-1]
    n_cmp = kct.shape[-1]
    kern = functools.partial(_nsa_kernel, n_slc=n_slc, n_sel=n_sel, kc=kc)
    per_bg = lambda b, g, i: (b, g, 0, 0)
    return pl.pallas_call(
        kern,
        grid=(bsz, groups, nb),
        in_specs=[pl.BlockSpec((None, None, None, rows, HEAD_DIM), lambda b, g, i: (b, g, i, 0, 0)),
                  pl.BlockSpec((None, None, None, rows, 3), lambda b, g, i: (b, g, i, 0, 0)),
                  pl.BlockSpec((None, None, HEAD_DIM, n_cmp), per_bg),
                  pl.BlockSpec((None, None, n_cmp, HEAD_DIM), per_bg),
                  pl.BlockSpec((None, None, HEAD_DIM, seq), per_bg),
                  pl.BlockSpec((None, None, seq, 2 * HEAD_DIM), per_bg),
                  pl.BlockSpec((None, None, HEAD_DIM, seq), per_bg),
                  pl.BlockSpec((None, None, seq, 2 * HEAD_DIM), per_bg),
                  pl.BlockSpec((n_cmp, LANES), lambda b, g, i: (0, 0))],
        out_specs=pl.BlockSpec((None, None, None, rows, HEAD_DIM), lambda b, g, i: (b, g, i, 0, 0)),
        out_shape=jax.ShapeDtypeStruct((bsz, groups, nb, rows, HEAD_DIM), f32),
        scratch_shapes=[pltpu.VMEM((rows, LANES), f32), pltpu.VMEM((2, rows, LANES), f32),
                        pltpu.VMEM((rows, 2 * HEAD_DIM), f32), pltpu.VMEM((2, rows, kc), MXU_DTYPE),
                        pltpu.VMEM((rows, WINDOW + Q_BLOCK), MXU_DTYPE)],
        compiler_params=_params("parallel", "parallel", "arbitrary"),
        name="nsa_attention",
    )(q, graw, kct, vc, kst, vs, kwt, vw, ovl)


def _diff_kernel(lam_ref, q_ref, kt_ref, v_ref, g_ref, o_ref, m_ref, alpha_ref, acc_ref, p_ref,
                 *, tq, kc, out_scale):
    i = pl.program_id(2)
    q0 = i * tq
    dv = v_ref.shape[-1] // 2
    groups = _row_slices(tq)
    col = lax.broadcasted_iota(i32, (Q_BLOCK, kc), 1)
    t_row = [q0 + r * Q_BLOCK + lax.broadcasted_iota(i32, (Q_BLOCK, kc), 0) for r in range(len(groups))]
    for half in range(2):
        _flash_reset(m_ref.at[half], acc_ref.at[half])
    alpha_ref[...] = jnp.zeros(alpha_ref.shape, f32)
    p_ref[...] = jnp.zeros(p_ref.shape, p_ref.dtype)

    def fold(c, buf):
        off = pl.multiple_of(jnp.maximum(c, 0) * kc, kc)
        v = v_ref[pl.ds(off, kc), :]
        for half in range(2):
            _flash_accumulate(v, alpha_ref.at[buf, half], p_ref.at[buf, half], acc_ref.at[half])

    def score(c, buf, masked):
        off = pl.multiple_of(c * kc, kc)
        for half in range(2):
            kt = kt_ref[half, :, pl.ds(off, kc)]
            for r, rows in enumerate(groups):
                bias = jnp.where(col + off <= t_row[r], 0.0, NEG) if masked else None
                _flash_rows(q_ref[half, rows, :], kt, bias, m_ref.at[half], alpha_ref.at[buf, half],
                            p_ref.at[buf, half], rows)

    def score_full(c, buf, carry):
        score(c, buf, False)
        return carry

    n_full = q0 // kc
    _skewed_chunks(n_full, score_full, fold, 0)

    def finish(buf):
        fold(n_full - 1, 1 - buf)
        score(n_full, buf, True)
        fold(n_full, buf)

    for buf in range(2):
        pl.when(n_full % 2 == buf)(functools.partial(finish, buf))
    a1 = acc_ref[0]
    a2 = acc_ref[1]
    o = a1[:, :dv] / a1[:, dv:] - lam_ref[0] * (a2[:, :dv] / a2[:, dv:])
    o = o * lax.rsqrt(jnp.mean(o * o, axis=-1, keepdims=True) + LN_EPS) * g_ref[...]
    o_ref[...] = o * out_scale


def diff_attention(lam, q, kt, v, subln_g, out_scale, tq=256, kc=512):
    bsz, heads, _, seq, _ = q.shape
    dv = v.shape[-1] // 2
    tq = min(tq, seq)
    kc = min(kc, seq)
    assert kc % tq == 0 and seq % kc == 0
    kern = functools.partial(_diff_kernel, tq=tq, kc=kc, out_scale=out_scale)
    return pl.pallas_call(
        kern,
        grid=(bsz, heads, seq // tq),
        in_specs=[pl.BlockSpec(memory_space=pltpu.SMEM),
                  pl.BlockSpec((None, None, 2, tq, HEAD_DIM), lambda b, h, i: (b, h, 0, i, 0)),
                  pl.BlockSpec((None, None, 2, HEAD_DIM, seq), lambda b, h, i: (b, h, 0, 0, 0)),
                  pl.BlockSpec((None, None, seq, 2 * dv), lambda b, h, i: (b, h, 0, 0)),
                  pl.BlockSpec((1, dv), lambda b, h, i: (0, 0))],
        out_specs=pl.BlockSpec((None, tq, dv), lambda b, h, i: (b, i, h)),
        out_shape=jax.ShapeDtypeStruct((bsz, seq, heads * dv), f32),
        scratch_shapes=[pltpu.VMEM((2, tq, LANES), f32), pltpu.VMEM((2, 2, tq, LANES), f32),
                        pltpu.VMEM((2, tq, 2 * dv), f32), pltpu.VMEM((2, 2, tq, kc), MXU_DTYPE)],
        compiler_params=_params("parallel", "parallel", "arbitrary"),
        name="diff_attention",
    )(lam, q, kt, v, subln_g.reshape(1, dv))


def _split(h, layout):
    offs = np.cumsum([w for _, w in layout])[:-1].tolist()
    return jnp.split(h, offs, axis=-1)


def _rope_tables(positions):
    inv_freq = ROPE_THETA ** (-jnp.arange(0, ROT_DIM, 2, dtype=f32) / ROT_DIM)
    ang = positions.astype(f32)[..., None] * inv_freq
    return jnp.cos(ang), jnp.sin(ang)


def _apply_rope(x, cos, sin):
    shape = cos.shape[:2] + (1,) * (x.ndim - 3) + cos.shape[-1:]
    c = cos.reshape(shape)
    s = sin.reshape(shape)
    half = ROT_DIM // 2
    x1, x2 = x[..., :half], x[..., half:ROT_DIM]
    return jnp.concatenate([x1 * c - x2 * s, x2 * c + x1 * s, x[..., ROT_DIM:]], axis=-1)


def _pad_cols(w, mult):
    n = w.shape[-1]
    pad = (-n) % mult
    return jnp.pad(w, ((0, 0), (0, pad))) if pad else w


def _overlap_matrix(n_cmp_rows, n_slc):
    c_start = np.arange(n_cmp_rows) * CMP_STRIDE
    s_start = np.arange(LANES) * SLC_LEN
    ovl = (c_start[:, None] < s_start[None, :] + SLC_LEN) & (c_start[:, None] + CMP_LEN > s_start[None, :])
    ovl = ovl & (np.arange(LANES)[None, :] < n_slc)
    return jnp.asarray(ovl.astype(np.float32))


def _ab_mixer(x2, bsz, seq, positions, cos, sin, w_in, pe_k, pe_v, ck1, ck2, cv1, cv2):
    nb = seq // Q_BLOCK
    width = sum(w for _, w in AB_LAYOUT)
    h = project(x2, _pad_cols(w_in, 2 * LANES).astype(MXU_DTYPE))[:, :width].reshape(bsz, seq, width)
    (q_a, k_a, v_a, q_idx, k_idx, w_idx, q_b, k_cmp, v_cmp, k_slc, v_slc, k_win, v_win, g_b) = _split(h, AB_LAYOUT)

    q_a = _apply_rope(q_a.reshape(bsz, seq, A_HEADS, HEAD_DIM), cos, sin) * QK_SCALE
    q_a = q_a.reshape(bsz, nb, Q_BLOCK, A_HEADS, HEAD_DIM).transpose(0, 1, 3, 2, 4)
    q_a = q_a.reshape(bsz, nb, A_HEADS * Q_BLOCK, HEAD_DIM).astype(MXU_DTYPE)
    k_a = _apply_rope(k_a, cos, sin).transpose(0, 2, 1).astype(MXU_DTYPE)
    q_idx = _apply_rope(q_idx.reshape(bsz, seq, IDX_HEADS, IDX_DIM), cos, sin)
    q_idx = q_idx.reshape(bsz, seq, IDX_HEADS * IDX_DIM).astype(MXU_DTYPE)
    k_idx = _apply_rope(k_idx, cos, sin).transpose(0, 2, 1).astype(MXU_DTYPE)
    w_idx = w_idx * (IDX_HEADS * IDX_DIM) ** -0.5
    o_a = dsa_attention(q_idx, w_idx, k_idx, q_a, k_a, _with_ones(v_a.astype(MXU_DTYPE)),
                        topk=min(DSA_TOPK, seq // 4))
    o_a = o_a.reshape(bsz, nb, A_HEADS, Q_BLOCK, HEAD_DIM).transpose(0, 1, 3, 2, 4)
    o_a = o_a.reshape(bsz * seq, A_HEADS * HEAD_DIM)

    groups = B_KV_GROUPS
    n_rows = seq // CMP_STRIDE
    n_cmp = (seq - CMP_LEN) // CMP_STRIDE + 1
    assert n_cmp == n_rows - 1
    n_slc = seq // SLC_LEN
    assert n_slc <= LANES

    def chunked(kv):
        kv = kv.reshape(bsz, n_rows, CMP_STRIDE, groups, HEAD_DIM).transpose(0, 3, 1, 2, 4)
        return kv.reshape(bsz, groups, n_rows, CMP_STRIDE * HEAD_DIM)

    def pe_halves(pe):
        return pe.reshape(2, CMP_STRIDE * HEAD_DIM)

    k_c = nsa_compress(chunked(k_cmp), pe_halves(pe_k), ck1.astype(MXU_DTYPE), ck2.astype(MXU_DTYPE))
    v_c = nsa_compress(chunked(v_cmp), pe_halves(pe_v), cv1.astype(MXU_DTYPE), cv2.astype(MXU_DTYPE))
    cmp_end = jnp.minimum(jnp.arange(n_rows) * CMP_STRIDE + CMP_LEN - 1, seq - 1)
    cos_c, sin_c = _rope_tables(positions[:, cmp_end])
    k_c = _apply_rope(k_c.transpose(0, 2, 1, 3), cos_c, sin_c)
    k_c = k_c.transpose(0, 2, 3, 1).astype(MXU_DTYPE)
    v_c = v_c.astype(MXU_DTYPE)

    def keys_t(k, rope):
        k = k.reshape(bsz, seq, groups, HEAD_DIM)
        if rope:
            k = _apply_rope(k, cos, sin)
        return k.transpose(0, 2, 3, 1).astype(MXU_DTYPE)

    def vals(v):
        return _with_ones(v.reshape(bsz, seq, groups, HEAD_DIM).transpose(0, 2, 1, 3).astype(MXU_DTYPE))

    def head_rows(a, last):
        a = a.reshape(bsz, nb, Q_BLOCK, groups, B_REP, last).transpose(0, 3, 1, 4, 2, 5)
        return a.reshape(bsz, groups, nb, B_REP * Q_BLOCK, last)

    q_b = _apply_rope(q_b.reshape(bsz, seq, B_HEADS, HEAD_DIM), cos, sin) * QK_SCALE
    q_b = head_rows(q_b.reshape(bsz, seq, B_HEADS * HEAD_DIM), HEAD_DIM).astype(MXU_DTYPE)
    o_b = nsa_attention(q_b, head_rows(g_b, 3), k_c, v_c, keys_t(k_slc, True), vals(v_slc),
                        keys_t(k_win, True), vals(v_win), _overlap_matrix(n_rows, n_slc),
                        n_slc=n_slc, n_sel=min(SLC_TOPN, n_slc))
    o_b = o_b.reshape(bsz, groups, nb, B_REP, Q_BLOCK, HEAD_DIM).transpose(0, 2, 4, 1, 3, 5)
    o_b = o_b.reshape(bsz * seq, B_HEADS * HEAD_DIM)
    return jnp.concatenate([o_a, o_b], axis=-1)


def _diff_mixer(x2, bsz, seq, cos, sin, w_in, lq1, lk1, lq2, lk2, subln_g, lam_init):
    h = project(x2, w_in.astype(MXU_DTYPE)).reshape(bsz, seq, -1)
    q, k, v = _split(h, C_LAYOUT)
    q = _apply_rope(q.reshape(bsz, seq, C_HEADS, 2, HEAD_DIM), cos, sin) * QK_SCALE
    k = _apply_rope(k.reshape(bsz, seq, C_HEADS, 2, HEAD_DIM), cos, sin)
    q = q.transpose(0, 2, 3, 1, 4).astype(MXU_DTYPE)
    kt = k.transpose(0, 2, 3, 4, 1).astype(MXU_DTYPE)
    v = v.reshape(bsz, seq, C_HEADS, 2 * HEAD_DIM).transpose(0, 2, 1, 3).astype(MXU_DTYPE)
    lam = (jnp.exp(jnp.sum(lq1 * lk1)) - jnp.exp(jnp.sum(lq2 * lk2)) + lam_init).reshape(1).astype(f32)
    o = diff_attention(lam, q, kt, _with_ones(v), subln_g, 1.0 - lam_init)
    return o.reshape(bsz * seq, C_HEADS * 2 * HEAD_DIM)


def kernel(x, positions, ab_w_in, cmp_pe_k, cmp_pe_v, cmp_k_w1, cmp_k_w2, cmp_v_w1, cmp_v_w2, ab_w_out, ln_ab_g, ln_ab_b, ffn_w1, ffn_w3, ffn_w2, ln_ffn_g, ln_ffn_b, c_w_in, lambda_q1, lambda_k1, lambda_q2, lambda_k2, c_subln_g, c_w_out, ln_c_g, ln_c_b, router_w, moe_w1, moe_w3, moe_w2, ln_moe_g, ln_moe_b):
    bsz, seq, d = x.shape
    assert seq % COUNT_STRIP == 0 and seq >= WINDOW + Q_BLOCK and d == D_MODEL
    cos, sin = _rope_tables(positions)
    x2 = x.reshape(bsz * seq, d)
    for layer in range(DEPTH):
        i = layer // 2
        if layer % 2 == 0:
            o = _ab_mixer(x2, bsz, seq, positions, cos, sin, ab_w_in[i], cmp_pe_k[i], cmp_pe_v[i],
                          cmp_k_w1[i], cmp_k_w2[i], cmp_v_w1[i], cmp_v_w2[i])
            x2 = project_residual_ln(o, ab_w_out[i].astype(MXU_DTYPE), x2, ln_ab_g[i], ln_ab_b[i])
            x2 = ffn_residual_ln(x2, ffn_w1[i].astype(MXU_DTYPE), ffn_w3[i].astype(MXU_DTYPE),
                                 ffn_w2[i].astype(MXU_DTYPE), ln_ffn_g[i], ln_ffn_b[i], tm=512, tf=1408)
        else:
            lam_init = 0.8 - 0.6 * math.exp(-0.3 * layer)
            o = _diff_mixer(x2, bsz, seq, cos, sin, c_w_in[i], lambda_q1[i], lambda_k1[i], lambda_q2[i],
                            lambda_k2[i], c_subln_g[i], lam_init)
            x2 = project_residual_ln(o, c_w_out[i].astype(MXU_DTYPE), x2, ln_c_g[i], ln_c_b[i])
            routes = route_top2(x2, router_w[i])
            x2 = moe_residual_ln(x2, routes, moe_w1[i].astype(MXU_DTYPE), moe_w3[i].astype(MXU_DTYPE),
                                 moe_w2[i].astype(MXU_DTYPE), ln_moe_g[i], ln_moe_b[i], tm=1024, tf=896)
    return x2.reshape(bsz, seq, d)
```

```python
import functools
import math

import numpy as np
import jax
import jax.numpy as jnp
from jax import lax
from jax.experimental import pallas as pl
from jax.experimental.pallas import tpu as pltpu

f32 = jnp.float32
i32 = jnp.int32
MXU_DTYPE = jnp.bfloat16
VMEM_LIMIT_BYTES = 56 * 1024 * 1024
LANES = 128

D_MODEL = 1024
DEPTH = 2
HEAD_DIM = 64
ROT_DIM = HEAD_DIM // 4
ROPE_THETA = 500000.0
Q_BLOCK = 128
NEG = -1e30
LN_EPS = 1e-5
A_HEADS = 8
IDX_HEADS = 4
IDX_DIM = 64
DSA_TOPK = 256
B_HEADS = 8
B_KV_GROUPS = 2
B_REP = B_HEADS // B_KV_GROUPS
CMP_LEN = 32
CMP_STRIDE = 16
CMP_HIDDEN = 128
SLC_LEN = 64
SLC_TOPN = 16
WINDOW = 512
FORCED_BOOST = 1e6
C_HEADS = 8
N_EXPERTS = 8
TOP_K = 2
DEEPNORM_ALPHA = (2 * DEPTH) ** 0.25
QK_SCALE = HEAD_DIM ** -0.5
KEY_NEG = int(np.float32(NEG).view(np.int32)) ^ 0x7FFFFFFF
INTERP_STEPS = 24
MAX_NARROW_STEPS = INTERP_STEPS + 34
COUNT_STRIP = 512
ROW_UNROLL = 4

AB_LAYOUT = (
    ("q_a", A_HEADS * HEAD_DIM), ("k_a", HEAD_DIM), ("v_a", HEAD_DIM),
    ("q_idx", IDX_HEADS * IDX_DIM), ("k_idx", IDX_DIM), ("w_idx", IDX_HEADS),
    ("q_b", B_HEADS * HEAD_DIM),
    ("k_cmp", B_KV_GROUPS * HEAD_DIM), ("v_cmp", B_KV_GROUPS * HEAD_DIM),
    ("k_slc", B_KV_GROUPS * HEAD_DIM), ("v_slc", B_KV_GROUPS * HEAD_DIM),
    ("k_win", B_KV_GROUPS * HEAD_DIM), ("v_win", B_KV_GROUPS * HEAD_DIM),
    ("gate_b", 3 * B_HEADS),
)
C_LAYOUT = (("q_c", C_HEADS * 2 * HEAD_DIM), ("k_c", C_HEADS * 2 * HEAD_DIM), ("v_c", C_HEADS * 2 * HEAD_DIM))


def _params(*sem):
    return pltpu.CompilerParams(dimension_semantics=sem, vmem_limit_bytes=VMEM_LIMIT_BYTES)


def _mm(a, b):
    return jnp.dot(a, b, preferred_element_type=f32)


def _layer_norm_rows(y, g, b):
    mu = jnp.mean(y, axis=-1, keepdims=True)
    yc = y - mu
    var = jnp.mean(yc * yc, axis=-1, keepdims=True)
    return yc * lax.rsqrt(var + LN_EPS) * g + b


def _proj_kernel(a_ref, w_ref, o_ref):
    o_ref[...] = _mm(a_ref[...].astype(MXU_DTYPE), w_ref[...]).astype(o_ref.dtype)


def project(a, w, tm=512):
    m, k = a.shape
    n = w.shape[1]
    return pl.pallas_call(
        _proj_kernel,
        grid=(m // tm,),
        in_specs=[pl.BlockSpec((tm, k), lambda i: (i, 0)), pl.BlockSpec((k, n), lambda i: (0, 0))],
        out_specs=pl.BlockSpec((tm, n), lambda i: (i, 0)),
        out_shape=jax.ShapeDtypeStruct((m, n), f32),
        compiler_params=_params("parallel"),
        name="project",
    )(a, w)


def _proj_ln_kernel(a_ref, w_ref, res_ref, g_ref, b_ref, o_ref):
    h = _mm(a_ref[...].astype(MXU_DTYPE), w_ref[...])
    o_ref[...] = _layer_norm_rows(DEEPNORM_ALPHA * res_ref[...] + h, g_ref[...], b_ref[...])


def project_residual_ln(a, w, res, g, b, tm=512):
    m, k = a.shape
    n = w.shape[1]
    return pl.pallas_call(
        _proj_ln_kernel,
        grid=(m // tm,),
        in_specs=[pl.BlockSpec((tm, k), lambda i: (i, 0)), pl.BlockSpec((k, n), lambda i: (0, 0)),
                  pl.BlockSpec((tm, n), lambda i: (i, 0)),
                  pl.BlockSpec((1, n), lambda i: (0, 0)), pl.BlockSpec((1, n), lambda i: (0, 0))],
        out_specs=pl.BlockSpec((tm, n), lambda i: (i, 0)),
        out_shape=jax.ShapeDtypeStruct((m, n), f32),
        compiler_params=_params("parallel"),
        name="project_residual_ln",
    )(a, w, res, g.reshape(1, n), b.reshape(1, n))


def _swiglu_tile(xb, w1_ref, w3_ref, w2_ref):
    a = _mm(xb, w1_ref[...])
    h = (a * jax.nn.sigmoid(a)) * _mm(xb, w3_ref[...])
    return _mm(h.astype(MXU_DTYPE), w2_ref[...])


def _ffn_ln_kernel(x_ref, w1_ref, w3_ref, w2_ref, g_ref, b_ref, o_ref, xb_ref, acc_ref):
    f = pl.program_id(1)

    @pl.when(f == 0)
    def _():
        xb_ref[...] = x_ref[...].astype(MXU_DTYPE)
        acc_ref[...] = jnp.zeros_like(acc_ref)

    acc_ref[...] += _swiglu_tile(xb_ref[...], w1_ref, w3_ref, w2_ref)

    @pl.when(f == pl.num_programs(1) - 1)
    def _():
        o_ref[...] = _layer_norm_rows(DEEPNORM_ALPHA * x_ref[...] + acc_ref[...], g_ref[...], b_ref[...])


def ffn_residual_ln(x, w1, w3, w2, g, b, tm, tf):
    m, d = x.shape
    ff = w1.shape[1]
    return pl.pallas_call(
        _ffn_ln_kernel,
        grid=(m // tm, ff // tf),
        in_specs=[pl.BlockSpec((tm, d), lambda i, f: (i, 0)),
                  pl.BlockSpec((d, tf), lambda i, f: (0, f)),
                  pl.BlockSpec((d, tf), lambda i, f: (0, f)),
                  pl.BlockSpec((tf, d), lambda i, f: (f, 0)),
                  pl.BlockSpec((1, d), lambda i, f: (0, 0)), pl.BlockSpec((1, d), lambda i, f: (0, 0))],
        out_specs=pl.BlockSpec((tm, d), lambda i, f: (i, 0)),
        out_shape=jax.ShapeDtypeStruct((m, d), f32),
        scratch_shapes=[pltpu.VMEM((tm, d), MXU_DTYPE), pltpu.VMEM((tm, d), f32)],
        compiler_params=_params("parallel", "arbitrary"),
        name="ffn_residual_ln",
    )(x, w1, w3, w2, g.reshape(1, d), b.reshape(1, d))


ROUTE_IDS = N_EXPERTS
ROUTE_GATES = N_EXPERTS + 2


def _router_kernel(x_ref, w_ref, o_ref, *, n_experts):
    logits = jnp.dot(x_ref[...], w_ref[...], preferred_element_type=f32, precision=lax.Precision.HIGHEST)
    lane = lax.broadcasted_iota(i32, logits.shape, 1).astype(f32)
    logits = jnp.where(lane < n_experts, logits, -jnp.inf)
    v1 = jnp.max(logits, axis=1, keepdims=True)
    i1 = jnp.min(jnp.where(logits == v1, lane, float(LANES)), axis=1, keepdims=True)
    rest = jnp.where(lane == i1, -jnp.inf, logits)
    v2 = jnp.max(rest, axis=1, keepdims=True)
    i2 = jnp.min(jnp.where(rest == v2, lane, float(LANES)), axis=1, keepdims=True)
    e2 = jnp.exp(v2 - v1)
    g1 = 1.0 / (1.0 + e2)
    g2 = e2 / (1.0 + e2)
    out = jnp.where(lane == ROUTE_IDS, i1, 0.0) + jnp.where(lane == ROUTE_IDS + 1, i2, 0.0)
    out = out + jnp.where(lane == ROUTE_GATES, g1, 0.0) + jnp.where(lane == ROUTE_GATES + 1, g2, 0.0)
    o_ref[...] = out


def route_top2(x, router_w, tm=512):
    m, d = x.shape
    n_experts = router_w.shape[1]
    w = jnp.zeros((d, LANES), f32).at[:, :n_experts].set(router_w)
    return pl.pallas_call(
        functools.partial(_router_kernel, n_experts=n_experts),
        grid=(m // tm,),
        in_specs=[pl.BlockSpec((tm, d), lambda i: (i, 0)), pl.BlockSpec((d, LANES), lambda i: (0, 0))],
        out_specs=pl.BlockSpec((tm, LANES), lambda i: (i, 0)),
        out_shape=jax.ShapeDtypeStruct((m, LANES), f32),
        compiler_params=_params("parallel"),
        name="route_top2",
    )(x, w)


def _moe_ln_kernel(tok_ref, gs_ref, off_ref, x_ref, w1_ref, w3_ref, w2_ref, g_ref, b_ref, o_ref,
                   xg_ref, xb_ref, y_ref, *, rt):
    c = pl.program_id(0)
    e = pl.program_id(1)
    f = pl.program_id(2)
    last_f = pl.num_programs(2) - 1
    start = off_ref[0, e]
    count = off_ref[0, e + 1] - start
    n_tiles = (count + rt - 1) // rt

    @pl.when((c == 0) & (e == 0) & (f == 0))
    def _():
        xg_ref[...] = jnp.zeros_like(xg_ref)

    @pl.when((e == 0) & (f == 0))
    def _():
        o_ref[...] = jnp.zeros_like(o_ref)

    def row_loop(body):
        def group(j, carry):
            for u in range(ROW_UNROLL):
                body(j * ROW_UNROLL + u)
            return carry

        def single(r, carry):
            body(r)
            return carry
        lax.fori_loop(0, count // ROW_UNROLL, group, 0)
        lax.fori_loop((count // ROW_UNROLL) * ROW_UNROLL, count, single, 0)

    @pl.when(f == 0)
    def _():
        def gather(r):
            t = tok_ref[0, start + r]
            xg_ref[pl.ds(r, 1), :] = x_ref[pl.ds(t, 1), :]
        row_loop(gather)

        def cast(j, carry):
            rows = pl.ds(pl.multiple_of(j * rt, rt), rt)
            xb_ref[rows, :] = xg_ref[rows, :].astype(MXU_DTYPE)
            return carry
        lax.fori_loop(0, n_tiles, cast, 0)

    def tile(j, carry):
        rows = pl.ds(pl.multiple_of(j * rt, rt), rt)
        y = _swiglu_tile(xb_ref[rows, :], w1_ref, w3_ref, w2_ref)

        @pl.when(f == 0)
        def _():
            y_ref[rows, :] = y

        @pl.when(f != 0)
        def _():
            y_ref[rows, :] += y
        return carry
    lax.fori_loop(0, n_tiles, tile, 0)

    @pl.when(f == last_f)
    def _():
        def scatter(r):
            t = tok_ref[0, start + r]
            o_ref[pl.ds(t, 1), :] += gs_ref[0, start + r] * y_ref[pl.ds(r, 1), :]
        row_loop(scatter)

    @pl.when((e == pl.num_programs(1) - 1) & (f == last_f))
    def _():
        o_ref[...] = _layer_norm_rows(DEEPNORM_ALPHA * x_ref[...] + o_ref[...], g_ref[...], b_ref[...])


def moe_residual_ln(x, routes, w1, w3, w2, g, b, tm, tf, rt=128):
    m, d = x.shape
    n_experts, _, ff = w1.shape
    n_chunks = m // tm
    ids = routes[:, ROUTE_IDS:ROUTE_IDS + TOP_K].astype(i32).reshape(n_chunks, tm * TOP_K)
    gts = routes[:, ROUTE_GATES:ROUTE_GATES + TOP_K].reshape(n_chunks, tm * TOP_K)
    order = jnp.argsort(ids, axis=1, stable=True).astype(i32)
    tok = order // TOP_K
    gs = jnp.take_along_axis(gts, order, axis=1)
    counts = jnp.sum(ids[:, :, None] == jnp.arange(n_experts, dtype=i32)[None, None, :], axis=1, dtype=i32)
    offs = jnp.concatenate([jnp.zeros((n_chunks, 1), i32), jnp.cumsum(counts, axis=1, dtype=i32)], axis=1)
    smem = lambda width: pl.BlockSpec((None, 1, width), lambda c, e, f: (c, 0, 0), memory_space=pltpu.SMEM)
    return pl.pallas_call(
        functools.partial(_moe_ln_kernel, rt=rt),
        grid=(n_chunks, n_experts, ff // tf),
        in_specs=[smem(tm * TOP_K), smem(tm * TOP_K), smem(n_experts + 1),
                  pl.BlockSpec((tm, d), lambda c, e, f: (c, 0), pipeline_mode=pl.Buffered(1)),
                  pl.BlockSpec((None, d, tf), lambda c, e, f: (e, 0, f)),
                  pl.BlockSpec((None, d, tf), lambda c, e, f: (e, 0, f)),
                  pl.BlockSpec((None, tf, d), lambda c, e, f: (e, f, 0)),
                  pl.BlockSpec((1, d), lambda c, e, f: (0, 0)), pl.BlockSpec((1, d), lambda c, e, f: (0, 0))],
        out_specs=pl.BlockSpec((tm, d), lambda c, e, f: (c, 0), pipeline_mode=pl.Buffered(1)),
        out_shape=jax.ShapeDtypeStruct((m, d), f32),
        scratch_shapes=[pltpu.VMEM((tm, d), f32), pltpu.VMEM((tm, d), MXU_DTYPE), pltpu.VMEM((tm, d), f32)],
        compiler_params=_params("arbitrary", "arbitrary", "arbitrary"),
        name="moe_residual_ln",
    )(tok[:, None, :], gs[:, None, :], offs[:, None, :], x, w1, w3, w2, g.reshape(1, d), b.reshape(1, d))


def _flash_reset(m_ref, acc_ref):
    m_ref[...] = jnp.full(m_ref.shape, NEG, f32)
    acc_ref[...] = jnp.zeros(acc_ref.shape, f32)


def _flash_rows(q, kt, bias, m_ref, alpha_ref, p_ref, rows):
    s = _mm(q, kt)
    if bias is not None:
        s = s + bias
    m_prev = m_ref[rows, :]
    m_next = jnp.maximum(m_prev, jnp.max(s, axis=1, keepdims=True))
    alpha_ref[rows, :] = jnp.exp(m_prev - m_next)
    m_ref[rows, :] = m_next
    p_ref[rows, :] = jnp.exp(s - _lane_tile(m_next, s.shape[1] // LANES)).astype(p_ref.dtype)


def _flash_accumulate(v_aug, alpha_ref, p_ref, acc_ref):
    alpha = _lane_tile(alpha_ref[...], acc_ref.shape[-1] // LANES)
    acc_ref[...] = alpha * acc_ref[...] + _mm(p_ref[...], v_aug)


def _skewed_chunks(n, score, fold, carry):
    def pair(j, carry):
        c = 2 * j
        fold(c - 1, 1)
        carry = score(c, 0, carry)
        fold(c, 0)
        return score(c + 1, 1, carry)

    def single(c, carry):
        fold(c - 1, 1)
        return score(c, 0, carry)

    carry = lax.fori_loop(0, n // 2, pair, carry)
    return lax.fori_loop(2 * (n // 2), n, single, carry)


def _fold_last(n, fold):
    for buf in range(2):
        pl.when((n - 1) % 2 == buf)(functools.partial(fold, n - 1, buf))


def _lane_tile(x, reps):
    return x if reps == 1 else jnp.concatenate([x] * reps, axis=1)


def _with_ones(v):
    return jnp.concatenate([v, jnp.ones_like(v)], axis=-1)


def _row_slices(n_rows):
    return [slice(r * Q_BLOCK, (r + 1) * Q_BLOCK) for r in range(n_rows // Q_BLOCK)]


def _dsa_kernel(qi_ref, wi_ref, kit_ref, qa_ref, kat_ref, va_ref, o_ref,
                skey_ref, wrep_ref, m_ref, alpha_ref, acc_ref, p_ref, *, topk, kc):
    qb = Q_BLOCK
    i = pl.program_id(1)
    q0 = i * qb
    n_chunks = (q0 + qb + kc - 1) // kc
    t_row = q0 + lax.broadcasted_iota(i32, (qb, kc), 0)
    col = lax.broadcasted_iota(i32, (qb, kc), 1)
    wi = wi_ref[...]
    qi = qi_ref[...]
    qis = [qi[:, h * IDX_DIM:(h + 1) * IDX_DIM] for h in range(IDX_HEADS)]
    for h in range(IDX_HEADS):
        wrep_ref[h] = jnp.broadcast_to(wi[:, h:h + 1], (qb, kc))

    def to_key(x):
        bits = pltpu.bitcast(x, i32)
        return jnp.where(bits < 0, bits ^ 0x7FFFFFFF, bits)

    def to_float(k):
        return pltpu.bitcast(jnp.where(k < 0, k ^ 0x7FFFFFFF, k), f32)

    def score_body(c, carry):
        s_min, s_max = carry
        off = pl.multiple_of(c * kc, kc)
        kt = kit_ref[:, pl.ds(off, kc)]
        s = jnp.zeros((qb, kc), f32)
        for h in range(IDX_HEADS):
            s = s + wrep_ref[h] * jnp.maximum(_mm(qis[h], kt), 0.0)
        causal = col + off <= t_row
        s_min = jnp.minimum(s_min, jnp.where(causal, s, jnp.inf))
        s = jnp.where(causal, s, NEG)
        skey_ref[:, pl.ds(off, kc)] = to_key(s)
        return s_min, jnp.maximum(s_max, s)

    n_strips = (q0 + qb + COUNT_STRIP - 1) // COUNT_STRIP
    s_min, s_max = lax.fori_loop(0, n_strips * (COUNT_STRIP // kc), score_body,
                                 (jnp.full((qb, kc), jnp.inf, f32), jnp.full((qb, kc), -jnp.inf, f32)))
    key_min = to_key(jnp.min(s_min, axis=1, keepdims=True))
    key_max = to_key(jnp.max(s_max, axis=1, keepdims=True))

    def count_ge(cand):
        def body(c, acc):
            off = pl.multiple_of(c * COUNT_STRIP, COUNT_STRIP)
            for j in range(COUNT_STRIP // LANES):
                acc = acc + jnp.where(skey_ref[:, pl.ds(off + j * LANES, LANES)] >= cand, 1.0, 0.0)
            return acc
        acc = lax.fori_loop(0, n_strips, body, jnp.zeros((qb, LANES), f32))
        return jnp.sum(acc, axis=1, keepdims=True)

    kf = float(topk)
    log_k = float(np.log(kf))

    def gap(c):
        return jnp.log(c + 0.5) - log_k

    n_causal = (q0 + 1 + lax.broadcasted_iota(i32, (qb, 1), 0)).astype(f32)
    c_zero = count_ge(jnp.zeros((qb, 1), i32))
    c_pos = count_ge(jnp.ones((qb, 1), i32))
    few = n_causal < kf
    pos = c_pos >= kf
    zero = (c_zero >= kf) & (c_pos < kf)
    lo = jnp.where(few, KEY_NEG, jnp.where(pos, 1, jnp.where(zero, 0, key_min)))
    hi = jnp.where(few, KEY_NEG + 1, jnp.where(pos, key_max + 1, jnp.where(zero, 1, 0)))
    c_lo = jnp.where(few, 2.0 * kf, jnp.where(pos, c_pos, jnp.where(zero, c_zero, n_causal)))
    c_hi = jnp.where(few, n_causal, jnp.where(pos, 0.0, jnp.where(zero, c_pos, c_zero)))

    def unsettled(lo, hi, c_lo):
        return jnp.sum(jnp.where((c_lo != kf) & (hi - lo > 1), 1.0, 0.0)) > 0.0

    def narrow(state):
        step, _, lo, hi, c_lo, g_lo, g_hi, last_up = state
        x_lo = to_float(lo)
        p_interp = to_key(x_lo + (to_float(hi) - x_lo) * (g_lo / (g_lo - g_hi)))
        p_mid = lo + lax.shift_right_logical(hi - lo, 1)
        p = jnp.where(step < INTERP_STEPS, p_interp, p_mid)
        p = jnp.maximum(jnp.minimum(p, hi - 1), jnp.minimum(lo + 1, hi - 1))
        c_p = count_ge(p)
        g_p = gap(c_p)
        up = c_p >= kf
        g_hi = jnp.where(up, jnp.where(last_up > 0.0, 0.5 * g_hi, g_hi), g_p)
        g_lo = jnp.where(up, g_p, jnp.where(last_up < 0.0, 0.5 * g_lo, g_lo))
        lo, c_lo, hi = jnp.where(up, p, lo), jnp.where(up, c_p, c_lo), jnp.where(up, hi, p)
        return step + 1, unsettled(lo, hi, c_lo), lo, hi, c_lo, g_lo, g_hi, jnp.where(up, 1.0, -1.0)

    state = (jnp.int32(0), unsettled(lo, hi, c_lo), lo, hi, c_lo, gap(c_lo), gap(c_hi), jnp.zeros((qb, 1), f32))
    state = lax.while_loop(lambda s: s[1] & (s[0] < MAX_NARROW_STEPS), narrow, state)
    thr = state[2]
    quota = kf - count_ge(thr + 1)

    before = (lax.broadcasted_iota(i32, (kc, kc), 0) < lax.broadcasted_iota(i32, (kc, kc), 1))
    before = jnp.where(before, 1.0, 0.0).astype(MXU_DTYPE)
    _flash_reset(m_ref, acc_ref)
    alpha_ref[...] = jnp.zeros(alpha_ref.shape, f32)
    p_ref[...] = jnp.zeros(p_ref.shape, p_ref.dtype)
    heads = _row_slices(A_HEADS * qb)

    def fold(c, buf):
        off = pl.multiple_of(jnp.maximum(c, 0) * kc, kc)
        _flash_accumulate(va_ref[pl.ds(off, kc), :], alpha_ref.at[buf], p_ref.at[buf], acc_ref)

    def score(c, buf, ties_seen):
        off = pl.multiple_of(c * kc, kc)
        key = skey_ref[:, pl.ds(off, kc)]
        eq = key == thr
        eqf = jnp.where(eq, 1.0, 0.0)
        rank = ties_seen + _mm(eqf.astype(MXU_DTYPE), before)
        sel = ((key > thr) | (eq & (rank < quota))) & (col + off <= t_row)
        bias = jnp.where(sel, 0.0, NEG)
        kt = kat_ref[:, pl.ds(off, kc)]
        for rows in heads:
            _flash_rows(qa_ref[rows, :], kt, bias, m_ref, alpha_ref.at[buf], p_ref.at[buf], rows)
        return ties_seen + jnp.sum(eqf, axis=1, keepdims=True)

    _skewed_chunks(n_chunks, score, fold, jnp.zeros((qb, 1), f32))
    _fold_last(n_chunks, fold)
    acc = acc_ref[...]
    o_ref[...] = acc[:, :HEAD_DIM] / acc[:, HEAD_DIM:]


def dsa_attention(qi, wi, kit, qa, kat, va, topk, kc=256):
    bsz, seq, _ = qi.shape
    nb = seq // Q_BLOCK
    rows = A_HEADS * Q_BLOCK
    kern = functools.partial(_dsa_kernel, topk=topk, kc=kc)
    return pl.pallas_call(
        kern,
        grid=(bsz, nb),
        in_specs=[pl.BlockSpec((None, Q_BLOCK, IDX_HEADS * IDX_DIM), lambda b, i: (b, i, 0)),
                  pl.BlockSpec((None, Q_BLOCK, IDX_HEADS), lambda b, i: (b, i, 0)),
                  pl.BlockSpec((None, IDX_DIM, seq), lambda b, i: (b, 0, 0)),
                  pl.BlockSpec((None, None, rows, HEAD_DIM), lambda b, i: (b, i, 0, 0)),
                  pl.BlockSpec((None, HEAD_DIM, seq), lambda b, i: (b, 0, 0)),
                  pl.BlockSpec((None, seq, 2 * HEAD_DIM), lambda b, i: (b, 0, 0))],
        out_specs=pl.BlockSpec((None, None, rows, HEAD_DIM), lambda b, i: (b, i, 0, 0)),
        out_shape=jax.ShapeDtypeStruct((bsz, nb, rows, HEAD_DIM), f32),
        scratch_shapes=[pltpu.VMEM((Q_BLOCK, seq), i32), pltpu.VMEM((IDX_HEADS, Q_BLOCK, kc), f32),
                        pltpu.VMEM((rows, LANES), f32), pltpu.VMEM((2, rows, LANES), f32),
                        pltpu.VMEM((rows, 2 * HEAD_DIM), f32), pltpu.VMEM((2, rows, kc), MXU_DTYPE)],
        compiler_params=_params("parallel", "arbitrary"),
        name="dsa_attention",
    )(qi, wi, kit, qa, kat, va)


def _compress_kernel(x_ref, pe_ref, w1_ref, w2_ref, o_ref):
    half = (CMP_LEN // 2) * HEAD_DIM
    x = x_ref[...]
    first = _mm((x + pe_ref[0:1, :]).astype(MXU_DTYPE), w1_ref[0:half, :])
    second = _mm((x + pe_ref[1:2, :]).astype(MXU_DTYPE), w1_ref[half:2 * half, :])
    pre = first + pltpu.roll(second, shift=x.shape[0] - 1, axis=0)
    hid = pre * jax.nn.sigmoid(pre)
    o_ref[...] = _mm(hid.astype(MXU_DTYPE), w2_ref[...])


def nsa_compress(chunks, pe2, w1, w2):
    bsz, groups, nck, width = chunks.shape
    return pl.pallas_call(
        _compress_kernel,
        grid=(bsz, groups),
        in_specs=[pl.BlockSpec((None, None, nck, width), lambda b, g: (b, g, 0, 0)),
                  pl.BlockSpec((2, width), lambda b, g: (0, 0)),
                  pl.BlockSpec((2 * width, CMP_HIDDEN), lambda b, g: (0, 0)),
                  pl.BlockSpec((CMP_HIDDEN, HEAD_DIM), lambda b, g: (0, 0))],
        out_specs=pl.BlockSpec((None, None, nck, HEAD_DIM), lambda b, g: (b, g, 0, 0)),
        out_shape=jax.ShapeDtypeStruct((bsz, groups, nck, HEAD_DIM), f32),
        compiler_params=_params("parallel", "parallel"),
        name="nsa_compress",
    )(chunks, pe2, w1, w2)


def _nsa_kernel(q_ref, graw_ref, kct_ref, vc_ref, kst_ref, vs_ref, kwt_ref, vw_ref, ovl_ref, o_ref,
                m_ref, alpha_ref, acc_ref, p_ref, pw_ref, *, n_slc, n_sel, kc):
    qb = Q_BLOCK
    i = pl.program_id(2)
    q0 = i * qb
    n_cmp = kct_ref.shape[1]
    heads = _row_slices(B_REP * qb)

    t_c = q0 + lax.broadcasted_iota(i32, (qb, n_cmp), 0)
    cmp_end = lax.broadcasted_iota(i32, (qb, n_cmp), 1) * CMP_STRIDE + (CMP_LEN - 1)
    vis = cmp_end <= t_c
    kct = kct_ref[...]
    p_sum = jnp.zeros((qb, n_cmp), f32)
    for rows in heads:
        lc = jnp.where(vis, _mm(q_ref[rows, :], kct), NEG)
        ec = jnp.where(vis, jnp.exp(lc - jnp.max(lc, axis=1, keepdims=True)), 0.0)
        den = jnp.sum(ec, axis=1, keepdims=True)
        p_c = ec / jnp.where(den > 0.0, den, 1.0)
        p_sum = p_sum + p_c
        acc_ref[rows, :HEAD_DIM] = _mm(p_c.astype(MXU_DTYPE), vc_ref[...])
    o_c = acc_ref[:, :HEAD_DIM]

    imp = jnp.dot(p_sum, ovl_ref[...], preferred_element_type=f32, precision=lax.Precision.HIGHEST).T
    t_q = q0 + lax.broadcasted_iota(i32, (LANES, qb), 1)
    blk = lax.broadcasted_iota(i32, (LANES, qb), 0)
    blk_t = t_q // SLC_LEN
    forced = (blk == 0) | (blk == blk_t) | (blk == blk_t - 1)
    imp = jnp.where(forced, FORCED_BOOST, imp)
    imp = jnp.where(blk * SLC_LEN <= t_q, imp, NEG)
    imp = jnp.where(blk < n_slc, imp, -jnp.inf)
    blk_f = blk.astype(f32)

    def pick(_, carry):
        imp, selm = carry
        best = jnp.max(imp, axis=0, keepdims=True)
        first = jnp.min(jnp.where(imp == best, blk_f, float(LANES)), axis=0, keepdims=True)
        hit = blk_f == first
        return jnp.where(hit, -jnp.inf, imp), jnp.where(hit, 1.0, selm)

    _, selm = lax.fori_loop(0, n_sel, pick, (imp, jnp.zeros((LANES, qb), f32)))
    selm = selm.T.astype(MXU_DTYPE)

    t_k = q0 + lax.broadcasted_iota(i32, (qb, kc), 0)
    col = lax.broadcasted_iota(i32, (qb, kc), 1)
    exp_row = lax.broadcasted_iota(i32, (LANES, kc), 0)
    exp_col = lax.broadcasted_iota(i32, (LANES, kc), 1)
    _flash_reset(m_ref, acc_ref)
    alpha_ref[...] = jnp.zeros(alpha_ref.shape, f32)
    p_ref[...] = jnp.zeros(p_ref.shape, p_ref.dtype)

    def fold(c, buf):
        off = pl.multiple_of(jnp.maximum(c, 0) * kc, kc)
        _flash_accumulate(vs_ref[pl.ds(off, kc), :], alpha_ref.at[buf], p_ref.at[buf], acc_ref)

    def score(c, buf, carry):
        off = pl.multiple_of(c * kc, kc)
        expand = jnp.where(exp_row == (exp_col + off) // SLC_LEN, 1.0, 0.0).astype(MXU_DTYPE)
        sel = (_mm(selm, expand) > 0.5) & (col + off <= t_k)
        bias = jnp.where(sel, 0.0, NEG)
        kt = kst_ref[:, pl.ds(off, kc)]
        for rows in heads:
            _flash_rows(q_ref[rows, :], kt, bias, m_ref, alpha_ref.at[buf], p_ref.at[buf], rows)
        return carry

    n_slc_chunks = (q0 + qb + kc - 1) // kc
    _skewed_chunks(n_slc_chunks, score, fold, 0)
    _fold_last(n_slc_chunks, fold)
    acc = acc_ref[...]
    o_s = acc[:, :HEAD_DIM] / acc[:, HEAD_DIM:]

    slab = WINDOW + qb
    w0 = pl.multiple_of(jnp.maximum(q0 - WINDOW, 0), qb)
    dist = (q0 + lax.broadcasted_iota(i32, (qb, slab), 0)) - (w0 + lax.broadcasted_iota(i32, (qb, slab), 1))
    bias = jnp.where((dist >= 0) & (dist < WINDOW), 0.0, NEG)
    kt = kwt_ref[:, pl.ds(w0, slab)]
    for rows in heads:
        s = _mm(q_ref[rows, :], kt) + bias
        pw_ref[rows, :] = jnp.exp(s - jnp.max(s, axis=1, keepdims=True)).astype(pw_ref.dtype)
    acc = _mm(pw_ref[...], vw_ref[pl.ds(w0, slab), :])
    o_w = acc[:, :HEAD_DIM] / acc[:, HEAD_DIM:]

    gates = jax.nn.sigmoid(graw_ref[...])
    o_ref[...] = gates[:, 0:1] * o_c + gates[:, 1:2] * o_s + gates[:, 2:3] * o_w


def nsa_attention(q, graw, kct, vc, kst, vs, kwt, vw, ovl, n_slc, n_sel, kc=256):
    bsz, groups, nb, rows, _ = q.shape
    seq = kst.shape[-1]
    n_cmp = kct.shape[-1]
    kern = functools.partial(_nsa_kernel, n_slc=n_slc, n_sel=n_sel, kc=kc)
    per_bg = lambda b, g, i: (b, g, 0, 0)
    return pl.pallas_call(
        kern,
        grid=(bsz, groups, nb),
        in_specs=[pl.BlockSpec((None, None, None, rows, HEAD_DIM), lambda b, g, i: (b, g, i, 0, 0)),
                  pl.BlockSpec((None, None, None, rows, 3), lambda b, g, i: (b, g, i, 0, 0)),
                  pl.BlockSpec((None, None, HEAD_DIM, n_cmp), per_bg),
                  pl.BlockSpec((None, None, n_cmp, HEAD_DIM), per_bg),
                  pl.BlockSpec((None, None, HEAD_DIM, seq), per_bg),
                  pl.BlockSpec((None, None, seq, 2 * HEAD_DIM), per_bg),
                  pl.BlockSpec((None, None, HEAD_DIM, seq), per_bg),
                  pl.BlockSpec((None, None, seq, 2 * HEAD_DIM), per_bg),
                  pl.BlockSpec((n_cmp, LANES), lambda b, g, i: (0, 0))],
        out_specs=pl.BlockSpec((None, None, None, rows, HEAD_DIM), lambda b, g, i: (b, g, i, 0, 0)),
        out_shape=jax.ShapeDtypeStruct((bsz, groups, nb, rows, HEAD_DIM), f32),
        scratch_shapes=[pltpu.VMEM((rows, LANES), f32), pltpu.VMEM((2, rows, LANES), f32),
                        pltpu.VMEM((rows, 2 * HEAD_DIM), f32), pltpu.VMEM((2, rows, kc), MXU_DTYPE),
                        pltpu.VMEM((rows, WINDOW + Q_BLOCK), MXU_DTYPE)],
        compiler_params=_params("parallel", "parallel", "arbitrary"),
        name="nsa_attention",
    )(q, graw, kct, vc, kst, vs, kwt, vw, ovl)


def _diff_kernel(lam_ref, q_ref, kt_ref, v_ref, g_ref, o_ref, m_ref, alpha_ref, acc_ref, p_ref,
                 *, tq, kc, out_scale):
    i = pl.program_id(2)
    q0 = i * tq
    dv = v_ref.shape[-1] // 2
    groups = _row_slices(tq)
    col = lax.broadcasted_iota(i32, (Q_BLOCK, kc), 1)
    t_row = [q0 + r * Q_BLOCK + lax.broadcasted_iota(i32, (Q_BLOCK, kc), 0) for r in range(len(groups))]
    for half in range(2):
        _flash_reset(m_ref.at[half], acc_ref.at[half])
    alpha_ref[...] = jnp.zeros(alpha_ref.shape, f32)
    p_ref[...] = jnp.zeros(p_ref.shape, p_ref.dtype)

    def fold(c, buf):
        off = pl.multiple_of(jnp.maximum(c, 0) * kc, kc)
        v = v_ref[pl.ds(off, kc), :]
        for half in range(2):
            _flash_accumulate(v, alpha_ref.at[buf, half], p_ref.at[buf, half], acc_ref.at[half])

    def score(c, buf, masked):
        off = pl.multiple_of(c * kc, kc)
        for half in range(2):
            kt = kt_ref[half, :, pl.ds(off, kc)]
            for r, rows in enumerate(groups):
                bias = jnp.where(col + off <= t_row[r], 0.0, NEG) if masked else None
                _flash_rows(q_ref[half, rows, :], kt, bias, m_ref.at[half], alpha_ref.at[buf, half],
                            p_ref.at[buf, half], rows)

    def score_full(c, buf, carry):
        score(c, buf, False)
        return carry

    n_full = q0 // kc
    _skewed_chunks(n_full, score_full, fold, 0)

    def finish(buf):
        fold(n_full - 1, 1 - buf)
        score(n_full, buf, True)
        fold(n_full, buf)

    for buf in range(2):
        pl.when(n_full % 2 == buf)(functools.partial(finish, buf))
    a1 = acc_ref[0]
    a2 = acc_ref[1]
    o = a1[:, :dv] / a1[:, dv:] - lam_ref[0] * (a2[:, :dv] / a2[:, dv:])
    o = o * lax.rsqrt(jnp.mean(o * o, axis=-1, keepdims=True) + LN_EPS) * g_ref[...]
    o_ref[...] = o * out_scale


def diff_attention(lam, q, kt, v, subln_g, out_scale, tq=256, kc=512):
    bsz, heads, _, seq, _ = q.shape
    dv = v.shape[-1] // 2
    tq = min(tq, seq)
    kc = min(kc, seq)
    assert kc % tq == 0 and seq % kc == 0
    kern = functools.partial(_diff_kernel, tq=tq, kc=kc, out_scale=out_scale)
    return pl.pallas_call(
        kern,
        grid=(bsz, heads, seq // tq),
        in_specs=[pl.BlockSpec(memory_space=pltpu.SMEM),
                  pl.BlockSpec((None, None, 2, tq, HEAD_DIM), lambda b, h, i: (b, h, 0, i, 0)),
                  pl.BlockSpec((None, None, 2, HEAD_DIM, seq), lambda b, h, i: (b, h, 0, 0, 0)),
                  pl.BlockSpec((None, None, seq, 2 * dv), lambda b, h, i: (b, h, 0, 0)),
                  pl.BlockSpec((1, dv), lambda b, h, i: (0, 0))],
        out_specs=pl.BlockSpec((None, tq, dv), lambda b, h, i: (b, i, h)),
        out_shape=jax.ShapeDtypeStruct((bsz, seq, heads * dv), f32),
        scratch_shapes=[pltpu.VMEM((2, tq, LANES), f32), pltpu.VMEM((2, 2, tq, LANES), f32),
                        pltpu.VMEM((2, tq, 2 * dv), f32), pltpu.VMEM((2, 2, tq, kc), MXU_DTYPE)],
        compiler_params=_params("parallel", "parallel", "arbitrary"),
        name="diff_attention",
    )(lam, q, kt, v, subln_g.reshape(1, dv))


def _split(h, layout):
    offs = np.cumsum([w for _, w in layout])[:-1].tolist()
    return jnp.split(h, offs, axis=-1)


def _rope_tables(positions):
    inv_freq = ROPE_THETA ** (-jnp.arange(0, ROT_DIM, 2, dtype=f32) / ROT_DIM)
    ang = positions.astype(f32)[..., None] * inv_freq
    return jnp.cos(ang), jnp.sin(ang)


def _apply_rope(x, cos, sin):
    shape = cos.shape[:2] + (1,) * (x.ndim - 3) + cos.shape[-1:]
    c = cos.reshape(shape)
    s = sin.reshape(shape)
    half = ROT_DIM // 2
    x1, x2 = x[..., :half], x[..., half:ROT_DIM]
    return jnp.concatenate([x1 * c - x2 * s, x2 * c + x1 * s, x[..., ROT_DIM:]], axis=-1)


def _pad_cols(w, mult):
    n = w.shape[-1]
    pad = (-n) % mult
    return jnp.pad(w, ((0, 0), (0, pad))) if pad else w


def _overlap_matrix(n_cmp_rows, n_slc):
    c_start = np.arange(n_cmp_rows) * CMP_STRIDE
    s_start = np.arange(LANES) * SLC_LEN
    ovl = (c_start[:, None] < s_start[None, :] + SLC_LEN) & (c_start[:, None] + CMP_LEN > s_start[None, :])
    ovl = ovl & (np.arange(LANES)[None, :] < n_slc)
    return jnp.asarray(ovl.astype(np.float32))


def _ab_mixer(x2, bsz, seq, positions, cos, sin, w_in, pe_k, pe_v, ck1, ck2, cv1, cv2):
    nb = seq // Q_BLOCK
    width = sum(w for _, w in AB_LAYOUT)
    h = project(x2, _pad_cols(w_in, 2 * LANES).astype(MXU_DTYPE))[:, :width].reshape(bsz, seq, width)
    (q_a, k_a, v_a, q_idx, k_idx, w_idx, q_b, k_cmp, v_cmp, k_slc, v_slc, k_win, v_win, g_b) = _split(h, AB_LAYOUT)

    q_a = _apply_rope(q_a.reshape(bsz, seq, A_HEADS, HEAD_DIM), cos, sin) * QK_SCALE
    q_a = q_a.reshape(bsz, nb, Q_BLOCK, A_HEADS, HEAD_DIM).transpose(0, 1, 3, 2, 4)
    q_a = q_a.reshape(bsz, nb, A_HEADS * Q_BLOCK, HEAD_DIM).astype(MXU_DTYPE)
    k_a = _apply_rope(k_a, cos, sin).transpose(0, 2, 1).astype(MXU_DTYPE)
    q_idx = _apply_rope(q_idx.reshape(bsz, seq, IDX_HEADS, IDX_DIM), cos, sin)
    q_idx = q_idx.reshape(bsz, seq, IDX_HEADS * IDX_DIM).astype(MXU_DTYPE)
    k_idx = _apply_rope(k_idx, cos, sin).transpose(0, 2, 1).astype(MXU_DTYPE)
    w_idx = w_idx * (IDX_HEADS * IDX_DIM) ** -0.5
    o_a = dsa_attention(q_idx, w_idx, k_idx, q_a, k_a, _with_ones(v_a.astype(MXU_DTYPE)),
                        topk=min(DSA_TOPK, seq // 4))
    o_a = o_a.reshape(bsz, nb, A_HEADS, Q_BLOCK, HEAD_DIM).transpose(0, 1, 3, 2, 4)
    o_a = o_a.reshape(bsz * seq, A_HEADS * HEAD_DIM)

    groups = B_KV_GROUPS
    n_rows = seq // CMP_STRIDE
    n_cmp = (seq - CMP_LEN) // CMP_STRIDE + 1
    assert n_cmp == n_rows - 1
    n_slc = seq // SLC_LEN
    assert n_slc <= LANES

    def chunked(kv):
        kv = kv.reshape(bsz, n_rows, CMP_STRIDE, groups, HEAD_DIM).transpose(0, 3, 1, 2, 4)
        return kv.reshape(bsz, groups, n_rows, CMP_STRIDE * HEAD_DIM)

    def pe_halves(pe):
        return pe.reshape(2, CMP_STRIDE * HEAD_DIM)

    k_c = nsa_compress(chunked(k_cmp), pe_halves(pe_k), ck1.astype(MXU_DTYPE), ck2.astype(MXU_DTYPE))
    v_c = nsa_compress(chunked(v_cmp), pe_halves(pe_v), cv1.astype(MXU_DTYPE), cv2.astype(MXU_DTYPE))
    cmp_end = jnp.minimum(jnp.arange(n_rows) * CMP_STRIDE + CMP_LEN - 1, seq - 1)
    cos_c, sin_c = _rope_tables(positions[:, cmp_end])
    k_c = _apply_rope(k_c.transpose(0, 2, 1, 3), cos_c, sin_c)
    k_c = k_c.transpose(0, 2, 3, 1).astype(MXU_DTYPE)
    v_c = v_c.astype(MXU_DTYPE)

    def keys_t(k, rope):
        k = k.reshape(bsz, seq, groups, HEAD_DIM)
        if rope:
            k = _apply_rope(k, cos, sin)
        return k.transpose(0, 2, 3, 1).astype(MXU_DTYPE)

    def vals(v):
        return _with_ones(v.reshape(bsz, seq, groups, HEAD_DIM).transpose(0, 2, 1, 3).astype(MXU_DTYPE))

    def head_rows(a, last):
        a = a.reshape(bsz, nb, Q_BLOCK, groups, B_REP, last).transpose(0, 3, 1, 4, 2, 5)
        return a.reshape(bsz, groups, nb, B_REP * Q_BLOCK, last)

    q_b = _apply_rope(q_b.reshape(bsz, seq, B_HEADS, HEAD_DIM), cos, sin) * QK_SCALE
    q_b = head_rows(q_b.reshape(bsz, seq, B_HEADS * HEAD_DIM), HEAD_DIM).astype(MXU_DTYPE)
    o_b = nsa_attention(q_b, head_rows(g_b, 3), k_c, v_c, keys_t(k_slc, True), vals(v_slc),
                        keys_t(k_win, True), vals(v_win), _overlap_matrix(n_rows, n_slc),
                        n_slc=n_slc, n_sel=min(SLC_TOPN, n_slc))
    o_b = o_b.reshape(bsz, groups, nb, B_REP, Q_BLOCK, HEAD_DIM).transpose(0, 2, 4, 1, 3, 5)
    o_b = o_b.reshape(bsz * seq, B_HEADS * HEAD_DIM)
    return jnp.concatenate([o_a, o_b], axis=-1)


def _diff_mixer(x2, bsz, seq, cos, sin, w_in, lq1, lk1, lq2, lk2, subln_g, lam_init):
    h = project(x2, w_in.astype(MXU_DTYPE)).reshape(bsz, seq, -1)
    q, k, v = _split(h, C_LAYOUT)
    q = _apply_rope(q.reshape(bsz, seq, C_HEADS, 2, HEAD_DIM), cos, sin) * QK_SCALE
    k = _apply_rope(k.reshape(bsz, seq, C_HEADS, 2, HEAD_DIM), cos, sin)
    q = q.transpose(0, 2, 3, 1, 4).astype(MXU_DTYPE)
    kt = k.transpose(0, 2, 3, 4, 1).astype(MXU_DTYPE)
    v = v.reshape(bsz, seq, C_HEADS, 2 * HEAD_DIM).transpose(0, 2, 1, 3).astype(MXU_DTYPE)
    lam = (jnp.exp(jnp.sum(lq1 * lk1)) - jnp.exp(jnp.sum(lq2 * lk2)) + lam_init).reshape(1).astype(f32)
    o = diff_attention(lam, q, kt, _with_ones(v), subln_g, 1.0 - lam_init)
    return o.reshape(bsz * seq, C_HEADS * 2 * HEAD_DIM)


def kernel(x, positions, ab_w_in, cmp_pe_k, cmp_pe_v, cmp_k_w1, cmp_k_w2, cmp_v_w1, cmp_v_w2, ab_w_out, ln_ab_g, ln_ab_b, ffn_w1, ffn_w3, ffn_w2, ln_ffn_g, ln_ffn_b, c_w_in, lambda_q1, lambda_k1, lambda_q2, lambda_k2, c_subln_g, c_w_out, ln_c_g, ln_c_b, router_w, moe_w1, moe_w3, moe_w2, ln_moe_g, ln_moe_b):
    bsz, seq, d = x.shape
    assert seq % COUNT_STRIP == 0 and seq >= WINDOW + Q_BLOCK and d == D_MODEL
    cos, sin = _rope_tables(positions)
    x2 = x.reshape(bsz * seq, d)
    for layer in range(DEPTH):
        i = layer // 2
        if layer % 2 == 0:
            o = _ab_mixer(x2, bsz, seq, positions, cos, sin, ab_w_in[i], cmp_pe_k[i], cmp_pe_v[i],
                          cmp_k_w1[i], cmp_k_w2[i], cmp_v_w1[i], cmp_v_w2[i])
            x2 = project_residual_ln(o, ab_w_out[i].astype(MXU_DTYPE), x2, ln_ab_g[i], ln_ab_b[i])
            x2 = ffn_residual_ln(x2, ffn_w1[i].astype(MXU_DTYPE), ffn_w3[i].astype(MXU_DTYPE),
                                 ffn_w2[i].astype(MXU_DTYPE), ln_ffn_g[i], ln_ffn_b[i], tm=512, tf=1408)
        else:
            lam_init = 0.8 - 0.6 * math.exp(-0.3 * layer)
            o = _diff_mixer(x2, bsz, seq, cos, sin, c_w_in[i], lambda_q1[i], lambda_k1[i], lambda_q2[i],
                            lambda_k2[i], c_subln_g[i], lam_init)
            x2 = project_residual_ln(o, c_w_out[i].astype(MXU_DTYPE), x2, ln_c_g[i], ln_c_b[i])
            routes = route_top2(x2, router_w[i])
            x2 = moe_residual_ln(x2, routes, moe_w1[i].astype(MXU_DTYPE), moe_w3[i].astype(MXU_DTYPE),
                                 moe_w2[i].astype(MXU_DTYPE), ln_moe_g[i], ln_moe_b[i], tm=min(2048, bsz * seq // 2), tf=896, rt=128)
    return x2.reshape(bsz, seq, d)
```

```python
import functools
import math

import numpy as np
import jax
import jax.numpy as jnp
from jax import lax
from jax.experimental import pallas as pl
from jax.experimental.pallas import tpu as pltpu

f32 = jnp.float32
i32 = jnp.int32
MXU_DTYPE = jnp.bfloat16
VMEM_LIMIT_BYTES = 56 * 1024 * 1024
LANES = 128

D_MODEL = 1024
DEPTH = 2
HEAD_DIM = 64
ROT_DIM = HEAD_DIM // 4
ROPE_THETA = 500000.0
Q_BLOCK = 128
NEG = -1e30
LN_EPS = 1e-5
A_HEADS = 8
IDX_HEADS = 4
IDX_DIM = 64
DSA_TOPK = 256
B_HEADS = 8
B_KV_GROUPS = 2
B_REP = B_HEADS // B_KV_GROUPS
CMP_LEN = 32
CMP_STRIDE = 16
CMP_HIDDEN = 128
SLC_LEN = 64
SLC_TOPN = 16
WINDOW = 512
FORCED_BOOST = 1e6
C_HEADS = 8
N_EXPERTS = 8
TOP_K = 2
DEEPNORM_ALPHA = (2 * DEPTH) ** 0.25
QK_SCALE = HEAD_DIM ** -0.5
INT_MIN = -(2 ** 31)
COUNT_STRIP = 512
ROW_UNROLL = 4

AB_LAYOUT = (
    ("q_a", A_HEADS * HEAD_DIM), ("k_a", HEAD_DIM), ("v_a", HEAD_DIM),
    ("q_idx", IDX_HEADS * IDX_DIM), ("k_idx", IDX_DIM), ("w_idx", IDX_HEADS),
    ("q_b", B_HEADS * HEAD_DIM),
    ("k_cmp", B_KV_GROUPS * HEAD_DIM), ("v_cmp", B_KV_GROUPS * HEAD_DIM),
    ("k_slc", B_KV_GROUPS * HEAD_DIM), ("v_slc", B_KV_GROUPS * HEAD_DIM),
    ("k_win", B_KV_GROUPS * HEAD_DIM), ("v_win", B_KV_GROUPS * HEAD_DIM),
    ("gate_b", 3 * B_HEADS),
)
C_LAYOUT = (("q_c", C_HEADS * 2 * HEAD_DIM), ("k_c", C_HEADS * 2 * HEAD_DIM), ("v_c", C_HEADS * 2 * HEAD_DIM))


def _params(*sem):
    return pltpu.CompilerParams(dimension_semantics=sem, vmem_limit_bytes=VMEM_LIMIT_BYTES)


def _mm(a, b):
    return jnp.dot(a, b, preferred_element_type=f32)


def _layer_norm_rows(y, g, b):
    mu = jnp.mean(y, axis=-1, keepdims=True)
    yc = y - mu
    var = jnp.mean(yc * yc, axis=-1, keepdims=True)
    return yc * lax.rsqrt(var + LN_EPS) * g + b


def _proj_kernel(a_ref, w_ref, o_ref):
    o_ref[...] = _mm(a_ref[...].astype(MXU_DTYPE), w_ref[...]).astype(o_ref.dtype)


def project(a, w, tm=512):
    m, k = a.shape
    n = w.shape[1]
    return pl.pallas_call(
        _proj_kernel,
        grid=(m // tm,),
        in_specs=[pl.BlockSpec((tm, k), lambda i: (i, 0)), pl.BlockSpec((k, n), lambda i: (0, 0))],
        out_specs=pl.BlockSpec((tm, n), lambda i: (i, 0)),
        out_shape=jax.ShapeDtypeStruct((m, n), f32),
        compiler_params=_params("parallel"),
        name="project",
    )(a, w)


def _proj_ln_kernel(a_ref, w_ref, res_ref, g_ref, b_ref, o_ref):
    h = _mm(a_ref[...].astype(MXU_DTYPE), w_ref[...])
    o_ref[...] = _layer_norm_rows(DEEPNORM_ALPHA * res_ref[...] + h, g_ref[...], b_ref[...])


def project_residual_ln(a, w, res, g, b, tm=512):
    m, k = a.shape
    n = w.shape[1]
    return pl.pallas_call(
        _proj_ln_kernel,
        grid=(m // tm,),
        in_specs=[pl.BlockSpec((tm, k), lambda i: (i, 0)), pl.BlockSpec((k, n), lambda i: (0, 0)),
                  pl.BlockSpec((tm, n), lambda i: (i, 0)),
                  pl.BlockSpec((1, n), lambda i: (0, 0)), pl.BlockSpec((1, n), lambda i: (0, 0))],
        out_specs=pl.BlockSpec((tm, n), lambda i: (i, 0)),
        out_shape=jax.ShapeDtypeStruct((m, n), f32),
        compiler_params=_params("parallel"),
        name="project_residual_ln",
    )(a, w, res, g.reshape(1, n), b.reshape(1, n))


def _swiglu_tile(xb, w1_ref, w3_ref, w2_ref):
    a = _mm(xb, w1_ref[...])
    h = (a * jax.nn.sigmoid(a)) * _mm(xb, w3_ref[...])
    return _mm(h.astype(MXU_DTYPE), w2_ref[...])


def _ffn_ln_kernel(x_ref, w1_ref, w3_ref, w2_ref, g_ref, b_ref, o_ref, xb_ref, acc_ref):
    f = pl.program_id(1)

    @pl.when(f == 0)
    def _():
        xb_ref[...] = x_ref[...].astype(MXU_DTYPE)
        acc_ref[...] = jnp.zeros_like(acc_ref)

    acc_ref[...] += _swiglu_tile(xb_ref[...], w1_ref, w3_ref, w2_ref)

    @pl.when(f == pl.num_programs(1) - 1)
    def _():
        o_ref[...] = _layer_norm_rows(DEEPNORM_ALPHA * x_ref[...] + acc_ref[...], g_ref[...], b_ref[...])


def ffn_residual_ln(x, w1, w3, w2, g, b, tm, tf):
    m, d = x.shape
    ff = w1.shape[1]
    return pl.pallas_call(
        _ffn_ln_kernel,
        grid=(m // tm, ff // tf),
        in_specs=[pl.BlockSpec((tm, d), lambda i, f: (i, 0)),
                  pl.BlockSpec((d, tf), lambda i, f: (0, f)),
                  pl.BlockSpec((d, tf), lambda i, f: (0, f)),
                  pl.BlockSpec((tf, d), lambda i, f: (f, 0)),
                  pl.BlockSpec((1, d), lambda i, f: (0, 0)), pl.BlockSpec((1, d), lambda i, f: (0, 0))],
        out_specs=pl.BlockSpec((tm, d), lambda i, f: (i, 0)),
        out_shape=jax.ShapeDtypeStruct((m, d), f32),
        scratch_shapes=[pltpu.VMEM((tm, d), MXU_DTYPE), pltpu.VMEM((tm, d), f32)],
        compiler_params=_params("parallel", "arbitrary"),
        name="ffn_residual_ln",
    )(x, w1, w3, w2, g.reshape(1, d), b.reshape(1, d))


ROUTE_IDS = N_EXPERTS
ROUTE_GATES = N_EXPERTS + 2


def _router_kernel(x_ref, w_ref, o_ref, *, n_experts):
    logits = jnp.dot(x_ref[...], w_ref[...], preferred_element_type=f32, precision=lax.Precision.HIGHEST)
    lane = lax.broadcasted_iota(i32, logits.shape, 1).astype(f32)
    logits = jnp.where(lane < n_experts, logits, -jnp.inf)
    v1 = jnp.max(logits, axis=1, keepdims=True)
    i1 = jnp.min(jnp.where(logits == v1, lane, float(LANES)), axis=1, keepdims=True)
    rest = jnp.where(lane == i1, -jnp.inf, logits)
    v2 = jnp.max(rest, axis=1, keepdims=True)
    i2 = jnp.min(jnp.where(rest == v2, lane, float(LANES)), axis=1, keepdims=True)
    e2 = jnp.exp(v2 - v1)
    g1 = 1.0 / (1.0 + e2)
    g2 = e2 / (1.0 + e2)
    out = jnp.where(lane == ROUTE_IDS, i1, 0.0) + jnp.where(lane == ROUTE_IDS + 1, i2, 0.0)
    out = out + jnp.where(lane == ROUTE_GATES, g1, 0.0) + jnp.where(lane == ROUTE_GATES + 1, g2, 0.0)
    o_ref[...] = out


def route_top2(x, router_w, tm=512):
    m, d = x.shape
    n_experts = router_w.shape[1]
    w = jnp.zeros((d, LANES), f32).at[:, :n_experts].set(router_w)
    return pl.pallas_call(
        functools.partial(_router_kernel, n_experts=n_experts),
        grid=(m // tm,),
        in_specs=[pl.BlockSpec((tm, d), lambda i: (i, 0)), pl.BlockSpec((d, LANES), lambda i: (0, 0))],
        out_specs=pl.BlockSpec((tm, LANES), lambda i: (i, 0)),
        out_shape=jax.ShapeDtypeStruct((m, LANES), f32),
        compiler_params=_params("parallel"),
        name="route_top2",
    )(x, w)


def _moe_ln_kernel(tok_ref, gs_ref, off_ref, x_ref, w1_ref, w3_ref, w2_ref, g_ref, b_ref, o_ref,
                   xg_ref, xb_ref, y_ref, *, rt):
    c = pl.program_id(0)
    e = pl.program_id(1)
    f = pl.program_id(2)
    last_f = pl.num_programs(2) - 1
    start = off_ref[0, e]
    count = off_ref[0, e + 1] - start
    n_tiles = (count + rt - 1) // rt

    @pl.when((c == 0) & (e == 0) & (f == 0))
    def _():
        xg_ref[...] = jnp.zeros_like(xg_ref)

    @pl.when((e == 0) & (f == 0))
    def _():
        o_ref[...] = jnp.zeros_like(o_ref)

    def row_loop(body):
        def group(j, carry):
            for u in range(ROW_UNROLL):
                body(j * ROW_UNROLL + u)
            return carry

        def single(r, carry):
            body(r)
            return carry
        lax.fori_loop(0, count // ROW_UNROLL, group, 0)
        lax.fori_loop((count // ROW_UNROLL) * ROW_UNROLL, count, single, 0)

    @pl.when(f == 0)
    def _():
        def gather(r):
            t = tok_ref[0, start + r]
            xg_ref[pl.ds(r, 1), :] = x_ref[pl.ds(t, 1), :]
        row_loop(gather)

        def cast(j, carry):
            rows = pl.ds(pl.multiple_of(j * rt, rt), rt)
            xb_ref[rows, :] = xg_ref[rows, :].astype(MXU_DTYPE)
            return carry
        lax.fori_loop(0, n_tiles, cast, 0)

    def tile(j, carry):
        rows = pl.ds(pl.multiple_of(j * rt, rt), rt)
        y = _swiglu_tile(xb_ref[rows, :], w1_ref, w3_ref, w2_ref)

        @pl.when(f == 0)
        def _():
            y_ref[rows, :] = y

        @pl.when(f != 0)
        def _():
            y_ref[rows, :] += y
        return carry
    lax.fori_loop(0, n_tiles, tile, 0)

    @pl.when(f == last_f)
    def _():
        def scatter(r):
            t = tok_ref[0, start + r]
            o_ref[pl.ds(t, 1), :] += gs_ref[0, start + r] * y_ref[pl.ds(r, 1), :]
        row_loop(scatter)

    @pl.when((e == pl.num_programs(1) - 1) & (f == last_f))
    def _():
        o_ref[...] = _layer_norm_rows(DEEPNORM_ALPHA * x_ref[...] + o_ref[...], g_ref[...], b_ref[...])


def moe_residual_ln(x, routes, w1, w3, w2, g, b, tm, tf, rt=128):
    m, d = x.shape
    n_experts, _, ff = w1.shape
    n_chunks = m // tm
    ids = routes[:, ROUTE_IDS:ROUTE_IDS + TOP_K].astype(i32).reshape(n_chunks, tm * TOP_K)
    gts = routes[:, ROUTE_GATES:ROUTE_GATES + TOP_K].reshape(n_chunks, tm * TOP_K)
    order = jnp.argsort(ids, axis=1, stable=True).astype(i32)
    tok = order // TOP_K
    gs = jnp.take_along_axis(gts, order, axis=1)
    counts = jnp.sum(ids[:, :, None] == jnp.arange(n_experts, dtype=i32)[None, None, :], axis=1, dtype=i32)
    offs = jnp.concatenate([jnp.zeros((n_chunks, 1), i32), jnp.cumsum(counts, axis=1, dtype=i32)], axis=1)
    smem = lambda width: pl.BlockSpec((None, 1, width), lambda c, e, f: (c, 0, 0), memory_space=pltpu.SMEM)
    return pl.pallas_call(
        functools.partial(_moe_ln_kernel, rt=rt),
        grid=(n_chunks, n_experts, ff // tf),
        in_specs=[smem(tm * TOP_K), smem(tm * TOP_K), smem(n_experts + 1),
                  pl.BlockSpec((tm, d), lambda c, e, f: (c, 0), pipeline_mode=pl.Buffered(1)),
                  pl.BlockSpec((None, d, tf), lambda c, e, f: (e, 0, f)),
                  pl.BlockSpec((None, d, tf), lambda c, e, f: (e, 0, f)),
                  pl.BlockSpec((None, tf, d), lambda c, e, f: (e, f, 0)),
                  pl.BlockSpec((1, d), lambda c, e, f: (0, 0)), pl.BlockSpec((1, d), lambda c, e, f: (0, 0))],
        out_specs=pl.BlockSpec((tm, d), lambda c, e, f: (c, 0), pipeline_mode=pl.Buffered(1)),
        out_shape=jax.ShapeDtypeStruct((m, d), f32),
        scratch_shapes=[pltpu.VMEM((tm, d), f32), pltpu.VMEM((tm, d), MXU_DTYPE), pltpu.VMEM((tm, d), f32)],
        compiler_params=_params("arbitrary", "arbitrary", "arbitrary"),
        name="moe_residual_ln",
    )(tok[:, None, :], gs[:, None, :], offs[:, None, :], x, w1, w3, w2, g.reshape(1, d), b.reshape(1, d))


def _flash_reset(m_ref, acc_ref):
    m_ref[...] = jnp.full(m_ref.shape, NEG, f32)
    acc_ref[...] = jnp.zeros(acc_ref.shape, f32)


def _flash_rows(q, kt, bias, m_ref, alpha_ref, p_ref, rows):
    s = _mm(q, kt)
    if bias is not None:
        s = s + bias
    m_prev = m_ref[rows, :]
    m_next = jnp.maximum(m_prev, jnp.max(s, axis=1, keepdims=True))
    alpha_ref[rows, :] = jnp.exp(m_prev - m_next)
    m_ref[rows, :] = m_next
    p_ref[rows, :] = jnp.exp(s - _lane_tile(m_next, s.shape[1] // LANES)).astype(p_ref.dtype)


def _flash_accumulate(v_aug, alpha_ref, p_ref, acc_ref):
    alpha = _lane_tile(alpha_ref[...], acc_ref.shape[-1] // LANES)
    acc_ref[...] = alpha * acc_ref[...] + _mm(p_ref[...], v_aug)


def _skewed_chunks(n, score, fold, carry):
    def pair(j, carry):
        c = 2 * j
        fold(c - 1, 1)
        carry = score(c, 0, carry)
        fold(c, 0)
        return score(c + 1, 1, carry)

    def single(c, carry):
        fold(c - 1, 1)
        return score(c, 0, carry)

    carry = lax.fori_loop(0, n // 2, pair, carry)
    return lax.fori_loop(2 * (n // 2), n, single, carry)


def _fold_last(n, fold):
    for buf in range(2):
        pl.when((n - 1) % 2 == buf)(functools.partial(fold, n - 1, buf))


def _lane_tile(x, reps):
    return x if reps == 1 else jnp.concatenate([x] * reps, axis=1)


def _with_ones(v):
    return jnp.concatenate([v, jnp.ones_like(v)], axis=-1)


def _row_slices(n_rows):
    return [slice(r * Q_BLOCK, (r + 1) * Q_BLOCK) for r in range(n_rows // Q_BLOCK)]


def _dsa_kernel(qi_ref, wi_ref, kit_ref, qa_ref, kat_ref, va_ref, o_ref,
                skey_ref, wrep_ref, m_ref, alpha_ref, acc_ref, p_ref, *, topk, kc):
    qb = Q_BLOCK
    i = pl.program_id(1)
    q0 = i * qb
    n_chunks = (q0 + qb + kc - 1) // kc
    t_row = q0 + lax.broadcasted_iota(i32, (qb, kc), 0)
    col = lax.broadcasted_iota(i32, (qb, kc), 1)
    wi = wi_ref[...]
    qi = qi_ref[...]
    qis = [qi[:, h * IDX_DIM:(h + 1) * IDX_DIM] for h in range(IDX_HEADS)]
    for h in range(IDX_HEADS):
        wrep_ref[h] = jnp.broadcast_to(wi[:, h:h + 1], (qb, kc))

    def score_body(c, carry):
        off = pl.multiple_of(c * kc, kc)
        kt = kit_ref[:, pl.ds(off, kc)]
        s = jnp.zeros((qb, kc), f32)
        for h in range(IDX_HEADS):
            s = s + wrep_ref[h] * jnp.maximum(_mm(qis[h], kt), 0.0)
        s = jnp.where(col + off <= t_row, s, NEG)
        bits = pltpu.bitcast(s, i32)
        skey_ref[:, pl.ds(off, kc)] = jnp.where(bits < 0, bits ^ 0x7FFFFFFF, bits)
        return carry

    n_strips = (q0 + qb + COUNT_STRIP - 1) // COUNT_STRIP
    lax.fori_loop(0, n_strips * (COUNT_STRIP // kc), score_body, 0)

    def count_ge(cand):
        def body(c, acc):
            off = pl.multiple_of(c * COUNT_STRIP, COUNT_STRIP)
            for j in range(COUNT_STRIP // LANES):
                acc = acc + jnp.where(skey_ref[:, pl.ds(off + j * LANES, LANES)] >= cand, 1.0, 0.0)
            return acc
        acc = lax.fori_loop(0, n_strips, body, jnp.zeros((qb, LANES), f32))
        return jnp.sum(acc, axis=1, keepdims=True)

    thr = jnp.where(count_ge(jnp.zeros((qb, 1), i32)) >= topk, 0, INT_MIN).astype(i32)

    def bisect(b, thr):
        cand = thr + jnp.left_shift(jnp.int32(1), 30 - b)
        return jnp.where(count_ge(cand) >= topk, cand, thr)

    thr = lax.fori_loop(0, 31, bisect, thr)
    quota = topk - count_ge(thr + 1)

    before = (lax.broadcasted_iota(i32, (kc, kc), 0) < lax.broadcasted_iota(i32, (kc, kc), 1))
    before = jnp.where(before, 1.0, 0.0).astype(MXU_DTYPE)
    _flash_reset(m_ref, acc_ref)
    alpha_ref[...] = jnp.zeros(alpha_ref.shape, f32)
    p_ref[...] = jnp.zeros(p_ref.shape, p_ref.dtype)
    heads = _row_slices(A_HEADS * qb)

    def fold(c, buf):
        off = pl.multiple_of(jnp.maximum(c, 0) * kc, kc)
        _flash_accumulate(va_ref[pl.ds(off, kc), :], alpha_ref.at[buf], p_ref.at[buf], acc_ref)

    def score(c, buf, ties_seen):
        off = pl.multiple_of(c * kc, kc)
        key = skey_ref[:, pl.ds(off, kc)]
        eq = key == thr
        eqf = jnp.where(eq, 1.0, 0.0)
        rank = ties_seen + _mm(eqf.astype(MXU_DTYPE), before)
        sel = ((key > thr) | (eq & (rank < quota))) & (col + off <= t_row)
        bias = jnp.where(sel, 0.0, NEG)
        kt = kat_ref[:, pl.ds(off, kc)]
        for rows in heads:
            _flash_rows(qa_ref[rows, :], kt, bias, m_ref, alpha_ref.at[buf], p_ref.at[buf], rows)
        return ties_seen + jnp.sum(eqf, axis=1, keepdims=True)

    _skewed_chunks(n_chunks, score, fold, jnp.zeros((qb, 1), f32))
    _fold_last(n_chunks, fold)
    acc = acc_ref[...]
    o_ref[...] = acc[:, :HEAD_DIM] / acc[:, HEAD_DIM:]


def dsa_attention(qi, wi, kit, qa, kat, va, topk, kc=512):
    bsz, seq, _ = qi.shape
    nb = seq // Q_BLOCK
    rows = A_HEADS * Q_BLOCK
    kern = functools.partial(_dsa_kernel, topk=topk, kc=kc)
    return pl.pallas_call(
        kern,
        grid=(bsz, nb),
        in_specs=[pl.BlockSpec((None, Q_BLOCK, IDX_HEADS * IDX_DIM), lambda b, i: (b, i, 0)),
                  pl.BlockSpec((None, Q_BLOCK, IDX_HEADS), lambda b, i: (b, i, 0)),
                  pl.BlockSpec((None, IDX_DIM, seq), lambda b, i: (b, 0, 0)),
                  pl.BlockSpec((None, None, rows, HEAD_DIM), lambda b, i: (b, i, 0, 0)),
                  pl.BlockSpec((None, HEAD_DIM, seq), lambda b, i: (b, 0, 0)),
                  pl.BlockSpec((None, seq, 2 * HEAD_DIM), lambda b, i: (b, 0, 0))],
        out_specs=pl.BlockSpec((None, None, rows, HEAD_DIM), lambda b, i: (b, i, 0, 0)),
        out_shape=jax.ShapeDtypeStruct((bsz, nb, rows, HEAD_DIM), f32),
        scratch_shapes=[pltpu.VMEM((Q_BLOCK, seq), i32), pltpu.VMEM((IDX_HEADS, Q_BLOCK, kc), f32),
                        pltpu.VMEM((rows, LANES), f32), pltpu.VMEM((2, rows, LANES), f32),
                        pltpu.VMEM((rows, 2 * HEAD_DIM), f32), pltpu.VMEM((2, rows, kc), MXU_DTYPE)],
        compiler_params=_params("parallel", "arbitrary"),
        name="dsa_attention",
    )(qi, wi, kit, qa, kat, va)


def _compress_kernel(x_ref, pe_ref, w1_ref, w2_ref, o_ref):
    half = (CMP_LEN // 2) * HEAD_DIM
    x = x_ref[...]
    first = _mm((x + pe_ref[0:1, :]).astype(MXU_DTYPE), w1_ref[0:half, :])
    second = _mm((x + pe_ref[1:2, :]).astype(MXU_DTYPE), w1_ref[half:2 * half, :])
    pre = first + pltpu.roll(second, shift=x.shape[0] - 1, axis=0)
    hid = pre * jax.nn.sigmoid(pre)
    o_ref[...] = _mm(hid.astype(MXU_DTYPE), w2_ref[...])


def nsa_compress(chunks, pe2, w1, w2):
    bsz, groups, nck, width = chunks.shape
    return pl.pallas_call(
        _compress_kernel,
        grid=(bsz, groups),
        in_specs=[pl.BlockSpec((None, None, nck, width), lambda b, g: (b, g, 0, 0)),
                  pl.BlockSpec((2, width), lambda b, g: (0, 0)),
                  pl.BlockSpec((2 * width, CMP_HIDDEN), lambda b, g: (0, 0)),
                  pl.BlockSpec((CMP_HIDDEN, HEAD_DIM), lambda b, g: (0, 0))],
        out_specs=pl.BlockSpec((None, None, nck, HEAD_DIM), lambda b, g: (b, g, 0, 0)),
        out_shape=jax.ShapeDtypeStruct((bsz, groups, nck, HEAD_DIM), f32),
        compiler_params=_params("parallel", "parallel"),
        name="nsa_compress",
    )(chunks, pe2, w1, w2)


def _nsa_kernel(q_ref, graw_ref, kct_ref, vc_ref, kst_ref, vs_ref, kwt_ref, vw_ref, ovl_ref, o_ref,
                m_ref, alpha_ref, acc_ref, p_ref, pw_ref, *, n_slc, n_sel, kc):
    qb = Q_BLOCK
    i = pl.program_id(2)
    q0 = i * qb
    n_cmp = kct_ref.shape[1]
    heads = _row_slices(B_REP * qb)

    t_c = q0 + lax.broadcasted_iota(i32, (qb, n_cmp), 0)
    cmp_end = lax.broadcasted_iota(i32, (qb, n_cmp), 1) * CMP_STRIDE + (CMP_LEN - 1)
    vis = cmp_end <= t_c
    kct = kct_ref[...]
    p_sum = jnp.zeros((qb, n_cmp), f32)
    for rows in heads:
        lc = jnp.where(vis, _mm(q_ref[rows, :], kct), NEG)
        ec = jnp.where(vis, jnp.exp(lc - jnp.max(lc, axis=1, keepdims=True)), 0.0)
        den = jnp.sum(ec, axis=1, keepdims=True)
        p_c = ec / jnp.where(den > 0.0, den, 1.0)
        p_sum = p_sum + p_c
        acc_ref[rows, :HEAD_DIM] = _mm(p_c.astype(MXU_DTYPE), vc_ref[...])
    o_c = acc_ref[:, :HEAD_DIM]

    imp = jnp.dot(p_sum, ovl_ref[...], preferred_element_type=f32, precision=lax.Precision.HIGHEST).T
    t_q = q0 + lax.broadcasted_iota(i32, (LANES, qb), 1)
    blk = lax.broadcasted_iota(i32, (LANES, qb), 0)
    blk_t = t_q // SLC_LEN
    forced = (blk == 0) | (blk == blk_t) | (blk == blk_t - 1)
    imp = jnp.where(forced, FORCED_BOOST, imp)
    imp = jnp.where(blk * SLC_LEN <= t_q, imp, NEG)
    imp = jnp.where(blk < n_slc, imp, -jnp.inf)
    blk_f = blk.astype(f32)

    def pick(_, carry):
        imp, selm = carry
        best = jnp.max(imp, axis=0, keepdims=True)
        first = jnp.min(jnp.where(imp == best, blk_f, float(LANES)), axis=0, keepdims=True)
        hit = blk_f == first
        return jnp.where(hit, -jnp.inf, imp), jnp.where(hit, 1.0, selm)

    _, selm = lax.fori_loop(0, n_sel, pick, (imp, jnp.zeros((LANES, qb), f32)))
    selm = selm.T.astype(MXU_DTYPE)

    t_k = q0 + lax.broadcasted_iota(i32, (qb, kc), 0)
    col = lax.broadcasted_iota(i32, (qb, kc), 1)
    exp_row = lax.broadcasted_iota(i32, (LANES, kc), 0)
    exp_col = lax.broadcasted_iota(i32, (LANES, kc), 1)
    _flash_reset(m_ref, acc_ref)
    alpha_ref[...] = jnp.zeros(alpha_ref.shape, f32)
    p_ref[...] = jnp.zeros(p_ref.shape, p_ref.dtype)

    def fold(c, buf):
        off = pl.multiple_of(jnp.maximum(c, 0) * kc, kc)
        _flash_accumulate(vs_ref[pl.ds(off, kc), :], alpha_ref.at[buf], p_ref.at[buf], acc_ref)

    def score(c, buf, carry):
        off = pl.multiple_of(c * kc, kc)
        expand = jnp.where(exp_row == (exp_col + off) // SLC_LEN, 1.0, 0.0).astype(MXU_DTYPE)
        sel = (_mm(selm, expand) > 0.5) & (col + off <= t_k)
        bias = jnp.where(sel, 0.0, NEG)
        kt = kst_ref[:, pl.ds(off, kc)]
        for rows in heads:
            _flash_rows(q_ref[rows, :], kt, bias, m_ref, alpha_ref.at[buf], p_ref.at[buf], rows)
        return carry

    n_slc_chunks = (q0 + qb + kc - 1) // kc
    _skewed_chunks(n_slc_chunks, score, fold, 0)
    _fold_last(n_slc_chunks, fold)
    acc = acc_ref[...]
    o_s = acc[:, :HEAD_DIM] / acc[:, HEAD_DIM:]

    slab = WINDOW + qb
    w0 = pl.multiple_of(jnp.maximum(q0 - WINDOW, 0), qb)
    dist = (q0 + lax.broadcasted_iota(i32, (qb, slab), 0)) - (w0 + lax.broadcasted_iota(i32, (qb, slab), 1))
    bias = jnp.where((dist >= 0) & (dist < WINDOW), 0.0, NEG)
    kt = kwt_ref[:, pl.ds(w0, slab)]
    for rows in heads:
        s = _mm(q_ref[rows, :], kt) + bias
        pw_ref[rows, :] = jnp.exp(s - jnp.max(s, axis=1, keepdims=True)).astype(pw_ref.dtype)
    acc = _mm(pw_ref[...], vw_ref[pl.ds(w0, slab), :])
    o_w = acc[:, :HEAD_DIM] / acc[:, HEAD_DIM:]

    gates = jax.nn.sigmoid(graw_ref[...])
    o_ref[...] = gates[:, 0:1] * o_c + gates[:, 1:2] * o_s + gates[:, 2:3] * o_w


def nsa_attention(q, graw, kct, vc, kst, vs, kwt, vw, ovl, n_slc, n_sel, kc=512):
    bsz, groups, nb, rows, _ = q.shape
    seq = kst.shape[-1]
    n_cmp = kct.shape[-1]
    kern = functools.partial(_nsa_kernel, n_slc=n_slc, n_sel=n_sel, kc=kc)
    per_bg = lambda b, g, i: (b, g, 0, 0)
    return pl.pallas_call(
        kern,
        grid=(bsz, groups, nb),
        in_specs=[pl.BlockSpec((None, None, None, rows, HEAD_DIM), lambda b, g, i: (b, g, i, 0, 0)),
                  pl.BlockSpec((None, None, None, rows, 3), lambda b, g, i: (b, g, i, 0, 0)),
                  pl.BlockSpec((None, None, HEAD_DIM, n_cmp), per_bg),
                  pl.BlockSpec((None, None, n_cmp, HEAD_DIM), per_bg),
                  pl.BlockSpec((None, None, HEAD_DIM, seq), per_bg),
                  pl.BlockSpec((None, None, seq, 2 * HEAD_DIM), per_bg),
                  pl.BlockSpec((None, None, HEAD_DIM, seq), per_bg),
                  pl.BlockSpec((None, None, seq, 2 * HEAD_DIM), per_bg),
                  pl.BlockSpec((n_cmp, LANES), lambda b, g, i: (0, 0))],
        out_specs=pl.BlockSpec((None, None, None, rows, HEAD_DIM), lambda b, g, i: (b, g, i, 0, 0)),
        out_shape=jax.ShapeDtypeStruct((bsz, groups, nb, rows, HEAD_DIM), f32),
        scratch_shapes=[pltpu.VMEM((rows, LANES), f32), pltpu.VMEM((2, rows, LANES), f32),
                        pltpu.VMEM((rows, 2 * HEAD_DIM), f32), pltpu.VMEM((2, rows, kc), MXU_DTYPE),
                        pltpu.VMEM((rows, WINDOW + Q_BLOCK), MXU_DTYPE)],
        compiler_params=_params("parallel", "parallel", "arbitrary"),
        name="nsa_attention",
    )(q, graw, kct, vc, kst, vs, kwt, vw, ovl)


def _diff_kernel(lam_ref, q_ref, kt_ref, v_ref, g_ref, o_ref, m_ref, alpha_ref, acc_ref, p_ref,
                 *, tq, kc, out_scale):
    i = pl.program_id(2)
    q0 = i * tq
    dv = v_ref.shape[-1] // 2
    groups = _row_slices(tq)
    col = lax.broadcasted_iota(i32, (Q_BLOCK, kc), 1)
    t_row = [q0 + r * Q_BLOCK + lax.broadcasted_iota(i32, (Q_BLOCK, kc), 0) for r in range(len(groups))]
    for half in range(2):
        _flash_reset(m_ref.at[half], acc_ref.at[half])
    alpha_ref[...] = jnp.zeros(alpha_ref.shape, f32)
    p_ref[...] = jnp.zeros(p_ref.shape, p_ref.dtype)

    def fold(c, buf):
        off = pl.multiple_of(jnp.maximum(c, 0) * kc, kc)
        v = v_ref[pl.ds(off, kc), :]
        for half in range(2):
            _flash_accumulate(v, alpha_ref.at[buf, half], p_ref.at[buf, half], acc_ref.at[half])

    def score(c, buf, masked):
        off = pl.multiple_of(c * kc, kc)
        for half in range(2):
            kt = kt_ref[half, :, pl.ds(off, kc)]
            for r, rows in enumerate(groups):
                bias = jnp.where(col + off <= t_row[r], 0.0, NEG) if masked else None
                _flash_rows(q_ref[half, rows, :], kt, bias, m_ref.at[half], alpha_ref.at[buf, half],
                            p_ref.at[buf, half], rows)

    def score_full(c, buf, carry):
        score(c, buf, False)
        return carry

    n_full = q0 // kc
    _skewed_chunks(n_full, score_full, fold, 0)

    def finish(buf):
        fold(n_full - 1, 1 - buf)
        score(n_full, buf, True)
        fold(n_full, buf)

    for buf in range(2):
        pl.when(n_full % 2 == buf)(functools.partial(finish, buf))
    a1 = acc_ref[0]
    a2 = acc_ref[1]
    o = a1[:, :dv] / a1[:, dv:] - lam_ref[0] * (a2[:, :dv] / a2[:, dv:])
    o = o * lax.rsqrt(jnp.mean(o * o, axis=-1, keepdims=True) + LN_EPS) * g_ref[...]
    o_ref[...] = o * out_scale


def diff_attention(lam, q, kt, v, subln_g, out_scale, tq=512, kc=1024):
    bsz, heads, _, seq, _ = q.shape
    dv = v.shape[-1] // 2
    tq = min(tq, seq)
    kc = min(kc, seq)
    assert kc % tq == 0 and seq % kc == 0
    kern = functools.partial(_diff_kernel, tq=tq, kc=kc, out_scale=out_scale)
    return pl.pallas_call(
        kern,
        grid=(bsz, heads, seq // tq),
        in_specs=[pl.BlockSpec(memory_space=pltpu.SMEM),
                  pl.BlockSpec((None, None, 2, tq, HEAD_DIM), lambda b, h, i: (b, h, 0, i, 0)),
                  pl.BlockSpec((None, None, 2, HEAD_DIM, seq), lambda b, h, i: (b, h, 0, 0, 0)),
                  pl.BlockSpec((None, None, seq, 2 * dv), lambda b, h, i: (b, h, 0, 0)),
                  pl.BlockSpec((1, dv), lambda b, h, i: (0, 0))],
        out_specs=pl.BlockSpec((None, tq, dv), lambda b, h, i: (b, i, h)),
        out_shape=jax.ShapeDtypeStruct((bsz, seq, heads * dv), f32),
        scratch_shapes=[pltpu.VMEM((2, tq, LANES), f32), pltpu.VMEM((2, 2, tq, LANES), f32),
                        pltpu.VMEM((2, tq, 2 * dv), f32), pltpu.VMEM((2, 2, tq, kc), MXU_DTYPE)],
        compiler_params=_params("parallel", "parallel", "arbitrary"),
        name="diff_attention",
    )(lam, q, kt, v, subln_g.reshape(1, dv))


def _split(h, layout):
    offs = np.cumsum([w for _, w in layout])[:-1].tolist()
    return jnp.split(h, offs, axis=-1)


def _rope_tables(positions):
    inv_freq = ROPE_THETA ** (-jnp.arange(0, ROT_DIM, 2, dtype=f32) / ROT_DIM)
    ang = positions.astype(f32)[..., None] * inv_freq
    return jnp.cos(ang), jnp.sin(ang)


def _apply_rope(x, cos, sin):
    shape = cos.shape[:2] + (1,) * (x.ndim - 3) + cos.shape[-1:]
    c = cos.reshape(shape)
    s = sin.reshape(shape)
    half = ROT_DIM // 2
    x1, x2 = x[..., :half], x[..., half:ROT_DIM]
    return jnp.concatenate([x1 * c - x2 * s, x2 * c + x1 * s, x[..., ROT_DIM:]], axis=-1)


def _pad_cols(w, mult):
    n = w.shape[-1]
    pad = (-n) % mult
    return jnp.pad(w, ((0, 0), (0, pad))) if pad else w


def _overlap_matrix(n_cmp_rows, n_slc):
    c_start = np.arange(n_cmp_rows) * CMP_STRIDE
    s_start = np.arange(LANES) * SLC_LEN
    ovl = (c_start[:, None] < s_start[None, :] + SLC_LEN) & (c_start[:, None] + CMP_LEN > s_start[None, :])
    ovl = ovl & (np.arange(LANES)[None, :] < n_slc)
    return jnp.asarray(ovl.astype(np.float32))


def _ab_mixer(x2, bsz, seq, positions, cos, sin, w_in, pe_k, pe_v, ck1, ck2, cv1, cv2):
    nb = seq // Q_BLOCK
    width = sum(w for _, w in AB_LAYOUT)
    h = project(x2, _pad_cols(w_in, 2 * LANES).astype(MXU_DTYPE))[:, :width].reshape(bsz, seq, width)
    (q_a, k_a, v_a, q_idx, k_idx, w_idx, q_b, k_cmp, v_cmp, k_slc, v_slc, k_win, v_win, g_b) = _split(h, AB_LAYOUT)

    q_a = _apply_rope(q_a.reshape(bsz, seq, A_HEADS, HEAD_DIM), cos, sin) * QK_SCALE
    q_a = q_a.reshape(bsz, nb, Q_BLOCK, A_HEADS, HEAD_DIM).transpose(0, 1, 3, 2, 4)
    q_a = q_a.reshape(bsz, nb, A_HEADS * Q_BLOCK, HEAD_DIM).astype(MXU_DTYPE)
    k_a = _apply_rope(k_a, cos, sin).transpose(0, 2, 1).astype(MXU_DTYPE)
    q_idx = _apply_rope(q_idx.reshape(bsz, seq, IDX_HEADS, IDX_DIM), cos, sin)
    q_idx = q_idx.reshape(bsz, seq, IDX_HEADS * IDX_DIM).astype(MXU_DTYPE)
    k_idx = _apply_rope(k_idx, cos, sin).transpose(0, 2, 1).astype(MXU_DTYPE)
    w_idx = w_idx * (IDX_HEADS * IDX_DIM) ** -0.5
    o_a = dsa_attention(q_idx, w_idx, k_idx, q_a, k_a, _with_ones(v_a.astype(MXU_DTYPE)),
                        topk=min(DSA_TOPK, seq // 4))
    o_a = o_a.reshape(bsz, nb, A_HEADS, Q_BLOCK, HEAD_DIM).transpose(0, 1, 3, 2, 4)
    o_a = o_a.reshape(bsz * seq, A_HEADS * HEAD_DIM)

    groups = B_KV_GROUPS
    n_rows = seq // CMP_STRIDE
    n_cmp = (seq - CMP_LEN) // CMP_STRIDE + 1
    assert n_cmp == n_rows - 1
    n_slc = seq // SLC_LEN
    assert n_slc <= LANES

    def chunked(kv):
        kv = kv.reshape(bsz, n_rows, CMP_STRIDE, groups, HEAD_DIM).transpose(0, 3, 1, 2, 4)
        return kv.reshape(bsz, groups, n_rows, CMP_STRIDE * HEAD_DIM)

    def pe_halves(pe):
        return pe.reshape(2, CMP_STRIDE * HEAD_DIM)

    k_c = nsa_compress(chunked(k_cmp), pe_halves(pe_k), ck1.astype(MXU_DTYPE), ck2.astype(MXU_DTYPE))
    v_c = nsa_compress(chunked(v_cmp), pe_halves(pe_v), cv1.astype(MXU_DTYPE), cv2.astype(MXU_DTYPE))
    cmp_end = jnp.minimum(jnp.arange(n_rows) * CMP_STRIDE + CMP_LEN - 1, seq - 1)
    cos_c, sin_c = _rope_tables(positions[:, cmp_end])
    k_c = _apply_rope(k_c.transpose(0, 2, 1, 3), cos_c, sin_c)
    k_c = k_c.transpose(0, 2, 3, 1).astype(MXU_DTYPE)
    v_c = v_c.astype(MXU_DTYPE)

    def keys_t(k, rope):
        k = k.reshape(bsz, seq, groups, HEAD_DIM)
        if rope:
            k = _apply_rope(k, cos, sin)
        return k.transpose(0, 2, 3, 1).astype(MXU_DTYPE)

    def vals(v):
        return _with_ones(v.reshape(bsz, seq, groups, HEAD_DIM).transpose(0, 2, 1, 3).astype(MXU_DTYPE))

    def head_rows(a, last):
        a = a.reshape(bsz, nb, Q_BLOCK, groups, B_REP, last).transpose(0, 3, 1, 4, 2, 5)
        return a.reshape(bsz, groups, nb, B_REP * Q_BLOCK, last)

    q_b = _apply_rope(q_b.reshape(bsz, seq, B_HEADS, HEAD_DIM), cos, sin) * QK_SCALE
    q_b = head_rows(q_b.reshape(bsz, seq, B_HEADS * HEAD_DIM), HEAD_DIM).astype(MXU_DTYPE)
    o_b = nsa_attention(q_b, head_rows(g_b, 3), k_c, v_c, keys_t(k_slc, True), vals(v_slc),
                        keys_t(k_win, True), vals(v_win), _overlap_matrix(n_rows, n_slc),
                        n_slc=n_slc, n_sel=min(SLC_TOPN, n_slc))
    o_b = o_b.reshape(bsz, groups, nb, B_REP, Q_BLOCK, HEAD_DIM).transpose(0, 2, 4, 1, 3, 5)
    o_b = o_b.reshape(bsz * seq, B_HEADS * HEAD_DIM)
    return jnp.concatenate([o_a, o_b], axis=-1)


def _diff_mixer(x2, bsz, seq, cos, sin, w_in, lq1, lk1, lq2, lk2, subln_g, lam_init):
    h = project(x2, w_in.astype(MXU_DTYPE)).reshape(bsz, seq, -1)
    q, k, v = _split(h, C_LAYOUT)
    q = _apply_rope(q.reshape(bsz, seq, C_HEADS, 2, HEAD_DIM), cos, sin) * QK_SCALE
    k = _apply_rope(k.reshape(bsz, seq, C_HEADS, 2, HEAD_DIM), cos, sin)
    q = q.transpose(0, 2, 3, 1, 4).astype(MXU_DTYPE)
    kt = k.transpose(0, 2, 3, 4, 1).astype(MXU_DTYPE)
    v = v.reshape(bsz, seq, C_HEADS, 2 * HEAD_DIM).transpose(0, 2, 1, 3).astype(MXU_DTYPE)
    lam = (jnp.exp(jnp.sum(lq1 * lk1)) - jnp.exp(jnp.sum(lq2 * lk2)) + lam_init).reshape(1).astype(f32)
    o = diff_attention(lam, q, kt, _with_ones(v), subln_g, 1.0 - lam_init)
    return o.reshape(bsz * seq, C_HEADS * 2 * HEAD_DIM)


def kernel(x, positions, ab_w_in, cmp_pe_k, cmp_pe_v, cmp_k_w1, cmp_k_w2, cmp_v_w1, cmp_v_w2, ab_w_out, ln_ab_g, ln_ab_b, ffn_w1, ffn_w3, ffn_w2, ln_ffn_g, ln_ffn_b, c_w_in, lambda_q1, lambda_k1, lambda_q2, lambda_k2, c_subln_g, c_w_out, ln_c_g, ln_c_b, router_w, moe_w1, moe_w3, moe_w2, ln_moe_g, ln_moe_b):
    bsz, seq, d = x.shape
    assert seq % COUNT_STRIP == 0 and seq >= WINDOW + Q_BLOCK and d == D_MODEL
    cos, sin = _rope_tables(positions)
    x2 = x.reshape(bsz * seq, d)
    for layer in range(DEPTH):
        i = layer // 2
        if layer % 2 == 0:
            o = _ab_mixer(x2, bsz, seq, positions, cos, sin, ab_w_in[i], cmp_pe_k[i], cmp_pe_v[i],
                          cmp_k_w1[i], cmp_k_w2[i], cmp_v_w1[i], cmp_v_w2[i])
            x2 = project_residual_ln(o, ab_w_out[i].astype(MXU_DTYPE), x2, ln_ab_g[i], ln_ab_b[i])
            x2 = ffn_residual_ln(x2, ffn_w1[i].astype(MXU_DTYPE), ffn_w3[i].astype(MXU_DTYPE),
                                 ffn_w2[i].astype(MXU_DTYPE), ln_ffn_g[i], ln_ffn_b[i], tm=512, tf=1408)
        else:
            lam_init = 0.8 - 0.6 * math.exp(-0.3 * layer)
            o = _diff_mixer(x2, bsz, seq, cos, sin, c_w_in[i], lambda_q1[i], lambda_k1[i], lambda_q2[i],
                            lambda_k2[i], c_subln_g[i], lam_init)
            x2 = project_residual_ln(o, c_w_out[i].astype(MXU_DTYPE), x2, ln_c_g[i], ln_c_b[i])
            routes = route_top2(x2, router_w[i])
            x2 = moe_residual_ln(x2, routes, moe_w1[i].astype(MXU_DTYPE), moe_w3[i].astype(MXU_DTYPE),
                                 moe_w2[i].astype(MXU_DTYPE), ln_moe_g[i], ln_moe_b[i], tm=min(2048, bsz * seq // 2), tf=896, rt=128)
    return x2.reshape(bsz, seq, d)
```

```python
import functools
import math

import numpy as np
import jax
import jax.numpy as jnp
from jax import lax
from jax.experimental import pallas as pl
from jax.experimental.pallas import tpu as pltpu

f32 = jnp.float32
i32 = jnp.int32
MXU_DTYPE = jnp.bfloat16
VMEM_LIMIT_BYTES = 56 * 1024 * 1024
LANES = 128

D_MODEL = 1024
DEPTH = 2
HEAD_DIM = 64
ROT_DIM = HEAD_DIM // 4
ROPE_THETA = 500000.0
Q_BLOCK = 128
NEG = -1e30
LN_EPS = 1e-5
A_HEADS = 8
IDX_HEADS = 4
IDX_DIM = 64
DSA_TOPK = 256
B_HEADS = 8
B_KV_GROUPS = 2
B_REP = B_HEADS // B_KV_GROUPS
CMP_LEN = 32
CMP_STRIDE = 16
CMP_HIDDEN = 128
SLC_LEN = 64
SLC_TOPN = 16
WINDOW = 512
FORCED_BOOST = 1e6
C_HEADS = 8
N_EXPERTS = 8
TOP_K = 2
DEEPNORM_ALPHA = (2 * DEPTH) ** 0.25
QK_SCALE = HEAD_DIM ** -0.5 * math.log2(math.e)
INT_MIN = -(2 ** 31)
COUNT_STRIP = 512
ROW_UNROLL = 4

AB_LAYOUT = (
    ("q_a", A_HEADS * HEAD_DIM), ("k_a", HEAD_DIM), ("v_a", HEAD_DIM),
    ("q_idx", IDX_HEADS * IDX_DIM), ("k_idx", IDX_DIM), ("w_idx", IDX_HEADS),
    ("q_b", B_HEADS * HEAD_DIM),
    ("k_cmp", B_KV_GROUPS * HEAD_DIM), ("v_cmp", B_KV_GROUPS * HEAD_DIM),
    ("k_slc", B_KV_GROUPS * HEAD_DIM), ("v_slc", B_KV_GROUPS * HEAD_DIM),
    ("k_win", B_KV_GROUPS * HEAD_DIM), ("v_win", B_KV_GROUPS * HEAD_DIM),
    ("gate_b", 3 * B_HEADS),
)
C_LAYOUT = (("q_c", C_HEADS * 2 * HEAD_DIM), ("k_c", C_HEADS * 2 * HEAD_DIM), ("v_c", C_HEADS * 2 * HEAD_DIM))


def _params(*sem):
    return pltpu.CompilerParams(dimension_semantics=sem, vmem_limit_bytes=VMEM_LIMIT_BYTES)


def _mm(a, b):
    return jnp.dot(a, b, preferred_element_type=f32)


def _layer_norm_rows(y, g, b):
    mu = jnp.mean(y, axis=-1, keepdims=True)
    yc = y - mu
    var = jnp.mean(yc * yc, axis=-1, keepdims=True)
    return yc * lax.rsqrt(var + LN_EPS) * g + b


def _proj_kernel(a_ref, w_ref, o_ref):
    o_ref[...] = _mm(a_ref[...].astype(MXU_DTYPE), w_ref[...]).astype(o_ref.dtype)


def project(a, w, tm=512):
    m, k = a.shape
    n = w.shape[1]
    return pl.pallas_call(
        _proj_kernel,
        grid=(m // tm,),
        in_specs=[pl.BlockSpec((tm, k), lambda i: (i, 0)), pl.BlockSpec((k, n), lambda i: (0, 0))],
        out_specs=pl.BlockSpec((tm, n), lambda i: (i, 0)),
        out_shape=jax.ShapeDtypeStruct((m, n), f32),
        compiler_params=_params("parallel"),
        name="project",
    )(a, w)


def _proj_ln_kernel(a_ref, w_ref, res_ref, g_ref, b_ref, o_ref):
    h = _mm(a_ref[...].astype(MXU_DTYPE), w_ref[...])
    o_ref[...] = _layer_norm_rows(DEEPNORM_ALPHA * res_ref[...] + h, g_ref[...], b_ref[...])


def project_residual_ln(a, w, res, g, b, tm=512):
    m, k = a.shape
    n = w.shape[1]
    return pl.pallas_call(
        _proj_ln_kernel,
        grid=(m // tm,),
        in_specs=[pl.BlockSpec((tm, k), lambda i: (i, 0)), pl.BlockSpec((k, n), lambda i: (0, 0)),
                  pl.BlockSpec((tm, n), lambda i: (i, 0)),
                  pl.BlockSpec((1, n), lambda i: (0, 0)), pl.BlockSpec((1, n), lambda i: (0, 0))],
        out_specs=pl.BlockSpec((tm, n), lambda i: (i, 0)),
        out_shape=jax.ShapeDtypeStruct((m, n), f32),
        compiler_params=_params("parallel"),
        name="project_residual_ln",
    )(a, w, res, g.reshape(1, n), b.reshape(1, n))


def _swiglu_tile(xb, w1_ref, w3_ref, w2_ref):
    a = _mm(xb, w1_ref[...])
    h = (a * jax.nn.sigmoid(a)) * _mm(xb, w3_ref[...])
    return _mm(h.astype(MXU_DTYPE), w2_ref[...])


def _ffn_ln_kernel(x_ref, w1_ref, w3_ref, w2_ref, g_ref, b_ref, o_ref, xb_ref, acc_ref):
    f = pl.program_id(1)

    @pl.when(f == 0)
    def _():
        xb_ref[...] = x_ref[...].astype(MXU_DTYPE)
        acc_ref[...] = jnp.zeros_like(acc_ref)

    acc_ref[...] += _swiglu_tile(xb_ref[...], w1_ref, w3_ref, w2_ref)

    @pl.when(f == pl.num_programs(1) - 1)
    def _():
        o_ref[...] = _layer_norm_rows(DEEPNORM_ALPHA * x_ref[...] + acc_ref[...], g_ref[...], b_ref[...])


def ffn_residual_ln(x, w1, w3, w2, g, b, tm, tf):
    m, d = x.shape
    ff = w1.shape[1]
    return pl.pallas_call(
        _ffn_ln_kernel,
        grid=(m // tm, ff // tf),
        in_specs=[pl.BlockSpec((tm, d), lambda i, f: (i, 0)),
                  pl.BlockSpec((d, tf), lambda i, f: (0, f)),
                  pl.BlockSpec((d, tf), lambda i, f: (0, f)),
                  pl.BlockSpec((tf, d), lambda i, f: (f, 0)),
                  pl.BlockSpec((1, d), lambda i, f: (0, 0)), pl.BlockSpec((1, d), lambda i, f: (0, 0))],
        out_specs=pl.BlockSpec((tm, d), lambda i, f: (i, 0)),
        out_shape=jax.ShapeDtypeStruct((m, d), f32),
        scratch_shapes=[pltpu.VMEM((tm, d), MXU_DTYPE), pltpu.VMEM((tm, d), f32)],
        compiler_params=_params("parallel", "arbitrary"),
        name="ffn_residual_ln",
    )(x, w1, w3, w2, g.reshape(1, d), b.reshape(1, d))


ROUTE_IDS = N_EXPERTS
ROUTE_GATES = N_EXPERTS + 2


def _router_kernel(x_ref, w_ref, o_ref, *, n_experts):
    logits = jnp.dot(x_ref[...], w_ref[...], preferred_element_type=f32, precision=lax.Precision.HIGHEST)
    lane = lax.broadcasted_iota(i32, logits.shape, 1).astype(f32)
    logits = jnp.where(lane < n_experts, logits, -jnp.inf)
    v1 = jnp.max(logits, axis=1, keepdims=True)
    i1 = jnp.min(jnp.where(logits == v1, lane, float(LANES)), axis=1, keepdims=True)
    rest = jnp.where(lane == i1, -jnp.inf, logits)
    v2 = jnp.max(rest, axis=1, keepdims=True)
    i2 = jnp.min(jnp.where(rest == v2, lane, float(LANES)), axis=1, keepdims=True)
    e2 = jnp.exp(v2 - v1)
    g1 = 1.0 / (1.0 + e2)
    g2 = e2 / (1.0 + e2)
    out = jnp.where(lane == ROUTE_IDS, i1, 0.0) + jnp.where(lane == ROUTE_IDS + 1, i2, 0.0)
    out = out + jnp.where(lane == ROUTE_GATES, g1, 0.0) + jnp.where(lane == ROUTE_GATES + 1, g2, 0.0)
    o_ref[...] = out


def route_top2(x, router_w, tm=512):
    m, d = x.shape
    n_experts = router_w.shape[1]
    w = jnp.zeros((d, LANES), f32).at[:, :n_experts].set(router_w)
    return pl.pallas_call(
        functools.partial(_router_kernel, n_experts=n_experts),
        grid=(m // tm,),
        in_specs=[pl.BlockSpec((tm, d), lambda i: (i, 0)), pl.BlockSpec((d, LANES), lambda i: (0, 0))],
        out_specs=pl.BlockSpec((tm, LANES), lambda i: (i, 0)),
        out_shape=jax.ShapeDtypeStruct((m, LANES), f32),
        compiler_params=_params("parallel"),
        name="route_top2",
    )(x, w)


def _moe_ln_kernel(tok_ref, gs_ref, off_ref, x_ref, w1_ref, w3_ref, w2_ref, g_ref, b_ref, o_ref,
                   xg_ref, xb_ref, y_ref, *, rt):
    c = pl.program_id(0)
    e = pl.program_id(1)
    f = pl.program_id(2)
    last_f = pl.num_programs(2) - 1
    start = off_ref[0, e]
    count = off_ref[0, e + 1] - start
    n_tiles = (count + rt - 1) // rt

    @pl.when((c == 0) & (e == 0) & (f == 0))
    def _():
        xg_ref[...] = jnp.zeros_like(xg_ref)

    @pl.when((e == 0) & (f == 0))
    def _():
        o_ref[...] = jnp.zeros_like(o_ref)

    def row_loop(body):
        def group(j, carry):
            for u in range(ROW_UNROLL):
                body(j * ROW_UNROLL + u)
            return carry

        def single(r, carry):
            body(r)
            return carry
        lax.fori_loop(0, count // ROW_UNROLL, group, 0)
        lax.fori_loop((count // ROW_UNROLL) * ROW_UNROLL, count, single, 0)

    @pl.when(f == 0)
    def _():
        def gather(r):
            t = tok_ref[0, start + r]
            xg_ref[pl.ds(r, 1), :] = x_ref[pl.ds(t, 1), :]
        row_loop(gather)

        def cast(j, carry):
            rows = pl.ds(pl.multiple_of(j * rt, rt), rt)
            xb_ref[rows, :] = xg_ref[rows, :].astype(MXU_DTYPE)
            return carry
        lax.fori_loop(0, n_tiles, cast, 0)

    def tile(j, carry):
        rows = pl.ds(pl.multiple_of(j * rt, rt), rt)
        y = _swiglu_tile(xb_ref[rows, :], w1_ref, w3_ref, w2_ref)

        @pl.when(f == 0)
        def _():
            y_ref[rows, :] = y

        @pl.when(f != 0)
        def _():
            y_ref[rows, :] += y
        return carry
    lax.fori_loop(0, n_tiles, tile, 0)

    @pl.when(f == last_f)
    def _():
        def scatter(r):
            t = tok_ref[0, start + r]
            o_ref[pl.ds(t, 1), :] += gs_ref[0, start + r] * y_ref[pl.ds(r, 1), :]
        row_loop(scatter)

    @pl.when((e == pl.num_programs(1) - 1) & (f == last_f))
    def _():
        o_ref[...] = _layer_norm_rows(DEEPNORM_ALPHA * x_ref[...] + o_ref[...], g_ref[...], b_ref[...])


def moe_residual_ln(x, routes, w1, w3, w2, g, b, tm, tf, rt=128):
    m, d = x.shape
    n_experts, _, ff = w1.shape
    n_chunks = m // tm
    ids = routes[:, ROUTE_IDS:ROUTE_IDS + TOP_K].astype(i32).reshape(n_chunks, tm * TOP_K)
    gts = routes[:, ROUTE_GATES:ROUTE_GATES + TOP_K].reshape(n_chunks, tm * TOP_K)
    order = jnp.argsort(ids, axis=1, stable=True).astype(i32)
    tok = order // TOP_K
    gs = jnp.take_along_axis(gts, order, axis=1)
    counts = jnp.sum(ids[:, :, None] == jnp.arange(n_experts, dtype=i32)[None, None, :], axis=1, dtype=i32)
    offs = jnp.concatenate([jnp.zeros((n_chunks, 1), i32), jnp.cumsum(counts, axis=1, dtype=i32)], axis=1)
    smem = lambda width: pl.BlockSpec((None, 1, width), lambda c, e, f: (c, 0, 0), memory_space=pltpu.SMEM)
    return pl.pallas_call(
        functools.partial(_moe_ln_kernel, rt=rt),
        grid=(n_chunks, n_experts, ff // tf),
        in_specs=[smem(tm * TOP_K), smem(tm * TOP_K), smem(n_experts + 1),
                  pl.BlockSpec((tm, d), lambda c, e, f: (c, 0), pipeline_mode=pl.Buffered(1)),
                  pl.BlockSpec((None, d, tf), lambda c, e, f: (e, 0, f)),
                  pl.BlockSpec((None, d, tf), lambda c, e, f: (e, 0, f)),
                  pl.BlockSpec((None, tf, d), lambda c, e, f: (e, f, 0)),
                  pl.BlockSpec((1, d), lambda c, e, f: (0, 0)), pl.BlockSpec((1, d), lambda c, e, f: (0, 0))],
        out_specs=pl.BlockSpec((tm, d), lambda c, e, f: (c, 0), pipeline_mode=pl.Buffered(1)),
        out_shape=jax.ShapeDtypeStruct((m, d), f32),
        scratch_shapes=[pltpu.VMEM((tm, d), f32), pltpu.VMEM((tm, d), MXU_DTYPE), pltpu.VMEM((tm, d), f32)],
        compiler_params=_params("arbitrary", "arbitrary", "arbitrary"),
        name="moe_residual_ln",
    )(tok[:, None, :], gs[:, None, :], offs[:, None, :], x, w1, w3, w2, g.reshape(1, d), b.reshape(1, d))


def _flash_reset(m_ref, acc_ref):
    m_ref[...] = jnp.full(m_ref.shape, NEG, f32)
    acc_ref[...] = jnp.zeros(acc_ref.shape, f32)


def _flash_rows(q, kt, bias, m_ref, alpha_ref, p_ref, rows):
    s = _mm(q, kt)
    if bias is not None:
        s = s + bias
    m_prev = m_ref[rows, :]
    m_next = jnp.maximum(m_prev, jnp.max(s, axis=1, keepdims=True))
    alpha_ref[rows, :] = jnp.exp2(m_prev - m_next)
    m_ref[rows, :] = m_next
    p_ref[rows, :] = jnp.exp2(s - _lane_tile(m_next, s.shape[1] // LANES)).astype(p_ref.dtype)


def _flash_accumulate(v_aug, alpha_ref, p_ref, acc_ref):
    alpha = _lane_tile(alpha_ref[...], acc_ref.shape[-1] // LANES)
    acc_ref[...] = alpha * acc_ref[...] + _mm(p_ref[...], v_aug)


def _skewed_chunks(n, score, fold, carry):
    def pair(j, carry):
        c = 2 * j
        fold(c - 1, 1)
        carry = score(c, 0, carry)
        fold(c, 0)
        return score(c + 1, 1, carry)

    def single(c, carry):
        fold(c - 1, 1)
        return score(c, 0, carry)

    carry = lax.fori_loop(0, n // 2, pair, carry)
    return lax.fori_loop(2 * (n // 2), n, single, carry)


def _fold_last(n, fold):
    for buf in range(2):
        pl.when((n - 1) % 2 == buf)(functools.partial(fold, n - 1, buf))


def _lane_tile(x, reps):
    return x if reps == 1 else jnp.concatenate([x] * reps, axis=1)


def _with_ones(v):
    return jnp.concatenate([v, jnp.ones_like(v)], axis=-1)


def _row_slices(n_rows):
    return [slice(r * Q_BLOCK, (r + 1) * Q_BLOCK) for r in range(n_rows // Q_BLOCK)]


def _dsa_kernel(qi_ref, wi_ref, kit_ref, qa_ref, kat_ref, va_ref, o_ref,
                skey_ref, wrep_ref, m_ref, alpha_ref, acc_ref, p_ref, qh_ref, *, topk, kc):
    qb = Q_BLOCK
    i = pl.program_id(1)
    q0 = i * qb
    n_chunks = (q0 + qb + kc - 1) // kc
    t_row = q0 + lax.broadcasted_iota(i32, (qb, kc), 0)
    col = lax.broadcasted_iota(i32, (qb, kc), 1)
    wi = wi_ref[...]
    qi = qi_ref[...]
    qis = [qi[:, h * IDX_DIM:(h + 1) * IDX_DIM] for h in range(IDX_HEADS)]
    qa = qa_ref[...]
    for h in range(A_HEADS):
        qh_ref[h * qb:(h + 1) * qb, :] = qa[:, h * HEAD_DIM:(h + 1) * HEAD_DIM]
    for h in range(IDX_HEADS):
        wrep_ref[h] = jnp.broadcast_to(wi[:, h:h + 1], (qb, kc))

    def score_body(c, carry):
        off = pl.multiple_of(c * kc, kc)
        kt = kit_ref[:, pl.ds(off, kc)]
        s = jnp.zeros((qb, kc), f32)
        for h in range(IDX_HEADS):
            s = s + wrep_ref[h] * jnp.maximum(_mm(qis[h], kt), 0.0)
        s = jnp.where(col + off <= t_row, s, NEG)
        bits = pltpu.bitcast(s, i32)
        skey_ref[:, pl.ds(off, kc)] = jnp.where(bits < 0, bits ^ 0x7FFFFFFF, bits)
        return carry

    n_strips = (q0 + qb + COUNT_STRIP - 1) // COUNT_STRIP
    lax.fori_loop(0, n_strips * (COUNT_STRIP // kc), score_body, 0)

    def count_ge(cand):
        def body(c, acc):
            off = pl.multiple_of(c * COUNT_STRIP, COUNT_STRIP)
            for j in range(COUNT_STRIP // LANES):
                acc = acc + jnp.where(skey_ref[:, pl.ds(off + j * LANES, LANES)] >= cand, 1.0, 0.0)
            return acc
        acc = lax.fori_loop(0, n_strips, body, jnp.zeros((qb, LANES), f32))
        return jnp.sum(acc, axis=1, keepdims=True)

    thr = jnp.where(count_ge(jnp.zeros((qb, 1), i32)) >= topk, 0, INT_MIN).astype(i32)

    def bisect(b, thr):
        cand = thr + jnp.left_shift(jnp.int32(1), 30 - b)
        return jnp.where(count_ge(cand) >= topk, cand, thr)

    thr = lax.fori_loop(0, 31, bisect, thr)
    quota = topk - count_ge(thr + 1)

    before = (lax.broadcasted_iota(i32, (kc, kc), 0) < lax.broadcasted_iota(i32, (kc, kc), 1))
    before = jnp.where(before, 1.0, 0.0).astype(MXU_DTYPE)
    _flash_reset(m_ref, acc_ref)
    alpha_ref[...] = jnp.zeros(alpha_ref.shape, f32)
    p_ref[...] = jnp.zeros(p_ref.shape, p_ref.dtype)
    heads = _row_slices(A_HEADS * qb)

    def fold(c, buf):
        off = pl.multiple_of(jnp.maximum(c, 0) * kc, kc)
        _flash_accumulate(va_ref[pl.ds(off, kc), :], alpha_ref.at[buf], p_ref.at[buf], acc_ref)

    def score(c, buf, ties_seen):
        off = pl.multiple_of(c * kc, kc)
        key = skey_ref[:, pl.ds(off, kc)]
        eq = key == thr
        eqf = jnp.where(eq, 1.0, 0.0)
        rank = ties_seen + _mm(eqf.astype(MXU_DTYPE), before)
        sel = ((key > thr) | (eq & (rank < quota))) & (col + off <= t_row)
        bias = jnp.where(sel, 0.0, NEG)
        kt = kat_ref[:, pl.ds(off, kc)]
        for rows in heads:
            _flash_rows(qh_ref[rows, :], kt, bias, m_ref, alpha_ref.at[buf], p_ref.at[buf], rows)
        return ties_seen + jnp.sum(eqf, axis=1, keepdims=True)

    _skewed_chunks(n_chunks, score, fold, jnp.zeros((qb, 1), f32))
    _fold_last(n_chunks, fold)
    acc = acc_ref[...]
    o = acc[:, :HEAD_DIM] / acc[:, HEAD_DIM:]
    for h in range(A_HEADS):
        o_ref[:, h * HEAD_DIM:(h + 1) * HEAD_DIM] = o[h * qb:(h + 1) * qb, :]


QPK_QA, QPK_QIDX, QPK_QB = 0, 2, 3
KPK_KA, KPK_KIDX, KPK_KSLC, KPK_KWIN = 0, 1, 2, 4
VPK_VA, VPK_VSLC, VPK_VWIN = 0, 1, 3
OUT_NSA = 2


def dsa_attention(wi, qpk, kpk, vpk, topk, kc=512):
    bsz, seq, _ = qpk.shape
    nb = seq // Q_BLOCK
    rows = A_HEADS * Q_BLOCK
    width = A_HEADS * HEAD_DIM
    kern = functools.partial(_dsa_kernel, topk=topk, kc=kc)
    return pl.pallas_call(
        kern,
        grid=(bsz, nb),
        in_specs=[pl.BlockSpec((None, Q_BLOCK, IDX_HEADS * IDX_DIM), lambda b, i: (b, i, QPK_QIDX)),
                  pl.BlockSpec((None, Q_BLOCK, IDX_HEADS), lambda b, i: (b, i, 0)),
                  pl.BlockSpec((None, IDX_DIM, seq), lambda b, i: (b, KPK_KIDX, 0)),
                  pl.BlockSpec((None, Q_BLOCK, width), lambda b, i: (b, i, QPK_QA)),
                  pl.BlockSpec((None, HEAD_DIM, seq), lambda b, i: (b, KPK_KA, 0)),
                  pl.BlockSpec((None, seq, 2 * HEAD_DIM), lambda b, i: (b, 0, VPK_VA))],
        out_specs=pl.BlockSpec((None, Q_BLOCK, width), lambda b, i: (b, i, 0)),
        out_shape=jax.ShapeDtypeStruct((bsz, seq, 2 * width), f32),
        scratch_shapes=[pltpu.VMEM((Q_BLOCK, seq), i32), pltpu.VMEM((IDX_HEADS, Q_BLOCK, kc), f32),
                        pltpu.VMEM((rows, LANES), f32), pltpu.VMEM((2, rows, LANES), f32),
                        pltpu.VMEM((rows, 2 * HEAD_DIM), f32), pltpu.VMEM((2, rows, kc), MXU_DTYPE),
                        pltpu.VMEM((rows, HEAD_DIM), MXU_DTYPE)],
        compiler_params=_params("parallel", "arbitrary"),
        name="dsa_attention",
    )(qpk, wi, kpk, qpk, kpk, vpk)


def _compress_kernel(x_ref, pe_ref, w1_ref, w2_ref, o_ref):
    half = (CMP_LEN // 2) * HEAD_DIM
    x = x_ref[...]
    first = _mm((x + pe_ref[0:1, :]).astype(MXU_DTYPE), w1_ref[0:half, :])
    second = _mm((x + pe_ref[1:2, :]).astype(MXU_DTYPE), w1_ref[half:2 * half, :])
    pre = first + pltpu.roll(second, shift=x.shape[0] - 1, axis=0)
    hid = pre * jax.nn.sigmoid(pre)
    o_ref[...] = _mm(hid.astype(MXU_DTYPE), w2_ref[...])


def nsa_compress(chunks, pe2, w1, w2):
    bsz, groups, nck, width = chunks.shape
    return pl.pallas_call(
        _compress_kernel,
        grid=(bsz, groups),
        in_specs=[pl.BlockSpec((None, None, nck, width), lambda b, g: (b, g, 0, 0)),
                  pl.BlockSpec((2, width), lambda b, g: (0, 0)),
                  pl.BlockSpec((2 * width, CMP_HIDDEN), lambda b, g: (0, 0)),
                  pl.BlockSpec((CMP_HIDDEN, HEAD_DIM), lambda b, g: (0, 0))],
        out_specs=pl.BlockSpec((None, None, nck, HEAD_DIM), lambda b, g: (b, g, 0, 0)),
        out_shape=jax.ShapeDtypeStruct((bsz, groups, nck, HEAD_DIM), f32),
        compiler_params=_params("parallel", "parallel"),
        name="nsa_compress",
    )(chunks, pe2, w1, w2)


def _nsa_kernel(qin_ref, graw_ref, kct_ref, vc_ref, kst_ref, vs_ref, kwt_ref, vw_ref, ovl_ref, _, o_ref,
                m_ref, alpha_ref, acc_ref, p_ref, pw_ref, q_ref, *, n_slc, n_sel, kc):
    qb = Q_BLOCK
    i = pl.program_id(2)
    q0 = i * qb
    n_cmp = kct_ref.shape[1]
    heads = _row_slices(B_REP * qb)
    qin = qin_ref[...]
    for r in range(B_REP):
        q_ref[r * qb:(r + 1) * qb, :] = qin[:, r * HEAD_DIM:(r + 1) * HEAD_DIM]

    t_c = q0 + lax.broadcasted_iota(i32, (qb, n_cmp), 0)
    cmp_end = lax.broadcasted_iota(i32, (qb, n_cmp), 1) * CMP_STRIDE + (CMP_LEN - 1)
    vis = cmp_end <= t_c
    kct = kct_ref[...]
    p_sum = jnp.zeros((qb, n_cmp), f32)
    for rows in heads:
        lc = jnp.where(vis, _mm(q_ref[rows, :], kct), NEG)
        ec = jnp.where(vis, jnp.exp2(lc - jnp.max(lc, axis=1, keepdims=True)), 0.0)
        den = jnp.sum(ec, axis=1, keepdims=True)
        p_c = ec / jnp.where(den > 0.0, den, 1.0)
        p_sum = p_sum + p_c
        acc_ref[rows, :HEAD_DIM] = _mm(p_c.astype(MXU_DTYPE), vc_ref[...])
    o_c = acc_ref[:, :HEAD_DIM]

    imp = jnp.dot(p_sum, ovl_ref[...], preferred_element_type=f32, precision=lax.Precision.HIGHEST).T
    t_q = q0 + lax.broadcasted_iota(i32, (LANES, qb), 1)
    blk = lax.broadcasted_iota(i32, (LANES, qb), 0)
    blk_t = t_q // SLC_LEN
    forced = (blk == 0) | (blk == blk_t) | (blk == blk_t - 1)
    imp = jnp.where(forced, FORCED_BOOST, imp)
    imp = jnp.where(blk * SLC_LEN <= t_q, imp, NEG)
    imp = jnp.where(blk < n_slc, imp, -jnp.inf)
    blk_f = blk.astype(f32)

    def pick(_, carry):
        imp, selm = carry
        best = jnp.max(imp, axis=0, keepdims=True)
        first = jnp.min(jnp.where(imp == best, blk_f, float(LANES)), axis=0, keepdims=True)
        hit = blk_f == first
        return jnp.where(hit, -jnp.inf, imp), jnp.where(hit, 1.0, selm)

    _, selm = lax.fori_loop(0, n_sel, pick, (imp, jnp.zeros((LANES, qb), f32)))
    selm = selm.T.astype(MXU_DTYPE)

    t_k = q0 + lax.broadcasted_iota(i32, (qb, kc), 0)
    col = lax.broadcasted_iota(i32, (qb, kc), 1)
    exp_row = lax.broadcasted_iota(i32, (LANES, kc), 0)
    exp_col = lax.broadcasted_iota(i32, (LANES, kc), 1)
    _flash_reset(m_ref, acc_ref)
    alpha_ref[...] = jnp.zeros(alpha_ref.shape, f32)
    p_ref[...] = jnp.zeros(p_ref.shape, p_ref.dtype)

    def fold(c, buf):
        off = pl.multiple_of(jnp.maximum(c, 0) * kc, kc)
        _flash_accumulate(vs_ref[pl.ds(off, kc), :], alpha_ref.at[buf], p_ref.at[buf], acc_ref)

    def score(c, buf, carry):
        off = pl.multiple_of(c * kc, kc)
        expand = jnp.where(exp_row == (exp_col + off) // SLC_LEN, 1.0, 0.0).astype(MXU_DTYPE)
        sel = (_mm(selm, expand) > 0.5) & (col + off <= t_k)
        bias = jnp.where(sel, 0.0, NEG)
        kt = kst_ref[:, pl.ds(off, kc)]
        for rows in heads:
            _flash_rows(q_ref[rows, :], kt, bias, m_ref, alpha_ref.at[buf], p_ref.at[buf], rows)
        return carry

    n_slc_chunks = (q0 + qb + kc - 1) // kc
    _skewed_chunks(n_slc_chunks, score, fold, 0)
    _fold_last(n_slc_chunks, fold)
    acc = acc_ref[...]
    o_s = acc[:, :HEAD_DIM] / acc[:, HEAD_DIM:]

    slab = WINDOW + qb
    w0 = pl.multiple_of(jnp.maximum(q0 - WINDOW, 0), qb)
    dist = (q0 + lax.broadcasted_iota(i32, (qb, slab), 0)) - (w0 + lax.broadcasted_iota(i32, (qb, slab), 1))
    bias = jnp.where((dist >= 0) & (dist < WINDOW), 0.0, NEG)
    kt = kwt_ref[:, pl.ds(w0, slab)]
    for rows in heads:
        s = _mm(q_ref[rows, :], kt) + bias
        pw_ref[rows, :] = jnp.exp2(s - jnp.max(s, axis=1, keepdims=True)).astype(pw_ref.dtype)
    acc = _mm(pw_ref[...], vw_ref[pl.ds(w0, slab), :])
    o_w = acc[:, :HEAD_DIM] / acc[:, HEAD_DIM:]

    gates = jax.nn.sigmoid(graw_ref[...])
    for r, rows in enumerate(heads):
        g_c, g_s, g_w = (gates[:, 3 * r + n:3 * r + n + 1] for n in range(3))
        o_ref[:, r * HEAD_DIM:(r + 1) * HEAD_DIM] = g_c * o_c[rows] + g_s * o_s[rows] + g_w * o_w[rows]


def nsa_attention(qpk, graw, kct, vc, kpk, vpk, ovl, out_ab, n_slc, n_sel, kc=512):
    bsz, seq, _ = qpk.shape
    groups = B_KV_GROUPS
    nb = seq // Q_BLOCK
    rows = B_REP * Q_BLOCK
    width = B_REP * HEAD_DIM
    n_cmp = kct.shape[-1]
    kern = functools.partial(_nsa_kernel, n_slc=n_slc, n_sel=n_sel, kc=kc)
    per_bg = lambda b, g, i: (b, g, 0, 0)
    return pl.pallas_call(
        kern,
        grid=(bsz, groups, nb),
        in_specs=[pl.BlockSpec((None, Q_BLOCK, width), lambda b, g, i: (b, i, QPK_QB + g)),
                  pl.BlockSpec((None, None, Q_BLOCK, B_REP * 3), lambda b, g, i: (b, g, i, 0)),
                  pl.BlockSpec((None, None, HEAD_DIM, n_cmp), per_bg),
                  pl.BlockSpec((None, None, n_cmp, HEAD_DIM), per_bg),
                  pl.BlockSpec((None, HEAD_DIM, seq), lambda b, g, i: (b, KPK_KSLC + g, 0)),
                  pl.BlockSpec((None, seq, 2 * HEAD_DIM), lambda b, g, i: (b, 0, VPK_VSLC + g)),
                  pl.BlockSpec((None, HEAD_DIM, seq), lambda b, g, i: (b, KPK_KWIN + g, 0)),
                  pl.BlockSpec((None, seq, 2 * HEAD_DIM), lambda b, g, i: (b, 0, VPK_VWIN + g)),
                  pl.BlockSpec((n_cmp, LANES), lambda b, g, i: (0, 0)),
                  pl.BlockSpec(memory_space=pl.ANY)],
        out_specs=pl.BlockSpec((None, Q_BLOCK, width), lambda b, g, i: (b, i, OUT_NSA + g)),
        out_shape=jax.ShapeDtypeStruct(out_ab.shape, f32),
        input_output_aliases={9: 0},
        scratch_shapes=[pltpu.VMEM((rows, LANES), f32), pltpu.VMEM((2, rows, LANES), f32),
                        pltpu.VMEM((rows, 2 * HEAD_DIM), f32), pltpu.VMEM((2, rows, kc), MXU_DTYPE),
                        pltpu.VMEM((rows, WINDOW + Q_BLOCK), MXU_DTYPE), pltpu.VMEM((rows, HEAD_DIM), MXU_DTYPE)],
        compiler_params=_params("parallel", "parallel", "arbitrary"),
        name="nsa_attention",
    )(qpk, graw, kct, vc, kpk, vpk, kpk, vpk, ovl, out_ab)


def _diff_kernel(lam_ref, qin_ref, kt_ref, v_ref, g_ref, o_ref, m_ref, alpha_ref, acc_ref, p_ref, q_ref,
                 *, tq, kc, out_scale):
    i = pl.program_id(2)
    q0 = i * tq
    dv = v_ref.shape[-1] // 2
    groups = _row_slices(tq)
    qin = qin_ref[...]
    for half in range(2):
        q_ref[half] = qin[:, half * HEAD_DIM:(half + 1) * HEAD_DIM]
    col = lax.broadcasted_iota(i32, (Q_BLOCK, kc), 1)
    t_row = [q0 + r * Q_BLOCK + lax.broadcasted_iota(i32, (Q_BLOCK, kc), 0) for r in range(len(groups))]
    for half in range(2):
        _flash_reset(m_ref.at[half], acc_ref.at[half])
    alpha_ref[...] = jnp.zeros(alpha_ref.shape, f32)
    p_ref[...] = jnp.zeros(p_ref.shape, p_ref.dtype)

    def fold(c, buf):
        off = pl.multiple_of(jnp.maximum(c, 0) * kc, kc)
        v = v_ref[pl.ds(off, kc), :]
        for half in range(2):
            _flash_accumulate(v, alpha_ref.at[buf, half], p_ref.at[buf, half], acc_ref.at[half])

    def score(c, buf, masked):
        off = pl.multiple_of(c * kc, kc)
        for half in range(2):
            kt = kt_ref[half * HEAD_DIM:(half + 1) * HEAD_DIM, pl.ds(off, kc)]
            for r, rows in enumerate(groups):
                bias = jnp.where(col + off <= t_row[r], 0.0, NEG) if masked else None
                _flash_rows(q_ref[half, rows, :], kt, bias, m_ref.at[half], alpha_ref.at[buf, half],
                            p_ref.at[buf, half], rows)

    def score_full(c, buf, carry):
        score(c, buf, False)
        return carry

    n_full = q0 // kc
    _skewed_chunks(n_full, score_full, fold, 0)

    def finish(buf):
        fold(n_full - 1, 1 - buf)
        score(n_full, buf, True)
        fold(n_full, buf)

    for buf in range(2):
        pl.when(n_full % 2 == buf)(functools.partial(finish, buf))
    a1 = acc_ref[0]
    a2 = acc_ref[1]
    o = a1[:, :dv] / a1[:, dv:] - lam_ref[0] * (a2[:, :dv] / a2[:, dv:])
    o = o * lax.rsqrt(jnp.mean(o * o, axis=-1, keepdims=True) + LN_EPS) * g_ref[...]
    o_ref[...] = o * out_scale


def diff_attention(lam, q, kt, v, subln_g, out_scale, tq=512, kc=1024):
    bsz, seq, width = q.shape
    dv = 2 * HEAD_DIM
    heads = width // dv
    tq = min(tq, seq)
    kc = min(kc, seq)
    assert kc % tq == 0 and seq % kc == 0
    kern = functools.partial(_diff_kernel, tq=tq, kc=kc, out_scale=out_scale)
    return pl.pallas_call(
        kern,
        grid=(bsz, heads, seq // tq),
        in_specs=[pl.BlockSpec(memory_space=pltpu.SMEM),
                  pl.BlockSpec((None, tq, dv), lambda b, h, i: (b, i, h)),
                  pl.BlockSpec((None, dv, seq), lambda b, h, i: (b, h, 0)),
                  pl.BlockSpec((None, seq, 2 * dv), lambda b, h, i: (b, 0, h)),
                  pl.BlockSpec((1, dv), lambda b, h, i: (0, 0))],
        out_specs=pl.BlockSpec((None, tq, dv), lambda b, h, i: (b, i, h)),
        out_shape=jax.ShapeDtypeStruct((bsz, seq, heads * dv), f32),
        scratch_shapes=[pltpu.VMEM((2, tq, LANES), f32), pltpu.VMEM((2, 2, tq, LANES), f32),
                        pltpu.VMEM((2, tq, 2 * dv), f32), pltpu.VMEM((2, 2, tq, kc), MXU_DTYPE),
                        pltpu.VMEM((2, tq, HEAD_DIM), MXU_DTYPE)],
        compiler_params=_params("parallel", "parallel", "arbitrary"),
        name="diff_attention",
    )(lam, q, kt, v, subln_g.reshape(1, dv))


def _split(h, layout):
    offs = np.cumsum([w for _, w in layout])[:-1].tolist()
    return jnp.split(h, offs, axis=-1)


def _rope_tables(positions):
    inv_freq = ROPE_THETA ** (-jnp.arange(0, ROT_DIM, 2, dtype=f32) / ROT_DIM)
    ang = positions.astype(f32)[..., None] * inv_freq
    return jnp.cos(ang), jnp.sin(ang)


def _apply_rope(x, cos, sin):
    shape = cos.shape[:2] + (1,) * (x.ndim - 3) + cos.shape[-1:]
    c = cos.reshape(shape)
    s = sin.reshape(shape)
    half = ROT_DIM // 2
    x1, x2 = x[..., :half], x[..., half:ROT_DIM]
    return jnp.concatenate([x1 * c - x2 * s, x2 * c + x1 * s, x[..., ROT_DIM:]], axis=-1)


def _pad_cols(w, mult):
    n = w.shape[-1]
    pad = (-n) % mult
    return jnp.pad(w, ((0, 0), (0, pad))) if pad else w


def _overlap_matrix(n_cmp_rows, n_slc):
    c_start = np.arange(n_cmp_rows) * CMP_STRIDE
    s_start = np.arange(LANES) * SLC_LEN
    ovl = (c_start[:, None] < s_start[None, :] + SLC_LEN) & (c_start[:, None] + CMP_LEN > s_start[None, :])
    ovl = ovl & (np.arange(LANES)[None, :] < n_slc)
    return jnp.asarray(ovl.astype(np.float32))


def _ab_mixer(x2, bsz, seq, positions, cos, sin, w_in, pe_k, pe_v, ck1, ck2, cv1, cv2):
    nb = seq // Q_BLOCK
    width = sum(w for _, w in AB_LAYOUT)
    h = project(x2, _pad_cols(w_in, 2 * LANES).astype(MXU_DTYPE))[:, :width].reshape(bsz, seq, width)
    (q_a, k_a, v_a, q_idx, k_idx, w_idx, q_b, k_cmp, v_cmp, k_slc, v_slc, k_win, v_win, g_b) = _split(h, AB_LAYOUT)

    def heads_rope(a, scale):
        a = _apply_rope(a.reshape(bsz, seq, -1, HEAD_DIM), cos, sin)
        return (a * scale if scale is not None else a).reshape(bsz, seq, -1)

    qpk = jnp.concatenate([heads_rope(q_a, QK_SCALE), heads_rope(q_idx, None), heads_rope(q_b, QK_SCALE)],
                          axis=-1).astype(MXU_DTYPE)
    kpk = heads_rope(jnp.concatenate([k_a, k_idx, k_slc, k_win], axis=-1), None)
    kpk = kpk.transpose(0, 2, 1).astype(MXU_DTYPE)
    vpk = jnp.concatenate([v_a, v_slc, v_win], axis=-1).reshape(bsz, seq, -1, HEAD_DIM)
    vpk = _with_ones(vpk.astype(MXU_DTYPE)).reshape(bsz, seq, -1)
    w_idx = w_idx * (IDX_HEADS * IDX_DIM) ** -0.5

    out_ab = dsa_attention(w_idx, qpk, kpk, vpk, topk=min(DSA_TOPK, seq // 4))

    groups = B_KV_GROUPS
    n_rows = seq // CMP_STRIDE
    n_cmp = (seq - CMP_LEN) // CMP_STRIDE + 1
    assert n_cmp == n_rows - 1
    n_slc = seq // SLC_LEN
    assert n_slc <= LANES

    def chunked(kv):
        kv = kv.reshape(bsz, n_rows, CMP_STRIDE, groups, HEAD_DIM).transpose(0, 3, 1, 2, 4)
        return kv.reshape(bsz, groups, n_rows, CMP_STRIDE * HEAD_DIM)

    def pe_halves(pe):
        return pe.reshape(2, CMP_STRIDE * HEAD_DIM)

    k_c = nsa_compress(chunked(k_cmp), pe_halves(pe_k), ck1.astype(MXU_DTYPE), ck2.astype(MXU_DTYPE))
    v_c = nsa_compress(chunked(v_cmp), pe_halves(pe_v), cv1.astype(MXU_DTYPE), cv2.astype(MXU_DTYPE))
    cmp_end = jnp.minimum(jnp.arange(n_rows) * CMP_STRIDE + CMP_LEN - 1, seq - 1)
    cos_c, sin_c = _rope_tables(positions[:, cmp_end])
    k_c = _apply_rope(k_c.transpose(0, 2, 1, 3), cos_c, sin_c)
    k_c = k_c.transpose(0, 2, 3, 1).astype(MXU_DTYPE)
    v_c = v_c.astype(MXU_DTYPE)

    graw = g_b.reshape(bsz, seq, groups, B_REP * 3).transpose(0, 2, 1, 3)
    out_ab = nsa_attention(qpk, graw, k_c, v_c, kpk, vpk, _overlap_matrix(n_rows, n_slc), out_ab,
                           n_slc=n_slc, n_sel=min(SLC_TOPN, n_slc))
    return out_ab.reshape(bsz * seq, (A_HEADS + B_HEADS) * HEAD_DIM)


def _diff_mixer(x2, bsz, seq, cos, sin, w_in, lq1, lk1, lq2, lk2, subln_g, lam_init):
    h = project(x2, w_in.astype(MXU_DTYPE)).reshape(bsz, seq, -1)
    q, k, v = _split(h, C_LAYOUT)

    def halves_rope(a):
        return _apply_rope(a.reshape(bsz, seq, 2 * C_HEADS, HEAD_DIM), cos, sin).reshape(bsz, seq, -1)

    q = (halves_rope(q) * QK_SCALE).astype(MXU_DTYPE)
    kt = halves_rope(k).transpose(0, 2, 1).astype(MXU_DTYPE)
    v = _with_ones(v.reshape(bsz, seq, C_HEADS, 2 * HEAD_DIM).astype(MXU_DTYPE)).reshape(bsz, seq, -1)
    lam = (jnp.exp(jnp.sum(lq1 * lk1)) - jnp.exp(jnp.sum(lq2 * lk2)) + lam_init).reshape(1).astype(f32)
    o = diff_attention(lam, q, kt, v, subln_g, 1.0 - lam_init)
    return o.reshape(bsz * seq, C_HEADS * 2 * HEAD_DIM)


def kernel(x, positions, ab_w_in, cmp_pe_k, cmp_pe_v, cmp_k_w1, cmp_k_w2, cmp_v_w1, cmp_v_w2, ab_w_out, ln_ab_g, ln_ab_b, ffn_w1, ffn_w3, ffn_w2, ln_ffn_g, ln_ffn_b, c_w_in, lambda_q1, lambda_k1, lambda_q2, lambda_k2, c_subln_g, c_w_out, ln_c_g, ln_c_b, router_w, moe_w1, moe_w3, moe_w2, ln_moe_g, ln_moe_b):
    bsz, seq, d = x.shape
    assert seq % COUNT_STRIP == 0 and seq >= WINDOW + Q_BLOCK and d == D_MODEL
    cos, sin = _rope_tables(positions)
    x2 = x.reshape(bsz * seq, d)
    for layer in range(DEPTH):
        i = layer // 2
        if layer % 2 == 0:
            o = _ab_mixer(x2, bsz, seq, positions, cos, sin, ab_w_in[i], cmp_pe_k[i], cmp_pe_v[i],
                          cmp_k_w1[i], cmp_k_w2[i], cmp_v_w1[i], cmp_v_w2[i])
            x2 = project_residual_ln(o, ab_w_out[i].astype(MXU_DTYPE), x2, ln_ab_g[i], ln_ab_b[i])
            x2 = ffn_residual_ln(x2, ffn_w1[i].astype(MXU_DTYPE), ffn_w3[i].astype(MXU_DTYPE),
                                 ffn_w2[i].astype(MXU_DTYPE), ln_ffn_g[i], ln_ffn_b[i], tm=512, tf=1408)
        else:
            lam_init = 0.8 - 0.6 * math.exp(-0.3 * layer)
            o = _diff_mixer(x2, bsz, seq, cos, sin, c_w_in[i], lambda_q1[i], lambda_k1[i], lambda_q2[i],
                            lambda_k2[i], c_subln_g[i], lam_init)
            x2 = project_residual_ln(o, c_w_out[i].astype(MXU_DTYPE), x2, ln_c_g[i], ln_c_b[i])
            routes = route_top2(x2, router_w[i])
            x2 = moe_residual_ln(x2, routes, moe_w1[i].astype(MXU_DTYPE), moe_w3[i].astype(MXU_DTYPE),
                                 moe_w2[i].astype(MXU_DTYPE), ln_moe_g[i], ln_moe_b[i], tm=min(2048, bsz * seq // 2), tf=896, rt=128)
    return x2.reshape(bsz, seq, d)
```

```python
import functools
import math

import numpy as np
import jax
import jax.numpy as jnp
from jax import lax
from jax.experimental import pallas as pl
from jax.experimental.pallas import tpu as pltpu

f32 = jnp.float32
i32 = jnp.int32
MXU_DTYPE = jnp.bfloat16
VMEM_LIMIT_BYTES = 56 * 1024 * 1024
LANES = 128

D_MODEL = 1024
DEPTH = 2
HEAD_DIM = 64
ROT_DIM = HEAD_DIM // 4
ROPE_THETA = 500000.0
Q_BLOCK = 128
NEG = -1e30
LN_EPS = 1e-5
A_HEADS = 8
IDX_HEADS = 4
IDX_DIM = 64
DSA_TOPK = 256
B_HEADS = 8
B_KV_GROUPS = 2
B_REP = B_HEADS // B_KV_GROUPS
CMP_LEN = 32
CMP_STRIDE = 16
CMP_HIDDEN = 128
SLC_LEN = 64
SLC_TOPN = 16
WINDOW = 512
FORCED_BOOST = 1e6
C_HEADS = 8
N_EXPERTS = 8
TOP_K = 2
DEEPNORM_ALPHA = (2 * DEPTH) ** 0.25
QK_SCALE = HEAD_DIM ** -0.5 * math.log2(math.e)
INT_MIN = -(2 ** 31)
COUNT_STRIP = 512
ROW_UNROLL = 4

AB_LAYOUT = (
    ("q_a", A_HEADS * HEAD_DIM), ("k_a", HEAD_DIM), ("v_a", HEAD_DIM),
    ("q_idx", IDX_HEADS * IDX_DIM), ("k_idx", IDX_DIM), ("w_idx", IDX_HEADS),
    ("q_b", B_HEADS * HEAD_DIM),
    ("k_cmp", B_KV_GROUPS * HEAD_DIM), ("v_cmp", B_KV_GROUPS * HEAD_DIM),
    ("k_slc", B_KV_GROUPS * HEAD_DIM), ("v_slc", B_KV_GROUPS * HEAD_DIM),
    ("k_win", B_KV_GROUPS * HEAD_DIM), ("v_win", B_KV_GROUPS * HEAD_DIM),
    ("gate_b", 3 * B_HEADS),
)
AB_PACKED = ("q_a", "q_idx", "q_b", "k_a", "k_idx", "k_slc", "k_win", "v_a", "v_slc", "v_win",
             "k_cmp", "v_cmp", "gate_b", "w_idx")
C_LAYOUT = (("q_c", C_HEADS * 2 * HEAD_DIM), ("k_c", C_HEADS * 2 * HEAD_DIM), ("v_c", C_HEADS * 2 * HEAD_DIM))


def _params(*sem):
    return pltpu.CompilerParams(dimension_semantics=sem, vmem_limit_bytes=VMEM_LIMIT_BYTES)


def _mm(a, b):
    return jnp.dot(a, b, preferred_element_type=f32)


def _layer_norm_rows(y, g, b):
    mu = jnp.mean(y, axis=-1, keepdims=True)
    yc = y - mu
    var = jnp.mean(yc * yc, axis=-1, keepdims=True)
    return yc * lax.rsqrt(var + LN_EPS) * g + b


def _proj_kernel(a_ref, w_ref, o_ref):
    o_ref[...] = _mm(a_ref[...].astype(MXU_DTYPE), w_ref[...]).astype(o_ref.dtype)


def project(a, w, tm=512):
    m, k = a.shape
    n = w.shape[1]
    return pl.pallas_call(
        _proj_kernel,
        grid=(m // tm,),
        in_specs=[pl.BlockSpec((tm, k), lambda i: (i, 0)), pl.BlockSpec((k, n), lambda i: (0, 0))],
        out_specs=pl.BlockSpec((tm, n), lambda i: (i, 0)),
        out_shape=jax.ShapeDtypeStruct((m, n), f32),
        compiler_params=_params("parallel"),
        name="project",
    )(a, w)


def _proj_ln_kernel(a_ref, w_ref, res_ref, g_ref, b_ref, o_ref):
    h = _mm(a_ref[...].astype(MXU_DTYPE), w_ref[...])
    o_ref[...] = _layer_norm_rows(DEEPNORM_ALPHA * res_ref[...] + h, g_ref[...], b_ref[...])


def project_residual_ln(a, w, res, g, b, tm=512):
    m, k = a.shape
    n = w.shape[1]
    return pl.pallas_call(
        _proj_ln_kernel,
        grid=(m // tm,),
        in_specs=[pl.BlockSpec((tm, k), lambda i: (i, 0)), pl.BlockSpec((k, n), lambda i: (0, 0)),
                  pl.BlockSpec((tm, n), lambda i: (i, 0)),
                  pl.BlockSpec((1, n), lambda i: (0, 0)), pl.BlockSpec((1, n), lambda i: (0, 0))],
        out_specs=pl.BlockSpec((tm, n), lambda i: (i, 0)),
        out_shape=jax.ShapeDtypeStruct((m, n), f32),
        compiler_params=_params("parallel"),
        name="project_residual_ln",
    )(a, w, res, g.reshape(1, n), b.reshape(1, n))


def _swiglu_tile(xb, w1_ref, w3_ref, w2_ref):
    a = _mm(xb, w1_ref[...])
    h = (a * jax.nn.sigmoid(a)) * _mm(xb, w3_ref[...])
    return _mm(h.astype(MXU_DTYPE), w2_ref[...])


def _ffn_ln_kernel(x_ref, w1_ref, w3_ref, w2_ref, g_ref, b_ref, o_ref, xb_ref, acc_ref):
    f = pl.program_id(1)

    @pl.when(f == 0)
    def _():
        xb_ref[...] = x_ref[...].astype(MXU_DTYPE)
        acc_ref[...] = jnp.zeros_like(acc_ref)

    acc_ref[...] += _swiglu_tile(xb_ref[...], w1_ref, w3_ref, w2_ref)

    @pl.when(f == pl.num_programs(1) - 1)
    def _():
        o_ref[...] = _layer_norm_rows(DEEPNORM_ALPHA * x_ref[...] + acc_ref[...], g_ref[...], b_ref[...])


def ffn_residual_ln(x, w1, w3, w2, g, b, tm, tf):
    m, d = x.shape
    ff = w1.shape[1]
    return pl.pallas_call(
        _ffn_ln_kernel,
        grid=(m // tm, ff // tf),
        in_specs=[pl.BlockSpec((tm, d), lambda i, f: (i, 0)),
                  pl.BlockSpec((d, tf), lambda i, f: (0, f)),
                  pl.BlockSpec((d, tf), lambda i, f: (0, f)),
                  pl.BlockSpec((tf, d), lambda i, f: (f, 0)),
                  pl.BlockSpec((1, d), lambda i, f: (0, 0)), pl.BlockSpec((1, d), lambda i, f: (0, 0))],
        out_specs=pl.BlockSpec((tm, d), lambda i, f: (i, 0)),
        out_shape=jax.ShapeDtypeStruct((m, d), f32),
        scratch_shapes=[pltpu.VMEM((tm, d), MXU_DTYPE), pltpu.VMEM((tm, d), f32)],
        compiler_params=_params("parallel", "arbitrary"),
        name="ffn_residual_ln",
    )(x, w1, w3, w2, g.reshape(1, d), b.reshape(1, d))


ROUTE_IDS = N_EXPERTS
ROUTE_GATES = N_EXPERTS + 2


def _router_kernel(x_ref, w_ref, o_ref, *, n_experts):
    logits = jnp.dot(x_ref[...], w_ref[...], preferred_element_type=f32, precision=lax.Precision.HIGHEST)
    lane = lax.broadcasted_iota(i32, logits.shape, 1).astype(f32)
    logits = jnp.where(lane < n_experts, logits, -jnp.inf)
    v1 = jnp.max(logits, axis=1, keepdims=True)
    i1 = jnp.min(jnp.where(logits == v1, lane, float(LANES)), axis=1, keepdims=True)
    rest = jnp.where(lane == i1, -jnp.inf, logits)
    v2 = jnp.max(rest, axis=1, keepdims=True)
    i2 = jnp.min(jnp.where(rest == v2, lane, float(LANES)), axis=1, keepdims=True)
    e2 = jnp.exp(v2 - v1)
    g1 = 1.0 / (1.0 + e2)
    g2 = e2 / (1.0 + e2)
    out = jnp.where(lane == ROUTE_IDS, i1, 0.0) + jnp.where(lane == ROUTE_IDS + 1, i2, 0.0)
    out = out + jnp.where(lane == ROUTE_GATES, g1, 0.0) + jnp.where(lane == ROUTE_GATES + 1, g2, 0.0)
    o_ref[...] = out


def route_top2(x, router_w, tm=512):
    m, d = x.shape
    n_experts = router_w.shape[1]
    w = jnp.zeros((d, LANES), f32).at[:, :n_experts].set(router_w)
    return pl.pallas_call(
        functools.partial(_router_kernel, n_experts=n_experts),
        grid=(m // tm,),
        in_specs=[pl.BlockSpec((tm, d), lambda i: (i, 0)), pl.BlockSpec((d, LANES), lambda i: (0, 0))],
        out_specs=pl.BlockSpec((tm, LANES), lambda i: (i, 0)),
        out_shape=jax.ShapeDtypeStruct((m, LANES), f32),
        compiler_params=_params("parallel"),
        name="route_top2",
    )(x, w)


def _moe_ln_kernel(tok_ref, gs_ref, off_ref, x_ref, w1_ref, w3_ref, w2_ref, g_ref, b_ref, o_ref,
                   xg_ref, xb_ref, y_ref, *, rt):
    c = pl.program_id(0)
    e = pl.program_id(1)
    f = pl.program_id(2)
    last_f = pl.num_programs(2) - 1
    start = off_ref[0, e]
    count = off_ref[0, e + 1] - start
    n_tiles = (count + rt - 1) // rt

    @pl.when((c == 0) & (e == 0) & (f == 0))
    def _():
        xg_ref[...] = jnp.zeros_like(xg_ref)

    @pl.when((e == 0) & (f == 0))
    def _():
        o_ref[...] = jnp.zeros_like(o_ref)

    def row_loop(body):
        def group(j, carry):
            for u in range(ROW_UNROLL):
                body(j * ROW_UNROLL + u)
            return carry

        def single(r, carry):
            body(r)
            return carry
        lax.fori_loop(0, count // ROW_UNROLL, group, 0)
        lax.fori_loop((count // ROW_UNROLL) * ROW_UNROLL, count, single, 0)

    @pl.when(f == 0)
    def _():
        def gather(r):
            t = tok_ref[0, start + r]
            xg_ref[pl.ds(r, 1), :] = x_ref[pl.ds(t, 1), :]
        row_loop(gather)

        def cast(j, carry):
            rows = pl.ds(pl.multiple_of(j * rt, rt), rt)
            xb_ref[rows, :] = xg_ref[rows, :].astype(MXU_DTYPE)
            return carry
        lax.fori_loop(0, n_tiles, cast, 0)

    def tile(j, carry):
        rows = pl.ds(pl.multiple_of(j * rt, rt), rt)
        y = _swiglu_tile(xb_ref[rows, :], w1_ref, w3_ref, w2_ref)

        @pl.when(f == 0)
        def _():
            y_ref[rows, :] = y

        @pl.when(f != 0)
        def _():
            y_ref[rows, :] += y
        return carry
    lax.fori_loop(0, n_tiles, tile, 0)

    @pl.when(f == last_f)
    def _():
        def scatter(r):
            t = tok_ref[0, start + r]
            o_ref[pl.ds(t, 1), :] += gs_ref[0, start + r] * y_ref[pl.ds(r, 1), :]
        row_loop(scatter)

    @pl.when((e == pl.num_programs(1) - 1) & (f == last_f))
    def _():
        o_ref[...] = _layer_norm_rows(DEEPNORM_ALPHA * x_ref[...] + o_ref[...], g_ref[...], b_ref[...])


def moe_residual_ln(x, routes, w1, w3, w2, g, b, tm, tf, rt=128):
    m, d = x.shape
    n_experts, _, ff = w1.shape
    n_chunks = m // tm
    ids = routes[:, ROUTE_IDS:ROUTE_IDS + TOP_K].astype(i32).reshape(n_chunks, tm * TOP_K)
    gts = routes[:, ROUTE_GATES:ROUTE_GATES + TOP_K].reshape(n_chunks, tm * TOP_K)
    order = jnp.argsort(ids, axis=1, stable=True).astype(i32)
    tok = order // TOP_K
    gs = jnp.take_along_axis(gts, order, axis=1)
    counts = jnp.sum(ids[:, :, None] == jnp.arange(n_experts, dtype=i32)[None, None, :], axis=1, dtype=i32)
    offs = jnp.concatenate([jnp.zeros((n_chunks, 1), i32), jnp.cumsum(counts, axis=1, dtype=i32)], axis=1)
    smem = lambda width: pl.BlockSpec((None, 1, width), lambda c, e, f: (c, 0, 0), memory_space=pltpu.SMEM)
    return pl.pallas_call(
        functools.partial(_moe_ln_kernel, rt=rt),
        grid=(n_chunks, n_experts, ff // tf),
        in_specs=[smem(tm * TOP_K), smem(tm * TOP_K), smem(n_experts + 1),
                  pl.BlockSpec((tm, d), lambda c, e, f: (c, 0), pipeline_mode=pl.Buffered(1)),
                  pl.BlockSpec((None, d, tf), lambda c, e, f: (e, 0, f)),
                  pl.BlockSpec((None, d, tf), lambda c, e, f: (e, 0, f)),
                  pl.BlockSpec((None, tf, d), lambda c, e, f: (e, f, 0)),
                  pl.BlockSpec((1, d), lambda c, e, f: (0, 0)), pl.BlockSpec((1, d), lambda c, e, f: (0, 0))],
        out_specs=pl.BlockSpec((tm, d), lambda c, e, f: (c, 0), pipeline_mode=pl.Buffered(1)),
        out_shape=jax.ShapeDtypeStruct((m, d), f32),
        scratch_shapes=[pltpu.VMEM((tm, d), f32), pltpu.VMEM((tm, d), MXU_DTYPE), pltpu.VMEM((tm, d), f32)],
        compiler_params=_params("arbitrary", "arbitrary", "arbitrary"),
        name="moe_residual_ln",
    )(tok[:, None, :], gs[:, None, :], offs[:, None, :], x, w1, w3, w2, g.reshape(1, d), b.reshape(1, d))


def _flash_reset(m_ref, acc_ref):
    m_ref[...] = jnp.full(m_ref.shape, NEG, f32)
    acc_ref[...] = jnp.zeros(acc_ref.shape, f32)


def _flash_rows(q, kt, bias, m_ref, alpha_ref, p_ref, rows):
    s = _mm(q, kt)
    if bias is not None:
        s = s + bias
    m_prev = m_ref[rows, :]
    m_next = jnp.maximum(m_prev, jnp.max(s, axis=1, keepdims=True))
    alpha_ref[rows, :] = jnp.exp2(m_prev - m_next)
    m_ref[rows, :] = m_next
    p_ref[rows, :] = jnp.exp2(s - _lane_tile(m_next, s.shape[1] // LANES)).astype(p_ref.dtype)


def _flash_accumulate(v_aug, alpha_ref, p_ref, acc_ref):
    alpha = _lane_tile(alpha_ref[...], acc_ref.shape[-1] // LANES)
    acc_ref[...] = alpha * acc_ref[...] + _mm(p_ref[...], v_aug)


def _skewed_chunks(n, score, fold, carry):
    def pair(j, carry):
        c = 2 * j
        fold(c - 1, 1)
        carry = score(c, 0, carry)
        fold(c, 0)
        return score(c + 1, 1, carry)

    def single(c, carry):
        fold(c - 1, 1)
        return score(c, 0, carry)

    carry = lax.fori_loop(0, n // 2, pair, carry)
    return lax.fori_loop(2 * (n // 2), n, single, carry)


def _fold_last(n, fold):
    for buf in range(2):
        pl.when((n - 1) % 2 == buf)(functools.partial(fold, n - 1, buf))


def _lane_tile(x, reps):
    return x if reps == 1 else jnp.concatenate([x] * reps, axis=1)


def _with_ones(v):
    return jnp.concatenate([v, jnp.ones_like(v)], axis=-1)


def _row_slices(n_rows):
    return [slice(r * Q_BLOCK, (r + 1) * Q_BLOCK) for r in range(n_rows // Q_BLOCK)]


def _dsa_kernel(qi_ref, wi_ref, kit_ref, qa_ref, kat_ref, va_ref, o_ref,
                skey_ref, wrep_ref, m_ref, alpha_ref, acc_ref, p_ref, qh_ref, *, topk, kc):
    qb = Q_BLOCK
    i = pl.program_id(1)
    q0 = i * qb
    n_chunks = (q0 + qb + kc - 1) // kc
    t_row = q0 + lax.broadcasted_iota(i32, (qb, kc), 0)
    col = lax.broadcasted_iota(i32, (qb, kc), 1)
    wi = wi_ref[...]
    qi = qi_ref[...]
    qis = [qi[:, h * IDX_DIM:(h + 1) * IDX_DIM] for h in range(IDX_HEADS)]
    qa = qa_ref[...]
    for h in range(A_HEADS):
        qh_ref[h * qb:(h + 1) * qb, :] = qa[:, h * HEAD_DIM:(h + 1) * HEAD_DIM]
    for h in range(IDX_HEADS):
        wrep_ref[h] = jnp.broadcast_to(wi[:, h:h + 1], (qb, kc))

    def score_body(c, carry):
        off = pl.multiple_of(c * kc, kc)
        kt = kit_ref[:, pl.ds(off, kc)]
        s = jnp.zeros((qb, kc), f32)
        for h in range(IDX_HEADS):
            s = s + wrep_ref[h] * jnp.maximum(_mm(qis[h], kt), 0.0)
        s = jnp.where(col + off <= t_row, s, NEG)
        bits = pltpu.bitcast(s, i32)
        skey_ref[:, pl.ds(off, kc)] = jnp.where(bits < 0, bits ^ 0x7FFFFFFF, bits)
        return carry

    n_strips = (q0 + qb + COUNT_STRIP - 1) // COUNT_STRIP
    lax.fori_loop(0, n_strips * (COUNT_STRIP // kc), score_body, 0)

    def count_ge(cand):
        def body(c, acc):
            off = pl.multiple_of(c * COUNT_STRIP, COUNT_STRIP)
            for j in range(COUNT_STRIP // LANES):
                acc = acc + jnp.where(skey_ref[:, pl.ds(off + j * LANES, LANES)] >= cand, 1.0, 0.0)
            return acc
        acc = lax.fori_loop(0, n_strips, body, jnp.zeros((qb, LANES), f32))
        return jnp.sum(acc, axis=1, keepdims=True)

    thr = jnp.where(count_ge(jnp.zeros((qb, 1), i32)) >= topk, 0, INT_MIN).astype(i32)

    def bisect(b, thr):
        cand = thr + jnp.left_shift(jnp.int32(1), 30 - b)
        return jnp.where(count_ge(cand) >= topk, cand, thr)

    thr = lax.fori_loop(0, 31, bisect, thr)
    quota = topk - count_ge(thr + 1)

    before = (lax.broadcasted_iota(i32, (kc, kc), 0) < lax.broadcasted_iota(i32, (kc, kc), 1))
    before = jnp.where(before, 1.0, 0.0).astype(MXU_DTYPE)
    _flash_reset(m_ref, acc_ref)
    alpha_ref[...] = jnp.zeros(alpha_ref.shape, f32)
    p_ref[...] = jnp.zeros(p_ref.shape, p_ref.dtype)
    heads = _row_slices(A_HEADS * qb)

    def fold(c, buf):
        off = pl.multiple_of(jnp.maximum(c, 0) * kc, kc)
        _flash_accumulate(va_ref[pl.ds(off, kc), :], alpha_ref.at[buf], p_ref.at[buf], acc_ref)

    def score(c, buf, ties_seen):
        off = pl.multiple_of(c * kc, kc)
        key = skey_ref[:, pl.ds(off, kc)]
        eq = key == thr
        eqf = jnp.where(eq, 1.0, 0.0)
        rank = ties_seen + _mm(eqf.astype(MXU_DTYPE), before)
        sel = ((key > thr) | (eq & (rank < quota))) & (col + off <= t_row)
        bias = jnp.where(sel, 0.0, NEG)
        kt = kat_ref[:, pl.ds(off, kc)]
        for rows in heads:
            _flash_rows(qh_ref[rows, :], kt, bias, m_ref, alpha_ref.at[buf], p_ref.at[buf], rows)
        return ties_seen + jnp.sum(eqf, axis=1, keepdims=True)

    _skewed_chunks(n_chunks, score, fold, jnp.zeros((qb, 1), f32))
    _fold_last(n_chunks, fold)
    acc = acc_ref[...]
    o = acc[:, :HEAD_DIM] / acc[:, HEAD_DIM:]
    for h in range(A_HEADS):
        o_ref[:, h * HEAD_DIM:(h + 1) * HEAD_DIM] = o[h * qb:(h + 1) * qb, :]


QPK_QA, QPK_QIDX, QPK_QB = 0, 2, 3
KPK_KA, KPK_KIDX, KPK_KSLC, KPK_KWIN = 0, 1, 2, 4
VPK_VA, VPK_VSLC, VPK_VWIN = 0, 1, 3
OUT_NSA = 2


def dsa_attention(wi, qpk, kpk, vpk, topk, kc=512):
    bsz, seq, _ = qpk.shape
    nb = seq // Q_BLOCK
    rows = A_HEADS * Q_BLOCK
    width = A_HEADS * HEAD_DIM
    kern = functools.partial(_dsa_kernel, topk=topk, kc=kc)
    return pl.pallas_call(
        kern,
        grid=(bsz, nb),
        in_specs=[pl.BlockSpec((None, Q_BLOCK, IDX_HEADS * IDX_DIM), lambda b, i: (b, i, QPK_QIDX)),
                  pl.BlockSpec((None, Q_BLOCK, IDX_HEADS), lambda b, i: (b, i, 0)),
                  pl.BlockSpec((None, IDX_DIM, seq), lambda b, i: (b, KPK_KIDX, 0)),
                  pl.BlockSpec((None, Q_BLOCK, width), lambda b, i: (b, i, QPK_QA)),
                  pl.BlockSpec((None, HEAD_DIM, seq), lambda b, i: (b, KPK_KA, 0)),
                  pl.BlockSpec((None, seq, 2 * HEAD_DIM), lambda b, i: (b, 0, VPK_VA))],
        out_specs=pl.BlockSpec((None, Q_BLOCK, width), lambda b, i: (b, i, 0)),
        out_shape=jax.ShapeDtypeStruct((bsz, seq, 2 * width), f32),
        scratch_shapes=[pltpu.VMEM((Q_BLOCK, seq), i32), pltpu.VMEM((IDX_HEADS, Q_BLOCK, kc), f32),
                        pltpu.VMEM((rows, LANES), f32), pltpu.VMEM((2, rows, LANES), f32),
                        pltpu.VMEM((rows, 2 * HEAD_DIM), f32), pltpu.VMEM((2, rows, kc), MXU_DTYPE),
                        pltpu.VMEM((rows, HEAD_DIM), MXU_DTYPE)],
        compiler_params=_params("parallel", "arbitrary"),
        name="dsa_attention",
    )(qpk, wi, kpk, qpk, kpk, vpk)


def _compress_kernel(x_ref, pe_ref, w1_ref, w2_ref, o_ref):
    half = (CMP_LEN // 2) * HEAD_DIM
    x = x_ref[...]
    first = _mm((x + pe_ref[0:1, :]).astype(MXU_DTYPE), w1_ref[0:half, :])
    second = _mm((x + pe_ref[1:2, :]).astype(MXU_DTYPE), w1_ref[half:2 * half, :])
    pre = first + pltpu.roll(second, shift=x.shape[0] - 1, axis=0)
    hid = pre * jax.nn.sigmoid(pre)
    o_ref[...] = _mm(hid.astype(MXU_DTYPE), w2_ref[...])


def nsa_compress(chunks, pe2, w1, w2):
    bsz, groups, nck, width = chunks.shape
    return pl.pallas_call(
        _compress_kernel,
        grid=(bsz, groups),
        in_specs=[pl.BlockSpec((None, None, nck, width), lambda b, g: (b, g, 0, 0)),
                  pl.BlockSpec((2, width), lambda b, g: (0, 0)),
                  pl.BlockSpec((2 * width, CMP_HIDDEN), lambda b, g: (0, 0)),
                  pl.BlockSpec((CMP_HIDDEN, HEAD_DIM), lambda b, g: (0, 0))],
        out_specs=pl.BlockSpec((None, None, nck, HEAD_DIM), lambda b, g: (b, g, 0, 0)),
        out_shape=jax.ShapeDtypeStruct((bsz, groups, nck, HEAD_DIM), f32),
        compiler_params=_params("parallel", "parallel"),
        name="nsa_compress",
    )(chunks, pe2, w1, w2)


def _nsa_kernel(qin_ref, graw_ref, kct_ref, vc_ref, kst_ref, vs_ref, kwt_ref, vw_ref, ovl_ref, _, o_ref,
                m_ref, alpha_ref, acc_ref, p_ref, pw_ref, q_ref, *, n_slc, n_sel, kc):
    qb = Q_BLOCK
    i = pl.program_id(2)
    q0 = i * qb
    n_cmp = kct_ref.shape[1]
    heads = _row_slices(B_REP * qb)
    qin = qin_ref[...]
    for r in range(B_REP):
        q_ref[r * qb:(r + 1) * qb, :] = qin[:, r * HEAD_DIM:(r + 1) * HEAD_DIM]

    t_c = q0 + lax.broadcasted_iota(i32, (qb, n_cmp), 0)
    cmp_end = lax.broadcasted_iota(i32, (qb, n_cmp), 1) * CMP_STRIDE + (CMP_LEN - 1)
    vis = cmp_end <= t_c
    kct = kct_ref[...]
    p_sum = jnp.zeros((qb, n_cmp), f32)
    for rows in heads:
        lc = jnp.where(vis, _mm(q_ref[rows, :], kct), NEG)
        ec = jnp.where(vis, jnp.exp2(lc - jnp.max(lc, axis=1, keepdims=True)), 0.0)
        den = jnp.sum(ec, axis=1, keepdims=True)
        p_c = ec / jnp.where(den > 0.0, den, 1.0)
        p_sum = p_sum + p_c
        acc_ref[rows, :HEAD_DIM] = _mm(p_c.astype(MXU_DTYPE), vc_ref[...])
    o_c = acc_ref[:, :HEAD_DIM]

    imp = jnp.dot(p_sum, ovl_ref[...], preferred_element_type=f32, precision=lax.Precision.HIGHEST).T
    t_q = q0 + lax.broadcasted_iota(i32, (LANES, qb), 1)
    blk = lax.broadcasted_iota(i32, (LANES, qb), 0)
    blk_t = t_q // SLC_LEN
    forced = (blk == 0) | (blk == blk_t) | (blk == blk_t - 1)
    imp = jnp.where(forced, FORCED_BOOST, imp)
    imp = jnp.where(blk * SLC_LEN <= t_q, imp, NEG)
    imp = jnp.where(blk < n_slc, imp, -jnp.inf)
    blk_f = blk.astype(f32)

    def pick(_, carry):
        imp, selm = carry
        best = jnp.max(imp, axis=0, keepdims=True)
        first = jnp.min(jnp.where(imp == best, blk_f, float(LANES)), axis=0, keepdims=True)
        hit = blk_f == first
        return jnp.where(hit, -jnp.inf, imp), jnp.where(hit, 1.0, selm)

    _, selm = lax.fori_loop(0, n_sel, pick, (imp, jnp.zeros((LANES, qb), f32)))
    selm = selm.T.astype(MXU_DTYPE)

    t_k = q0 + lax.broadcasted_iota(i32, (qb, kc), 0)
    col = lax.broadcasted_iota(i32, (qb, kc), 1)
    exp_row = lax.broadcasted_iota(i32, (LANES, kc), 0)
    exp_col = lax.broadcasted_iota(i32, (LANES, kc), 1)
    _flash_reset(m_ref, acc_ref)
    alpha_ref[...] = jnp.zeros(alpha_ref.shape, f32)
    p_ref[...] = jnp.zeros(p_ref.shape, p_ref.dtype)

    def fold(c, buf):
        off = pl.multiple_of(jnp.maximum(c, 0) * kc, kc)
        _flash_accumulate(vs_ref[pl.ds(off, kc), :], alpha_ref.at[buf], p_ref.at[buf], acc_ref)

    def score(c, buf, carry):
        off = pl.multiple_of(c * kc, kc)
        expand = jnp.where(exp_row == (exp_col + off) // SLC_LEN, 1.0, 0.0).astype(MXU_DTYPE)
        sel = (_mm(selm, expand) > 0.5) & (col + off <= t_k)
        bias = jnp.where(sel, 0.0, NEG)
        kt = kst_ref[:, pl.ds(off, kc)]
        for rows in heads:
            _flash_rows(q_ref[rows, :], kt, bias, m_ref, alpha_ref.at[buf], p_ref.at[buf], rows)
        return carry

    n_slc_chunks = (q0 + qb + kc - 1) // kc
    _skewed_chunks(n_slc_chunks, score, fold, 0)
    _fold_last(n_slc_chunks, fold)
    acc = acc_ref[...]
    o_s = acc[:, :HEAD_DIM] / acc[:, HEAD_DIM:]

    slab = WINDOW + qb
    w0 = pl.multiple_of(jnp.maximum(q0 - WINDOW, 0), qb)
    dist = (q0 + lax.broadcasted_iota(i32, (qb, slab), 0)) - (w0 + lax.broadcasted_iota(i32, (qb, slab), 1))
    bias = jnp.where((dist >= 0) & (dist < WINDOW), 0.0, NEG)
    kt = kwt_ref[:, pl.ds(w0, slab)]
    for rows in heads:
        s = _mm(q_ref[rows, :], kt) + bias
        pw_ref[rows, :] = jnp.exp2(s - jnp.max(s, axis=1, keepdims=True)).astype(pw_ref.dtype)
    acc = _mm(pw_ref[...], vw_ref[pl.ds(w0, slab), :])
    o_w = acc[:, :HEAD_DIM] / acc[:, HEAD_DIM:]

    gates = jax.nn.sigmoid(graw_ref[...])
    for r, rows in enumerate(heads):
        g_c, g_s, g_w = (gates[:, 3 * r + n:3 * r + n + 1] for n in range(3))
        o_ref[:, r * HEAD_DIM:(r + 1) * HEAD_DIM] = g_c * o_c[rows] + g_s * o_s[rows] + g_w * o_w[rows]


def nsa_attention(qpk, graw, kct, vc, kpk, vpk, ovl, out_ab, n_slc, n_sel, kc=512):
    bsz, seq, _ = qpk.shape
    groups = B_KV_GROUPS
    nb = seq // Q_BLOCK
    rows = B_REP * Q_BLOCK
    width = B_REP * HEAD_DIM
    n_cmp = kct.shape[-1]
    kern = functools.partial(_nsa_kernel, n_slc=n_slc, n_sel=n_sel, kc=kc)
    per_bg = lambda b, g, i: (b, g, 0, 0)
    return pl.pallas_call(
        kern,
        grid=(bsz, groups, nb),
        in_specs=[pl.BlockSpec((None, Q_BLOCK, width), lambda b, g, i: (b, i, QPK_QB + g)),
                  pl.BlockSpec((None, None, Q_BLOCK, B_REP * 3), lambda b, g, i: (b, g, i, 0)),
                  pl.BlockSpec((None, None, HEAD_DIM, n_cmp), per_bg),
                  pl.BlockSpec((None, None, n_cmp, HEAD_DIM), per_bg),
                  pl.BlockSpec((None, HEAD_DIM, seq), lambda b, g, i: (b, KPK_KSLC + g, 0)),
                  pl.BlockSpec((None, seq, 2 * HEAD_DIM), lambda b, g, i: (b, 0, VPK_VSLC + g)),
                  pl.BlockSpec((None, HEAD_DIM, seq), lambda b, g, i: (b, KPK_KWIN + g, 0)),
                  pl.BlockSpec((None, seq, 2 * HEAD_DIM), lambda b, g, i: (b, 0, VPK_VWIN + g)),
                  pl.BlockSpec((n_cmp, LANES), lambda b, g, i: (0, 0)),
                  pl.BlockSpec(memory_space=pl.ANY)],
        out_specs=pl.BlockSpec((None, Q_BLOCK, width), lambda b, g, i: (b, i, OUT_NSA + g)),
        out_shape=jax.ShapeDtypeStruct(out_ab.shape, f32),
        input_output_aliases={9: 0},
        scratch_shapes=[pltpu.VMEM((rows, LANES), f32), pltpu.VMEM((2, rows, LANES), f32),
                        pltpu.VMEM((rows, 2 * HEAD_DIM), f32), pltpu.VMEM((2, rows, kc), MXU_DTYPE),
                        pltpu.VMEM((rows, WINDOW + Q_BLOCK), MXU_DTYPE), pltpu.VMEM((rows, HEAD_DIM), MXU_DTYPE)],
        compiler_params=_params("parallel", "parallel", "arbitrary"),
        name="nsa_attention",
    )(qpk, graw, kct, vc, kpk, vpk, kpk, vpk, ovl, out_ab)


def _diff_kernel(lam_ref, qin_ref, kt_ref, v_ref, g_ref, o_ref, m_ref, alpha_ref, acc_ref, p_ref, q_ref,
                 *, tq, kc, out_scale):
    i = pl.program_id(2)
    q0 = i * tq
    dv = v_ref.shape[-1] // 2
    groups = _row_slices(tq)
    qin = qin_ref[...]
    for half in range(2):
        q_ref[half] = qin[:, half * HEAD_DIM:(half + 1) * HEAD_DIM]
    col = lax.broadcasted_iota(i32, (Q_BLOCK, kc), 1)
    t_row = [q0 + r * Q_BLOCK + lax.broadcasted_iota(i32, (Q_BLOCK, kc), 0) for r in range(len(groups))]
    for half in range(2):
        _flash_reset(m_ref.at[half], acc_ref.at[half])
    alpha_ref[...] = jnp.zeros(alpha_ref.shape, f32)
    p_ref[...] = jnp.zeros(p_ref.shape, p_ref.dtype)

    def fold(c, buf):
        off = pl.multiple_of(jnp.maximum(c, 0) * kc, kc)
        v = v_ref[pl.ds(off, kc), :]
        for half in range(2):
            _flash_accumulate(v, alpha_ref.at[buf, half], p_ref.at[buf, half], acc_ref.at[half])

    def score(c, buf, masked):
        off = pl.multiple_of(c * kc, kc)
        for half in range(2):
            kt = kt_ref[half * HEAD_DIM:(half + 1) * HEAD_DIM, pl.ds(off, kc)]
            for r, rows in enumerate(groups):
                bias = jnp.where(col + off <= t_row[r], 0.0, NEG) if masked else None
                _flash_rows(q_ref[half, rows, :], kt, bias, m_ref.at[half], alpha_ref.at[buf, half],
                            p_ref.at[buf, half], rows)

    def score_full(c, buf, carry):
        score(c, buf, False)
        return carry

    n_full = q0 // kc
    _skewed_chunks(n_full, score_full, fold, 0)

    def finish(buf):
        fold(n_full - 1, 1 - buf)
        score(n_full, buf, True)
        fold(n_full, buf)

    for buf in range(2):
        pl.when(n_full % 2 == buf)(functools.partial(finish, buf))
    a1 = acc_ref[0]
    a2 = acc_ref[1]
    o = a1[:, :dv] / a1[:, dv:] - lam_ref[0] * (a2[:, :dv] / a2[:, dv:])
    o = o * lax.rsqrt(jnp.mean(o * o, axis=-1, keepdims=True) + LN_EPS) * g_ref[...]
    o_ref[...] = o * out_scale


def diff_attention(lam, q, kt, v, subln_g, out_scale, tq=512, kc=1024):
    bsz, seq, width = q.shape
    dv = 2 * HEAD_DIM
    heads = width // dv
    tq = min(tq, seq)
    kc = min(kc, seq)
    assert kc % tq == 0 and seq % kc == 0
    kern = functools.partial(_diff_kernel, tq=tq, kc=kc, out_scale=out_scale)
    return pl.pallas_call(
        kern,
        grid=(bsz, heads, seq // tq),
        in_specs=[pl.BlockSpec(memory_space=pltpu.SMEM),
                  pl.BlockSpec((None, tq, dv), lambda b, h, i: (b, i, h)),
                  pl.BlockSpec((None, dv, seq), lambda b, h, i: (b, h, 0)),
                  pl.BlockSpec((None, seq, 2 * dv), lambda b, h, i: (b, 0, h)),
                  pl.BlockSpec((1, dv), lambda b, h, i: (0, 0))],
        out_specs=pl.BlockSpec((None, tq, dv), lambda b, h, i: (b, i, h)),
        out_shape=jax.ShapeDtypeStruct((bsz, seq, heads * dv), f32),
        scratch_shapes=[pltpu.VMEM((2, tq, LANES), f32), pltpu.VMEM((2, 2, tq, LANES), f32),
                        pltpu.VMEM((2, tq, 2 * dv), f32), pltpu.VMEM((2, 2, tq, kc), MXU_DTYPE),
                        pltpu.VMEM((2, tq, HEAD_DIM), MXU_DTYPE)],
        compiler_params=_params("parallel", "parallel", "arbitrary"),
        name="diff_attention",
    )(lam, q, kt, v, subln_g.reshape(1, dv))


def _split(h, layout):
    offs = np.cumsum([w for _, w in layout])[:-1].tolist()
    return jnp.split(h, offs, axis=-1)


def _rope_tables(positions):
    inv_freq = ROPE_THETA ** (-jnp.arange(0, ROT_DIM, 2, dtype=f32) / ROT_DIM)
    ang = positions.astype(f32)[..., None] * inv_freq
    return jnp.cos(ang), jnp.sin(ang)


def _apply_rope(x, cos, sin):
    shape = cos.shape[:2] + (1,) * (x.ndim - 3) + cos.shape[-1:]
    c = cos.reshape(shape)
    s = sin.reshape(shape)
    half = ROT_DIM // 2
    x1, x2 = x[..., :half], x[..., half:ROT_DIM]
    return jnp.concatenate([x1 * c - x2 * s, x2 * c + x1 * s, x[..., ROT_DIM:]], axis=-1)


def _pad_cols(w, mult):
    n = w.shape[-1]
    pad = (-n) % mult
    return jnp.pad(w, ((0, 0), (0, pad))) if pad else w


def _overlap_matrix(n_cmp_rows, n_slc):
    c_start = np.arange(n_cmp_rows) * CMP_STRIDE
    s_start = np.arange(LANES) * SLC_LEN
    ovl = (c_start[:, None] < s_start[None, :] + SLC_LEN) & (c_start[:, None] + CMP_LEN > s_start[None, :])
    ovl = ovl & (np.arange(LANES)[None, :] < n_slc)
    return jnp.asarray(ovl.astype(np.float32))


def _ab_mixer(x2, bsz, seq, positions, cos, sin, w_in, pe_k, pe_v, ck1, ck2, cv1, cv2):
    nb = seq // Q_BLOCK
    widths = dict(AB_LAYOUT)
    starts = dict(zip(widths, np.cumsum([0] + [w for _, w in AB_LAYOUT[:-1]]).tolist()))
    perm = np.concatenate([np.arange(starts[n], starts[n] + widths[n]) for n in AB_PACKED])
    h = project(x2, _pad_cols(w_in[:, perm], 2 * LANES).astype(MXU_DTYPE)).reshape(bsz, seq, -1)
    ends = np.cumsum([widths[n] for n in AB_PACKED]).tolist()
    cut = dict(zip(AB_PACKED, ends))
    q_all = h[..., :cut["q_b"]]
    k_all = h[..., cut["q_b"]:cut["k_win"]]
    v_all = h[..., cut["k_win"]:cut["v_win"]]
    k_cmp = h[..., cut["v_win"]:cut["k_cmp"]]
    v_cmp = h[..., cut["k_cmp"]:cut["v_cmp"]]
    g_b = h[..., cut["v_cmp"]:cut["gate_b"]]
    w_idx = h[..., cut["gate_b"]:cut["w_idx"]] * (IDX_HEADS * IDX_DIM) ** -0.5

    def heads_rope(a):
        return _apply_rope(a.reshape(bsz, seq, -1, HEAD_DIM), cos, sin).reshape(bsz, seq, -1)

    q_scale = np.concatenate([np.full(widths[n], 1.0 if n == "q_idx" else QK_SCALE, np.float32)
                              for n in ("q_a", "q_idx", "q_b")])
    qpk = (heads_rope(q_all) * q_scale).astype(MXU_DTYPE)
    kpk = heads_rope(k_all).transpose(0, 2, 1).astype(MXU_DTYPE)
    vpk = _with_ones(v_all.reshape(bsz, seq, -1, HEAD_DIM).astype(MXU_DTYPE)).reshape(bsz, seq, -1)

    out_ab = dsa_attention(w_idx, qpk, kpk, vpk, topk=min(DSA_TOPK, seq // 4))

    groups = B_KV_GROUPS
    n_rows = seq // CMP_STRIDE
    n_cmp = (seq - CMP_LEN) // CMP_STRIDE + 1
    assert n_cmp == n_rows - 1
    n_slc = seq // SLC_LEN
    assert n_slc <= LANES

    def chunked(kv):
        kv = kv.reshape(bsz, n_rows, CMP_STRIDE, groups, HEAD_DIM).transpose(0, 3, 1, 2, 4)
        return kv.reshape(bsz, groups, n_rows, CMP_STRIDE * HEAD_DIM)

    def pe_halves(pe):
        return pe.reshape(2, CMP_STRIDE * HEAD_DIM)

    k_c = nsa_compress(chunked(k_cmp), pe_halves(pe_k), ck1.astype(MXU_DTYPE), ck2.astype(MXU_DTYPE))
    v_c = nsa_compress(chunked(v_cmp), pe_halves(pe_v), cv1.astype(MXU_DTYPE), cv2.astype(MXU_DTYPE))
    cmp_end = jnp.minimum(jnp.arange(n_rows) * CMP_STRIDE + CMP_LEN - 1, seq - 1)
    cos_c, sin_c = _rope_tables(positions[:, cmp_end])
    k_c = _apply_rope(k_c.transpose(0, 2, 1, 3), cos_c, sin_c)
    k_c = k_c.transpose(0, 2, 3, 1).astype(MXU_DTYPE)
    v_c = v_c.astype(MXU_DTYPE)

    graw = g_b.reshape(bsz, seq, groups, B_REP * 3).transpose(0, 2, 1, 3)
    out_ab = nsa_attention(qpk, graw, k_c, v_c, kpk, vpk, _overlap_matrix(n_rows, n_slc), out_ab,
                           n_slc=n_slc, n_sel=min(SLC_TOPN, n_slc))
    return out_ab.reshape(bsz * seq, (A_HEADS + B_HEADS) * HEAD_DIM)


def _diff_mixer(x2, bsz, seq, cos, sin, w_in, lq1, lk1, lq2, lk2, subln_g, lam_init):
    h = project(x2, w_in.astype(MXU_DTYPE)).reshape(bsz, seq, -1)
    q, k, v = _split(h, C_LAYOUT)

    def halves_rope(a):
        return _apply_rope(a.reshape(bsz, seq, 2 * C_HEADS, HEAD_DIM), cos, sin).reshape(bsz, seq, -1)

    q = (halves_rope(q) * QK_SCALE).astype(MXU_DTYPE)
    kt = halves_rope(k).transpose(0, 2, 1).astype(MXU_DTYPE)
    v = _with_ones(v.reshape(bsz, seq, C_HEADS, 2 * HEAD_DIM).astype(MXU_DTYPE)).reshape(bsz, seq, -1)
    lam = (jnp.exp(jnp.sum(lq1 * lk1)) - jnp.exp(jnp.sum(lq2 * lk2)) + lam_init).reshape(1).astype(f32)
    o = diff_attention(lam, q, kt, v, subln_g, 1.0 - lam_init)
    return o.reshape(bsz * seq, C_HEADS * 2 * HEAD_DIM)


def kernel(x, positions, ab_w_in, cmp_pe_k, cmp_pe_v, cmp_k_w1, cmp_k_w2, cmp_v_w1, cmp_v_w2, ab_w_out, ln_ab_g, ln_ab_b, ffn_w1, ffn_w3, ffn_w2, ln_ffn_g, ln_ffn_b, c_w_in, lambda_q1, lambda_k1, lambda_q2, lambda_k2, c_subln_g, c_w_out, ln_c_g, ln_c_b, router_w, moe_w1, moe_w3, moe_w2, ln_moe_g, ln_moe_b):
    bsz, seq, d = x.shape
    assert seq % COUNT_STRIP == 0 and seq >= WINDOW + Q_BLOCK and d == D_MODEL
    cos, sin = _rope_tables(positions)
    x2 = x.reshape(bsz * seq, d)
    for layer in range(DEPTH):
        i = layer // 2
        if layer % 2 == 0:
            o = _ab_mixer(x2, bsz, seq, positions, cos, sin, ab_w_in[i], cmp_pe_k[i], cmp_pe_v[i],
                          cmp_k_w1[i], cmp_k_w2[i], cmp_v_w1[i], cmp_v_w2[i])
            x2 = project_residual_ln(o, ab_w_out[i].astype(MXU_DTYPE), x2, ln_ab_g[i], ln_ab_b[i])
            x2 = ffn_residual_ln(x2, ffn_w1[i].astype(MXU_DTYPE), ffn_w3[i].astype(MXU_DTYPE),
                                 ffn_w2[i].astype(MXU_DTYPE), ln_ffn_g[i], ln_ffn_b[i], tm=512, tf=1408)
        else:
            lam_init = 0.8 - 0.6 * math.exp(-0.3 * layer)
            o = _diff_mixer(x2, bsz, seq, cos, sin, c_w_in[i], lambda_q1[i], lambda_k1[i], lambda_q2[i],
                            lambda_k2[i], c_subln_g[i], lam_init)
            x2 = project_residual_ln(o, c_w_out[i].astype(MXU_DTYPE), x2, ln_c_g[i], ln_c_b[i])
            routes = route_top2(x2, router_w[i])
            x2 = moe_residual_ln(x2, routes, moe_w1[i].astype(MXU_DTYPE), moe_w3[i].astype(MXU_DTYPE),
                                 moe_w2[i].astype(MXU_DTYPE), ln_moe_g[i], ln_moe_b[i], tm=min(2048, bsz * seq // 2), tf=896, rt=128)
    return x2.reshape(bsz, seq, d)
```

```python
import functools
import math

import numpy as np
import jax
import jax.numpy as jnp
from jax import lax
from jax.experimental import pallas as pl
from jax.experimental.pallas import tpu as pltpu

f32 = jnp.float32
i32 = jnp.int32
MXU_DTYPE = jnp.bfloat16
VMEM_LIMIT_BYTES = 56 * 1024 * 1024
LANES = 128

D_MODEL = 1024
DEPTH = 2
HEAD_DIM = 64
ROT_DIM = HEAD_DIM // 4
ROPE_THETA = 500000.0
Q_BLOCK = 128
NEG = -1e30
LN_EPS = 1e-5
A_HEADS = 8
IDX_HEADS = 4
IDX_DIM = 64
DSA_TOPK = 256
B_HEADS = 8
B_KV_GROUPS = 2
B_REP = B_HEADS // B_KV_GROUPS
CMP_LEN = 32
CMP_STRIDE = 16
CMP_HIDDEN = 128
SLC_LEN = 64
SLC_TOPN = 16
WINDOW = 512
FORCED_BOOST = 1e6
C_HEADS = 8
N_EXPERTS = 8
TOP_K = 2
DEEPNORM_ALPHA = (2 * DEPTH) ** 0.25
QK_SCALE = HEAD_DIM ** -0.5 * math.log2(math.e)
INT_MIN = -(2 ** 31)
COUNT_STRIP = 512
ROW_UNROLL = 4

AB_LAYOUT = (
    ("q_a", A_HEADS * HEAD_DIM), ("k_a", HEAD_DIM), ("v_a", HEAD_DIM),
    ("q_idx", IDX_HEADS * IDX_DIM), ("k_idx", IDX_DIM), ("w_idx", IDX_HEADS),
    ("q_b", B_HEADS * HEAD_DIM),
    ("k_cmp", B_KV_GROUPS * HEAD_DIM), ("v_cmp", B_KV_GROUPS * HEAD_DIM),
    ("k_slc", B_KV_GROUPS * HEAD_DIM), ("v_slc", B_KV_GROUPS * HEAD_DIM),
    ("k_win", B_KV_GROUPS * HEAD_DIM), ("v_win", B_KV_GROUPS * HEAD_DIM),
    ("gate_b", 3 * B_HEADS),
)
AB_PACKED = ("q_a", "q_idx", "q_b", "k_a", "k_idx", "k_slc", "k_win", "v_a", "v_slc", "v_win",
             "k_cmp", "v_cmp", "gate_b", "w_idx")
C_LAYOUT = (("q_c", C_HEADS * 2 * HEAD_DIM), ("k_c", C_HEADS * 2 * HEAD_DIM), ("v_c", C_HEADS * 2 * HEAD_DIM))


def _params(*sem):
    return pltpu.CompilerParams(dimension_semantics=sem, vmem_limit_bytes=VMEM_LIMIT_BYTES)


def _mm(a, b):
    return jnp.dot(a, b, preferred_element_type=f32)


def _layer_norm_rows(y, g, b):
    mu = jnp.mean(y, axis=-1, keepdims=True)
    yc = y - mu
    var = jnp.mean(yc * yc, axis=-1, keepdims=True)
    return yc * lax.rsqrt(var + LN_EPS) * g + b


def _proj_kernel(a_ref, w_ref, o_ref):
    o_ref[...] = _mm(a_ref[...].astype(MXU_DTYPE), w_ref[...]).astype(o_ref.dtype)


def project(a, w, tm=512):
    m, k = a.shape
    n = w.shape[1]
    return pl.pallas_call(
        _proj_kernel,
        grid=(m // tm,),
        in_specs=[pl.BlockSpec((tm, k), lambda i: (i, 0)), pl.BlockSpec((k, n), lambda i: (0, 0))],
        out_specs=pl.BlockSpec((tm, n), lambda i: (i, 0)),
        out_shape=jax.ShapeDtypeStruct((m, n), f32),
        compiler_params=_params("parallel"),
        name="project",
    )(a, w)


def _proj_ln_kernel(a_ref, w_ref, res_ref, g_ref, b_ref, o_ref):
    h = _mm(a_ref[...].astype(MXU_DTYPE), w_ref[...])
    o_ref[...] = _layer_norm_rows(DEEPNORM_ALPHA * res_ref[...] + h, g_ref[...], b_ref[...])


def project_residual_ln(a, w, res, g, b, tm=512):
    m, k = a.shape
    n = w.shape[1]
    return pl.pallas_call(
        _proj_ln_kernel,
        grid=(m // tm,),
        in_specs=[pl.BlockSpec((tm, k), lambda i: (i, 0)), pl.BlockSpec((k, n), lambda i: (0, 0)),
                  pl.BlockSpec((tm, n), lambda i: (i, 0)),
                  pl.BlockSpec((1, n), lambda i: (0, 0)), pl.BlockSpec((1, n), lambda i: (0, 0))],
        out_specs=pl.BlockSpec((tm, n), lambda i: (i, 0)),
        out_shape=jax.ShapeDtypeStruct((m, n), f32),
        compiler_params=_params("parallel"),
        name="project_residual_ln",
    )(a, w, res, g.reshape(1, n), b.reshape(1, n))


def _swiglu_tile(xb, w1_ref, w3_ref, w2_ref):
    a = _mm(xb, w1_ref[...])
    h = (a * jax.nn.sigmoid(a)) * _mm(xb, w3_ref[...])
    return _mm(h.astype(MXU_DTYPE), w2_ref[...])


def _ffn_ln_kernel(x_ref, w1_ref, w3_ref, w2_ref, g_ref, b_ref, o_ref, xb_ref, acc_ref):
    f = pl.program_id(1)

    @pl.when(f == 0)
    def _():
        xb_ref[...] = x_ref[...].astype(MXU_DTYPE)
        acc_ref[...] = jnp.zeros_like(acc_ref)

    acc_ref[...] += _swiglu_tile(xb_ref[...], w1_ref, w3_ref, w2_ref)

    @pl.when(f == pl.num_programs(1) - 1)
    def _():
        o_ref[...] = _layer_norm_rows(DEEPNORM_ALPHA * x_ref[...] + acc_ref[...], g_ref[...], b_ref[...])


def ffn_residual_ln(x, w1, w3, w2, g, b, tm, tf):
    m, d = x.shape
    ff = w1.shape[1]
    return pl.pallas_call(
        _ffn_ln_kernel,
        grid=(m // tm, ff // tf),
        in_specs=[pl.BlockSpec((tm, d), lambda i, f: (i, 0)),
                  pl.BlockSpec((d, tf), lambda i, f: (0, f)),
                  pl.BlockSpec((d, tf), lambda i, f: (0, f)),
                  pl.BlockSpec((tf, d), lambda i, f: (f, 0)),
                  pl.BlockSpec((1, d), lambda i, f: (0, 0)), pl.BlockSpec((1, d), lambda i, f: (0, 0))],
        out_specs=pl.BlockSpec((tm, d), lambda i, f: (i, 0)),
        out_shape=jax.ShapeDtypeStruct((m, d), f32),
        scratch_shapes=[pltpu.VMEM((tm, d), MXU_DTYPE), pltpu.VMEM((tm, d), f32)],
        compiler_params=_params("parallel", "arbitrary"),
        name="ffn_residual_ln",
    )(x, w1, w3, w2, g.reshape(1, d), b.reshape(1, d))


ROUTE_IDS = N_EXPERTS
ROUTE_GATES = N_EXPERTS + 2


def _router_kernel(x_ref, w_ref, o_ref, *, n_experts):
    logits = jnp.dot(x_ref[...], w_ref[...], preferred_element_type=f32, precision=lax.Precision.HIGHEST)
    lane = lax.broadcasted_iota(i32, logits.shape, 1).astype(f32)
    logits = jnp.where(lane < n_experts, logits, -jnp.inf)
    v1 = jnp.max(logits, axis=1, keepdims=True)
    i1 = jnp.min(jnp.where(logits == v1, lane, float(LANES)), axis=1, keepdims=True)
    rest = jnp.where(lane == i1, -jnp.inf, logits)
    v2 = jnp.max(rest, axis=1, keepdims=True)
    i2 = jnp.min(jnp.where(rest == v2, lane, float(LANES)), axis=1, keepdims=True)
    e2 = jnp.exp(v2 - v1)
    g1 = 1.0 / (1.0 + e2)
    g2 = e2 / (1.0 + e2)
    out = jnp.where(lane == ROUTE_IDS, i1, 0.0) + jnp.where(lane == ROUTE_IDS + 1, i2, 0.0)
    out = out + jnp.where(lane == ROUTE_GATES, g1, 0.0) + jnp.where(lane == ROUTE_GATES + 1, g2, 0.0)
    o_ref[...] = out


def route_top2(x, router_w, tm=512):
    m, d = x.shape
    n_experts = router_w.shape[1]
    w = jnp.zeros((d, LANES), f32).at[:, :n_experts].set(router_w)
    return pl.pallas_call(
        functools.partial(_router_kernel, n_experts=n_experts),
        grid=(m // tm,),
        in_specs=[pl.BlockSpec((tm, d), lambda i: (i, 0)), pl.BlockSpec((d, LANES), lambda i: (0, 0))],
        out_specs=pl.BlockSpec((tm, LANES), lambda i: (i, 0)),
        out_shape=jax.ShapeDtypeStruct((m, LANES), f32),
        compiler_params=_params("parallel"),
        name="route_top2",
    )(x, w)


def _moe_ln_kernel(tok_ref, gs_ref, off_ref, x_ref, w1_ref, w3_ref, w2_ref, g_ref, b_ref, o_ref,
                   xg_ref, xb_ref, y_ref, *, rt):
    c = pl.program_id(0)
    e = pl.program_id(1)
    f = pl.program_id(2)
    last_f = pl.num_programs(2) - 1
    start = off_ref[0, e]
    count = off_ref[0, e + 1] - start
    n_tiles = (count + rt - 1) // rt

    @pl.when((c == 0) & (e == 0) & (f == 0))
    def _():
        xg_ref[...] = jnp.zeros_like(xg_ref)

    @pl.when((e == 0) & (f == 0))
    def _():
        o_ref[...] = jnp.zeros_like(o_ref)

    def row_loop(body):
        def group(j, carry):
            for u in range(ROW_UNROLL):
                body(j * ROW_UNROLL + u)
            return carry

        def single(r, carry):
            body(r)
            return carry
        lax.fori_loop(0, count // ROW_UNROLL, group, 0)
        lax.fori_loop((count // ROW_UNROLL) * ROW_UNROLL, count, single, 0)

    @pl.when(f == 0)
    def _():
        def gather(r):
            t = tok_ref[0, start + r]
            xg_ref[pl.ds(r, 1), :] = x_ref[pl.ds(t, 1), :]
        row_loop(gather)

        def cast(j, carry):
            rows = pl.ds(pl.multiple_of(j * rt, rt), rt)
            xb_ref[rows, :] = xg_ref[rows, :].astype(MXU_DTYPE)
            return carry
        lax.fori_loop(0, n_tiles, cast, 0)

    def tile(j, carry):
        rows = pl.ds(pl.multiple_of(j * rt, rt), rt)
        y = _swiglu_tile(xb_ref[rows, :], w1_ref, w3_ref, w2_ref)

        @pl.when(f == 0)
        def _():
            y_ref[rows, :] = y

        @pl.when(f != 0)
        def _():
            y_ref[rows, :] += y
        return carry
    lax.fori_loop(0, n_tiles, tile, 0)

    @pl.when(f == last_f)
    def _():
        def scatter(r):
            t = tok_ref[0, start + r]
            o_ref[pl.ds(t, 1), :] += gs_ref[0, start + r] * y_ref[pl.ds(r, 1), :]
        row_loop(scatter)

    @pl.when((e == pl.num_programs(1) - 1) & (f == last_f))
    def _():
        o_ref[...] = _layer_norm_rows(DEEPNORM_ALPHA * x_ref[...] + o_ref[...], g_ref[...], b_ref[...])


def moe_residual_ln(x, routes, w1, w3, w2, g, b, tm, tf, rt=128):
    m, d = x.shape
    n_experts, _, ff = w1.shape
    n_chunks = m // tm
    ids = routes[:, ROUTE_IDS:ROUTE_IDS + TOP_K].astype(i32).reshape(n_chunks, tm * TOP_K)
    gts = routes[:, ROUTE_GATES:ROUTE_GATES + TOP_K].reshape(n_chunks, tm * TOP_K)
    order = jnp.argsort(ids, axis=1, stable=True).astype(i32)
    tok = order // TOP_K
    gs = jnp.take_along_axis(gts, order, axis=1)
    counts = jnp.sum(ids[:, :, None] == jnp.arange(n_experts, dtype=i32)[None, None, :], axis=1, dtype=i32)
    offs = jnp.concatenate([jnp.zeros((n_chunks, 1), i32), jnp.cumsum(counts, axis=1, dtype=i32)], axis=1)
    smem = lambda width: pl.BlockSpec((None, 1, width), lambda c, e, f: (c, 0, 0), memory_space=pltpu.SMEM)
    return pl.pallas_call(
        functools.partial(_moe_ln_kernel, rt=rt),
        grid=(n_chunks, n_experts, ff // tf),
        in_specs=[smem(tm * TOP_K), smem(tm * TOP_K), smem(n_experts + 1),
                  pl.BlockSpec((tm, d), lambda c, e, f: (c, 0), pipeline_mode=pl.Buffered(1)),
                  pl.BlockSpec((None, d, tf), lambda c, e, f: (e, 0, f)),
                  pl.BlockSpec((None, d, tf), lambda c, e, f: (e, 0, f)),
                  pl.BlockSpec((None, tf, d), lambda c, e, f: (e, f, 0)),
                  pl.BlockSpec((1, d), lambda c, e, f: (0, 0)), pl.BlockSpec((1, d), lambda c, e, f: (0, 0))],
        out_specs=pl.BlockSpec((tm, d), lambda c, e, f: (c, 0), pipeline_mode=pl.Buffered(1)),
        out_shape=jax.ShapeDtypeStruct((m, d), f32),
        scratch_shapes=[pltpu.VMEM((tm, d), f32), pltpu.VMEM((tm, d), MXU_DTYPE), pltpu.VMEM((tm, d), f32)],
        compiler_params=_params("arbitrary", "arbitrary", "arbitrary"),
        name="moe_residual_ln",
    )(tok[:, None, :], gs[:, None, :], offs[:, None, :], x, w1, w3, w2, g.reshape(1, d), b.reshape(1, d))


def _flash_reset(m_ref, acc_ref):
    m_ref[...] = jnp.full(m_ref.shape, NEG, f32)
    acc_ref[...] = jnp.zeros(acc_ref.shape, f32)


def _flash_rows(q, kt, bias, m_ref, alpha_ref, p_ref, rows):
    s = _mm(q, kt)
    if bias is not None:
        s = s + bias
    m_prev = m_ref[rows, :]
    m_next = jnp.maximum(m_prev, jnp.max(s, axis=1, keepdims=True))
    alpha_ref[rows, :] = jnp.exp2(m_prev - m_next)
    m_ref[rows, :] = m_next
    p_ref[rows, :] = jnp.exp2(s - _lane_tile(m_next, s.shape[1] // LANES)).astype(p_ref.dtype)


def _flash_accumulate(v_aug, alpha_ref, p_ref, acc_ref):
    alpha = _lane_tile(alpha_ref[...], acc_ref.shape[-1] // LANES)
    acc_ref[...] = alpha * acc_ref[...] + _mm(p_ref[...], v_aug)


def _skewed_chunks(n, score, fold, carry):
    def pair(j, carry):
        c = 2 * j
        fold(c - 1, 1)
        carry = score(c, 0, carry)
        fold(c, 0)
        return score(c + 1, 1, carry)

    def single(c, carry):
        fold(c - 1, 1)
        return score(c, 0, carry)

    carry = lax.fori_loop(0, n // 2, pair, carry)
    return lax.fori_loop(2 * (n // 2), n, single, carry)


def _fold_last(n, fold):
    for buf in range(2):
        pl.when((n - 1) % 2 == buf)(functools.partial(fold, n - 1, buf))


def _lane_tile(x, reps):
    return x if reps == 1 else jnp.concatenate([x] * reps, axis=1)


def _with_ones(v):
    return jnp.concatenate([v, jnp.ones_like(v)], axis=-1)


def _row_slices(n_rows):
    return [slice(r * Q_BLOCK, (r + 1) * Q_BLOCK) for r in range(n_rows // Q_BLOCK)]


def _dsa_kernel(qi_ref, wi_ref, kit_ref, qa_ref, kat_ref, va_ref, o_ref,
                skey_ref, wrep_ref, m_ref, alpha_ref, acc_ref, p_ref, qh_ref, *, topk, kc):
    qb = Q_BLOCK
    i = pl.program_id(1)
    q0 = i * qb
    n_chunks = (q0 + qb + kc - 1) // kc
    t_row = q0 + lax.broadcasted_iota(i32, (qb, kc), 0)
    col = lax.broadcasted_iota(i32, (qb, kc), 1)
    wi = wi_ref[...]
    qi = qi_ref[...]
    qis = [qi[:, h * IDX_DIM:(h + 1) * IDX_DIM] for h in range(IDX_HEADS)]
    qa = qa_ref[...]
    for h in range(A_HEADS):
        qh_ref[h * qb:(h + 1) * qb, :] = qa[:, h * HEAD_DIM:(h + 1) * HEAD_DIM]
    for h in range(IDX_HEADS):
        wrep_ref[h] = jnp.broadcast_to(wi[:, h:h + 1], (qb, kc))

    def score_body(c, carry):
        off = pl.multiple_of(c * kc, kc)
        kt = kit_ref[:, pl.ds(off, kc)]
        s = jnp.zeros((qb, kc), f32)
        for h in range(IDX_HEADS):
            s = s + wrep_ref[h] * jnp.maximum(_mm(qis[h], kt), 0.0)
        s = jnp.where(col + off <= t_row, s, NEG)
        bits = pltpu.bitcast(s, i32)
        skey_ref[:, pl.ds(off, kc)] = jnp.where(bits < 0, bits ^ 0x7FFFFFFF, bits)
        return carry

    n_strips = (q0 + qb + COUNT_STRIP - 1) // COUNT_STRIP
    lax.fori_loop(0, n_strips * (COUNT_STRIP // kc), score_body, 0)

    def count_ge(cand):
        def body(c, acc):
            off = pl.multiple_of(c * COUNT_STRIP, COUNT_STRIP)
            for j in range(COUNT_STRIP // LANES):
                acc = acc + jnp.where(skey_ref[:, pl.ds(off + j * LANES, LANES)] >= cand, 1.0, 0.0)
            return acc
        acc = lax.fori_loop(0, n_strips, body, jnp.zeros((qb, LANES), f32))
        return jnp.sum(acc, axis=1, keepdims=True)

    thr = jnp.where(count_ge(jnp.zeros((qb, 1), i32)) >= topk, 0, INT_MIN).astype(i32)

    def bisect(b, thr):
        cand = thr + jnp.left_shift(jnp.int32(1), 30 - b)
        return jnp.where(count_ge(cand) >= topk, cand, thr)

    thr = lax.fori_loop(0, 31, bisect, thr)
    quota = topk - count_ge(thr + 1)

    before = (lax.broadcasted_iota(i32, (kc, kc), 0) < lax.broadcasted_iota(i32, (kc, kc), 1))
    before = jnp.where(before, 1.0, 0.0).astype(MXU_DTYPE)
    _flash_reset(m_ref, acc_ref)
    alpha_ref[...] = jnp.zeros(alpha_ref.shape, f32)
    p_ref[...] = jnp.zeros(p_ref.shape, p_ref.dtype)
    heads = _row_slices(A_HEADS * qb)

    def fold(c, buf):
        off = pl.multiple_of(jnp.maximum(c, 0) * kc, kc)
        _flash_accumulate(va_ref[pl.ds(off, kc), :], alpha_ref.at[buf], p_ref.at[buf], acc_ref)

    def score(c, buf, ties_seen):
        off = pl.multiple_of(c * kc, kc)
        key = skey_ref[:, pl.ds(off, kc)]
        eq = key == thr
        eqf = jnp.where(eq, 1.0, 0.0)
        rank = ties_seen + _mm(eqf.astype(MXU_DTYPE), before)
        sel = ((key > thr) | (eq & (rank < quota))) & (col + off <= t_row)
        bias = jnp.where(sel, 0.0, NEG)
        kt = kat_ref[:, pl.ds(off, kc)]
        for rows in heads:
            _flash_rows(qh_ref[rows, :], kt, bias, m_ref, alpha_ref.at[buf], p_ref.at[buf], rows)
        return ties_seen + jnp.sum(eqf, axis=1, keepdims=True)

    _skewed_chunks(n_chunks, score, fold, jnp.zeros((qb, 1), f32))
    _fold_last(n_chunks, fold)
    acc = acc_ref[...]
    o = acc[:, :HEAD_DIM] / acc[:, HEAD_DIM:]
    for h in range(A_HEADS):
        o_ref[:, h * HEAD_DIM:(h + 1) * HEAD_DIM] = o[h * qb:(h + 1) * qb, :]


QPK_QA, QPK_QIDX, QPK_QB = 0, 2, 3
KPK_KA, KPK_KIDX, KPK_KSLC, KPK_KWIN = 0, 1, 2, 4
VPK_VA, VPK_VSLC, VPK_VWIN = 0, 1, 3
OUT_NSA = 2


def dsa_attention(wi, qpk, kpk, vpk, topk, kc=512):
    bsz, seq, _ = qpk.shape
    nb = seq // Q_BLOCK
    rows = A_HEADS * Q_BLOCK
    width = A_HEADS * HEAD_DIM
    kern = functools.partial(_dsa_kernel, topk=topk, kc=kc)
    return pl.pallas_call(
        kern,
        grid=(bsz, nb),
        in_specs=[pl.BlockSpec((None, Q_BLOCK, IDX_HEADS * IDX_DIM), lambda b, i: (b, i, QPK_QIDX)),
                  pl.BlockSpec((None, Q_BLOCK, IDX_HEADS), lambda b, i: (b, i, 0)),
                  pl.BlockSpec((None, IDX_DIM, seq), lambda b, i: (b, KPK_KIDX, 0)),
                  pl.BlockSpec((None, Q_BLOCK, width), lambda b, i: (b, i, QPK_QA)),
                  pl.BlockSpec((None, HEAD_DIM, seq), lambda b, i: (b, KPK_KA, 0)),
                  pl.BlockSpec((None, seq, 2 * HEAD_DIM), lambda b, i: (b, 0, VPK_VA))],
        out_specs=pl.BlockSpec((None, Q_BLOCK, width), lambda b, i: (b, i, 0)),
        out_shape=jax.ShapeDtypeStruct((bsz, seq, 2 * width), f32),
        scratch_shapes=[pltpu.VMEM((Q_BLOCK, seq), i32), pltpu.VMEM((IDX_HEADS, Q_BLOCK, kc), f32),
                        pltpu.VMEM((rows, LANES), f32), pltpu.VMEM((2, rows, LANES), f32),
                        pltpu.VMEM((rows, 2 * HEAD_DIM), f32), pltpu.VMEM((2, rows, kc), MXU_DTYPE),
                        pltpu.VMEM((rows, HEAD_DIM), MXU_DTYPE)],
        compiler_params=_params("parallel", "arbitrary"),
        name="dsa_attention",
    )(qpk, wi, kpk, qpk, kpk, vpk)


def _compress_kernel(x_ref, pe_ref, w1_ref, w2_ref, o_ref):
    half = (CMP_LEN // 2) * HEAD_DIM
    x = x_ref[...]
    first = _mm((x + pe_ref[0:1, :]).astype(MXU_DTYPE), w1_ref[0:half, :])
    second = _mm((x + pe_ref[1:2, :]).astype(MXU_DTYPE), w1_ref[half:2 * half, :])
    pre = first + pltpu.roll(second, shift=x.shape[0] - 1, axis=0)
    hid = pre * jax.nn.sigmoid(pre)
    o_ref[...] = _mm(hid.astype(MXU_DTYPE), w2_ref[...])


def nsa_compress(chunks, pe2, w1, w2):
    bsz, groups, nck, width = chunks.shape
    return pl.pallas_call(
        _compress_kernel,
        grid=(bsz, groups),
        in_specs=[pl.BlockSpec((None, None, nck, width), lambda b, g: (b, g, 0, 0)),
                  pl.BlockSpec((2, width), lambda b, g: (0, 0)),
                  pl.BlockSpec((2 * width, CMP_HIDDEN), lambda b, g: (0, 0)),
                  pl.BlockSpec((CMP_HIDDEN, HEAD_DIM), lambda b, g: (0, 0))],
        out_specs=pl.BlockSpec((None, None, nck, HEAD_DIM), lambda b, g: (b, g, 0, 0)),
        out_shape=jax.ShapeDtypeStruct((bsz, groups, nck, HEAD_DIM), f32),
        compiler_params=_params("parallel", "parallel"),
        name="nsa_compress",
    )(chunks, pe2, w1, w2)


def _nsa_kernel(qin_ref, graw_ref, kct_ref, vc_ref, kst_ref, vs_ref, kwt_ref, vw_ref, ovl_ref, _, o_ref,
                m_ref, alpha_ref, acc_ref, p_ref, pw_ref, q_ref, *, n_slc, n_sel, kc):
    qb = Q_BLOCK
    i = pl.program_id(2)
    q0 = i * qb
    n_cmp = kct_ref.shape[1]
    heads = _row_slices(B_REP * qb)
    qin = qin_ref[...]
    for r in range(B_REP):
        q_ref[r * qb:(r + 1) * qb, :] = qin[:, r * HEAD_DIM:(r + 1) * HEAD_DIM]

    t_c = q0 + lax.broadcasted_iota(i32, (qb, n_cmp), 0)
    cmp_end = lax.broadcasted_iota(i32, (qb, n_cmp), 1) * CMP_STRIDE + (CMP_LEN - 1)
    vis = cmp_end <= t_c
    kct = kct_ref[...]
    p_sum = jnp.zeros((qb, n_cmp), f32)
    for rows in heads:
        lc = jnp.where(vis, _mm(q_ref[rows, :], kct), NEG)
        ec = jnp.where(vis, jnp.exp2(lc - jnp.max(lc, axis=1, keepdims=True)), 0.0)
        den = jnp.sum(ec, axis=1, keepdims=True)
        p_c = ec / jnp.where(den > 0.0, den, 1.0)
        p_sum = p_sum + p_c
        acc_ref[rows, :HEAD_DIM] = _mm(p_c.astype(MXU_DTYPE), vc_ref[...])
    o_c = acc_ref[:, :HEAD_DIM]

    imp = jnp.dot(p_sum, ovl_ref[...], preferred_element_type=f32, precision=lax.Precision.HIGHEST).T
    t_q = q0 + lax.broadcasted_iota(i32, (LANES, qb), 1)
    blk = lax.broadcasted_iota(i32, (LANES, qb), 0)
    blk_t = t_q // SLC_LEN
    forced = (blk == 0) | (blk == blk_t) | (blk == blk_t - 1)
    imp = jnp.where(forced, FORCED_BOOST, imp)
    imp = jnp.where(blk * SLC_LEN <= t_q, imp, NEG)
    imp = jnp.where(blk < n_slc, imp, -jnp.inf)
    blk_f = blk.astype(f32)

    def pick(_, carry):
        imp, selm = carry
        best = jnp.max(imp, axis=0, keepdims=True)
        first = jnp.min(jnp.where(imp == best, blk_f, float(LANES)), axis=0, keepdims=True)
        hit = blk_f == first
        return jnp.where(hit, -jnp.inf, imp), jnp.where(hit, 1.0, selm)

    _, selm = lax.fori_loop(0, n_sel, pick, (imp, jnp.zeros((LANES, qb), f32)))
    selm = selm.T.astype(MXU_DTYPE)

    t_k = q0 + lax.broadcasted_iota(i32, (qb, kc), 0)
    col = lax.broadcasted_iota(i32, (qb, kc), 1)
    exp_row = lax.broadcasted_iota(i32, (LANES, kc), 0)
    exp_col = lax.broadcasted_iota(i32, (LANES, kc), 1)
    _flash_reset(m_ref, acc_ref)
    alpha_ref[...] = jnp.zeros(alpha_ref.shape, f32)
    p_ref[...] = jnp.zeros(p_ref.shape, p_ref.dtype)

    def fold(c, buf):
        off = pl.multiple_of(jnp.maximum(c, 0) * kc, kc)
        _flash_accumulate(vs_ref[pl.ds(off, kc), :], alpha_ref.at[buf], p_ref.at[buf], acc_ref)

    def score(c, buf, carry):
        off = pl.multiple_of(c * kc, kc)
        expand = jnp.where(exp_row == (exp_col + off) // SLC_LEN, 1.0, 0.0).astype(MXU_DTYPE)
        sel = (_mm(selm, expand) > 0.5) & (col + off <= t_k)
        bias = jnp.where(sel, 0.0, NEG)
        kt = kst_ref[:, pl.ds(off, kc)]
        for rows in heads:
            _flash_rows(q_ref[rows, :], kt, bias, m_ref, alpha_ref.at[buf], p_ref.at[buf], rows)
        return carry

    n_slc_chunks = (q0 + qb + kc - 1) // kc
    _skewed_chunks(n_slc_chunks, score, fold, 0)
    _fold_last(n_slc_chunks, fold)
    acc = acc_ref[...]
    o_s = acc[:, :HEAD_DIM] / acc[:, HEAD_DIM:]

    slab = WINDOW + qb
    w0 = pl.multiple_of(jnp.maximum(q0 - WINDOW, 0), qb)
    dist = (q0 + lax.broadcasted_iota(i32, (qb, slab), 0)) - (w0 + lax.broadcasted_iota(i32, (qb, slab), 1))
    bias = jnp.where((dist >= 0) & (dist < WINDOW), 0.0, NEG)
    kt = kwt_ref[:, pl.ds(w0, slab)]
    for rows in heads:
        s = _mm(q_ref[rows, :], kt) + bias
        pw_ref[rows, :] = jnp.exp2(s - jnp.max(s, axis=1, keepdims=True)).astype(pw_ref.dtype)
    acc = _mm(pw_ref[...], vw_ref[pl.ds(w0, slab), :])
    o_w = acc[:, :HEAD_DIM] / acc[:, HEAD_DIM:]

    gates = jax.nn.sigmoid(graw_ref[...])
    for r, rows in enumerate(heads):
        g_c, g_s, g_w = (gates[:, 3 * r + n:3 * r + n + 1] for n in range(3))
        o_ref[:, r * HEAD_DIM:(r + 1) * HEAD_DIM] = g_c * o_c[rows] + g_s * o_s[rows] + g_w * o_w[rows]


def nsa_attention(qpk, graw, kct, vc, kpk, vpk, ovl, out_ab, n_slc, n_sel, kc=512):
    bsz, seq, _ = qpk.shape
    groups = B_KV_GROUPS
    nb = seq // Q_BLOCK
    rows = B_REP * Q_BLOCK
    width = B_REP * HEAD_DIM
    n_cmp = kct.shape[-1]
    kern = functools.partial(_nsa_kernel, n_slc=n_slc, n_sel=n_sel, kc=kc)
    per_bg = lambda b, g, i: (b, g, 0, 0)
    return pl.pallas_call(
        kern,
        grid=(bsz, groups, nb),
        in_specs=[pl.BlockSpec((None, Q_BLOCK, width), lambda b, g, i: (b, i, QPK_QB + g)),
                  pl.BlockSpec((None, None, Q_BLOCK, B_REP * 3), lambda b, g, i: (b, g, i, 0)),
                  pl.BlockSpec((None, None, HEAD_DIM, n_cmp), per_bg),
                  pl.BlockSpec((None, None, n_cmp, HEAD_DIM), per_bg),
                  pl.BlockSpec((None, HEAD_DIM, seq), lambda b, g, i: (b, KPK_KSLC + g, 0)),
                  pl.BlockSpec((None, seq, 2 * HEAD_DIM), lambda b, g, i: (b, 0, VPK_VSLC + g)),
                  pl.BlockSpec((None, HEAD_DIM, seq), lambda b, g, i: (b, KPK_KWIN + g, 0)),
                  pl.BlockSpec((None, seq, 2 * HEAD_DIM), lambda b, g, i: (b, 0, VPK_VWIN + g)),
                  pl.BlockSpec((n_cmp, LANES), lambda b, g, i: (0, 0)),
                  pl.BlockSpec(memory_space=pl.ANY)],
        out_specs=pl.BlockSpec((None, Q_BLOCK, width), lambda b, g, i: (b, i, OUT_NSA + g)),
        out_shape=jax.ShapeDtypeStruct(out_ab.shape, f32),
        input_output_aliases={9: 0},
        scratch_shapes=[pltpu.VMEM((rows, LANES), f32), pltpu.VMEM((2, rows, LANES), f32),
                        pltpu.VMEM((rows, 2 * HEAD_DIM), f32), pltpu.VMEM((2, rows, kc), MXU_DTYPE),
                        pltpu.VMEM((rows, WINDOW + Q_BLOCK), MXU_DTYPE), pltpu.VMEM((rows, HEAD_DIM), MXU_DTYPE)],
        compiler_params=_params("parallel", "parallel", "arbitrary"),
        name="nsa_attention",
    )(qpk, graw, kct, vc, kpk, vpk, kpk, vpk, ovl, out_ab)


def _diff_kernel(lam_ref, qin_ref, kt_ref, v_ref, g_ref, o_ref, m_ref, alpha_ref, acc_ref, p_ref, q_ref,
                 *, tq, kc, out_scale):
    i = pl.program_id(2)
    q0 = i * tq
    dv = v_ref.shape[-1] // 2
    groups = _row_slices(tq)
    qin = qin_ref[...]
    for half in range(2):
        q_ref[half] = qin[:, half * HEAD_DIM:(half + 1) * HEAD_DIM]
    col = lax.broadcasted_iota(i32, (Q_BLOCK, kc), 1)
    t_row = [q0 + r * Q_BLOCK + lax.broadcasted_iota(i32, (Q_BLOCK, kc), 0) for r in range(len(groups))]
    for half in range(2):
        _flash_reset(m_ref.at[half], acc_ref.at[half])
    alpha_ref[...] = jnp.zeros(alpha_ref.shape, f32)
    p_ref[...] = jnp.zeros(p_ref.shape, p_ref.dtype)

    def fold(c, buf):
        off = pl.multiple_of(jnp.maximum(c, 0) * kc, kc)
        v = v_ref[pl.ds(off, kc), :]
        for half in range(2):
            _flash_accumulate(v, alpha_ref.at[buf, half], p_ref.at[buf, half], acc_ref.at[half])

    def score(c, buf, masked):
        off = pl.multiple_of(c * kc, kc)
        for half in range(2):
            kt = kt_ref[half * HEAD_DIM:(half + 1) * HEAD_DIM, pl.ds(off, kc)]
            for r, rows in enumerate(groups):
                bias = jnp.where(col + off <= t_row[r], 0.0, NEG) if masked else None
                _flash_rows(q_ref[half, rows, :], kt, bias, m_ref.at[half], alpha_ref.at[buf, half],
                            p_ref.at[buf, half], rows)

    def score_full(c, buf, carry):
        score(c, buf, False)
        return carry

    n_full = q0 // kc
    _skewed_chunks(n_full, score_full, fold, 0)

    def finish(buf):
        fold(n_full - 1, 1 - buf)
        score(n_full, buf, True)
        fold(n_full, buf)

    for buf in range(2):
        pl.when(n_full % 2 == buf)(functools.partial(finish, buf))
    a1 = acc_ref[0]
    a2 = acc_ref[1]
    o = a1[:, :dv] / a1[:, dv:] - lam_ref[0] * (a2[:, :dv] / a2[:, dv:])
    o = o * lax.rsqrt(jnp.mean(o * o, axis=-1, keepdims=True) + LN_EPS) * g_ref[...]
    o_ref[...] = o * out_scale


def diff_attention(lam, q, kt, v, subln_g, out_scale, tq=512, kc=1024):
    bsz, seq, width = q.shape
    dv = 2 * HEAD_DIM
    heads = width // dv
    tq = min(tq, seq)
    kc = min(kc, seq)
    assert kc % tq == 0 and seq % kc == 0
    kern = functools.partial(_diff_kernel, tq=tq, kc=kc, out_scale=out_scale)
    return pl.pallas_call(
        kern,
        grid=(bsz, heads, seq // tq),
        in_specs=[pl.BlockSpec(memory_space=pltpu.SMEM),
                  pl.BlockSpec((None, tq, dv), lambda b, h, i: (b, i, h)),
                  pl.BlockSpec((None, dv, seq), lambda b, h, i: (b, h, 0)),
                  pl.BlockSpec((None, seq, 2 * dv), lambda b, h, i: (b, 0, h)),
                  pl.BlockSpec((1, dv), lambda b, h, i: (0, 0))],
        out_specs=pl.BlockSpec((None, tq, dv), lambda b, h, i: (b, i, h)),
        out_shape=jax.ShapeDtypeStruct((bsz, seq, heads * dv), f32),
        scratch_shapes=[pltpu.VMEM((2, tq, LANES), f32), pltpu.VMEM((2, 2, tq, LANES), f32),
                        pltpu.VMEM((2, tq, 2 * dv), f32), pltpu.VMEM((2, 2, tq, kc), MXU_DTYPE),
                        pltpu.VMEM((2, tq, HEAD_DIM), MXU_DTYPE)],
        compiler_params=_params("parallel", "parallel", "arbitrary"),
        name="diff_attention",
    )(lam, q, kt, v, subln_g.reshape(1, dv))


def _rope_tables(positions):
    inv_freq = ROPE_THETA ** (-jnp.arange(0, ROT_DIM, 2, dtype=f32) / ROT_DIM)
    ang = positions.astype(f32)[..., None] * inv_freq
    return jnp.cos(ang), jnp.sin(ang)


def _apply_rope(x, cos, sin):
    shape = cos.shape[:2] + (1,) * (x.ndim - 3) + cos.shape[-1:]
    c = cos.reshape(shape)
    s = sin.reshape(shape)
    half = ROT_DIM // 2
    x1, x2 = x[..., :half], x[..., half:ROT_DIM]
    return jnp.concatenate([x1 * c - x2 * s, x2 * c + x1 * s, x[..., ROT_DIM:]], axis=-1)


def _rope_heads_flat(x, cos, sin, col_scale=None):
    heads = x.shape[-1] // HEAD_DIM
    half = ROT_DIM // 2
    pad = jnp.zeros(cos.shape[:-1] + (HEAD_DIM - ROT_DIM,), f32)
    zero = jnp.zeros_like(sin)
    keep = jnp.tile(jnp.concatenate([cos, cos, pad + 1.0], axis=-1), (1, 1, heads))
    from_hi = jnp.tile(jnp.concatenate([-sin, zero, pad], axis=-1), (1, 1, heads))
    from_lo = jnp.tile(jnp.concatenate([zero, sin, pad], axis=-1), (1, 1, heads))
    y = x * keep + jnp.roll(x, -half, axis=-1) * from_hi + jnp.roll(x, half, axis=-1) * from_lo
    return y if col_scale is None else y * col_scale


def _pad_cols(w, mult):
    n = w.shape[-1]
    pad = (-n) % mult
    return jnp.pad(w, ((0, 0), (0, pad))) if pad else w


def _overlap_matrix(n_cmp_rows, n_slc):
    c_start = np.arange(n_cmp_rows) * CMP_STRIDE
    s_start = np.arange(LANES) * SLC_LEN
    ovl = (c_start[:, None] < s_start[None, :] + SLC_LEN) & (c_start[:, None] + CMP_LEN > s_start[None, :])
    ovl = ovl & (np.arange(LANES)[None, :] < n_slc)
    return jnp.asarray(ovl.astype(np.float32))


def _ab_mixer(x2, bsz, seq, positions, cos, sin, w_in, pe_k, pe_v, ck1, ck2, cv1, cv2):
    nb = seq // Q_BLOCK
    widths = dict(AB_LAYOUT)
    starts = dict(zip(widths, np.cumsum([0] + [w for _, w in AB_LAYOUT[:-1]]).tolist()))
    perm = np.concatenate([np.arange(starts[n], starts[n] + widths[n]) for n in AB_PACKED])
    h = project(x2, _pad_cols(w_in[:, perm], 2 * LANES).astype(MXU_DTYPE)).reshape(bsz, seq, -1)
    ends = np.cumsum([widths[n] for n in AB_PACKED]).tolist()
    cut = dict(zip(AB_PACKED, ends))
    q_all = h[..., :cut["q_b"]]
    k_all = h[..., cut["q_b"]:cut["k_win"]]
    v_all = h[..., cut["k_win"]:cut["v_win"]]
    k_cmp = h[..., cut["v_win"]:cut["k_cmp"]]
    v_cmp = h[..., cut["k_cmp"]:cut["v_cmp"]]
    g_b = h[..., cut["v_cmp"]:cut["gate_b"]]
    w_idx = h[..., cut["gate_b"]:cut["w_idx"]] * (IDX_HEADS * IDX_DIM) ** -0.5

    q_scale = np.concatenate([np.full(widths[n], 1.0 if n == "q_idx" else QK_SCALE, np.float32)
                              for n in ("q_a", "q_idx", "q_b")])
    qpk = _rope_heads_flat(q_all, cos, sin, q_scale).astype(MXU_DTYPE)
    kpk = _rope_heads_flat(k_all, cos, sin).transpose(0, 2, 1).astype(MXU_DTYPE)
    vpk = _with_ones(v_all.reshape(bsz, seq, -1, HEAD_DIM).astype(MXU_DTYPE)).reshape(bsz, seq, -1)

    out_ab = dsa_attention(w_idx, qpk, kpk, vpk, topk=min(DSA_TOPK, seq // 4))

    groups = B_KV_GROUPS
    n_rows = seq // CMP_STRIDE
    n_cmp = (seq - CMP_LEN) // CMP_STRIDE + 1
    assert n_cmp == n_rows - 1
    n_slc = seq // SLC_LEN
    assert n_slc <= LANES

    def chunked(kv):
        kv = kv.reshape(bsz, n_rows, CMP_STRIDE, groups, HEAD_DIM).transpose(0, 3, 1, 2, 4)
        return kv.reshape(bsz, groups, n_rows, CMP_STRIDE * HEAD_DIM)

    def pe_halves(pe):
        return pe.reshape(2, CMP_STRIDE * HEAD_DIM)

    k_c = nsa_compress(chunked(k_cmp), pe_halves(pe_k), ck1.astype(MXU_DTYPE), ck2.astype(MXU_DTYPE))
    v_c = nsa_compress(chunked(v_cmp), pe_halves(pe_v), cv1.astype(MXU_DTYPE), cv2.astype(MXU_DTYPE))
    cmp_end = jnp.minimum(jnp.arange(n_rows) * CMP_STRIDE + CMP_LEN - 1, seq - 1)
    cos_c, sin_c = _rope_tables(positions[:, cmp_end])
    k_c = _apply_rope(k_c.transpose(0, 2, 1, 3), cos_c, sin_c)
    k_c = k_c.transpose(0, 2, 3, 1).astype(MXU_DTYPE)
    v_c = v_c.astype(MXU_DTYPE)

    graw = g_b.reshape(bsz, seq, groups, B_REP * 3).transpose(0, 2, 1, 3)
    out_ab = nsa_attention(qpk, graw, k_c, v_c, kpk, vpk, _overlap_matrix(n_rows, n_slc), out_ab,
                           n_slc=n_slc, n_sel=min(SLC_TOPN, n_slc))
    return out_ab.reshape(bsz * seq, (A_HEADS + B_HEADS) * HEAD_DIM)


def _diff_mixer(x2, bsz, seq, cos, sin, w_in, lq1, lk1, lq2, lk2, subln_g, lam_init):
    h = project(x2, w_in.astype(MXU_DTYPE)).reshape(bsz, seq, -1)
    cw = C_HEADS * 2 * HEAD_DIM
    q, k, v = h[..., :cw], h[..., cw:2 * cw], h[..., 2 * cw:]

    q = _rope_heads_flat(q, cos, sin, QK_SCALE).astype(MXU_DTYPE)
    kt = _rope_heads_flat(k, cos, sin).transpose(0, 2, 1).astype(MXU_DTYPE)
    v = _with_ones(v.reshape(bsz, seq, C_HEADS, 2 * HEAD_DIM).astype(MXU_DTYPE)).reshape(bsz, seq, -1)
    lam = (jnp.exp(jnp.sum(lq1 * lk1)) - jnp.exp(jnp.sum(lq2 * lk2)) + lam_init).reshape(1).astype(f32)
    o = diff_attention(lam, q, kt, v, subln_g, 1.0 - lam_init)
    return o.reshape(bsz * seq, C_HEADS * 2 * HEAD_DIM)


def kernel(x, positions, ab_w_in, cmp_pe_k, cmp_pe_v, cmp_k_w1, cmp_k_w2, cmp_v_w1, cmp_v_w2, ab_w_out, ln_ab_g, ln_ab_b, ffn_w1, ffn_w3, ffn_w2, ln_ffn_g, ln_ffn_b, c_w_in, lambda_q1, lambda_k1, lambda_q2, lambda_k2, c_subln_g, c_w_out, ln_c_g, ln_c_b, router_w, moe_w1, moe_w3, moe_w2, ln_moe_g, ln_moe_b):
    bsz, seq, d = x.shape
    assert seq % COUNT_STRIP == 0 and seq >= WINDOW + Q_BLOCK and d == D_MODEL
    cos, sin = _rope_tables(positions)
    x2 = x.reshape(bsz * seq, d)
    for layer in range(DEPTH):
        i = layer // 2
        if layer % 2 == 0:
            o = _ab_mixer(x2, bsz, seq, positions, cos, sin, ab_w_in[i], cmp_pe_k[i], cmp_pe_v[i],
                          cmp_k_w1[i], cmp_k_w2[i], cmp_v_w1[i], cmp_v_w2[i])
            x2 = project_residual_ln(o, ab_w_out[i].astype(MXU_DTYPE), x2, ln_ab_g[i], ln_ab_b[i])
            x2 = ffn_residual_ln(x2, ffn_w1[i].astype(MXU_DTYPE), ffn_w3[i].astype(MXU_DTYPE),
                                 ffn_w2[i].astype(MXU_DTYPE), ln_ffn_g[i], ln_ffn_b[i], tm=512, tf=1408)
        else:
            lam_init = 0.8 - 0.6 * math.exp(-0.3 * layer)
            o = _diff_mixer(x2, bsz, seq, cos, sin, c_w_in[i], lambda_q1[i], lambda_k1[i], lambda_q2[i],
                            lambda_k2[i], c_subln_g[i], lam_init)
            x2 = project_residual_ln(o, c_w_out[i].astype(MXU_DTYPE), x2, ln_c_g[i], ln_c_b[i])
            routes = route_top2(x2, router_w[i])
            x2 = moe_residual_ln(x2, routes, moe_w1[i].astype(MXU_DTYPE), moe_w3[i].astype(MXU_DTYPE),
                                 moe_w2[i].astype(MXU_DTYPE), ln_moe_g[i], ln_moe_b[i], tm=min(2048, bsz * seq // 2), tf=896, rt=128)
    return x2.reshape(bsz, seq, d)
```

```python
import functools
import math

import numpy as np
import jax
import jax.numpy as jnp
from jax import lax
from jax.experimental import pallas as pl
from jax.experimental.pallas import tpu as pltpu

f32 = jnp.float32
i32 = jnp.int32
MXU_DTYPE = jnp.bfloat16
VMEM_LIMIT_BYTES = 56 * 1024 * 1024
LANES = 128

D_MODEL = 1024
DEPTH = 2
HEAD_DIM = 64
ROT_DIM = HEAD_DIM // 4
ROPE_THETA = 500000.0
Q_BLOCK = 128
NEG = -1e30
LN_EPS = 1e-5
A_HEADS = 8
IDX_HEADS = 4
IDX_DIM = 64
DSA_TOPK = 256
B_HEADS = 8
B_KV_GROUPS = 2
B_REP = B_HEADS // B_KV_GROUPS
CMP_LEN = 32
CMP_STRIDE = 16
CMP_HIDDEN = 128
SLC_LEN = 64
SLC_TOPN = 16
WINDOW = 512
FORCED_BOOST = 1e6
C_HEADS = 8
N_EXPERTS = 8
TOP_K = 2
DEEPNORM_ALPHA = (2 * DEPTH) ** 0.25
QK_SCALE = HEAD_DIM ** -0.5 * math.log2(math.e)
INT_MIN = -(2 ** 31)
COUNT_STRIP = 512
ROW_UNROLL = 4

AB_LAYOUT = (
    ("q_a", A_HEADS * HEAD_DIM), ("k_a", HEAD_DIM), ("v_a", HEAD_DIM),
    ("q_idx", IDX_HEADS * IDX_DIM), ("k_idx", IDX_DIM), ("w_idx", IDX_HEADS),
    ("q_b", B_HEADS * HEAD_DIM),
    ("k_cmp", B_KV_GROUPS * HEAD_DIM), ("v_cmp", B_KV_GROUPS * HEAD_DIM),
    ("k_slc", B_KV_GROUPS * HEAD_DIM), ("v_slc", B_KV_GROUPS * HEAD_DIM),
    ("k_win", B_KV_GROUPS * HEAD_DIM), ("v_win", B_KV_GROUPS * HEAD_DIM),
    ("gate_b", 3 * B_HEADS),
)


def _params(*sem):
    return pltpu.CompilerParams(dimension_semantics=sem, vmem_limit_bytes=VMEM_LIMIT_BYTES)


def _mm(a, b):
    return jnp.dot(a, b, preferred_element_type=f32)


def _layer_norm_rows(y, g, b):
    mu = jnp.mean(y, axis=-1, keepdims=True)
    yc = y - mu
    var = jnp.mean(yc * yc, axis=-1, keepdims=True)
    return yc * lax.rsqrt(var + LN_EPS) * g + b


def _rope_tile(x, keep, hi, lo):
    return x * keep + pltpu.roll(x, LANES - ROT_DIM // 2, 1) * hi + pltpu.roll(x, ROT_DIM // 2, 1) * lo


def _tile(h, j):
    return h[:, j * LANES:(j + 1) * LANES]


AB_Q_TILES = 10
AB_QIDX_TILES = (4, 5)
AB_K_TILES = 3
AB_V_HEADS = 5
AB_MISC_TILE = 15
AB_CMP_TILE = 16
AB_TILES = 18


def _proj_pack_ab_kernel(x_ref, w_ref, keep_ref, hi_ref, lo_ref, q_ref, kt_ref, v_ref, misc_ref, cmp_ref):
    h = _mm(x_ref[...].astype(MXU_DTYPE), w_ref[...])
    keep, hi, lo = keep_ref[...], hi_ref[...], lo_ref[...]
    for j in range(AB_Q_TILES):
        t = _rope_tile(_tile(h, j), keep, hi, lo)
        q_ref[:, j * LANES:(j + 1) * LANES] = (t if j in AB_QIDX_TILES else t * QK_SCALE).astype(q_ref.dtype)
    k = jnp.concatenate([_rope_tile(_tile(h, AB_Q_TILES + j), keep, hi, lo) for j in range(AB_K_TILES)], axis=1)
    kt_ref[...] = k.T.astype(kt_ref.dtype)
    low = lax.broadcasted_iota(i32, keep.shape, 1) < HEAD_DIM
    for j in range(AB_V_HEADS):
        t = _tile(h, AB_Q_TILES + AB_K_TILES + j // 2)
        t = pltpu.roll(t, HEAD_DIM, 1) if j % 2 else t
        v_ref[:, j * LANES:(j + 1) * LANES] = jnp.where(low, t, 1.0).astype(v_ref.dtype)
    misc_ref[...] = _tile(h, AB_MISC_TILE)
    cmp_ref[...] = h[:, AB_CMP_TILE * LANES:AB_TILES * LANES]


def project_pack_ab(x, w, coef, tm=512):
    bsz, seq, d = x.shape
    row = lambda width: pl.BlockSpec((None, tm, width), lambda b, i: (b, i, 0))
    kt_rows = AB_K_TILES * LANES
    return pl.pallas_call(
        _proj_pack_ab_kernel,
        grid=(bsz, seq // tm),
        in_specs=[row(d), pl.BlockSpec((d, AB_TILES * LANES), lambda b, i: (0, 0)), row(LANES), row(LANES), row(LANES)],
        out_specs=[row(AB_Q_TILES * LANES), pl.BlockSpec((None, kt_rows, tm), lambda b, i: (b, 0, i)),
                   row(AB_V_HEADS * LANES), row(LANES), row(2 * LANES)],
        out_shape=[jax.ShapeDtypeStruct((bsz, seq, AB_Q_TILES * LANES), MXU_DTYPE),
                   jax.ShapeDtypeStruct((bsz, kt_rows, seq), MXU_DTYPE),
                   jax.ShapeDtypeStruct((bsz, seq, AB_V_HEADS * LANES), MXU_DTYPE),
                   jax.ShapeDtypeStruct((bsz, seq, LANES), f32),
                   jax.ShapeDtypeStruct((bsz, seq, 2 * LANES), f32)],
        compiler_params=_params("parallel", "parallel"),
        name="project_pack_ab",
    )(x, w, *coef)


def _proj_pack_c_kernel(x_ref, w_ref, keep_ref, hi_ref, lo_ref, q_ref, kt_ref, v_ref):
    h = _mm(x_ref[...].astype(MXU_DTYPE), w_ref[...])
    keep, hi, lo = keep_ref[...], hi_ref[...], lo_ref[...]
    n = q_ref.shape[1] // LANES
    for j in range(n):
        q_ref[:, j * LANES:(j + 1) * LANES] = (_rope_tile(_tile(h, j), keep, hi, lo) * QK_SCALE).astype(q_ref.dtype)
    k = jnp.concatenate([_rope_tile(_tile(h, n + j), keep, hi, lo) for j in range(n)], axis=1)
    kt_ref[...] = k.T.astype(kt_ref.dtype)
    ones = jnp.ones(keep.shape, v_ref.dtype)
    for j in range(n):
        v_ref[:, 2 * j * LANES:(2 * j + 1) * LANES] = _tile(h, 2 * n + j).astype(v_ref.dtype)
        v_ref[:, (2 * j + 1) * LANES:(2 * j + 2) * LANES] = ones


def project_pack_c(x, w, coef, tm=512):
    bsz, seq, d = x.shape
    width = w.shape[1] // 3
    row = lambda cols: pl.BlockSpec((None, tm, cols), lambda b, i: (b, i, 0))
    return pl.pallas_call(
        _proj_pack_c_kernel,
        grid=(bsz, seq // tm),
        in_specs=[row(d), pl.BlockSpec((d, 3 * width), lambda b, i: (0, 0)), row(LANES), row(LANES), row(LANES)],
        out_specs=[row(width), pl.BlockSpec((None, width, tm), lambda b, i: (b, 0, i)), row(2 * width)],
        out_shape=[jax.ShapeDtypeStruct((bsz, seq, width), MXU_DTYPE),
                   jax.ShapeDtypeStruct((bsz, width, seq), MXU_DTYPE),
                   jax.ShapeDtypeStruct((bsz, seq, 2 * width), MXU_DTYPE)],
        compiler_params=_params("parallel", "parallel"),
        name="project_pack_c",
    )(x, w, *coef)


def _proj_ln_kernel(a_ref, w_ref, res_ref, g_ref, b_ref, o_ref):
    h = _mm(a_ref[...].astype(MXU_DTYPE), w_ref[...])
    o_ref[...] = _layer_norm_rows(DEEPNORM_ALPHA * res_ref[...] + h, g_ref[...], b_ref[...])


def project_residual_ln(a, w, res, g, b, tm=512):
    m, k = a.shape
    n = w.shape[1]
    return pl.pallas_call(
        _proj_ln_kernel,
        grid=(m // tm,),
        in_specs=[pl.BlockSpec((tm, k), lambda i: (i, 0)), pl.BlockSpec((k, n), lambda i: (0, 0)),
                  pl.BlockSpec((tm, n), lambda i: (i, 0)),
                  pl.BlockSpec((1, n), lambda i: (0, 0)), pl.BlockSpec((1, n), lambda i: (0, 0))],
        out_specs=pl.BlockSpec((tm, n), lambda i: (i, 0)),
        out_shape=jax.ShapeDtypeStruct((m, n), f32),
        compiler_params=_params("parallel"),
        name="project_residual_ln",
    )(a, w, res, g.reshape(1, n), b.reshape(1, n))


def _swiglu_tile(xb, w1_ref, w3_ref, w2_ref):
    a = _mm(xb, w1_ref[...])
    h = (a * jax.nn.sigmoid(a)) * _mm(xb, w3_ref[...])
    return _mm(h.astype(MXU_DTYPE), w2_ref[...])


def _ffn_ln_kernel(x_ref, w1_ref, w3_ref, w2_ref, g_ref, b_ref, o_ref, xb_ref, acc_ref):
    f = pl.program_id(1)

    @pl.when(f == 0)
    def _():
        xb_ref[...] = x_ref[...].astype(MXU_DTYPE)
        acc_ref[...] = jnp.zeros_like(acc_ref)

    acc_ref[...] += _swiglu_tile(xb_ref[...], w1_ref, w3_ref, w2_ref)

    @pl.when(f == pl.num_programs(1) - 1)
    def _():
        o_ref[...] = _layer_norm_rows(DEEPNORM_ALPHA * x_ref[...] + acc_ref[...], g_ref[...], b_ref[...])


def ffn_residual_ln(x, w1, w3, w2, g, b, tm, tf):
    m, d = x.shape
    ff = w1.shape[1]
    return pl.pallas_call(
        _ffn_ln_kernel,
        grid=(m // tm, ff // tf),
        in_specs=[pl.BlockSpec((tm, d), lambda i, f: (i, 0)),
                  pl.BlockSpec((d, tf), lambda i, f: (0, f)),
                  pl.BlockSpec((d, tf), lambda i, f: (0, f)),
                  pl.BlockSpec((tf, d), lambda i, f: (f, 0)),
                  pl.BlockSpec((1, d), lambda i, f: (0, 0)), pl.BlockSpec((1, d), lambda i, f: (0, 0))],
        out_specs=pl.BlockSpec((tm, d), lambda i, f: (i, 0)),
        out_shape=jax.ShapeDtypeStruct((m, d), f32),
        scratch_shapes=[pltpu.VMEM((tm, d), MXU_DTYPE), pltpu.VMEM((tm, d), f32)],
        compiler_params=_params("parallel", "arbitrary"),
        name="ffn_residual_ln",
    )(x, w1, w3, w2, g.reshape(1, d), b.reshape(1, d))


ROUTE_IDS = N_EXPERTS
ROUTE_GATES = N_EXPERTS + 2


def _router_kernel(x_ref, w_ref, o_ref, *, n_experts):
    logits = jnp.dot(x_ref[...], w_ref[...], preferred_element_type=f32, precision=lax.Precision.HIGHEST)
    lane = lax.broadcasted_iota(i32, logits.shape, 1).astype(f32)
    logits = jnp.where(lane < n_experts, logits, -jnp.inf)
    v1 = jnp.max(logits, axis=1, keepdims=True)
    i1 = jnp.min(jnp.where(logits == v1, lane, float(LANES)), axis=1, keepdims=True)
    rest = jnp.where(lane == i1, -jnp.inf, logits)
    v2 = jnp.max(rest, axis=1, keepdims=True)
    i2 = jnp.min(jnp.where(rest == v2, lane, float(LANES)), axis=1, keepdims=True)
    e2 = jnp.exp(v2 - v1)
    g1 = 1.0 / (1.0 + e2)
    g2 = e2 / (1.0 + e2)
    out = jnp.where(lane == ROUTE_IDS, i1, 0.0) + jnp.where(lane == ROUTE_IDS + 1, i2, 0.0)
    out = out + jnp.where(lane == ROUTE_GATES, g1, 0.0) + jnp.where(lane == ROUTE_GATES + 1, g2, 0.0)
    o_ref[...] = out


def route_top2(x, router_w, tm=512):
    m, d = x.shape
    n_experts = router_w.shape[1]
    w = jnp.zeros((d, LANES), f32).at[:, :n_experts].set(router_w)
    return pl.pallas_call(
        functools.partial(_router_kernel, n_experts=n_experts),
        grid=(m // tm,),
        in_specs=[pl.BlockSpec((tm, d), lambda i: (i, 0)), pl.BlockSpec((d, LANES), lambda i: (0, 0))],
        out_specs=pl.BlockSpec((tm, LANES), lambda i: (i, 0)),
        out_shape=jax.ShapeDtypeStruct((m, LANES), f32),
        compiler_params=_params("parallel"),
        name="route_top2",
    )(x, w)


def _moe_ln_kernel(tok_ref, gs_ref, off_ref, x_ref, w1_ref, w3_ref, w2_ref, g_ref, b_ref, o_ref,
                   xg_ref, xb_ref, y_ref, *, rt):
    c = pl.program_id(0)
    e = pl.program_id(1)
    f = pl.program_id(2)
    last_f = pl.num_programs(2) - 1
    start = off_ref[0, e]
    count = off_ref[0, e + 1] - start
    n_tiles = (count + rt - 1) // rt

    @pl.when((c == 0) & (e == 0) & (f == 0))
    def _():
        xg_ref[...] = jnp.zeros_like(xg_ref)

    @pl.when((e == 0) & (f == 0))
    def _():
        o_ref[...] = jnp.zeros_like(o_ref)

    def row_loop(body):
        def group(j, carry):
            for u in range(ROW_UNROLL):
                body(j * ROW_UNROLL + u)
            return carry

        def single(r, carry):
            body(r)
            return carry
        lax.fori_loop(0, count // ROW_UNROLL, group, 0)
        lax.fori_loop((count // ROW_UNROLL) * ROW_UNROLL, count, single, 0)

    @pl.when(f == 0)
    def _():
        def gather(r):
            t = tok_ref[0, start + r]
            xg_ref[pl.ds(r, 1), :] = x_ref[pl.ds(t, 1), :]
        row_loop(gather)

        def cast(j, carry):
            rows = pl.ds(pl.multiple_of(j * rt, rt), rt)
            xb_ref[rows, :] = xg_ref[rows, :].astype(MXU_DTYPE)
            return carry
        lax.fori_loop(0, n_tiles, cast, 0)

    def tile(j, carry):
        rows = pl.ds(pl.multiple_of(j * rt, rt), rt)
        y = _swiglu_tile(xb_ref[rows, :], w1_ref, w3_ref, w2_ref)

        @pl.when(f == 0)
        def _():
            y_ref[rows, :] = y

        @pl.when(f != 0)
        def _():
            y_ref[rows, :] += y
        return carry
    lax.fori_loop(0, n_tiles, tile, 0)

    @pl.when(f == last_f)
    def _():
        def updated(r):
            t = tok_ref[0, start + r]
            return t, o_ref[pl.ds(t, 1), :] + gs_ref[0, start + r] * y_ref[pl.ds(r, 1), :]

        def group(j, carry):
            rows = [updated(j * ROW_UNROLL + u) for u in range(ROW_UNROLL)]
            for t, row in rows:
                o_ref[pl.ds(t, 1), :] = row
            return carry

        def single(r, carry):
            t, row = updated(r)
            o_ref[pl.ds(t, 1), :] = row
            return carry
        lax.fori_loop(0, count // ROW_UNROLL, group, 0)
        lax.fori_loop((count // ROW_UNROLL) * ROW_UNROLL, count, single, 0)

    @pl.when((e == pl.num_programs(1) - 1) & (f == last_f))
    def _():
        o_ref[...] = _layer_norm_rows(DEEPNORM_ALPHA * x_ref[...] + o_ref[...], g_ref[...], b_ref[...])


def moe_residual_ln(x, routes, w1, w3, w2, g, b, tm, tf, rt=128):
    m, d = x.shape
    n_experts, _, ff = w1.shape
    n_chunks = m // tm
    ids = routes[:, ROUTE_IDS:ROUTE_IDS + TOP_K].astype(i32).reshape(n_chunks, tm * TOP_K)
    gts = routes[:, ROUTE_GATES:ROUTE_GATES + TOP_K].reshape(n_chunks, tm * TOP_K)
    order = jnp.argsort(ids, axis=1, stable=True).astype(i32)
    tok = order // TOP_K
    gs = jnp.take_along_axis(gts, order, axis=1)
    counts = jnp.sum(ids[:, :, None] == jnp.arange(n_experts, dtype=i32)[None, None, :], axis=1, dtype=i32)
    offs = jnp.concatenate([jnp.zeros((n_chunks, 1), i32), jnp.cumsum(counts, axis=1, dtype=i32)], axis=1)
    smem = lambda width: pl.BlockSpec((None, 1, width), lambda c, e, f: (c, 0, 0), memory_space=pltpu.SMEM)
    return pl.pallas_call(
        functools.partial(_moe_ln_kernel, rt=rt),
        grid=(n_chunks, n_experts, ff // tf),
        in_specs=[smem(tm * TOP_K), smem(tm * TOP_K), smem(n_experts + 1),
                  pl.BlockSpec((tm, d), lambda c, e, f: (c, 0), pipeline_mode=pl.Buffered(1)),
                  pl.BlockSpec((None, d, tf), lambda c, e, f: (e, 0, f)),
                  pl.BlockSpec((None, d, tf), lambda c, e, f: (e, 0, f)),
                  pl.BlockSpec((None, tf, d), lambda c, e, f: (e, f, 0)),
                  pl.BlockSpec((1, d), lambda c, e, f: (0, 0)), pl.BlockSpec((1, d), lambda c, e, f: (0, 0))],
        out_specs=pl.BlockSpec((tm, d), lambda c, e, f: (c, 0), pipeline_mode=pl.Buffered(1)),
        out_shape=jax.ShapeDtypeStruct((m, d), f32),
        scratch_shapes=[pltpu.VMEM((tm, d), f32), pltpu.VMEM((tm, d), MXU_DTYPE), pltpu.VMEM((tm, d), f32)],
        compiler_params=_params("arbitrary", "arbitrary", "arbitrary"),
        name="moe_residual_ln",
    )(tok[:, None, :], gs[:, None, :], offs[:, None, :], x, w1, w3, w2, g.reshape(1, d), b.reshape(1, d))


def _flash_reset(m_ref, acc_ref):
    m_ref[...] = jnp.full(m_ref.shape, NEG, f32)
    acc_ref[...] = jnp.zeros(acc_ref.shape, f32)


def _flash_rows(q, kt, bias, m_ref, alpha_ref, p_ref, rows):
    s = _mm(q, kt)
    if bias is not None:
        s = s + bias
    m_prev = m_ref[rows, :]
    m_next = jnp.maximum(m_prev, jnp.max(s, axis=1, keepdims=True))
    alpha_ref[rows, :] = jnp.exp2(m_prev - m_next)
    m_ref[rows, :] = m_next
    p_ref[rows, :] = jnp.exp2(s - _lane_tile(m_next, s.shape[1] // LANES)).astype(p_ref.dtype)


def _flash_accumulate(v_aug, alpha_ref, p_ref, acc_ref):
    alpha = _lane_tile(alpha_ref[...], acc_ref.shape[-1] // LANES)
    acc_ref[...] = alpha * acc_ref[...] + _mm(p_ref[...], v_aug)


def _skewed_chunks(n, score, fold, carry):
    def pair(j, carry):
        c = 2 * j
        fold(c - 1, 1)
        carry = score(c, 0, carry)
        fold(c, 0)
        return score(c + 1, 1, carry)

    def single(c, carry):
        fold(c - 1, 1)
        return score(c, 0, carry)

    carry = lax.fori_loop(0, n // 2, pair, carry)
    return lax.fori_loop(2 * (n // 2), n, single, carry)


def _fold_last(n, fold):
    for buf in range(2):
        pl.when((n - 1) % 2 == buf)(functools.partial(fold, n - 1, buf))


def _lane_tile(x, reps):
    return x if reps == 1 else jnp.concatenate([x] * reps, axis=1)


def _row_slices(n_rows):
    return [slice(r * Q_BLOCK, (r + 1) * Q_BLOCK) for r in range(n_rows // Q_BLOCK)]


def _dsa_kernel(qi_ref, wi_ref, kit_ref, qa_ref, kat_ref, va_ref, o_ref,
                skey_ref, wrep_ref, m_ref, alpha_ref, acc_ref, p_ref, qh_ref, *, topk, kc):
    qb = Q_BLOCK
    i = pl.program_id(1)
    q0 = i * qb
    n_chunks = (q0 + qb + kc - 1) // kc
    t_row = q0 + lax.broadcasted_iota(i32, (qb, kc), 0)
    col = lax.broadcasted_iota(i32, (qb, kc), 1)
    wi = wi_ref[...]
    qi = qi_ref[...]
    qis = [qi[:, h * IDX_DIM:(h + 1) * IDX_DIM] for h in range(IDX_HEADS)]
    qa = qa_ref[...]
    for h in range(A_HEADS):
        qh_ref[h * qb:(h + 1) * qb, :] = qa[:, h * HEAD_DIM:(h + 1) * HEAD_DIM]
    for h in range(IDX_HEADS):
        wrep_ref[h] = jnp.broadcast_to(wi[:, h:h + 1], (qb, kc))

    def score_body(c, carry):
        off = pl.multiple_of(c * kc, kc)
        kt = kit_ref[:, pl.ds(off, kc)]
        s = jnp.zeros((qb, kc), f32)
        for h in range(IDX_HEADS):
            s = s + wrep_ref[h] * jnp.maximum(_mm(qis[h], kt), 0.0)
        s = jnp.where(col + off <= t_row, s, NEG)
        bits = pltpu.bitcast(s, i32)
        skey_ref[:, pl.ds(off, kc)] = jnp.where(bits < 0, bits ^ 0x7FFFFFFF, bits)
        return carry

    n_strips = (q0 + qb + COUNT_STRIP - 1) // COUNT_STRIP
    lax.fori_loop(0, n_strips * (COUNT_STRIP // kc), score_body, 0)

    def count_ge(cand):
        def body(c, acc):
            off = pl.multiple_of(c * COUNT_STRIP, COUNT_STRIP)
            for j in range(COUNT_STRIP // LANES):
                acc = acc + jnp.where(skey_ref[:, pl.ds(off + j * LANES, LANES)] >= cand, 1.0, 0.0)
            return acc
        acc = lax.fori_loop(0, n_strips, body, jnp.zeros((qb, LANES), f32))
        return jnp.sum(acc, axis=1, keepdims=True)

    thr = jnp.where(count_ge(jnp.zeros((qb, 1), i32)) >= topk, 0, INT_MIN).astype(i32)

    def bisect(b, thr):
        cand = thr + jnp.left_shift(jnp.int32(1), 30 - b)
        return jnp.where(count_ge(cand) >= topk, cand, thr)

    thr = lax.fori_loop(0, 31, bisect, thr)
    quota = topk - count_ge(thr + 1)

    before = (lax.broadcasted_iota(i32, (kc, kc), 0) < lax.broadcasted_iota(i32, (kc, kc), 1))
    before = jnp.where(before, 1.0, 0.0).astype(MXU_DTYPE)
    _flash_reset(m_ref, acc_ref)
    alpha_ref[...] = jnp.zeros(alpha_ref.shape, f32)
    p_ref[...] = jnp.zeros(p_ref.shape, p_ref.dtype)
    heads = _row_slices(A_HEADS * qb)

    def fold(c, buf):
        off = pl.multiple_of(jnp.maximum(c, 0) * kc, kc)
        _flash_accumulate(va_ref[pl.ds(off, kc), :], alpha_ref.at[buf], p_ref.at[buf], acc_ref)

    def score(c, buf, ties_seen):
        off = pl.multiple_of(c * kc, kc)
        key = skey_ref[:, pl.ds(off, kc)]
        eq = key == thr
        eqf = jnp.where(eq, 1.0, 0.0)
        rank = ties_seen + _mm(eqf.astype(MXU_DTYPE), before)
        sel = ((key > thr) | (eq & (rank < quota))) & (col + off <= t_row)
        bias = jnp.where(sel, 0.0, NEG)
        kt = kat_ref[:, pl.ds(off, kc)]
        for rows in heads:
            _flash_rows(qh_ref[rows, :], kt, bias, m_ref, alpha_ref.at[buf], p_ref.at[buf], rows)
        return ties_seen + jnp.sum(eqf, axis=1, keepdims=True)

    _skewed_chunks(n_chunks, score, fold, jnp.zeros((qb, 1), f32))
    _fold_last(n_chunks, fold)
    acc = acc_ref[...]
    o = acc[:, :HEAD_DIM] / acc[:, HEAD_DIM:]
    for h in range(A_HEADS):
        o_ref[:, h * HEAD_DIM:(h + 1) * HEAD_DIM] = o[h * qb:(h + 1) * qb, :]


QPK_QA, QPK_QIDX, QPK_QB = 0, 2, 3
KPK_KA, KPK_KIDX, KPK_KSLC, KPK_KWIN = 0, 1, 2, 4
VPK_VA, VPK_VSLC, VPK_VWIN = 0, 1, 3
OUT_NSA = 2


def dsa_attention(wi, qpk, kpk, vpk, topk, kc=512):
    bsz, seq, _ = qpk.shape
    nb = seq // Q_BLOCK
    rows = A_HEADS * Q_BLOCK
    width = A_HEADS * HEAD_DIM
    kern = functools.partial(_dsa_kernel, topk=topk, kc=kc)
    return pl.pallas_call(
        kern,
        grid=(bsz, nb),
        in_specs=[pl.BlockSpec((None, Q_BLOCK, IDX_HEADS * IDX_DIM), lambda b, i: (b, i, QPK_QIDX)),
                  pl.BlockSpec((None, Q_BLOCK, IDX_HEADS), lambda b, i: (b, i, 0)),
                  pl.BlockSpec((None, IDX_DIM, seq), lambda b, i: (b, KPK_KIDX, 0)),
                  pl.BlockSpec((None, Q_BLOCK, width), lambda b, i: (b, i, QPK_QA)),
                  pl.BlockSpec((None, HEAD_DIM, seq), lambda b, i: (b, KPK_KA, 0)),
                  pl.BlockSpec((None, seq, 2 * HEAD_DIM), lambda b, i: (b, 0, VPK_VA))],
        out_specs=pl.BlockSpec((None, Q_BLOCK, width), lambda b, i: (b, i, 0)),
        out_shape=jax.ShapeDtypeStruct((bsz, seq, 2 * width), f32),
        scratch_shapes=[pltpu.VMEM((Q_BLOCK, seq), i32), pltpu.VMEM((IDX_HEADS, Q_BLOCK, kc), f32),
                        pltpu.VMEM((rows, LANES), f32), pltpu.VMEM((2, rows, LANES), f32),
                        pltpu.VMEM((rows, 2 * HEAD_DIM), f32), pltpu.VMEM((2, rows, kc), MXU_DTYPE),
                        pltpu.VMEM((rows, HEAD_DIM), MXU_DTYPE)],
        compiler_params=_params("parallel", "arbitrary"),
        name="dsa_attention",
    )(qpk, wi, kpk, qpk, kpk, vpk)


def _compress_kernel(x_ref, pe_ref, w1_ref, w2_ref, o_ref):
    half = (CMP_LEN // 2) * HEAD_DIM
    x = x_ref[...]
    first = _mm((x + pe_ref[0:1, :]).astype(MXU_DTYPE), w1_ref[0:half, :])
    second = _mm((x + pe_ref[1:2, :]).astype(MXU_DTYPE), w1_ref[half:2 * half, :])
    pre = first + pltpu.roll(second, shift=x.shape[0] - 1, axis=0)
    hid = pre * jax.nn.sigmoid(pre)
    o_ref[...] = _mm(hid.astype(MXU_DTYPE), w2_ref[...])


def nsa_compress(chunks, pe2, w1, w2):
    bsz, groups, nck, width = chunks.shape
    return pl.pallas_call(
        _compress_kernel,
        grid=(bsz, groups),
        in_specs=[pl.BlockSpec((None, None, nck, width), lambda b, g: (b, g, 0, 0)),
                  pl.BlockSpec((2, width), lambda b, g: (0, 0)),
                  pl.BlockSpec((2 * width, CMP_HIDDEN), lambda b, g: (0, 0)),
                  pl.BlockSpec((CMP_HIDDEN, HEAD_DIM), lambda b, g: (0, 0))],
        out_specs=pl.BlockSpec((None, None, nck, HEAD_DIM), lambda b, g: (b, g, 0, 0)),
        out_shape=jax.ShapeDtypeStruct((bsz, groups, nck, HEAD_DIM), f32),
        compiler_params=_params("parallel", "parallel"),
        name="nsa_compress",
    )(chunks, pe2, w1, w2)


def _nsa_kernel(qin_ref, graw_ref, kct_ref, vc_ref, kst_ref, vs_ref, kwt_ref, vw_ref, ovl_ref, _, o_ref,
                m_ref, alpha_ref, acc_ref, p_ref, pw_ref, q_ref, *, n_slc, n_sel, kc):
    qb = Q_BLOCK
    i = pl.program_id(2)
    q0 = i * qb
    n_cmp = kct_ref.shape[1]
    heads = _row_slices(B_REP * qb)
    qin = qin_ref[...]
    for r in range(B_REP):
        q_ref[r * qb:(r + 1) * qb, :] = qin[:, r * HEAD_DIM:(r + 1) * HEAD_DIM]

    t_c = q0 + lax.broadcasted_iota(i32, (qb, n_cmp), 0)
    cmp_end = lax.broadcasted_iota(i32, (qb, n_cmp), 1) * CMP_STRIDE + (CMP_LEN - 1)
    vis = cmp_end <= t_c
    kct = kct_ref[...]
    p_sum = jnp.zeros((qb, n_cmp), f32)
    for rows in heads:
        lc = jnp.where(vis, _mm(q_ref[rows, :], kct), NEG)
        ec = jnp.where(vis, jnp.exp2(lc - jnp.max(lc, axis=1, keepdims=True)), 0.0)
        den = jnp.sum(ec, axis=1, keepdims=True)
        p_c = ec / jnp.where(den > 0.0, den, 1.0)
        p_sum = p_sum + p_c
        acc_ref[rows, :HEAD_DIM] = _mm(p_c.astype(MXU_DTYPE), vc_ref[...])
    o_c = acc_ref[:, :HEAD_DIM]

    imp = jnp.dot(p_sum, ovl_ref[...], preferred_element_type=f32, precision=lax.Precision.HIGHEST).T
    t_q = q0 + lax.broadcasted_iota(i32, (LANES, qb), 1)
    blk = lax.broadcasted_iota(i32, (LANES, qb), 0)
    blk_t = t_q // SLC_LEN
    forced = (blk == 0) | (blk == blk_t) | (blk == blk_t - 1)
    imp = jnp.where(forced, FORCED_BOOST, imp)
    imp = jnp.where(blk * SLC_LEN <= t_q, imp, NEG)
    imp = jnp.where(blk < n_slc, imp, -jnp.inf)
    blk_f = blk.astype(f32)

    def pick(_, carry):
        imp, selm = carry
        best = jnp.max(imp, axis=0, keepdims=True)
        first = jnp.min(jnp.where(imp == best, blk_f, float(LANES)), axis=0, keepdims=True)
        hit = blk_f == first
        return jnp.where(hit, -jnp.inf, imp), jnp.where(hit, 1.0, selm)

    _, selm = lax.fori_loop(0, n_sel, pick, (imp, jnp.zeros((LANES, qb), f32)))
    selm = selm.T.astype(MXU_DTYPE)

    t_k = q0 + lax.broadcasted_iota(i32, (qb, kc), 0)
    col = lax.broadcasted_iota(i32, (qb, kc), 1)
    exp_row = lax.broadcasted_iota(i32, (LANES, kc), 0)
    exp_col = lax.broadcasted_iota(i32, (LANES, kc), 1)
    _flash_reset(m_ref, acc_ref)
    alpha_ref[...] = jnp.zeros(alpha_ref.shape, f32)
    p_ref[...] = jnp.zeros(p_ref.shape, p_ref.dtype)

    def fold(c, buf):
        off = pl.multiple_of(jnp.maximum(c, 0) * kc, kc)
        _flash_accumulate(vs_ref[pl.ds(off, kc), :], alpha_ref.at[buf], p_ref.at[buf], acc_ref)

    def score(c, buf, carry):
        off = pl.multiple_of(c * kc, kc)
        expand = jnp.where(exp_row == (exp_col + off) // SLC_LEN, 1.0, 0.0).astype(MXU_DTYPE)
        sel = (_mm(selm, expand) > 0.5) & (col + off <= t_k)
        bias = jnp.where(sel, 0.0, NEG)
        kt = kst_ref[:, pl.ds(off, kc)]
        for rows in heads:
            _flash_rows(q_ref[rows, :], kt, bias, m_ref, alpha_ref.at[buf], p_ref.at[buf], rows)
        return carry

    n_slc_chunks = (q0 + qb + kc - 1) // kc
    _skewed_chunks(n_slc_chunks, score, fold, 0)
    _fold_last(n_slc_chunks, fold)
    acc = acc_ref[...]
    o_s = acc[:, :HEAD_DIM] / acc[:, HEAD_DIM:]

    slab = WINDOW + qb
    w0 = pl.multiple_of(jnp.maximum(q0 - WINDOW, 0), qb)
    dist = (q0 + lax.broadcasted_iota(i32, (qb, slab), 0)) - (w0 + lax.broadcasted_iota(i32, (qb, slab), 1))
    bias = jnp.where((dist >= 0) & (dist < WINDOW), 0.0, NEG)
    kt = kwt_ref[:, pl.ds(w0, slab)]
    for rows in heads:
        s = _mm(q_ref[rows, :], kt) + bias
        pw_ref[rows, :] = jnp.exp2(s - jnp.max(s, axis=1, keepdims=True)).astype(pw_ref.dtype)
    acc = _mm(pw_ref[...], vw_ref[pl.ds(w0, slab), :])
    o_w = acc[:, :HEAD_DIM] / acc[:, HEAD_DIM:]

    gates = jax.nn.sigmoid(graw_ref[...])
    for r, rows in enumerate(heads):
        g_c, g_s, g_w = (gates[:, 3 * r + n:3 * r + n + 1] for n in range(3))
        o_ref[:, r * HEAD_DIM:(r + 1) * HEAD_DIM] = g_c * o_c[rows] + g_s * o_s[rows] + g_w * o_w[rows]


def nsa_attention(qpk, graw, kct, vc, kpk, vpk, ovl, out_ab, n_slc, n_sel, kc=512):
    bsz, seq, _ = qpk.shape
    groups = B_KV_GROUPS
    nb = seq // Q_BLOCK
    rows = B_REP * Q_BLOCK
    width = B_REP * HEAD_DIM
    n_cmp = kct.shape[-1]
    kern = functools.partial(_nsa_kernel, n_slc=n_slc, n_sel=n_sel, kc=kc)
    per_bg = lambda b, g, i: (b, g, 0, 0)
    return pl.pallas_call(
        kern,
        grid=(bsz, groups, nb),
        in_specs=[pl.BlockSpec((None, Q_BLOCK, width), lambda b, g, i: (b, i, QPK_QB + g)),
                  pl.BlockSpec((None, None, Q_BLOCK, B_REP * 3), lambda b, g, i: (b, g, i, 0)),
                  pl.BlockSpec((None, None, HEAD_DIM, n_cmp), per_bg),
                  pl.BlockSpec((None, None, n_cmp, HEAD_DIM), per_bg),
                  pl.BlockSpec((None, HEAD_DIM, seq), lambda b, g, i: (b, KPK_KSLC + g, 0)),
                  pl.BlockSpec((None, seq, 2 * HEAD_DIM), lambda b, g, i: (b, 0, VPK_VSLC + g)),
                  pl.BlockSpec((None, HEAD_DIM, seq), lambda b, g, i: (b, KPK_KWIN + g, 0)),
                  pl.BlockSpec((None, seq, 2 * HEAD_DIM), lambda b, g, i: (b, 0, VPK_VWIN + g)),
                  pl.BlockSpec((n_cmp, LANES), lambda b, g, i: (0, 0)),
                  pl.BlockSpec(memory_space=pl.ANY)],
        out_specs=pl.BlockSpec((None, Q_BLOCK, width), lambda b, g, i: (b, i, OUT_NSA + g)),
        out_shape=jax.ShapeDtypeStruct(out_ab.shape, f32),
        input_output_aliases={9: 0},
        scratch_shapes=[pltpu.VMEM((rows, LANES), f32), pltpu.VMEM((2, rows, LANES), f32),
                        pltpu.VMEM((rows, 2 * HEAD_DIM), f32), pltpu.VMEM((2, rows, kc), MXU_DTYPE),
                        pltpu.VMEM((rows, WINDOW + Q_BLOCK), MXU_DTYPE), pltpu.VMEM((rows, HEAD_DIM), MXU_DTYPE)],
        compiler_params=_params("parallel", "parallel", "arbitrary"),
        name="nsa_attention",
    )(qpk, graw, kct, vc, kpk, vpk, kpk, vpk, ovl, out_ab)


def _diff_kernel(lam_ref, qin_ref, kt_ref, v_ref, g_ref, o_ref, m_ref, alpha_ref, acc_ref, p_ref, q_ref,
                 *, tq, kc, out_scale):
    i = pl.program_id(2)
    q0 = i * tq
    dv = v_ref.shape[-1] // 2
    groups = _row_slices(tq)
    qin = qin_ref[...]
    for half in range(2):
        q_ref[half] = qin[:, half * HEAD_DIM:(half + 1) * HEAD_DIM]
    col = lax.broadcasted_iota(i32, (Q_BLOCK, kc), 1)
    t_row = [q0 + r * Q_BLOCK + lax.broadcasted_iota(i32, (Q_BLOCK, kc), 0) for r in range(len(groups))]
    for half in range(2):
        _flash_reset(m_ref.at[half], acc_ref.at[half])
    alpha_ref[...] = jnp.zeros(alpha_ref.shape, f32)
    p_ref[...] = jnp.zeros(p_ref.shape, p_ref.dtype)

    def fold(c, buf):
        off = pl.multiple_of(jnp.maximum(c, 0) * kc, kc)
        v = v_ref[pl.ds(off, kc), :]
        for half in range(2):
            _flash_accumulate(v, alpha_ref.at[buf, half], p_ref.at[buf, half], acc_ref.at[half])

    def score(c, buf, masked):
        off = pl.multiple_of(c * kc, kc)
        for half in range(2):
            kt = kt_ref[half * HEAD_DIM:(half + 1) * HEAD_DIM, pl.ds(off, kc)]
            for r, rows in enumerate(groups):
                bias = jnp.where(col + off <= t_row[r], 0.0, NEG) if masked else None
                _flash_rows(q_ref[half, rows, :], kt, bias, m_ref.at[half], alpha_ref.at[buf, half],
                            p_ref.at[buf, half], rows)

    def score_full(c, buf, carry):
        score(c, buf, False)
        return carry

    n_full = q0 // kc
    _skewed_chunks(n_full, score_full, fold, 0)

    def finish(buf):
        fold(n_full - 1, 1 - buf)
        score(n_full, buf, True)
        fold(n_full, buf)

    for buf in range(2):
        pl.when(n_full % 2 == buf)(functools.partial(finish, buf))
    a1 = acc_ref[0]
    a2 = acc_ref[1]
    o = a1[:, :dv] / a1[:, dv:] - lam_ref[0] * (a2[:, :dv] / a2[:, dv:])
    o = o * lax.rsqrt(jnp.mean(o * o, axis=-1, keepdims=True) + LN_EPS) * g_ref[...]
    o_ref[...] = o * out_scale


def diff_attention(lam, q, kt, v, subln_g, out_scale, tq=512, kc=1024):
    bsz, seq, width = q.shape
    dv = 2 * HEAD_DIM
    heads = width // dv
    tq = min(tq, seq)
    kc = min(kc, seq)
    assert kc % tq == 0 and seq % kc == 0
    kern = functools.partial(_diff_kernel, tq=tq, kc=kc, out_scale=out_scale)
    return pl.pallas_call(
        kern,
        grid=(bsz, heads, seq // tq),
        in_specs=[pl.BlockSpec(memory_space=pltpu.SMEM),
                  pl.BlockSpec((None, tq, dv), lambda b, h, i: (b, i, h)),
                  pl.BlockSpec((None, dv, seq), lambda b, h, i: (b, h, 0)),
                  pl.BlockSpec((None, seq, 2 * dv), lambda b, h, i: (b, 0, h)),
                  pl.BlockSpec((1, dv), lambda b, h, i: (0, 0))],
        out_specs=pl.BlockSpec((None, tq, dv), lambda b, h, i: (b, i, h)),
        out_shape=jax.ShapeDtypeStruct((bsz, seq, heads * dv), f32),
        scratch_shapes=[pltpu.VMEM((2, tq, LANES), f32), pltpu.VMEM((2, 2, tq, LANES), f32),
                        pltpu.VMEM((2, tq, 2 * dv), f32), pltpu.VMEM((2, 2, tq, kc), MXU_DTYPE),
                        pltpu.VMEM((2, tq, HEAD_DIM), MXU_DTYPE)],
        compiler_params=_params("parallel", "parallel", "arbitrary"),
        name="diff_attention",
    )(lam, q, kt, v, subln_g.reshape(1, dv))


def _rope_tables(positions):
    inv_freq = ROPE_THETA ** (-jnp.arange(0, ROT_DIM, 2, dtype=f32) / ROT_DIM)
    ang = positions.astype(f32)[..., None] * inv_freq
    return jnp.cos(ang), jnp.sin(ang)


def _apply_rope(x, cos, sin):
    shape = cos.shape[:2] + (1,) * (x.ndim - 3) + cos.shape[-1:]
    c = cos.reshape(shape)
    s = sin.reshape(shape)
    half = ROT_DIM // 2
    x1, x2 = x[..., :half], x[..., half:ROT_DIM]
    return jnp.concatenate([x1 * c - x2 * s, x2 * c + x1 * s, x[..., ROT_DIM:]], axis=-1)


def _rope_coefficients(cos, sin):
    lead = cos.shape[:-1]
    rest = jnp.zeros(lead + (HEAD_DIM - ROT_DIM,), f32)
    none = jnp.zeros_like(sin)
    twice = lambda a: jnp.concatenate([a, a], axis=-1)
    return (twice(jnp.concatenate([cos, cos, rest + 1.0], axis=-1)),
            twice(jnp.concatenate([-sin, none, rest], axis=-1)),
            twice(jnp.concatenate([none, sin, rest], axis=-1)))


def _ab_weight_columns(w_in):
    widths = dict(AB_LAYOUT)
    starts = dict(zip(widths, np.cumsum([0] + [w for _, w in AB_LAYOUT[:-1]]).tolist()))
    cols = lambda names: np.concatenate([np.arange(starts[n], starts[n] + widths[n]) for n in names])
    first = ("q_a", "q_idx", "q_b", "k_a", "k_idx", "k_slc", "k_win", "v_a", "v_slc", "v_win", "gate_b", "w_idx")
    n_first = sum(widths[n] for n in first)
    pad = AB_CMP_TILE * LANES - n_first
    assert 0 <= pad < HEAD_DIM
    w = jnp.concatenate([w_in[:, cols(first)], jnp.zeros((w_in.shape[0], pad), w_in.dtype),
                         w_in[:, cols(("k_cmp", "v_cmp"))]], axis=1)
    assert w.shape[1] == AB_TILES * LANES
    return w.astype(MXU_DTYPE)


def _overlap_matrix(n_cmp_rows, n_slc):
    c_start = np.arange(n_cmp_rows) * CMP_STRIDE
    s_start = np.arange(LANES) * SLC_LEN
    ovl = (c_start[:, None] < s_start[None, :] + SLC_LEN) & (c_start[:, None] + CMP_LEN > s_start[None, :])
    ovl = ovl & (np.arange(LANES)[None, :] < n_slc)
    return jnp.asarray(ovl.astype(np.float32))


def _ab_mixer(x2, bsz, seq, positions, coef, w_in, pe_k, pe_v, ck1, ck2, cv1, cv2):
    qpk, kpk, vpk, misc, cmp = project_pack_ab(x2.reshape(bsz, seq, -1), _ab_weight_columns(w_in), coef)
    gate_w = B_HEADS * 3
    g_b = misc[..., HEAD_DIM:HEAD_DIM + gate_w]
    w_idx = misc[..., HEAD_DIM + gate_w:HEAD_DIM + gate_w + IDX_HEADS] * (IDX_HEADS * IDX_DIM) ** -0.5
    k_cmp, v_cmp = cmp[..., :LANES], cmp[..., LANES:]

    out_ab = dsa_attention(w_idx, qpk, kpk, vpk, topk=min(DSA_TOPK, seq // 4))

    groups = B_KV_GROUPS
    n_rows = seq // CMP_STRIDE
    n_cmp = (seq - CMP_LEN) // CMP_STRIDE + 1
    assert n_cmp == n_rows - 1
    n_slc = seq // SLC_LEN
    assert n_slc <= LANES

    def chunked(kv):
        kv = kv.reshape(bsz, n_rows, CMP_STRIDE, groups, HEAD_DIM).transpose(0, 3, 1, 2, 4)
        return kv.reshape(bsz, groups, n_rows, CMP_STRIDE * HEAD_DIM)

    def pe_halves(pe):
        return pe.reshape(2, CMP_STRIDE * HEAD_DIM)

    k_c = nsa_compress(chunked(k_cmp), pe_halves(pe_k), ck1.astype(MXU_DTYPE), ck2.astype(MXU_DTYPE))
    v_c = nsa_compress(chunked(v_cmp), pe_halves(pe_v), cv1.astype(MXU_DTYPE), cv2.astype(MXU_DTYPE))
    cmp_end = jnp.minimum(jnp.arange(n_rows) * CMP_STRIDE + CMP_LEN - 1, seq - 1)
    cos_c, sin_c = _rope_tables(positions[:, cmp_end])
    k_c = _apply_rope(k_c.transpose(0, 2, 1, 3), cos_c, sin_c)
    k_c = k_c.transpose(0, 2, 3, 1).astype(MXU_DTYPE)
    v_c = v_c.astype(MXU_DTYPE)

    graw = g_b.reshape(bsz, seq, groups, B_REP * 3).transpose(0, 2, 1, 3)
    out_ab = nsa_attention(qpk, graw, k_c, v_c, kpk, vpk, _overlap_matrix(n_rows, n_slc), out_ab,
                           n_slc=n_slc, n_sel=min(SLC_TOPN, n_slc))
    return out_ab.reshape(bsz * seq, (A_HEADS + B_HEADS) * HEAD_DIM)


def _diff_mixer(x2, bsz, seq, coef, w_in, lq1, lk1, lq2, lk2, subln_g, lam_init):
    q, kt, v = project_pack_c(x2.reshape(bsz, seq, -1), w_in.astype(MXU_DTYPE), coef)
    lam =(jnp.exp(jnp.sum(lq1 * lk1)) - jnp.exp(jnp.sum(lq2 * lk2)) + lam_init).reshape(1).astype(f32)
    o = diff_attention(lam, q, kt, v, subln_g, 1.0 - lam_init)
    return o.reshape(bsz * seq, C_HEADS * 2 * HEAD_DIM)


def kernel(x, positions, ab_w_in, cmp_pe_k, cmp_pe_v, cmp_k_w1, cmp_k_w2, cmp_v_w1, cmp_v_w2, ab_w_out, ln_ab_g, ln_ab_b, ffn_w1, ffn_w3, ffn_w2, ln_ffn_g, ln_ffn_b, c_w_in, lambda_q1, lambda_k1, lambda_q2, lambda_k2, c_subln_g, c_w_out, ln_c_g, ln_c_b, router_w, moe_w1, moe_w3, moe_w2, ln_moe_g, ln_moe_b):
    bsz, seq, d = x.shape
    assert seq % COUNT_STRIP == 0 and seq >= WINDOW + Q_BLOCK and d == D_MODEL
    coef = _rope_coefficients(*_rope_tables(positions))
    x2 = x.reshape(bsz * seq, d)
    for layer in range(DEPTH):
        i = layer // 2
        if layer % 2 == 0:
            o = _ab_mixer(x2, bsz, seq, positions, coef, ab_w_in[i], cmp_pe_k[i], cmp_pe_v[i],
                          cmp_k_w1[i], cmp_k_w2[i], cmp_v_w1[i], cmp_v_w2[i])
            x2 = project_residual_ln(o, ab_w_out[i].astype(MXU_DTYPE), x2, ln_ab_g[i], ln_ab_b[i])
            x2 = ffn_residual_ln(x2, ffn_w1[i].astype(MXU_DTYPE), ffn_w3[i].astype(MXU_DTYPE),
                                 ffn_w2[i].astype(MXU_DTYPE), ln_ffn_g[i], ln_ffn_b[i], tm=512, tf=1408)
        else:
            lam_init = 0.8 - 0.6 * math.exp(-0.3 * layer)
            o = _diff_mixer(x2, bsz, seq, coef, c_w_in[i], lambda_q1[i], lambda_k1[i], lambda_q2[i],
                            lambda_k2[i], c_subln_g[i], lam_init)
            x2 = project_residual_ln(o, c_w_out[i].astype(MXU_DTYPE), x2, ln_c_g[i], ln_c_b[i])
            routes = route_top2(x2, router_w[i])
            x2 = moe_residual_ln(x2, routes, moe_w1[i].astype(MXU_DTYPE), moe_w3[i].astype(MXU_DTYPE),
                                 moe_w2[i].astype(MXU_DTYPE), ln_moe_g[i], ln_moe_b[i], tm=min(2048, bsz * seq // 2), tf=896, rt=128)
    return x2.reshape(bsz, seq, d)
```

```python
import functools
import math

import numpy as np
import jax
import jax.numpy as jnp
from jax import lax
from jax.experimental import pallas as pl
from jax.experimental.pallas import tpu as pltpu

f32 = jnp.float32
i32 = jnp.int32
MXU_DTYPE = jnp.bfloat16
VMEM_LIMIT_BYTES = 56 * 1024 * 1024
LANES = 128

D_MODEL = 1024
DEPTH = 2
HEAD_DIM = 64
ROT_DIM = HEAD_DIM // 4
ROPE_THETA = 500000.0
Q_BLOCK = 128
NEG = -1e30
LN_EPS = 1e-5
A_HEADS = 8
IDX_HEADS = 4
IDX_DIM = 64
DSA_TOPK = 256
B_HEADS = 8
B_KV_GROUPS = 2
B_REP = B_HEADS // B_KV_GROUPS
CMP_LEN = 32
CMP_STRIDE = 16
CMP_HIDDEN = 128
SLC_LEN = 64
SLC_TOPN = 16
WINDOW = 512
FORCED_BOOST = 1e6
C_HEADS = 8
N_EXPERTS = 8
TOP_K = 2
DEEPNORM_ALPHA = (2 * DEPTH) ** 0.25
QK_SCALE = HEAD_DIM ** -0.5 * math.log2(math.e)
INT_MIN = -(2 ** 31)
COUNT_STRIP = 512
ROW_UNROLL = 4

AB_LAYOUT = (
    ("q_a", A_HEADS * HEAD_DIM), ("k_a", HEAD_DIM), ("v_a", HEAD_DIM),
    ("q_idx", IDX_HEADS * IDX_DIM), ("k_idx", IDX_DIM), ("w_idx", IDX_HEADS),
    ("q_b", B_HEADS * HEAD_DIM),
    ("k_cmp", B_KV_GROUPS * HEAD_DIM), ("v_cmp", B_KV_GROUPS * HEAD_DIM),
    ("k_slc", B_KV_GROUPS * HEAD_DIM), ("v_slc", B_KV_GROUPS * HEAD_DIM),
    ("k_win", B_KV_GROUPS * HEAD_DIM), ("v_win", B_KV_GROUPS * HEAD_DIM),
    ("gate_b", 3 * B_HEADS),
)


def _params(*sem):
    return pltpu.CompilerParams(dimension_semantics=sem, vmem_limit_bytes=VMEM_LIMIT_BYTES)


def _mm(a, b):
    return jnp.dot(a, b, preferred_element_type=f32)


def _layer_norm_rows(y, g, b):
    mu = jnp.mean(y, axis=-1, keepdims=True)
    yc = y - mu
    var = jnp.mean(yc * yc, axis=-1, keepdims=True)
    return yc * lax.rsqrt(var + LN_EPS) * g + b


def _rope_tile(x, keep, hi, lo):
    return x * keep + pltpu.roll(x, LANES - ROT_DIM // 2, 1) * hi + pltpu.roll(x, ROT_DIM // 2, 1) * lo


def _tile(h, j):
    return h[:, j * LANES:(j + 1) * LANES]


AB_Q_TILES = 10
AB_QIDX_TILES = (4, 5)
AB_K_TILES = 3
AB_V_HEADS = 5
AB_MISC_TILE = 15
AB_CMP_TILE = 16
AB_TILES = 18


def _proj_pack_ab_kernel(x_ref, w_ref, keep_ref, hi_ref, lo_ref, q_ref, kt_ref, v_ref, misc_ref, cmp_ref):
    h = _mm(x_ref[...].astype(MXU_DTYPE), w_ref[...])
    keep, hi, lo = keep_ref[...], hi_ref[...], lo_ref[...]
    for j in range(AB_Q_TILES):
        t = _rope_tile(_tile(h, j), keep, hi, lo)
        q_ref[:, j * LANES:(j + 1) * LANES] = (t if j in AB_QIDX_TILES else t * QK_SCALE).astype(q_ref.dtype)
    k = jnp.concatenate([_rope_tile(_tile(h, AB_Q_TILES + j), keep, hi, lo) for j in range(AB_K_TILES)], axis=1)
    kt_ref[...] = k.T.astype(kt_ref.dtype)
    low = lax.broadcasted_iota(i32, keep.shape, 1) < HEAD_DIM
    for j in range(AB_V_HEADS):
        t = _tile(h, AB_Q_TILES + AB_K_TILES + j // 2)
        t = pltpu.roll(t, HEAD_DIM, 1) if j % 2 else t
        v_ref[:, j * LANES:(j + 1) * LANES] = jnp.where(low, t, 1.0).astype(v_ref.dtype)
    misc_ref[...] = _tile(h, AB_MISC_TILE)
    cmp_ref[...] = h[:, AB_CMP_TILE * LANES:AB_TILES * LANES]


def project_pack_ab(x, w, coef, tm=512):
    bsz, seq, d = x.shape
    row = lambda width: pl.BlockSpec((None, tm, width), lambda b, i: (b, i, 0))
    kt_rows = AB_K_TILES * LANES
    return pl.pallas_call(
        _proj_pack_ab_kernel,
        grid=(bsz, seq // tm),
        in_specs=[row(d), pl.BlockSpec((d, AB_TILES * LANES), lambda b, i: (0, 0)), row(LANES), row(LANES), row(LANES)],
        out_specs=[row(AB_Q_TILES * LANES), pl.BlockSpec((None, kt_rows, tm), lambda b, i: (b, 0, i)),
                   row(AB_V_HEADS * LANES), row(LANES), row(2 * LANES)],
        out_shape=[jax.ShapeDtypeStruct((bsz, seq, AB_Q_TILES * LANES), MXU_DTYPE),
                   jax.ShapeDtypeStruct((bsz, kt_rows, seq), MXU_DTYPE),
                   jax.ShapeDtypeStruct((bsz, seq, AB_V_HEADS * LANES), MXU_DTYPE),
                   jax.ShapeDtypeStruct((bsz, seq, LANES), f32),
                   jax.ShapeDtypeStruct((bsz, seq, 2 * LANES), f32)],
        compiler_params=_params("parallel", "parallel"),
        name="project_pack_ab",
    )(x, w, *coef)


def _proj_pack_c_kernel(x_ref, w_ref, keep_ref, hi_ref, lo_ref, q_ref, kt_ref, v_ref):
    h = _mm(x_ref[...].astype(MXU_DTYPE), w_ref[...])
    keep, hi, lo = keep_ref[...], hi_ref[...], lo_ref[...]
    n = q_ref.shape[1] // LANES
    for j in range(n):
        q_ref[:, j * LANES:(j + 1) * LANES] = (_rope_tile(_tile(h, j), keep, hi, lo) * QK_SCALE).astype(q_ref.dtype)
    k = jnp.concatenate([_rope_tile(_tile(h, n + j), keep, hi, lo) for j in range(n)], axis=1)
    kt_ref[...] = k.T.astype(kt_ref.dtype)
    ones = jnp.ones(keep.shape, v_ref.dtype)
    for j in range(n):
        v_ref[:, 2 * j * LANES:(2 * j + 1) * LANES] = _tile(h, 2 * n + j).astype(v_ref.dtype)
        v_ref[:, (2 * j + 1) * LANES:(2 * j + 2) * LANES] = ones


def project_pack_c(x, w, coef, tm=512):
    bsz, seq, d = x.shape
    width = w.shape[1] // 3
    row = lambda cols: pl.BlockSpec((None, tm, cols), lambda b, i: (b, i, 0))
    return pl.pallas_call(
        _proj_pack_c_kernel,
        grid=(bsz, seq // tm),
        in_specs=[row(d), pl.BlockSpec((d, 3 * width), lambda b, i: (0, 0)), row(LANES), row(LANES), row(LANES)],
        out_specs=[row(width), pl.BlockSpec((None, width, tm), lambda b, i: (b, 0, i)), row(2 * width)],
        out_shape=[jax.ShapeDtypeStruct((bsz, seq, width), MXU_DTYPE),
                   jax.ShapeDtypeStruct((bsz, width, seq), MXU_DTYPE),
                   jax.ShapeDtypeStruct((bsz, seq, 2 * width), MXU_DTYPE)],
        compiler_params=_params("parallel", "parallel"),
        name="project_pack_c",
    )(x, w, *coef)


def _proj_ln_kernel(a_ref, w_ref, res_ref, g_ref, b_ref, o_ref):
    h = _mm(a_ref[...].astype(MXU_DTYPE), w_ref[...])
    o_ref[...] = _layer_norm_rows(DEEPNORM_ALPHA * res_ref[...] + h, g_ref[...], b_ref[...])


def project_residual_ln(a, w, res, g, b, tm=512):
    m, k = a.shape
    n = w.shape[1]
    return pl.pallas_call(
        _proj_ln_kernel,
        grid=(m // tm,),
        in_specs=[pl.BlockSpec((tm, k), lambda i: (i, 0)), pl.BlockSpec((k, n), lambda i: (0, 0)),
                  pl.BlockSpec((tm, n), lambda i: (i, 0)),
                  pl.BlockSpec((1, n), lambda i: (0, 0)), pl.BlockSpec((1, n), lambda i: (0, 0))],
        out_specs=pl.BlockSpec((tm, n), lambda i: (i, 0)),
        out_shape=jax.ShapeDtypeStruct((m, n), f32),
        compiler_params=_params("parallel"),
        name="project_residual_ln",
    )(a, w, res, g.reshape(1, n), b.reshape(1, n))


def _swiglu_tile(xb, w1_ref, w3_ref, w2_ref):
    a = _mm(xb, w1_ref[...])
    h = (a * jax.nn.sigmoid(a)) * _mm(xb, w3_ref[...])
    return _mm(h.astype(MXU_DTYPE), w2_ref[...])


def _ffn_ln_kernel(x_ref, w1_ref, w3_ref, w2_ref, g_ref, b_ref, o_ref, xb_ref, acc_ref):
    f = pl.program_id(1)

    @pl.when(f == 0)
    def _():
        xb_ref[...] = x_ref[...].astype(MXU_DTYPE)
        acc_ref[...] = jnp.zeros_like(acc_ref)

    acc_ref[...] += _swiglu_tile(xb_ref[...], w1_ref, w3_ref, w2_ref)

    @pl.when(f == pl.num_programs(1) - 1)
    def _():
        o_ref[...] = _layer_norm_rows(DEEPNORM_ALPHA * x_ref[...] + acc_ref[...], g_ref[...], b_ref[...])


def ffn_residual_ln(x, w1, w3, w2, g, b, tm, tf):
    m, d = x.shape
    ff = w1.shape[1]
    return pl.pallas_call(
        _ffn_ln_kernel,
        grid=(m // tm, ff // tf),
        in_specs=[pl.BlockSpec((tm, d), lambda i, f: (i, 0)),
                  pl.BlockSpec((d, tf), lambda i, f: (0, f)),
                  pl.BlockSpec((d, tf), lambda i, f: (0, f)),
                  pl.BlockSpec((tf, d), lambda i, f: (f, 0)),
                  pl.BlockSpec((1, d), lambda i, f: (0, 0)), pl.BlockSpec((1, d), lambda i, f: (0, 0))],
        out_specs=pl.BlockSpec((tm, d), lambda i, f: (i, 0)),
        out_shape=jax.ShapeDtypeStruct((m, d), f32),
        scratch_shapes=[pltpu.VMEM((tm, d), MXU_DTYPE), pltpu.VMEM((tm, d), f32)],
        compiler_params=_params("parallel", "arbitrary"),
        name="ffn_residual_ln",
    )(x, w1, w3, w2, g.reshape(1, d), b.reshape(1, d))


ROUTE_IDS = N_EXPERTS
ROUTE_GATES = N_EXPERTS + 2


def _router_kernel(x_ref, w_ref, o_ref, *, n_experts):
    logits = jnp.dot(x_ref[...], w_ref[...], preferred_element_type=f32, precision=lax.Precision.HIGHEST)
    lane = lax.broadcasted_iota(i32, logits.shape, 1).astype(f32)
    logits = jnp.where(lane < n_experts, logits, -jnp.inf)
    v1 = jnp.max(logits, axis=1, keepdims=True)
    i1 = jnp.min(jnp.where(logits == v1, lane, float(LANES)), axis=1, keepdims=True)
    rest = jnp.where(lane == i1, -jnp.inf, logits)
    v2 = jnp.max(rest, axis=1, keepdims=True)
    i2 = jnp.min(jnp.where(rest == v2, lane, float(LANES)), axis=1, keepdims=True)
    e2 = jnp.exp(v2 - v1)
    g1 = 1.0 / (1.0 + e2)
    g2 = e2 / (1.0 + e2)
    out = jnp.where(lane == ROUTE_IDS, i1, 0.0) + jnp.where(lane == ROUTE_IDS + 1, i2, 0.0)
    out = out + jnp.where(lane == ROUTE_GATES, g1, 0.0) + jnp.where(lane == ROUTE_GATES + 1, g2, 0.0)
    o_ref[...] = out


def route_top2(x, router_w, tm=512):
    m, d = x.shape
    n_experts = router_w.shape[1]
    w = jnp.zeros((d, LANES), f32).at[:, :n_experts].set(router_w)
    return pl.pallas_call(
        functools.partial(_router_kernel, n_experts=n_experts),
        grid=(m // tm,),
        in_specs=[pl.BlockSpec((tm, d), lambda i: (i, 0)), pl.BlockSpec((d, LANES), lambda i: (0, 0))],
        out_specs=pl.BlockSpec((tm, LANES), lambda i: (i, 0)),
        out_shape=jax.ShapeDtypeStruct((m, LANES), f32),
        compiler_params=_params("parallel"),
        name="route_top2",
    )(x, w)


def _moe_ln_kernel(tok_ref, gs_ref, off_ref, x_ref, w1_ref, w3_ref, w2_ref, g_ref, b_ref, o_ref,
                   xg_ref, xb_ref, y_ref, *, rt):
    c = pl.program_id(0)
    e = pl.program_id(1)
    f = pl.program_id(2)
    last_f = pl.num_programs(2) - 1
    start = off_ref[0, e]
    count = off_ref[0, e + 1] - start
    n_tiles = (count + rt - 1) // rt

    @pl.when((c == 0) & (e == 0) & (f == 0))
    def _():
        xg_ref[...] = jnp.zeros_like(xg_ref)

    @pl.when((e == 0) & (f == 0))
    def _():
        o_ref[...] = jnp.zeros_like(o_ref)

    def row_loop(body):
        def group(j, carry):
            for u in range(ROW_UNROLL):
                body(j * ROW_UNROLL + u)
            return carry

        def single(r, carry):
            body(r)
            return carry
        lax.fori_loop(0, count // ROW_UNROLL, group, 0)
        lax.fori_loop((count // ROW_UNROLL) * ROW_UNROLL, count, single, 0)

    @pl.when(f == 0)
    def _():
        def gather(r):
            t = tok_ref[0, start + r]
            xg_ref[pl.ds(r, 1), :] = x_ref[pl.ds(t, 1), :]
        row_loop(gather)

        def cast(j, carry):
            rows = pl.ds(pl.multiple_of(j * rt, rt), rt)
            xb_ref[rows, :] = xg_ref[rows, :].astype(MXU_DTYPE)
            return carry
        lax.fori_loop(0, n_tiles, cast, 0)

    def tile(j, carry):
        rows = pl.ds(pl.multiple_of(j * rt, rt), rt)
        y = _swiglu_tile(xb_ref[rows, :], w1_ref, w3_ref, w2_ref)

        @pl.when(f == 0)
        def _():
            y_ref[rows, :] = y

        @pl.when(f != 0)
        def _():
            y_ref[rows, :] += y
        return carry
    lax.fori_loop(0, n_tiles, tile, 0)

    @pl.when(f == last_f)
    def _():
        def updated(r):
            t = tok_ref[0, start + r]
            return t, o_ref[pl.ds(t, 1), :] + gs_ref[0, start + r] * y_ref[pl.ds(r, 1), :]

        def group(j, carry):
            rows = [updated(j * ROW_UNROLL + u) for u in range(ROW_UNROLL)]
            for t, row in rows:
                o_ref[pl.ds(t, 1), :] = row
            return carry

        def single(r, carry):
            t, row = updated(r)
            o_ref[pl.ds(t, 1), :] = row
            return carry
        lax.fori_loop(0, count // ROW_UNROLL, group, 0)
        lax.fori_loop((count // ROW_UNROLL) * ROW_UNROLL, count, single, 0)

    @pl.when((e == pl.num_programs(1) - 1) & (f == last_f))
    def _():
        o_ref[...] = _layer_norm_rows(DEEPNORM_ALPHA * x_ref[...] + o_ref[...], g_ref[...], b_ref[...])


def moe_residual_ln(x, routes, w1, w3, w2, g, b, tm, tf, rt=128):
    m, d = x.shape
    n_experts, _, ff = w1.shape
    n_chunks = m // tm
    ids = routes[:, ROUTE_IDS:ROUTE_IDS + TOP_K].astype(i32).reshape(n_chunks, tm * TOP_K)
    gts = routes[:, ROUTE_GATES:ROUTE_GATES + TOP_K].reshape(n_chunks, tm * TOP_K)
    order = jnp.argsort(ids, axis=1, stable=True).astype(i32)
    tok = order // TOP_K
    gs = jnp.take_along_axis(gts, order, axis=1)
    counts = jnp.sum(ids[:, :, None] == jnp.arange(n_experts, dtype=i32)[None, None, :], axis=1, dtype=i32)
    offs = jnp.concatenate([jnp.zeros((n_chunks, 1), i32), jnp.cumsum(counts, axis=1, dtype=i32)], axis=1)
    smem = lambda width: pl.BlockSpec((None, 1, width), lambda c, e, f: (c, 0, 0), memory_space=pltpu.SMEM)
    return pl.pallas_call(
        functools.partial(_moe_ln_kernel, rt=rt),
        grid=(n_chunks, n_experts, ff // tf),
        in_specs=[smem(tm * TOP_K), smem(tm * TOP_K), smem(n_experts + 1),
                  pl.BlockSpec((tm, d), lambda c, e, f: (c, 0), pipeline_mode=pl.Buffered(1)),
                  pl.BlockSpec((None, d, tf), lambda c, e, f: (e, 0, f)),
                  pl.BlockSpec((None, d, tf), lambda c, e, f: (e, 0, f)),
                  pl.BlockSpec((None, tf, d), lambda c, e, f: (e, f, 0)),
                  pl.BlockSpec((1, d), lambda c, e, f: (0, 0)), pl.BlockSpec((1, d), lambda c, e, f: (0, 0))],
        out_specs=pl.BlockSpec((tm, d), lambda c, e, f: (c, 0), pipeline_mode=pl.Buffered(1)),
        out_shape=jax.ShapeDtypeStruct((m, d), f32),
        scratch_shapes=[pltpu.VMEM((tm, d), f32), pltpu.VMEM((tm, d), MXU_DTYPE), pltpu.VMEM((tm, d), f32)],
        compiler_params=_params("arbitrary", "arbitrary", "arbitrary"),
        name="moe_residual_ln",
    )(tok[:, None, :], gs[:, None, :], offs[:, None, :], x, w1, w3, w2, g.reshape(1, d), b.reshape(1, d))


def _flash_reset(m_ref, acc_ref):
    m_ref[...] = jnp.full(m_ref.shape, NEG, f32)
    acc_ref[...] = jnp.zeros(acc_ref.shape, f32)


def _flash_rows(q, kt, bias, m_ref, alpha_ref, p_ref, rows):
    s = _mm(q, kt)
    if bias is not None:
        s = s + bias
    m_prev = m_ref[rows, :]
    m_next = jnp.maximum(m_prev, jnp.max(s, axis=1, keepdims=True))
    alpha_ref[rows, :] = jnp.exp2(m_prev - m_next)
    m_ref[rows, :] = m_next
    p_ref[rows, :s.shape[1]] = jnp.exp2(s - _lane_tile(m_next, s.shape[1] // LANES)).astype(p_ref.dtype)


def _flash_accumulate(v_aug, alpha_ref, p_ref, acc_ref):
    alpha = _lane_tile(alpha_ref[...], acc_ref.shape[-1] // LANES)
    acc_ref[...] = alpha * acc_ref[...] + _mm(p_ref[:, :v_aug.shape[0]], v_aug)


def _skewed_chunks(n, score, fold, carry):
    def pair(j, carry):
        c = 2 * j
        fold(c - 1, 1)
        carry = score(c, 0, carry)
        fold(c, 0)
        return score(c + 1, 1, carry)

    def single(c, carry):
        fold(c - 1, 1)
        return score(c, 0, carry)

    carry = lax.fori_loop(0, n // 2, pair, carry)
    return lax.fori_loop(2 * (n // 2), n, single, carry)


def _fold_last(n, fold):
    for buf in range(2):
        pl.when((n - 1) % 2 == buf)(functools.partial(fold, n - 1, buf))


def _lane_tile(x, reps):
    return x if reps == 1 else jnp.concatenate([x] * reps, axis=1)


def _row_slices(n_rows):
    return [slice(r * Q_BLOCK, (r + 1) * Q_BLOCK) for r in range(n_rows // Q_BLOCK)]


def _dsa_kernel(qi_ref, wi_ref, kit_ref, qa_ref, kat_ref, va_ref, o_ref,
                skey_ref, wrep_ref, m_ref, alpha_ref, acc_ref, p_ref, qh_ref, *, topk, kc):
    qb = Q_BLOCK
    i = pl.program_id(1)
    q0 = i * qb
    n_chunks = (q0 + qb + kc - 1) // kc
    t_row = q0 + lax.broadcasted_iota(i32, (qb, kc), 0)
    col = lax.broadcasted_iota(i32, (qb, kc), 1)
    wi = wi_ref[...]
    qi = qi_ref[...]
    qis = [qi[:, h * IDX_DIM:(h + 1) * IDX_DIM] for h in range(IDX_HEADS)]
    qa = qa_ref[...]
    for h in range(A_HEADS):
        qh_ref[h * qb:(h + 1) * qb, :] = qa[:, h * HEAD_DIM:(h + 1) * HEAD_DIM]
    for h in range(IDX_HEADS):
        wrep_ref[h] = jnp.broadcast_to(wi[:, h:h + 1], (qb, kc))

    def score_body(c, carry):
        off = pl.multiple_of(c * kc, kc)
        kt = kit_ref[:, pl.ds(off, kc)]
        s = jnp.zeros((qb, kc), f32)
        for h in range(IDX_HEADS):
            s = s + wrep_ref[h] * jnp.maximum(_mm(qis[h], kt), 0.0)
        s = jnp.where(col + off <= t_row, s, NEG)
        bits = pltpu.bitcast(s, i32)
        skey_ref[:, pl.ds(off, kc)] = jnp.where(bits < 0, bits ^ 0x7FFFFFFF, bits)
        return carry

    n_strips = (q0 + qb + COUNT_STRIP - 1) // COUNT_STRIP
    lax.fori_loop(0, n_strips * (COUNT_STRIP // kc), score_body, 0)

    def count_ge(cand):
        def body(c, acc):
            off = pl.multiple_of(c * COUNT_STRIP, COUNT_STRIP)
            for j in range(COUNT_STRIP // LANES):
                acc = acc + jnp.where(skey_ref[:, pl.ds(off + j * LANES, LANES)] >= cand, 1.0, 0.0)
            return acc
        acc = lax.fori_loop(0, n_strips, body, jnp.zeros((qb, LANES), f32))
        return jnp.sum(acc, axis=1, keepdims=True)

    thr = jnp.where(count_ge(jnp.zeros((qb, 1), i32)) >= topk, 0, INT_MIN).astype(i32)

    def bisect(b, thr):
        cand = thr + jnp.left_shift(jnp.int32(1), 30 - b)
        return jnp.where(count_ge(cand) >= topk, cand, thr)

    thr = lax.fori_loop(0, 31, bisect, thr)
    quota = topk - count_ge(thr + 1)

    before = (lax.broadcasted_iota(i32, (kc, kc), 0) < lax.broadcasted_iota(i32, (kc, kc), 1))
    before = jnp.where(before, 1.0, 0.0).astype(MXU_DTYPE)
    _flash_reset(m_ref, acc_ref)
    alpha_ref[...] = jnp.zeros(alpha_ref.shape, f32)
    p_ref[...] = jnp.zeros(p_ref.shape, p_ref.dtype)
    heads = _row_slices(A_HEADS * qb)

    def fold(c, buf):
        off = pl.multiple_of(jnp.maximum(c, 0) * kc, kc)
        _flash_accumulate(va_ref[pl.ds(off, kc), :], alpha_ref.at[buf], p_ref.at[buf], acc_ref)

    def score(c, buf, ties_seen):
        off = pl.multiple_of(c * kc, kc)
        key = skey_ref[:, pl.ds(off, kc)]
        eq = key == thr
        eqf = jnp.where(eq, 1.0, 0.0)
        rank = ties_seen + _mm(eqf.astype(MXU_DTYPE), before)
        sel = ((key > thr) | (eq & (rank < quota))) & (col + off <= t_row)
        bias = jnp.where(sel, 0.0, NEG)
        kt = kat_ref[:, pl.ds(off, kc)]
        for rows in heads:
            _flash_rows(qh_ref[rows, :], kt, bias, m_ref, alpha_ref.at[buf], p_ref.at[buf], rows)
        return ties_seen + jnp.sum(eqf, axis=1, keepdims=True)

    _skewed_chunks(n_chunks, score, fold, jnp.zeros((qb, 1), f32))
    _fold_last(n_chunks, fold)
    acc = acc_ref[...]
    o = acc[:, :HEAD_DIM] / acc[:, HEAD_DIM:]
    for h in range(A_HEADS):
        o_ref[:, h * HEAD_DIM:(h + 1) * HEAD_DIM] = o[h * qb:(h + 1) * qb, :]


QPK_QA, QPK_QIDX, QPK_QB = 0, 2, 3
KPK_KA, KPK_KIDX, KPK_KSLC, KPK_KWIN = 0, 1, 2, 4
VPK_VA, VPK_VSLC, VPK_VWIN = 0, 1, 3
OUT_NSA = 2


def dsa_attention(wi, qpk, kpk, vpk, topk, kc=512):
    bsz, seq, _ = qpk.shape
    nb = seq // Q_BLOCK
    rows = A_HEADS * Q_BLOCK
    width = A_HEADS * HEAD_DIM
    kern = functools.partial(_dsa_kernel, topk=topk, kc=kc)
    return pl.pallas_call(
        kern,
        grid=(bsz, nb),
        in_specs=[pl.BlockSpec((None, Q_BLOCK, IDX_HEADS * IDX_DIM), lambda b, i: (b, i, QPK_QIDX)),
                  pl.BlockSpec((None, Q_BLOCK, IDX_HEADS), lambda b, i: (b, i, 0)),
                  pl.BlockSpec((None, IDX_DIM, seq), lambda b, i: (b, KPK_KIDX, 0)),
                  pl.BlockSpec((None, Q_BLOCK, width), lambda b, i: (b, i, QPK_QA)),
                  pl.BlockSpec((None, HEAD_DIM, seq), lambda b, i: (b, KPK_KA, 0)),
                  pl.BlockSpec((None, seq, 2 * HEAD_DIM), lambda b, i: (b, 0, VPK_VA))],
        out_specs=pl.BlockSpec((None, Q_BLOCK, width), lambda b, i: (b, i, 0)),
        out_shape=jax.ShapeDtypeStruct((bsz, seq, 2 * width), f32),
        scratch_shapes=[pltpu.VMEM((Q_BLOCK, seq), i32), pltpu.VMEM((IDX_HEADS, Q_BLOCK, kc), f32),
                        pltpu.VMEM((rows, LANES), f32), pltpu.VMEM((2, rows, LANES), f32),
                        pltpu.VMEM((rows, 2 * HEAD_DIM), f32), pltpu.VMEM((2, rows, kc), MXU_DTYPE),
                        pltpu.VMEM((rows, HEAD_DIM), MXU_DTYPE)],
        compiler_params=_params("parallel", "arbitrary"),
        name="dsa_attention",
    )(qpk, wi, kpk, qpk, kpk, vpk)


def _compress_kernel(x_ref, pe_ref, w1_ref, w2_ref, o_ref):
    half = (CMP_LEN // 2) * HEAD_DIM
    x = x_ref[...]
    first = _mm((x + pe_ref[0:1, :]).astype(MXU_DTYPE), w1_ref[0:half, :])
    second = _mm((x + pe_ref[1:2, :]).astype(MXU_DTYPE), w1_ref[half:2 * half, :])
    pre = first + pltpu.roll(second, shift=x.shape[0] - 1, axis=0)
    hid = pre * jax.nn.sigmoid(pre)
    o_ref[...] = _mm(hid.astype(MXU_DTYPE), w2_ref[...])


def nsa_compress(chunks, pe2, w1, w2):
    bsz, groups, nck, width = chunks.shape
    return pl.pallas_call(
        _compress_kernel,
        grid=(bsz, groups),
        in_specs=[pl.BlockSpec((None, None, nck, width), lambda b, g: (b, g, 0, 0)),
                  pl.BlockSpec((2, width), lambda b, g: (0, 0)),
                  pl.BlockSpec((2 * width, CMP_HIDDEN), lambda b, g: (0, 0)),
                  pl.BlockSpec((CMP_HIDDEN, HEAD_DIM), lambda b, g: (0, 0))],
        out_specs=pl.BlockSpec((None, None, nck, HEAD_DIM), lambda b, g: (b, g, 0, 0)),
        out_shape=jax.ShapeDtypeStruct((bsz, groups, nck, HEAD_DIM), f32),
        compiler_params=_params("parallel", "parallel"),
        name="nsa_compress",
    )(chunks, pe2, w1, w2)


def _nsa_kernel(qin_ref, graw_ref, kct_ref, vc_ref, kst_ref, vs_ref, kwt_ref, vw_ref, ovl_ref, _, o_ref,
                m_ref, alpha_ref, acc_ref, p_ref, pw_ref, q_ref, imp_ref, *, n_slc, n_sel, kc):
    qb = Q_BLOCK
    i = pl.program_id(2)
    q0 = i * qb
    n_cmp = kct_ref.shape[1]
    heads = _row_slices(B_REP * qb)
    qin = qin_ref[...]
    for r in range(B_REP):
        q_ref[r * qb:(r + 1) * qb, :] = qin[:, r * HEAD_DIM:(r + 1) * HEAD_DIM]

    def compressed(width):
        t_c = q0 + lax.broadcasted_iota(i32, (qb, width), 0)
        cmp_end = lax.broadcasted_iota(i32, (qb, width), 1) * CMP_STRIDE + (CMP_LEN - 1)
        vis = cmp_end <= t_c
        kct = kct_ref[:, :width]
        p_sum = jnp.zeros((qb, width), f32)
        for rows in heads:
            lc = jnp.where(vis, _mm(q_ref[rows, :], kct), NEG)
            ec = jnp.where(vis, jnp.exp2(lc - jnp.max(lc, axis=1, keepdims=True)), 0.0)
            den = jnp.sum(ec, axis=1, keepdims=True)
            p_c = ec / jnp.where(den > 0.0, den, 1.0)
            p_sum = p_sum + p_c
            acc_ref[rows, :HEAD_DIM] = _mm(p_c.astype(MXU_DTYPE), vc_ref[:width, :])
        imp_ref[...] = jnp.dot(p_sum, ovl_ref[:width, :], preferred_element_type=f32,
                               precision=lax.Precision.HIGHEST)

    if n_cmp % (2 * LANES) == 0:
        first_half_only = (q0 + qb) * 2 <= n_cmp * CMP_STRIDE
        pl.when(first_half_only)(functools.partial(compressed, n_cmp // 2))
        pl.when(jnp.logical_not(first_half_only))(functools.partial(compressed, n_cmp))
    else:
        compressed(n_cmp)
    o_c = acc_ref[:, :HEAD_DIM]

    imp = imp_ref[...].T
    t_q = q0 + lax.broadcasted_iota(i32, (LANES, qb), 1)
    blk = lax.broadcasted_iota(i32, (LANES, qb), 0)
    blk_t = t_q // SLC_LEN
    forced = (blk == 0) | (blk == blk_t) | (blk == blk_t - 1)
    imp = jnp.where(forced, FORCED_BOOST, imp)
    imp = jnp.where(blk * SLC_LEN <= t_q, imp, NEG)
    imp = jnp.where(blk < n_slc, imp, -jnp.inf)
    blk_f = blk.astype(f32)

    def pick(_, carry):
        imp, selm = carry
        best = jnp.max(imp, axis=0, keepdims=True)
        first = jnp.min(jnp.where(imp == best, blk_f, float(LANES)), axis=0, keepdims=True)
        hit = blk_f == first
        return jnp.where(hit, -jnp.inf, imp), jnp.where(hit, 1.0, selm)

    _, selm = lax.fori_loop(0, n_sel, pick, (imp, jnp.zeros((LANES, qb), f32)))
    selm = selm.T.astype(MXU_DTYPE)

    t_k = q0 + lax.broadcasted_iota(i32, (qb, kc), 0)
    col = lax.broadcasted_iota(i32, (qb, kc), 1)
    exp_row = lax.broadcasted_iota(i32, (LANES, kc), 0)
    exp_col = lax.broadcasted_iota(i32, (LANES, kc), 1)
    _flash_reset(m_ref, acc_ref)
    alpha_ref[...] = jnp.zeros(alpha_ref.shape, f32)
    p_ref[...] = jnp.zeros(p_ref.shape, p_ref.dtype)

    def fold(c, buf):
        off = pl.multiple_of(jnp.maximum(c, 0) * kc, kc)
        _flash_accumulate(vs_ref[pl.ds(off, kc), :], alpha_ref.at[buf], p_ref.at[buf], acc_ref)

    def score(c, buf, carry):
        off = pl.multiple_of(c * kc, kc)
        expand = jnp.where(exp_row == (exp_col + off) // SLC_LEN, 1.0, 0.0).astype(MXU_DTYPE)
        sel = (_mm(selm, expand) > 0.5) & (col + off <= t_k)
        bias = jnp.where(sel, 0.0, NEG)
        kt = kst_ref[:, pl.ds(off, kc)]
        for rows in heads:
            _flash_rows(q_ref[rows, :], kt, bias, m_ref, alpha_ref.at[buf], p_ref.at[buf], rows)
        return carry

    n_slc_chunks = (q0 + qb + kc - 1) // kc
    _skewed_chunks(n_slc_chunks, score, fold, 0)
    _fold_last(n_slc_chunks, fold)
    acc = acc_ref[...]
    o_s = acc[:, :HEAD_DIM] / acc[:, HEAD_DIM:]

    slab = WINDOW + qb
    w0 = pl.multiple_of(jnp.maximum(q0 - WINDOW, 0), qb)
    dist = (q0 + lax.broadcasted_iota(i32, (qb, slab), 0)) - (w0 + lax.broadcasted_iota(i32, (qb, slab), 1))
    bias = jnp.where((dist >= 0) & (dist < WINDOW), 0.0, NEG)
    kt = kwt_ref[:, pl.ds(w0, slab)]
    for rows in heads:
        s = _mm(q_ref[rows, :], kt) + bias
        pw_ref[rows, :] = jnp.exp2(s - jnp.max(s, axis=1, keepdims=True)).astype(pw_ref.dtype)
    acc = _mm(pw_ref[...], vw_ref[pl.ds(w0, slab), :])
    o_w = acc[:, :HEAD_DIM] / acc[:, HEAD_DIM:]

    gates = jax.nn.sigmoid(graw_ref[...])
    for r, rows in enumerate(heads):
        g_c, g_s, g_w = (gates[:, 3 * r + n:3 * r + n + 1] for n in range(3))
        o_ref[:, r * HEAD_DIM:(r + 1) * HEAD_DIM] = g_c * o_c[rows] + g_s * o_s[rows] + g_w * o_w[rows]


def nsa_attention(qpk, graw, kct, vc, kpk, vpk, ovl, out_ab, n_slc, n_sel, kc=512):
    bsz, seq, _ = qpk.shape
    groups = B_KV_GROUPS
    nb = seq // Q_BLOCK
    rows = B_REP * Q_BLOCK
    width = B_REP * HEAD_DIM
    n_cmp = kct.shape[-1]
    kern = functools.partial(_nsa_kernel, n_slc=n_slc, n_sel=n_sel, kc=kc)
    per_bg = lambda b, g, i: (b, g, 0, 0)
    return pl.pallas_call(
        kern,
        grid=(bsz, groups, nb),
        in_specs=[pl.BlockSpec((None, Q_BLOCK, width), lambda b, g, i: (b, i, QPK_QB + g)),
                  pl.BlockSpec((None, None, Q_BLOCK, B_REP * 3), lambda b, g, i: (b, g, i, 0)),
                  pl.BlockSpec((None, None, HEAD_DIM, n_cmp), per_bg),
                  pl.BlockSpec((None, None, n_cmp, HEAD_DIM), per_bg),
                  pl.BlockSpec((None, HEAD_DIM, seq), lambda b, g, i: (b, KPK_KSLC + g, 0)),
                  pl.BlockSpec((None, seq, 2 * HEAD_DIM), lambda b, g, i: (b, 0, VPK_VSLC + g)),
                  pl.BlockSpec((None, HEAD_DIM, seq), lambda b, g, i: (b, KPK_KWIN + g, 0)),
                  pl.BlockSpec((None, seq, 2 * HEAD_DIM), lambda b, g, i: (b, 0, VPK_VWIN + g)),
                  pl.BlockSpec((n_cmp, LANES), lambda b, g, i: (0, 0)),
                  pl.BlockSpec(memory_space=pl.ANY)],
        out_specs=pl.BlockSpec((None, Q_BLOCK, width), lambda b, g, i: (b, i, OUT_NSA + g)),
        out_shape=jax.ShapeDtypeStruct(out_ab.shape, f32),
        input_output_aliases={9: 0},
        scratch_shapes=[pltpu.VMEM((rows, LANES), f32), pltpu.VMEM((2, rows, LANES), f32),
                        pltpu.VMEM((rows, 2 * HEAD_DIM), f32), pltpu.VMEM((2, rows, kc), MXU_DTYPE),
                        pltpu.VMEM((rows, WINDOW + Q_BLOCK), MXU_DTYPE), pltpu.VMEM((rows, HEAD_DIM), MXU_DTYPE),
                        pltpu.VMEM((Q_BLOCK, LANES), f32)],
        compiler_params=_params("parallel", "parallel", "arbitrary"),
        name="nsa_attention",
    )(qpk, graw, kct, vc, kpk, vpk, kpk, vpk, ovl, out_ab)


def _diff_kernel(lam_ref, qin_ref, kt_ref, v_ref, g_ref, o_ref, m_ref, alpha_ref, acc_ref, p_ref, q_ref,
                 *, tq, kc, out_scale):
    i = pl.program_id(2)
    q0 = i * tq
    dv = v_ref.shape[-1] // 2
    groups = _row_slices(tq)
    qin = qin_ref[...]
    for half in range(2):
        q_ref[half] = qin[:, half * HEAD_DIM:(half + 1) * HEAD_DIM]
    for half in range(2):
        _flash_reset(m_ref.at[half], acc_ref.at[half])
    alpha_ref[...] = jnp.zeros(alpha_ref.shape, f32)
    p_ref[...] = jnp.zeros(p_ref.shape, p_ref.dtype)

    def fold(c, buf, sub=(0, kc // tq)):
        off = pl.multiple_of(jnp.maximum(c, 0) * kc + sub[0] * tq, tq)
        v = v_ref[pl.ds(off, sub[1] * tq), :]
        for half in range(2):
            _flash_accumulate(v, alpha_ref.at[buf, half], p_ref.at[buf, half], acc_ref.at[half])

    def score(c, buf, masked, sub=(0, kc // tq)):
        off = pl.multiple_of(c * kc + sub[0] * tq, tq)
        width = sub[1] * tq
        for half in range(2):
            kt = kt_ref[half * HEAD_DIM:(half + 1) * HEAD_DIM, pl.ds(off, width)]
            for r, rows in enumerate(groups):
                bias = None
                if masked:
                    key = off + lax.broadcasted_iota(i32, (Q_BLOCK, width), 1)
                    t = q0 + r * Q_BLOCK + lax.broadcasted_iota(i32, (Q_BLOCK, width), 0)
                    bias = jnp.where(key <= t, 0.0, NEG)
                _flash_rows(q_ref[half, rows, :], kt, bias, m_ref.at[half], alpha_ref.at[buf, half],
                            p_ref.at[buf, half], rows)

    def score_full(c, buf, carry):
        score(c, buf, False)
        return carry

    n_full = q0 // kc
    _skewed_chunks(n_full, score_full, fold, 0)

    own = (q0 - n_full * kc) // tq

    def finish(buf, own_piece):
        fold(n_full - 1, 1 - buf)
        if own_piece:
            score(n_full, buf, False, (0, own_piece))
            fold(n_full, buf, (0, own_piece))
            buf = 1 - buf
        score(n_full, buf, True, (own_piece, 1))
        fold(n_full, buf, (own_piece, 1))

    for buf in range(2):
        for own_piece in range(kc // tq):
            pl.when((n_full % 2 == buf) & (own == own_piece))(functools.partial(finish, buf, own_piece))
    a1 = acc_ref[0]
    a2 = acc_ref[1]
    o = a1[:, :dv] / a1[:, dv:] - lam_ref[0] * (a2[:, :dv] / a2[:, dv:])
    o = o * lax.rsqrt(jnp.mean(o * o, axis=-1, keepdims=True) + LN_EPS) * g_ref[...]
    o_ref[...] = o * out_scale


def diff_attention(lam, q, kt, v, subln_g, out_scale, tq=512, kc=1024):
    bsz, seq, width = q.shape
    dv = 2 * HEAD_DIM
    heads = width // dv
    tq = min(tq, seq)
    kc = min(kc, seq)
    assert kc % tq == 0 and seq % kc == 0
    kern = functools.partial(_diff_kernel, tq=tq, kc=kc, out_scale=out_scale)
    return pl.pallas_call(
        kern,
        grid=(bsz, heads, seq // tq),
        in_specs=[pl.BlockSpec(memory_space=pltpu.SMEM),
                  pl.BlockSpec((None, tq, dv), lambda b, h, i: (b, i, h)),
                  pl.BlockSpec((None, dv, seq), lambda b, h, i: (b, h, 0)),
                  pl.BlockSpec((None, seq, 2 * dv), lambda b, h, i: (b, 0, h)),
                  pl.BlockSpec((1, dv), lambda b, h, i: (0, 0))],
        out_specs=pl.BlockSpec((None, tq, dv), lambda b, h, i: (b, i, h)),
        out_shape=jax.ShapeDtypeStruct((bsz, seq, heads * dv), f32),
        scratch_shapes=[pltpu.VMEM((2, tq, LANES), f32), pltpu.VMEM((2, 2, tq, LANES), f32),
                        pltpu.VMEM((2, tq, 2 * dv), f32), pltpu.VMEM((2, 2, tq, kc), MXU_DTYPE),
                        pltpu.VMEM((2, tq, HEAD_DIM), MXU_DTYPE)],
        compiler_params=_params("parallel", "parallel", "arbitrary"),
        name="diff_attention",
    )(lam, q, kt, v, subln_g.reshape(1, dv))


def _rope_tables(positions):
    inv_freq = ROPE_THETA ** (-jnp.arange(0, ROT_DIM, 2, dtype=f32) / ROT_DIM)
    ang = positions.astype(f32)[..., None] * inv_freq
    return jnp.cos(ang), jnp.sin(ang)


def _apply_rope(x, cos, sin):
    shape = cos.shape[:2] + (1,) * (x.ndim - 3) + cos.shape[-1:]
    c = cos.reshape(shape)
    s = sin.reshape(shape)
    half = ROT_DIM // 2
    x1, x2 = x[..., :half], x[..., half:ROT_DIM]
    return jnp.concatenate([x1 * c - x2 * s, x2 * c + x1 * s, x[..., ROT_DIM:]], axis=-1)


def _rope_coefficients(cos, sin):
    lead = cos.shape[:-1]
    rest = jnp.zeros(lead + (HEAD_DIM - ROT_DIM,), f32)
    none = jnp.zeros_like(sin)
    twice = lambda a: jnp.concatenate([a, a], axis=-1)
    return (twice(jnp.concatenate([cos, cos, rest + 1.0], axis=-1)),
            twice(jnp.concatenate([-sin, none, rest], axis=-1)),
            twice(jnp.concatenate([none, sin, rest], axis=-1)))


def _ab_weight_columns(w_in):
    widths = dict(AB_LAYOUT)
    starts = dict(zip(widths, np.cumsum([0] + [w for _, w in AB_LAYOUT[:-1]]).tolist()))
    cols = lambda names: np.concatenate([np.arange(starts[n], starts[n] + widths[n]) for n in names])
    first = ("q_a", "q_idx", "q_b", "k_a", "k_idx", "k_slc", "k_win", "v_a", "v_slc", "v_win", "gate_b", "w_idx")
    n_first = sum(widths[n] for n in first)
    pad = AB_CMP_TILE * LANES - n_first
    assert 0 <= pad < HEAD_DIM
    w = jnp.concatenate([w_in[:, cols(first)], jnp.zeros((w_in.shape[0], pad), w_in.dtype),
                         w_in[:, cols(("k_cmp", "v_cmp"))]], axis=1)
    assert w.shape[1] == AB_TILES * LANES
    return w.astype(MXU_DTYPE)


def _overlap_matrix(n_cmp_rows, n_slc):
    c_start = np.arange(n_cmp_rows) * CMP_STRIDE
    s_start = np.arange(LANES) * SLC_LEN
    ovl = (c_start[:, None] < s_start[None, :] + SLC_LEN) & (c_start[:, None] + CMP_LEN > s_start[None, :])
    ovl = ovl & (np.arange(LANES)[None, :] < n_slc)
    return jnp.asarray(ovl.astype(np.float32))


def _ab_mixer(x2, bsz, seq, positions, coef, w_in, pe_k, pe_v, ck1, ck2, cv1, cv2):
    qpk, kpk, vpk, misc, cmp = project_pack_ab(x2.reshape(bsz, seq, -1), _ab_weight_columns(w_in), coef)
    gate_w = B_HEADS * 3
    g_b = misc[..., HEAD_DIM:HEAD_DIM + gate_w]
    w_idx = misc[..., HEAD_DIM + gate_w:HEAD_DIM + gate_w + IDX_HEADS] * (IDX_HEADS * IDX_DIM) ** -0.5
    k_cmp, v_cmp = cmp[..., :LANES], cmp[..., LANES:]

    out_ab = dsa_attention(w_idx, qpk, kpk, vpk, topk=min(DSA_TOPK, seq // 4))

    groups = B_KV_GROUPS
    n_rows = seq // CMP_STRIDE
    n_cmp = (seq - CMP_LEN) // CMP_STRIDE + 1
    assert n_cmp == n_rows - 1
    n_slc = seq // SLC_LEN
    assert n_slc <= LANES

    def chunked(kv):
        kv = kv.reshape(bsz, n_rows, CMP_STRIDE, groups, HEAD_DIM).transpose(0, 3, 1, 2, 4)
        return kv.reshape(bsz, groups, n_rows, CMP_STRIDE * HEAD_DIM)

    def pe_halves(pe):
        return pe.reshape(2, CMP_STRIDE * HEAD_DIM)

    k_c = nsa_compress(chunked(k_cmp), pe_halves(pe_k), ck1.astype(MXU_DTYPE), ck2.astype(MXU_DTYPE))
    v_c = nsa_compress(chunked(v_cmp), pe_halves(pe_v), cv1.astype(MXU_DTYPE), cv2.astype(MXU_DTYPE))
    cmp_end = jnp.minimum(jnp.arange(n_rows) * CMP_STRIDE + CMP_LEN - 1, seq - 1)
    cos_c, sin_c = _rope_tables(positions[:, cmp_end])
    k_c = _apply_rope(k_c.transpose(0, 2, 1, 3), cos_c, sin_c)
    k_c = k_c.transpose(0, 2, 3, 1).astype(MXU_DTYPE)
    v_c = v_c.astype(MXU_DTYPE)

    graw = g_b.reshape(bsz, seq, groups, B_REP * 3).transpose(0, 2, 1, 3)
    out_ab = nsa_attention(qpk, graw, k_c, v_c, kpk, vpk, _overlap_matrix(n_rows, n_slc), out_ab,
                           n_slc=n_slc, n_sel=min(SLC_TOPN, n_slc))
    return out_ab.reshape(bsz * seq, (A_HEADS + B_HEADS) * HEAD_DIM)


def _diff_mixer(x2, bsz, seq, coef, w_in, lq1, lk1, lq2, lk2, subln_g, lam_init):
    q, kt, v = project_pack_c(x2.reshape(bsz, seq, -1), w_in.astype(MXU_DTYPE), coef)
    lam =(jnp.exp(jnp.sum(lq1 * lk1)) - jnp.exp(jnp.sum(lq2 * lk2)) + lam_init).reshape(1).astype(f32)
    o = diff_attention(lam, q, kt, v, subln_g, 1.0 - lam_init)
    return o.reshape(bsz * seq, C_HEADS * 2 * HEAD_DIM)


def kernel(x, positions, ab_w_in, cmp_pe_k, cmp_pe_v, cmp_k_w1, cmp_k_w2, cmp_v_w1, cmp_v_w2, ab_w_out, ln_ab_g, ln_ab_b, ffn_w1, ffn_w3, ffn_w2, ln_ffn_g, ln_ffn_b, c_w_in, lambda_q1, lambda_k1, lambda_q2, lambda_k2, c_subln_g, c_w_out, ln_c_g, ln_c_b, router_w, moe_w1, moe_w3, moe_w2, ln_moe_g, ln_moe_b):
    bsz, seq, d = x.shape
    assert seq % COUNT_STRIP == 0 and seq >= WINDOW + Q_BLOCK and d == D_MODEL
    coef = _rope_coefficients(*_rope_tables(positions))
    x2 = x.reshape(bsz * seq, d)
    for layer in range(DEPTH):
        i = layer // 2
        if layer % 2 == 0:
            o = _ab_mixer(x2, bsz, seq, positions, coef, ab_w_in[i], cmp_pe_k[i], cmp_pe_v[i],
                          cmp_k_w1[i], cmp_k_w2[i], cmp_v_w1[i], cmp_v_w2[i])
            x2 = project_residual_ln(o, ab_w_out[i].astype(MXU_DTYPE), x2, ln_ab_g[i], ln_ab_b[i])
            x2 = ffn_residual_ln(x2, ffn_w1[i].astype(MXU_DTYPE), ffn_w3[i].astype(MXU_DTYPE),
                                 ffn_w2[i].astype(MXU_DTYPE), ln_ffn_g[i], ln_ffn_b[i], tm=512, tf=1408)
        else:
            lam_init = 0.8 - 0.6 * math.exp(-0.3 * layer)
            o = _diff_mixer(x2, bsz, seq, coef, c_w_in[i], lambda_q1[i], lambda_k1[i], lambda_q2[i],
                            lambda_k2[i], c_subln_g[i], lam_init)
            x2 = project_residual_ln(o, c_w_out[i].astype(MXU_DTYPE), x2, ln_c_g[i], ln_c_b[i])
            routes = route_top2(x2, router_w[i])
            x2 = moe_residual_ln(x2, routes, moe_w1[i].astype(MXU_DTYPE), moe_w3[i].astype(MXU_DTYPE),
                                 moe_w2[i].astype(MXU_DTYPE), ln_moe_g[i], ln_moe_b[i], tm=min(2048, bsz * seq // 2), tf=896, rt=128)
    return x2.reshape(bsz, seq, d)
```

```python
import functools
import math

import numpy as np
import jax
import jax.numpy as jnp
from jax import lax
from jax.experimental import pallas as pl
from jax.experimental.pallas import tpu as pltpu

f32 = jnp.float32
i32 = jnp.int32
MXU_DTYPE = jnp.bfloat16
VMEM_LIMIT_BYTES = 56 * 1024 * 1024
LANES = 128

D_MODEL = 1024
DEPTH = 2
HEAD_DIM = 64
ROT_DIM = HEAD_DIM // 4
ROPE_THETA = 500000.0
Q_BLOCK = 128
NEG = -1e30
LN_EPS = 1e-5
A_HEADS = 8
IDX_HEADS = 4
IDX_DIM = 64
DSA_TOPK = 256
B_HEADS = 8
B_KV_GROUPS = 2
B_REP = B_HEADS // B_KV_GROUPS
CMP_LEN = 32
CMP_STRIDE = 16
CMP_HIDDEN = 128
SLC_LEN = 64
SLC_TOPN = 16
WINDOW = 512
FORCED_BOOST = 1e6
C_HEADS = 8
N_EXPERTS = 8
TOP_K = 2
DEEPNORM_ALPHA = (2 * DEPTH) ** 0.25
QK_SCALE = HEAD_DIM ** -0.5 * math.log2(math.e)
INT_MIN = -(2 ** 31)
HI16 = -(2 ** 16)
EXPONENT_BITS = 0x7F800000
EXPONENT_LSB = 0x00800000
COUNT_STRIP = 512
ROW_UNROLL = 4

AB_LAYOUT = (
    ("q_a", A_HEADS * HEAD_DIM), ("k_a", HEAD_DIM), ("v_a", HEAD_DIM),
    ("q_idx", IDX_HEADS * IDX_DIM), ("k_idx", IDX_DIM), ("w_idx", IDX_HEADS),
    ("q_b", B_HEADS * HEAD_DIM),
    ("k_cmp", B_KV_GROUPS * HEAD_DIM), ("v_cmp", B_KV_GROUPS * HEAD_DIM),
    ("k_slc", B_KV_GROUPS * HEAD_DIM), ("v_slc", B_KV_GROUPS * HEAD_DIM),
    ("k_win", B_KV_GROUPS * HEAD_DIM), ("v_win", B_KV_GROUPS * HEAD_DIM),
    ("gate_b", 3 * B_HEADS),
)


def _params(*sem):
    return pltpu.CompilerParams(dimension_semantics=sem, vmem_limit_bytes=VMEM_LIMIT_BYTES)


def _mm(a, b):
    return jnp.dot(a, b, preferred_element_type=f32)


def _layer_norm_rows(y, g, b):
    mu = jnp.mean(y, axis=-1, keepdims=True)
    yc = y - mu
    var = jnp.mean(yc * yc, axis=-1, keepdims=True)
    return yc * lax.rsqrt(var + LN_EPS) * g + b


def _rope_tile(x, keep, hi, lo):
    return x * keep + pltpu.roll(x, LANES - ROT_DIM // 2, 1) * hi + pltpu.roll(x, ROT_DIM // 2, 1) * lo


def _tile(h, j):
    return h[:, j * LANES:(j + 1) * LANES]


AB_Q_TILES = 10
AB_QIDX_TILES = (4, 5)
AB_K_TILES = 3
AB_V_HEADS = 5
AB_MISC_TILE = 15
AB_CMP_TILE = 16
AB_TILES = 18


def _proj_pack_ab_kernel(x_ref, w_ref, keep_ref, hi_ref, lo_ref, q_ref, kt_ref, v_ref, misc_ref, cmp_ref):
    h = _mm(x_ref[...].astype(MXU_DTYPE), w_ref[...])
    keep, hi, lo = keep_ref[...], hi_ref[...], lo_ref[...]
    for j in range(AB_Q_TILES):
        t = _rope_tile(_tile(h, j), keep, hi, lo)
        q_ref[:, j * LANES:(j + 1) * LANES] = (t if j in AB_QIDX_TILES else t * QK_SCALE).astype(q_ref.dtype)
    k = jnp.concatenate([_rope_tile(_tile(h, AB_Q_TILES + j), keep, hi, lo) for j in range(AB_K_TILES)], axis=1)
    kt_ref[...] = k.T.astype(kt_ref.dtype)
    low = lax.broadcasted_iota(i32, keep.shape, 1) < HEAD_DIM
    for j in range(AB_V_HEADS):
        t = _tile(h, AB_Q_TILES + AB_K_TILES + j // 2)
        t = pltpu.roll(t, HEAD_DIM, 1) if j % 2 else t
        v_ref[:, j * LANES:(j + 1) * LANES] = jnp.where(low, t, 1.0).astype(v_ref.dtype)
    misc_ref[...] = _tile(h, AB_MISC_TILE)
    cmp_ref[...] = h[:, AB_CMP_TILE * LANES:AB_TILES * LANES]


def project_pack_ab(x, w, coef, tm=512):
    bsz, seq, d = x.shape
    row = lambda width: pl.BlockSpec((None, tm, width), lambda b, i: (b, i, 0))
    kt_rows = AB_K_TILES * LANES
    return pl.pallas_call(
        _proj_pack_ab_kernel,
        grid=(bsz, seq // tm),
        in_specs=[row(d), pl.BlockSpec((d, AB_TILES * LANES), lambda b, i: (0, 0)), row(LANES), row(LANES), row(LANES)],
        out_specs=[row(AB_Q_TILES * LANES), pl.BlockSpec((None, kt_rows, tm), lambda b, i: (b, 0, i)),
                   row(AB_V_HEADS * LANES), row(LANES), row(2 * LANES)],
        out_shape=[jax.ShapeDtypeStruct((bsz, seq, AB_Q_TILES * LANES), MXU_DTYPE),
                   jax.ShapeDtypeStruct((bsz, kt_rows, seq), MXU_DTYPE),
                   jax.ShapeDtypeStruct((bsz, seq, AB_V_HEADS * LANES), MXU_DTYPE),
                   jax.ShapeDtypeStruct((bsz, seq, LANES), f32),
                   jax.ShapeDtypeStruct((bsz, seq, 2 * LANES), f32)],
        compiler_params=_params("parallel", "parallel"),
        name="project_pack_ab",
    )(x, w, *coef)


def _proj_pack_c_kernel(x_ref, w_ref, keep_ref, hi_ref, lo_ref, q_ref, kt_ref, v_ref):
    h = _mm(x_ref[...].astype(MXU_DTYPE), w_ref[...])
    keep, hi, lo = keep_ref[...], hi_ref[...], lo_ref[...]
    n = q_ref.shape[1] // LANES
    for j in range(n):
        q_ref[:, j * LANES:(j + 1) * LANES] = (_rope_tile(_tile(h, j), keep, hi, lo) * QK_SCALE).astype(q_ref.dtype)
    k = jnp.concatenate([_rope_tile(_tile(h, n + j), keep, hi, lo) for j in range(n)], axis=1)
    kt_ref[...] = k.T.astype(kt_ref.dtype)
    ones = jnp.ones(keep.shape, v_ref.dtype)
    for j in range(n):
        v_ref[:, 2 * j * LANES:(2 * j + 1) * LANES] = _tile(h, 2 * n + j).astype(v_ref.dtype)
        v_ref[:, (2 * j + 1) * LANES:(2 * j + 2) * LANES] = ones


def project_pack_c(x, w, coef, tm=512):
    bsz, seq, d = x.shape
    width = w.shape[1] // 3
    row = lambda cols: pl.BlockSpec((None, tm, cols), lambda b, i: (b, i, 0))
    return pl.pallas_call(
        _proj_pack_c_kernel,
        grid=(bsz, seq // tm),
        in_specs=[row(d), pl.BlockSpec((d, 3 * width), lambda b, i: (0, 0)), row(LANES), row(LANES), row(LANES)],
        out_specs=[row(width), pl.BlockSpec((None, width, tm), lambda b, i: (b, 0, i)), row(2 * width)],
        out_shape=[jax.ShapeDtypeStruct((bsz, seq, width), MXU_DTYPE),
                   jax.ShapeDtypeStruct((bsz, width, seq), MXU_DTYPE),
                   jax.ShapeDtypeStruct((bsz, seq, 2 * width), MXU_DTYPE)],
        compiler_params=_params("parallel", "parallel"),
        name="project_pack_c",
    )(x, w, *coef)


def _proj_ln_kernel(a_ref, w_ref, res_ref, g_ref, b_ref, o_ref):
    h = _mm(a_ref[...].astype(MXU_DTYPE), w_ref[...])
    o_ref[...] = _layer_norm_rows(DEEPNORM_ALPHA * res_ref[...] + h, g_ref[...], b_ref[...])


def project_residual_ln(a, w, res, g, b, tm=512):
    m, k = a.shape
    n = w.shape[1]
    return pl.pallas_call(
        _proj_ln_kernel,
        grid=(m // tm,),
        in_specs=[pl.BlockSpec((tm, k), lambda i: (i, 0)), pl.BlockSpec((k, n), lambda i: (0, 0)),
                  pl.BlockSpec((tm, n), lambda i: (i, 0)),
                  pl.BlockSpec((1, n), lambda i: (0, 0)), pl.BlockSpec((1, n), lambda i: (0, 0))],
        out_specs=pl.BlockSpec((tm, n), lambda i: (i, 0)),
        out_shape=jax.ShapeDtypeStruct((m, n), f32),
        compiler_params=_params("parallel"),
        name="project_residual_ln",
    )(a, w, res, g.reshape(1, n), b.reshape(1, n))


def _swiglu_tile(xb, w1_ref, w3_ref, w2_ref):
    a = _mm(xb, w1_ref[...])
    h = (a * jax.nn.sigmoid(a)) * _mm(xb, w3_ref[...])
    return _mm(h.astype(MXU_DTYPE), w2_ref[...])


def _ffn_ln_kernel(x_ref, w1_ref, w3_ref, w2_ref, g_ref, b_ref, o_ref, xb_ref, acc_ref):
    f = pl.program_id(1)

    @pl.when(f == 0)
    def _():
        xb_ref[...] = x_ref[...].astype(MXU_DTYPE)
        acc_ref[...] = jnp.zeros_like(acc_ref)

    acc_ref[...] += _swiglu_tile(xb_ref[...], w1_ref, w3_ref, w2_ref)

    @pl.when(f == pl.num_programs(1) - 1)
    def _():
        o_ref[...] = _layer_norm_rows(DEEPNORM_ALPHA * x_ref[...] + acc_ref[...], g_ref[...], b_ref[...])


def ffn_residual_ln(x, w1, w3, w2, g, b, tm, tf):
    m, d = x.shape
    ff = w1.shape[1]
    return pl.pallas_call(
        _ffn_ln_kernel,
        grid=(m // tm, ff // tf),
        in_specs=[pl.BlockSpec((tm, d), lambda i, f: (i, 0)),
                  pl.BlockSpec((d, tf), lambda i, f: (0, f)),
                  pl.BlockSpec((d, tf), lambda i, f: (0, f)),
                  pl.BlockSpec((tf, d), lambda i, f: (f, 0)),
                  pl.BlockSpec((1, d), lambda i, f: (0, 0)), pl.BlockSpec((1, d), lambda i, f: (0, 0))],
        out_specs=pl.BlockSpec((tm, d), lambda i, f: (i, 0)),
        out_shape=jax.ShapeDtypeStruct((m, d), f32),
        scratch_shapes=[pltpu.VMEM((tm, d), MXU_DTYPE), pltpu.VMEM((tm, d), f32)],
        compiler_params=_params("parallel", "arbitrary"),
        name="ffn_residual_ln",
    )(x, w1, w3, w2, g.reshape(1, d), b.reshape(1, d))


ROUTE_IDS = N_EXPERTS
ROUTE_GATES = N_EXPERTS + 2


def _router_kernel(x_ref, w_ref, o_ref, *, n_experts):
    logits = jnp.dot(x_ref[...], w_ref[...], preferred_element_type=f32, precision=lax.Precision.HIGHEST)
    lane = lax.broadcasted_iota(i32, logits.shape, 1).astype(f32)
    logits = jnp.where(lane < n_experts, logits, -jnp.inf)
    v1 = jnp.max(logits, axis=1, keepdims=True)
    i1 = jnp.min(jnp.where(logits == v1, lane, float(LANES)), axis=1, keepdims=True)
    rest = jnp.where(lane == i1, -jnp.inf, logits)
    v2 = jnp.max(rest, axis=1, keepdims=True)
    i2 = jnp.min(jnp.where(rest == v2, lane, float(LANES)), axis=1, keepdims=True)
    e2 = jnp.exp(v2 - v1)
    g1 = 1.0 / (1.0 + e2)
    g2 = e2 / (1.0 + e2)
    out = jnp.where(lane == ROUTE_IDS, i1, 0.0) + jnp.where(lane == ROUTE_IDS + 1, i2, 0.0)
    out = out + jnp.where(lane == ROUTE_GATES, g1, 0.0) + jnp.where(lane == ROUTE_GATES + 1, g2, 0.0)
    o_ref[...] = out


def route_top2(x, router_w, tm=512):
    m, d = x.shape
    n_experts = router_w.shape[1]
    w = jnp.zeros((d, LANES), f32).at[:, :n_experts].set(router_w)
    return pl.pallas_call(
        functools.partial(_router_kernel, n_experts=n_experts),
        grid=(m // tm,),
        in_specs=[pl.BlockSpec((tm, d), lambda i: (i, 0)), pl.BlockSpec((d, LANES), lambda i: (0, 0))],
        out_specs=pl.BlockSpec((tm, LANES), lambda i: (i, 0)),
        out_shape=jax.ShapeDtypeStruct((m, LANES), f32),
        compiler_params=_params("parallel"),
        name="route_top2",
    )(x, w)


def _moe_ln_kernel(tok_ref, gs_ref, off_ref, x_ref, w1_ref, w3_ref, w2_ref, g_ref, b_ref, o_ref,
                   xg_ref, xb_ref, y_ref, *, rt):
    c = pl.program_id(0)
    e = pl.program_id(1)
    f = pl.program_id(2)
    last_f = pl.num_programs(2) - 1
    start = off_ref[0, e]
    count = off_ref[0, e + 1] - start
    n_tiles = (count + rt - 1) // rt

    @pl.when((c == 0) & (e == 0) & (f == 0))
    def _():
        xg_ref[...] = jnp.zeros_like(xg_ref)

    @pl.when((e == 0) & (f == 0))
    def _():
        o_ref[...] = jnp.zeros_like(o_ref)

    def row_loop(body):
        def group(j, carry):
            for u in range(ROW_UNROLL):
                body(j * ROW_UNROLL + u)
            return carry

        def single(r, carry):
            body(r)
            return carry
        lax.fori_loop(0, count // ROW_UNROLL, group, 0)
        lax.fori_loop((count // ROW_UNROLL) * ROW_UNROLL, count, single, 0)

    @pl.when(f == 0)
    def _():
        def gather(r):
            t = tok_ref[0, start + r]
            xg_ref[pl.ds(r, 1), :] = x_ref[pl.ds(t, 1), :]
        row_loop(gather)

        def cast(j, carry):
            rows = pl.ds(pl.multiple_of(j * rt, rt), rt)
            xb_ref[rows, :] = xg_ref[rows, :].astype(MXU_DTYPE)
            return carry
        lax.fori_loop(0, n_tiles, cast, 0)

    def tile(j, carry):
        rows = pl.ds(pl.multiple_of(j * rt, rt), rt)
        y = _swiglu_tile(xb_ref[rows, :], w1_ref, w3_ref, w2_ref)

        @pl.when(f == 0)
        def _():
            y_ref[rows, :] = y

        @pl.when(f != 0)
        def _():
            y_ref[rows, :] += y
        return carry
    lax.fori_loop(0, n_tiles, tile, 0)

    @pl.when(f == last_f)
    def _():
        def updated(r):
            t = tok_ref[0, start + r]
            return t, o_ref[pl.ds(t, 1), :] + gs_ref[0, start + r] * y_ref[pl.ds(r, 1), :]

        def group(j, carry):
            rows = [updated(j * ROW_UNROLL + u) for u in range(ROW_UNROLL)]
            for t, row in rows:
                o_ref[pl.ds(t, 1), :] = row
            return carry

        def single(r, carry):
            t, row = updated(r)
            o_ref[pl.ds(t, 1), :] = row
            return carry
        lax.fori_loop(0, count // ROW_UNROLL, group, 0)
        lax.fori_loop((count // ROW_UNROLL) * ROW_UNROLL, count, single, 0)

    @pl.when((e == pl.num_programs(1) - 1) & (f == last_f))
    def _():
        o_ref[...] = _layer_norm_rows(DEEPNORM_ALPHA * x_ref[...] + o_ref[...], g_ref[...], b_ref[...])


def moe_residual_ln(x, routes, w1, w3, w2, g, b, tm, tf, rt=128):
    m, d = x.shape
    n_experts, _, ff = w1.shape
    n_chunks = m // tm
    ids = routes[:, ROUTE_IDS:ROUTE_IDS + TOP_K].astype(i32).reshape(n_chunks, tm * TOP_K)
    gts = routes[:, ROUTE_GATES:ROUTE_GATES + TOP_K].reshape(n_chunks, tm * TOP_K)
    order = jnp.argsort(ids, axis=1, stable=True).astype(i32)
    tok = order // TOP_K
    gs = jnp.take_along_axis(gts, order, axis=1)
    counts = jnp.sum(ids[:, :, None] == jnp.arange(n_experts, dtype=i32)[None, None, :], axis=1, dtype=i32)
    offs = jnp.concatenate([jnp.zeros((n_chunks, 1), i32), jnp.cumsum(counts, axis=1, dtype=i32)], axis=1)
    smem = lambda width: pl.BlockSpec((None, 1, width), lambda c, e, f: (c, 0, 0), memory_space=pltpu.SMEM)
    return pl.pallas_call(
        functools.partial(_moe_ln_kernel, rt=rt),
        grid=(n_chunks, n_experts, ff // tf),
        in_specs=[smem(tm * TOP_K), smem(tm * TOP_K), smem(n_experts + 1),
                  pl.BlockSpec((tm, d), lambda c, e, f: (c, 0), pipeline_mode=pl.Buffered(1)),
                  pl.BlockSpec((None, d, tf), lambda c, e, f: (e, 0, f)),
                  pl.BlockSpec((None, d, tf), lambda c, e, f: (e, 0, f)),
                  pl.BlockSpec((None, tf, d), lambda c, e, f: (e, f, 0)),
                  pl.BlockSpec((1, d), lambda c, e, f: (0, 0)), pl.BlockSpec((1, d), lambda c, e, f: (0, 0))],
        out_specs=pl.BlockSpec((tm, d), lambda c, e, f: (c, 0), pipeline_mode=pl.Buffered(1)),
        out_shape=jax.ShapeDtypeStruct((m, d), f32),
        scratch_shapes=[pltpu.VMEM((tm, d), f32), pltpu.VMEM((tm, d), MXU_DTYPE), pltpu.VMEM((tm, d), f32)],
        compiler_params=_params("arbitrary", "arbitrary", "arbitrary"),
        name="moe_residual_ln",
    )(tok[:, None, :], gs[:, None, :], offs[:, None, :], x, w1, w3, w2, g.reshape(1, d), b.reshape(1, d))


def _flash_reset(m_ref, acc_ref):
    m_ref[...] = jnp.full(m_ref.shape, NEG, f32)
    acc_ref[...] = jnp.zeros(acc_ref.shape, f32)


def _flash_rows(q, kt, bias, m_ref, alpha_ref, p_ref, rows):
    s = _mm(q, kt)
    if bias is not None:
        s = s + bias
    m_prev = m_ref[rows, :]
    m_next = jnp.maximum(m_prev, jnp.max(s, axis=1, keepdims=True))
    alpha_ref[rows, :] = jnp.exp2(m_prev - m_next)
    m_ref[rows, :] = m_next
    p_ref[rows, :s.shape[1]] = jnp.exp2(s - _lane_tile(m_next, s.shape[1] // LANES)).astype(p_ref.dtype)


def _flash_accumulate(v_aug, alpha_ref, p_ref, acc_ref):
    alpha = _lane_tile(alpha_ref[...], acc_ref.shape[-1] // LANES)
    acc_ref[...] = alpha * acc_ref[...] + _mm(p_ref[:, :v_aug.shape[0]], v_aug)


def _skewed_chunks(n, score, fold, carry):
    def pair(j, carry):
        c = 2 * j
        fold(c - 1, 1)
        carry = score(c, 0, carry)
        fold(c, 0)
        return score(c + 1, 1, carry)

    def single(c, carry):
        fold(c - 1, 1)
        return score(c, 0, carry)

    carry = lax.fori_loop(0, n // 2, pair, carry)
    return lax.fori_loop(2 * (n // 2), n, single, carry)


def _fold_last(n, fold):
    for buf in range(2):
        pl.when((n - 1) % 2 == buf)(functools.partial(fold, n - 1, buf))


def _lane_tile(x, reps):
    return x if reps == 1 else jnp.concatenate([x] * reps, axis=1)


def _row_slices(n_rows):
    return [slice(r * Q_BLOCK, (r + 1) * Q_BLOCK) for r in range(n_rows // Q_BLOCK)]


def _dsa_kernel(qi_ref, wi_ref, kit_ref, qa_ref, kat_ref, va_ref, o_ref,
                skey_ref, wrep_ref, m_ref, alpha_ref, acc_ref, p_ref, qh_ref, shi_ref, *, topk, kc):
    qb = Q_BLOCK
    i = pl.program_id(1)
    q0 = i * qb
    n_chunks = (q0 + qb + kc - 1) // kc
    t_row = q0 + lax.broadcasted_iota(i32, (qb, kc), 0)
    col = lax.broadcasted_iota(i32, (qb, kc), 1)
    wi = wi_ref[...]
    qi = qi_ref[...]
    qis = [qi[:, h * IDX_DIM:(h + 1) * IDX_DIM] for h in range(IDX_HEADS)]
    qa = qa_ref[...]
    for h in range(A_HEADS):
        qh_ref[h * qb:(h + 1) * qb, :] = qa[:, h * HEAD_DIM:(h + 1) * HEAD_DIM]
    for h in range(IDX_HEADS):
        wrep_ref[h] = jnp.broadcast_to(wi[:, h:h + 1], (qb, kc))

    def score_body(c, carry):
        off = pl.multiple_of(c * kc, kc)
        kt = kit_ref[:, pl.ds(off, kc)]
        s = jnp.zeros((qb, kc), f32)
        for h in range(IDX_HEADS):
            s = s + wrep_ref[h] * jnp.maximum(_mm(qis[h], kt), 0.0)
        s = jnp.where(col + off <= t_row, s, NEG)
        bits = pltpu.bitcast(s, i32)
        bits = jnp.where((bits & EXPONENT_BITS) == 0, 0, bits)
        skey_ref[:, pl.ds(off, kc)] = jnp.where(bits < 0, bits ^ 0x7FFFFFFF, bits)
        shi_ref[:, pl.ds(off, kc)] = pltpu.bitcast(bits & HI16, f32).astype(jnp.bfloat16)
        return carry

    n_strips = (q0 + qb + COUNT_STRIP - 1) // COUNT_STRIP
    lax.fori_loop(0, n_strips * (COUNT_STRIP // kc), score_body, 0)

    def count_ge(cand):
        def body(c, acc):
            off = pl.multiple_of(c * COUNT_STRIP, COUNT_STRIP)
            for j in range(COUNT_STRIP // LANES):
                acc = acc + jnp.where(skey_ref[:, pl.ds(off + j * LANES, LANES)] >= cand, 1.0, 0.0)
            return acc
        acc = lax.fori_loop(0, n_strips, body, jnp.zeros((qb, LANES), f32))
        return jnp.sum(acc, axis=1, keepdims=True)

    def count_ge_hi(cand):
        cbits = jnp.where(cand < 0, cand ^ 0x7FFFFFFF, cand) & HI16
        cbits = jnp.where((cbits > 0) & (cbits < EXPONENT_LSB), EXPONENT_LSB, cbits)
        level = jnp.broadcast_to(pltpu.bitcast(cbits, f32).astype(jnp.bfloat16), (qb, LANES))
        one = jnp.ones((qb, LANES), jnp.bfloat16)

        def body(c, acc):
            off = pl.multiple_of(c * COUNT_STRIP, COUNT_STRIP)
            for j in range(COUNT_STRIP // LANES):
                acc = acc + jnp.where(shi_ref[:, pl.ds(off + j * LANES, LANES)] >= level, one, 0.0 * one)
            return acc
        acc = lax.fori_loop(0, n_strips, body, jnp.zeros((qb, LANES), jnp.bfloat16))
        return jnp.sum(acc.astype(f32), axis=1, keepdims=True)

    thr = jnp.where(count_ge_hi(jnp.zeros((qb, 1), i32)) >= topk, 0, INT_MIN).astype(i32)

    def bisect_hi(b, thr):
        cand = thr + jnp.left_shift(jnp.int32(1), 30 - b)
        return jnp.where(count_ge_hi(cand) >= topk, cand, thr)

    def bisect(b, thr):
        cand = thr + jnp.left_shift(jnp.int32(1), 30 - b)
        return jnp.where(count_ge(cand) >= topk, cand, thr)

    thr = lax.fori_loop(0, 15, bisect_hi, thr)
    thr = lax.fori_loop(15, 31, bisect, thr)
    quota = topk - count_ge(thr + 1)

    before = (lax.broadcasted_iota(i32, (kc, kc), 0) < lax.broadcasted_iota(i32, (kc, kc), 1))
    before = jnp.where(before, 1.0, 0.0).astype(MXU_DTYPE)
    _flash_reset(m_ref, acc_ref)
    alpha_ref[...] = jnp.zeros(alpha_ref.shape, f32)
    p_ref[...] = jnp.zeros(p_ref.shape, p_ref.dtype)
    heads = _row_slices(A_HEADS * qb)

    def fold(c, buf):
        off = pl.multiple_of(jnp.maximum(c, 0) * kc, kc)
        _flash_accumulate(va_ref[pl.ds(off, kc), :], alpha_ref.at[buf], p_ref.at[buf], acc_ref)

    def score(c, buf, ties_seen):
        off = pl.multiple_of(c * kc, kc)
        key = skey_ref[:, pl.ds(off, kc)]
        eq = key == thr
        eqf = jnp.where(eq, 1.0, 0.0)
        rank = ties_seen + _mm(eqf.astype(MXU_DTYPE), before)
        sel = ((key > thr) | (eq & (rank < quota))) & (col + off <= t_row)
        bias = jnp.where(sel, 0.0, NEG)
        kt = kat_ref[:, pl.ds(off, kc)]
        for rows in heads:
            _flash_rows(qh_ref[rows, :], kt, bias, m_ref, alpha_ref.at[buf], p_ref.at[buf], rows)
        return ties_seen + jnp.sum(eqf, axis=1, keepdims=True)

    _skewed_chunks(n_chunks, score, fold, jnp.zeros((qb, 1), f32))
    _fold_last(n_chunks, fold)
    acc = acc_ref[...]
    o = acc[:, :HEAD_DIM] / acc[:, HEAD_DIM:]
    for h in range(A_HEADS):
        o_ref[:, h * HEAD_DIM:(h + 1) * HEAD_DIM] = o[h * qb:(h + 1) * qb, :]


QPK_QA, QPK_QIDX, QPK_QB = 0, 2, 3
KPK_KA, KPK_KIDX, KPK_KSLC, KPK_KWIN = 0, 1, 2, 4
VPK_VA, VPK_VSLC, VPK_VWIN = 0, 1, 3
OUT_NSA = 2


def dsa_attention(wi, qpk, kpk, vpk, topk, kc=512):
    bsz, seq, _ = qpk.shape
    nb = seq // Q_BLOCK
    rows = A_HEADS * Q_BLOCK
    width = A_HEADS * HEAD_DIM
    kern = functools.partial(_dsa_kernel, topk=topk, kc=kc)
    return pl.pallas_call(
        kern,
        grid=(bsz, nb),
        in_specs=[pl.BlockSpec((None, Q_BLOCK, IDX_HEADS * IDX_DIM), lambda b, i: (b, i, QPK_QIDX)),
                  pl.BlockSpec((None, Q_BLOCK, IDX_HEADS), lambda b, i: (b, i, 0)),
                  pl.BlockSpec((None, IDX_DIM, seq), lambda b, i: (b, KPK_KIDX, 0)),
                  pl.BlockSpec((None, Q_BLOCK, width), lambda b, i: (b, i, QPK_QA)),
                  pl.BlockSpec((None, HEAD_DIM, seq), lambda b, i: (b, KPK_KA, 0)),
                  pl.BlockSpec((None, seq, 2 * HEAD_DIM), lambda b, i: (b, 0, VPK_VA))],
        out_specs=pl.BlockSpec((None, Q_BLOCK, width), lambda b, i: (b, i, 0)),
        out_shape=jax.ShapeDtypeStruct((bsz, seq, 2 * width), f32),
        scratch_shapes=[pltpu.VMEM((Q_BLOCK, seq), i32), pltpu.VMEM((IDX_HEADS, Q_BLOCK, kc), f32),
                        pltpu.VMEM((rows, LANES), f32), pltpu.VMEM((2, rows, LANES), f32),
                        pltpu.VMEM((rows, 2 * HEAD_DIM), f32), pltpu.VMEM((2, rows, kc), MXU_DTYPE),
                        pltpu.VMEM((rows, HEAD_DIM), MXU_DTYPE), pltpu.VMEM((Q_BLOCK, seq), jnp.bfloat16)],
        compiler_params=_params("parallel", "arbitrary"),
        name="dsa_attention",
    )(qpk, wi, kpk, qpk, kpk, vpk)


def _compress_kernel(x_ref, pe_ref, w1_ref, w2_ref, o_ref):
    half = (CMP_LEN // 2) * HEAD_DIM
    x = x_ref[...]
    first = _mm((x + pe_ref[0:1, :]).astype(MXU_DTYPE), w1_ref[0:half, :])
    second = _mm((x + pe_ref[1:2, :]).astype(MXU_DTYPE), w1_ref[half:2 * half, :])
    pre = first + pltpu.roll(second, shift=x.shape[0] - 1, axis=0)
    hid = pre * jax.nn.sigmoid(pre)
    o_ref[...] = _mm(hid.astype(MXU_DTYPE), w2_ref[...])


def nsa_compress(chunks, pe2, w1, w2):
    bsz, groups, nck, width = chunks.shape
    return pl.pallas_call(
        _compress_kernel,
        grid=(bsz, groups),
        in_specs=[pl.BlockSpec((None, None, nck, width), lambda b, g: (b, g, 0, 0)),
                  pl.BlockSpec((2, width), lambda b, g: (0, 0)),
                  pl.BlockSpec((2 * width, CMP_HIDDEN), lambda b, g: (0, 0)),
                  pl.BlockSpec((CMP_HIDDEN, HEAD_DIM), lambda b, g: (0, 0))],
        out_specs=pl.BlockSpec((None, None, nck, HEAD_DIM), lambda b, g: (b, g, 0, 0)),
        out_shape=jax.ShapeDtypeStruct((bsz, groups, nck, HEAD_DIM), f32),
        compiler_params=_params("parallel", "parallel"),
        name="nsa_compress",
    )(chunks, pe2, w1, w2)


def _nsa_kernel(qin_ref, graw_ref, kct_ref, vc_ref, kst_ref, vs_ref, kwt_ref, vw_ref, ovl_ref, _, o_ref,
                m_ref, alpha_ref, acc_ref, p_ref, pw_ref, q_ref, imp_ref, *, n_slc, n_sel, kc):
    qb = Q_BLOCK
    i = pl.program_id(2)
    q0 = i * qb
    n_cmp = kct_ref.shape[1]
    heads = _row_slices(B_REP * qb)
    qin = qin_ref[...]
    for r in range(B_REP):
        q_ref[r * qb:(r + 1) * qb, :] = qin[:, r * HEAD_DIM:(r + 1) * HEAD_DIM]

    def compressed(width):
        t_c = q0 + lax.broadcasted_iota(i32, (qb, width), 0)
        cmp_end = lax.broadcasted_iota(i32, (qb, width), 1) * CMP_STRIDE + (CMP_LEN - 1)
        vis = cmp_end <= t_c
        kct = kct_ref[:, :width]
        p_sum = jnp.zeros((qb, width), f32)
        for rows in heads:
            lc = jnp.where(vis, _mm(q_ref[rows, :], kct), NEG)
            ec = jnp.where(vis, jnp.exp2(lc - jnp.max(lc, axis=1, keepdims=True)), 0.0)
            den = jnp.sum(ec, axis=1, keepdims=True)
            p_c = ec / jnp.where(den > 0.0, den, 1.0)
            p_sum = p_sum + p_c
            acc_ref[rows, :HEAD_DIM] = _mm(p_c.astype(MXU_DTYPE), vc_ref[:width, :])
        imp_ref[...] = jnp.dot(p_sum, ovl_ref[:width, :], preferred_element_type=f32,
                               precision=lax.Precision.HIGHEST)

    if n_cmp % (2 * LANES) == 0:
        first_half_only = (q0 + qb) * 2 <= n_cmp * CMP_STRIDE
        pl.when(first_half_only)(functools.partial(compressed, n_cmp // 2))
        pl.when(jnp.logical_not(first_half_only))(functools.partial(compressed, n_cmp))
    else:
        compressed(n_cmp)
    o_c = acc_ref[:, :HEAD_DIM]

    imp = imp_ref[...].T
    t_q = q0 + lax.broadcasted_iota(i32, (LANES, qb), 1)
    blk = lax.broadcasted_iota(i32, (LANES, qb), 0)
    blk_t = t_q // SLC_LEN
    forced = (blk == 0) | (blk == blk_t) | (blk == blk_t - 1)
    imp = jnp.where(forced, FORCED_BOOST, imp)
    imp = jnp.where(blk * SLC_LEN <= t_q, imp, NEG)
    imp = jnp.where(blk < n_slc, imp, -jnp.inf)
    blk_f = blk.astype(f32)

    def pick(_, carry):
        imp, selm = carry
        best = jnp.max(imp, axis=0, keepdims=True)
        first = jnp.min(jnp.where(imp == best, blk_f, float(LANES)), axis=0, keepdims=True)
        hit = blk_f == first
        return jnp.where(hit, -jnp.inf, imp), jnp.where(hit, 1.0, selm)

    _, selm = lax.fori_loop(0, n_sel, pick, (imp, jnp.zeros((LANES, qb), f32)))
    selm = selm.T.astype(MXU_DTYPE)

    t_k = q0 + lax.broadcasted_iota(i32, (qb, kc), 0)
    col = lax.broadcasted_iota(i32, (qb, kc), 1)
    exp_row = lax.broadcasted_iota(i32, (LANES, kc), 0)
    exp_col = lax.broadcasted_iota(i32, (LANES, kc), 1)
    _flash_reset(m_ref, acc_ref)
    alpha_ref[...] = jnp.zeros(alpha_ref.shape, f32)
    p_ref[...] = jnp.zeros(p_ref.shape, p_ref.dtype)

    def fold(c, buf):
        off = pl.multiple_of(jnp.maximum(c, 0) * kc, kc)
        _flash_accumulate(vs_ref[pl.ds(off, kc), :], alpha_ref.at[buf], p_ref.at[buf], acc_ref)

    def score(c, buf, carry):
        off = pl.multiple_of(c * kc, kc)
        expand = jnp.where(exp_row == (exp_col + off) // SLC_LEN, 1.0, 0.0).astype(MXU_DTYPE)
        sel = (_mm(selm, expand) > 0.5) & (col + off <= t_k)
        bias = jnp.where(sel, 0.0, NEG)
        kt = kst_ref[:, pl.ds(off, kc)]
        for rows in heads:
            _flash_rows(q_ref[rows, :], kt, bias, m_ref, alpha_ref.at[buf], p_ref.at[buf], rows)
        return carry

    n_slc_chunks = (q0 + qb + kc - 1) // kc
    _skewed_chunks(n_slc_chunks, score, fold, 0)
    _fold_last(n_slc_chunks, fold)
    acc = acc_ref[...]
    o_s = acc[:, :HEAD_DIM] / acc[:, HEAD_DIM:]

    slab = WINDOW + qb
    w0 = pl.multiple_of(jnp.maximum(q0 - WINDOW, 0), qb)
    dist = (q0 + lax.broadcasted_iota(i32, (qb, slab), 0)) - (w0 + lax.broadcasted_iota(i32, (qb, slab), 1))
    bias = jnp.where((dist >= 0) & (dist < WINDOW), 0.0, NEG)
    kt = kwt_ref[:, pl.ds(w0, slab)]
    for rows in heads:
        s = _mm(q_ref[rows, :], kt) + bias
        pw_ref[rows, :] = jnp.exp2(s - jnp.max(s, axis=1, keepdims=True)).astype(pw_ref.dtype)
    acc = _mm(pw_ref[...], vw_ref[pl.ds(w0, slab), :])
    o_w = acc[:, :HEAD_DIM] / acc[:, HEAD_DIM:]

    gates = jax.nn.sigmoid(graw_ref[...])
    for r, rows in enumerate(heads):
        g_c, g_s, g_w = (gates[:, 3 * r + n:3 * r + n + 1] for n in range(3))
        o_ref[:, r * HEAD_DIM:(r + 1) * HEAD_DIM] = g_c * o_c[rows] + g_s * o_s[rows] + g_w * o_w[rows]


def nsa_attention(qpk, graw, kct, vc, kpk, vpk, ovl, out_ab, n_slc, n_sel, kc=512):
    bsz, seq, _ = qpk.shape
    groups = B_KV_GROUPS
    nb = seq // Q_BLOCK
    rows = B_REP * Q_BLOCK
    width = B_REP * HEAD_DIM
    n_cmp = kct.shape[-1]
    kern = functools.partial(_nsa_kernel, n_slc=n_slc, n_sel=n_sel, kc=kc)
    per_bg = lambda b, g, i: (b, g, 0, 0)
    return pl.pallas_call(
        kern,
        grid=(bsz, groups, nb),
        in_specs=[pl.BlockSpec((None, Q_BLOCK, width), lambda b, g, i: (b, i, QPK_QB + g)),
                  pl.BlockSpec((None, None, Q_BLOCK, B_REP * 3), lambda b, g, i: (b, g, i, 0)),
                  pl.BlockSpec((None, None, HEAD_DIM, n_cmp), per_bg),
                  pl.BlockSpec((None, None, n_cmp, HEAD_DIM), per_bg),
                  pl.BlockSpec((None, HEAD_DIM, seq), lambda b, g, i: (b, KPK_KSLC + g, 0)),
                  pl.BlockSpec((None, seq, 2 * HEAD_DIM), lambda b, g, i: (b, 0, VPK_VSLC + g)),
                  pl.BlockSpec((None, HEAD_DIM, seq), lambda b, g, i: (b, KPK_KWIN + g, 0)),
                  pl.BlockSpec((None, seq, 2 * HEAD_DIM), lambda b, g, i: (b, 0, VPK_VWIN + g)),
                  pl.BlockSpec((n_cmp, LANES), lambda b, g, i: (0, 0)),
                  pl.BlockSpec(memory_space=pl.ANY)],
        out_specs=pl.BlockSpec((None, Q_BLOCK, width), lambda b, g, i: (b, i, OUT_NSA + g)),
        out_shape=jax.ShapeDtypeStruct(out_ab.shape, f32),
        input_output_aliases={9: 0},
        scratch_shapes=[pltpu.VMEM((rows, LANES), f32), pltpu.VMEM((2, rows, LANES), f32),
                        pltpu.VMEM((rows, 2 * HEAD_DIM), f32), pltpu.VMEM((2, rows, kc), MXU_DTYPE),
                        pltpu.VMEM((rows, WINDOW + Q_BLOCK), MXU_DTYPE), pltpu.VMEM((rows, HEAD_DIM), MXU_DTYPE),
                        pltpu.VMEM((Q_BLOCK, LANES), f32)],
        compiler_params=_params("parallel", "parallel", "arbitrary"),
        name="nsa_attention",
    )(qpk, graw, kct, vc, kpk, vpk, kpk, vpk, ovl, out_ab)


def _diff_kernel(lam_ref, qin_ref, kt_ref, v_ref, g_ref, o_ref, m_ref, alpha_ref, acc_ref, p_ref, q_ref,
                 *, tq, kc, out_scale):
    i = pl.program_id(2)
    q0 = i * tq
    dv = v_ref.shape[-1] // 2
    groups = _row_slices(tq)
    qin = qin_ref[...]
    for half in range(2):
        q_ref[half] = qin[:, half * HEAD_DIM:(half + 1) * HEAD_DIM]
    for half in range(2):
        _flash_reset(m_ref.at[half], acc_ref.at[half])
    alpha_ref[...] = jnp.zeros(alpha_ref.shape, f32)
    p_ref[...] = jnp.zeros(p_ref.shape, p_ref.dtype)

    def fold(c, buf, sub=(0, kc // tq)):
        off = pl.multiple_of(jnp.maximum(c, 0) * kc + sub[0] * tq, tq)
        v = v_ref[pl.ds(off, sub[1] * tq), :]
        for half in range(2):
            _flash_accumulate(v, alpha_ref.at[buf, half], p_ref.at[buf, half], acc_ref.at[half])

    def score(c, buf, masked, sub=(0, kc // tq)):
        off = pl.multiple_of(c * kc + sub[0] * tq, tq)
        width = sub[1] * tq
        for half in range(2):
            kt = kt_ref[half * HEAD_DIM:(half + 1) * HEAD_DIM, pl.ds(off, width)]
            for r, rows in enumerate(groups):
                bias = None
                if masked:
                    key = off + lax.broadcasted_iota(i32, (Q_BLOCK, width), 1)
                    t = q0 + r * Q_BLOCK + lax.broadcasted_iota(i32, (Q_BLOCK, width), 0)
                    bias = jnp.where(key <= t, 0.0, NEG)
                _flash_rows(q_ref[half, rows, :], kt, bias, m_ref.at[half], alpha_ref.at[buf, half],
                            p_ref.at[buf, half], rows)

    def score_full(c, buf, carry):
        score(c, buf, False)
        return carry

    n_full = q0 // kc
    _skewed_chunks(n_full, score_full, fold, 0)

    own = (q0 - n_full * kc) // tq

    def finish(buf, own_piece):
        fold(n_full - 1, 1 - buf)
        if own_piece:
            score(n_full, buf, False, (0, own_piece))
            fold(n_full, buf, (0, own_piece))
            buf = 1 - buf
        score(n_full, buf, True, (own_piece, 1))
        fold(n_full, buf, (own_piece, 1))

    for buf in range(2):
        for own_piece in range(kc // tq):
            pl.when((n_full % 2 == buf) & (own == own_piece))(functools.partial(finish, buf, own_piece))
    a1 = acc_ref[0]
    a2 = acc_ref[1]
    o = a1[:, :dv] / a1[:, dv:] - lam_ref[0] * (a2[:, :dv] / a2[:, dv:])
    o = o * lax.rsqrt(jnp.mean(o * o, axis=-1, keepdims=True) + LN_EPS) * g_ref[...]
    o_ref[...] = o * out_scale


def diff_attention(lam, q, kt, v, subln_g, out_scale, tq=512, kc=1024):
    bsz, seq, width = q.shape
    dv = 2 * HEAD_DIM
    heads = width // dv
    tq = min(tq, seq)
    kc = min(kc, seq)
    assert kc % tq == 0 and seq % kc == 0
    kern = functools.partial(_diff_kernel, tq=tq, kc=kc, out_scale=out_scale)
    return pl.pallas_call(
        kern,
        grid=(bsz, heads, seq // tq),
        in_specs=[pl.BlockSpec(memory_space=pltpu.SMEM),
                  pl.BlockSpec((None, tq, dv), lambda b, h, i: (b, i, h)),
                  pl.BlockSpec((None, dv, seq), lambda b, h, i: (b, h, 0)),
                  pl.BlockSpec((None, seq, 2 * dv), lambda b, h, i: (b, 0, h)),
                  pl.BlockSpec((1, dv), lambda b, h, i: (0, 0))],
        out_specs=pl.BlockSpec((None, tq, dv), lambda b, h, i: (b, i, h)),
        out_shape=jax.ShapeDtypeStruct((bsz, seq, heads * dv), f32),
        scratch_shapes=[pltpu.VMEM((2, tq, LANES), f32), pltpu.VMEM((2, 2, tq, LANES), f32),
                        pltpu.VMEM((2, tq, 2 * dv), f32), pltpu.VMEM((2, 2, tq, kc), MXU_DTYPE),
                        pltpu.VMEM((2, tq, HEAD_DIM), MXU_DTYPE)],
        compiler_params=_params("parallel", "parallel", "arbitrary"),
        name="diff_attention",
    )(lam, q, kt, v, subln_g.reshape(1, dv))


def _rope_tables(positions):
    inv_freq = ROPE_THETA ** (-jnp.arange(0, ROT_DIM, 2, dtype=f32) / ROT_DIM)
    ang = positions.astype(f32)[..., None] * inv_freq
    return jnp.cos(ang), jnp.sin(ang)


def _apply_rope(x, cos, sin):
    shape = cos.shape[:2] + (1,) * (x.ndim - 3) + cos.shape[-1:]
    c = cos.reshape(shape)
    s = sin.reshape(shape)
    half = ROT_DIM // 2
    x1, x2 = x[..., :half], x[..., half:ROT_DIM]
    return jnp.concatenate([x1 * c - x2 * s, x2 * c + x1 * s, x[..., ROT_DIM:]], axis=-1)


def _rope_coefficients(cos, sin):
    lead = cos.shape[:-1]
    rest = jnp.zeros(lead + (HEAD_DIM - ROT_DIM,), f32)
    none = jnp.zeros_like(sin)
    twice = lambda a: jnp.concatenate([a, a], axis=-1)
    return (twice(jnp.concatenate([cos, cos, rest + 1.0], axis=-1)),
            twice(jnp.concatenate([-sin, none, rest], axis=-1)),
            twice(jnp.concatenate([none, sin, rest], axis=-1)))


def _ab_weight_columns(w_in):
    widths = dict(AB_LAYOUT)
    starts = dict(zip(widths, np.cumsum([0] + [w for _, w in AB_LAYOUT[:-1]]).tolist()))
    cols = lambda names: np.concatenate([np.arange(starts[n], starts[n] + widths[n]) for n in names])
    first = ("q_a", "q_idx", "q_b", "k_a", "k_idx", "k_slc", "k_win", "v_a", "v_slc", "v_win", "gate_b", "w_idx")
    n_first = sum(widths[n] for n in first)
    pad = AB_CMP_TILE * LANES - n_first
    assert 0 <= pad < HEAD_DIM
    w = jnp.concatenate([w_in[:, cols(first)], jnp.zeros((w_in.shape[0], pad), w_in.dtype),
                         w_in[:, cols(("k_cmp", "v_cmp"))]], axis=1)
    assert w.shape[1] == AB_TILES * LANES
    return w.astype(MXU_DTYPE)


def _overlap_matrix(n_cmp_rows, n_slc):
    c_start = np.arange(n_cmp_rows) * CMP_STRIDE
    s_start = np.arange(LANES) * SLC_LEN
    ovl = (c_start[:, None] < s_start[None, :] + SLC_LEN) & (c_start[:, None] + CMP_LEN > s_start[None, :])
    ovl = ovl & (np.arange(LANES)[None, :] < n_slc)
    return jnp.asarray(ovl.astype(np.float32))


def _ab_mixer(x2, bsz, seq, positions, coef, w_in, pe_k, pe_v, ck1, ck2, cv1, cv2):
    qpk, kpk, vpk, misc, cmp = project_pack_ab(x2.reshape(bsz, seq, -1), _ab_weight_columns(w_in), coef)
    gate_w = B_HEADS * 3
    g_b = misc[..., HEAD_DIM:HEAD_DIM + gate_w]
    w_idx = misc[..., HEAD_DIM + gate_w:HEAD_DIM + gate_w + IDX_HEADS] * (IDX_HEADS * IDX_DIM) ** -0.5
    k_cmp, v_cmp = cmp[..., :LANES], cmp[..., LANES:]

    out_ab = dsa_attention(w_idx, qpk, kpk, vpk, topk=min(DSA_TOPK, seq // 4))

    groups = B_KV_GROUPS
    n_rows = seq // CMP_STRIDE
    n_cmp = (seq - CMP_LEN) // CMP_STRIDE + 1
    assert n_cmp == n_rows - 1
    n_slc = seq // SLC_LEN
    assert n_slc <= LANES

    def chunked(kv):
        kv = kv.reshape(bsz, n_rows, CMP_STRIDE, groups, HEAD_DIM).transpose(0, 3, 1, 2, 4)
        return kv.reshape(bsz, groups, n_rows, CMP_STRIDE * HEAD_DIM)

    def pe_halves(pe):
        return pe.reshape(2, CMP_STRIDE * HEAD_DIM)

    k_c = nsa_compress(chunked(k_cmp), pe_halves(pe_k), ck1.astype(MXU_DTYPE), ck2.astype(MXU_DTYPE))
    v_c = nsa_compress(chunked(v_cmp), pe_halves(pe_v), cv1.astype(MXU_DTYPE), cv2.astype(MXU_DTYPE))
    cmp_end = jnp.minimum(jnp.arange(n_rows) * CMP_STRIDE + CMP_LEN - 1, seq - 1)
    cos_c, sin_c = _rope_tables(positions[:, cmp_end])
    k_c = _apply_rope(k_c.transpose(0, 2, 1, 3), cos_c, sin_c)
    k_c = k_c.transpose(0, 2, 3, 1).astype(MXU_DTYPE)
    v_c = v_c.astype(MXU_DTYPE)

    graw = g_b.reshape(bsz, seq, groups, B_REP * 3).transpose(0, 2, 1, 3)
    out_ab = nsa_attention(qpk, graw, k_c, v_c, kpk, vpk, _overlap_matrix(n_rows, n_slc), out_ab,
                           n_slc=n_slc, n_sel=min(SLC_TOPN, n_slc))
    return out_ab.reshape(bsz * seq, (A_HEADS + B_HEADS) * HEAD_DIM)


def _diff_mixer(x2, bsz, seq, coef, w_in, lq1, lk1, lq2, lk2, subln_g, lam_init):
    q, kt, v = project_pack_c(x2.reshape(bsz, seq, -1), w_in.astype(MXU_DTYPE), coef)
    lam =(jnp.exp(jnp.sum(lq1 * lk1)) - jnp.exp(jnp.sum(lq2 * lk2)) + lam_init).reshape(1).astype(f32)
    o = diff_attention(lam, q, kt, v, subln_g, 1.0 - lam_init)
    return o.reshape(bsz * seq, C_HEADS * 2 * HEAD_DIM)


def kernel(x, positions, ab_w_in, cmp_pe_k, cmp_pe_v, cmp_k_w1, cmp_k_w2, cmp_v_w1, cmp_v_w2, ab_w_out, ln_ab_g, ln_ab_b, ffn_w1, ffn_w3, ffn_w2, ln_ffn_g, ln_ffn_b, c_w_in, lambda_q1, lambda_k1, lambda_q2, lambda_k2, c_subln_g, c_w_out, ln_c_g, ln_c_b, router_w, moe_w1, moe_w3, moe_w2, ln_moe_g, ln_moe_b):
    bsz, seq, d = x.shape
    assert seq % COUNT_STRIP == 0 and seq >= WINDOW + Q_BLOCK and d == D_MODEL
    coef = _rope_coefficients(*_rope_tables(positions))
    x2 = x.reshape(bsz * seq, d)
    for layer in range(DEPTH):
        i = layer // 2
        if layer % 2 == 0:
            o = _ab_mixer(x2, bsz, seq, positions, coef, ab_w_in[i], cmp_pe_k[i], cmp_pe_v[i],
                          cmp_k_w1[i], cmp_k_w2[i], cmp_v_w1[i], cmp_v_w2[i])
            x2 = project_residual_ln(o, ab_w_out[i].astype(MXU_DTYPE), x2, ln_ab_g[i], ln_ab_b[i])
            x2 = ffn_residual_ln(x2, ffn_w1[i].astype(MXU_DTYPE), ffn_w3[i].astype(MXU_DTYPE),
                                 ffn_w2[i].astype(MXU_DTYPE), ln_ffn_g[i], ln_ffn_b[i], tm=512, tf=1408)
        else:
            lam_init = 0.8 - 0.6 * math.exp(-0.3 * layer)
            o = _diff_mixer(x2, bsz, seq, coef, c_w_in[i], lambda_q1[i], lambda_k1[i], lambda_q2[i],
                            lambda_k2[i], c_subln_g[i], lam_init)
            x2 = project_residual_ln(o, c_w_out[i].astype(MXU_DTYPE), x2, ln_c_g[i], ln_c_b[i])
            routes = route_top2(x2, router_w[i])
            x2 = moe_residual_ln(x2, routes, moe_w1[i].astype(MXU_DTYPE), moe_w3[i].astype(MXU_DTYPE),
                                 moe_w2[i].astype(MXU_DTYPE), ln_moe_g[i], ln_moe_b[i], tm=min(2048, bsz * seq // 2), tf=896, rt=128)
    return x2.reshape(bsz, seq, d)
```

```python
import functools
import math

import numpy as np
import jax
import jax.numpy as jnp
from jax import lax
from jax.experimental import pallas as pl
from jax.experimental.pallas import tpu as pltpu

f32 = jnp.float32
i32 = jnp.int32
MXU_DTYPE = jnp.bfloat16
VMEM_LIMIT_BYTES = 56 * 1024 * 1024
LANES = 128

D_MODEL = 1024
DEPTH = 2
HEAD_DIM = 64
ROT_DIM = HEAD_DIM // 4
ROPE_THETA = 500000.0
Q_BLOCK = 128
NEG = -1e30
LN_EPS = 1e-5
A_HEADS = 8
IDX_HEADS = 4
IDX_DIM = 64
DSA_TOPK = 256
B_HEADS = 8
B_KV_GROUPS = 2
B_REP = B_HEADS // B_KV_GROUPS
CMP_LEN = 32
CMP_STRIDE = 16
CMP_HIDDEN = 128
SLC_LEN = 64
SLC_TOPN = 16
WINDOW = 512
FORCED_BOOST = 1e6
C_HEADS = 8
N_EXPERTS = 8
TOP_K = 2
DEEPNORM_ALPHA = (2 * DEPTH) ** 0.25
QK_SCALE = HEAD_DIM ** -0.5 * math.log2(math.e)
INT_MIN = -(2 ** 31)
COUNT_STRIP = 512
ROW_UNROLL = 4

AB_LAYOUT = (
    ("q_a", A_HEADS * HEAD_DIM), ("k_a", HEAD_DIM), ("v_a", HEAD_DIM),
    ("q_idx", IDX_HEADS * IDX_DIM), ("k_idx", IDX_DIM), ("w_idx", IDX_HEADS),
    ("q_b", B_HEADS * HEAD_DIM),
    ("k_cmp", B_KV_GROUPS * HEAD_DIM), ("v_cmp", B_KV_GROUPS * HEAD_DIM),
    ("k_slc", B_KV_GROUPS * HEAD_DIM), ("v_slc", B_KV_GROUPS * HEAD_DIM),
    ("k_win", B_KV_GROUPS * HEAD_DIM), ("v_win", B_KV_GROUPS * HEAD_DIM),
    ("gate_b", 3 * B_HEADS),
)


def _params(*sem):
    return pltpu.CompilerParams(dimension_semantics=sem, vmem_limit_bytes=VMEM_LIMIT_BYTES)


def _mm(a, b):
    return jnp.dot(a, b, preferred_element_type=f32)


def _layer_norm_rows(y, g, b):
    mu = jnp.mean(y, axis=-1, keepdims=True)
    yc = y - mu
    var = jnp.mean(yc * yc, axis=-1, keepdims=True)
    return yc * lax.rsqrt(var + LN_EPS) * g + b


def _rope_tile(x, keep, hi, lo):
    return x * keep + pltpu.roll(x, LANES - ROT_DIM // 2, 1) * hi + pltpu.roll(x, ROT_DIM // 2, 1) * lo


def _tile(h, j):
    return h[:, j * LANES:(j + 1) * LANES]


AB_Q_TILES = 10
AB_QIDX_TILES = (4, 5)
AB_K_TILES = 3
AB_V_HEADS = 5
AB_MISC_TILE = 15
AB_CMP_TILE = 16
AB_TILES = 18


def _proj_pack_ab_kernel(x_ref, w_ref, coef_ref, q_ref, kt_ref, v_ref, misc_ref, cmp_ref):
    h = _mm(x_ref[...].astype(MXU_DTYPE), w_ref[...])
    keep, hi, lo = (coef_ref[:, j * LANES:(j + 1) * LANES] for j in range(3))
    for j in range(AB_Q_TILES):
        t = _rope_tile(_tile(h, j), keep, hi, lo)
        q_ref[:, j * LANES:(j + 1) * LANES] = (t if j in AB_QIDX_TILES else t * QK_SCALE).astype(q_ref.dtype)
    k = jnp.concatenate([_rope_tile(_tile(h, AB_Q_TILES + j), keep, hi, lo) for j in range(AB_K_TILES)], axis=1)
    kt_ref[...] = k.T.astype(kt_ref.dtype)
    low = lax.broadcasted_iota(i32, keep.shape, 1) < HEAD_DIM
    for j in range(AB_V_HEADS):
        t = _tile(h, AB_Q_TILES + AB_K_TILES + j // 2)
        t = pltpu.roll(t, HEAD_DIM, 1) if j % 2 else t
        v_ref[:, j * LANES:(j + 1) * LANES] = jnp.where(low, t, 1.0).astype(v_ref.dtype)
    misc_ref[...] = _tile(h, AB_MISC_TILE)
    cmp_ref[...] = h[:, AB_CMP_TILE * LANES:AB_TILES * LANES]


def project_pack_ab(x, w, coef, tm=512):
    bsz, seq, d = x.shape
    row = lambda width: pl.BlockSpec((None, tm, width), lambda b, i: (b, i, 0))
    kt_rows = AB_K_TILES * LANES
    return pl.pallas_call(
        _proj_pack_ab_kernel,
        grid=(bsz, seq // tm),
        in_specs=[row(d), pl.BlockSpec((d, AB_TILES * LANES), lambda b, i: (0, 0)), row(3 * LANES)],
        out_specs=[row(AB_Q_TILES * LANES), pl.BlockSpec((None, kt_rows, tm), lambda b, i: (b, 0, i)),
                   row(AB_V_HEADS * LANES), row(LANES), row(2 * LANES)],
        out_shape=[jax.ShapeDtypeStruct((bsz, seq, AB_Q_TILES * LANES), MXU_DTYPE),
                   jax.ShapeDtypeStruct((bsz, kt_rows, seq), MXU_DTYPE),
                   jax.ShapeDtypeStruct((bsz, seq, AB_V_HEADS * LANES), MXU_DTYPE),
                   jax.ShapeDtypeStruct((bsz, seq, LANES), f32),
                   jax.ShapeDtypeStruct((bsz, seq, 2 * LANES), f32)],
        compiler_params=_params("parallel", "parallel"),
        name="project_pack_ab",
    )(x, w, coef)


def _proj_pack_c_kernel(x_ref, w_ref, coef_ref, q_ref, kt_ref, v_ref):
    h = _mm(x_ref[...].astype(MXU_DTYPE), w_ref[...])
    keep, hi, lo = (coef_ref[:, j * LANES:(j + 1) * LANES] for j in range(3))
    n = q_ref.shape[1] // LANES
    for j in range(n):
        q_ref[:, j * LANES:(j + 1) * LANES] = (_rope_tile(_tile(h, j), keep, hi, lo) * QK_SCALE).astype(q_ref.dtype)
    k = jnp.concatenate([_rope_tile(_tile(h, n + j), keep, hi, lo) for j in range(n)], axis=1)
    kt_ref[...] = k.T.astype(kt_ref.dtype)
    ones = jnp.ones(keep.shape, v_ref.dtype)
    for j in range(n):
        v_ref[:, 2 * j * LANES:(2 * j + 1) * LANES] = _tile(h, 2 * n + j).astype(v_ref.dtype)
        v_ref[:, (2 * j + 1) * LANES:(2 * j + 2) * LANES] = ones


def project_pack_c(x, w, coef, tm=512):
    bsz, seq, d = x.shape
    width = w.shape[1] // 3
    row = lambda cols: pl.BlockSpec((None, tm, cols), lambda b, i: (b, i, 0))
    return pl.pallas_call(
        _proj_pack_c_kernel,
        grid=(bsz, seq // tm),
        in_specs=[row(d), pl.BlockSpec((d, 3 * width), lambda b, i: (0, 0)), row(3 * LANES)],
        out_specs=[row(width), pl.BlockSpec((None, width, tm), lambda b, i: (b, 0, i)), row(2 * width)],
        out_shape=[jax.ShapeDtypeStruct((bsz, seq, width), MXU_DTYPE),
                   jax.ShapeDtypeStruct((bsz, width, seq), MXU_DTYPE),
                   jax.ShapeDtypeStruct((bsz, seq, 2 * width), MXU_DTYPE)],
        compiler_params=_params("parallel", "parallel"),
        name="project_pack_c",
    )(x, w, coef)


def _proj_ln_kernel(a_ref, w_ref, res_ref, g_ref, b_ref, o_ref):
    h = _mm(a_ref[...].astype(MXU_DTYPE), w_ref[...])
    o_ref[...] = _layer_norm_rows(DEEPNORM_ALPHA * res_ref[...] + h, g_ref[...], b_ref[...])


def project_residual_ln(a, w, res, g, b, tm=512):
    m, k = a.shape
    n = w.shape[1]
    return pl.pallas_call(
        _proj_ln_kernel,
        grid=(m // tm,),
        in_specs=[pl.BlockSpec((tm, k), lambda i: (i, 0)), pl.BlockSpec((k, n), lambda i: (0, 0)),
                  pl.BlockSpec((tm, n), lambda i: (i, 0)),
                  pl.BlockSpec((1, n), lambda i: (0, 0)), pl.BlockSpec((1, n), lambda i: (0, 0))],
        out_specs=pl.BlockSpec((tm, n), lambda i: (i, 0)),
        out_shape=jax.ShapeDtypeStruct((m, n), f32),
        compiler_params=_params("parallel"),
        name="project_residual_ln",
    )(a, w, res, g.reshape(1, n), b.reshape(1, n))


def _swiglu_tile(xb, w1_ref, w3_ref, w2_ref):
    a = _mm(xb, w1_ref[...])
    h = (a * jax.nn.sigmoid(a)) * _mm(xb, w3_ref[...])
    return _mm(h.astype(MXU_DTYPE), w2_ref[...])


def _ffn_ln_kernel(x_ref, w1_ref, w3_ref, w2_ref, g_ref, b_ref, o_ref, xb_ref, acc_ref):
    f = pl.program_id(1)

    @pl.when(f == 0)
    def _():
        xb_ref[...] = x_ref[...].astype(MXU_DTYPE)
        acc_ref[...] = jnp.zeros_like(acc_ref)

    acc_ref[...] += _swiglu_tile(xb_ref[...], w1_ref, w3_ref, w2_ref)

    @pl.when(f == pl.num_programs(1) - 1)
    def _():
        o_ref[...] = _layer_norm_rows(DEEPNORM_ALPHA * x_ref[...] + acc_ref[...], g_ref[...], b_ref[...])


def ffn_residual_ln(x, w1, w3, w2, g, b, tm, tf):
    m, d = x.shape
    ff = w1.shape[1]
    resident = dict(pipeline_mode=pl.Buffered(1)) if tf == ff else {}
    return pl.pallas_call(
        _ffn_ln_kernel,
        grid=(m // tm, ff // tf),
        in_specs=[pl.BlockSpec((tm, d), lambda i, f: (i, 0)),
                  pl.BlockSpec((d, tf), lambda i, f: (0, f), **resident),
                  pl.BlockSpec((d, tf), lambda i, f: (0, f), **resident),
                  pl.BlockSpec((tf, d), lambda i, f: (f, 0), **resident),
                  pl.BlockSpec((1, d), lambda i, f: (0, 0)), pl.BlockSpec((1, d), lambda i, f: (0, 0))],
        out_specs=pl.BlockSpec((tm, d), lambda i, f: (i, 0)),
        out_shape=jax.ShapeDtypeStruct((m, d), f32),
        scratch_shapes=[pltpu.VMEM((tm, d), MXU_DTYPE), pltpu.VMEM((tm, d), f32)],
        compiler_params=_params("parallel", "arbitrary"),
        name="ffn_residual_ln",
    )(x, w1, w3, w2, g.reshape(1, d), b.reshape(1, d))


ROUTE_IDS = N_EXPERTS
ROUTE_GATES = N_EXPERTS + 2


def _router_kernel(x_ref, w_ref, o_ref, *, n_experts):
    logits = jnp.dot(x_ref[...], w_ref[...], preferred_element_type=f32, precision=lax.Precision.HIGHEST)
    lane = lax.broadcasted_iota(i32, logits.shape, 1).astype(f32)
    logits = jnp.where(lane < n_experts, logits, -jnp.inf)
    v1 = jnp.max(logits, axis=1, keepdims=True)
    i1 = jnp.min(jnp.where(logits == v1, lane, float(LANES)), axis=1, keepdims=True)
    rest = jnp.where(lane == i1, -jnp.inf, logits)
    v2 = jnp.max(rest, axis=1, keepdims=True)
    i2 = jnp.min(jnp.where(rest == v2, lane, float(LANES)), axis=1, keepdims=True)
    e2 = jnp.exp(v2 - v1)
    g1 = 1.0 / (1.0 + e2)
    g2 = e2 / (1.0 + e2)
    out = jnp.where(lane == ROUTE_IDS, i1, 0.0) + jnp.where(lane == ROUTE_IDS + 1, i2, 0.0)
    out = out + jnp.where(lane == ROUTE_GATES, g1, 0.0) + jnp.where(lane == ROUTE_GATES + 1, g2, 0.0)
    o_ref[...] = out


def route_top2(x, router_w, tm=512):
    m, d = x.shape
    n_experts = router_w.shape[1]
    w = jnp.zeros((d, LANES), f32).at[:, :n_experts].set(router_w)
    return pl.pallas_call(
        functools.partial(_router_kernel, n_experts=n_experts),
        grid=(m // tm,),
        in_specs=[pl.BlockSpec((tm, d), lambda i: (i, 0)), pl.BlockSpec((d, LANES), lambda i: (0, 0))],
        out_specs=pl.BlockSpec((tm, LANES), lambda i: (i, 0)),
        out_shape=jax.ShapeDtypeStruct((m, LANES), f32),
        compiler_params=_params("parallel"),
        name="route_top2",
    )(x, w)


def _moe_ln_kernel(tok_ref, gs_ref, off_ref, x_ref, w1_ref, w3_ref, w2_ref, g_ref, b_ref, o_ref,
                   xg_ref, xb_ref, y_ref, *, rt):
    c = pl.program_id(0)
    e = pl.program_id(1)
    f = pl.program_id(2)
    last_f = pl.num_programs(2) - 1
    start = off_ref[0, e]
    count = off_ref[0, e + 1] - start
    n_tiles = (count + rt - 1) // rt

    @pl.when((c == 0) & (e == 0) & (f == 0))
    def _():
        xg_ref[...] = jnp.zeros_like(xg_ref)

    @pl.when((e == 0) & (f == 0))
    def _():
        o_ref[...] = jnp.zeros_like(o_ref)

    def row_loop(body):
        def group(j, carry):
            for u in range(ROW_UNROLL):
                body(j * ROW_UNROLL + u)
            return carry

        def single(r, carry):
            body(r)
            return carry
        lax.fori_loop(0, count // ROW_UNROLL, group, 0)
        lax.fori_loop((count // ROW_UNROLL) * ROW_UNROLL, count, single, 0)

    @pl.when(f == 0)
    def _():
        def gather(r):
            t = tok_ref[0, start + r]
            xg_ref[pl.ds(r, 1), :] = x_ref[pl.ds(t, 1), :]
        row_loop(gather)

        def cast(j, carry):
            rows = pl.ds(pl.multiple_of(j * rt, rt), rt)
            xb_ref[rows, :] = xg_ref[rows, :].astype(MXU_DTYPE)
            return carry
        lax.fori_loop(0, n_tiles, cast, 0)

    def tile(j, carry):
        rows = pl.ds(pl.multiple_of(j * rt, rt), rt)
        y = _swiglu_tile(xb_ref[rows, :], w1_ref, w3_ref, w2_ref)

        @pl.when(f == 0)
        def _():
            y_ref[rows, :] = y

        @pl.when(f != 0)
        def _():
            y_ref[rows, :] += y
        return carry
    lax.fori_loop(0, n_tiles, tile, 0)

    @pl.when(f == last_f)
    def _():
        def updated(r):
            t = tok_ref[0, start + r]
            return t, o_ref[pl.ds(t, 1), :] + gs_ref[0, start + r] * y_ref[pl.ds(r, 1), :]

        def group(j, carry):
            rows = [updated(j * ROW_UNROLL + u) for u in range(ROW_UNROLL)]
            for t, row in rows:
                o_ref[pl.ds(t, 1), :] = row
            return carry

        def single(r, carry):
            t, row = updated(r)
            o_ref[pl.ds(t, 1), :] = row
            return carry
        lax.fori_loop(0, count // ROW_UNROLL, group, 0)
        lax.fori_loop((count // ROW_UNROLL) * ROW_UNROLL, count, single, 0)

    @pl.when((e == pl.num_programs(1) - 1) & (f == last_f))
    def _():
        o_ref[...] = _layer_norm_rows(DEEPNORM_ALPHA * x_ref[...] + o_ref[...], g_ref[...], b_ref[...])


def moe_residual_ln(x, routes, w1, w3, w2, g, b, tm, tf, rt=128):
    m, d = x.shape
    n_experts, _, ff = w1.shape
    n_chunks = m // tm
    ids = routes[:, ROUTE_IDS:ROUTE_IDS + TOP_K].astype(i32).reshape(n_chunks, tm * TOP_K)
    gts = routes[:, ROUTE_GATES:ROUTE_GATES + TOP_K].reshape(n_chunks, tm * TOP_K)
    order = jnp.argsort(ids, axis=1, stable=True).astype(i32)
    tok = order // TOP_K
    gs = jnp.take_along_axis(gts, order, axis=1)
    counts = jnp.sum(ids[:, :, None] == jnp.arange(n_experts, dtype=i32)[None, None, :], axis=1, dtype=i32)
    offs = jnp.concatenate([jnp.zeros((n_chunks, 1), i32), jnp.cumsum(counts, axis=1, dtype=i32)], axis=1)
    smem = lambda width: pl.BlockSpec((None, 1, width), lambda c, e, f: (c, 0, 0), memory_space=pltpu.SMEM)
    return pl.pallas_call(
        functools.partial(_moe_ln_kernel, rt=rt),
        grid=(n_chunks, n_experts, ff // tf),
        in_specs=[smem(tm * TOP_K), smem(tm * TOP_K), smem(n_experts + 1),
                  pl.BlockSpec((tm, d), lambda c, e, f: (c, 0), pipeline_mode=pl.Buffered(1)),
                  pl.BlockSpec((None, d, tf), lambda c, e, f: (e, 0, f)),
                  pl.BlockSpec((None, d, tf), lambda c, e, f: (e, 0, f)),
                  pl.BlockSpec((None, tf, d), lambda c, e, f: (e, f, 0)),
                  pl.BlockSpec((1, d), lambda c, e, f: (0, 0)), pl.BlockSpec((1, d), lambda c, e, f: (0, 0))],
        out_specs=pl.BlockSpec((tm, d), lambda c, e, f: (c, 0), pipeline_mode=pl.Buffered(1)),
        out_shape=jax.ShapeDtypeStruct((m, d), f32),
        scratch_shapes=[pltpu.VMEM((tm, d), f32), pltpu.VMEM((tm, d), MXU_DTYPE), pltpu.VMEM((tm, d), f32)],
        compiler_params=_params("arbitrary", "arbitrary", "arbitrary"),
        name="moe_residual_ln",
    )(tok[:, None, :], gs[:, None, :], offs[:, None, :], x, w1, w3, w2, g.reshape(1, d), b.reshape(1, d))


def _flash_reset(m_ref, acc_ref):
    m_ref[...] = jnp.full(m_ref.shape, NEG, f32)
    acc_ref[...] = jnp.zeros(acc_ref.shape, f32)


def _flash_rows(q, kt, bias, m_ref, alpha_ref, p_ref, rows):
    s = _mm(q, kt)
    if bias is not None:
        s = s + bias
    m_prev = m_ref[rows, :]
    m_next = jnp.maximum(m_prev, jnp.max(s, axis=1, keepdims=True))
    alpha_ref[rows, :] = jnp.exp2(m_prev - m_next)
    m_ref[rows, :] = m_next
    p_ref[rows, :s.shape[1]] = jnp.exp2(s - _lane_tile(m_next, s.shape[1] // LANES)).astype(p_ref.dtype)


def _flash_accumulate(v_aug, alpha_ref, p_ref, acc_ref):
    alpha = _lane_tile(alpha_ref[...], acc_ref.shape[-1] // LANES)
    acc_ref[...] = alpha * acc_ref[...] + _mm(p_ref[:, :v_aug.shape[0]], v_aug)


def _skewed_chunks(n, score, fold, carry):
    def pair(j, carry):
        c = 2 * j
        fold(c - 1, 1)
        carry = score(c, 0, carry)
        fold(c, 0)
        return score(c + 1, 1, carry)

    def single(c, carry):
        fold(c - 1, 1)
        return score(c, 0, carry)

    carry = lax.fori_loop(0, n // 2, pair, carry)
    return lax.fori_loop(2 * (n // 2), n, single, carry)


def _fold_last(n, fold):
    for buf in range(2):
        pl.when((n - 1) % 2 == buf)(functools.partial(fold, n - 1, buf))


def _lane_tile(x, reps):
    return x if reps == 1 else jnp.concatenate([x] * reps, axis=1)


def _row_slices(n_rows):
    return [slice(r * Q_BLOCK, (r + 1) * Q_BLOCK) for r in range(n_rows // Q_BLOCK)]


def _dsa_kernel(qi_ref, wi_ref, kit_ref, qa_ref, kat_ref, va_ref, o_ref,
                skey_ref, wrep_ref, m_ref, alpha_ref, acc_ref, p_ref, qh_ref, *, topk, kc):
    qb = Q_BLOCK
    i = pl.program_id(1)
    q0 = i * qb
    n_chunks = (q0 + qb + kc - 1) // kc
    t_row = q0 + lax.broadcasted_iota(i32, (qb, kc), 0)
    col = lax.broadcasted_iota(i32, (qb, kc), 1)
    wi = wi_ref[...]
    qi = qi_ref[...]
    qis = [qi[:, h * IDX_DIM:(h + 1) * IDX_DIM] for h in range(IDX_HEADS)]
    qa = qa_ref[...]
    for h in range(A_HEADS):
        qh_ref[h * qb:(h + 1) * qb, :] = qa[:, h * HEAD_DIM:(h + 1) * HEAD_DIM]
    for h in range(IDX_HEADS):
        wrep_ref[h] = jnp.broadcast_to(wi[:, h:h + 1], (qb, kc))

    def score_body(c, carry):
        off = pl.multiple_of(c * kc, kc)
        kt = kit_ref[:, pl.ds(off, kc)]
        s = jnp.zeros((qb, kc), f32)
        for h in range(IDX_HEADS):
            s = s + wrep_ref[h] * jnp.maximum(_mm(qis[h], kt), 0.0)
        s = jnp.where(col + off <= t_row, s, NEG)
        bits = pltpu.bitcast(s, i32)
        skey_ref[:, pl.ds(off, kc)] = jnp.where(bits < 0, bits ^ 0x7FFFFFFF, bits)
        return carry

    n_strips = (q0 + qb + COUNT_STRIP - 1) // COUNT_STRIP
    lax.fori_loop(0, n_strips * (COUNT_STRIP // kc), score_body, 0)

    def count_ge(cand):
        def body(c, acc):
            off = pl.multiple_of(c * COUNT_STRIP, COUNT_STRIP)
            for j in range(COUNT_STRIP // LANES):
                acc = acc + jnp.where(skey_ref[:, pl.ds(off + j * LANES, LANES)] >= cand, 1.0, 0.0)
            return acc
        acc = lax.fori_loop(0, n_strips, body, jnp.zeros((qb, LANES), f32))
        return jnp.sum(acc, axis=1, keepdims=True)

    thr = jnp.where(count_ge(jnp.zeros((qb, 1), i32)) >= topk, 0, INT_MIN).astype(i32)

    def bisect(b, thr):
        cand = thr + jnp.left_shift(jnp.int32(1), 30 - b)
        return jnp.where(count_ge(cand) >= topk, cand, thr)

    thr = lax.fori_loop(0, 31, bisect, thr)
    quota = topk - count_ge(thr + 1)

    before = (lax.broadcasted_iota(i32, (kc, kc), 0) < lax.broadcasted_iota(i32, (kc, kc), 1))
    before = jnp.where(before, 1.0, 0.0).astype(MXU_DTYPE)
    _flash_reset(m_ref, acc_ref)
    alpha_ref[...] = jnp.zeros(alpha_ref.shape, f32)
    p_ref[...] = jnp.zeros(p_ref.shape, p_ref.dtype)
    heads = _row_slices(A_HEADS * qb)

    def fold(c, buf):
        off = pl.multiple_of(jnp.maximum(c, 0) * kc, kc)
        _flash_accumulate(va_ref[pl.ds(off, kc), :], alpha_ref.at[buf], p_ref.at[buf], acc_ref)

    def score(c, buf, ties_seen):
        off = pl.multiple_of(c * kc, kc)
        key = skey_ref[:, pl.ds(off, kc)]
        eq = key == thr
        eqf = jnp.where(eq, 1.0, 0.0)
        rank = ties_seen + _mm(eqf.astype(MXU_DTYPE), before)
        sel = ((key > thr) | (eq & (rank < quota))) & (col + off <= t_row)
        bias = jnp.where(sel, 0.0, NEG)
        kt = kat_ref[:, pl.ds(off, kc)]
        for rows in heads:
            _flash_rows(qh_ref[rows, :], kt, bias, m_ref, alpha_ref.at[buf], p_ref.at[buf], rows)
        return ties_seen + jnp.sum(eqf, axis=1, keepdims=True)

    _skewed_chunks(n_chunks, score, fold, jnp.zeros((qb, 1), f32))
    _fold_last(n_chunks, fold)
    acc = acc_ref[...]
    o = acc[:, :HEAD_DIM] / acc[:, HEAD_DIM:]
    for h in range(A_HEADS):
        o_ref[:, h * HEAD_DIM:(h + 1) * HEAD_DIM] = o[h * qb:(h + 1) * qb, :]


QPK_QA, QPK_QIDX, QPK_QB = 0, 2, 3
KPK_KA, KPK_KIDX, KPK_KSLC, KPK_KWIN = 0, 1, 2, 4
VPK_VA, VPK_VSLC, VPK_VWIN = 0, 1, 3
OUT_NSA = 2


def dsa_attention(wi, qpk, kpk, vpk, topk, kc=512):
    bsz, seq, _ = qpk.shape
    nb = seq // Q_BLOCK
    rows = A_HEADS * Q_BLOCK
    width = A_HEADS * HEAD_DIM
    kern = functools.partial(_dsa_kernel, topk=topk, kc=kc)
    return pl.pallas_call(
        kern,
        grid=(bsz, nb),
        in_specs=[pl.BlockSpec((None, Q_BLOCK, IDX_HEADS * IDX_DIM), lambda b, i: (b, i, QPK_QIDX)),
                  pl.BlockSpec((None, Q_BLOCK, IDX_HEADS), lambda b, i: (b, i, 0)),
                  pl.BlockSpec((None, IDX_DIM, seq), lambda b, i: (b, KPK_KIDX, 0)),
                  pl.BlockSpec((None, Q_BLOCK, width), lambda b, i: (b, i, QPK_QA)),
                  pl.BlockSpec((None, HEAD_DIM, seq), lambda b, i: (b, KPK_KA, 0)),
                  pl.BlockSpec((None, seq, 2 * HEAD_DIM), lambda b, i: (b, 0, VPK_VA))],
        out_specs=pl.BlockSpec((None, Q_BLOCK, width), lambda b, i: (b, i, 0)),
        out_shape=jax.ShapeDtypeStruct((bsz, seq, 2 * width), f32),
        scratch_shapes=[pltpu.VMEM((Q_BLOCK, seq), i32), pltpu.VMEM((IDX_HEADS, Q_BLOCK, kc), f32),
                        pltpu.VMEM((rows, LANES), f32), pltpu.VMEM((2, rows, LANES), f32),
                        pltpu.VMEM((rows, 2 * HEAD_DIM), f32), pltpu.VMEM((2, rows, kc), MXU_DTYPE),
                        pltpu.VMEM((rows, HEAD_DIM), MXU_DTYPE)],
        compiler_params=_params("parallel", "arbitrary"),
        name="dsa_attention",
    )(qpk, wi, kpk, qpk, kpk, vpk)


def _compress_kernel(x_ref, pe_ref, w1_ref, w2_ref, o_ref):
    half = (CMP_LEN // 2) * HEAD_DIM
    x = x_ref[...]
    first = _mm((x + pe_ref[0:1, :]).astype(MXU_DTYPE), w1_ref[0:half, :])
    second = _mm((x + pe_ref[1:2, :]).astype(MXU_DTYPE), w1_ref[half:2 * half, :])
    pre = first + pltpu.roll(second, shift=x.shape[0] - 1, axis=0)
    hid = pre * jax.nn.sigmoid(pre)
    o_ref[...] = _mm(hid.astype(MXU_DTYPE), w2_ref[...])


def nsa_compress(chunks, pe2, w1, w2):
    bsz, groups, nck, width = chunks.shape
    return pl.pallas_call(
        _compress_kernel,
        grid=(bsz, groups),
        in_specs=[pl.BlockSpec((None, None, nck, width), lambda b, g: (b, g, 0, 0)),
                  pl.BlockSpec((2, width), lambda b, g: (0, 0)),
                  pl.BlockSpec((2 * width, CMP_HIDDEN), lambda b, g: (0, 0)),
                  pl.BlockSpec((CMP_HIDDEN, HEAD_DIM), lambda b, g: (0, 0))],
        out_specs=pl.BlockSpec((None, None, nck, HEAD_DIM), lambda b, g: (b, g, 0, 0)),
        out_shape=jax.ShapeDtypeStruct((bsz, groups, nck, HEAD_DIM), f32),
        compiler_params=_params("parallel", "parallel"),
        name="nsa_compress",
    )(chunks, pe2, w1, w2)


def _nsa_kernel(qin_ref, graw_ref, kct_ref, vc_ref, kst_ref, vs_ref, kwt_ref, vw_ref, ovl_ref, _, o_ref,
                m_ref, alpha_ref, acc_ref, p_ref, pw_ref, q_ref, imp_ref, *, n_slc, n_sel, kc):
    qb = Q_BLOCK
    i = pl.program_id(2)
    q0 = i * qb
    n_cmp = kct_ref.shape[1]
    heads = _row_slices(B_REP * qb)
    qin = qin_ref[...]
    for r in range(B_REP):
        q_ref[r * qb:(r + 1) * qb, :] = qin[:, r * HEAD_DIM:(r + 1) * HEAD_DIM]

    def compressed(width):
        t_c = q0 + lax.broadcasted_iota(i32, (qb, width), 0)
        cmp_end = lax.broadcasted_iota(i32, (qb, width), 1) * CMP_STRIDE + (CMP_LEN - 1)
        vis = cmp_end <= t_c
        kct = kct_ref[:, :width]
        p_sum = jnp.zeros((qb, width), f32)
        for rows in heads:
            lc = jnp.where(vis, _mm(q_ref[rows, :], kct), NEG)
            ec = jnp.where(vis, jnp.exp2(lc - jnp.max(lc, axis=1, keepdims=True)), 0.0)
            den = jnp.sum(ec, axis=1, keepdims=True)
            p_c = ec / jnp.where(den > 0.0, den, 1.0)
            p_sum = p_sum + p_c
            acc_ref[rows, :HEAD_DIM] = _mm(p_c.astype(MXU_DTYPE), vc_ref[:width, :])
        imp_ref[...] = jnp.dot(p_sum, ovl_ref[:width, :], preferred_element_type=f32,
                               precision=lax.Precision.HIGHEST)

    if n_cmp % (2 * LANES) == 0:
        first_half_only = (q0 + qb) * 2 <= n_cmp * CMP_STRIDE
        pl.when(first_half_only)(functools.partial(compressed, n_cmp // 2))
        pl.when(jnp.logical_not(first_half_only))(functools.partial(compressed, n_cmp))
    else:
        compressed(n_cmp)
    o_c = acc_ref[:, :HEAD_DIM]

    imp = imp_ref[...].T
    t_q = q0 + lax.broadcasted_iota(i32, (LANES, qb), 1)
    blk = lax.broadcasted_iota(i32, (LANES, qb), 0)
    blk_t = t_q // SLC_LEN
    forced = (blk == 0) | (blk == blk_t) | (blk == blk_t - 1)
    imp = jnp.where(forced, FORCED_BOOST, imp)
    imp = jnp.where(blk * SLC_LEN <= t_q, imp, NEG)
    imp = jnp.where(blk < n_slc, imp, -jnp.inf)
    blk_f = blk.astype(f32)

    def pick(_, carry):
        imp, selm = carry
        best = jnp.max(imp, axis=0, keepdims=True)
        first = jnp.min(jnp.where(imp == best, blk_f, float(LANES)), axis=0, keepdims=True)
        hit = blk_f == first
        return jnp.where(hit, -jnp.inf, imp), jnp.where(hit, 1.0, selm)

    _, selm = lax.fori_loop(0, n_sel, pick, (imp, jnp.zeros((LANES, qb), f32)))
    selm = selm.T.astype(MXU_DTYPE)

    t_k = q0 + lax.broadcasted_iota(i32, (qb, kc), 0)
    col = lax.broadcasted_iota(i32, (qb, kc), 1)
    exp_row = lax.broadcasted_iota(i32, (LANES, kc), 0)
    exp_col = lax.broadcasted_iota(i32, (LANES, kc), 1)
    _flash_reset(m_ref, acc_ref)
    alpha_ref[...] = jnp.zeros(alpha_ref.shape, f32)
    p_ref[...] = jnp.zeros(p_ref.shape, p_ref.dtype)

    def fold(c, buf):
        off = pl.multiple_of(jnp.maximum(c, 0) * kc, kc)
        _flash_accumulate(vs_ref[pl.ds(off, kc), :], alpha_ref.at[buf], p_ref.at[buf], acc_ref)

    def score(c, buf, carry):
        off = pl.multiple_of(c * kc, kc)
        expand = jnp.where(exp_row == (exp_col + off) // SLC_LEN, 1.0, 0.0).astype(MXU_DTYPE)
        sel = (_mm(selm, expand) > 0.5) & (col + off <= t_k)
        bias = jnp.where(sel, 0.0, NEG)
        kt = kst_ref[:, pl.ds(off, kc)]
        for rows in heads:
            _flash_rows(q_ref[rows, :], kt, bias, m_ref, alpha_ref.at[buf], p_ref.at[buf], rows)
        return carry

    n_slc_chunks = (q0 + qb + kc - 1) // kc
    _skewed_chunks(n_slc_chunks, score, fold, 0)
    _fold_last(n_slc_chunks, fold)
    acc = acc_ref[...]
    o_s = acc[:, :HEAD_DIM] / acc[:, HEAD_DIM:]

    slab = WINDOW + qb
    w0 = pl.multiple_of(jnp.maximum(q0 - WINDOW, 0), qb)
    dist = (q0 + lax.broadcasted_iota(i32, (qb, slab), 0)) - (w0 + lax.broadcasted_iota(i32, (qb, slab), 1))
    bias = jnp.where((dist >= 0) & (dist < WINDOW), 0.0, NEG)
    kt = kwt_ref[:, pl.ds(w0, slab)]
    for rows in heads:
        s = _mm(q_ref[rows, :], kt) + bias
        pw_ref[rows, :] = jnp.exp2(s - jnp.max(s, axis=1, keepdims=True)).astype(pw_ref.dtype)
    acc = _mm(pw_ref[...], vw_ref[pl.ds(w0, slab), :])
    o_w = acc[:, :HEAD_DIM] / acc[:, HEAD_DIM:]

    gates = jax.nn.sigmoid(graw_ref[...])
    for r, rows in enumerate(heads):
        g_c, g_s, g_w = (gates[:, 3 * r + n:3 * r + n + 1] for n in range(3))
        o_ref[:, r * HEAD_DIM:(r + 1) * HEAD_DIM] = g_c * o_c[rows] + g_s * o_s[rows] + g_w * o_w[rows]


def nsa_attention(qpk, graw, kct, vc, kpk, vpk, ovl, out_ab, n_slc, n_sel, kc=512):
    bsz, seq, _ = qpk.shape
    groups = B_KV_GROUPS
    nb = seq // Q_BLOCK
    rows = B_REP * Q_BLOCK
    width = B_REP * HEAD_DIM
    n_cmp = kct.shape[-1]
    kern = functools.partial(_nsa_kernel, n_slc=n_slc, n_sel=n_sel, kc=kc)
    per_bg = lambda b, g, i: (b, g, 0, 0)
    return pl.pallas_call(
        kern,
        grid=(bsz, groups, nb),
        in_specs=[pl.BlockSpec((None, Q_BLOCK, width), lambda b, g, i: (b, i, QPK_QB + g)),
                  pl.BlockSpec((None, None, Q_BLOCK, B_REP * 3), lambda b, g, i: (b, g, i, 0)),
                  pl.BlockSpec((None, None, HEAD_DIM, n_cmp), per_bg),
                  pl.BlockSpec((None, None, n_cmp, HEAD_DIM), per_bg),
                  pl.BlockSpec((None, HEAD_DIM, seq), lambda b, g, i: (b, KPK_KSLC + g, 0)),
                  pl.BlockSpec((None, seq, 2 * HEAD_DIM), lambda b, g, i: (b, 0, VPK_VSLC + g)),
                  pl.BlockSpec((None, HEAD_DIM, seq), lambda b, g, i: (b, KPK_KWIN + g, 0)),
                  pl.BlockSpec((None, seq, 2 * HEAD_DIM), lambda b, g, i: (b, 0, VPK_VWIN + g)),
                  pl.BlockSpec((n_cmp, LANES), lambda b, g, i: (0, 0)),
                  pl.BlockSpec(memory_space=pl.ANY)],
        out_specs=pl.BlockSpec((None, Q_BLOCK, width), lambda b, g, i: (b, i, OUT_NSA + g)),
        out_shape=jax.ShapeDtypeStruct(out_ab.shape, f32),
        input_output_aliases={9: 0},
        scratch_shapes=[pltpu.VMEM((rows, LANES), f32), pltpu.VMEM((2, rows, LANES), f32),
                        pltpu.VMEM((rows, 2 * HEAD_DIM), f32), pltpu.VMEM((2, rows, kc), MXU_DTYPE),
                        pltpu.VMEM((rows, WINDOW + Q_BLOCK), MXU_DTYPE), pltpu.VMEM((rows, HEAD_DIM), MXU_DTYPE),
                        pltpu.VMEM((Q_BLOCK, LANES), f32)],
        compiler_params=_params("parallel", "parallel", "arbitrary"),
        name="nsa_attention",
    )(qpk, graw, kct, vc, kpk, vpk, kpk, vpk, ovl, out_ab)


def _diff_kernel(lam_ref, qin_ref, kt_ref, v_ref, g_ref, o_ref, m_ref, alpha_ref, acc_ref, p_ref, q_ref,
                 *, tq, kc, out_scale):
    i = pl.program_id(2)
    q0 = i * tq
    dv = v_ref.shape[-1] // 2
    groups = _row_slices(tq)
    qin = qin_ref[...]
    for half in range(2):
        q_ref[half] = qin[:, half * HEAD_DIM:(half + 1) * HEAD_DIM]
    for half in range(2):
        _flash_reset(m_ref.at[half], acc_ref.at[half])
    alpha_ref[...] = jnp.zeros(alpha_ref.shape, f32)
    p_ref[...] = jnp.zeros(p_ref.shape, p_ref.dtype)

    def fold(c, buf, sub=(0, kc // tq)):
        off = pl.multiple_of(jnp.maximum(c, 0) * kc + sub[0] * tq, tq)
        v = v_ref[pl.ds(off, sub[1] * tq), :]
        for half in range(2):
            _flash_accumulate(v, alpha_ref.at[buf, half], p_ref.at[buf, half], acc_ref.at[half])

    def score(c, buf, masked, sub=(0, kc // tq)):
        off = pl.multiple_of(c * kc + sub[0] * tq, tq)
        width = sub[1] * tq
        for half in range(2):
            kt = kt_ref[half * HEAD_DIM:(half + 1) * HEAD_DIM, pl.ds(off, width)]
            for r, rows in enumerate(groups):
                bias = None
                if masked:
                    key = off + lax.broadcasted_iota(i32, (Q_BLOCK, width), 1)
                    t = q0 + r * Q_BLOCK + lax.broadcasted_iota(i32, (Q_BLOCK, width), 0)
                    bias = jnp.where(key <= t, 0.0, NEG)
                _flash_rows(q_ref[half, rows, :], kt, bias, m_ref.at[half], alpha_ref.at[buf, half],
                            p_ref.at[buf, half], rows)

    def score_full(c, buf, carry):
        score(c, buf, False)
        return carry

    n_full = q0 // kc
    _skewed_chunks(n_full, score_full, fold, 0)

    own = (q0 - n_full * kc) // tq

    def finish(buf, own_piece):
        fold(n_full - 1, 1 - buf)
        if own_piece:
            score(n_full, buf, False, (0, own_piece))
            fold(n_full, buf, (0, own_piece))
            buf = 1 - buf
        score(n_full, buf, True, (own_piece, 1))
        fold(n_full, buf, (own_piece, 1))

    for buf in range(2):
        for own_piece in range(kc // tq):
            pl.when((n_full % 2 == buf) & (own == own_piece))(functools.partial(finish, buf, own_piece))
    a1 = acc_ref[0]
    a2 = acc_ref[1]
    o = a1[:, :dv] / a1[:, dv:] - lam_ref[0] * (a2[:, :dv] / a2[:, dv:])
    o = o * lax.rsqrt(jnp.mean(o * o, axis=-1, keepdims=True) + LN_EPS) * g_ref[...]
    o_ref[...] = o * out_scale


def diff_attention(lam, q, kt, v, subln_g, out_scale, tq=512, kc=1024):
    bsz, seq, width = q.shape
    dv = 2 * HEAD_DIM
    heads = width // dv
    tq = min(tq, seq)
    kc = min(kc, seq)
    assert kc % tq == 0 and seq % kc == 0
    kern = functools.partial(_diff_kernel, tq=tq, kc=kc, out_scale=out_scale)
    return pl.pallas_call(
        kern,
        grid=(bsz, heads, seq // tq),
        in_specs=[pl.BlockSpec(memory_space=pltpu.SMEM),
                  pl.BlockSpec((None, tq, dv), lambda b, h, i: (b, i, h)),
                  pl.BlockSpec((None, dv, seq), lambda b, h, i: (b, h, 0)),
                  pl.BlockSpec((None, seq, 2 * dv), lambda b, h, i: (b, 0, h)),
                  pl.BlockSpec((1, dv), lambda b, h, i: (0, 0))],
        out_specs=pl.BlockSpec((None, tq, dv), lambda b, h, i: (b, i, h)),
        out_shape=jax.ShapeDtypeStruct((bsz, seq, heads * dv), f32),
        scratch_shapes=[pltpu.VMEM((2, tq, LANES), f32), pltpu.VMEM((2, 2, tq, LANES), f32),
                        pltpu.VMEM((2, tq, 2 * dv), f32), pltpu.VMEM((2, 2, tq, kc), MXU_DTYPE),
                        pltpu.VMEM((2, tq, HEAD_DIM), MXU_DTYPE)],
        compiler_params=_params("parallel", "parallel", "arbitrary"),
        name="diff_attention",
    )(lam, q, kt, v, subln_g.reshape(1, dv))


def _rope_tables(positions):
    inv_freq = ROPE_THETA ** (-jnp.arange(0, ROT_DIM, 2, dtype=f32) / ROT_DIM)
    ang = positions.astype(f32)[..., None] * inv_freq
    return jnp.cos(ang), jnp.sin(ang)


def _apply_rope(x, cos, sin):
    shape = cos.shape[:2] + (1,) * (x.ndim - 3) + cos.shape[-1:]
    c = cos.reshape(shape)
    s = sin.reshape(shape)
    half = ROT_DIM // 2
    x1, x2 = x[..., :half], x[..., half:ROT_DIM]
    return jnp.concatenate([x1 * c - x2 * s, x2 * c + x1 * s, x[..., ROT_DIM:]], axis=-1)


def _rope_coefficients(cos, sin):
    lead = cos.shape[:-1]
    rest = jnp.zeros(lead + (HEAD_DIM - ROT_DIM,), f32)
    none = jnp.zeros_like(sin)
    keep, hi, lo = [cos, cos, rest + 1.0], [-sin, none, rest], [none, sin, rest]
    return jnp.concatenate(2 * keep + 2 * hi + 2 * lo, axis=-1)


def _ab_weight_columns(w_in):
    widths = dict(AB_LAYOUT)
    starts = dict(zip(widths, np.cumsum([0] + [w for _, w in AB_LAYOUT[:-1]]).tolist()))
    cols = lambda names: np.concatenate([np.arange(starts[n], starts[n] + widths[n]) for n in names])
    first = ("q_a", "q_idx", "q_b", "k_a", "k_idx", "k_slc", "k_win", "v_a", "v_slc", "v_win", "gate_b", "w_idx")
    n_first = sum(widths[n] for n in first)
    pad = AB_CMP_TILE * LANES - n_first
    assert 0 <= pad < HEAD_DIM
    w = jnp.concatenate([w_in[:, cols(first)], jnp.zeros((w_in.shape[0], pad), w_in.dtype),
                         w_in[:, cols(("k_cmp", "v_cmp"))]], axis=1)
    assert w.shape[1] == AB_TILES * LANES
    return w.astype(MXU_DTYPE)


def _overlap_matrix(n_cmp_rows, n_slc):
    c_start = np.arange(n_cmp_rows) * CMP_STRIDE
    s_start = np.arange(LANES) * SLC_LEN
    ovl = (c_start[:, None] < s_start[None, :] + SLC_LEN) & (c_start[:, None] + CMP_LEN > s_start[None, :])
    ovl = ovl & (np.arange(LANES)[None, :] < n_slc)
    return jnp.asarray(ovl.astype(np.float32))


def _ab_mixer(x2, bsz, seq, positions, coef, w_in, pe_k, pe_v, ck1, ck2, cv1, cv2):
    qpk, kpk, vpk, misc, cmp = project_pack_ab(x2.reshape(bsz, seq, -1), _ab_weight_columns(w_in), coef)
    gate_w = B_HEADS * 3
    g_b = misc[..., HEAD_DIM:HEAD_DIM + gate_w]
    w_idx = misc[..., HEAD_DIM + gate_w:HEAD_DIM + gate_w + IDX_HEADS] * (IDX_HEADS * IDX_DIM) ** -0.5
    k_cmp, v_cmp = cmp[..., :LANES], cmp[..., LANES:]

    out_ab = dsa_attention(w_idx, qpk, kpk, vpk, topk=min(DSA_TOPK, seq // 4))

    groups = B_KV_GROUPS
    n_rows = seq // CMP_STRIDE
    n_cmp = (seq - CMP_LEN) // CMP_STRIDE + 1
    assert n_cmp == n_rows - 1
    n_slc = seq // SLC_LEN
    assert n_slc <= LANES

    def chunked(kv):
        kv = kv.reshape(bsz, n_rows, CMP_STRIDE, groups, HEAD_DIM).transpose(0, 3, 1, 2, 4)
        return kv.reshape(bsz, groups, n_rows, CMP_STRIDE * HEAD_DIM)

    def pe_halves(pe):
        return pe.reshape(2, CMP_STRIDE * HEAD_DIM)

    k_c = nsa_compress(chunked(k_cmp), pe_halves(pe_k), ck1.astype(MXU_DTYPE), ck2.astype(MXU_DTYPE))
    v_c = nsa_compress(chunked(v_cmp), pe_halves(pe_v), cv1.astype(MXU_DTYPE), cv2.astype(MXU_DTYPE))
    cmp_end = jnp.minimum(jnp.arange(n_rows) * CMP_STRIDE + CMP_LEN - 1, seq - 1)
    cos_c, sin_c = _rope_tables(positions[:, cmp_end])
    k_c = _apply_rope(k_c.transpose(0, 2, 1, 3), cos_c, sin_c)
    k_c = k_c.transpose(0, 2, 3, 1).astype(MXU_DTYPE)
    v_c = v_c.astype(MXU_DTYPE)

    graw = g_b.reshape(bsz, seq, groups, B_REP * 3).transpose(0, 2, 1, 3)
    out_ab = nsa_attention(qpk, graw, k_c, v_c, kpk, vpk, _overlap_matrix(n_rows, n_slc), out_ab,
                           n_slc=n_slc, n_sel=min(SLC_TOPN, n_slc))
    return out_ab.reshape(bsz * seq, (A_HEADS + B_HEADS) * HEAD_DIM)


def _diff_mixer(x2, bsz, seq, coef, w_in, lq1, lk1, lq2, lk2, subln_g, lam_init):
    q, kt, v = project_pack_c(x2.reshape(bsz, seq, -1), w_in.astype(MXU_DTYPE), coef)
    lam =(jnp.exp(jnp.sum(lq1 * lk1)) - jnp.exp(jnp.sum(lq2 * lk2)) + lam_init).reshape(1).astype(f32)
    o = diff_attention(lam, q, kt, v, subln_g, 1.0 - lam_init)
    return o.reshape(bsz * seq, C_HEADS * 2 * HEAD_DIM)


def kernel(x, positions, ab_w_in, cmp_pe_k, cmp_pe_v, cmp_k_w1, cmp_k_w2, cmp_v_w1, cmp_v_w2, ab_w_out, ln_ab_g, ln_ab_b, ffn_w1, ffn_w3, ffn_w2, ln_ffn_g, ln_ffn_b, c_w_in, lambda_q1, lambda_k1, lambda_q2, lambda_k2, c_subln_g, c_w_out, ln_c_g, ln_c_b, router_w, moe_w1, moe_w3, moe_w2, ln_moe_g, ln_moe_b):
    bsz, seq, d = x.shape
    assert seq % COUNT_STRIP == 0 and seq >= WINDOW + Q_BLOCK and d == D_MODEL
    coef = _rope_coefficients(*_rope_tables(positions))
    x2 = x.reshape(bsz * seq, d)
    for layer in range(DEPTH):
        i = layer // 2
        if layer % 2 == 0:
            o = _ab_mixer(x2, bsz, seq, positions, coef, ab_w_in[i], cmp_pe_k[i], cmp_pe_v[i],
                          cmp_k_w1[i], cmp_k_w2[i], cmp_v_w1[i], cmp_v_w2[i])
            x2 = project_residual_ln(o, ab_w_out[i].astype(MXU_DTYPE), x2, ln_ab_g[i], ln_ab_b[i])
            x2 = ffn_residual_ln(x2, ffn_w1[i].astype(MXU_DTYPE), ffn_w3[i].astype(MXU_DTYPE),
                                 ffn_w2[i].astype(MXU_DTYPE), ln_ffn_g[i], ln_ffn_b[i], tm=512, tf=ffn_w1.shape[-1])
        else:
            lam_init = 0.8 - 0.6 * math.exp(-0.3 * layer)
            o = _diff_mixer(x2, bsz, seq, coef, c_w_in[i], lambda_q1[i], lambda_k1[i], lambda_q2[i],
                            lambda_k2[i], c_subln_g[i], lam_init)
            x2 = project_residual_ln(o, c_w_out[i].astype(MXU_DTYPE), x2, ln_c_g[i], ln_c_b[i])
            routes = route_top2(x2, router_w[i])
            x2 = moe_residual_ln(x2, routes, moe_w1[i].astype(MXU_DTYPE), moe_w3[i].astype(MXU_DTYPE),
                                 moe_w2[i].astype(MXU_DTYPE), ln_moe_g[i], ln_moe_b[i], tm=min(2048, bsz * seq // 2), tf=896, rt=128)
    return x2.reshape(bsz, seq, d)
```

```python
import functools
import math

import numpy as np
import jax
import jax.numpy as jnp
from jax import lax
from jax.experimental import pallas as pl
from jax.experimental.pallas import tpu as pltpu

f32 = jnp.float32
i32 = jnp.int32
MXU_DTYPE = jnp.bfloat16
VMEM_LIMIT_BYTES = 56 * 1024 * 1024
LANES = 128

D_MODEL = 1024
DEPTH = 2
HEAD_DIM = 64
ROT_DIM = HEAD_DIM // 4
ROPE_THETA = 500000.0
Q_BLOCK = 128
NEG = -1e30
LN_EPS = 1e-5
A_HEADS = 8
IDX_HEADS = 4
IDX_DIM = 64
DSA_TOPK = 256
B_HEADS = 8
B_KV_GROUPS = 2
B_REP = B_HEADS // B_KV_GROUPS
CMP_LEN = 32
CMP_STRIDE = 16
CMP_HIDDEN = 128
SLC_LEN = 64
SLC_TOPN = 16
WINDOW = 512
FORCED_BOOST = 1e6
C_HEADS = 8
N_EXPERTS = 8
TOP_K = 2
DEEPNORM_ALPHA = (2 * DEPTH) ** 0.25
QK_SCALE = HEAD_DIM ** -0.5 * math.log2(math.e)
INT_MIN = -(2 ** 31)
COUNT_STRIP = 512
ROW_UNROLL = 4

AB_LAYOUT = (
    ("q_a", A_HEADS * HEAD_DIM), ("k_a", HEAD_DIM), ("v_a", HEAD_DIM),
    ("q_idx", IDX_HEADS * IDX_DIM), ("k_idx", IDX_DIM), ("w_idx", IDX_HEADS),
    ("q_b", B_HEADS * HEAD_DIM),
    ("k_cmp", B_KV_GROUPS * HEAD_DIM), ("v_cmp", B_KV_GROUPS * HEAD_DIM),
    ("k_slc", B_KV_GROUPS * HEAD_DIM), ("v_slc", B_KV_GROUPS * HEAD_DIM),
    ("k_win", B_KV_GROUPS * HEAD_DIM), ("v_win", B_KV_GROUPS * HEAD_DIM),
    ("gate_b", 3 * B_HEADS),
)


def _params(*sem):
    return pltpu.CompilerParams(dimension_semantics=sem, vmem_limit_bytes=VMEM_LIMIT_BYTES)


def _mm(a, b):
    return jnp.dot(a, b, preferred_element_type=f32)


def _layer_norm_rows(y, g, b):
    mu = jnp.mean(y, axis=-1, keepdims=True)
    yc = y - mu
    var = jnp.mean(yc * yc, axis=-1, keepdims=True)
    return yc * lax.rsqrt(var + LN_EPS) * g + b


def _rope_tile(x, keep, hi, lo):
    return x * keep + pltpu.roll(x, LANES - ROT_DIM // 2, 1) * hi + pltpu.roll(x, ROT_DIM // 2, 1) * lo


def _tile(h, j):
    return h[:, j * LANES:(j + 1) * LANES]


AB_Q_TILES = 10
AB_QIDX_TILES = (4, 5)
AB_K_TILES = 3
AB_V_HEADS = 5
AB_MISC_TILE = 15
AB_CMP_TILE = 16
AB_TILES = 18


def _proj_pack_ab_kernel(x_ref, w_ref, coef_ref, q_ref, kt_ref, v_ref, misc_ref, cmp_ref):
    h = _mm(x_ref[...].astype(MXU_DTYPE), w_ref[...])
    keep, hi, lo = (coef_ref[:, j * LANES:(j + 1) * LANES] for j in range(3))
    for j in range(AB_Q_TILES):
        t = _rope_tile(_tile(h, j), keep, hi, lo)
        q_ref[:, j * LANES:(j + 1) * LANES] = (t if j in AB_QIDX_TILES else t * QK_SCALE).astype(q_ref.dtype)
    k = jnp.concatenate([_rope_tile(_tile(h, AB_Q_TILES + j), keep, hi, lo) for j in range(AB_K_TILES)], axis=1)
    kt_ref[...] = k.T.astype(kt_ref.dtype)
    low = lax.broadcasted_iota(i32, keep.shape, 1) < HEAD_DIM
    for j in range(AB_V_HEADS):
        t = _tile(h, AB_Q_TILES + AB_K_TILES + j // 2)
        t = pltpu.roll(t, HEAD_DIM, 1) if j % 2 else t
        v_ref[:, j * LANES:(j + 1) * LANES] = jnp.where(low, t, 1.0).astype(v_ref.dtype)
    misc_ref[...] = _tile(h, AB_MISC_TILE)
    cmp_ref[...] = h[:, AB_CMP_TILE * LANES:AB_TILES * LANES]


def project_pack_ab(x, w, coef, tm=512):
    bsz, seq, d = x.shape
    row = lambda width: pl.BlockSpec((None, tm, width), lambda b, i: (b, i, 0))
    kt_rows = AB_K_TILES * LANES
    return pl.pallas_call(
        _proj_pack_ab_kernel,
        grid=(bsz, seq // tm),
        in_specs=[row(d), pl.BlockSpec((d, AB_TILES * LANES), lambda b, i: (0, 0)), row(3 * LANES)],
        out_specs=[row(AB_Q_TILES * LANES), pl.BlockSpec((None, kt_rows, tm), lambda b, i: (b, 0, i)),
                   row(AB_V_HEADS * LANES), row(LANES), row(2 * LANES)],
        out_shape=[jax.ShapeDtypeStruct((bsz, seq, AB_Q_TILES * LANES), MXU_DTYPE),
                   jax.ShapeDtypeStruct((bsz, kt_rows, seq), MXU_DTYPE),
                   jax.ShapeDtypeStruct((bsz, seq, AB_V_HEADS * LANES), MXU_DTYPE),
                   jax.ShapeDtypeStruct((bsz, seq, LANES), f32),
                   jax.ShapeDtypeStruct((bsz, seq, 2 * LANES), f32)],
        compiler_params=_params("parallel", "parallel"),
        name="project_pack_ab",
    )(x, w, coef)


def _proj_pack_c_kernel(x_ref, w_ref, coef_ref, q_ref, kt_ref, v_ref):
    h = _mm(x_ref[...].astype(MXU_DTYPE), w_ref[...])
    keep, hi, lo = (coef_ref[:, j * LANES:(j + 1) * LANES] for j in range(3))
    n = q_ref.shape[1] // LANES
    for j in range(n):
        q_ref[:, j * LANES:(j + 1) * LANES] = (_rope_tile(_tile(h, j), keep, hi, lo) * QK_SCALE).astype(q_ref.dtype)
    k = jnp.concatenate([_rope_tile(_tile(h, n + j), keep, hi, lo) for j in range(n)], axis=1)
    kt_ref[...] = k.T.astype(kt_ref.dtype)
    ones = jnp.ones(keep.shape, v_ref.dtype)
    for j in range(n):
        v_ref[:, 2 * j * LANES:(2 * j + 1) * LANES] = _tile(h, 2 * n + j).astype(v_ref.dtype)
        v_ref[:, (2 * j + 1) * LANES:(2 * j + 2) * LANES] = ones


def project_pack_c(x, w, coef, tm=512):
    bsz, seq, d = x.shape
    width = w.shape[1] // 3
    row = lambda cols: pl.BlockSpec((None, tm, cols), lambda b, i: (b, i, 0))
    return pl.pallas_call(
        _proj_pack_c_kernel,
        grid=(bsz, seq // tm),
        in_specs=[row(d), pl.BlockSpec((d, 3 * width), lambda b, i: (0, 0)), row(3 * LANES)],
        out_specs=[row(width), pl.BlockSpec((None, width, tm), lambda b, i: (b, 0, i)), row(2 * width)],
        out_shape=[jax.ShapeDtypeStruct((bsz, seq, width), MXU_DTYPE),
                   jax.ShapeDtypeStruct((bsz, width, seq), MXU_DTYPE),
                   jax.ShapeDtypeStruct((bsz, seq, 2 * width), MXU_DTYPE)],
        compiler_params=_params("parallel", "parallel"),
        name="project_pack_c",
    )(x, w, coef)


def _proj_ln_kernel(a_ref, w_ref, res_ref, g_ref, b_ref, o_ref):
    h = _mm(a_ref[...].astype(MXU_DTYPE), w_ref[...])
    o_ref[...] = _layer_norm_rows(DEEPNORM_ALPHA * res_ref[...] + h, g_ref[...], b_ref[...])


def project_residual_ln(a, w, res, g, b, tm=512):
    m, k = a.shape
    n = w.shape[1]
    return pl.pallas_call(
        _proj_ln_kernel,
        grid=(m // tm,),
        in_specs=[pl.BlockSpec((tm, k), lambda i: (i, 0)), pl.BlockSpec((k, n), lambda i: (0, 0)),
                  pl.BlockSpec((tm, n), lambda i: (i, 0)),
                  pl.BlockSpec((1, n), lambda i: (0, 0)), pl.BlockSpec((1, n), lambda i: (0, 0))],
        out_specs=pl.BlockSpec((tm, n), lambda i: (i, 0)),
        out_shape=jax.ShapeDtypeStruct((m, n), f32),
        compiler_params=_params("parallel"),
        name="project_residual_ln",
    )(a, w, res, g.reshape(1, n), b.reshape(1, n))


def _swiglu_tile(xb, w1_ref, w3_ref, w2_ref):
    a = _mm(xb, w1_ref[...])
    h = (a * jax.nn.sigmoid(a)) * _mm(xb, w3_ref[...])
    return _mm(h.astype(MXU_DTYPE), w2_ref[...])


def _ffn_ln_kernel(x_ref, w1_ref, w3_ref, w2_ref, g_ref, b_ref, o_ref, xb_ref, acc_ref):
    f = pl.program_id(1)

    @pl.when(f == 0)
    def _():
        xb_ref[...] = x_ref[...].astype(MXU_DTYPE)
        acc_ref[...] = jnp.zeros_like(acc_ref)

    acc_ref[...] += _swiglu_tile(xb_ref[...], w1_ref, w3_ref, w2_ref)

    @pl.when(f == pl.num_programs(1) - 1)
    def _():
        o_ref[...] = _layer_norm_rows(DEEPNORM_ALPHA * x_ref[...] + acc_ref[...], g_ref[...], b_ref[...])


def ffn_residual_ln(x, w1, w3, w2, g, b, tm, tf):
    m, d = x.shape
    ff = w1.shape[1]
    resident = dict(pipeline_mode=pl.Buffered(1)) if tf == ff else {}
    return pl.pallas_call(
        _ffn_ln_kernel,
        grid=(m // tm, ff // tf),
        in_specs=[pl.BlockSpec((tm, d), lambda i, f: (i, 0)),
                  pl.BlockSpec((d, tf), lambda i, f: (0, f), **resident),
                  pl.BlockSpec((d, tf), lambda i, f: (0, f), **resident),
                  pl.BlockSpec((tf, d), lambda i, f: (f, 0), **resident),
                  pl.BlockSpec((1, d), lambda i, f: (0, 0)), pl.BlockSpec((1, d), lambda i, f: (0, 0))],
        out_specs=pl.BlockSpec((tm, d), lambda i, f: (i, 0)),
        out_shape=jax.ShapeDtypeStruct((m, d), f32),
        scratch_shapes=[pltpu.VMEM((tm, d), MXU_DTYPE), pltpu.VMEM((tm, d), f32)],
        compiler_params=_params("parallel", "arbitrary"),
        name="ffn_residual_ln",
    )(x, w1, w3, w2, g.reshape(1, d), b.reshape(1, d))


ROUTE_IDS = N_EXPERTS
ROUTE_GATES = N_EXPERTS + 2


def _router_kernel(x_ref, w_ref, o_ref, *, n_experts):
    logits = jnp.dot(x_ref[...], w_ref[...], preferred_element_type=f32, precision=lax.Precision.HIGHEST)
    lane = lax.broadcasted_iota(i32, logits.shape, 1).astype(f32)
    logits = jnp.where(lane < n_experts, logits, -jnp.inf)
    v1 = jnp.max(logits, axis=1, keepdims=True)
    i1 = jnp.min(jnp.where(logits == v1, lane, float(LANES)), axis=1, keepdims=True)
    rest = jnp.where(lane == i1, -jnp.inf, logits)
    v2 = jnp.max(rest, axis=1, keepdims=True)
    i2 = jnp.min(jnp.where(rest == v2, lane, float(LANES)), axis=1, keepdims=True)
    e2 = jnp.exp(v2 - v1)
    g1 = 1.0 / (1.0 + e2)
    g2 = e2 / (1.0 + e2)
    out = jnp.where(lane == ROUTE_IDS, i1, 0.0) + jnp.where(lane == ROUTE_IDS + 1, i2, 0.0)
    out = out + jnp.where(lane == ROUTE_GATES, g1, 0.0) + jnp.where(lane == ROUTE_GATES + 1, g2, 0.0)
    o_ref[...] = out


def route_top2(x, router_w, tm=512):
    m, d = x.shape
    n_experts = router_w.shape[1]
    w = jnp.zeros((d, LANES), f32).at[:, :n_experts].set(router_w)
    return pl.pallas_call(
        functools.partial(_router_kernel, n_experts=n_experts),
        grid=(m // tm,),
        in_specs=[pl.BlockSpec((tm, d), lambda i: (i, 0)), pl.BlockSpec((d, LANES), lambda i: (0, 0))],
        out_specs=pl.BlockSpec((tm, LANES), lambda i: (i, 0)),
        out_shape=jax.ShapeDtypeStruct((m, LANES), f32),
        compiler_params=_params("parallel"),
        name="route_top2",
    )(x, w)


def _moe_ln_kernel(tok_ref, gs_ref, off_ref, x_ref, w1_ref, w3_ref, w2_ref, g_ref, b_ref, o_ref,
                   xg_ref, xb_ref, y_ref, *, rt):
    c = pl.program_id(0)
    e = pl.program_id(1)
    f = pl.program_id(2)
    last_f = pl.num_programs(2) - 1
    start = off_ref[0, e]
    count = off_ref[0, e + 1] - start
    n_tiles = (count + rt - 1) // rt

    @pl.when((c == 0) & (e == 0) & (f == 0))
    def _():
        xg_ref[...] = jnp.zeros_like(xg_ref)

    @pl.when((e == 0) & (f == 0))
    def _():
        o_ref[...] = jnp.zeros_like(o_ref)

    def row_loop(body):
        def group(j, carry):
            for u in range(ROW_UNROLL):
                body(j * ROW_UNROLL + u)
            return carry

        def single(r, carry):
            body(r)
            return carry
        lax.fori_loop(0, count // ROW_UNROLL, group, 0)
        lax.fori_loop((count // ROW_UNROLL) * ROW_UNROLL, count, single, 0)

    @pl.when(f == 0)
    def _():
        def gather(r):
            t = tok_ref[0, start + r]
            xg_ref[pl.ds(r, 1), :] = x_ref[pl.ds(t, 1), :]
        row_loop(gather)

        def cast(j, carry):
            rows = pl.ds(pl.multiple_of(j * rt, rt), rt)
            xb_ref[rows, :] = xg_ref[rows, :].astype(MXU_DTYPE)
            return carry
        lax.fori_loop(0, n_tiles, cast, 0)

    def tile(j, carry):
        rows = pl.ds(pl.multiple_of(j * rt, rt), rt)
        y = _swiglu_tile(xb_ref[rows, :], w1_ref, w3_ref, w2_ref)

        @pl.when(f == 0)
        def _():
            y_ref[rows, :] = y

        @pl.when(f != 0)
        def _():
            y_ref[rows, :] += y
        return carry
    lax.fori_loop(0, n_tiles, tile, 0)

    @pl.when(f == last_f)
    def _():
        def updated(r):
            t = tok_ref[0, start + r]
            return t, o_ref[pl.ds(t, 1), :] + gs_ref[0, start + r] * y_ref[pl.ds(r, 1), :]

        def group(j, carry):
            rows = [updated(j * ROW_UNROLL + u) for u in range(ROW_UNROLL)]
            for t, row in rows:
                o_ref[pl.ds(t, 1), :] = row
            return carry

        def single(r, carry):
            t, row = updated(r)
            o_ref[pl.ds(t, 1), :] = row
            return carry
        lax.fori_loop(0, count // ROW_UNROLL, group, 0)
        lax.fori_loop((count // ROW_UNROLL) * ROW_UNROLL, count, single, 0)

    @pl.when((e == pl.num_programs(1) - 1) & (f == last_f))
    def _():
        o_ref[...] = _layer_norm_rows(DEEPNORM_ALPHA * x_ref[...] + o_ref[...], g_ref[...], b_ref[...])


def moe_residual_ln(x, routes, w1, w3, w2, g, b, tm, tf, rt=128):
    m, d = x.shape
    n_experts, _, ff = w1.shape
    n_chunks = m // tm
    ids = routes[:, ROUTE_IDS:ROUTE_IDS + TOP_K].astype(i32).reshape(n_chunks, tm * TOP_K)
    gts = routes[:, ROUTE_GATES:ROUTE_GATES + TOP_K].reshape(n_chunks, tm * TOP_K)
    order = jnp.argsort(ids, axis=1, stable=True).astype(i32)
    tok = order // TOP_K
    gs = jnp.take_along_axis(gts, order, axis=1)
    counts = jnp.sum(ids[:, :, None] == jnp.arange(n_experts, dtype=i32)[None, None, :], axis=1, dtype=i32)
    offs = jnp.concatenate([jnp.zeros((n_chunks, 1), i32), jnp.cumsum(counts, axis=1, dtype=i32)], axis=1)
    smem = lambda width: pl.BlockSpec((None, 1, width), lambda c, e, f: (c, 0, 0), memory_space=pltpu.SMEM)
    return pl.pallas_call(
        functools.partial(_moe_ln_kernel, rt=rt),
        grid=(n_chunks, n_experts, ff // tf),
        in_specs=[smem(tm * TOP_K), smem(tm * TOP_K), smem(n_experts + 1),
                  pl.BlockSpec((tm, d), lambda c, e, f: (c, 0), pipeline_mode=pl.Buffered(1)),
                  pl.BlockSpec((None, d, tf), lambda c, e, f: (e, 0, f)),
                  pl.BlockSpec((None, d, tf), lambda c, e, f: (e, 0, f)),
                  pl.BlockSpec((None, tf, d), lambda c, e, f: (e, f, 0)),
                  pl.BlockSpec((1, d), lambda c, e, f: (0, 0)), pl.BlockSpec((1, d), lambda c, e, f: (0, 0))],
        out_specs=pl.BlockSpec((tm, d), lambda c, e, f: (c, 0), pipeline_mode=pl.Buffered(1)),
        out_shape=jax.ShapeDtypeStruct((m, d), f32),
        scratch_shapes=[pltpu.VMEM((tm, d), f32), pltpu.VMEM((tm, d), MXU_DTYPE), pltpu.VMEM((tm, d), f32)],
        compiler_params=_params("arbitrary", "arbitrary", "arbitrary"),
        name="moe_residual_ln",
    )(tok[:, None, :], gs[:, None, :], offs[:, None, :], x, w1, w3, w2, g.reshape(1, d), b.reshape(1, d))


def _flash_reset(m_ref, acc_ref):
    m_ref[...] = jnp.full(m_ref.shape, NEG, f32)
    acc_ref[...] = jnp.zeros(acc_ref.shape, f32)


def _flash_rows(q, kt, bias, m_ref, alpha_ref, p_ref, rows):
    s = _mm(q, kt)
    if bias is not None:
        s = s + bias
    m_prev = m_ref[rows, :]
    m_next = jnp.maximum(m_prev, jnp.max(s, axis=1, keepdims=True))
    alpha_ref[rows, :] = jnp.exp2(m_prev - m_next)
    m_ref[rows, :] = m_next
    p_ref[rows, :s.shape[1]] = jnp.exp2(s - _lane_tile(m_next, s.shape[1] // LANES)).astype(p_ref.dtype)


def _flash_accumulate(v_aug, alpha_ref, p_ref, acc_ref):
    alpha = _lane_tile(alpha_ref[...], acc_ref.shape[-1] // LANES)
    acc_ref[...] = alpha * acc_ref[...] + _mm(p_ref[:, :v_aug.shape[0]], v_aug)


def _skewed_chunks(n, score, fold, carry):
    def pair(j, carry):
        c = 2 * j
        fold(c - 1, 1)
        carry = score(c, 0, carry)
        fold(c, 0)
        return score(c + 1, 1, carry)

    def single(c, carry):
        fold(c - 1, 1)
        return score(c, 0, carry)

    carry = lax.fori_loop(0, n // 2, pair, carry)
    return lax.fori_loop(2 * (n // 2), n, single, carry)


def _fold_last(n, fold):
    for buf in range(2):
        pl.when((n - 1) % 2 == buf)(functools.partial(fold, n - 1, buf))


def _lane_tile(x, reps):
    return x if reps == 1 else jnp.concatenate([x] * reps, axis=1)


def _row_slices(n_rows):
    return [slice(r * Q_BLOCK, (r + 1) * Q_BLOCK) for r in range(n_rows // Q_BLOCK)]


def _dsa_kernel(qi_ref, wi_ref, kit_ref, qa_ref, kat_ref, va_ref, o_ref,
                skey_ref, wrep_ref, m_ref, alpha_ref, acc_ref, p_ref, qh_ref, *, topk, kc):
    qb = Q_BLOCK
    i = pl.program_id(1)
    q0 = i * qb
    n_chunks = (q0 + qb + kc - 1) // kc
    t_row = q0 + lax.broadcasted_iota(i32, (qb, kc), 0)
    col = lax.broadcasted_iota(i32, (qb, kc), 1)
    wi = wi_ref[...]
    qi = qi_ref[...]
    qis = [qi[:, h * IDX_DIM:(h + 1) * IDX_DIM] for h in range(IDX_HEADS)]
    qa = qa_ref[...]
    for h in range(A_HEADS):
        qh_ref[h * qb:(h + 1) * qb, :] = qa[:, h * HEAD_DIM:(h + 1) * HEAD_DIM]
    for h in range(IDX_HEADS):
        wrep_ref[h] = jnp.broadcast_to(wi[:, h:h + 1], (qb, kc))

    def score_body(c, carry):
        off = pl.multiple_of(c * kc, kc)
        kt = kit_ref[:, pl.ds(off, kc)]
        s = jnp.zeros((qb, kc), f32)
        for h in range(IDX_HEADS):
            s = s + wrep_ref[h] * jnp.maximum(_mm(qis[h], kt), 0.0)
        s = jnp.where(col + off <= t_row, s, NEG)
        bits = pltpu.bitcast(s, i32)
        skey_ref[:, pl.ds(off, kc)] = jnp.where(bits < 0, bits ^ 0x7FFFFFFF, bits)
        return carry

    n_strips = (q0 + qb + COUNT_STRIP - 1) // COUNT_STRIP
    lax.fori_loop(0, n_strips * (COUNT_STRIP // kc), score_body, 0)

    def count_ge(cand):
        def body(c, acc):
            off = pl.multiple_of(c * COUNT_STRIP, COUNT_STRIP)
            for j in range(COUNT_STRIP // LANES):
                acc = acc + jnp.where(skey_ref[:, pl.ds(off + j * LANES, LANES)] >= cand, 1.0, 0.0)
            return acc
        acc = lax.fori_loop(0, n_strips, body, jnp.zeros((qb, LANES), f32))
        return jnp.sum(acc, axis=1, keepdims=True)

    thr = jnp.where(count_ge(jnp.zeros((qb, 1), i32)) >= topk, 0, INT_MIN).astype(i32)

    def bisect(b, thr):
        cand = thr + jnp.left_shift(jnp.int32(1), 30 - b)
        return jnp.where(count_ge(cand) >= topk, cand, thr)

    thr = lax.fori_loop(0, 31, bisect, thr)
    quota = topk - count_ge(thr + 1)

    before = (lax.broadcasted_iota(i32, (kc, kc), 0) < lax.broadcasted_iota(i32, (kc, kc), 1))
    before = jnp.where(before, 1.0, 0.0).astype(MXU_DTYPE)
    _flash_reset(m_ref, acc_ref)
    alpha_ref[...] = jnp.zeros(alpha_ref.shape, f32)
    p_ref[...] = jnp.zeros(p_ref.shape, p_ref.dtype)
    heads = _row_slices(A_HEADS * qb)

    def fold(c, buf):
        off = pl.multiple_of(jnp.maximum(c, 0) * kc, kc)
        _flash_accumulate(va_ref[pl.ds(off, kc), :], alpha_ref.at[buf], p_ref.at[buf], acc_ref)

    def score(c, buf, ties_seen):
        off = pl.multiple_of(c * kc, kc)
        key = skey_ref[:, pl.ds(off, kc)]
        eq = key == thr
        eqf = jnp.where(eq, 1.0, 0.0)
        rank = ties_seen + _mm(eqf.astype(MXU_DTYPE), before)
        sel = ((key > thr) | (eq & (rank < quota))) & (col + off <= t_row)
        bias = jnp.where(sel, 0.0, NEG)
        kt = kat_ref[:, pl.ds(off, kc)]
        for rows in heads:
            _flash_rows(qh_ref[rows, :], kt, bias, m_ref, alpha_ref.at[buf], p_ref.at[buf], rows)
        return ties_seen + jnp.sum(eqf, axis=1, keepdims=True)

    _skewed_chunks(n_chunks, score, fold, jnp.zeros((qb, 1), f32))
    _fold_last(n_chunks, fold)
    acc = acc_ref[...]
    o = acc[:, :HEAD_DIM] / acc[:, HEAD_DIM:]
    for h in range(A_HEADS):
        o_ref[:, h * HEAD_DIM:(h + 1) * HEAD_DIM] = o[h * qb:(h + 1) * qb, :]


QPK_QA, QPK_QIDX, QPK_QB = 0, 2, 3
KPK_KA, KPK_KIDX, KPK_KSLC, KPK_KWIN = 0, 1, 2, 4
VPK_VA, VPK_VSLC, VPK_VWIN = 0, 1, 3
OUT_NSA = 2


def dsa_attention(wi, qpk, kpk, vpk, topk, kc=512):
    bsz, seq, _ = qpk.shape
    nb = seq // Q_BLOCK
    rows = A_HEADS * Q_BLOCK
    width = A_HEADS * HEAD_DIM
    kern = functools.partial(_dsa_kernel, topk=topk, kc=kc)
    return pl.pallas_call(
        kern,
        grid=(bsz, nb),
        in_specs=[pl.BlockSpec((None, Q_BLOCK, IDX_HEADS * IDX_DIM), lambda b, i: (b, i, QPK_QIDX)),
                  pl.BlockSpec((None, Q_BLOCK, IDX_HEADS), lambda b, i: (b, i, 0)),
                  pl.BlockSpec((None, IDX_DIM, seq), lambda b, i: (b, KPK_KIDX, 0)),
                  pl.BlockSpec((None, Q_BLOCK, width), lambda b, i: (b, i, QPK_QA)),
                  pl.BlockSpec((None, HEAD_DIM, seq), lambda b, i: (b, KPK_KA, 0)),
                  pl.BlockSpec((None, seq, 2 * HEAD_DIM), lambda b, i: (b, 0, VPK_VA))],
        out_specs=pl.BlockSpec((None, Q_BLOCK, width), lambda b, i: (b, i, 0)),
        out_shape=jax.ShapeDtypeStruct((bsz, seq, 2 * width), f32),
        scratch_shapes=[pltpu.VMEM((Q_BLOCK, seq), i32), pltpu.VMEM((IDX_HEADS, Q_BLOCK, kc), f32),
                        pltpu.VMEM((rows, LANES), f32), pltpu.VMEM((2, rows, LANES), f32),
                        pltpu.VMEM((rows, 2 * HEAD_DIM), f32), pltpu.VMEM((2, rows, kc), MXU_DTYPE),
                        pltpu.VMEM((rows, HEAD_DIM), MXU_DTYPE)],
        compiler_params=_params("parallel", "arbitrary"),
        name="dsa_attention",
    )(qpk, wi, kpk, qpk, kpk, vpk)


def _compress_kernel(x_ref, pe_ref, w1_ref, w2_ref, o_ref):
    half = (CMP_LEN // 2) * HEAD_DIM
    x = x_ref[...]
    first = _mm((x + pe_ref[0:1, :]).astype(MXU_DTYPE), w1_ref[0:half, :])
    second = _mm((x + pe_ref[1:2, :]).astype(MXU_DTYPE), w1_ref[half:2 * half, :])
    pre = first + pltpu.roll(second, shift=x.shape[0] - 1, axis=0)
    hid = pre * jax.nn.sigmoid(pre)
    o_ref[...] = _mm(hid.astype(MXU_DTYPE), w2_ref[...])


def nsa_compress(chunks, pe2, w1, w2):
    bsz, groups, nck, width = chunks.shape
    return pl.pallas_call(
        _compress_kernel,
        grid=(bsz, groups),
        in_specs=[pl.BlockSpec((None, None, nck, width), lambda b, g: (b, g, 0, 0)),
                  pl.BlockSpec((2, width), lambda b, g: (0, 0)),
                  pl.BlockSpec((2 * width, CMP_HIDDEN), lambda b, g: (0, 0)),
                  pl.BlockSpec((CMP_HIDDEN, HEAD_DIM), lambda b, g: (0, 0))],
        out_specs=pl.BlockSpec((None, None, nck, HEAD_DIM), lambda b, g: (b, g, 0, 0)),
        out_shape=jax.ShapeDtypeStruct((bsz, groups, nck, HEAD_DIM), f32),
        compiler_params=_params("parallel", "parallel"),
        name="nsa_compress",
    )(chunks, pe2, w1, w2)


def _nsa_kernel(qin_ref, graw_ref, kct_ref, vc_ref, kst_ref, vs_ref, kwt_ref, vw_ref, ovl_ref, _, o_ref,
                m_ref, alpha_ref, acc_ref, p_ref, pw_ref, q_ref, imp_ref, *, n_slc, n_sel, kc):
    qb = Q_BLOCK
    i = pl.program_id(2)
    q0 = i * qb
    n_cmp = kct_ref.shape[1]
    heads = _row_slices(B_REP * qb)
    qin = qin_ref[...]
    for r in range(B_REP):
        q_ref[r * qb:(r + 1) * qb, :] = qin[:, r * HEAD_DIM:(r + 1) * HEAD_DIM]

    def compressed(width):
        t_c = q0 + lax.broadcasted_iota(i32, (qb, width), 0)
        cmp_end = lax.broadcasted_iota(i32, (qb, width), 1) * CMP_STRIDE + (CMP_LEN - 1)
        vis = cmp_end <= t_c
        kct = kct_ref[:, :width]
        p_sum = jnp.zeros((qb, width), f32)
        for rows in heads:
            lc = jnp.where(vis, _mm(q_ref[rows, :], kct), NEG)
            ec = jnp.where(vis, jnp.exp2(lc - jnp.max(lc, axis=1, keepdims=True)), 0.0)
            den = jnp.sum(ec, axis=1, keepdims=True)
            p_c = ec / jnp.where(den > 0.0, den, 1.0)
            p_sum = p_sum + p_c
            acc_ref[rows, :HEAD_DIM] = _mm(p_c.astype(MXU_DTYPE), vc_ref[:width, :])
        imp_ref[...] = jnp.dot(p_sum, ovl_ref[:width, :], preferred_element_type=f32,
                               precision=lax.Precision.HIGHEST)

    if n_cmp % (2 * LANES) == 0:
        first_half_only = (q0 + qb) * 2 <= n_cmp * CMP_STRIDE
        pl.when(first_half_only)(functools.partial(compressed, n_cmp // 2))
        pl.when(jnp.logical_not(first_half_only))(functools.partial(compressed, n_cmp))
    else:
        compressed(n_cmp)
    o_c = acc_ref[:, :HEAD_DIM]

    imp = imp_ref[...].T
    t_q = q0 + lax.broadcasted_iota(i32, (LANES, qb), 1)
    blk = lax.broadcasted_iota(i32, (LANES, qb), 0)
    blk_t = t_q // SLC_LEN
    forced = (blk == 0) | (blk == blk_t) | (blk == blk_t - 1)
    imp = jnp.where(forced, FORCED_BOOST, imp)
    imp = jnp.where(blk * SLC_LEN <= t_q, imp, NEG)
    imp = jnp.where(blk < n_slc, imp, -jnp.inf)
    blk_f = blk.astype(f32)

    def pick(_, carry):
        imp, selm = carry
        best = jnp.max(imp, axis=0, keepdims=True)
        first = jnp.min(jnp.where(imp == best, blk_f, float(LANES)), axis=0, keepdims=True)
        hit = blk_f == first
        return jnp.where(hit, -jnp.inf, imp), jnp.where(hit, 1.0, selm)

    _, selm = lax.fori_loop(0, n_sel, pick, (imp, jnp.zeros((LANES, qb), f32)))
    selm = selm.T.astype(MXU_DTYPE)

    t_k = q0 + lax.broadcasted_iota(i32, (qb, kc), 0)
    col = lax.broadcasted_iota(i32, (qb, kc), 1)
    exp_row = lax.broadcasted_iota(i32, (LANES, kc), 0)
    exp_col = lax.broadcasted_iota(i32, (LANES, kc), 1)
    _flash_reset(m_ref, acc_ref)
    alpha_ref[...] = jnp.zeros(alpha_ref.shape, f32)
    p_ref[...] = jnp.zeros(p_ref.shape, p_ref.dtype)

    def fold(c, buf):
        off = pl.multiple_of(jnp.maximum(c, 0) * kc, kc)
        _flash_accumulate(vs_ref[pl.ds(off, kc), :], alpha_ref.at[buf], p_ref.at[buf], acc_ref)

    def score(c, buf, carry):
        off = pl.multiple_of(c * kc, kc)
        expand = jnp.where(exp_row == (exp_col + off) // SLC_LEN, 1.0, 0.0).astype(MXU_DTYPE)
        sel = (_mm(selm, expand) > 0.5) & (col + off <= t_k)
        bias = jnp.where(sel, 0.0, NEG)
        kt = kst_ref[:, pl.ds(off, kc)]
        for rows in heads:
            _flash_rows(q_ref[rows, :], kt, bias, m_ref, alpha_ref.at[buf], p_ref.at[buf], rows)
        return carry

    n_slc_chunks = (q0 + qb + kc - 1) // kc
    _skewed_chunks(n_slc_chunks, score, fold, 0)
    _fold_last(n_slc_chunks, fold)
    acc = acc_ref[...]
    o_s = acc[:, :HEAD_DIM] / acc[:, HEAD_DIM:]

    slab = WINDOW + qb
    w0 = pl.multiple_of(jnp.maximum(q0 - WINDOW, 0), qb)
    dist = (q0 + lax.broadcasted_iota(i32, (qb, slab), 0)) - (w0 + lax.broadcasted_iota(i32, (qb, slab), 1))
    bias = jnp.where((dist >= 0) & (dist < WINDOW), 0.0, NEG)
    kt = kwt_ref[:, pl.ds(w0, slab)]
    for rows in heads:
        s = _mm(q_ref[rows, :], kt) + bias
        pw_ref[rows, :] = jnp.exp2(s - jnp.max(s, axis=1, keepdims=True)).astype(pw_ref.dtype)
    acc = _mm(pw_ref[...], vw_ref[pl.ds(w0, slab), :])
    o_w = acc[:, :HEAD_DIM] / acc[:, HEAD_DIM:]

    gates = jax.nn.sigmoid(graw_ref[...])
    for r, rows in enumerate(heads):
        g_c, g_s, g_w = (gates[:, 3 * r + n:3 * r + n + 1] for n in range(3))
        o_ref[:, r * HEAD_DIM:(r + 1) * HEAD_DIM] = g_c * o_c[rows] + g_s * o_s[rows] + g_w * o_w[rows]


def nsa_attention(qpk, graw, kct, vc, kpk, vpk, ovl, out_ab, n_slc, n_sel, kc=512):
    bsz, seq, _ = qpk.shape
    groups = B_KV_GROUPS
    nb = seq // Q_BLOCK
    rows = B_REP * Q_BLOCK
    width = B_REP * HEAD_DIM
    n_cmp = kct.shape[-1]
    kern = functools.partial(_nsa_kernel, n_slc=n_slc, n_sel=n_sel, kc=kc)
    per_bg = lambda b, g, i: (b, g, 0, 0)
    return pl.pallas_call(
        kern,
        grid=(bsz, groups, nb),
        in_specs=[pl.BlockSpec((None, Q_BLOCK, width), lambda b, g, i: (b, i, QPK_QB + g)),
                  pl.BlockSpec((None, None, Q_BLOCK, B_REP * 3), lambda b, g, i: (b, g, i, 0)),
                  pl.BlockSpec((None, None, HEAD_DIM, n_cmp), per_bg),
                  pl.BlockSpec((None, None, n_cmp, HEAD_DIM), per_bg),
                  pl.BlockSpec((None, HEAD_DIM, seq), lambda b, g, i: (b, KPK_KSLC + g, 0)),
                  pl.BlockSpec((None, seq, 2 * HEAD_DIM), lambda b, g, i: (b, 0, VPK_VSLC + g)),
                  pl.BlockSpec((None, HEAD_DIM, seq), lambda b, g, i: (b, KPK_KWIN + g, 0)),
                  pl.BlockSpec((None, seq, 2 * HEAD_DIM), lambda b, g, i: (b, 0, VPK_VWIN + g)),
                  pl.BlockSpec((n_cmp, LANES), lambda b, g, i: (0, 0)),
                  pl.BlockSpec(memory_space=pl.ANY)],
        out_specs=pl.BlockSpec((None, Q_BLOCK, width), lambda b, g, i: (b, i, OUT_NSA + g)),
        out_shape=jax.ShapeDtypeStruct(out_ab.shape, f32),
        input_output_aliases={9: 0},
        scratch_shapes=[pltpu.VMEM((rows, LANES), f32), pltpu.VMEM((2, rows, LANES), f32),
                        pltpu.VMEM((rows, 2 * HEAD_DIM), f32), pltpu.VMEM((2, rows, kc), MXU_DTYPE),
                        pltpu.VMEM((rows, WINDOW + Q_BLOCK), MXU_DTYPE), pltpu.VMEM((rows, HEAD_DIM), MXU_DTYPE),
                        pltpu.VMEM((Q_BLOCK, LANES), f32)],
        compiler_params=_params("parallel", "parallel", "arbitrary"),
        name="nsa_attention",
    )(qpk, graw, kct, vc, kpk, vpk, kpk, vpk, ovl, out_ab)


def _diff_kernel(lam_ref, qin_ref, kt_ref, v_ref, g_ref, o_ref, m_ref, alpha_ref, acc_ref, p_ref, q_ref,
                 *, tq, kc, out_scale):
    i = pl.program_id(2)
    q0 = i * tq
    dv = v_ref.shape[-1] // 2
    groups = _row_slices(tq)
    qin = qin_ref[...]
    for half in range(2):
        q_ref[half] = qin[:, half * HEAD_DIM:(half + 1) * HEAD_DIM]
    for half in range(2):
        _flash_reset(m_ref.at[half], acc_ref.at[half])
    alpha_ref[...] = jnp.zeros(alpha_ref.shape, f32)
    p_ref[...] = jnp.zeros(p_ref.shape, p_ref.dtype)

    def fold(c, buf, sub=(0, kc // tq)):
        off = pl.multiple_of(jnp.maximum(c, 0) * kc + sub[0] * tq, tq)
        v = v_ref[pl.ds(off, sub[1] * tq), :]
        for half in range(2):
            _flash_accumulate(v, alpha_ref.at[buf, half], p_ref.at[buf, half], acc_ref.at[half])

    def score(c, buf, masked, sub=(0, kc // tq)):
        off = pl.multiple_of(c * kc + sub[0] * tq, tq)
        width = sub[1] * tq
        for half in range(2):
            kt = kt_ref[half * HEAD_DIM:(half + 1) * HEAD_DIM, pl.ds(off, width)]
            for r, rows in enumerate(groups):
                bias = None
                if masked:
                    key = off + lax.broadcasted_iota(i32, (Q_BLOCK, width), 1)
                    t = q0 + r * Q_BLOCK + lax.broadcasted_iota(i32, (Q_BLOCK, width), 0)
                    bias = jnp.where(key <= t, 0.0, NEG)
                _flash_rows(q_ref[half, rows, :], kt, bias, m_ref.at[half], alpha_ref.at[buf, half],
                            p_ref.at[buf, half], rows)

    def score_full(c, buf, carry):
        score(c, buf, False)
        return carry

    n_full = q0 // kc
    _skewed_chunks(n_full, score_full, fold, 0)

    own = (q0 - n_full * kc) // tq

    def finish(buf, own_piece):
        fold(n_full - 1, 1 - buf)
        if own_piece:
            score(n_full, buf, False, (0, own_piece))
            fold(n_full, buf, (0, own_piece))
            buf = 1 - buf
        score(n_full, buf, True, (own_piece, 1))
        fold(n_full, buf, (own_piece, 1))

    for buf in range(2):
        for own_piece in range(kc // tq):
            pl.when((n_full % 2 == buf) & (own == own_piece))(functools.partial(finish, buf, own_piece))
    a1 = acc_ref[0]
    a2 = acc_ref[1]
    o = a1[:, :dv] / a1[:, dv:] - lam_ref[0] * (a2[:, :dv] / a2[:, dv:])
    o = o * lax.rsqrt(jnp.mean(o * o, axis=-1, keepdims=True) + LN_EPS) * g_ref[...]
    o_ref[...] = o * out_scale


def diff_attention(lam, q, kt, v, subln_g, out_scale, tq=512, kc=1024):
    bsz, seq, width = q.shape
    dv = 2 * HEAD_DIM
    heads = width // dv
    tq = min(tq, seq)
    kc = min(kc, seq)
    assert kc % tq == 0 and seq % kc == 0
    kern = functools.partial(_diff_kernel, tq=tq, kc=kc, out_scale=out_scale)
    return pl.pallas_call(
        kern,
        grid=(bsz, heads, seq // tq),
        in_specs=[pl.BlockSpec(memory_space=pltpu.SMEM),
                  pl.BlockSpec((None, tq, dv), lambda b, h, i: (b, i, h)),
                  pl.BlockSpec((None, dv, seq), lambda b, h, i: (b, h, 0)),
                  pl.BlockSpec((None, seq, 2 * dv), lambda b, h, i: (b, 0, h)),
                  pl.BlockSpec((1, dv), lambda b, h, i: (0, 0))],
        out_specs=pl.BlockSpec((None, tq, dv), lambda b, h, i: (b, i, h)),
        out_shape=jax.ShapeDtypeStruct((bsz, seq, heads * dv), f32),
        scratch_shapes=[pltpu.VMEM((2, tq, LANES), f32), pltpu.VMEM((2, 2, tq, LANES), f32),
                        pltpu.VMEM((2, tq, 2 * dv), f32), pltpu.VMEM((2, 2, tq, kc), MXU_DTYPE),
                        pltpu.VMEM((2, tq, HEAD_DIM), MXU_DTYPE)],
        compiler_params=_params("parallel", "parallel", "arbitrary"),
        name="diff_attention",
    )(lam, q, kt, v, subln_g.reshape(1, dv))


def _rope_tables(positions):
    inv_freq = ROPE_THETA ** (-jnp.arange(0, ROT_DIM, 2, dtype=f32) / ROT_DIM)
    ang = positions.astype(f32)[..., None] * inv_freq
    return jnp.cos(ang), jnp.sin(ang)


def _apply_rope(x, cos, sin):
    shape = cos.shape[:2] + (1,) * (x.ndim - 3) + cos.shape[-1:]
    c = cos.reshape(shape)
    s = sin.reshape(shape)
    half = ROT_DIM // 2
    x1, x2 = x[..., :half], x[..., half:ROT_DIM]
    return jnp.concatenate([x1 * c - x2 * s, x2 * c + x1 * s, x[..., ROT_DIM:]], axis=-1)


def _rope_coefficients(cos, sin):
    half = ROT_DIM // 2
    spread = np.zeros((ROT_DIM, 3 * LANES), np.float32)
    offset = np.zeros((3 * LANES,), np.float32)
    for lane in range(LANES):
        d = lane % HEAD_DIM
        if d >= ROT_DIM:
            offset[lane] = 1.0
        elif d < half:
            spread[d, lane], spread[half + d, LANES + lane] = 1.0, -1.0
        else:
            spread[d - half, lane], spread[d, 2 * LANES + lane] = 1.0, 1.0
    table = jnp.concatenate([cos, sin], axis=-1)
    return jnp.dot(table, spread, precision=lax.Precision.HIGHEST) + offset


def _ab_weight_columns(w_in):
    widths = dict(AB_LAYOUT)
    starts = dict(zip(widths, np.cumsum([0] + [w for _, w in AB_LAYOUT[:-1]]).tolist()))
    cols = lambda names: np.concatenate([np.arange(starts[n], starts[n] + widths[n]) for n in names])
    first = ("q_a", "q_idx", "q_b", "k_a", "k_idx", "k_slc", "k_win", "v_a", "v_slc", "v_win", "gate_b", "w_idx")
    n_first = sum(widths[n] for n in first)
    pad = AB_CMP_TILE * LANES - n_first
    assert 0 <= pad < HEAD_DIM
    w = jnp.concatenate([w_in[:, cols(first)], jnp.zeros((w_in.shape[0], pad), w_in.dtype),
                         w_in[:, cols(("k_cmp", "v_cmp"))]], axis=1)
    assert w.shape[1] == AB_TILES * LANES
    return w.astype(MXU_DTYPE)


def _overlap_matrix(n_cmp_rows, n_slc):
    c_start = np.arange(n_cmp_rows) * CMP_STRIDE
    s_start = np.arange(LANES) * SLC_LEN
    ovl = (c_start[:, None] < s_start[None, :] + SLC_LEN) & (c_start[:, None] + CMP_LEN > s_start[None, :])
    ovl = ovl & (np.arange(LANES)[None, :] < n_slc)
    return jnp.asarray(ovl.astype(np.float32))


def _ab_mixer(x2, bsz, seq, positions, coef, w_in, pe_k, pe_v, ck1, ck2, cv1, cv2):
    qpk, kpk, vpk, misc, cmp = project_pack_ab(x2.reshape(bsz, seq, -1), _ab_weight_columns(w_in), coef)
    gate_w = B_HEADS * 3
    g_b = misc[..., HEAD_DIM:HEAD_DIM + gate_w]
    w_idx = misc[..., HEAD_DIM + gate_w:HEAD_DIM + gate_w + IDX_HEADS] * (IDX_HEADS * IDX_DIM) ** -0.5
    k_cmp, v_cmp = cmp[..., :LANES], cmp[..., LANES:]

    out_ab = dsa_attention(w_idx, qpk, kpk, vpk, topk=min(DSA_TOPK, seq // 4))

    groups = B_KV_GROUPS
    n_rows = seq // CMP_STRIDE
    n_cmp = (seq - CMP_LEN) // CMP_STRIDE + 1
    assert n_cmp == n_rows - 1
    n_slc = seq // SLC_LEN
    assert n_slc <= LANES

    def chunked(kv):
        kv = kv.reshape(bsz, n_rows, CMP_STRIDE, groups, HEAD_DIM).transpose(0, 3, 1, 2, 4)
        return kv.reshape(bsz, groups, n_rows, CMP_STRIDE * HEAD_DIM)

    def pe_halves(pe):
        return pe.reshape(2, CMP_STRIDE * HEAD_DIM)

    k_c = nsa_compress(chunked(k_cmp), pe_halves(pe_k), ck1.astype(MXU_DTYPE), ck2.astype(MXU_DTYPE))
    v_c = nsa_compress(chunked(v_cmp), pe_halves(pe_v), cv1.astype(MXU_DTYPE), cv2.astype(MXU_DTYPE))
    cmp_end = jnp.minimum(jnp.arange(n_rows) * CMP_STRIDE + CMP_LEN - 1, seq - 1)
    cos_c, sin_c = _rope_tables(positions[:, cmp_end])
    k_c = _apply_rope(k_c.transpose(0, 2, 1, 3), cos_c, sin_c)
    k_c = k_c.transpose(0, 2, 3, 1).astype(MXU_DTYPE)
    v_c = v_c.astype(MXU_DTYPE)

    graw = g_b.reshape(bsz, seq, groups, B_REP * 3).transpose(0, 2, 1, 3)
    out_ab = nsa_attention(qpk, graw, k_c, v_c, kpk, vpk, _overlap_matrix(n_rows, n_slc), out_ab,
                           n_slc=n_slc, n_sel=min(SLC_TOPN, n_slc))
    return out_ab.reshape(bsz * seq, (A_HEADS + B_HEADS) * HEAD_DIM)


def _diff_mixer(x2, bsz, seq, coef, w_in, lq1, lk1, lq2, lk2, subln_g, lam_init):
    q, kt, v = project_pack_c(x2.reshape(bsz, seq, -1), w_in.astype(MXU_DTYPE), coef)
    lam =(jnp.exp(jnp.sum(lq1 * lk1)) - jnp.exp(jnp.sum(lq2 * lk2)) + lam_init).reshape(1).astype(f32)
    o = diff_attention(lam, q, kt, v, subln_g, 1.0 - lam_init)
    return o.reshape(bsz * seq, C_HEADS * 2 * HEAD_DIM)


def kernel(x, positions, ab_w_in, cmp_pe_k, cmp_pe_v, cmp_k_w1, cmp_k_w2, cmp_v_w1, cmp_v_w2, ab_w_out, ln_ab_g, ln_ab_b, ffn_w1, ffn_w3, ffn_w2, ln_ffn_g, ln_ffn_b, c_w_in, lambda_q1, lambda_k1, lambda_q2, lambda_k2, c_subln_g, c_w_out, ln_c_g, ln_c_b, router_w, moe_w1, moe_w3, moe_w2, ln_moe_g, ln_moe_b):
    bsz, seq, d = x.shape
    assert seq % COUNT_STRIP == 0 and seq >= WINDOW + Q_BLOCK and d == D_MODEL
    coef = _rope_coefficients(*_rope_tables(positions))
    x2 = x.reshape(bsz * seq, d)
    for layer in range(DEPTH):
        i = layer // 2
        if layer % 2 == 0:
            o = _ab_mixer(x2, bsz, seq, positions, coef, ab_w_in[i], cmp_pe_k[i], cmp_pe_v[i],
                          cmp_k_w1[i], cmp_k_w2[i], cmp_v_w1[i], cmp_v_w2[i])
            x2 = project_residual_ln(o, ab_w_out[i].astype(MXU_DTYPE), x2, ln_ab_g[i], ln_ab_b[i])
            x2 = ffn_residual_ln(x2, ffn_w1[i].astype(MXU_DTYPE), ffn_w3[i].astype(MXU_DTYPE),
                                 ffn_w2[i].astype(MXU_DTYPE), ln_ffn_g[i], ln_ffn_b[i], tm=512, tf=ffn_w1.shape[-1])
        else:
            lam_init = 0.8 - 0.6 * math.exp(-0.3 * layer)
            o = _diff_mixer(x2, bsz, seq, coef, c_w_in[i], lambda_q1[i], lambda_k1[i], lambda_q2[i],
                            lambda_k2[i], c_subln_g[i], lam_init)
            x2 = project_residual_ln(o, c_w_out[i].astype(MXU_DTYPE), x2, ln_c_g[i], ln_c_b[i])
            routes = route_top2(x2, router_w[i])
            x2 = moe_residual_ln(x2, routes, moe_w1[i].astype(MXU_DTYPE), moe_w3[i].astype(MXU_DTYPE),
                                 moe_w2[i].astype(MXU_DTYPE), ln_moe_g[i], ln_moe_b[i], tm=min(2048, bsz * seq // 2), tf=896, rt=128)
    return x2.reshape(bsz, seq, d)
```

```python
import functools
import math

import numpy as np
import jax
import jax.numpy as jnp
from jax import lax
from jax.experimental import pallas as pl
from jax.experimental.pallas import tpu as pltpu

f32 = jnp.float32
i32 = jnp.int32
MXU_DTYPE = jnp.bfloat16
VMEM_LIMIT_BYTES = 56 * 1024 * 1024
LANES = 128

D_MODEL = 1024
DEPTH = 2
HEAD_DIM = 64
ROT_DIM = HEAD_DIM // 4
ROPE_THETA = 500000.0
Q_BLOCK = 128
NEG = -1e30
LN_EPS = 1e-5
A_HEADS = 8
IDX_HEADS = 4
IDX_DIM = 64
DSA_TOPK = 256
B_HEADS = 8
B_KV_GROUPS = 2
B_REP = B_HEADS // B_KV_GROUPS
CMP_LEN = 32
CMP_STRIDE = 16
CMP_HIDDEN = 128
SLC_LEN = 64
SLC_TOPN = 16
WINDOW = 512
FORCED_BOOST = 1e6
C_HEADS = 8
N_EXPERTS = 8
TOP_K = 2
DEEPNORM_ALPHA = (2 * DEPTH) ** 0.25
QK_SCALE = HEAD_DIM ** -0.5 * math.log2(math.e)
INT_MIN = -(2 ** 31)
COUNT_STRIP = 512
ROW_UNROLL = 4

AB_LAYOUT = (
    ("q_a", A_HEADS * HEAD_DIM), ("k_a", HEAD_DIM), ("v_a", HEAD_DIM),
    ("q_idx", IDX_HEADS * IDX_DIM), ("k_idx", IDX_DIM), ("w_idx", IDX_HEADS),
    ("q_b", B_HEADS * HEAD_DIM),
    ("k_cmp", B_KV_GROUPS * HEAD_DIM), ("v_cmp", B_KV_GROUPS * HEAD_DIM),
    ("k_slc", B_KV_GROUPS * HEAD_DIM), ("v_slc", B_KV_GROUPS * HEAD_DIM),
    ("k_win", B_KV_GROUPS * HEAD_DIM), ("v_win", B_KV_GROUPS * HEAD_DIM),
    ("gate_b", 3 * B_HEADS),
)


def _params(*sem):
    return pltpu.CompilerParams(dimension_semantics=sem, vmem_limit_bytes=VMEM_LIMIT_BYTES)


def _mm(a, b):
    return jnp.dot(a, b, preferred_element_type=f32)


def _layer_norm_rows(y, g, b):
    mu = jnp.mean(y, axis=-1, keepdims=True)
    yc = y - mu
    var = jnp.mean(yc * yc, axis=-1, keepdims=True)
    return yc * lax.rsqrt(var + LN_EPS) * g + b


def _rope_tile(x, keep, hi, lo):
    return x * keep + pltpu.roll(x, LANES - ROT_DIM // 2, 1) * hi + pltpu.roll(x, ROT_DIM // 2, 1) * lo


def _tile(h, j):
    return h[:, j * LANES:(j + 1) * LANES]


AB_Q_TILES = 10
AB_QIDX_TILES = (4, 5)
AB_K_TILES = 3
AB_V_HEADS = 5
AB_MISC_TILE = 15
AB_CMP_TILE = 16
AB_TILES = 18


def _proj_pack_ab_kernel(x_ref, w_ref, coef_ref, q_ref, kt_ref, v_ref, misc_ref, cmp_ref):
    h = _mm(x_ref[...].astype(MXU_DTYPE), w_ref[...])
    keep, hi, lo = (coef_ref[:, j * LANES:(j + 1) * LANES] for j in range(3))
    for j in range(AB_Q_TILES):
        t = _rope_tile(_tile(h, j), keep, hi, lo)
        q_ref[:, j * LANES:(j + 1) * LANES] = (t if j in AB_QIDX_TILES else t * QK_SCALE).astype(q_ref.dtype)
    k = jnp.concatenate([_rope_tile(_tile(h, AB_Q_TILES + j), keep, hi, lo) for j in range(AB_K_TILES)], axis=1)
    kt_ref[...] = k.T.astype(kt_ref.dtype)
    low = lax.broadcasted_iota(i32, keep.shape, 1) < HEAD_DIM
    for j in range(AB_V_HEADS):
        t = _tile(h, AB_Q_TILES + AB_K_TILES + j // 2)
        t = pltpu.roll(t, HEAD_DIM, 1) if j % 2 else t
        v_ref[:, j * LANES:(j + 1) * LANES] = jnp.where(low, t, 1.0).astype(v_ref.dtype)
    misc_ref[...] = _tile(h, AB_MISC_TILE)
    cmp_ref[...] = h[:, AB_CMP_TILE * LANES:AB_TILES * LANES]


def project_pack_ab(x, w, coef, tm=512):
    bsz, seq, d = x.shape
    row = lambda width: pl.BlockSpec((None, tm, width), lambda b, i: (b, i, 0))
    kt_rows = AB_K_TILES * LANES
    return pl.pallas_call(
        _proj_pack_ab_kernel,
        grid=(bsz, seq // tm),
        in_specs=[row(d), pl.BlockSpec((d, AB_TILES * LANES), lambda b, i: (0, 0)), row(3 * LANES)],
        out_specs=[row(AB_Q_TILES * LANES), pl.BlockSpec((None, kt_rows, tm), lambda b, i: (b, 0, i)),
                   row(AB_V_HEADS * LANES), row(LANES), row(2 * LANES)],
        out_shape=[jax.ShapeDtypeStruct((bsz, seq, AB_Q_TILES * LANES), MXU_DTYPE),
                   jax.ShapeDtypeStruct((bsz, kt_rows, seq), MXU_DTYPE),
                   jax.ShapeDtypeStruct((bsz, seq, AB_V_HEADS * LANES), MXU_DTYPE),
                   jax.ShapeDtypeStruct((bsz, seq, LANES), f32),
                   jax.ShapeDtypeStruct((bsz, seq, 2 * LANES), f32)],
        compiler_params=_params("parallel", "parallel"),
        name="project_pack_ab",
    )(x, w, coef)


def _proj_pack_c_kernel(x_ref, w_ref, coef_ref, q_ref, kt_ref, v_ref):
    h = _mm(x_ref[...].astype(MXU_DTYPE), w_ref[...])
    keep, hi, lo = (coef_ref[:, j * LANES:(j + 1) * LANES] for j in range(3))
    n = q_ref.shape[1] // LANES
    for j in range(n):
        q_ref[:, j * LANES:(j + 1) * LANES] = (_rope_tile(_tile(h, j), keep, hi, lo) * QK_SCALE).astype(q_ref.dtype)
    k = jnp.concatenate([_rope_tile(_tile(h, n + j), keep, hi, lo) for j in range(n)], axis=1)
    kt_ref[...] = k.T.astype(kt_ref.dtype)
    ones = jnp.ones(keep.shape, v_ref.dtype)
    for j in range(n):
        v_ref[:, 2 * j * LANES:(2 * j + 1) * LANES] = _tile(h, 2 * n + j).astype(v_ref.dtype)
        v_ref[:, (2 * j + 1) * LANES:(2 * j + 2) * LANES] = ones


def project_pack_c(x, w, coef, tm=512):
    bsz, seq, d = x.shape
    width = w.shape[1] // 3
    row = lambda cols: pl.BlockSpec((None, tm, cols), lambda b, i: (b, i, 0))
    return pl.pallas_call(
        _proj_pack_c_kernel,
        grid=(bsz, seq // tm),
        in_specs=[row(d), pl.BlockSpec((d, 3 * width), lambda b, i: (0, 0)), row(3 * LANES)],
        out_specs=[row(width), pl.BlockSpec((None, width, tm), lambda b, i: (b, 0, i)), row(2 * width)],
        out_shape=[jax.ShapeDtypeStruct((bsz, seq, width), MXU_DTYPE),
                   jax.ShapeDtypeStruct((bsz, width, seq), MXU_DTYPE),
                   jax.ShapeDtypeStruct((bsz, seq, 2 * width), MXU_DTYPE)],
        compiler_params=_params("parallel", "parallel"),
        name="project_pack_c",
    )(x, w, coef)


def _proj_ln_kernel(a_ref, w_ref, res_ref, g_ref, b_ref, o_ref):
    h = _mm(a_ref[...].astype(MXU_DTYPE), w_ref[...])
    o_ref[...] = _layer_norm_rows(DEEPNORM_ALPHA * res_ref[...] + h, g_ref[...], b_ref[...])


def project_residual_ln(a, w, res, g, b, tm=512):
    m, k = a.shape
    n = w.shape[1]
    return pl.pallas_call(
        _proj_ln_kernel,
        grid=(m // tm,),
        in_specs=[pl.BlockSpec((tm, k), lambda i: (i, 0)), pl.BlockSpec((k, n), lambda i: (0, 0)),
                  pl.BlockSpec((tm, n), lambda i: (i, 0)),
                  pl.BlockSpec((1, n), lambda i: (0, 0)), pl.BlockSpec((1, n), lambda i: (0, 0))],
        out_specs=pl.BlockSpec((tm, n), lambda i: (i, 0)),
        out_shape=jax.ShapeDtypeStruct((m, n), f32),
        compiler_params=_params("parallel"),
        name="project_residual_ln",
    )(a, w, res, g.reshape(1, n), b.reshape(1, n))


def _swiglu_tile(xb, w1_ref, w3_ref, w2_ref):
    a = _mm(xb, w1_ref[...])
    h = (a * jax.nn.sigmoid(a)) * _mm(xb, w3_ref[...])
    return _mm(h.astype(MXU_DTYPE), w2_ref[...])


def _ffn_ln_kernel(x_ref, w1_ref, w3_ref, w2_ref, g_ref, b_ref, o_ref, xb_ref, acc_ref):
    f = pl.program_id(1)

    @pl.when(f == 0)
    def _():
        xb_ref[...] = x_ref[...].astype(MXU_DTYPE)
        acc_ref[...] = jnp.zeros_like(acc_ref)

    acc_ref[...] += _swiglu_tile(xb_ref[...], w1_ref, w3_ref, w2_ref)

    @pl.when(f == pl.num_programs(1) - 1)
    def _():
        o_ref[...] = _layer_norm_rows(DEEPNORM_ALPHA * x_ref[...] + acc_ref[...], g_ref[...], b_ref[...])


def ffn_residual_ln(x, w1, w3, w2, g, b, tm, tf):
    m, d = x.shape
    ff = w1.shape[1]
    resident = dict(pipeline_mode=pl.Buffered(1)) if tf == ff else {}
    return pl.pallas_call(
        _ffn_ln_kernel,
        grid=(m // tm, ff // tf),
        in_specs=[pl.BlockSpec((tm, d), lambda i, f: (i, 0)),
                  pl.BlockSpec((d, tf), lambda i, f: (0, f), **resident),
                  pl.BlockSpec((d, tf), lambda i, f: (0, f), **resident),
                  pl.BlockSpec((tf, d), lambda i, f: (f, 0), **resident),
                  pl.BlockSpec((1, d), lambda i, f: (0, 0)), pl.BlockSpec((1, d), lambda i, f: (0, 0))],
        out_specs=pl.BlockSpec((tm, d), lambda i, f: (i, 0)),
        out_shape=jax.ShapeDtypeStruct((m, d), f32),
        scratch_shapes=[pltpu.VMEM((tm, d), MXU_DTYPE), pltpu.VMEM((tm, d), f32)],
        compiler_params=_params("parallel", "arbitrary"),
        name="ffn_residual_ln",
    )(x, w1, w3, w2, g.reshape(1, d), b.reshape(1, d))


ROUTE_IDS = N_EXPERTS
ROUTE_GATES = N_EXPERTS + 2


def _router_kernel(x_ref, w_ref, o_ref, *, n_experts):
    logits = jnp.dot(x_ref[...], w_ref[...], preferred_element_type=f32, precision=lax.Precision.HIGHEST)
    lane = lax.broadcasted_iota(i32, logits.shape, 1).astype(f32)
    logits = jnp.where(lane < n_experts, logits, -jnp.inf)
    v1 = jnp.max(logits, axis=1, keepdims=True)
    i1 = jnp.min(jnp.where(logits == v1, lane, float(LANES)), axis=1, keepdims=True)
    rest = jnp.where(lane == i1, -jnp.inf, logits)
    v2 = jnp.max(rest, axis=1, keepdims=True)
    i2 = jnp.min(jnp.where(rest == v2, lane, float(LANES)), axis=1, keepdims=True)
    e2 = jnp.exp(v2 - v1)
    g1 = 1.0 / (1.0 + e2)
    g2 = e2 / (1.0 + e2)
    out = jnp.where(lane == ROUTE_IDS, i1, 0.0) + jnp.where(lane == ROUTE_IDS + 1, i2, 0.0)
    out = out + jnp.where(lane == ROUTE_GATES, g1, 0.0) + jnp.where(lane == ROUTE_GATES + 1, g2, 0.0)
    o_ref[...] = out


def route_top2(x, router_w, tm=512):
    m, d = x.shape
    n_experts = router_w.shape[1]
    w = jnp.zeros((d, LANES), f32).at[:, :n_experts].set(router_w)
    return pl.pallas_call(
        functools.partial(_router_kernel, n_experts=n_experts),
        grid=(m // tm,),
        in_specs=[pl.BlockSpec((tm, d), lambda i: (i, 0)), pl.BlockSpec((d, LANES), lambda i: (0, 0))],
        out_specs=pl.BlockSpec((tm, LANES), lambda i: (i, 0)),
        out_shape=jax.ShapeDtypeStruct((m, LANES), f32),
        compiler_params=_params("parallel"),
        name="route_top2",
    )(x, w)


def _moe_ln_kernel(tok_ref, gs_ref, off_ref, x_ref, w1_ref, w3_ref, w2_ref, g_ref, b_ref, o_ref,
                   xg_ref, xb_ref, y_ref, *, rt):
    c = pl.program_id(0)
    e = pl.program_id(1)
    f = pl.program_id(2)
    last_f = pl.num_programs(2) - 1
    start = off_ref[0, e]
    count = off_ref[0, e + 1] - start
    n_tiles = (count + rt - 1) // rt

    @pl.when((c == 0) & (e == 0) & (f == 0))
    def _():
        xg_ref[...] = jnp.zeros_like(xg_ref)

    @pl.when((e == 0) & (f == 0))
    def _():
        o_ref[...] = jnp.zeros_like(o_ref)

    def row_loop(body):
        def group(j, carry):
            for u in range(ROW_UNROLL):
                body(j * ROW_UNROLL + u)
            return carry

        def single(r, carry):
            body(r)
            return carry
        lax.fori_loop(0, count // ROW_UNROLL, group, 0)
        lax.fori_loop((count // ROW_UNROLL) * ROW_UNROLL, count, single, 0)

    @pl.when(f == 0)
    def _():
        def gather(r):
            t = tok_ref[0, start + r]
            xg_ref[pl.ds(r, 1), :] = x_ref[pl.ds(t, 1), :]
        row_loop(gather)

        def cast(j, carry):
            rows = pl.ds(pl.multiple_of(j * rt, rt), rt)
            xb_ref[rows, :] = xg_ref[rows, :].astype(MXU_DTYPE)
            return carry
        lax.fori_loop(0, n_tiles, cast, 0)

    def tile(j, carry):
        rows = pl.ds(pl.multiple_of(j * rt, rt), rt)
        y = _swiglu_tile(xb_ref[rows, :], w1_ref, w3_ref, w2_ref)

        @pl.when(f == 0)
        def _():
            y_ref[rows, :] = y

        @pl.when(f != 0)
        def _():
            y_ref[rows, :] += y
        return carry
    lax.fori_loop(0, n_tiles, tile, 0)

    @pl.when(f == last_f)
    def _():
        def updated(r):
            t = tok_ref[0, start + r]
            return t, o_ref[pl.ds(t, 1), :] + gs_ref[0, start + r] * y_ref[pl.ds(r, 1), :]

        def group(j, carry):
            rows = [updated(j * ROW_UNROLL + u) for u in range(ROW_UNROLL)]
            for t, row in rows:
                o_ref[pl.ds(t, 1), :] = row
            return carry

        def single(r, carry):
            t, row = updated(r)
            o_ref[pl.ds(t, 1), :] = row
            return carry
        lax.fori_loop(0, count // ROW_UNROLL, group, 0)
        lax.fori_loop((count // ROW_UNROLL) * ROW_UNROLL, count, single, 0)

    @pl.when((e == pl.num_programs(1) - 1) & (f == last_f))
    def _():
        o_ref[...] = _layer_norm_rows(DEEPNORM_ALPHA * x_ref[...] + o_ref[...], g_ref[...], b_ref[...])


def moe_residual_ln(x, routes, w1, w3, w2, g, b, tm, tf, rt=128):
    m, d = x.shape
    n_experts, _, ff = w1.shape
    n_chunks = m // tm
    ids = routes[:, ROUTE_IDS:ROUTE_IDS + TOP_K].astype(i32).reshape(n_chunks, tm * TOP_K)
    gts = routes[:, ROUTE_GATES:ROUTE_GATES + TOP_K].reshape(n_chunks, tm * TOP_K)
    order = jnp.argsort(ids, axis=1, stable=True).astype(i32)
    tok = order // TOP_K
    gs = jnp.take_along_axis(gts, order, axis=1)
    counts = jnp.sum(ids[:, :, None] == jnp.arange(n_experts, dtype=i32)[None, None, :], axis=1, dtype=i32)
    offs = jnp.concatenate([jnp.zeros((n_chunks, 1), i32), jnp.cumsum(counts, axis=1, dtype=i32)], axis=1)
    smem = lambda width: pl.BlockSpec((None, 1, width), lambda c, e, f: (c, 0, 0), memory_space=pltpu.SMEM)
    return pl.pallas_call(
        functools.partial(_moe_ln_kernel, rt=rt),
        grid=(n_chunks, n_experts, ff // tf),
        in_specs=[smem(tm * TOP_K), smem(tm * TOP_K), smem(n_experts + 1),
                  pl.BlockSpec((tm, d), lambda c, e, f: (c, 0), pipeline_mode=pl.Buffered(1)),
                  pl.BlockSpec((None, d, tf), lambda c, e, f: (e, 0, f)),
                  pl.BlockSpec((None, d, tf), lambda c, e, f: (e, 0, f)),
                  pl.BlockSpec((None, tf, d), lambda c, e, f: (e, f, 0)),
                  pl.BlockSpec((1, d), lambda c, e, f: (0, 0)), pl.BlockSpec((1, d), lambda c, e, f: (0, 0))],
        out_specs=pl.BlockSpec((tm, d), lambda c, e, f: (c, 0), pipeline_mode=pl.Buffered(1)),
        out_shape=jax.ShapeDtypeStruct((m, d), f32),
        scratch_shapes=[pltpu.VMEM((tm, d), f32), pltpu.VMEM((tm, d), MXU_DTYPE), pltpu.VMEM((tm, d), f32)],
        compiler_params=_params("arbitrary", "arbitrary", "arbitrary"),
        name="moe_residual_ln",
    )(tok[:, None, :], gs[:, None, :], offs[:, None, :], x, w1, w3, w2, g.reshape(1, d), b.reshape(1, d))


def _flash_reset(m_ref, acc_ref):
    m_ref[...] = jnp.full(m_ref.shape, NEG, f32)
    acc_ref[...] = jnp.zeros(acc_ref.shape, f32)


def _flash_rows(q, kt, bias, m_ref, alpha_ref, p_ref, rows):
    s = _mm(q, kt)
    if bias is not None:
        s = s + bias
    m_prev = m_ref[rows, :]
    m_next = jnp.maximum(m_prev, jnp.max(s, axis=1, keepdims=True))
    alpha_ref[rows, :] = jnp.exp2(m_prev - m_next)
    m_ref[rows, :] = m_next
    p_ref[rows, :s.shape[1]] = jnp.exp2(s - _lane_tile(m_next, s.shape[1] // LANES)).astype(p_ref.dtype)


def _flash_accumulate(v_aug, alpha_ref, p_ref, acc_ref):
    alpha = _lane_tile(alpha_ref[...], acc_ref.shape[-1] // LANES)
    acc_ref[...] = alpha * acc_ref[...] + _mm(p_ref[:, :v_aug.shape[0]], v_aug)


def _skewed_chunks(n, score, fold, carry):
    def pair(j, carry):
        c = 2 * j
        fold(c - 1, 1)
        carry = score(c, 0, carry)
        fold(c, 0)
        return score(c + 1, 1, carry)

    def single(c, carry):
        fold(c - 1, 1)
        return score(c, 0, carry)

    carry = lax.fori_loop(0, n // 2, pair, carry)
    return lax.fori_loop(2 * (n // 2), n, single, carry)


def _fold_last(n, fold):
    for buf in range(2):
        pl.when((n - 1) % 2 == buf)(functools.partial(fold, n - 1, buf))


def _lane_tile(x, reps):
    return x if reps == 1 else jnp.concatenate([x] * reps, axis=1)


def _row_slices(n_rows):
    return [slice(r * Q_BLOCK, (r + 1) * Q_BLOCK) for r in range(n_rows // Q_BLOCK)]


def _dsa_kernel(qi_ref, wi_ref, kit_ref, qa_ref, kat_ref, va_ref, o_ref,
                skey_ref, wrep_ref, m_ref, alpha_ref, acc_ref, p_ref, qh_ref, *, topk, kc):
    qb = Q_BLOCK
    i = pl.program_id(1)
    q0 = i * qb
    n_chunks = (q0 + qb + kc - 1) // kc
    t_row = q0 + lax.broadcasted_iota(i32, (qb, kc), 0)
    col = lax.broadcasted_iota(i32, (qb, kc), 1)
    wi = wi_ref[...]
    qi = qi_ref[...]
    qis = [qi[:, h * IDX_DIM:(h + 1) * IDX_DIM] for h in range(IDX_HEADS)]
    qa = qa_ref[...]
    for h in range(A_HEADS):
        qh_ref[h * qb:(h + 1) * qb, :] = qa[:, h * HEAD_DIM:(h + 1) * HEAD_DIM]
    for h in range(IDX_HEADS):
        wrep_ref[h] = jnp.broadcast_to(wi[:, h:h + 1], (qb, kc))

    def score_body(c, carry):
        off = pl.multiple_of(c * kc, kc)
        kt = kit_ref[:, pl.ds(off, kc)]
        s = jnp.zeros((qb, kc), f32)
        for h in range(IDX_HEADS):
            s = s + wrep_ref[h] * jnp.maximum(_mm(qis[h], kt), 0.0)
        s = jnp.where(col + off <= t_row, s, NEG)
        bits = pltpu.bitcast(s, i32)
        skey_ref[:, pl.ds(off, kc)] = jnp.where(bits < 0, bits ^ 0x7FFFFFFF, bits)
        return carry

    n_strips = (q0 + qb + COUNT_STRIP - 1) // COUNT_STRIP
    lax.fori_loop(0, n_strips * (COUNT_STRIP // kc), score_body, 0)

    def count_ge(cand):
        def body(c, acc):
            off = pl.multiple_of(c * COUNT_STRIP, COUNT_STRIP)
            for j in range(COUNT_STRIP // LANES):
                acc = acc + jnp.where(skey_ref[:, pl.ds(off + j * LANES, LANES)] >= cand, 1.0, 0.0)
            return acc
        acc = lax.fori_loop(0, n_strips, body, jnp.zeros((qb, LANES), f32))
        return jnp.sum(acc, axis=1, keepdims=True)

    thr = jnp.where(count_ge(jnp.zeros((qb, 1), i32)) >= topk, 0, INT_MIN).astype(i32)

    def bisect(b, thr):
        cand = thr + jnp.left_shift(jnp.int32(1), 30 - b)
        return jnp.where(count_ge(cand) >= topk, cand, thr)

    thr = lax.fori_loop(0, 31, bisect, thr)
    quota = topk - count_ge(thr + 1)

    before = (lax.broadcasted_iota(i32, (kc, kc), 0) < lax.broadcasted_iota(i32, (kc, kc), 1))
    before = jnp.where(before, 1.0, 0.0).astype(MXU_DTYPE)
    _flash_reset(m_ref, acc_ref)
    alpha_ref[...] = jnp.zeros(alpha_ref.shape, f32)
    p_ref[...] = jnp.zeros(p_ref.shape, p_ref.dtype)
    heads = _row_slices(A_HEADS * qb)

    def fold(c, buf):
        off = pl.multiple_of(jnp.maximum(c, 0) * kc, kc)
        _flash_accumulate(va_ref[pl.ds(off, kc), :], alpha_ref.at[buf], p_ref.at[buf], acc_ref)

    def score(c, buf, ties_seen):
        off = pl.multiple_of(c * kc, kc)
        key = skey_ref[:, pl.ds(off, kc)]
        eq = key == thr
        eqf = jnp.where(eq, 1.0, 0.0)
        rank = ties_seen + _mm(eqf.astype(MXU_DTYPE), before)
        sel = ((key > thr) | (eq & (rank < quota))) & (col + off <= t_row)
        bias = jnp.where(sel, 0.0, NEG)
        kt = kat_ref[:, pl.ds(off, kc)]
        for rows in heads:
            _flash_rows(qh_ref[rows, :], kt, bias, m_ref, alpha_ref.at[buf], p_ref.at[buf], rows)
        return ties_seen + jnp.sum(eqf, axis=1, keepdims=True)

    _skewed_chunks(n_chunks, score, fold, jnp.zeros((qb, 1), f32))
    _fold_last(n_chunks, fold)
    acc = acc_ref[...]
    o = acc[:, :HEAD_DIM] / acc[:, HEAD_DIM:]
    for h in range(A_HEADS):
        o_ref[:, h * HEAD_DIM:(h + 1) * HEAD_DIM] = o[h * qb:(h + 1) * qb, :]


QPK_QA, QPK_QIDX, QPK_QB = 0, 2, 3
KPK_KA, KPK_KIDX, KPK_KSLC, KPK_KWIN = 0, 1, 2, 4
VPK_VA, VPK_VSLC, VPK_VWIN = 0, 1, 3
OUT_NSA = 2


def dsa_attention(wi, qpk, kpk, vpk, topk, kc=512):
    bsz, seq, _ = qpk.shape
    nb = seq // Q_BLOCK
    rows = A_HEADS * Q_BLOCK
    width = A_HEADS * HEAD_DIM
    kern = functools.partial(_dsa_kernel, topk=topk, kc=kc)
    return pl.pallas_call(
        kern,
        grid=(bsz, nb),
        in_specs=[pl.BlockSpec((None, Q_BLOCK, IDX_HEADS * IDX_DIM), lambda b, i: (b, i, QPK_QIDX)),
                  pl.BlockSpec((None, Q_BLOCK, IDX_HEADS), lambda b, i: (b, i, 0)),
                  pl.BlockSpec((None, IDX_DIM, seq), lambda b, i: (b, KPK_KIDX, 0)),
                  pl.BlockSpec((None, Q_BLOCK, width), lambda b, i: (b, i, QPK_QA)),
                  pl.BlockSpec((None, HEAD_DIM, seq), lambda b, i: (b, KPK_KA, 0)),
                  pl.BlockSpec((None, seq, 2 * HEAD_DIM), lambda b, i: (b, 0, VPK_VA))],
        out_specs=pl.BlockSpec((None, Q_BLOCK, width), lambda b, i: (b, i, 0)),
        out_shape=jax.ShapeDtypeStruct((bsz, seq, 2 * width), f32),
        scratch_shapes=[pltpu.VMEM((Q_BLOCK, seq), i32), pltpu.VMEM((IDX_HEADS, Q_BLOCK, kc), f32),
                        pltpu.VMEM((rows, LANES), f32), pltpu.VMEM((2, rows, LANES), f32),
                        pltpu.VMEM((rows, 2 * HEAD_DIM), f32), pltpu.VMEM((2, rows, kc), MXU_DTYPE),
                        pltpu.VMEM((rows, HEAD_DIM), MXU_DTYPE)],
        compiler_params=_params("parallel", "arbitrary"),
        name="dsa_attention",
    )(qpk, wi, kpk, qpk, kpk, vpk)


def _compress_kernel(x_ref, pe_ref, w1_ref, w2_ref, o_ref):
    n_rows = o_ref.shape[1]
    parts = CMP_LEN // CMP_STRIDE
    acc = [[jnp.zeros((n_rows, CMP_HIDDEN), f32) for _ in range(parts)] for _ in range(B_KV_GROUPS)]
    for t in range(CMP_STRIDE):
        tok = x_ref[pl.ds(t, n_rows, stride=CMP_STRIDE), :]
        for g in range(B_KV_GROUPS):
            tok_g = tok[:, g * HEAD_DIM:(g + 1) * HEAD_DIM]
            for part in range(parts):
                l = part * CMP_STRIDE + t
                lhs = (tok_g + pe_ref[l:l + 1, :]).astype(MXU_DTYPE)
                acc[g][part] = acc[g][part] + _mm(lhs, w1_ref[l * HEAD_DIM:(l + 1) * HEAD_DIM, :])
    for g in range(B_KV_GROUPS):
        pre = acc[g][0] + pltpu.roll(acc[g][1], shift=n_rows - 1, axis=0)
        hid = pre * jax.nn.sigmoid(pre)
        o_ref[g] = _mm(hid.astype(MXU_DTYPE), w2_ref[...])


def nsa_compress(cmp, pe, w1, w2):
    bsz, seq, _ = cmp.shape
    n_rows = seq // CMP_STRIDE
    width = B_KV_GROUPS * HEAD_DIM
    return pl.pallas_call(
        _compress_kernel,
        grid=(bsz, 2),
        in_specs=[pl.BlockSpec((None, seq, width), lambda b, s: (b, 0, s)),
                  pl.BlockSpec((None, CMP_LEN, HEAD_DIM), lambda b, s: (s, 0, 0)),
                  pl.BlockSpec((None, CMP_LEN * HEAD_DIM, CMP_HIDDEN), lambda b, s: (s, 0, 0)),
                  pl.BlockSpec((None, CMP_HIDDEN, HEAD_DIM), lambda b, s: (s, 0, 0))],
        out_specs=pl.BlockSpec((None, None, B_KV_GROUPS, n_rows, HEAD_DIM), lambda b, s: (b, s, 0, 0, 0)),
        out_shape=jax.ShapeDtypeStruct((bsz, 2, B_KV_GROUPS, n_rows, HEAD_DIM), f32),
        compiler_params=_params("parallel", "parallel"),
        name="nsa_compress",
    )(cmp, pe, w1, w2)


def _nsa_kernel(qin_ref, graw_ref, kct_ref, vc_ref, kst_ref, vs_ref, kwt_ref, vw_ref, ovl_ref, _, o_ref,
                m_ref, alpha_ref, acc_ref, p_ref, pw_ref, q_ref, imp_ref, *, n_slc, n_sel, kc):
    qb = Q_BLOCK
    i = pl.program_id(2)
    q0 = i * qb
    n_cmp = kct_ref.shape[1]
    heads = _row_slices(B_REP * qb)
    qin = qin_ref[...]
    for r in range(B_REP):
        q_ref[r * qb:(r + 1) * qb, :] = qin[:, r * HEAD_DIM:(r + 1) * HEAD_DIM]

    def compressed(width):
        t_c = q0 + lax.broadcasted_iota(i32, (qb, width), 0)
        cmp_end = lax.broadcasted_iota(i32, (qb, width), 1) * CMP_STRIDE + (CMP_LEN - 1)
        vis = cmp_end <= t_c
        kct = kct_ref[:, :width]
        p_sum = jnp.zeros((qb, width), f32)
        for rows in heads:
            lc = jnp.where(vis, _mm(q_ref[rows, :], kct), NEG)
            ec = jnp.where(vis, jnp.exp2(lc - jnp.max(lc, axis=1, keepdims=True)), 0.0)
            den = jnp.sum(ec, axis=1, keepdims=True)
            p_c = ec / jnp.where(den > 0.0, den, 1.0)
            p_sum = p_sum + p_c
            acc_ref[rows, :HEAD_DIM] = _mm(p_c.astype(MXU_DTYPE), vc_ref[:width, :])
        imp_ref[...] = jnp.dot(p_sum, ovl_ref[:width, :], preferred_element_type=f32,
                               precision=lax.Precision.HIGHEST)

    if n_cmp % (2 * LANES) == 0:
        first_half_only = (q0 + qb) * 2 <= n_cmp * CMP_STRIDE
        pl.when(first_half_only)(functools.partial(compressed, n_cmp // 2))
        pl.when(jnp.logical_not(first_half_only))(functools.partial(compressed, n_cmp))
    else:
        compressed(n_cmp)
    o_c = acc_ref[:, :HEAD_DIM]

    imp = imp_ref[...].T
    t_q = q0 + lax.broadcasted_iota(i32, (LANES, qb), 1)
    blk = lax.broadcasted_iota(i32, (LANES, qb), 0)
    blk_t = t_q // SLC_LEN
    forced = (blk == 0) | (blk == blk_t) | (blk == blk_t - 1)
    imp = jnp.where(forced, FORCED_BOOST, imp)
    imp = jnp.where(blk * SLC_LEN <= t_q, imp, NEG)
    imp = jnp.where(blk < n_slc, imp, -jnp.inf)
    blk_f = blk.astype(f32)

    def pick(_, carry):
        imp, selm = carry
        best = jnp.max(imp, axis=0, keepdims=True)
        first = jnp.min(jnp.where(imp == best, blk_f, float(LANES)), axis=0, keepdims=True)
        hit = blk_f == first
        return jnp.where(hit, -jnp.inf, imp), jnp.where(hit, 1.0, selm)

    _, selm = lax.fori_loop(0, n_sel, pick, (imp, jnp.zeros((LANES, qb), f32)))
    selm = selm.T.astype(MXU_DTYPE)

    t_k = q0 + lax.broadcasted_iota(i32, (qb, kc), 0)
    col = lax.broadcasted_iota(i32, (qb, kc), 1)
    exp_row = lax.broadcasted_iota(i32, (LANES, kc), 0)
    exp_col = lax.broadcasted_iota(i32, (LANES, kc), 1)
    _flash_reset(m_ref, acc_ref)
    alpha_ref[...] = jnp.zeros(alpha_ref.shape, f32)
    p_ref[...] = jnp.zeros(p_ref.shape, p_ref.dtype)

    def fold(c, buf):
        off = pl.multiple_of(jnp.maximum(c, 0) * kc, kc)
        _flash_accumulate(vs_ref[pl.ds(off, kc), :], alpha_ref.at[buf], p_ref.at[buf], acc_ref)

    def score(c, buf, carry):
        off = pl.multiple_of(c * kc, kc)
        expand = jnp.where(exp_row == (exp_col + off) // SLC_LEN, 1.0, 0.0).astype(MXU_DTYPE)
        sel = (_mm(selm, expand) > 0.5) & (col + off <= t_k)
        bias = jnp.where(sel, 0.0, NEG)
        kt = kst_ref[:, pl.ds(off, kc)]
        for rows in heads:
            _flash_rows(q_ref[rows, :], kt, bias, m_ref, alpha_ref.at[buf], p_ref.at[buf], rows)
        return carry

    n_slc_chunks = (q0 + qb + kc - 1) // kc
    _skewed_chunks(n_slc_chunks, score, fold, 0)
    _fold_last(n_slc_chunks, fold)
    acc = acc_ref[...]
    o_s = acc[:, :HEAD_DIM] / acc[:, HEAD_DIM:]

    slab = WINDOW + qb
    w0 = pl.multiple_of(jnp.maximum(q0 - WINDOW, 0), qb)
    dist = (q0 + lax.broadcasted_iota(i32, (qb, slab), 0)) - (w0 + lax.broadcasted_iota(i32, (qb, slab), 1))
    bias = jnp.where((dist >= 0) & (dist < WINDOW), 0.0, NEG)
    kt = kwt_ref[:, pl.ds(w0, slab)]
    for rows in heads:
        s = _mm(q_ref[rows, :], kt) + bias
        pw_ref[rows, :] = jnp.exp2(s - jnp.max(s, axis=1, keepdims=True)).astype(pw_ref.dtype)
    acc = _mm(pw_ref[...], vw_ref[pl.ds(w0, slab), :])
    o_w = acc[:, :HEAD_DIM] / acc[:, HEAD_DIM:]

    gates = jax.nn.sigmoid(graw_ref[...])
    for r, rows in enumerate(heads):
        g_c, g_s, g_w = (gates[:, 3 * r + n:3 * r + n + 1] for n in range(3))
        o_ref[:, r * HEAD_DIM:(r + 1) * HEAD_DIM] = g_c * o_c[rows] + g_s * o_s[rows] + g_w * o_w[rows]


def nsa_attention(qpk, graw, kct, vc, kpk, vpk, ovl, out_ab, n_slc, n_sel, kc=512):
    bsz, seq, _ = qpk.shape
    groups = B_KV_GROUPS
    nb = seq // Q_BLOCK
    rows = B_REP * Q_BLOCK
    width = B_REP * HEAD_DIM
    n_cmp = kct.shape[-1]
    kern = functools.partial(_nsa_kernel, n_slc=n_slc, n_sel=n_sel, kc=kc)
    per_bg = lambda b, g, i: (b, g, 0, 0)
    return pl.pallas_call(
        kern,
        grid=(bsz, groups, nb),
        in_specs=[pl.BlockSpec((None, Q_BLOCK, width), lambda b, g, i: (b, i, QPK_QB + g)),
                  pl.BlockSpec((None, None, Q_BLOCK, B_REP * 3), lambda b, g, i: (b, g, i, 0)),
                  pl.BlockSpec((None, None, HEAD_DIM, n_cmp), per_bg),
                  pl.BlockSpec((None, None, n_cmp, HEAD_DIM), per_bg),
                  pl.BlockSpec((None, HEAD_DIM, seq), lambda b, g, i: (b, KPK_KSLC + g, 0)),
                  pl.BlockSpec((None, seq, 2 * HEAD_DIM), lambda b, g, i: (b, 0, VPK_VSLC + g)),
                  pl.BlockSpec((None, HEAD_DIM, seq), lambda b, g, i: (b, KPK_KWIN + g, 0)),
                  pl.BlockSpec((None, seq, 2 * HEAD_DIM), lambda b, g, i: (b, 0, VPK_VWIN + g)),
                  pl.BlockSpec((n_cmp, LANES), lambda b, g, i: (0, 0)),
                  pl.BlockSpec(memory_space=pl.ANY)],
        out_specs=pl.BlockSpec((None, Q_BLOCK, width), lambda b, g, i: (b, i, OUT_NSA + g)),
        out_shape=jax.ShapeDtypeStruct(out_ab.shape, f32),
        input_output_aliases={9: 0},
        scratch_shapes=[pltpu.VMEM((rows, LANES), f32), pltpu.VMEM((2, rows, LANES), f32),
                        pltpu.VMEM((rows, 2 * HEAD_DIM), f32), pltpu.VMEM((2, rows, kc), MXU_DTYPE),
                        pltpu.VMEM((rows, WINDOW + Q_BLOCK), MXU_DTYPE), pltpu.VMEM((rows, HEAD_DIM), MXU_DTYPE),
                        pltpu.VMEM((Q_BLOCK, LANES), f32)],
        compiler_params=_params("parallel", "parallel", "arbitrary"),
        name="nsa_attention",
    )(qpk, graw, kct, vc, kpk, vpk, kpk, vpk, ovl, out_ab)


def _diff_kernel(lam_ref, qin_ref, kt_ref, v_ref, g_ref, o_ref, m_ref, alpha_ref, acc_ref, p_ref, q_ref,
                 *, tq, kc, out_scale):
    i = pl.program_id(2)
    q0 = i * tq
    dv = v_ref.shape[-1] // 2
    groups = _row_slices(tq)
    qin = qin_ref[...]
    for half in range(2):
        q_ref[half] = qin[:, half * HEAD_DIM:(half + 1) * HEAD_DIM]
    for half in range(2):
        _flash_reset(m_ref.at[half], acc_ref.at[half])
    alpha_ref[...] = jnp.zeros(alpha_ref.shape, f32)
    p_ref[...] = jnp.zeros(p_ref.shape, p_ref.dtype)

    def fold(c, buf, sub=(0, kc // tq)):
        off = pl.multiple_of(jnp.maximum(c, 0) * kc + sub[0] * tq, tq)
        v = v_ref[pl.ds(off, sub[1] * tq), :]
        for half in range(2):
            _flash_accumulate(v, alpha_ref.at[buf, half], p_ref.at[buf, half], acc_ref.at[half])

    def score(c, buf, masked, sub=(0, kc // tq)):
        off = pl.multiple_of(c * kc + sub[0] * tq, tq)
        width = sub[1] * tq
        for half in range(2):
            kt = kt_ref[half * HEAD_DIM:(half + 1) * HEAD_DIM, pl.ds(off, width)]
            for r, rows in enumerate(groups):
                bias = None
                if masked:
                    key = off + lax.broadcasted_iota(i32, (Q_BLOCK, width), 1)
                    t = q0 + r * Q_BLOCK + lax.broadcasted_iota(i32, (Q_BLOCK, width), 0)
                    bias = jnp.where(key <= t, 0.0, NEG)
                _flash_rows(q_ref[half, rows, :], kt, bias, m_ref.at[half], alpha_ref.at[buf, half],
                            p_ref.at[buf, half], rows)

    def score_full(c, buf, carry):
        score(c, buf, False)
        return carry

    n_full = q0 // kc
    _skewed_chunks(n_full, score_full, fold, 0)

    own = (q0 - n_full * kc) // tq

    def finish(buf, own_piece):
        fold(n_full - 1, 1 - buf)
        if own_piece:
            score(n_full, buf, False, (0, own_piece))
            fold(n_full, buf, (0, own_piece))
            buf = 1 - buf
        score(n_full, buf, True, (own_piece, 1))
        fold(n_full, buf, (own_piece, 1))

    for buf in range(2):
        for own_piece in range(kc // tq):
            pl.when((n_full % 2 == buf) & (own == own_piece))(functools.partial(finish, buf, own_piece))
    a1 = acc_ref[0]
    a2 = acc_ref[1]
    o = a1[:, :dv] / a1[:, dv:] - lam_ref[0] * (a2[:, :dv] / a2[:, dv:])
    o = o * lax.rsqrt(jnp.mean(o * o, axis=-1, keepdims=True) + LN_EPS) * g_ref[...]
    o_ref[...] = o * out_scale


def diff_attention(lam, q, kt, v, subln_g, out_scale, tq=512, kc=1024):
    bsz, seq, width = q.shape
    dv = 2 * HEAD_DIM
    heads = width // dv
    tq = min(tq, seq)
    kc = min(kc, seq)
    assert kc % tq == 0 and seq % kc == 0
    kern = functools.partial(_diff_kernel, tq=tq, kc=kc, out_scale=out_scale)
    return pl.pallas_call(
        kern,
        grid=(bsz, heads, seq // tq),
        in_specs=[pl.BlockSpec(memory_space=pltpu.SMEM),
                  pl.BlockSpec((None, tq, dv), lambda b, h, i: (b, i, h)),
                  pl.BlockSpec((None, dv, seq), lambda b, h, i: (b, h, 0)),
                  pl.BlockSpec((None, seq, 2 * dv), lambda b, h, i: (b, 0, h)),
                  pl.BlockSpec((1, dv), lambda b, h, i: (0, 0))],
        out_specs=pl.BlockSpec((None, tq, dv), lambda b, h, i: (b, i, h)),
        out_shape=jax.ShapeDtypeStruct((bsz, seq, heads * dv), f32),
        scratch_shapes=[pltpu.VMEM((2, tq, LANES), f32), pltpu.VMEM((2, 2, tq, LANES), f32),
                        pltpu.VMEM((2, tq, 2 * dv), f32), pltpu.VMEM((2, 2, tq, kc), MXU_DTYPE),
                        pltpu.VMEM((2, tq, HEAD_DIM), MXU_DTYPE)],
        compiler_params=_params("parallel", "parallel", "arbitrary"),
        name="diff_attention",
    )(lam, q, kt, v, subln_g.reshape(1, dv))


def _rope_tables(positions):
    inv_freq = ROPE_THETA ** (-jnp.arange(0, ROT_DIM, 2, dtype=f32) / ROT_DIM)
    ang = positions.astype(f32)[..., None] * inv_freq
    return jnp.cos(ang), jnp.sin(ang)


def _apply_rope(x, cos, sin):
    shape = cos.shape[:2] + (1,) * (x.ndim - 3) + cos.shape[-1:]
    c = cos.reshape(shape)
    s = sin.reshape(shape)
    half = ROT_DIM // 2
    x1, x2 = x[..., :half], x[..., half:ROT_DIM]
    return jnp.concatenate([x1 * c - x2 * s, x2 * c + x1 * s, x[..., ROT_DIM:]], axis=-1)


def _rope_coefficients(cos, sin):
    half = ROT_DIM // 2
    spread = np.zeros((ROT_DIM, 3 * LANES), np.float32)
    offset = np.zeros((3 * LANES,), np.float32)
    for lane in range(LANES):
        d = lane % HEAD_DIM
        if d >= ROT_DIM:
            offset[lane] = 1.0
        elif d < half:
            spread[d, lane], spread[half + d, LANES + lane] = 1.0, -1.0
        else:
            spread[d - half, lane], spread[d, 2 * LANES + lane] = 1.0, 1.0
    table = jnp.concatenate([cos, sin], axis=-1)
    return jnp.dot(table, spread, precision=lax.Precision.HIGHEST) + offset


def _ab_weight_columns(w_in):
    widths = dict(AB_LAYOUT)
    starts = dict(zip(widths, np.cumsum([0] + [w for _, w in AB_LAYOUT[:-1]]).tolist()))
    cols = lambda names: np.concatenate([np.arange(starts[n], starts[n] + widths[n]) for n in names])
    first = ("q_a", "q_idx", "q_b", "k_a", "k_idx", "k_slc", "k_win", "v_a", "v_slc", "v_win", "gate_b", "w_idx")
    n_first = sum(widths[n] for n in first)
    pad = AB_CMP_TILE * LANES - n_first
    assert 0 <= pad < HEAD_DIM
    w = jnp.concatenate([w_in[:, cols(first)], jnp.zeros((w_in.shape[0], pad), w_in.dtype),
                         w_in[:, cols(("k_cmp", "v_cmp"))]], axis=1)
    assert w.shape[1] == AB_TILES * LANES
    return w.astype(MXU_DTYPE)


def _overlap_matrix(n_cmp_rows, n_slc):
    c_start = np.arange(n_cmp_rows) * CMP_STRIDE
    s_start = np.arange(LANES) * SLC_LEN
    ovl = (c_start[:, None] < s_start[None, :] + SLC_LEN) & (c_start[:, None] + CMP_LEN > s_start[None, :])
    ovl = ovl & (np.arange(LANES)[None, :] < n_slc)
    return jnp.asarray(ovl.astype(np.float32))


def _ab_mixer(x2, bsz, seq, positions, coef, w_in, pe_k, pe_v, ck1, ck2, cv1, cv2):
    qpk, kpk, vpk, misc, cmp = project_pack_ab(x2.reshape(bsz, seq, -1), _ab_weight_columns(w_in), coef)
    gate_w = B_HEADS * 3
    g_b = misc[..., HEAD_DIM:HEAD_DIM + gate_w]
    w_idx = misc[..., HEAD_DIM + gate_w:HEAD_DIM + gate_w + IDX_HEADS] * (IDX_HEADS * IDX_DIM) ** -0.5

    out_ab = dsa_attention(w_idx, qpk, kpk, vpk, topk=min(DSA_TOPK, seq // 4))

    groups = B_KV_GROUPS
    n_rows = seq // CMP_STRIDE
    n_cmp = (seq - CMP_LEN) // CMP_STRIDE + 1
    assert n_cmp == n_rows - 1
    n_slc = seq // SLC_LEN
    assert n_slc <= LANES

    kv_c = nsa_compress(cmp, jnp.stack([pe_k, pe_v]), jnp.stack([ck1, cv1]).astype(MXU_DTYPE),
                        jnp.stack([ck2, cv2]).astype(MXU_DTYPE))
    k_c, v_c = kv_c[:, 0], kv_c[:, 1]
    cmp_end = jnp.minimum(jnp.arange(n_rows) * CMP_STRIDE + CMP_LEN - 1, seq - 1)
    cos_c, sin_c = _rope_tables(positions[:, cmp_end])
    k_c = _apply_rope(k_c.transpose(0, 2, 1, 3), cos_c, sin_c)
    k_c = k_c.transpose(0, 2, 3, 1).astype(MXU_DTYPE)
    v_c = v_c.astype(MXU_DTYPE)

    graw = g_b.reshape(bsz, seq, groups, B_REP * 3).transpose(0, 2, 1, 3)
    out_ab = nsa_attention(qpk, graw, k_c, v_c, kpk, vpk, _overlap_matrix(n_rows, n_slc), out_ab,
                           n_slc=n_slc, n_sel=min(SLC_TOPN, n_slc))
    return out_ab.reshape(bsz * seq, (A_HEADS + B_HEADS) * HEAD_DIM)


def _diff_mixer(x2, bsz, seq, coef, w_in, lq1, lk1, lq2, lk2, subln_g, lam_init):
    q, kt, v = project_pack_c(x2.reshape(bsz, seq, -1), w_in.astype(MXU_DTYPE), coef)
    lam =(jnp.exp(jnp.sum(lq1 * lk1)) - jnp.exp(jnp.sum(lq2 * lk2)) + lam_init).reshape(1).astype(f32)
    o = diff_attention(lam, q, kt, v, subln_g, 1.0 - lam_init)
    return o.reshape(bsz * seq, C_HEADS * 2 * HEAD_DIM)


def kernel(x, positions, ab_w_in, cmp_pe_k, cmp_pe_v, cmp_k_w1, cmp_k_w2, cmp_v_w1, cmp_v_w2, ab_w_out, ln_ab_g, ln_ab_b, ffn_w1, ffn_w3, ffn_w2, ln_ffn_g, ln_ffn_b, c_w_in, lambda_q1, lambda_k1, lambda_q2, lambda_k2, c_subln_g, c_w_out, ln_c_g, ln_c_b, router_w, moe_w1, moe_w3, moe_w2, ln_moe_g, ln_moe_b):
    bsz, seq, d = x.shape
    assert seq % COUNT_STRIP == 0 and seq >= WINDOW + Q_BLOCK and d == D_MODEL
    coef = _rope_coefficients(*_rope_tables(positions))
    x2 = x.reshape(bsz * seq, d)
    for layer in range(DEPTH):
        i = layer // 2
        if layer % 2 == 0:
            o = _ab_mixer(x2, bsz, seq, positions, coef, ab_w_in[i], cmp_pe_k[i], cmp_pe_v[i],
                          cmp_k_w1[i], cmp_k_w2[i], cmp_v_w1[i], cmp_v_w2[i])
            x2 = project_residual_ln(o, ab_w_out[i].astype(MXU_DTYPE), x2, ln_ab_g[i], ln_ab_b[i])
            x2 = ffn_residual_ln(x2, ffn_w1[i].astype(MXU_DTYPE), ffn_w3[i].astype(MXU_DTYPE),
                                 ffn_w2[i].astype(MXU_DTYPE), ln_ffn_g[i], ln_ffn_b[i], tm=512, tf=ffn_w1.shape[-1])
        else:
            lam_init = 0.8 - 0.6 * math.exp(-0.3 * layer)
            o = _diff_mixer(x2, bsz, seq, coef, c_w_in[i], lambda_q1[i], lambda_k1[i], lambda_q2[i],
                            lambda_k2[i], c_subln_g[i], lam_init)
            x2 = project_residual_ln(o, c_w_out[i].astype(MXU_DTYPE), x2, ln_c_g[i], ln_c_b[i])
            routes = route_top2(x2, router_w[i])
            x2 = moe_residual_ln(x2, routes, moe_w1[i].astype(MXU_DTYPE), moe_w3[i].astype(MXU_DTYPE),
                                 moe_w2[i].astype(MXU_DTYPE), ln_moe_g[i], ln_moe_b[i], tm=min(2048, bsz * seq // 2), tf=896, rt=128)
    return x2.reshape(bsz, seq, d)
```

```python
import functools
import math

import numpy as np
import jax
import jax.numpy as jnp
from jax import lax
from jax.experimental import pallas as pl
from jax.experimental.pallas import tpu as pltpu

f32 = jnp.float32
i32 = jnp.int32
MXU_DTYPE = jnp.bfloat16
VMEM_LIMIT_BYTES = 56 * 1024 * 1024
LANES = 128

D_MODEL = 1024
DEPTH = 2
HEAD_DIM = 64
ROT_DIM = HEAD_DIM // 4
ROPE_THETA = 500000.0
Q_BLOCK = 128
NEG = -1e30
LN_EPS = 1e-5
A_HEADS = 8
IDX_HEADS = 4
IDX_DIM = 64
DSA_TOPK = 256
B_HEADS = 8
B_KV_GROUPS = 2
B_REP = B_HEADS // B_KV_GROUPS
CMP_LEN = 32
CMP_STRIDE = 16
CMP_HIDDEN = 128
SLC_LEN = 64
SLC_TOPN = 16
WINDOW = 512
FORCED_BOOST = 1e6
C_HEADS = 8
N_EXPERTS = 8
TOP_K = 2
DEEPNORM_ALPHA = (2 * DEPTH) ** 0.25
QK_SCALE = HEAD_DIM ** -0.5 * math.log2(math.e)
INT_MIN = -(2 ** 31)
COUNT_STRIP = 512
ROW_UNROLL = 4

AB_LAYOUT = (
    ("q_a", A_HEADS * HEAD_DIM), ("k_a", HEAD_DIM), ("v_a", HEAD_DIM),
    ("q_idx", IDX_HEADS * IDX_DIM), ("k_idx", IDX_DIM), ("w_idx", IDX_HEADS),
    ("q_b", B_HEADS * HEAD_DIM),
    ("k_cmp", B_KV_GROUPS * HEAD_DIM), ("v_cmp", B_KV_GROUPS * HEAD_DIM),
    ("k_slc", B_KV_GROUPS * HEAD_DIM), ("v_slc", B_KV_GROUPS * HEAD_DIM),
    ("k_win", B_KV_GROUPS * HEAD_DIM), ("v_win", B_KV_GROUPS * HEAD_DIM),
    ("gate_b", 3 * B_HEADS),
)


def _params(*sem):
    return pltpu.CompilerParams(dimension_semantics=sem, vmem_limit_bytes=VMEM_LIMIT_BYTES)


def _mm(a, b):
    return jnp.dot(a, b, preferred_element_type=f32)


def _layer_norm_rows(y, g, b):
    mu = jnp.mean(y, axis=-1, keepdims=True)
    yc = y - mu
    var = jnp.mean(yc * yc, axis=-1, keepdims=True)
    return yc * lax.rsqrt(var + LN_EPS) * g + b


def _rope_tile(x, keep, hi, lo):
    return x * keep + pltpu.roll(x, LANES - ROT_DIM // 2, 1) * hi + pltpu.roll(x, ROT_DIM // 2, 1) * lo


def _tile(h, j):
    return h[:, j * LANES:(j + 1) * LANES]


AB_Q_TILES = 10
AB_QIDX_TILES = (4, 5)
AB_K_TILES = 3
AB_V_HEADS = 5
AB_MISC_TILE = 15
AB_CMP_TILE = 16
AB_TILES = 18


def _proj_pack_ab_kernel(x_ref, w_ref, coef_ref, q_ref, kt_ref, v_ref, misc_ref, cmp_ref):
    h = _mm(x_ref[...].astype(MXU_DTYPE), w_ref[...])
    keep, hi, lo = (coef_ref[:, j * LANES:(j + 1) * LANES] for j in range(3))
    for j in range(AB_Q_TILES):
        t = _rope_tile(_tile(h, j), keep, hi, lo)
        q_ref[:, j * LANES:(j + 1) * LANES] = (t if j in AB_QIDX_TILES else t * QK_SCALE).astype(q_ref.dtype)
    k = jnp.concatenate([_rope_tile(_tile(h, AB_Q_TILES + j), keep, hi, lo) for j in range(AB_K_TILES)], axis=1)
    kt_ref[...] = k.T.astype(kt_ref.dtype)
    low = lax.broadcasted_iota(i32, keep.shape, 1) < HEAD_DIM
    for j in range(AB_V_HEADS):
        t = _tile(h, AB_Q_TILES + AB_K_TILES + j // 2)
        t = pltpu.roll(t, HEAD_DIM, 1) if j % 2 else t
        v_ref[:, j * LANES:(j + 1) * LANES] = jnp.where(low, t, 1.0).astype(v_ref.dtype)
    misc_ref[...] = _tile(h, AB_MISC_TILE)
    cmp_ref[...] = h[:, AB_CMP_TILE * LANES:AB_TILES * LANES]


def project_pack_ab(x, w, coef, tm=512):
    bsz, seq, d = x.shape
    row = lambda width: pl.BlockSpec((None, tm, width), lambda b, i: (b, i, 0))
    kt_rows = AB_K_TILES * LANES
    return pl.pallas_call(
        _proj_pack_ab_kernel,
        grid=(bsz, seq // tm),
        in_specs=[row(d), pl.BlockSpec((d, AB_TILES * LANES), lambda b, i: (0, 0)), row(3 * LANES)],
        out_specs=[row(AB_Q_TILES * LANES), pl.BlockSpec((None, kt_rows, tm), lambda b, i: (b, 0, i)),
                   row(AB_V_HEADS * LANES), row(LANES), row(2 * LANES)],
        out_shape=[jax.ShapeDtypeStruct((bsz, seq, AB_Q_TILES * LANES), MXU_DTYPE),
                   jax.ShapeDtypeStruct((bsz, kt_rows, seq), MXU_DTYPE),
                   jax.ShapeDtypeStruct((bsz, seq, AB_V_HEADS * LANES), MXU_DTYPE),
                   jax.ShapeDtypeStruct((bsz, seq, LANES), f32),
                   jax.ShapeDtypeStruct((bsz, seq, 2 * LANES), f32)],
        compiler_params=_params("parallel", "parallel"),
        name="project_pack_ab",
    )(x, w, coef)


def _proj_pack_c_kernel(x_ref, w_ref, coef_ref, q_ref, kt_ref, v_ref):
    h = _mm(x_ref[...].astype(MXU_DTYPE), w_ref[...])
    keep, hi, lo = (coef_ref[:, j * LANES:(j + 1) * LANES] for j in range(3))
    n = q_ref.shape[1] // LANES
    for j in range(n):
        q_ref[:, j * LANES:(j + 1) * LANES] = (_rope_tile(_tile(h, j), keep, hi, lo) * QK_SCALE).astype(q_ref.dtype)
    k = jnp.concatenate([_rope_tile(_tile(h, n + j), keep, hi, lo) for j in range(n)], axis=1)
    kt_ref[...] = k.T.astype(kt_ref.dtype)
    ones = jnp.ones(keep.shape, v_ref.dtype)
    for j in range(n):
        v_ref[:, 2 * j * LANES:(2 * j + 1) * LANES] = _tile(h, 2 * n + j).astype(v_ref.dtype)
        v_ref[:, (2 * j + 1) * LANES:(2 * j + 2) * LANES] = ones


def project_pack_c(x, w, coef, tm=512):
    bsz, seq, d = x.shape
    width = w.shape[1] // 3
    row = lambda cols: pl.BlockSpec((None, tm, cols), lambda b, i: (b, i, 0))
    return pl.pallas_call(
        _proj_pack_c_kernel,
        grid=(bsz, seq // tm),
        in_specs=[row(d), pl.BlockSpec((d, 3 * width), lambda b, i: (0, 0)), row(3 * LANES)],
        out_specs=[row(width), pl.BlockSpec((None, width, tm), lambda b, i: (b, 0, i)), row(2 * width)],
        out_shape=[jax.ShapeDtypeStruct((bsz, seq, width), MXU_DTYPE),
                   jax.ShapeDtypeStruct((bsz, width, seq), MXU_DTYPE),
                   jax.ShapeDtypeStruct((bsz, seq, 2 * width), MXU_DTYPE)],
        compiler_params=_params("parallel", "parallel"),
        name="project_pack_c",
    )(x, w, coef)


def _proj_ln_kernel(a_ref, w_ref, res_ref, g_ref, b_ref, *rest, n_experts):
    h = _mm(a_ref[...].astype(MXU_DTYPE), w_ref[...])
    y = _layer_norm_rows(DEEPNORM_ALPHA * res_ref[...] + h, g_ref[...], b_ref[...])
    if n_experts:
        router_ref, o_ref, routes_ref = rest
        routes_ref[...] = _route_rows(y, router_ref[...], n_experts)
    else:
        o_ref, = rest
    o_ref[...] = y


def project_residual_ln(a, w, res, g, b, router_w=None, tm=512):
    m, k = a.shape
    n = w.shape[1]
    rows = lambda width: pl.BlockSpec((tm, width), lambda i: (i, 0))
    whole = lambda r, c: pl.BlockSpec((r, c), lambda i: (0, 0))
    in_specs = [rows(k), whole(k, n), rows(n), whole(1, n), whole(1, n)]
    args = [a, w, res, g.reshape(1, n), b.reshape(1, n)]
    out_specs, out_shape, n_experts = rows(n), jax.ShapeDtypeStruct((m, n), f32), 0
    if router_w is not None:
        n_experts = router_w.shape[1]
        in_specs.append(whole(n, LANES))
        args.append(jnp.zeros((n, LANES), f32).at[:, :n_experts].set(router_w))
        out_specs, out_shape = [out_specs, rows(LANES)], [out_shape, jax.ShapeDtypeStruct((m, LANES), f32)]
    return pl.pallas_call(
        functools.partial(_proj_ln_kernel, n_experts=n_experts),
        grid=(m // tm,),
        in_specs=in_specs,
        out_specs=out_specs,
        out_shape=out_shape,
        compiler_params=_params("parallel"),
        name="project_residual_ln",
    )(*args)


def _swiglu_tile(xb, w1_ref, w3_ref, w2_ref):
    a = _mm(xb, w1_ref[...])
    h = (a * jax.nn.sigmoid(a)) * _mm(xb, w3_ref[...])
    return _mm(h.astype(MXU_DTYPE), w2_ref[...])


def _ffn_ln_kernel(x_ref, w1_ref, w3_ref, w2_ref, g_ref, b_ref, o_ref, xb_ref, acc_ref):
    f = pl.program_id(1)

    @pl.when(f == 0)
    def _():
        xb_ref[...] = x_ref[...].astype(MXU_DTYPE)
        acc_ref[...] = jnp.zeros_like(acc_ref)

    acc_ref[...] += _swiglu_tile(xb_ref[...], w1_ref, w3_ref, w2_ref)

    @pl.when(f == pl.num_programs(1) - 1)
    def _():
        o_ref[...] = _layer_norm_rows(DEEPNORM_ALPHA * x_ref[...] + acc_ref[...], g_ref[...], b_ref[...])


def ffn_residual_ln(x, w1, w3, w2, g, b, tm, tf):
    m, d = x.shape
    ff = w1.shape[1]
    resident = dict(pipeline_mode=pl.Buffered(1)) if tf == ff else {}
    return pl.pallas_call(
        _ffn_ln_kernel,
        grid=(m // tm, ff // tf),
        in_specs=[pl.BlockSpec((tm, d), lambda i, f: (i, 0)),
                  pl.BlockSpec((d, tf), lambda i, f: (0, f), **resident),
                  pl.BlockSpec((d, tf), lambda i, f: (0, f), **resident),
                  pl.BlockSpec((tf, d), lambda i, f: (f, 0), **resident),
                  pl.BlockSpec((1, d), lambda i, f: (0, 0)), pl.BlockSpec((1, d), lambda i, f: (0, 0))],
        out_specs=pl.BlockSpec((tm, d), lambda i, f: (i, 0)),
        out_shape=jax.ShapeDtypeStruct((m, d), f32),
        scratch_shapes=[pltpu.VMEM((tm, d), MXU_DTYPE), pltpu.VMEM((tm, d), f32)],
        compiler_params=_params("parallel", "arbitrary"),
        name="ffn_residual_ln",
    )(x, w1, w3, w2, g.reshape(1, d), b.reshape(1, d))


ROUTE_IDS = N_EXPERTS
ROUTE_GATES = N_EXPERTS + 2


def _route_rows(x, w, n_experts):
    logits = jnp.dot(x, w, preferred_element_type=f32, precision=lax.Precision.HIGHEST)
    lane = lax.broadcasted_iota(i32, logits.shape, 1).astype(f32)
    logits = jnp.where(lane < n_experts, logits, -jnp.inf)
    v1 = jnp.max(logits, axis=1, keepdims=True)
    i1 = jnp.min(jnp.where(logits == v1, lane, float(LANES)), axis=1, keepdims=True)
    rest = jnp.where(lane == i1, -jnp.inf, logits)
    v2 = jnp.max(rest, axis=1, keepdims=True)
    i2 = jnp.min(jnp.where(rest == v2, lane, float(LANES)), axis=1, keepdims=True)
    e2 = jnp.exp(v2 - v1)
    g1 = 1.0 / (1.0 + e2)
    g2 = e2 / (1.0 + e2)
    out = jnp.where(lane == ROUTE_IDS, i1, 0.0) + jnp.where(lane == ROUTE_IDS + 1, i2, 0.0)
    return out + jnp.where(lane == ROUTE_GATES, g1, 0.0) + jnp.where(lane == ROUTE_GATES + 1, g2, 0.0)


def _moe_ln_kernel(tok_ref, gs_ref, off_ref, x_ref, w1_ref, w3_ref, w2_ref, g_ref, b_ref, o_ref,
                   xg_ref, xb_ref, y_ref, *, rt):
    c = pl.program_id(0)
    e = pl.program_id(1)
    f = pl.program_id(2)
    last_f = pl.num_programs(2) - 1
    start = off_ref[0, e]
    count = off_ref[0, e + 1] - start
    n_tiles = (count + rt - 1) // rt

    @pl.when((c == 0) & (e == 0) & (f == 0))
    def _():
        xg_ref[...] = jnp.zeros_like(xg_ref)

    @pl.when((e == 0) & (f == 0))
    def _():
        o_ref[...] = jnp.zeros_like(o_ref)

    def row_loop(body):
        def group(j, carry):
            for u in range(ROW_UNROLL):
                body(j * ROW_UNROLL + u)
            return carry

        def single(r, carry):
            body(r)
            return carry
        lax.fori_loop(0, count // ROW_UNROLL, group, 0)
        lax.fori_loop((count // ROW_UNROLL) * ROW_UNROLL, count, single, 0)

    @pl.when(f == 0)
    def _():
        def gather(r):
            t = tok_ref[0, start + r]
            xg_ref[pl.ds(r, 1), :] = x_ref[pl.ds(t, 1), :]
        row_loop(gather)

        def cast(j, carry):
            rows = pl.ds(pl.multiple_of(j * rt, rt), rt)
            xb_ref[rows, :] = xg_ref[rows, :].astype(MXU_DTYPE)
            return carry
        lax.fori_loop(0, n_tiles, cast, 0)

    def tile(j, carry):
        rows = pl.ds(pl.multiple_of(j * rt, rt), rt)
        y = _swiglu_tile(xb_ref[rows, :], w1_ref, w3_ref, w2_ref)

        @pl.when(f == 0)
        def _():
            y_ref[rows, :] = y

        @pl.when(f != 0)
        def _():
            y_ref[rows, :] += y
        return carry
    lax.fori_loop(0, n_tiles, tile, 0)

    @pl.when(f == last_f)
    def _():
        def updated(r):
            t = tok_ref[0, start + r]
            return t, o_ref[pl.ds(t, 1), :] + gs_ref[0, start + r] * y_ref[pl.ds(r, 1), :]

        def group(j, carry):
            rows = [updated(j * ROW_UNROLL + u) for u in range(ROW_UNROLL)]
            for t, row in rows:
                o_ref[pl.ds(t, 1), :] = row
            return carry

        def single(r, carry):
            t, row = updated(r)
            o_ref[pl.ds(t, 1), :] = row
            return carry
        lax.fori_loop(0, count // ROW_UNROLL, group, 0)
        lax.fori_loop((count // ROW_UNROLL) * ROW_UNROLL, count, single, 0)

    @pl.when((e == pl.num_programs(1) - 1) & (f == last_f))
    def _():
        o_ref[...] = _layer_norm_rows(DEEPNORM_ALPHA * x_ref[...] + o_ref[...], g_ref[...], b_ref[...])


def moe_residual_ln(x, routes, w1, w3, w2, g, b, tm, tf, rt=128):
    m, d = x.shape
    n_experts, _, ff = w1.shape
    n_chunks = m // tm
    ids = routes[:, ROUTE_IDS:ROUTE_IDS + TOP_K].astype(i32).reshape(n_chunks, tm * TOP_K)
    gts = routes[:, ROUTE_GATES:ROUTE_GATES + TOP_K].reshape(n_chunks, tm * TOP_K)
    order = jnp.argsort(ids, axis=1, stable=True).astype(i32)
    tok = order // TOP_K
    gs = jnp.take_along_axis(gts, order, axis=1)
    counts = jnp.sum(ids[:, :, None] == jnp.arange(n_experts, dtype=i32)[None, None, :], axis=1, dtype=i32)
    offs = jnp.concatenate([jnp.zeros((n_chunks, 1), i32), jnp.cumsum(counts, axis=1, dtype=i32)], axis=1)
    smem = lambda width: pl.BlockSpec((None, 1, width), lambda c, e, f: (c, 0, 0), memory_space=pltpu.SMEM)
    return pl.pallas_call(
        functools.partial(_moe_ln_kernel, rt=rt),
        grid=(n_chunks, n_experts, ff // tf),
        in_specs=[smem(tm * TOP_K), smem(tm * TOP_K), smem(n_experts + 1),
                  pl.BlockSpec((tm, d), lambda c, e, f: (c, 0), pipeline_mode=pl.Buffered(1)),
                  pl.BlockSpec((None, d, tf), lambda c, e, f: (e, 0, f)),
                  pl.BlockSpec((None, d, tf), lambda c, e, f: (e, 0, f)),
                  pl.BlockSpec((None, tf, d), lambda c, e, f: (e, f, 0)),
                  pl.BlockSpec((1, d), lambda c, e, f: (0, 0)), pl.BlockSpec((1, d), lambda c, e, f: (0, 0))],
        out_specs=pl.BlockSpec((tm, d), lambda c, e, f: (c, 0), pipeline_mode=pl.Buffered(1)),
        out_shape=jax.ShapeDtypeStruct((m, d), f32),
        scratch_shapes=[pltpu.VMEM((tm, d), f32), pltpu.VMEM((tm, d), MXU_DTYPE), pltpu.VMEM((tm, d), f32)],
        compiler_params=_params("arbitrary", "arbitrary", "arbitrary"),
        name="moe_residual_ln",
    )(tok[:, None, :], gs[:, None, :], offs[:, None, :], x, w1, w3, w2, g.reshape(1, d), b.reshape(1, d))


def _flash_reset(m_ref, acc_ref):
    m_ref[...] = jnp.full(m_ref.shape, NEG, f32)
    acc_ref[...] = jnp.zeros(acc_ref.shape, f32)


def _flash_rows(q, kt, bias, m_ref, alpha_ref, p_ref, rows):
    s = _mm(q, kt)
    if bias is not None:
        s = s + bias
    m_prev = m_ref[rows, :]
    m_next = jnp.maximum(m_prev, jnp.max(s, axis=1, keepdims=True))
    alpha_ref[rows, :] = jnp.exp2(m_prev - m_next)
    m_ref[rows, :] = m_next
    p_ref[rows, :s.shape[1]] = jnp.exp2(s - _lane_tile(m_next, s.shape[1] // LANES)).astype(p_ref.dtype)


def _flash_accumulate(v_aug, alpha_ref, p_ref, acc_ref):
    alpha = _lane_tile(alpha_ref[...], acc_ref.shape[-1] // LANES)
    acc_ref[...] = alpha * acc_ref[...] + _mm(p_ref[:, :v_aug.shape[0]], v_aug)


def _skewed_chunks(n, score, fold, carry):
    def pair(j, carry):
        c = 2 * j
        fold(c - 1, 1)
        carry = score(c, 0, carry)
        fold(c, 0)
        return score(c + 1, 1, carry)

    def single(c, carry):
        fold(c - 1, 1)
        return score(c, 0, carry)

    carry = lax.fori_loop(0, n // 2, pair, carry)
    return lax.fori_loop(2 * (n // 2), n, single, carry)


def _fold_last(n, fold):
    for buf in range(2):
        pl.when((n - 1) % 2 == buf)(functools.partial(fold, n - 1, buf))


def _lane_tile(x, reps):
    return x if reps == 1 else jnp.concatenate([x] * reps, axis=1)


def _row_slices(n_rows):
    return [slice(r * Q_BLOCK, (r + 1) * Q_BLOCK) for r in range(n_rows // Q_BLOCK)]


def _dsa_kernel(qi_ref, wi_ref, kit_ref, qa_ref, kat_ref, va_ref, o_ref,
                skey_ref, wrep_ref, m_ref, alpha_ref, acc_ref, p_ref, qh_ref, *, topk, kc):
    qb = Q_BLOCK
    i = pl.program_id(1)
    q0 = i * qb
    n_chunks = (q0 + qb + kc - 1) // kc
    t_row = q0 + lax.broadcasted_iota(i32, (qb, kc), 0)
    col = lax.broadcasted_iota(i32, (qb, kc), 1)
    wi = wi_ref[...]
    qi = qi_ref[...]
    qis = [qi[:, h * IDX_DIM:(h + 1) * IDX_DIM] for h in range(IDX_HEADS)]
    qa = qa_ref[...]
    for h in range(A_HEADS):
        qh_ref[h * qb:(h + 1) * qb, :] = qa[:, h * HEAD_DIM:(h + 1) * HEAD_DIM]
    for h in range(IDX_HEADS):
        wrep_ref[h] = jnp.broadcast_to(wi[:, h:h + 1], (qb, kc))

    def score_body(c, carry):
        off = pl.multiple_of(c * kc, kc)
        kt = kit_ref[:, pl.ds(off, kc)]
        s = jnp.zeros((qb, kc), f32)
        for h in range(IDX_HEADS):
            s = s + wrep_ref[h] * jnp.maximum(_mm(qis[h], kt), 0.0)
        s = jnp.where(col + off <= t_row, s, NEG)
        bits = pltpu.bitcast(s, i32)
        skey_ref[:, pl.ds(off, kc)] = jnp.where(bits < 0, bits ^ 0x7FFFFFFF, bits)
        return carry

    n_strips = (q0 + qb + COUNT_STRIP - 1) // COUNT_STRIP
    lax.fori_loop(0, n_strips * (COUNT_STRIP // kc), score_body, 0)

    def count_ge(cand):
        def body(c, acc):
            off = pl.multiple_of(c * COUNT_STRIP, COUNT_STRIP)
            for j in range(COUNT_STRIP // LANES):
                acc = acc + jnp.where(skey_ref[:, pl.ds(off + j * LANES, LANES)] >= cand, 1.0, 0.0)
            return acc
        acc = lax.fori_loop(0, n_strips, body, jnp.zeros((qb, LANES), f32))
        return jnp.sum(acc, axis=1, keepdims=True)

    thr = jnp.where(count_ge(jnp.zeros((qb, 1), i32)) >= topk, 0, INT_MIN).astype(i32)

    def bisect(b, thr):
        cand = thr + jnp.left_shift(jnp.int32(1), 30 - b)
        return jnp.where(count_ge(cand) >= topk, cand, thr)

    thr = lax.fori_loop(0, 31, bisect, thr)
    quota = topk - count_ge(thr + 1)

    before = (lax.broadcasted_iota(i32, (kc, kc), 0) < lax.broadcasted_iota(i32, (kc, kc), 1))
    before = jnp.where(before, 1.0, 0.0).astype(MXU_DTYPE)
    _flash_reset(m_ref, acc_ref)
    alpha_ref[...] = jnp.zeros(alpha_ref.shape, f32)
    p_ref[...] = jnp.zeros(p_ref.shape, p_ref.dtype)
    heads = _row_slices(A_HEADS * qb)

    def fold(c, buf):
        off = pl.multiple_of(jnp.maximum(c, 0) * kc, kc)
        _flash_accumulate(va_ref[pl.ds(off, kc), :], alpha_ref.at[buf], p_ref.at[buf], acc_ref)

    def score(c, buf, ties_seen):
        off = pl.multiple_of(c * kc, kc)
        key = skey_ref[:, pl.ds(off, kc)]
        eq = key == thr
        eqf = jnp.where(eq, 1.0, 0.0)
        rank = ties_seen + _mm(eqf.astype(MXU_DTYPE), before)
        sel = ((key > thr) | (eq & (rank < quota))) & (col + off <= t_row)
        bias = jnp.where(sel, 0.0, NEG)
        kt = kat_ref[:, pl.ds(off, kc)]
        for rows in heads:
            _flash_rows(qh_ref[rows, :], kt, bias, m_ref, alpha_ref.at[buf], p_ref.at[buf], rows)
        return ties_seen + jnp.sum(eqf, axis=1, keepdims=True)

    _skewed_chunks(n_chunks, score, fold, jnp.zeros((qb, 1), f32))
    _fold_last(n_chunks, fold)
    acc = acc_ref[...]
    o = acc[:, :HEAD_DIM] / acc[:, HEAD_DIM:]
    for h in range(A_HEADS):
        o_ref[:, h * HEAD_DIM:(h + 1) * HEAD_DIM] = o[h * qb:(h + 1) * qb, :]


QPK_QA, QPK_QIDX, QPK_QB = 0, 2, 3
KPK_KA, KPK_KIDX, KPK_KSLC, KPK_KWIN = 0, 1, 2, 4
VPK_VA, VPK_VSLC, VPK_VWIN = 0, 1, 3
OUT_NSA = 2


def dsa_attention(wi, qpk, kpk, vpk, topk, kc=512):
    bsz, seq, _ = qpk.shape
    nb = seq // Q_BLOCK
    rows = A_HEADS * Q_BLOCK
    width = A_HEADS * HEAD_DIM
    kern = functools.partial(_dsa_kernel, topk=topk, kc=kc)
    return pl.pallas_call(
        kern,
        grid=(bsz, nb),
        in_specs=[pl.BlockSpec((None, Q_BLOCK, IDX_HEADS * IDX_DIM), lambda b, i: (b, i, QPK_QIDX)),
                  pl.BlockSpec((None, Q_BLOCK, IDX_HEADS), lambda b, i: (b, i, 0)),
                  pl.BlockSpec((None, IDX_DIM, seq), lambda b, i: (b, KPK_KIDX, 0)),
                  pl.BlockSpec((None, Q_BLOCK, width), lambda b, i: (b, i, QPK_QA)),
                  pl.BlockSpec((None, HEAD_DIM, seq), lambda b, i: (b, KPK_KA, 0)),
                  pl.BlockSpec((None, seq, 2 * HEAD_DIM), lambda b, i: (b, 0, VPK_VA))],
        out_specs=pl.BlockSpec((None, Q_BLOCK, width), lambda b, i: (b, i, 0)),
        out_shape=jax.ShapeDtypeStruct((bsz, seq, 2 * width), f32),
        scratch_shapes=[pltpu.VMEM((Q_BLOCK, seq), i32), pltpu.VMEM((IDX_HEADS, Q_BLOCK, kc), f32),
                        pltpu.VMEM((rows, LANES), f32), pltpu.VMEM((2, rows, LANES), f32),
                        pltpu.VMEM((rows, 2 * HEAD_DIM), f32), pltpu.VMEM((2, rows, kc), MXU_DTYPE),
                        pltpu.VMEM((rows, HEAD_DIM), MXU_DTYPE)],
        compiler_params=_params("parallel", "arbitrary"),
        name="dsa_attention",
    )(qpk, wi, kpk, qpk, kpk, vpk)


def _compress_kernel(x_ref, pe_ref, w1_ref, w2_ref, o_ref):
    n_rows = o_ref.shape[1]
    parts = CMP_LEN // CMP_STRIDE
    acc = [[jnp.zeros((n_rows, CMP_HIDDEN), f32) for _ in range(parts)] for _ in range(B_KV_GROUPS)]
    for t in range(CMP_STRIDE):
        tok = x_ref[pl.ds(t, n_rows, stride=CMP_STRIDE), :]
        for g in range(B_KV_GROUPS):
            tok_g = tok[:, g * HEAD_DIM:(g + 1) * HEAD_DIM]
            for part in range(parts):
                l = part * CMP_STRIDE + t
                lhs = (tok_g + pe_ref[l:l + 1, :]).astype(MXU_DTYPE)
                acc[g][part] = acc[g][part] + _mm(lhs, w1_ref[l * HEAD_DIM:(l + 1) * HEAD_DIM, :])
    for g in range(B_KV_GROUPS):
        pre = acc[g][0] + pltpu.roll(acc[g][1], shift=n_rows - 1, axis=0)
        hid = pre * jax.nn.sigmoid(pre)
        o_ref[g] = _mm(hid.astype(MXU_DTYPE), w2_ref[...])


def nsa_compress(cmp, pe, w1, w2):
    bsz, seq, _ = cmp.shape
    n_rows = seq // CMP_STRIDE
    width = B_KV_GROUPS * HEAD_DIM
    return pl.pallas_call(
        _compress_kernel,
        grid=(bsz, 2),
        in_specs=[pl.BlockSpec((None, seq, width), lambda b, s: (b, 0, s)),
                  pl.BlockSpec((None, CMP_LEN, HEAD_DIM), lambda b, s: (s, 0, 0)),
                  pl.BlockSpec((None, CMP_LEN * HEAD_DIM, CMP_HIDDEN), lambda b, s: (s, 0, 0)),
                  pl.BlockSpec((None, CMP_HIDDEN, HEAD_DIM), lambda b, s: (s, 0, 0))],
        out_specs=pl.BlockSpec((None, None, B_KV_GROUPS, n_rows, HEAD_DIM), lambda b, s: (b, s, 0, 0, 0)),
        out_shape=jax.ShapeDtypeStruct((bsz, 2, B_KV_GROUPS, n_rows, HEAD_DIM), f32),
        compiler_params=_params("parallel", "parallel"),
        name="nsa_compress",
    )(cmp, pe, w1, w2)


def _nsa_kernel(qin_ref, graw_ref, kct_ref, vc_ref, kst_ref, vs_ref, kwt_ref, vw_ref, ovl_ref, _, o_ref,
                m_ref, alpha_ref, acc_ref, p_ref, pw_ref, q_ref, imp_ref, *, n_slc, n_sel, kc):
    qb = Q_BLOCK
    i = pl.program_id(2)
    q0 = i * qb
    n_cmp = kct_ref.shape[1]
    heads = _row_slices(B_REP * qb)
    qin = qin_ref[...]
    for r in range(B_REP):
        q_ref[r * qb:(r + 1) * qb, :] = qin[:, r * HEAD_DIM:(r + 1) * HEAD_DIM]

    def compressed(width):
        t_c = q0 + lax.broadcasted_iota(i32, (qb, width), 0)
        cmp_end = lax.broadcasted_iota(i32, (qb, width), 1) * CMP_STRIDE + (CMP_LEN - 1)
        vis = cmp_end <= t_c
        kct = kct_ref[:, :width]
        p_sum = jnp.zeros((qb, width), f32)
        for rows in heads:
            lc = jnp.where(vis, _mm(q_ref[rows, :], kct), NEG)
            ec = jnp.where(vis, jnp.exp2(lc - jnp.max(lc, axis=1, keepdims=True)), 0.0)
            den = jnp.sum(ec, axis=1, keepdims=True)
            p_c = ec / jnp.where(den > 0.0, den, 1.0)
            p_sum = p_sum + p_c
            acc_ref[rows, :HEAD_DIM] = _mm(p_c.astype(MXU_DTYPE), vc_ref[:width, :])
        imp_ref[...] = jnp.dot(p_sum, ovl_ref[:width, :], preferred_element_type=f32,
                               precision=lax.Precision.HIGHEST)

    if n_cmp % (2 * LANES) == 0:
        first_half_only = (q0 + qb) * 2 <= n_cmp * CMP_STRIDE
        pl.when(first_half_only)(functools.partial(compressed, n_cmp // 2))
        pl.when(jnp.logical_not(first_half_only))(functools.partial(compressed, n_cmp))
    else:
        compressed(n_cmp)
    o_c = acc_ref[:, :HEAD_DIM]

    imp = imp_ref[...].T
    t_q = q0 + lax.broadcasted_iota(i32, (LANES, qb), 1)
    blk = lax.broadcasted_iota(i32, (LANES, qb), 0)
    blk_t = t_q // SLC_LEN
    forced = (blk == 0) | (blk == blk_t) | (blk == blk_t - 1)
    imp = jnp.where(forced, FORCED_BOOST, imp)
    imp = jnp.where(blk * SLC_LEN <= t_q, imp, NEG)
    imp = jnp.where(blk < n_slc, imp, -jnp.inf)
    blk_f = blk.astype(f32)

    def pick(_, carry):
        imp, selm = carry
        best = jnp.max(imp, axis=0, keepdims=True)
        first = jnp.min(jnp.where(imp == best, blk_f, float(LANES)), axis=0, keepdims=True)
        hit = blk_f == first
        return jnp.where(hit, -jnp.inf, imp), jnp.where(hit, 1.0, selm)

    _, selm = lax.fori_loop(0, n_sel, pick, (imp, jnp.zeros((LANES, qb), f32)))
    selm = selm.T.astype(MXU_DTYPE)

    t_k = q0 + lax.broadcasted_iota(i32, (qb, kc), 0)
    col = lax.broadcasted_iota(i32, (qb, kc), 1)
    exp_row = lax.broadcasted_iota(i32, (LANES, kc), 0)
    exp_col = lax.broadcasted_iota(i32, (LANES, kc), 1)
    _flash_reset(m_ref, acc_ref)
    alpha_ref[...] = jnp.zeros(alpha_ref.shape, f32)
    p_ref[...] = jnp.zeros(p_ref.shape, p_ref.dtype)

    def fold(c, buf):
        off = pl.multiple_of(jnp.maximum(c, 0) * kc, kc)
        _flash_accumulate(vs_ref[pl.ds(off, kc), :], alpha_ref.at[buf], p_ref.at[buf], acc_ref)

    def score(c, buf, carry):
        off = pl.multiple_of(c * kc, kc)
        expand = jnp.where(exp_row == (exp_col + off) // SLC_LEN, 1.0, 0.0).astype(MXU_DTYPE)
        sel = (_mm(selm, expand) > 0.5) & (col + off <= t_k)
        bias = jnp.where(sel, 0.0, NEG)
        kt = kst_ref[:, pl.ds(off, kc)]
        for rows in heads:
            _flash_rows(q_ref[rows, :], kt, bias, m_ref, alpha_ref.at[buf], p_ref.at[buf], rows)
        return carry

    n_slc_chunks = (q0 + qb + kc - 1) // kc
    _skewed_chunks(n_slc_chunks, score, fold, 0)
    _fold_last(n_slc_chunks, fold)
    acc = acc_ref[...]
    o_s = acc[:, :HEAD_DIM] / acc[:, HEAD_DIM:]

    slab = WINDOW + qb
    w0 = pl.multiple_of(jnp.maximum(q0 - WINDOW, 0), qb)
    dist = (q0 + lax.broadcasted_iota(i32, (qb, slab), 0)) - (w0 + lax.broadcasted_iota(i32, (qb, slab), 1))
    bias = jnp.where((dist >= 0) & (dist < WINDOW), 0.0, NEG)
    kt = kwt_ref[:, pl.ds(w0, slab)]
    for rows in heads:
        s = _mm(q_ref[rows, :], kt) + bias
        pw_ref[rows, :] = jnp.exp2(s - jnp.max(s, axis=1, keepdims=True)).astype(pw_ref.dtype)
    acc = _mm(pw_ref[...], vw_ref[pl.ds(w0, slab), :])
    o_w = acc[:, :HEAD_DIM] / acc[:, HEAD_DIM:]

    gates = jax.nn.sigmoid(graw_ref[...])
    for r, rows in enumerate(heads):
        g_c, g_s, g_w = (gates[:, 3 * r + n:3 * r + n + 1] for n in range(3))
        o_ref[:, r * HEAD_DIM:(r + 1) * HEAD_DIM] = g_c * o_c[rows] + g_s * o_s[rows] + g_w * o_w[rows]


def nsa_attention(qpk, graw, kct, vc, kpk, vpk, ovl, out_ab, n_slc, n_sel, kc=512):
    bsz, seq, _ = qpk.shape
    groups = B_KV_GROUPS
    nb = seq // Q_BLOCK
    rows = B_REP * Q_BLOCK
    width = B_REP * HEAD_DIM
    n_cmp = kct.shape[-1]
    kern = functools.partial(_nsa_kernel, n_slc=n_slc, n_sel=n_sel, kc=kc)
    per_bg = lambda b, g, i: (b, g, 0, 0)
    return pl.pallas_call(
        kern,
        grid=(bsz, groups, nb),
        in_specs=[pl.BlockSpec((None, Q_BLOCK, width), lambda b, g, i: (b, i, QPK_QB + g)),
                  pl.BlockSpec((None, None, Q_BLOCK, B_REP * 3), lambda b, g, i: (b, g, i, 0)),
                  pl.BlockSpec((None, None, HEAD_DIM, n_cmp), per_bg),
                  pl.BlockSpec((None, None, n_cmp, HEAD_DIM), per_bg),
                  pl.BlockSpec((None, HEAD_DIM, seq), lambda b, g, i: (b, KPK_KSLC + g, 0)),
                  pl.BlockSpec((None, seq, 2 * HEAD_DIM), lambda b, g, i: (b, 0, VPK_VSLC + g)),
                  pl.BlockSpec((None, HEAD_DIM, seq), lambda b, g, i: (b, KPK_KWIN + g, 0)),
                  pl.BlockSpec((None, seq, 2 * HEAD_DIM), lambda b, g, i: (b, 0, VPK_VWIN + g)),
                  pl.BlockSpec((n_cmp, LANES), lambda b, g, i: (0, 0)),
                  pl.BlockSpec(memory_space=pl.ANY)],
        out_specs=pl.BlockSpec((None, Q_BLOCK, width), lambda b, g, i: (b, i, OUT_NSA + g)),
        out_shape=jax.ShapeDtypeStruct(out_ab.shape, f32),
        input_output_aliases={9: 0},
        scratch_shapes=[pltpu.VMEM((rows, LANES), f32), pltpu.VMEM((2, rows, LANES), f32),
                        pltpu.VMEM((rows, 2 * HEAD_DIM), f32), pltpu.VMEM((2, rows, kc), MXU_DTYPE),
                        pltpu.VMEM((rows, WINDOW + Q_BLOCK), MXU_DTYPE), pltpu.VMEM((rows, HEAD_DIM), MXU_DTYPE),
                        pltpu.VMEM((Q_BLOCK, LANES), f32)],
        compiler_params=_params("parallel", "parallel", "arbitrary"),
        name="nsa_attention",
    )(qpk, graw, kct, vc, kpk, vpk, kpk, vpk, ovl, out_ab)


def _diff_kernel(lam_ref, qin_ref, kt_ref, v_ref, g_ref, o_ref, m_ref, alpha_ref, acc_ref, p_ref, q_ref,
                 *, tq, kc, out_scale):
    i = pl.program_id(2)
    q0 = i * tq
    dv = v_ref.shape[-1] // 2
    groups = _row_slices(tq)
    qin = qin_ref[...]
    for half in range(2):
        q_ref[half] = qin[:, half * HEAD_DIM:(half + 1) * HEAD_DIM]
    for half in range(2):
        _flash_reset(m_ref.at[half], acc_ref.at[half])
    alpha_ref[...] = jnp.zeros(alpha_ref.shape, f32)
    p_ref[...] = jnp.zeros(p_ref.shape, p_ref.dtype)

    def fold(c, buf, sub=(0, kc // tq)):
        off = pl.multiple_of(jnp.maximum(c, 0) * kc + sub[0] * tq, tq)
        v = v_ref[pl.ds(off, sub[1] * tq), :]
        for half in range(2):
            _flash_accumulate(v, alpha_ref.at[buf, half], p_ref.at[buf, half], acc_ref.at[half])

    def score(c, buf, masked, sub=(0, kc // tq)):
        off = pl.multiple_of(c * kc + sub[0] * tq, tq)
        width = sub[1] * tq
        for half in range(2):
            kt = kt_ref[half * HEAD_DIM:(half + 1) * HEAD_DIM, pl.ds(off, width)]
            for r, rows in enumerate(groups):
                bias = None
                if masked:
                    key = off + lax.broadcasted_iota(i32, (Q_BLOCK, width), 1)
                    t = q0 + r * Q_BLOCK + lax.broadcasted_iota(i32, (Q_BLOCK, width), 0)
                    bias = jnp.where(key <= t, 0.0, NEG)
                _flash_rows(q_ref[half, rows, :], kt, bias, m_ref.at[half], alpha_ref.at[buf, half],
                            p_ref.at[buf, half], rows)

    def score_full(c, buf, carry):
        score(c, buf, False)
        return carry

    n_full = q0 // kc
    _skewed_chunks(n_full, score_full, fold, 0)

    own = (q0 - n_full * kc) // tq

    def finish(buf, own_piece):
        fold(n_full - 1, 1 - buf)
        if own_piece:
            score(n_full, buf, False, (0, own_piece))
            fold(n_full, buf, (0, own_piece))
            buf = 1 - buf
        score(n_full, buf, True, (own_piece, 1))
        fold(n_full, buf, (own_piece, 1))

    for buf in range(2):
        for own_piece in range(kc // tq):
            pl.when((n_full % 2 == buf) & (own == own_piece))(functools.partial(finish, buf, own_piece))
    a1 = acc_ref[0]
    a2 = acc_ref[1]
    o = a1[:, :dv] / a1[:, dv:] - lam_ref[0] * (a2[:, :dv] / a2[:, dv:])
    o = o * lax.rsqrt(jnp.mean(o * o, axis=-1, keepdims=True) + LN_EPS) * g_ref[...]
    o_ref[...] = o * out_scale


def diff_attention(lam, q, kt, v, subln_g, out_scale, tq=512, kc=1024):
    bsz, seq, width = q.shape
    dv = 2 * HEAD_DIM
    heads = width // dv
    tq = min(tq, seq)
    kc = min(kc, seq)
    assert kc % tq == 0 and seq % kc == 0
    kern = functools.partial(_diff_kernel, tq=tq, kc=kc, out_scale=out_scale)
    return pl.pallas_call(
        kern,
        grid=(bsz, heads, seq // tq),
        in_specs=[pl.BlockSpec(memory_space=pltpu.SMEM),
                  pl.BlockSpec((None, tq, dv), lambda b, h, i: (b, i, h)),
                  pl.BlockSpec((None, dv, seq), lambda b, h, i: (b, h, 0)),
                  pl.BlockSpec((None, seq, 2 * dv), lambda b, h, i: (b, 0, h)),
                  pl.BlockSpec((1, dv), lambda b, h, i: (0, 0))],
        out_specs=pl.BlockSpec((None, tq, dv), lambda b, h, i: (b, i, h)),
        out_shape=jax.ShapeDtypeStruct((bsz, seq, heads * dv), f32),
        scratch_shapes=[pltpu.VMEM((2, tq, LANES), f32), pltpu.VMEM((2, 2, tq, LANES), f32),
                        pltpu.VMEM((2, tq, 2 * dv), f32), pltpu.VMEM((2, 2, tq, kc), MXU_DTYPE),
                        pltpu.VMEM((2, tq, HEAD_DIM), MXU_DTYPE)],
        compiler_params=_params("parallel", "parallel", "arbitrary"),
        name="diff_attention",
    )(lam, q, kt, v, subln_g.reshape(1, dv))


def _rope_tables(positions):
    inv_freq = ROPE_THETA ** (-jnp.arange(0, ROT_DIM, 2, dtype=f32) / ROT_DIM)
    ang = positions.astype(f32)[..., None] * inv_freq
    return jnp.cos(ang), jnp.sin(ang)


def _apply_rope(x, cos, sin):
    shape = cos.shape[:2] + (1,) * (x.ndim - 3) + cos.shape[-1:]
    c = cos.reshape(shape)
    s = sin.reshape(shape)
    half = ROT_DIM // 2
    x1, x2 = x[..., :half], x[..., half:ROT_DIM]
    return jnp.concatenate([x1 * c - x2 * s, x2 * c + x1 * s, x[..., ROT_DIM:]], axis=-1)


def _rope_coefficients(cos, sin):
    half = ROT_DIM // 2
    spread = np.zeros((ROT_DIM, 3 * LANES), np.float32)
    offset = np.zeros((3 * LANES,), np.float32)
    for lane in range(LANES):
        d = lane % HEAD_DIM
        if d >= ROT_DIM:
            offset[lane] = 1.0
        elif d < half:
            spread[d, lane], spread[half + d, LANES + lane] = 1.0, -1.0
        else:
            spread[d - half, lane], spread[d, 2 * LANES + lane] = 1.0, 1.0
    table = jnp.concatenate([cos, sin], axis=-1)
    return jnp.dot(table, spread, precision=lax.Precision.HIGHEST) + offset


def _ab_weight_columns(w_in):
    widths = dict(AB_LAYOUT)
    starts = dict(zip(widths, np.cumsum([0] + [w for _, w in AB_LAYOUT[:-1]]).tolist()))
    cols = lambda names: np.concatenate([np.arange(starts[n], starts[n] + widths[n]) for n in names])
    first = ("q_a", "q_idx", "q_b", "k_a", "k_idx", "k_slc", "k_win", "v_a", "v_slc", "v_win", "gate_b", "w_idx")
    n_first = sum(widths[n] for n in first)
    pad = AB_CMP_TILE * LANES - n_first
    assert 0 <= pad < HEAD_DIM
    w = jnp.concatenate([w_in[:, cols(first)], jnp.zeros((w_in.shape[0], pad), w_in.dtype),
                         w_in[:, cols(("k_cmp", "v_cmp"))]], axis=1)
    assert w.shape[1] == AB_TILES * LANES
    return w.astype(MXU_DTYPE)


def _overlap_matrix(n_cmp_rows, n_slc):
    c_start = np.arange(n_cmp_rows) * CMP_STRIDE
    s_start = np.arange(LANES) * SLC_LEN
    ovl = (c_start[:, None] < s_start[None, :] + SLC_LEN) & (c_start[:, None] + CMP_LEN > s_start[None, :])
    ovl = ovl & (np.arange(LANES)[None, :] < n_slc)
    return jnp.asarray(ovl.astype(np.float32))


def _ab_mixer(x2, bsz, seq, positions, coef, w_in, pe_k, pe_v, ck1, ck2, cv1, cv2):
    qpk, kpk, vpk, misc, cmp = project_pack_ab(x2.reshape(bsz, seq, -1), _ab_weight_columns(w_in), coef)
    gate_w = B_HEADS * 3
    g_b = misc[..., HEAD_DIM:HEAD_DIM + gate_w]
    w_idx = misc[..., HEAD_DIM + gate_w:HEAD_DIM + gate_w + IDX_HEADS] * (IDX_HEADS * IDX_DIM) ** -0.5

    out_ab = dsa_attention(w_idx, qpk, kpk, vpk, topk=min(DSA_TOPK, seq // 4))

    groups = B_KV_GROUPS
    n_rows = seq // CMP_STRIDE
    n_cmp = (seq - CMP_LEN) // CMP_STRIDE + 1
    assert n_cmp == n_rows - 1
    n_slc = seq // SLC_LEN
    assert n_slc <= LANES

    kv_c = nsa_compress(cmp, jnp.stack([pe_k, pe_v]), jnp.stack([ck1, cv1]).astype(MXU_DTYPE),
                        jnp.stack([ck2, cv2]).astype(MXU_DTYPE))
    k_c, v_c = kv_c[:, 0], kv_c[:, 1]
    cmp_end = jnp.minimum(jnp.arange(n_rows) * CMP_STRIDE + CMP_LEN - 1, seq - 1)
    cos_c, sin_c = _rope_tables(positions[:, cmp_end])
    k_c = _apply_rope(k_c.transpose(0, 2, 1, 3), cos_c, sin_c)
    k_c = k_c.transpose(0, 2, 3, 1).astype(MXU_DTYPE)
    v_c = v_c.astype(MXU_DTYPE)

    graw = g_b.reshape(bsz, seq, groups, B_REP * 3).transpose(0, 2, 1, 3)
    out_ab = nsa_attention(qpk, graw, k_c, v_c, kpk, vpk, _overlap_matrix(n_rows, n_slc), out_ab,
                           n_slc=n_slc, n_sel=min(SLC_TOPN, n_slc))
    return out_ab.reshape(bsz * seq, (A_HEADS + B_HEADS) * HEAD_DIM)


def _diff_mixer(x2, bsz, seq, coef, w_in, lq1, lk1, lq2, lk2, subln_g, lam_init):
    q, kt, v = project_pack_c(x2.reshape(bsz, seq, -1), w_in.astype(MXU_DTYPE), coef)
    lam =(jnp.exp(jnp.sum(lq1 * lk1)) - jnp.exp(jnp.sum(lq2 * lk2)) + lam_init).reshape(1).astype(f32)
    o = diff_attention(lam, q, kt, v, subln_g, 1.0 - lam_init)
    return o.reshape(bsz * seq, C_HEADS * 2 * HEAD_DIM)


def kernel(x, positions, ab_w_in, cmp_pe_k, cmp_pe_v, cmp_k_w1, cmp_k_w2, cmp_v_w1, cmp_v_w2, ab_w_out, ln_ab_g, ln_ab_b, ffn_w1, ffn_w3, ffn_w2, ln_ffn_g, ln_ffn_b, c_w_in, lambda_q1, lambda_k1, lambda_q2, lambda_k2, c_subln_g, c_w_out, ln_c_g, ln_c_b, router_w, moe_w1, moe_w3, moe_w2, ln_moe_g, ln_moe_b):
    bsz, seq, d = x.shape
    assert seq % COUNT_STRIP == 0 and seq >= WINDOW + Q_BLOCK and d == D_MODEL
    coef = _rope_coefficients(*_rope_tables(positions))
    x2 = x.reshape(bsz * seq, d)
    for layer in range(DEPTH):
        i = layer // 2
        if layer % 2 == 0:
            o = _ab_mixer(x2, bsz, seq, positions, coef, ab_w_in[i], cmp_pe_k[i], cmp_pe_v[i],
                          cmp_k_w1[i], cmp_k_w2[i], cmp_v_w1[i], cmp_v_w2[i])
            x2 = project_residual_ln(o, ab_w_out[i].astype(MXU_DTYPE), x2, ln_ab_g[i], ln_ab_b[i])
            x2 = ffn_residual_ln(x2, ffn_w1[i].astype(MXU_DTYPE), ffn_w3[i].astype(MXU_DTYPE),
                                 ffn_w2[i].astype(MXU_DTYPE), ln_ffn_g[i], ln_ffn_b[i], tm=512, tf=ffn_w1.shape[-1])
        else:
            lam_init = 0.8 - 0.6 * math.exp(-0.3 * layer)
            o = _diff_mixer(x2, bsz, seq, coef, c_w_in[i], lambda_q1[i], lambda_k1[i], lambda_q2[i],
                            lambda_k2[i], c_subln_g[i], lam_init)
            x2, routes = project_residual_ln(o, c_w_out[i].astype(MXU_DTYPE), x2, ln_c_g[i], ln_c_b[i],
                                             router_w=router_w[i])
            x2 = moe_residual_ln(x2, routes, moe_w1[i].astype(MXU_DTYPE), moe_w3[i].astype(MXU_DTYPE),
                                 moe_w2[i].astype(MXU_DTYPE), ln_moe_g[i], ln_moe_b[i], tm=min(2048, bsz * seq // 2), tf=896, rt=128)
    return x2.reshape(bsz, seq, d)
```

```python
import functools
import math

import numpy as np
import jax
import jax.numpy as jnp
from jax import lax
from jax.experimental import pallas as pl
from jax.experimental.pallas import tpu as pltpu

f32 = jnp.float32
i32 = jnp.int32
MXU_DTYPE = jnp.bfloat16
VMEM_LIMIT_BYTES = 56 * 1024 * 1024
LANES = 128

D_MODEL = 1024
DEPTH = 2
HEAD_DIM = 64
ROT_DIM = HEAD_DIM // 4
ROPE_THETA = 500000.0
Q_BLOCK = 128
NEG = -1e30
LN_EPS = 1e-5
A_HEADS = 8
IDX_HEADS = 4
IDX_DIM = 64
DSA_TOPK = 256
B_HEADS = 8
B_KV_GROUPS = 2
B_REP = B_HEADS // B_KV_GROUPS
CMP_LEN = 32
CMP_STRIDE = 16
CMP_HIDDEN = 128
SLC_LEN = 64
SLC_TOPN = 16
WINDOW = 512
FORCED_BOOST = 1e6
C_HEADS = 8
N_EXPERTS = 8
TOP_K = 2
DEEPNORM_ALPHA = (2 * DEPTH) ** 0.25
QK_SCALE = HEAD_DIM ** -0.5 * math.log2(math.e)
INT_MIN = -(2 ** 31)
COUNT_STRIP = 512
RANK_PIECE = 256
ROW_UNROLL = 4

AB_LAYOUT = (
    ("q_a", A_HEADS * HEAD_DIM), ("k_a", HEAD_DIM), ("v_a", HEAD_DIM),
    ("q_idx", IDX_HEADS * IDX_DIM), ("k_idx", IDX_DIM), ("w_idx", IDX_HEADS),
    ("q_b", B_HEADS * HEAD_DIM),
    ("k_cmp", B_KV_GROUPS * HEAD_DIM), ("v_cmp", B_KV_GROUPS * HEAD_DIM),
    ("k_slc", B_KV_GROUPS * HEAD_DIM), ("v_slc", B_KV_GROUPS * HEAD_DIM),
    ("k_win", B_KV_GROUPS * HEAD_DIM), ("v_win", B_KV_GROUPS * HEAD_DIM),
    ("gate_b", 3 * B_HEADS),
)


def _params(*sem):
    return pltpu.CompilerParams(dimension_semantics=sem, vmem_limit_bytes=VMEM_LIMIT_BYTES)


def _mm(a, b):
    return jnp.dot(a, b, preferred_element_type=f32)


def _layer_norm_rows(y, g, b):
    mu = jnp.mean(y, axis=-1, keepdims=True)
    yc = y - mu
    var = jnp.mean(yc * yc, axis=-1, keepdims=True)
    return yc * lax.rsqrt(var + LN_EPS) * g + b


def _rope_tile(x, keep, hi, lo):
    return x * keep + pltpu.roll(x, LANES - ROT_DIM // 2, 1) * hi + pltpu.roll(x, ROT_DIM // 2, 1) * lo


def _tile(h, j):
    return h[:, j * LANES:(j + 1) * LANES]


AB_Q_TILES = 10
AB_QIDX_TILES = (4, 5)
AB_K_TILES = 3
AB_V_HEADS = 5
AB_MISC_TILE = 15
AB_CMP_TILE = 16
AB_TILES = 18


def _proj_pack_ab_kernel(x_ref, w_ref, coef_ref, q_ref, kt_ref, v_ref, misc_ref, cmp_ref):
    h = _mm(x_ref[...].astype(MXU_DTYPE), w_ref[...])
    keep, hi, lo = (coef_ref[:, j * LANES:(j + 1) * LANES] for j in range(3))
    for j in range(AB_Q_TILES):
        t = _rope_tile(_tile(h, j), keep, hi, lo)
        q_ref[:, j * LANES:(j + 1) * LANES] = (t if j in AB_QIDX_TILES else t * QK_SCALE).astype(q_ref.dtype)
    k = jnp.concatenate([_rope_tile(_tile(h, AB_Q_TILES + j), keep, hi, lo) for j in range(AB_K_TILES)], axis=1)
    kt_ref[...] = k.T.astype(kt_ref.dtype)
    low = lax.broadcasted_iota(i32, keep.shape, 1) < HEAD_DIM
    for j in range(AB_V_HEADS):
        t = _tile(h, AB_Q_TILES + AB_K_TILES + j // 2)
        t = pltpu.roll(t, HEAD_DIM, 1) if j % 2 else t
        v_ref[:, j * LANES:(j + 1) * LANES] = jnp.where(low, t, 1.0).astype(v_ref.dtype)
    misc_ref[...] = _tile(h, AB_MISC_TILE)
    cmp_ref[...] = h[:, AB_CMP_TILE * LANES:AB_TILES * LANES]


def project_pack_ab(x, w, coef, tm=512):
    bsz, seq, d = x.shape
    row = lambda width: pl.BlockSpec((None, tm, width), lambda b, i: (b, i, 0))
    kt_rows = AB_K_TILES * LANES
    return pl.pallas_call(
        _proj_pack_ab_kernel,
        grid=(bsz, seq // tm),
        in_specs=[row(d), pl.BlockSpec((d, AB_TILES * LANES), lambda b, i: (0, 0)), row(3 * LANES)],
        out_specs=[row(AB_Q_TILES * LANES), pl.BlockSpec((None, kt_rows, tm), lambda b, i: (b, 0, i)),
                   row(AB_V_HEADS * LANES), row(LANES), row(2 * LANES)],
        out_shape=[jax.ShapeDtypeStruct((bsz, seq, AB_Q_TILES * LANES), MXU_DTYPE),
                   jax.ShapeDtypeStruct((bsz, kt_rows, seq), MXU_DTYPE),
                   jax.ShapeDtypeStruct((bsz, seq, AB_V_HEADS * LANES), MXU_DTYPE),
                   jax.ShapeDtypeStruct((bsz, seq, LANES), f32),
                   jax.ShapeDtypeStruct((bsz, seq, 2 * LANES), f32)],
        compiler_params=_params("parallel", "parallel"),
        name="project_pack_ab",
    )(x, w, coef)


def _proj_pack_c_kernel(x_ref, w_ref, coef_ref, q_ref, kt_ref, v_ref):
    h = _mm(x_ref[...].astype(MXU_DTYPE), w_ref[...])
    keep, hi, lo = (coef_ref[:, j * LANES:(j + 1) * LANES] for j in range(3))
    n = q_ref.shape[1] // LANES
    for j in range(n):
        q_ref[:, j * LANES:(j + 1) * LANES] = (_rope_tile(_tile(h, j), keep, hi, lo) * QK_SCALE).astype(q_ref.dtype)
    k = jnp.concatenate([_rope_tile(_tile(h, n + j), keep, hi, lo) for j in range(n)], axis=1)
    kt_ref[...] = k.T.astype(kt_ref.dtype)
    ones = jnp.ones(keep.shape, v_ref.dtype)
    for j in range(n):
        v_ref[:, 2 * j * LANES:(2 * j + 1) * LANES] = _tile(h, 2 * n + j).astype(v_ref.dtype)
        v_ref[:, (2 * j + 1) * LANES:(2 * j + 2) * LANES] = ones


def project_pack_c(x, w, coef, tm=512):
    bsz, seq, d = x.shape
    width = w.shape[1] // 3
    row = lambda cols: pl.BlockSpec((None, tm, cols), lambda b, i: (b, i, 0))
    return pl.pallas_call(
        _proj_pack_c_kernel,
        grid=(bsz, seq // tm),
        in_specs=[row(d), pl.BlockSpec((d, 3 * width), lambda b, i: (0, 0)), row(3 * LANES)],
        out_specs=[row(width), pl.BlockSpec((None, width, tm), lambda b, i: (b, 0, i)), row(2 * width)],
        out_shape=[jax.ShapeDtypeStruct((bsz, seq, width), MXU_DTYPE),
                   jax.ShapeDtypeStruct((bsz, width, seq), MXU_DTYPE),
                   jax.ShapeDtypeStruct((bsz, seq, 2 * width), MXU_DTYPE)],
        compiler_params=_params("parallel", "parallel"),
        name="project_pack_c",
    )(x, w, coef)


def _proj_ln_kernel(a_ref, w_ref, res_ref, g_ref, b_ref, o_ref):
    h = _mm(a_ref[...].astype(MXU_DTYPE), w_ref[...])
    o_ref[...] = _layer_norm_rows(DEEPNORM_ALPHA * res_ref[...] + h, g_ref[...], b_ref[...])


def project_residual_ln(a, w, res, g, b, tm=512):
    m, k = a.shape
    n = w.shape[1]
    return pl.pallas_call(
        _proj_ln_kernel,
        grid=(m // tm,),
        in_specs=[pl.BlockSpec((tm, k), lambda i: (i, 0)), pl.BlockSpec((k, n), lambda i: (0, 0)),
                  pl.BlockSpec((tm, n), lambda i: (i, 0)),
                  pl.BlockSpec((1, n), lambda i: (0, 0)), pl.BlockSpec((1, n), lambda i: (0, 0))],
        out_specs=pl.BlockSpec((tm, n), lambda i: (i, 0)),
        out_shape=jax.ShapeDtypeStruct((m, n), f32),
        compiler_params=_params("parallel"),
        name="project_residual_ln",
    )(a, w, res, g.reshape(1, n), b.reshape(1, n))


def _swiglu_tile(xb, w1_ref, w3_ref, w2_ref):
    a = _mm(xb, w1_ref[...])
    h = (a * jax.nn.sigmoid(a)) * _mm(xb, w3_ref[...])
    return _mm(h.astype(MXU_DTYPE), w2_ref[...])


def _ffn_ln_kernel(x_ref, w1_ref, w3_ref, w2_ref, g_ref, b_ref, o_ref, xb_ref, acc_ref):
    f = pl.program_id(1)

    @pl.when(f == 0)
    def _():
        xb_ref[...] = x_ref[...].astype(MXU_DTYPE)
        acc_ref[...] = jnp.zeros_like(acc_ref)

    acc_ref[...] += _swiglu_tile(xb_ref[...], w1_ref, w3_ref, w2_ref)

    @pl.when(f == pl.num_programs(1) - 1)
    def _():
        o_ref[...] = _layer_norm_rows(DEEPNORM_ALPHA * x_ref[...] + acc_ref[...], g_ref[...], b_ref[...])


def ffn_residual_ln(x, w1, w3, w2, g, b, tm, tf):
    m, d = x.shape
    ff = w1.shape[1]
    resident = dict(pipeline_mode=pl.Buffered(1)) if tf == ff else {}
    return pl.pallas_call(
        _ffn_ln_kernel,
        grid=(m // tm, ff // tf),
        in_specs=[pl.BlockSpec((tm, d), lambda i, f: (i, 0)),
                  pl.BlockSpec((d, tf), lambda i, f: (0, f), **resident),
                  pl.BlockSpec((d, tf), lambda i, f: (0, f), **resident),
                  pl.BlockSpec((tf, d), lambda i, f: (f, 0), **resident),
                  pl.BlockSpec((1, d), lambda i, f: (0, 0)), pl.BlockSpec((1, d), lambda i, f: (0, 0))],
        out_specs=pl.BlockSpec((tm, d), lambda i, f: (i, 0)),
        out_shape=jax.ShapeDtypeStruct((m, d), f32),
        scratch_shapes=[pltpu.VMEM((tm, d), MXU_DTYPE), pltpu.VMEM((tm, d), f32)],
        compiler_params=_params("parallel", "arbitrary"),
        name="ffn_residual_ln",
    )(x, w1, w3, w2, g.reshape(1, d), b.reshape(1, d))


ROUTE_IDS = N_EXPERTS
ROUTE_GATES = N_EXPERTS + 2


def _router_kernel(x_ref, w_ref, o_ref, *, n_experts):
    logits = jnp.dot(x_ref[...], w_ref[...], preferred_element_type=f32, precision=lax.Precision.HIGHEST)
    lane = lax.broadcasted_iota(i32, logits.shape, 1).astype(f32)
    logits = jnp.where(lane < n_experts, logits, -jnp.inf)
    v1 = jnp.max(logits, axis=1, keepdims=True)
    i1 = jnp.min(jnp.where(logits == v1, lane, float(LANES)), axis=1, keepdims=True)
    rest = jnp.where(lane == i1, -jnp.inf, logits)
    v2 = jnp.max(rest, axis=1, keepdims=True)
    i2 = jnp.min(jnp.where(rest == v2, lane, float(LANES)), axis=1, keepdims=True)
    e2 = jnp.exp(v2 - v1)
    g1 = 1.0 / (1.0 + e2)
    g2 = e2 / (1.0 + e2)
    out = jnp.where(lane == ROUTE_IDS, i1, 0.0) + jnp.where(lane == ROUTE_IDS + 1, i2, 0.0)
    out = out + jnp.where(lane == ROUTE_GATES, g1, 0.0) + jnp.where(lane == ROUTE_GATES + 1, g2, 0.0)
    o_ref[...] = out


def route_top2(x, router_w, tm=512):
    m, d = x.shape
    n_experts = router_w.shape[1]
    w = jnp.zeros((d, LANES), f32).at[:, :n_experts].set(router_w)
    return pl.pallas_call(
        functools.partial(_router_kernel, n_experts=n_experts),
        grid=(m // tm,),
        in_specs=[pl.BlockSpec((tm, d), lambda i: (i, 0)), pl.BlockSpec((d, LANES), lambda i: (0, 0))],
        out_specs=pl.BlockSpec((tm, LANES), lambda i: (i, 0)),
        out_shape=jax.ShapeDtypeStruct((m, LANES), f32),
        compiler_params=_params("parallel"),
        name="route_top2",
    )(x, w)


def _moe_ln_kernel(tok_ref, gs_ref, off_ref, x_ref, w1_ref, w3_ref, w2_ref, g_ref, b_ref, o_ref,
                   xg_ref, xb_ref, y_ref, *, rt):
    c = pl.program_id(0)
    e = pl.program_id(1)
    f = pl.program_id(2)
    last_f = pl.num_programs(2) - 1
    start = off_ref[0, e]
    count = off_ref[0, e + 1] - start
    n_tiles = (count + rt - 1) // rt

    @pl.when((c == 0) & (e == 0) & (f == 0))
    def _():
        xg_ref[...] = jnp.zeros_like(xg_ref)

    @pl.when((e == 0) & (f == 0))
    def _():
        o_ref[...] = jnp.zeros_like(o_ref)

    def row_loop(body):
        def group(j, carry):
            for u in range(ROW_UNROLL):
                body(j * ROW_UNROLL + u)
            return carry

        def single(r, carry):
            body(r)
            return carry
        lax.fori_loop(0, count // ROW_UNROLL, group, 0)
        lax.fori_loop((count // ROW_UNROLL) * ROW_UNROLL, count, single, 0)

    @pl.when(f == 0)
    def _():
        def gather(r):
            t = tok_ref[0, start + r]
            xg_ref[pl.ds(r, 1), :] = x_ref[pl.ds(t, 1), :]
        row_loop(gather)

        def cast(j, carry):
            rows = pl.ds(pl.multiple_of(j * rt, rt), rt)
            xb_ref[rows, :] = xg_ref[rows, :].astype(MXU_DTYPE)
            return carry
        lax.fori_loop(0, n_tiles, cast, 0)

    def tile(j, carry):
        rows = pl.ds(pl.multiple_of(j * rt, rt), rt)
        y = _swiglu_tile(xb_ref[rows, :], w1_ref, w3_ref, w2_ref)

        @pl.when(f == 0)
        def _():
            y_ref[rows, :] = y

        @pl.when(f != 0)
        def _():
            y_ref[rows, :] += y
        return carry
    lax.fori_loop(0, n_tiles, tile, 0)

    @pl.when(f == last_f)
    def _():
        def updated(r):
            t = tok_ref[0, start + r]
            return t, o_ref[pl.ds(t, 1), :] + gs_ref[0, start + r] * y_ref[pl.ds(r, 1), :]

        def group(j, carry):
            rows = [updated(j * ROW_UNROLL + u) for u in range(ROW_UNROLL)]
            for t, row in rows:
                o_ref[pl.ds(t, 1), :] = row
            return carry

        def single(r, carry):
            t, row = updated(r)
            o_ref[pl.ds(t, 1), :] = row
            return carry
        lax.fori_loop(0, count // ROW_UNROLL, group, 0)
        lax.fori_loop((count // ROW_UNROLL) * ROW_UNROLL, count, single, 0)

    @pl.when((e == pl.num_programs(1) - 1) & (f == last_f))
    def _():
        o_ref[...] = _layer_norm_rows(DEEPNORM_ALPHA * x_ref[...] + o_ref[...], g_ref[...], b_ref[...])


def moe_residual_ln(x, routes, w1, w3, w2, g, b, tm, tf, rt=128):
    m, d = x.shape
    n_experts, _, ff = w1.shape
    n_chunks = m // tm
    ids = routes[:, ROUTE_IDS:ROUTE_IDS + TOP_K].astype(i32).reshape(n_chunks, tm * TOP_K)
    gts = routes[:, ROUTE_GATES:ROUTE_GATES + TOP_K].reshape(n_chunks, tm * TOP_K)
    order = jnp.argsort(ids, axis=1, stable=True).astype(i32)
    tok = order // TOP_K
    gs = jnp.take_along_axis(gts, order, axis=1)
    counts = jnp.sum(ids[:, :, None] == jnp.arange(n_experts, dtype=i32)[None, None, :], axis=1, dtype=i32)
    offs = jnp.concatenate([jnp.zeros((n_chunks, 1), i32), jnp.cumsum(counts, axis=1, dtype=i32)], axis=1)
    smem = lambda width: pl.BlockSpec((None, 1, width), lambda c, e, f: (c, 0, 0), memory_space=pltpu.SMEM)
    return pl.pallas_call(
        functools.partial(_moe_ln_kernel, rt=rt),
        grid=(n_chunks, n_experts, ff // tf),
        in_specs=[smem(tm * TOP_K), smem(tm * TOP_K), smem(n_experts + 1),
                  pl.BlockSpec((tm, d), lambda c, e, f: (c, 0), pipeline_mode=pl.Buffered(1)),
                  pl.BlockSpec((None, d, tf), lambda c, e, f: (e, 0, f)),
                  pl.BlockSpec((None, d, tf), lambda c, e, f: (e, 0, f)),
                  pl.BlockSpec((None, tf, d), lambda c, e, f: (e, f, 0)),
                  pl.BlockSpec((1, d), lambda c, e, f: (0, 0)), pl.BlockSpec((1, d), lambda c, e, f: (0, 0))],
        out_specs=pl.BlockSpec((tm, d), lambda c, e, f: (c, 0), pipeline_mode=pl.Buffered(1)),
        out_shape=jax.ShapeDtypeStruct((m, d), f32),
        scratch_shapes=[pltpu.VMEM((tm, d), f32), pltpu.VMEM((tm, d), MXU_DTYPE), pltpu.VMEM((tm, d), f32)],
        compiler_params=_params("arbitrary", "arbitrary", "arbitrary"),
        name="moe_residual_ln",
    )(tok[:, None, :], gs[:, None, :], offs[:, None, :], x, w1, w3, w2, g.reshape(1, d), b.reshape(1, d))


def _flash_reset(m_ref, acc_ref):
    m_ref[...] = jnp.full(m_ref.shape, NEG, f32)
    acc_ref[...] = jnp.zeros(acc_ref.shape, f32)


def _flash_rows(q, kt, bias, m_ref, alpha_ref, p_ref, rows):
    s = _mm(q, kt)
    if bias is not None:
        s = s + bias
    m_prev = m_ref[rows, :]
    m_next = jnp.maximum(m_prev, jnp.max(s, axis=1, keepdims=True))
    alpha_ref[rows, :] = jnp.exp2(m_prev - m_next)
    m_ref[rows, :] = m_next
    p_ref[rows, :s.shape[1]] = jnp.exp2(s - _lane_tile(m_next, s.shape[1] // LANES)).astype(p_ref.dtype)


def _flash_accumulate(v_aug, alpha_ref, p_ref, acc_ref):
    alpha = _lane_tile(alpha_ref[...], acc_ref.shape[-1] // LANES)
    acc_ref[...] = alpha * acc_ref[...] + _mm(p_ref[:, :v_aug.shape[0]], v_aug)


def _skewed_chunks(n, score, fold, carry):
    def pair(j, carry):
        c = 2 * j
        fold(c - 1, 1)
        carry = score(c, 0, carry)
        fold(c, 0)
        return score(c + 1, 1, carry)

    def single(c, carry):
        fold(c - 1, 1)
        return score(c, 0, carry)

    carry = lax.fori_loop(0, n // 2, pair, carry)
    return lax.fori_loop(2 * (n // 2), n, single, carry)


def _fold_last(n, fold):
    for buf in range(2):
        pl.when((n - 1) % 2 == buf)(functools.partial(fold, n - 1, buf))


def _lane_tile(x, reps):
    return x if reps == 1 else jnp.concatenate([x] * reps, axis=1)


def _row_slices(n_rows):
    return [slice(r * Q_BLOCK, (r + 1) * Q_BLOCK) for r in range(n_rows // Q_BLOCK)]


def _dsa_kernel(qi_ref, wi_ref, kit_ref, qa_ref, kat_ref, va_ref, o_ref,
                skey_ref, wrep_ref, m_ref, alpha_ref, acc_ref, p_ref, qh_ref, *, topk, kc):
    qb = Q_BLOCK
    i = pl.program_id(1)
    q0 = i * qb
    n_chunks = (q0 + qb + kc - 1) // kc
    t_row = q0 + lax.broadcasted_iota(i32, (qb, kc), 0)
    col = lax.broadcasted_iota(i32, (qb, kc), 1)
    wi = wi_ref[...]
    qi = qi_ref[...]
    qis = [qi[:, h * IDX_DIM:(h + 1) * IDX_DIM] for h in range(IDX_HEADS)]
    qa = qa_ref[...]
    for h in range(A_HEADS):
        qh_ref[h * qb:(h + 1) * qb, :] = qa[:, h * HEAD_DIM:(h + 1) * HEAD_DIM]
    for h in range(IDX_HEADS):
        wrep_ref[h] = jnp.broadcast_to(wi[:, h:h + 1], (qb, kc))

    def score_body(c, carry):
        off = pl.multiple_of(c * kc, kc)
        kt = kit_ref[:, pl.ds(off, kc)]
        s = jnp.zeros((qb, kc), f32)
        for h in range(IDX_HEADS):
            s = s + wrep_ref[h] * jnp.maximum(_mm(qis[h], kt), 0.0)
        s = jnp.where(col + off <= t_row, s, NEG)
        bits = pltpu.bitcast(s, i32)
        skey_ref[:, pl.ds(off, kc)] = jnp.where(bits < 0, bits ^ 0x7FFFFFFF, bits)
        return carry

    n_strips = (q0 + qb + COUNT_STRIP - 1) // COUNT_STRIP
    lax.fori_loop(0, n_strips * (COUNT_STRIP // kc), score_body, 0)

    def count_ge(cand):
        def body(c, acc):
            off = pl.multiple_of(c * COUNT_STRIP, COUNT_STRIP)
            for j in range(COUNT_STRIP // LANES):
                acc = acc + jnp.where(skey_ref[:, pl.ds(off + j * LANES, LANES)] >= cand, 1.0, 0.0)
            return acc
        acc = lax.fori_loop(0, n_strips, body, jnp.zeros((qb, LANES), f32))
        return jnp.sum(acc, axis=1, keepdims=True)

    thr = jnp.where(count_ge(jnp.zeros((qb, 1), i32)) >= topk, 0, INT_MIN).astype(i32)

    def bisect(b, thr):
        cand = thr + jnp.left_shift(jnp.int32(1), 30 - b)
        return jnp.where(count_ge(cand) >= topk, cand, thr)

    thr = lax.fori_loop(0, 31, bisect, thr)
    quota = topk - count_ge(thr + 1)

    rank_w = min(kc, RANK_PIECE)
    before = (lax.broadcasted_iota(i32, (rank_w, rank_w), 0) < lax.broadcasted_iota(i32, (rank_w, rank_w), 1))
    before = jnp.where(before, 1.0, 0.0).astype(MXU_DTYPE)
    _flash_reset(m_ref, acc_ref)
    alpha_ref[...] = jnp.zeros(alpha_ref.shape, f32)
    p_ref[...] = jnp.zeros(p_ref.shape, p_ref.dtype)
    heads = _row_slices(A_HEADS * qb)

    def fold(c, buf):
        off = pl.multiple_of(jnp.maximum(c, 0) * kc, kc)
        _flash_accumulate(va_ref[pl.ds(off, kc), :], alpha_ref.at[buf], p_ref.at[buf], acc_ref)

    def score(c, buf, ties_seen):
        off = pl.multiple_of(c * kc, kc)
        key = skey_ref[:, pl.ds(off, kc)]
        eq = key == thr
        eqf = jnp.where(eq, 1.0, 0.0)
        ranks = []
        for j in range(kc // rank_w):
            piece = eqf[:, j * rank_w:(j + 1) * rank_w]
            ranks.append(ties_seen + _mm(piece.astype(MXU_DTYPE), before))
            ties_seen = ties_seen + jnp.sum(piece, axis=1, keepdims=True)
        rank = jnp.concatenate(ranks, axis=1)
        sel = ((key > thr) | (eq & (rank < quota))) & (col + off <= t_row)
        bias = jnp.where(sel, 0.0, NEG)
        kt = kat_ref[:, pl.ds(off, kc)]
        for rows in heads:
            _flash_rows(qh_ref[rows, :], kt, bias, m_ref, alpha_ref.at[buf], p_ref.at[buf], rows)
        return ties_seen

    _skewed_chunks(n_chunks, score, fold, jnp.zeros((qb, 1), f32))
    _fold_last(n_chunks, fold)
    acc = acc_ref[...]
    o = acc[:, :HEAD_DIM] / acc[:, HEAD_DIM:]
    for h in range(A_HEADS):
        o_ref[:, h * HEAD_DIM:(h + 1) * HEAD_DIM] = o[h * qb:(h + 1) * qb, :]


QPK_QA, QPK_QIDX, QPK_QB = 0, 2, 3
KPK_KA, KPK_KIDX, KPK_KSLC, KPK_KWIN = 0, 1, 2, 4
VPK_VA, VPK_VSLC, VPK_VWIN = 0, 1, 3
OUT_NSA = 2


def dsa_attention(wi, qpk, kpk, vpk, topk, kc=512):
    bsz, seq, _ = qpk.shape
    nb = seq // Q_BLOCK
    rows = A_HEADS * Q_BLOCK
    width = A_HEADS * HEAD_DIM
    kern = functools.partial(_dsa_kernel, topk=topk, kc=kc)
    return pl.pallas_call(
        kern,
        grid=(bsz, nb),
        in_specs=[pl.BlockSpec((None, Q_BLOCK, IDX_HEADS * IDX_DIM), lambda b, i: (b, i, QPK_QIDX)),
                  pl.BlockSpec((None, Q_BLOCK, IDX_HEADS), lambda b, i: (b, i, 0)),
                  pl.BlockSpec((None, IDX_DIM, seq), lambda b, i: (b, KPK_KIDX, 0)),
                  pl.BlockSpec((None, Q_BLOCK, width), lambda b, i: (b, i, QPK_QA)),
                  pl.BlockSpec((None, HEAD_DIM, seq), lambda b, i: (b, KPK_KA, 0)),
                  pl.BlockSpec((None, seq, 2 * HEAD_DIM), lambda b, i: (b, 0, VPK_VA))],
        out_specs=pl.BlockSpec((None, Q_BLOCK, width), lambda b, i: (b, i, 0)),
        out_shape=jax.ShapeDtypeStruct((bsz, seq, 2 * width), f32),
        scratch_shapes=[pltpu.VMEM((Q_BLOCK, seq), i32), pltpu.VMEM((IDX_HEADS, Q_BLOCK, kc), f32),
                        pltpu.VMEM((rows, LANES), f32), pltpu.VMEM((2, rows, LANES), f32),
                        pltpu.VMEM((rows, 2 * HEAD_DIM), f32), pltpu.VMEM((2, rows, kc), MXU_DTYPE),
                        pltpu.VMEM((rows, HEAD_DIM), MXU_DTYPE)],
        compiler_params=_params("parallel", "arbitrary"),
        name="dsa_attention",
    )(qpk, wi, kpk, qpk, kpk, vpk)


def _compress_kernel(x_ref, pe_ref, w1_ref, w2_ref, o_ref):
    n_rows = o_ref.shape[1]
    parts = CMP_LEN // CMP_STRIDE
    acc = [[jnp.zeros((n_rows, CMP_HIDDEN), f32) for _ in range(parts)] for _ in range(B_KV_GROUPS)]
    for t in range(CMP_STRIDE):
        tok = x_ref[pl.ds(t, n_rows, stride=CMP_STRIDE), :]
        for g in range(B_KV_GROUPS):
            tok_g = tok[:, g * HEAD_DIM:(g + 1) * HEAD_DIM]
            for part in range(parts):
                l = part * CMP_STRIDE + t
                lhs = (tok_g + pe_ref[l:l + 1, :]).astype(MXU_DTYPE)
                acc[g][part] = acc[g][part] + _mm(lhs, w1_ref[l * HEAD_DIM:(l + 1) * HEAD_DIM, :])
    for g in range(B_KV_GROUPS):
        pre = acc[g][0] + pltpu.roll(acc[g][1], shift=n_rows - 1, axis=0)
        hid = pre * jax.nn.sigmoid(pre)
        o_ref[g] = _mm(hid.astype(MXU_DTYPE), w2_ref[...])


def nsa_compress(cmp, pe, w1, w2):
    bsz, seq, _ = cmp.shape
    n_rows = seq // CMP_STRIDE
    width = B_KV_GROUPS * HEAD_DIM
    return pl.pallas_call(
        _compress_kernel,
        grid=(bsz, 2),
        in_specs=[pl.BlockSpec((None, seq, width), lambda b, s: (b, 0, s)),
                  pl.BlockSpec((None, CMP_LEN, HEAD_DIM), lambda b, s: (s, 0, 0)),
                  pl.BlockSpec((None, CMP_LEN * HEAD_DIM, CMP_HIDDEN), lambda b, s: (s, 0, 0)),
                  pl.BlockSpec((None, CMP_HIDDEN, HEAD_DIM), lambda b, s: (s, 0, 0))],
        out_specs=pl.BlockSpec((None, None, B_KV_GROUPS, n_rows, HEAD_DIM), lambda b, s: (b, s, 0, 0, 0)),
        out_shape=jax.ShapeDtypeStruct((bsz, 2, B_KV_GROUPS, n_rows, HEAD_DIM), f32),
        compiler_params=_params("parallel", "parallel"),
        name="nsa_compress",
    )(cmp, pe, w1, w2)


def _nsa_kernel(qin_ref, graw_ref, kct_ref, vc_ref, kst_ref, vs_ref, kwt_ref, vw_ref, ovl_ref, _, o_ref,
                m_ref, alpha_ref, acc_ref, p_ref, pw_ref, q_ref, imp_ref, *, n_slc, n_sel, kc):
    qb = Q_BLOCK
    i = pl.program_id(2)
    q0 = i * qb
    n_cmp = kct_ref.shape[1]
    heads = _row_slices(B_REP * qb)
    qin = qin_ref[...]
    for r in range(B_REP):
        q_ref[r * qb:(r + 1) * qb, :] = qin[:, r * HEAD_DIM:(r + 1) * HEAD_DIM]

    def compressed(width):
        t_c = q0 + lax.broadcasted_iota(i32, (qb, width), 0)
        cmp_end = lax.broadcasted_iota(i32, (qb, width), 1) * CMP_STRIDE + (CMP_LEN - 1)
        vis = cmp_end <= t_c
        kct = kct_ref[:, :width]
        p_sum = jnp.zeros((qb, width), f32)
        for rows in heads:
            lc = jnp.where(vis, _mm(q_ref[rows, :], kct), NEG)
            ec = jnp.where(vis, jnp.exp2(lc - jnp.max(lc, axis=1, keepdims=True)), 0.0)
            den = jnp.sum(ec, axis=1, keepdims=True)
            p_c = ec / jnp.where(den > 0.0, den, 1.0)
            p_sum = p_sum + p_c
            acc_ref[rows, :HEAD_DIM] = _mm(p_c.astype(MXU_DTYPE), vc_ref[:width, :])
        imp_ref[...] = jnp.dot(p_sum, ovl_ref[:width, :], preferred_element_type=f32,
                               precision=lax.Precision.HIGHEST)

    if n_cmp % (2 * LANES) == 0:
        first_half_only = (q0 + qb) * 2 <= n_cmp * CMP_STRIDE
        pl.when(first_half_only)(functools.partial(compressed, n_cmp // 2))
        pl.when(jnp.logical_not(first_half_only))(functools.partial(compressed, n_cmp))
    else:
        compressed(n_cmp)
    o_c = acc_ref[:, :HEAD_DIM]

    imp = imp_ref[...].T
    t_q = q0 + lax.broadcasted_iota(i32, (LANES, qb), 1)
    blk = lax.broadcasted_iota(i32, (LANES, qb), 0)
    blk_t = t_q // SLC_LEN
    forced = (blk == 0) | (blk == blk_t) | (blk == blk_t - 1)
    imp = jnp.where(forced, FORCED_BOOST, imp)
    imp = jnp.where(blk * SLC_LEN <= t_q, imp, NEG)
    imp = jnp.where(blk < n_slc, imp, -jnp.inf)
    blk_f = blk.astype(f32)

    def pick(_, carry):
        imp, selm = carry
        best = jnp.max(imp, axis=0, keepdims=True)
        first = jnp.min(jnp.where(imp == best, blk_f, float(LANES)), axis=0, keepdims=True)
        hit = blk_f == first
        return jnp.where(hit, -jnp.inf, imp), jnp.where(hit, 1.0, selm)

    _, selm = lax.fori_loop(0, n_sel, pick, (imp, jnp.zeros((LANES, qb), f32)))
    selm = selm.T.astype(MXU_DTYPE)

    t_k = q0 + lax.broadcasted_iota(i32, (qb, kc), 0)
    col = lax.broadcasted_iota(i32, (qb, kc), 1)
    exp_row = lax.broadcasted_iota(i32, (LANES, kc), 0)
    exp_col = lax.broadcasted_iota(i32, (LANES, kc), 1)
    _flash_reset(m_ref, acc_ref)
    alpha_ref[...] = jnp.zeros(alpha_ref.shape, f32)
    p_ref[...] = jnp.zeros(p_ref.shape, p_ref.dtype)

    def fold(c, buf):
        off = pl.multiple_of(jnp.maximum(c, 0) * kc, kc)
        _flash_accumulate(vs_ref[pl.ds(off, kc), :], alpha_ref.at[buf], p_ref.at[buf], acc_ref)

    def score(c, buf, carry):
        off = pl.multiple_of(c * kc, kc)
        expand = jnp.where(exp_row == (exp_col + off) // SLC_LEN, 1.0, 0.0).astype(MXU_DTYPE)
        sel = (_mm(selm, expand) > 0.5) & (col + off <= t_k)
        bias = jnp.where(sel, 0.0, NEG)
        kt = kst_ref[:, pl.ds(off, kc)]
        for rows in heads:
            _flash_rows(q_ref[rows, :], kt, bias, m_ref, alpha_ref.at[buf], p_ref.at[buf], rows)
        return carry

    n_slc_chunks = (q0 + qb + kc - 1) // kc
    _skewed_chunks(n_slc_chunks, score, fold, 0)
    _fold_last(n_slc_chunks, fold)
    acc = acc_ref[...]
    o_s = acc[:, :HEAD_DIM] / acc[:, HEAD_DIM:]

    slab = WINDOW + qb
    w0 = pl.multiple_of(jnp.maximum(q0 - WINDOW, 0), qb)
    dist = (q0 + lax.broadcasted_iota(i32, (qb, slab), 0)) - (w0 + lax.broadcasted_iota(i32, (qb, slab), 1))
    bias = jnp.where((dist >= 0) & (dist < WINDOW), 0.0, NEG)
    kt = kwt_ref[:, pl.ds(w0, slab)]
    for rows in heads:
        s = _mm(q_ref[rows, :], kt) + bias
        pw_ref[rows, :] = jnp.exp2(s - jnp.max(s, axis=1, keepdims=True)).astype(pw_ref.dtype)
    acc = _mm(pw_ref[...], vw_ref[pl.ds(w0, slab), :])
    o_w = acc[:, :HEAD_DIM] / acc[:, HEAD_DIM:]

    gates = jax.nn.sigmoid(graw_ref[...])
    for r, rows in enumerate(heads):
        g_c, g_s, g_w = (gates[:, 3 * r + n:3 * r + n + 1] for n in range(3))
        o_ref[:, r * HEAD_DIM:(r + 1) * HEAD_DIM] = g_c * o_c[rows] + g_s * o_s[rows] + g_w * o_w[rows]


def nsa_attention(qpk, graw, kct, vc, kpk, vpk, ovl, out_ab, n_slc, n_sel, kc=512):
    bsz, seq, _ = qpk.shape
    groups = B_KV_GROUPS
    nb = seq // Q_BLOCK
    rows = B_REP * Q_BLOCK
    width = B_REP * HEAD_DIM
    n_cmp = kct.shape[-1]
    kern = functools.partial(_nsa_kernel, n_slc=n_slc, n_sel=n_sel, kc=kc)
    per_bg = lambda b, g, i: (b, g, 0, 0)
    return pl.pallas_call(
        kern,
        grid=(bsz, groups, nb),
        in_specs=[pl.BlockSpec((None, Q_BLOCK, width), lambda b, g, i: (b, i, QPK_QB + g)),
                  pl.BlockSpec((None, None, Q_BLOCK, B_REP * 3), lambda b, g, i: (b, g, i, 0)),
                  pl.BlockSpec((None, None, HEAD_DIM, n_cmp), per_bg),
                  pl.BlockSpec((None, None, n_cmp, HEAD_DIM), per_bg),
                  pl.BlockSpec((None, HEAD_DIM, seq), lambda b, g, i: (b, KPK_KSLC + g, 0)),
                  pl.BlockSpec((None, seq, 2 * HEAD_DIM), lambda b, g, i: (b, 0, VPK_VSLC + g)),
                  pl.BlockSpec((None, HEAD_DIM, seq), lambda b, g, i: (b, KPK_KWIN + g, 0)),
                  pl.BlockSpec((None, seq, 2 * HEAD_DIM), lambda b, g, i: (b, 0, VPK_VWIN + g)),
                  pl.BlockSpec((n_cmp, LANES), lambda b, g, i: (0, 0)),
                  pl.BlockSpec(memory_space=pl.ANY)],
        out_specs=pl.BlockSpec((None, Q_BLOCK, width), lambda b, g, i: (b, i, OUT_NSA + g)),
        out_shape=jax.ShapeDtypeStruct(out_ab.shape, f32),
        input_output_aliases={9: 0},
        scratch_shapes=[pltpu.VMEM((rows, LANES), f32), pltpu.VMEM((2, rows, LANES), f32),
                        pltpu.VMEM((rows, 2 * HEAD_DIM), f32), pltpu.VMEM((2, rows, kc), MXU_DTYPE),
                        pltpu.VMEM((rows, WINDOW + Q_BLOCK), MXU_DTYPE), pltpu.VMEM((rows, HEAD_DIM), MXU_DTYPE),
                        pltpu.VMEM((Q_BLOCK, LANES), f32)],
        compiler_params=_params("parallel", "parallel", "arbitrary"),
        name="nsa_attention",
    )(qpk, graw, kct, vc, kpk, vpk, kpk, vpk, ovl, out_ab)


def _diff_kernel(lam_ref, qin_ref, kt_ref, v_ref, g_ref, o_ref, m_ref, alpha_ref, acc_ref, p_ref, q_ref,
                 *, tq, kc, out_scale):
    i = pl.program_id(2)
    q0 = i * tq
    dv = v_ref.shape[-1] // 2
    groups = _row_slices(tq)
    qin = qin_ref[...]
    for half in range(2):
        q_ref[half] = qin[:, half * HEAD_DIM:(half + 1) * HEAD_DIM]
    for half in range(2):
        _flash_reset(m_ref.at[half], acc_ref.at[half])
    alpha_ref[...] = jnp.zeros(alpha_ref.shape, f32)
    p_ref[...] = jnp.zeros(p_ref.shape, p_ref.dtype)

    def fold(c, buf, sub=(0, kc // tq)):
        off = pl.multiple_of(jnp.maximum(c, 0) * kc + sub[0] * tq, tq)
        v = v_ref[pl.ds(off, sub[1] * tq), :]
        for half in range(2):
            _flash_accumulate(v, alpha_ref.at[buf, half], p_ref.at[buf, half], acc_ref.at[half])

    def score(c, buf, masked, sub=(0, kc // tq)):
        off = pl.multiple_of(c * kc + sub[0] * tq, tq)
        width = sub[1] * tq
        for half in range(2):
            kt = kt_ref[half * HEAD_DIM:(half + 1) * HEAD_DIM, pl.ds(off, width)]
            for r, rows in enumerate(groups):
                bias = None
                if masked:
                    key = off + lax.broadcasted_iota(i32, (Q_BLOCK, width), 1)
                    t = q0 + r * Q_BLOCK + lax.broadcasted_iota(i32, (Q_BLOCK, width), 0)
                    bias = jnp.where(key <= t, 0.0, NEG)
                _flash_rows(q_ref[half, rows, :], kt, bias, m_ref.at[half], alpha_ref.at[buf, half],
                            p_ref.at[buf, half], rows)

    def score_full(c, buf, carry):
        score(c, buf, False)
        return carry

    n_full = q0 // kc
    _skewed_chunks(n_full, score_full, fold, 0)

    own = (q0 - n_full * kc) // tq

    def finish(buf, own_piece):
        fold(n_full - 1, 1 - buf)
        if own_piece:
            score(n_full, buf, False, (0, own_piece))
            fold(n_full, buf, (0, own_piece))
            buf = 1 - buf
        score(n_full, buf, True, (own_piece, 1))
        fold(n_full, buf, (own_piece, 1))

    for buf in range(2):
        for own_piece in range(kc // tq):
            pl.when((n_full % 2 == buf) & (own == own_piece))(functools.partial(finish, buf, own_piece))
    a1 = acc_ref[0]
    a2 = acc_ref[1]
    o = a1[:, :dv] / a1[:, dv:] - lam_ref[0] * (a2[:, :dv] / a2[:, dv:])
    o = o * lax.rsqrt(jnp.mean(o * o, axis=-1, keepdims=True) + LN_EPS) * g_ref[...]
    o_ref[...] = o * out_scale


def diff_attention(lam, q, kt, v, subln_g, out_scale, tq=512, kc=1024):
    bsz, seq, width = q.shape
    dv = 2 * HEAD_DIM
    heads = width // dv
    tq = min(tq, seq)
    kc = min(kc, seq)
    assert kc % tq == 0 and seq % kc == 0
    kern = functools.partial(_diff_kernel, tq=tq, kc=kc, out_scale=out_scale)
    return pl.pallas_call(
        kern,
        grid=(bsz, heads, seq // tq),
        in_specs=[pl.BlockSpec(memory_space=pltpu.SMEM),
                  pl.BlockSpec((None, tq, dv), lambda b, h, i: (b, i, h)),
                  pl.BlockSpec((None, dv, seq), lambda b, h, i: (b, h, 0)),
                  pl.BlockSpec((None, seq, 2 * dv), lambda b, h, i: (b, 0, h)),
                  pl.BlockSpec((1, dv), lambda b, h, i: (0, 0))],
        out_specs=pl.BlockSpec((None, tq, dv), lambda b, h, i: (b, i, h)),
        out_shape=jax.ShapeDtypeStruct((bsz, seq, heads * dv), f32),
        scratch_shapes=[pltpu.VMEM((2, tq, LANES), f32), pltpu.VMEM((2, 2, tq, LANES), f32),
                        pltpu.VMEM((2, tq, 2 * dv), f32), pltpu.VMEM((2, 2, tq, kc), MXU_DTYPE),
                        pltpu.VMEM((2, tq, HEAD_DIM), MXU_DTYPE)],
        compiler_params=_params("parallel", "parallel", "arbitrary"),
        name="diff_attention",
    )(lam, q, kt, v, subln_g.reshape(1, dv))


def _rope_tables(positions):
    inv_freq = ROPE_THETA ** (-jnp.arange(0, ROT_DIM, 2, dtype=f32) / ROT_DIM)
    ang = positions.astype(f32)[..., None] * inv_freq
    return jnp.cos(ang), jnp.sin(ang)


def _apply_rope(x, cos, sin):
    shape = cos.shape[:2] + (1,) * (x.ndim - 3) + cos.shape[-1:]
    c = cos.reshape(shape)
    s = sin.reshape(shape)
    half = ROT_DIM // 2
    x1, x2 = x[..., :half], x[..., half:ROT_DIM]
    return jnp.concatenate([x1 * c - x2 * s, x2 * c + x1 * s, x[..., ROT_DIM:]], axis=-1)


def _rope_coefficients(cos, sin):
    half = ROT_DIM // 2
    spread = np.zeros((ROT_DIM, 3 * LANES), np.float32)
    offset = np.zeros((3 * LANES,), np.float32)
    for lane in range(LANES):
        d = lane % HEAD_DIM
        if d >= ROT_DIM:
            offset[lane] = 1.0
        elif d < half:
            spread[d, lane], spread[half + d, LANES + lane] = 1.0, -1.0
        else:
            spread[d - half, lane], spread[d, 2 * LANES + lane] = 1.0, 1.0
    table = jnp.concatenate([cos, sin], axis=-1)
    return jnp.dot(table, spread, precision=lax.Precision.HIGHEST) + offset


def _ab_weight_columns(w_in):
    widths = dict(AB_LAYOUT)
    starts = dict(zip(widths, np.cumsum([0] + [w for _, w in AB_LAYOUT[:-1]]).tolist()))
    cols = lambda names: np.concatenate([np.arange(starts[n], starts[n] + widths[n]) for n in names])
    first = ("q_a", "q_idx", "q_b", "k_a", "k_idx", "k_slc", "k_win", "v_a", "v_slc", "v_win", "gate_b", "w_idx")
    n_first = sum(widths[n] for n in first)
    pad = AB_CMP_TILE * LANES - n_first
    assert 0 <= pad < HEAD_DIM
    w = jnp.concatenate([w_in[:, cols(first)], jnp.zeros((w_in.shape[0], pad), w_in.dtype),
                         w_in[:, cols(("k_cmp", "v_cmp"))]], axis=1)
    assert w.shape[1] == AB_TILES * LANES
    return w.astype(MXU_DTYPE)


def _overlap_matrix(n_cmp_rows, n_slc):
    c_start = np.arange(n_cmp_rows) * CMP_STRIDE
    s_start = np.arange(LANES) * SLC_LEN
    ovl = (c_start[:, None] < s_start[None, :] + SLC_LEN) & (c_start[:, None] + CMP_LEN > s_start[None, :])
    ovl = ovl & (np.arange(LANES)[None, :] < n_slc)
    return jnp.asarray(ovl.astype(np.float32))


def _ab_mixer(x2, bsz, seq, positions, coef, w_in, pe_k, pe_v, ck1, ck2, cv1, cv2):
    qpk, kpk, vpk, misc, cmp = project_pack_ab(x2.reshape(bsz, seq, -1), _ab_weight_columns(w_in), coef)
    gate_w = B_HEADS * 3
    g_b = misc[..., HEAD_DIM:HEAD_DIM + gate_w]
    w_idx = misc[..., HEAD_DIM + gate_w:HEAD_DIM + gate_w + IDX_HEADS] * (IDX_HEADS * IDX_DIM) ** -0.5

    out_ab = dsa_attention(w_idx, qpk, kpk, vpk, topk=min(DSA_TOPK, seq // 4))

    groups = B_KV_GROUPS
    n_rows = seq // CMP_STRIDE
    n_cmp = (seq - CMP_LEN) // CMP_STRIDE + 1
    assert n_cmp == n_rows - 1
    n_slc = seq // SLC_LEN
    assert n_slc <= LANES

    kv_c = nsa_compress(cmp, jnp.stack([pe_k, pe_v]), jnp.stack([ck1, cv1]).astype(MXU_DTYPE),
                        jnp.stack([ck2, cv2]).astype(MXU_DTYPE))
    k_c, v_c = kv_c[:, 0], kv_c[:, 1]
    cmp_end = jnp.minimum(jnp.arange(n_rows) * CMP_STRIDE + CMP_LEN - 1, seq - 1)
    cos_c, sin_c = _rope_tables(positions[:, cmp_end])
    k_c = _apply_rope(k_c.transpose(0, 2, 1, 3), cos_c, sin_c)
    k_c = k_c.transpose(0, 2, 3, 1).astype(MXU_DTYPE)
    v_c = v_c.astype(MXU_DTYPE)

    graw = g_b.reshape(bsz, seq, groups, B_REP * 3).transpose(0, 2, 1, 3)
    out_ab = nsa_attention(qpk, graw, k_c, v_c, kpk, vpk, _overlap_matrix(n_rows, n_slc), out_ab,
                           n_slc=n_slc, n_sel=min(SLC_TOPN, n_slc))
    return out_ab.reshape(bsz * seq, (A_HEADS + B_HEADS) * HEAD_DIM)


def _diff_mixer(x2, bsz, seq, coef, w_in, lq1, lk1, lq2, lk2, subln_g, lam_init):
    q, kt, v = project_pack_c(x2.reshape(bsz, seq, -1), w_in.astype(MXU_DTYPE), coef)
    lam =(jnp.exp(jnp.sum(lq1 * lk1)) - jnp.exp(jnp.sum(lq2 * lk2)) + lam_init).reshape(1).astype(f32)
    o = diff_attention(lam, q, kt, v, subln_g, 1.0 - lam_init)
    return o.reshape(bsz * seq, C_HEADS * 2 * HEAD_DIM)


def kernel(x, positions, ab_w_in, cmp_pe_k, cmp_pe_v, cmp_k_w1, cmp_k_w2, cmp_v_w1, cmp_v_w2, ab_w_out, ln_ab_g, ln_ab_b, ffn_w1, ffn_w3, ffn_w2, ln_ffn_g, ln_ffn_b, c_w_in, lambda_q1, lambda_k1, lambda_q2, lambda_k2, c_subln_g, c_w_out, ln_c_g, ln_c_b, router_w, moe_w1, moe_w3, moe_w2, ln_moe_g, ln_moe_b):
    bsz, seq, d = x.shape
    assert seq % COUNT_STRIP == 0 and seq >= WINDOW + Q_BLOCK and d == D_MODEL
    coef = _rope_coefficients(*_rope_tables(positions))
    x2 = x.reshape(bsz * seq, d)
    for layer in range(DEPTH):
        i = layer // 2
        if layer % 2 == 0:
            o = _ab_mixer(x2, bsz, seq, positions, coef, ab_w_in[i], cmp_pe_k[i], cmp_pe_v[i],
                          cmp_k_w1[i], cmp_k_w2[i], cmp_v_w1[i], cmp_v_w2[i])
            x2 = project_residual_ln(o, ab_w_out[i].astype(MXU_DTYPE), x2, ln_ab_g[i], ln_ab_b[i])
            x2 = ffn_residual_ln(x2, ffn_w1[i].astype(MXU_DTYPE), ffn_w3[i].astype(MXU_DTYPE),
                                 ffn_w2[i].astype(MXU_DTYPE), ln_ffn_g[i], ln_ffn_b[i], tm=512, tf=ffn_w1.shape[-1])
        else:
            lam_init = 0.8 - 0.6 * math.exp(-0.3 * layer)
            o = _diff_mixer(x2, bsz, seq, coef, c_w_in[i], lambda_q1[i], lambda_k1[i], lambda_q2[i],
                            lambda_k2[i], c_subln_g[i], lam_init)
            x2 = project_residual_ln(o, c_w_out[i].astype(MXU_DTYPE), x2, ln_c_g[i], ln_c_b[i])
            routes = route_top2(x2, router_w[i])
            x2 = moe_residual_ln(x2, routes, moe_w1[i].astype(MXU_DTYPE), moe_w3[i].astype(MXU_DTYPE),
                                 moe_w2[i].astype(MXU_DTYPE), ln_moe_g[i], ln_moe_b[i], tm=min(2048, bsz * seq // 2), tf=896, rt=128)
    return x2.reshape(bsz, seq, d)
```

```python
import functools
import math

import numpy as np
import jax
import jax.numpy as jnp
from jax import lax
from jax.experimental import pallas as pl
from jax.experimental.pallas import tpu as pltpu

f32 = jnp.float32
i32 = jnp.int32
MXU_DTYPE = jnp.bfloat16
VMEM_LIMIT_BYTES = 56 * 1024 * 1024
LANES = 128

D_MODEL = 1024
DEPTH = 2
HEAD_DIM = 64
ROT_DIM = HEAD_DIM // 4
ROPE_THETA = 500000.0
Q_BLOCK = 128
NEG = -1e30
LN_EPS = 1e-5
A_HEADS = 8
IDX_HEADS = 4
IDX_DIM = 64
DSA_TOPK = 256
B_HEADS = 8
B_KV_GROUPS = 2
B_REP = B_HEADS // B_KV_GROUPS
CMP_LEN = 32
CMP_STRIDE = 16
CMP_HIDDEN = 128
SLC_LEN = 64
SLC_TOPN = 16
WINDOW = 512
FORCED_BOOST = 1e6
C_HEADS = 8
N_EXPERTS = 8
TOP_K = 2
DEEPNORM_ALPHA = (2 * DEPTH) ** 0.25
QK_SCALE = HEAD_DIM ** -0.5 * math.log2(math.e)
INT_MIN = -(2 ** 31)
COUNT_STRIP = 512
RANK_PIECE = 256
ROW_UNROLL = 4

AB_LAYOUT = (
    ("q_a", A_HEADS * HEAD_DIM), ("k_a", HEAD_DIM), ("v_a", HEAD_DIM),
    ("q_idx", IDX_HEADS * IDX_DIM), ("k_idx", IDX_DIM), ("w_idx", IDX_HEADS),
    ("q_b", B_HEADS * HEAD_DIM),
    ("k_cmp", B_KV_GROUPS * HEAD_DIM), ("v_cmp", B_KV_GROUPS * HEAD_DIM),
    ("k_slc", B_KV_GROUPS * HEAD_DIM), ("v_slc", B_KV_GROUPS * HEAD_DIM),
    ("k_win", B_KV_GROUPS * HEAD_DIM), ("v_win", B_KV_GROUPS * HEAD_DIM),
    ("gate_b", 3 * B_HEADS),
)


def _params(*sem):
    return pltpu.CompilerParams(dimension_semantics=sem, vmem_limit_bytes=VMEM_LIMIT_BYTES)


def _mm(a, b):
    return jnp.dot(a, b, preferred_element_type=f32)


def _layer_norm_rows(y, g, b):
    mu = jnp.mean(y, axis=-1, keepdims=True)
    yc = y - mu
    var = jnp.mean(yc * yc, axis=-1, keepdims=True)
    return yc * lax.rsqrt(var + LN_EPS) * g + b


def _rope_tile(x, keep, hi, lo):
    return x * keep + pltpu.roll(x, LANES - ROT_DIM // 2, 1) * hi + pltpu.roll(x, ROT_DIM // 2, 1) * lo


def _tile(h, j):
    return h[:, j * LANES:(j + 1) * LANES]


AB_Q_TILES = 10
AB_QIDX_TILES = (4, 5)
AB_K_TILES = 3
AB_V_HEADS = 5
AB_MISC_TILE = 15
AB_CMP_TILE = 16
AB_TILES = 18


def _proj_pack_ab_kernel(x_ref, w_ref, coef_ref, q_ref, kt_ref, v_ref, misc_ref, cmp_ref):
    h = _mm(x_ref[...].astype(MXU_DTYPE), w_ref[...])
    keep, hi, lo = (coef_ref[:, j * LANES:(j + 1) * LANES] for j in range(3))
    for j in range(AB_Q_TILES):
        t = _rope_tile(_tile(h, j), keep, hi, lo)
        q_ref[:, j * LANES:(j + 1) * LANES] = (t if j in AB_QIDX_TILES else t * QK_SCALE).astype(q_ref.dtype)
    k = jnp.concatenate([_rope_tile(_tile(h, AB_Q_TILES + j), keep, hi, lo) for j in range(AB_K_TILES)], axis=1)
    kt_ref[...] = k.T.astype(kt_ref.dtype)
    low = lax.broadcasted_iota(i32, keep.shape, 1) < HEAD_DIM
    for j in range(AB_V_HEADS):
        t = _tile(h, AB_Q_TILES + AB_K_TILES + j // 2)
        t = pltpu.roll(t, HEAD_DIM, 1) if j % 2 else t
        v_ref[:, j * LANES:(j + 1) * LANES] = jnp.where(low, t, 1.0).astype(v_ref.dtype)
    misc_ref[...] = _tile(h, AB_MISC_TILE)
    cmp_ref[...] = h[:, AB_CMP_TILE * LANES:AB_TILES * LANES]


def project_pack_ab(x, w, coef, tm=512):
    bsz, seq, d = x.shape
    row = lambda width: pl.BlockSpec((None, tm, width), lambda b, i: (b, i, 0))
    kt_rows = AB_K_TILES * LANES
    return pl.pallas_call(
        _proj_pack_ab_kernel,
        grid=(bsz, seq // tm),
        in_specs=[row(d), pl.BlockSpec((d, AB_TILES * LANES), lambda b, i: (0, 0)), row(3 * LANES)],
        out_specs=[row(AB_Q_TILES * LANES), pl.BlockSpec((None, kt_rows, tm), lambda b, i: (b, 0, i)),
                   row(AB_V_HEADS * LANES), row(LANES), row(2 * LANES)],
        out_shape=[jax.ShapeDtypeStruct((bsz, seq, AB_Q_TILES * LANES), MXU_DTYPE),
                   jax.ShapeDtypeStruct((bsz, kt_rows, seq), MXU_DTYPE),
                   jax.ShapeDtypeStruct((bsz, seq, AB_V_HEADS * LANES), MXU_DTYPE),
                   jax.ShapeDtypeStruct((bsz, seq, LANES), f32),
                   jax.ShapeDtypeStruct((bsz, seq, 2 * LANES), f32)],
        compiler_params=_params("parallel", "parallel"),
        name="project_pack_ab",
    )(x, w, coef)


def _proj_pack_c_kernel(x_ref, w_ref, coef_ref, q_ref, kt_ref, v_ref):
    h = _mm(x_ref[...].astype(MXU_DTYPE), w_ref[...])
    keep, hi, lo = (coef_ref[:, j * LANES:(j + 1) * LANES] for j in range(3))
    n = q_ref.shape[1] // LANES
    for j in range(n):
        q_ref[:, j * LANES:(j + 1) * LANES] = (_rope_tile(_tile(h, j), keep, hi, lo) * QK_SCALE).astype(q_ref.dtype)
    k = jnp.concatenate([_rope_tile(_tile(h, n + j), keep, hi, lo) for j in range(n)], axis=1)
    kt_ref[...] = k.T.astype(kt_ref.dtype)
    ones = jnp.ones(keep.shape, v_ref.dtype)
    for j in range(n):
        v_ref[:, 2 * j * LANES:(2 * j + 1) * LANES] = _tile(h, 2 * n + j).astype(v_ref.dtype)
        v_ref[:, (2 * j + 1) * LANES:(2 * j + 2) * LANES] = ones


def project_pack_c(x, w, coef, tm=512):
    bsz, seq, d = x.shape
    width = w.shape[1] // 3
    row = lambda cols: pl.BlockSpec((None, tm, cols), lambda b, i: (b, i, 0))
    return pl.pallas_call(
        _proj_pack_c_kernel,
        grid=(bsz, seq // tm),
        in_specs=[row(d), pl.BlockSpec((d, 3 * width), lambda b, i: (0, 0)), row(3 * LANES)],
        out_specs=[row(width), pl.BlockSpec((None, width, tm), lambda b, i: (b, 0, i)), row(2 * width)],
        out_shape=[jax.ShapeDtypeStruct((bsz, seq, width), MXU_DTYPE),
                   jax.ShapeDtypeStruct((bsz, width, seq), MXU_DTYPE),
                   jax.ShapeDtypeStruct((bsz, seq, 2 * width), MXU_DTYPE)],
        compiler_params=_params("parallel", "parallel"),
        name="project_pack_c",
    )(x, w, coef)


def _proj_ln_kernel(a_ref, w_ref, res_ref, g_ref, b_ref, o_ref):
    h = _mm(a_ref[...].astype(MXU_DTYPE), w_ref[...])
    o_ref[...] = _layer_norm_rows(DEEPNORM_ALPHA * res_ref[...] + h, g_ref[...], b_ref[...])


def project_residual_ln(a, w, res, g, b, tm=512):
    m, k = a.shape
    n = w.shape[1]
    return pl.pallas_call(
        _proj_ln_kernel,
        grid=(m // tm,),
        in_specs=[pl.BlockSpec((tm, k), lambda i: (i, 0)), pl.BlockSpec((k, n), lambda i: (0, 0)),
                  pl.BlockSpec((tm, n), lambda i: (i, 0)),
                  pl.BlockSpec((1, n), lambda i: (0, 0)), pl.BlockSpec((1, n), lambda i: (0, 0))],
        out_specs=pl.BlockSpec((tm, n), lambda i: (i, 0)),
        out_shape=jax.ShapeDtypeStruct((m, n), f32),
        compiler_params=_params("parallel"),
        name="project_residual_ln",
    )(a, w, res, g.reshape(1, n), b.reshape(1, n))


def _swiglu_tile(xb, w1_ref, w3_ref, w2_ref):
    a = _mm(xb, w1_ref[...])
    h = (a * jax.nn.sigmoid(a)) * _mm(xb, w3_ref[...])
    return _mm(h.astype(MXU_DTYPE), w2_ref[...])


def _ffn_ln_kernel(x_ref, w1_ref, w3_ref, w2_ref, g_ref, b_ref, o_ref, xb_ref, acc_ref):
    f = pl.program_id(1)

    @pl.when(f == 0)
    def _():
        xb_ref[...] = x_ref[...].astype(MXU_DTYPE)
        acc_ref[...] = jnp.zeros_like(acc_ref)

    acc_ref[...] += _swiglu_tile(xb_ref[...], w1_ref, w3_ref, w2_ref)

    @pl.when(f == pl.num_programs(1) - 1)
    def _():
        o_ref[...] = _layer_norm_rows(DEEPNORM_ALPHA * x_ref[...] + acc_ref[...], g_ref[...], b_ref[...])


def ffn_residual_ln(x, w1, w3, w2, g, b, tm, tf):
    m, d = x.shape
    ff = w1.shape[1]
    resident = dict(pipeline_mode=pl.Buffered(1)) if tf == ff else {}
    return pl.pallas_call(
        _ffn_ln_kernel,
        grid=(m // tm, ff // tf),
        in_specs=[pl.BlockSpec((tm, d), lambda i, f: (i, 0)),
                  pl.BlockSpec((d, tf), lambda i, f: (0, f), **resident),
                  pl.BlockSpec((d, tf), lambda i, f: (0, f), **resident),
                  pl.BlockSpec((tf, d), lambda i, f: (f, 0), **resident),
                  pl.BlockSpec((1, d), lambda i, f: (0, 0)), pl.BlockSpec((1, d), lambda i, f: (0, 0))],
        out_specs=pl.BlockSpec((tm, d), lambda i, f: (i, 0)),
        out_shape=jax.ShapeDtypeStruct((m, d), f32),
        scratch_shapes=[pltpu.VMEM((tm, d), MXU_DTYPE), pltpu.VMEM((tm, d), f32)],
        compiler_params=_params("parallel", "arbitrary"),
        name="ffn_residual_ln",
    )(x, w1, w3, w2, g.reshape(1, d), b.reshape(1, d))


ROUTE_IDS = N_EXPERTS
ROUTE_GATES = N_EXPERTS + 2


def _router_kernel(x_ref, w_ref, o_ref, *, n_experts):
    logits = jnp.dot(x_ref[...], w_ref[...], preferred_element_type=f32, precision=lax.Precision.HIGHEST)
    lane = lax.broadcasted_iota(i32, logits.shape, 1).astype(f32)
    logits = jnp.where(lane < n_experts, logits, -jnp.inf)
    v1 = jnp.max(logits, axis=1, keepdims=True)
    i1 = jnp.min(jnp.where(logits == v1, lane, float(LANES)), axis=1, keepdims=True)
    rest = jnp.where(lane == i1, -jnp.inf, logits)
    v2 = jnp.max(rest, axis=1, keepdims=True)
    i2 = jnp.min(jnp.where(rest == v2, lane, float(LANES)), axis=1, keepdims=True)
    e2 = jnp.exp(v2 - v1)
    g1 = 1.0 / (1.0 + e2)
    g2 = e2 / (1.0 + e2)
    out = jnp.where(lane == ROUTE_IDS, i1, 0.0) + jnp.where(lane == ROUTE_IDS + 1, i2, 0.0)
    out = out + jnp.where(lane == ROUTE_GATES, g1, 0.0) + jnp.where(lane == ROUTE_GATES + 1, g2, 0.0)
    o_ref[...] = out


def route_top2(x, router_w, tm=512):
    m, d = x.shape
    n_experts = router_w.shape[1]
    w = jnp.zeros((d, LANES), f32).at[:, :n_experts].set(router_w)
    return pl.pallas_call(
        functools.partial(_router_kernel, n_experts=n_experts),
        grid=(m // tm,),
        in_specs=[pl.BlockSpec((tm, d), lambda i: (i, 0)), pl.BlockSpec((d, LANES), lambda i: (0, 0))],
        out_specs=pl.BlockSpec((tm, LANES), lambda i: (i, 0)),
        out_shape=jax.ShapeDtypeStruct((m, LANES), f32),
        compiler_params=_params("parallel"),
        name="route_top2",
    )(x, w)


def _moe_ln_kernel(tok_ref, gs_ref, off_ref, x_ref, w1_ref, w3_ref, w2_ref, g_ref, b_ref, o_ref,
                   xg_ref, xb_ref, y_ref, *, rt):
    c = pl.program_id(0)
    e = pl.program_id(1)
    f = pl.program_id(2)
    last_f = pl.num_programs(2) - 1
    start = off_ref[0, e]
    count = off_ref[0, e + 1] - start
    n_tiles = (count + rt - 1) // rt

    @pl.when((c == 0) & (e == 0) & (f == 0))
    def _():
        xg_ref[...] = jnp.zeros_like(xg_ref)

    @pl.when((e == 0) & (f == 0))
    def _():
        o_ref[...] = jnp.zeros_like(o_ref)

    def row_loop(body):
        def group(j, carry):
            for u in range(ROW_UNROLL):
                body(j * ROW_UNROLL + u)
            return carry

        def single(r, carry):
            body(r)
            return carry
        lax.fori_loop(0, count // ROW_UNROLL, group, 0)
        lax.fori_loop((count // ROW_UNROLL) * ROW_UNROLL, count, single, 0)

    @pl.when(f == 0)
    def _():
        def gather(r):
            t = tok_ref[0, start + r]
            xg_ref[pl.ds(r, 1), :] = x_ref[pl.ds(t, 1), :]
        row_loop(gather)

        def cast(j, carry):
            rows = pl.ds(pl.multiple_of(j * rt, rt), rt)
            xb_ref[rows, :] = xg_ref[rows, :].astype(MXU_DTYPE)
            return carry
        lax.fori_loop(0, n_tiles, cast, 0)

    def tile(j, carry):
        rows = pl.ds(pl.multiple_of(j * rt, rt), rt)
        y = _swiglu_tile(xb_ref[rows, :], w1_ref, w3_ref, w2_ref)

        @pl.when(f == 0)
        def _():
            y_ref[rows, :] = y

        @pl.when(f != 0)
        def _():
            y_ref[rows, :] += y
        return carry
    lax.fori_loop(0, n_tiles, tile, 0)

    @pl.when(f == last_f)
    def _():
        def updated(r):
            t = tok_ref[0, start + r]
            return t, o_ref[pl.ds(t, 1), :] + gs_ref[0, start + r] * y_ref[pl.ds(r, 1), :]

        def group(j, carry):
            rows = [updated(j * ROW_UNROLL + u) for u in range(ROW_UNROLL)]
            for t, row in rows:
                o_ref[pl.ds(t, 1), :] = row
            return carry

        def single(r, carry):
            t, row = updated(r)
            o_ref[pl.ds(t, 1), :] = row
            return carry
        lax.fori_loop(0, count // ROW_UNROLL, group, 0)
        lax.fori_loop((count // ROW_UNROLL) * ROW_UNROLL, count, single, 0)

    @pl.when((e == pl.num_programs(1) - 1) & (f == last_f))
    def _():
        o_ref[...] = _layer_norm_rows(DEEPNORM_ALPHA * x_ref[...] + o_ref[...], g_ref[...], b_ref[...])


def moe_residual_ln(x, routes, w1, w3, w2, g, b, tm, tf, rt=128):
    m, d = x.shape
    n_experts, _, ff = w1.shape
    n_chunks = m // tm
    ids = routes[:, ROUTE_IDS:ROUTE_IDS + TOP_K].astype(i32).reshape(n_chunks, tm * TOP_K)
    gts = routes[:, ROUTE_GATES:ROUTE_GATES + TOP_K].reshape(n_chunks, tm * TOP_K)
    order = jnp.argsort(ids, axis=1, stable=True).astype(i32)
    tok = order // TOP_K
    gs = jnp.take_along_axis(gts, order, axis=1)
    counts = jnp.sum(ids[:, :, None] == jnp.arange(n_experts, dtype=i32)[None, None, :], axis=1, dtype=i32)
    offs = jnp.concatenate([jnp.zeros((n_chunks, 1), i32), jnp.cumsum(counts, axis=1, dtype=i32)], axis=1)
    smem = lambda width: pl.BlockSpec((None, 1, width), lambda c, e, f: (c, 0, 0), memory_space=pltpu.SMEM)
    return pl.pallas_call(
        functools.partial(_moe_ln_kernel, rt=rt),
        grid=(n_chunks, n_experts, ff // tf),
        in_specs=[smem(tm * TOP_K), smem(tm * TOP_K), smem(n_experts + 1),
                  pl.BlockSpec((tm, d), lambda c, e, f: (c, 0), pipeline_mode=pl.Buffered(1)),
                  pl.BlockSpec((None, d, tf), lambda c, e, f: (e, 0, f)),
                  pl.BlockSpec((None, d, tf), lambda c, e, f: (e, 0, f)),
                  pl.BlockSpec((None, tf, d), lambda c, e, f: (e, f, 0)),
                  pl.BlockSpec((1, d), lambda c, e, f: (0, 0)), pl.BlockSpec((1, d), lambda c, e, f: (0, 0))],
        out_specs=pl.BlockSpec((tm, d), lambda c, e, f: (c, 0), pipeline_mode=pl.Buffered(1)),
        out_shape=jax.ShapeDtypeStruct((m, d), f32),
        scratch_shapes=[pltpu.VMEM((tm, d), f32), pltpu.VMEM((tm, d), MXU_DTYPE), pltpu.VMEM((tm, d), f32)],
        compiler_params=_params("arbitrary", "arbitrary", "arbitrary"),
        name="moe_residual_ln",
    )(tok[:, None, :], gs[:, None, :], offs[:, None, :], x, w1, w3, w2, g.reshape(1, d), b.reshape(1, d))


def _flash_reset(m_ref, acc_ref):
    m_ref[...] = jnp.full(m_ref.shape, NEG, f32)
    acc_ref[...] = jnp.zeros(acc_ref.shape, f32)


def _flash_rows(q, kt, bias, m_ref, alpha_ref, p_ref, rows):
    s = _mm(q, kt)
    if bias is not None:
        s = s + bias
    m_prev = m_ref[rows, :]
    m_next = jnp.maximum(m_prev, jnp.max(s, axis=1, keepdims=True))
    alpha_ref[rows, :] = jnp.exp2(m_prev - m_next)
    m_ref[rows, :] = m_next
    p_ref[rows, :s.shape[1]] = jnp.exp2(s - _lane_tile(m_next, s.shape[1] // LANES)).astype(p_ref.dtype)


def _flash_accumulate(v_aug, alpha_ref, p_ref, acc_ref):
    alpha = _lane_tile(alpha_ref[...], acc_ref.shape[-1] // LANES)
    acc_ref[...] = alpha * acc_ref[...] + _mm(p_ref[:, :v_aug.shape[0]], v_aug)


def _skewed_chunks(n, score, fold, carry):
    def pair(j, carry):
        c = 2 * j
        fold(c - 1, 1)
        carry = score(c, 0, carry)
        fold(c, 0)
        return score(c + 1, 1, carry)

    def single(c, carry):
        fold(c - 1, 1)
        return score(c, 0, carry)

    carry = lax.fori_loop(0, n // 2, pair, carry)
    return lax.fori_loop(2 * (n // 2), n, single, carry)


def _fold_last(n, fold):
    for buf in range(2):
        pl.when((n - 1) % 2 == buf)(functools.partial(fold, n - 1, buf))


def _lane_tile(x, reps):
    return x if reps == 1 else jnp.concatenate([x] * reps, axis=1)


def _row_slices(n_rows):
    return [slice(r * Q_BLOCK, (r + 1) * Q_BLOCK) for r in range(n_rows // Q_BLOCK)]


def _dsa_kernel(qi_ref, wi_ref, kit_ref, qa_ref, kat_ref, va_ref, o_ref,
                skey_ref, wrep_ref, m_ref, alpha_ref, acc_ref, p_ref, qh_ref, *, topk, kc):
    qb = Q_BLOCK
    i = pl.program_id(1)
    q0 = i * qb
    n_chunks = (q0 + qb + kc - 1) // kc
    t_row = q0 + lax.broadcasted_iota(i32, (qb, kc), 0)
    col = lax.broadcasted_iota(i32, (qb, kc), 1)
    wi = wi_ref[...]
    qi = qi_ref[...]
    qis = [qi[:, h * IDX_DIM:(h + 1) * IDX_DIM] for h in range(IDX_HEADS)]
    qa = qa_ref[...]
    for h in range(A_HEADS):
        qh_ref[h * qb:(h + 1) * qb, :] = qa[:, h * HEAD_DIM:(h + 1) * HEAD_DIM]
    for h in range(IDX_HEADS):
        wrep_ref[h] = jnp.broadcast_to(wi[:, h:h + 1], (qb, kc))

    def score_body(c, carry):
        off = pl.multiple_of(c * kc, kc)
        kt = kit_ref[:, pl.ds(off, kc)]
        s = jnp.zeros((qb, kc), f32)
        for h in range(IDX_HEADS):
            s = s + wrep_ref[h] * jnp.maximum(_mm(qis[h], kt), 0.0)
        s = jnp.where(col + off <= t_row, s, NEG)
        bits = pltpu.bitcast(s, i32)
        skey_ref[:, pl.ds(off, kc)] = jnp.where(bits < 0, bits ^ 0x7FFFFFFF, bits)
        return carry

    n_strips = (q0 + qb + COUNT_STRIP - 1) // COUNT_STRIP
    lax.fori_loop(0, n_strips * (COUNT_STRIP // kc), score_body, 0)

    def count_ge(cand):
        def body(c, acc):
            off = pl.multiple_of(c * COUNT_STRIP, COUNT_STRIP)
            for j in range(COUNT_STRIP // LANES):
                acc = acc + jnp.where(skey_ref[:, pl.ds(off + j * LANES, LANES)] >= cand, 1.0, 0.0)
            return acc
        acc = lax.fori_loop(0, n_strips, body, jnp.zeros((qb, LANES), f32))
        return jnp.sum(acc, axis=1, keepdims=True)

    thr = jnp.where(count_ge(jnp.zeros((qb, 1), i32)) >= topk, 0, INT_MIN).astype(i32)

    def bisect(b, thr):
        cand = thr + jnp.left_shift(jnp.int32(1), 30 - b)
        return jnp.where(count_ge(cand) >= topk, cand, thr)

    thr = lax.fori_loop(0, 31, bisect, thr)
    quota = topk - count_ge(thr + 1)

    rank_w = min(kc, RANK_PIECE)
    before = (lax.broadcasted_iota(i32, (rank_w, rank_w), 0) < lax.broadcasted_iota(i32, (rank_w, rank_w), 1))
    before = jnp.where(before, 1.0, 0.0).astype(MXU_DTYPE)
    _flash_reset(m_ref, acc_ref)
    alpha_ref[1] = jnp.zeros(alpha_ref.shape[1:], f32)
    p_ref[1] = jnp.zeros(p_ref.shape[1:], p_ref.dtype)
    heads = _row_slices(A_HEADS * qb)

    def fold(c, buf):
        off = pl.multiple_of(jnp.maximum(c, 0) * kc, kc)
        _flash_accumulate(va_ref[pl.ds(off, kc), :], alpha_ref.at[buf], p_ref.at[buf], acc_ref)

    def score(c, buf, ties_seen):
        off = pl.multiple_of(c * kc, kc)
        key = skey_ref[:, pl.ds(off, kc)]
        eq = key == thr
        eqf = jnp.where(eq, 1.0, 0.0)
        ranks = []
        for j in range(kc // rank_w):
            piece = eqf[:, j * rank_w:(j + 1) * rank_w]
            ranks.append(ties_seen + _mm(piece.astype(MXU_DTYPE), before))
            ties_seen = ties_seen + jnp.sum(piece, axis=1, keepdims=True)
        rank = jnp.concatenate(ranks, axis=1)
        sel = ((key > thr) | (eq & (rank < quota))) & (col + off <= t_row)
        bias = jnp.where(sel, 0.0, NEG)
        kt = kat_ref[:, pl.ds(off, kc)]
        for rows in heads:
            _flash_rows(qh_ref[rows, :], kt, bias, m_ref, alpha_ref.at[buf], p_ref.at[buf], rows)
        return ties_seen

    _skewed_chunks(n_chunks, score, fold, jnp.zeros((qb, 1), f32))
    _fold_last(n_chunks, fold)
    acc = acc_ref[...]
    o = acc[:, :HEAD_DIM] / acc[:, HEAD_DIM:]
    for h in range(A_HEADS):
        o_ref[:, h * HEAD_DIM:(h + 1) * HEAD_DIM] = o[h * qb:(h + 1) * qb, :]


QPK_QA, QPK_QIDX, QPK_QB = 0, 2, 3
KPK_KA, KPK_KIDX, KPK_KSLC, KPK_KWIN = 0, 1, 2, 4
VPK_VA, VPK_VSLC, VPK_VWIN = 0, 1, 3
OUT_NSA = 2


def dsa_attention(wi, qpk, kpk, vpk, topk, kc=512):
    bsz, seq, _ = qpk.shape
    nb = seq // Q_BLOCK
    rows = A_HEADS * Q_BLOCK
    width = A_HEADS * HEAD_DIM
    kern = functools.partial(_dsa_kernel, topk=topk, kc=kc)
    return pl.pallas_call(
        kern,
        grid=(bsz, nb),
        in_specs=[pl.BlockSpec((None, Q_BLOCK, IDX_HEADS * IDX_DIM), lambda b, i: (b, i, QPK_QIDX)),
                  pl.BlockSpec((None, Q_BLOCK, IDX_HEADS), lambda b, i: (b, i, 0)),
                  pl.BlockSpec((None, IDX_DIM, seq), lambda b, i: (b, KPK_KIDX, 0)),
                  pl.BlockSpec((None, Q_BLOCK, width), lambda b, i: (b, i, QPK_QA)),
                  pl.BlockSpec((None, HEAD_DIM, seq), lambda b, i: (b, KPK_KA, 0)),
                  pl.BlockSpec((None, seq, 2 * HEAD_DIM), lambda b, i: (b, 0, VPK_VA))],
        out_specs=pl.BlockSpec((None, Q_BLOCK, width), lambda b, i: (b, i, 0)),
        out_shape=jax.ShapeDtypeStruct((bsz, seq, 2 * width), f32),
        scratch_shapes=[pltpu.VMEM((Q_BLOCK, seq), i32), pltpu.VMEM((IDX_HEADS, Q_BLOCK, kc), f32),
                        pltpu.VMEM((rows, LANES), f32), pltpu.VMEM((2, rows, LANES), f32),
                        pltpu.VMEM((rows, 2 * HEAD_DIM), f32), pltpu.VMEM((2, rows, kc), MXU_DTYPE),
                        pltpu.VMEM((rows, HEAD_DIM), MXU_DTYPE)],
        compiler_params=_params("parallel", "arbitrary"),
        name="dsa_attention",
    )(qpk, wi, kpk, qpk, kpk, vpk)


def _compress_kernel(x_ref, pe_ref, w1_ref, w2_ref, o_ref):
    n_rows = o_ref.shape[1]
    parts = CMP_LEN // CMP_STRIDE
    acc = [[jnp.zeros((n_rows, CMP_HIDDEN), f32) for _ in range(parts)] for _ in range(B_KV_GROUPS)]
    for t in range(CMP_STRIDE):
        tok = x_ref[pl.ds(t, n_rows, stride=CMP_STRIDE), :]
        for g in range(B_KV_GROUPS):
            tok_g = tok[:, g * HEAD_DIM:(g + 1) * HEAD_DIM]
            for part in range(parts):
                l = part * CMP_STRIDE + t
                lhs = (tok_g + pe_ref[l:l + 1, :]).astype(MXU_DTYPE)
                acc[g][part] = acc[g][part] + _mm(lhs, w1_ref[l * HEAD_DIM:(l + 1) * HEAD_DIM, :])
    for g in range(B_KV_GROUPS):
        pre = acc[g][0] + pltpu.roll(acc[g][1], shift=n_rows - 1, axis=0)
        hid = pre * jax.nn.sigmoid(pre)
        o_ref[g] = _mm(hid.astype(MXU_DTYPE), w2_ref[...])


def nsa_compress(cmp, pe, w1, w2):
    bsz, seq, _ = cmp.shape
    n_rows = seq // CMP_STRIDE
    width = B_KV_GROUPS * HEAD_DIM
    return pl.pallas_call(
        _compress_kernel,
        grid=(bsz, 2),
        in_specs=[pl.BlockSpec((None, seq, width), lambda b, s: (b, 0, s)),
                  pl.BlockSpec((None, CMP_LEN, HEAD_DIM), lambda b, s: (s, 0, 0)),
                  pl.BlockSpec((None, CMP_LEN * HEAD_DIM, CMP_HIDDEN), lambda b, s: (s, 0, 0)),
                  pl.BlockSpec((None, CMP_HIDDEN, HEAD_DIM), lambda b, s: (s, 0, 0))],
        out_specs=pl.BlockSpec((None, None, B_KV_GROUPS, n_rows, HEAD_DIM), lambda b, s: (b, s, 0, 0, 0)),
        out_shape=jax.ShapeDtypeStruct((bsz, 2, B_KV_GROUPS, n_rows, HEAD_DIM), f32),
        compiler_params=_params("parallel", "parallel"),
        name="nsa_compress",
    )(cmp, pe, w1, w2)


def _nsa_kernel(qin_ref, graw_ref, kct_ref, vc_ref, kst_ref, vs_ref, kwt_ref, vw_ref, ovl_ref, _, o_ref,
                m_ref, alpha_ref, acc_ref, p_ref, pw_ref, q_ref, imp_ref, *, n_slc, n_sel, kc):
    qb = Q_BLOCK
    i = pl.program_id(2)
    q0 = i * qb
    n_cmp = kct_ref.shape[1]
    heads = _row_slices(B_REP * qb)
    qin = qin_ref[...]
    for r in range(B_REP):
        q_ref[r * qb:(r + 1) * qb, :] = qin[:, r * HEAD_DIM:(r + 1) * HEAD_DIM]

    def compressed(width):
        t_c = q0 + lax.broadcasted_iota(i32, (qb, width), 0)
        cmp_end = lax.broadcasted_iota(i32, (qb, width), 1) * CMP_STRIDE + (CMP_LEN - 1)
        vis = cmp_end <= t_c
        kct = kct_ref[:, :width]
        p_sum = jnp.zeros((qb, width), f32)
        for rows in heads:
            lc = jnp.where(vis, _mm(q_ref[rows, :], kct), NEG)
            ec = jnp.where(vis, jnp.exp2(lc - jnp.max(lc, axis=1, keepdims=True)), 0.0)
            den = jnp.sum(ec, axis=1, keepdims=True)
            p_c = ec / jnp.where(den > 0.0, den, 1.0)
            p_sum = p_sum + p_c
            acc_ref[rows, :HEAD_DIM] = _mm(p_c.astype(MXU_DTYPE), vc_ref[:width, :])
        imp_ref[...] = jnp.dot(p_sum, ovl_ref[:width, :], preferred_element_type=f32,
                               precision=lax.Precision.HIGHEST)

    if n_cmp % (2 * LANES) == 0:
        first_half_only = (q0 + qb) * 2 <= n_cmp * CMP_STRIDE
        pl.when(first_half_only)(functools.partial(compressed, n_cmp // 2))
        pl.when(jnp.logical_not(first_half_only))(functools.partial(compressed, n_cmp))
    else:
        compressed(n_cmp)
    o_c = acc_ref[:, :HEAD_DIM]

    imp = imp_ref[...].T
    t_q = q0 + lax.broadcasted_iota(i32, (LANES, qb), 1)
    blk = lax.broadcasted_iota(i32, (LANES, qb), 0)
    blk_t = t_q // SLC_LEN
    forced = (blk == 0) | (blk == blk_t) | (blk == blk_t - 1)
    imp = jnp.where(forced, FORCED_BOOST, imp)
    imp = jnp.where(blk * SLC_LEN <= t_q, imp, NEG)
    imp = jnp.where(blk < n_slc, imp, -jnp.inf)
    blk_f = blk.astype(f32)

    def pick(_, carry):
        imp, selm = carry
        best = jnp.max(imp, axis=0, keepdims=True)
        first = jnp.min(jnp.where(imp == best, blk_f, float(LANES)), axis=0, keepdims=True)
        hit = blk_f == first
        return jnp.where(hit, -jnp.inf, imp), jnp.where(hit, 1.0, selm)

    _, selm = lax.fori_loop(0, n_sel, pick, (imp, jnp.zeros((LANES, qb), f32)))
    selm = selm.T.astype(MXU_DTYPE)

    t_k = q0 + lax.broadcasted_iota(i32, (qb, kc), 0)
    col = lax.broadcasted_iota(i32, (qb, kc), 1)
    exp_row = lax.broadcasted_iota(i32, (LANES, kc), 0)
    exp_col = lax.broadcasted_iota(i32, (LANES, kc), 1)
    _flash_reset(m_ref, acc_ref)
    alpha_ref[1] = jnp.zeros(alpha_ref.shape[1:], f32)
    p_ref[1] = jnp.zeros(p_ref.shape[1:], p_ref.dtype)

    def fold(c, buf):
        off = pl.multiple_of(jnp.maximum(c, 0) * kc, kc)
        _flash_accumulate(vs_ref[pl.ds(off, kc), :], alpha_ref.at[buf], p_ref.at[buf], acc_ref)

    def score(c, buf, carry):
        off = pl.multiple_of(c * kc, kc)
        expand = jnp.where(exp_row == (exp_col + off) // SLC_LEN, 1.0, 0.0).astype(MXU_DTYPE)
        sel = (_mm(selm, expand) > 0.5) & (col + off <= t_k)
        bias = jnp.where(sel, 0.0, NEG)
        kt = kst_ref[:, pl.ds(off, kc)]
        for rows in heads:
            _flash_rows(q_ref[rows, :], kt, bias, m_ref, alpha_ref.at[buf], p_ref.at[buf], rows)
        return carry

    n_slc_chunks = (q0 + qb + kc - 1) // kc
    _skewed_chunks(n_slc_chunks, score, fold, 0)
    _fold_last(n_slc_chunks, fold)
    acc = acc_ref[...]
    o_s = acc[:, :HEAD_DIM] / acc[:, HEAD_DIM:]

    slab = WINDOW + qb
    w0 = pl.multiple_of(jnp.maximum(q0 - WINDOW, 0), qb)
    dist = (q0 + lax.broadcasted_iota(i32, (qb, slab), 0)) - (w0 + lax.broadcasted_iota(i32, (qb, slab), 1))
    bias = jnp.where((dist >= 0) & (dist < WINDOW), 0.0, NEG)
    kt = kwt_ref[:, pl.ds(w0, slab)]
    for rows in heads:
        s = _mm(q_ref[rows, :], kt) + bias
        pw_ref[rows, :] = jnp.exp2(s - jnp.max(s, axis=1, keepdims=True)).astype(pw_ref.dtype)
    acc = _mm(pw_ref[...], vw_ref[pl.ds(w0, slab), :])
    o_w = acc[:, :HEAD_DIM] / acc[:, HEAD_DIM:]

    gates = jax.nn.sigmoid(graw_ref[...])
    for r, rows in enumerate(heads):
        g_c, g_s, g_w = (gates[:, 3 * r + n:3 * r + n + 1] for n in range(3))
        o_ref[:, r * HEAD_DIM:(r + 1) * HEAD_DIM] = g_c * o_c[rows] + g_s * o_s[rows] + g_w * o_w[rows]


def nsa_attention(qpk, graw, kct, vc, kpk, vpk, ovl, out_ab, n_slc, n_sel, kc=512):
    bsz, seq, _ = qpk.shape
    groups = B_KV_GROUPS
    nb = seq // Q_BLOCK
    rows = B_REP * Q_BLOCK
    width = B_REP * HEAD_DIM
    n_cmp = kct.shape[-1]
    kern = functools.partial(_nsa_kernel, n_slc=n_slc, n_sel=n_sel, kc=kc)
    per_bg = lambda b, g, i: (b, g, 0, 0)
    return pl.pallas_call(
        kern,
        grid=(bsz, groups, nb),
        in_specs=[pl.BlockSpec((None, Q_BLOCK, width), lambda b, g, i: (b, i, QPK_QB + g)),
                  pl.BlockSpec((None, None, Q_BLOCK, B_REP * 3), lambda b, g, i: (b, g, i, 0)),
                  pl.BlockSpec((None, None, HEAD_DIM, n_cmp), per_bg),
                  pl.BlockSpec((None, None, n_cmp, HEAD_DIM), per_bg),
                  pl.BlockSpec((None, HEAD_DIM, seq), lambda b, g, i: (b, KPK_KSLC + g, 0)),
                  pl.BlockSpec((None, seq, 2 * HEAD_DIM), lambda b, g, i: (b, 0, VPK_VSLC + g)),
                  pl.BlockSpec((None, HEAD_DIM, seq), lambda b, g, i: (b, KPK_KWIN + g, 0)),
                  pl.BlockSpec((None, seq, 2 * HEAD_DIM), lambda b, g, i: (b, 0, VPK_VWIN + g)),
                  pl.BlockSpec((n_cmp, LANES), lambda b, g, i: (0, 0)),
                  pl.BlockSpec(memory_space=pl.ANY)],
        out_specs=pl.BlockSpec((None, Q_BLOCK, width), lambda b, g, i: (b, i, OUT_NSA + g)),
        out_shape=jax.ShapeDtypeStruct(out_ab.shape, f32),
        input_output_aliases={9: 0},
        scratch_shapes=[pltpu.VMEM((rows, LANES), f32), pltpu.VMEM((2, rows, LANES), f32),
                        pltpu.VMEM((rows, 2 * HEAD_DIM), f32), pltpu.VMEM((2, rows, kc), MXU_DTYPE),
                        pltpu.VMEM((rows, WINDOW + Q_BLOCK), MXU_DTYPE), pltpu.VMEM((rows, HEAD_DIM), MXU_DTYPE),
                        pltpu.VMEM((Q_BLOCK, LANES), f32)],
        compiler_params=_params("parallel", "parallel", "arbitrary"),
        name="nsa_attention",
    )(qpk, graw, kct, vc, kpk, vpk, kpk, vpk, ovl, out_ab)


def _diff_kernel(lam_ref, qin_ref, kt_ref, v_ref, g_ref, o_ref, m_ref, alpha_ref, acc_ref, p_ref, q_ref,
                 *, tq, kc, out_scale):
    i = pl.program_id(2)
    q0 = i * tq
    dv = v_ref.shape[-1] // 2
    groups = _row_slices(tq)
    qin = qin_ref[...]
    for half in range(2):
        q_ref[half] = qin[:, half * HEAD_DIM:(half + 1) * HEAD_DIM]
    for half in range(2):
        _flash_reset(m_ref.at[half], acc_ref.at[half])
    alpha_ref[1] = jnp.zeros(alpha_ref.shape[1:], f32)
    p_ref[1] = jnp.zeros(p_ref.shape[1:], p_ref.dtype)

    def fold(c, buf, sub=(0, kc // tq)):
        off = pl.multiple_of(jnp.maximum(c, 0) * kc + sub[0] * tq, tq)
        v = v_ref[pl.ds(off, sub[1] * tq), :]
        for half in range(2):
            _flash_accumulate(v, alpha_ref.at[buf, half], p_ref.at[buf, half], acc_ref.at[half])

    def score(c, buf, masked, sub=(0, kc // tq)):
        off = pl.multiple_of(c * kc + sub[0] * tq, tq)
        width = sub[1] * tq
        for half in range(2):
            kt = kt_ref[half * HEAD_DIM:(half + 1) * HEAD_DIM, pl.ds(off, width)]
            for r, rows in enumerate(groups):
                bias = None
                if masked:
                    key = off + lax.broadcasted_iota(i32, (Q_BLOCK, width), 1)
                    t = q0 + r * Q_BLOCK + lax.broadcasted_iota(i32, (Q_BLOCK, width), 0)
                    bias = jnp.where(key <= t, 0.0, NEG)
                _flash_rows(q_ref[half, rows, :], kt, bias, m_ref.at[half], alpha_ref.at[buf, half],
                            p_ref.at[buf, half], rows)

    def score_full(c, buf, carry):
        score(c, buf, False)
        return carry

    n_full = q0 // kc
    _skewed_chunks(n_full, score_full, fold, 0)

    own = (q0 - n_full * kc) // tq

    def finish(buf, own_piece):
        fold(n_full - 1, 1 - buf)
        if own_piece:
            score(n_full, buf, False, (0, own_piece))
            fold(n_full, buf, (0, own_piece))
            buf = 1 - buf
        score(n_full, buf, True, (own_piece, 1))
        fold(n_full, buf, (own_piece, 1))

    for buf in range(2):
        for own_piece in range(kc // tq):
            pl.when((n_full % 2 == buf) & (own == own_piece))(functools.partial(finish, buf, own_piece))
    a1 = acc_ref[0]
    a2 = acc_ref[1]
    o = a1[:, :dv] / a1[:, dv:] - lam_ref[0] * (a2[:, :dv] / a2[:, dv:])
    o = o * lax.rsqrt(jnp.mean(o * o, axis=-1, keepdims=True) + LN_EPS) * g_ref[...]
    o_ref[...] = o * out_scale


def diff_attention(lam, q, kt, v, subln_g, out_scale, tq=512, kc=1024):
    bsz, seq, width = q.shape
    dv = 2 * HEAD_DIM
    heads = width // dv
    tq = min(tq, seq)
    kc = min(kc, seq)
    assert kc % tq == 0 and seq % kc == 0
    kern = functools.partial(_diff_kernel, tq=tq, kc=kc, out_scale=out_scale)
    return pl.pallas_call(
        kern,
        grid=(bsz, heads, seq // tq),
        in_specs=[pl.BlockSpec(memory_space=pltpu.SMEM),
                  pl.BlockSpec((None, tq, dv), lambda b, h, i: (b, i, h)),
                  pl.BlockSpec((None, dv, seq), lambda b, h, i: (b, h, 0)),
                  pl.BlockSpec((None, seq, 2 * dv), lambda b, h, i: (b, 0, h)),
                  pl.BlockSpec((1, dv), lambda b, h, i: (0, 0))],
        out_specs=pl.BlockSpec((None, tq, dv), lambda b, h, i: (b, i, h)),
        out_shape=jax.ShapeDtypeStruct((bsz, seq, heads * dv), f32),
        scratch_shapes=[pltpu.VMEM((2, tq, LANES), f32), pltpu.VMEM((2, 2, tq, LANES), f32),
                        pltpu.VMEM((2, tq, 2 * dv), f32), pltpu.VMEM((2, 2, tq, kc), MXU_DTYPE),
                        pltpu.VMEM((2, tq, HEAD_DIM), MXU_DTYPE)],
        compiler_params=_params("parallel", "parallel", "arbitrary"),
        name="diff_attention",
    )(lam, q, kt, v, subln_g.reshape(1, dv))


def _rope_tables(positions):
    inv_freq = ROPE_THETA ** (-jnp.arange(0, ROT_DIM, 2, dtype=f32) / ROT_DIM)
    ang = positions.astype(f32)[..., None] * inv_freq
    return jnp.cos(ang), jnp.sin(ang)


def _apply_rope(x, cos, sin):
    shape = cos.shape[:2] + (1,) * (x.ndim - 3) + cos.shape[-1:]
    c = cos.reshape(shape)
    s = sin.reshape(shape)
    half = ROT_DIM // 2
    x1, x2 = x[..., :half], x[..., half:ROT_DIM]
    return jnp.concatenate([x1 * c - x2 * s, x2 * c + x1 * s, x[..., ROT_DIM:]], axis=-1)


def _rope_coefficients(cos, sin):
    half = ROT_DIM // 2
    spread = np.zeros((ROT_DIM, 3 * LANES), np.float32)
    offset = np.zeros((3 * LANES,), np.float32)
    for lane in range(LANES):
        d = lane % HEAD_DIM
        if d >= ROT_DIM:
            offset[lane] = 1.0
        elif d < half:
            spread[d, lane], spread[half + d, LANES + lane] = 1.0, -1.0
        else:
            spread[d - half, lane], spread[d, 2 * LANES + lane] = 1.0, 1.0
    table = jnp.concatenate([cos, sin], axis=-1)
    return jnp.dot(table, spread, precision=lax.Precision.HIGHEST) + offset


def _ab_weight_columns(w_in):
    widths = dict(AB_LAYOUT)
    starts = dict(zip(widths, np.cumsum([0] + [w for _, w in AB_LAYOUT[:-1]]).tolist()))
    cols = lambda names: np.concatenate([np.arange(starts[n], starts[n] + widths[n]) for n in names])
    first = ("q_a", "q_idx", "q_b", "k_a", "k_idx", "k_slc", "k_win", "v_a", "v_slc", "v_win", "gate_b", "w_idx")
    n_first = sum(widths[n] for n in first)
    pad = AB_CMP_TILE * LANES - n_first
    assert 0 <= pad < HEAD_DIM
    w = jnp.concatenate([w_in[:, cols(first)], jnp.zeros((w_in.shape[0], pad), w_in.dtype),
                         w_in[:, cols(("k_cmp", "v_cmp"))]], axis=1)
    assert w.shape[1] == AB_TILES * LANES
    return w.astype(MXU_DTYPE)


def _overlap_matrix(n_cmp_rows, n_slc):
    c_start = np.arange(n_cmp_rows) * CMP_STRIDE
    s_start = np.arange(LANES) * SLC_LEN
    ovl = (c_start[:, None] < s_start[None, :] + SLC_LEN) & (c_start[:, None] + CMP_LEN > s_start[None, :])
    ovl = ovl & (np.arange(LANES)[None, :] < n_slc)
    return jnp.asarray(ovl.astype(np.float32))


def _ab_mixer(x2, bsz, seq, positions, coef, w_in, pe_k, pe_v, ck1, ck2, cv1, cv2):
    qpk, kpk, vpk, misc, cmp = project_pack_ab(x2.reshape(bsz, seq, -1), _ab_weight_columns(w_in), coef)
    gate_w = B_HEADS * 3
    g_b = misc[..., HEAD_DIM:HEAD_DIM + gate_w]
    w_idx = misc[..., HEAD_DIM + gate_w:HEAD_DIM + gate_w + IDX_HEADS] * (IDX_HEADS * IDX_DIM) ** -0.5

    out_ab = dsa_attention(w_idx, qpk, kpk, vpk, topk=min(DSA_TOPK, seq // 4))

    groups = B_KV_GROUPS
    n_rows = seq // CMP_STRIDE
    n_cmp = (seq - CMP_LEN) // CMP_STRIDE + 1
    assert n_cmp == n_rows - 1
    n_slc = seq // SLC_LEN
    assert n_slc <= LANES

    kv_c = nsa_compress(cmp, jnp.stack([pe_k, pe_v]), jnp.stack([ck1, cv1]).astype(MXU_DTYPE),
                        jnp.stack([ck2, cv2]).astype(MXU_DTYPE))
    k_c, v_c = kv_c[:, 0], kv_c[:, 1]
    cmp_end = jnp.minimum(jnp.arange(n_rows) * CMP_STRIDE + CMP_LEN - 1, seq - 1)
    cos_c, sin_c = _rope_tables(positions[:, cmp_end])
    k_c = _apply_rope(k_c.transpose(0, 2, 1, 3), cos_c, sin_c)
    k_c = k_c.transpose(0, 2, 3, 1).astype(MXU_DTYPE)
    v_c = v_c.astype(MXU_DTYPE)

    graw = g_b.reshape(bsz, seq, groups, B_REP * 3).transpose(0, 2, 1, 3)
    out_ab = nsa_attention(qpk, graw, k_c, v_c, kpk, vpk, _overlap_matrix(n_rows, n_slc), out_ab,
                           n_slc=n_slc, n_sel=min(SLC_TOPN, n_slc))
    return out_ab.reshape(bsz * seq, (A_HEADS + B_HEADS) * HEAD_DIM)


def _diff_mixer(x2, bsz, seq, coef, w_in, lq1, lk1, lq2, lk2, subln_g, lam_init):
    q, kt, v = project_pack_c(x2.reshape(bsz, seq, -1), w_in.astype(MXU_DTYPE), coef)
    lam =(jnp.exp(jnp.sum(lq1 * lk1)) - jnp.exp(jnp.sum(lq2 * lk2)) + lam_init).reshape(1).astype(f32)
    o = diff_attention(lam, q, kt, v, subln_g, 1.0 - lam_init)
    return o.reshape(bsz * seq, C_HEADS * 2 * HEAD_DIM)


def kernel(x, positions, ab_w_in, cmp_pe_k, cmp_pe_v, cmp_k_w1, cmp_k_w2, cmp_v_w1, cmp_v_w2, ab_w_out, ln_ab_g, ln_ab_b, ffn_w1, ffn_w3, ffn_w2, ln_ffn_g, ln_ffn_b, c_w_in, lambda_q1, lambda_k1, lambda_q2, lambda_k2, c_subln_g, c_w_out, ln_c_g, ln_c_b, router_w, moe_w1, moe_w3, moe_w2, ln_moe_g, ln_moe_b):
    bsz, seq, d = x.shape
    assert seq % COUNT_STRIP == 0 and seq >= WINDOW + Q_BLOCK and d == D_MODEL
    coef = _rope_coefficients(*_rope_tables(positions))
    x2 = x.reshape(bsz * seq, d)
    for layer in range(DEPTH):
        i = layer // 2
        if layer % 2 == 0:
            o = _ab_mixer(x2, bsz, seq, positions, coef, ab_w_in[i], cmp_pe_k[i], cmp_pe_v[i],
                          cmp_k_w1[i], cmp_k_w2[i], cmp_v_w1[i], cmp_v_w2[i])
            x2 = project_residual_ln(o, ab_w_out[i].astype(MXU_DTYPE), x2, ln_ab_g[i], ln_ab_b[i])
            x2 = ffn_residual_ln(x2, ffn_w1[i].astype(MXU_DTYPE), ffn_w3[i].astype(MXU_DTYPE),
                                 ffn_w2[i].astype(MXU_DTYPE), ln_ffn_g[i], ln_ffn_b[i], tm=512, tf=ffn_w1.shape[-1])
        else:
            lam_init = 0.8 - 0.6 * math.exp(-0.3 * layer)
            o = _diff_mixer(x2, bsz, seq, coef, c_w_in[i], lambda_q1[i], lambda_k1[i], lambda_q2[i],
                            lambda_k2[i], c_subln_g[i], lam_init)
            x2 = project_residual_ln(o, c_w_out[i].astype(MXU_DTYPE), x2, ln_c_g[i], ln_c_b[i])
            routes = route_top2(x2, router_w[i])
            x2 = moe_residual_ln(x2, routes, moe_w1[i].astype(MXU_DTYPE), moe_w3[i].astype(MXU_DTYPE),
                                 moe_w2[i].astype(MXU_DTYPE), ln_moe_g[i], ln_moe_b[i], tm=min(2048, bsz * seq // 2), tf=896, rt=128)
    return x2.reshape(bsz, seq, d)
```

```python
import functools
import math

import numpy as np
import jax
import jax.numpy as jnp
from jax import lax
from jax.experimental import pallas as pl
from jax.experimental.pallas import tpu as pltpu

f32 = jnp.float32
i32 = jnp.int32
MXU_DTYPE = jnp.bfloat16
VMEM_LIMIT_BYTES = 56 * 1024 * 1024
LANES = 128

D_MODEL = 1024
DEPTH = 2
HEAD_DIM = 64
ROT_DIM = HEAD_DIM // 4
ROPE_THETA = 500000.0
Q_BLOCK = 128
NEG = -1e30
LN_EPS = 1e-5
A_HEADS = 8
IDX_HEADS = 4
IDX_DIM = 64
DSA_TOPK = 256
B_HEADS = 8
B_KV_GROUPS = 2
B_REP = B_HEADS // B_KV_GROUPS
CMP_LEN = 32
CMP_STRIDE = 16
CMP_HIDDEN = 128
SLC_LEN = 64
SLC_TOPN = 16
WINDOW = 512
FORCED_BOOST = 1e6
C_HEADS = 8
N_EXPERTS = 8
TOP_K = 2
DEEPNORM_ALPHA = (2 * DEPTH) ** 0.25
QK_SCALE = HEAD_DIM ** -0.5 * math.log2(math.e)
INT_MIN = -(2 ** 31)
COUNT_STRIP = 512
RANK_PIECE = 256
ROW_UNROLL = 4

AB_LAYOUT = (
    ("q_a", A_HEADS * HEAD_DIM), ("k_a", HEAD_DIM), ("v_a", HEAD_DIM),
    ("q_idx", IDX_HEADS * IDX_DIM), ("k_idx", IDX_DIM), ("w_idx", IDX_HEADS),
    ("q_b", B_HEADS * HEAD_DIM),
    ("k_cmp", B_KV_GROUPS * HEAD_DIM), ("v_cmp", B_KV_GROUPS * HEAD_DIM),
    ("k_slc", B_KV_GROUPS * HEAD_DIM), ("v_slc", B_KV_GROUPS * HEAD_DIM),
    ("k_win", B_KV_GROUPS * HEAD_DIM), ("v_win", B_KV_GROUPS * HEAD_DIM),
    ("gate_b", 3 * B_HEADS),
)


def _params(*sem):
    return pltpu.CompilerParams(dimension_semantics=sem, vmem_limit_bytes=VMEM_LIMIT_BYTES)


def _mm(a, b):
    return jnp.dot(a, b, preferred_element_type=f32)


def _layer_norm_rows(y, g, b):
    mu = jnp.mean(y, axis=-1, keepdims=True)
    yc = y - mu
    var = jnp.mean(yc * yc, axis=-1, keepdims=True)
    return yc * lax.rsqrt(var + LN_EPS) * g + b


def _rope_tile(x, keep, hi, lo):
    return x * keep + pltpu.roll(x, LANES - ROT_DIM // 2, 1) * hi + pltpu.roll(x, ROT_DIM // 2, 1) * lo


def _tile(h, j):
    return h[:, j * LANES:(j + 1) * LANES]


AB_Q_TILES = 10
AB_QIDX_TILES = (4, 5)
AB_K_TILES = 3
AB_V_HEADS = 5
AB_MISC_TILE = 15
AB_CMP_TILE = 16
AB_TILES = 18


def _proj_pack_ab_kernel(x_ref, w_ref, coef_ref, q_ref, kt_ref, v_ref, misc_ref, cmp_ref):
    h = _mm(x_ref[...].astype(MXU_DTYPE), w_ref[...])
    keep, hi, lo = (coef_ref[:, j * LANES:(j + 1) * LANES] for j in range(3))
    for j in range(AB_Q_TILES):
        t = _rope_tile(_tile(h, j), keep, hi, lo)
        q_ref[:, j * LANES:(j + 1) * LANES] = (t if j in AB_QIDX_TILES else t * QK_SCALE).astype(q_ref.dtype)
    k = jnp.concatenate([_rope_tile(_tile(h, AB_Q_TILES + j), keep, hi, lo) for j in range(AB_K_TILES)], axis=1)
    kt_ref[...] = k.T.astype(kt_ref.dtype)
    low = lax.broadcasted_iota(i32, keep.shape, 1) < HEAD_DIM
    for j in range(AB_V_HEADS):
        t = _tile(h, AB_Q_TILES + AB_K_TILES + j // 2)
        t = pltpu.roll(t, HEAD_DIM, 1) if j % 2 else t
        v_ref[:, j * LANES:(j + 1) * LANES] = jnp.where(low, t, 1.0).astype(v_ref.dtype)
    misc_ref[...] = _tile(h, AB_MISC_TILE)
    cmp_ref[...] = h[:, AB_CMP_TILE * LANES:AB_TILES * LANES]


def project_pack_ab(x, w, coef, tm=512):
    bsz, seq, d = x.shape
    row = lambda width: pl.BlockSpec((None, tm, width), lambda b, i: (b, i, 0))
    kt_rows = AB_K_TILES * LANES
    return pl.pallas_call(
        _proj_pack_ab_kernel,
        grid=(bsz, seq // tm),
        in_specs=[row(d), pl.BlockSpec((d, AB_TILES * LANES), lambda b, i: (0, 0)), row(3 * LANES)],
        out_specs=[row(AB_Q_TILES * LANES), pl.BlockSpec((None, kt_rows, tm), lambda b, i: (b, 0, i)),
                   row(AB_V_HEADS * LANES), row(LANES), row(2 * LANES)],
        out_shape=[jax.ShapeDtypeStruct((bsz, seq, AB_Q_TILES * LANES), MXU_DTYPE),
                   jax.ShapeDtypeStruct((bsz, kt_rows, seq), MXU_DTYPE),
                   jax.ShapeDtypeStruct((bsz, seq, AB_V_HEADS * LANES), MXU_DTYPE),
                   jax.ShapeDtypeStruct((bsz, seq, LANES), f32),
                   jax.ShapeDtypeStruct((bsz, seq, 2 * LANES), f32)],
        compiler_params=_params("parallel", "parallel"),
        name="project_pack_ab",
    )(x, w, coef)


def _proj_pack_c_kernel(x_ref, w_ref, coef_ref, q_ref, kt_ref, v_ref):
    h = _mm(x_ref[...].astype(MXU_DTYPE), w_ref[...])
    keep, hi, lo = (coef_ref[:, j * LANES:(j + 1) * LANES] for j in range(3))
    n = q_ref.shape[1] // LANES
    for j in range(n):
        q_ref[:, j * LANES:(j + 1) * LANES] = (_rope_tile(_tile(h, j), keep, hi, lo) * QK_SCALE).astype(q_ref.dtype)
    k = jnp.concatenate([_rope_tile(_tile(h, n + j), keep, hi, lo) for j in range(n)], axis=1)
    kt_ref[...] = k.T.astype(kt_ref.dtype)
    ones = jnp.ones(keep.shape, v_ref.dtype)
    for j in range(n):
        v_ref[:, 2 * j * LANES:(2 * j + 1) * LANES] = _tile(h, 2 * n + j).astype(v_ref.dtype)
        v_ref[:, (2 * j + 1) * LANES:(2 * j + 2) * LANES] = ones


def project_pack_c(x, w, coef, tm=512):
    bsz, seq, d = x.shape
    width = w.shape[1] // 3
    row = lambda cols: pl.BlockSpec((None, tm, cols), lambda b, i: (b, i, 0))
    return pl.pallas_call(
        _proj_pack_c_kernel,
        grid=(bsz, seq // tm),
        in_specs=[row(d), pl.BlockSpec((d, 3 * width), lambda b, i: (0, 0)), row(3 * LANES)],
        out_specs=[row(width), pl.BlockSpec((None, width, tm), lambda b, i: (b, 0, i)), row(2 * width)],
        out_shape=[jax.ShapeDtypeStruct((bsz, seq, width), MXU_DTYPE),
                   jax.ShapeDtypeStruct((bsz, width, seq), MXU_DTYPE),
                   jax.ShapeDtypeStruct((bsz, seq, 2 * width), MXU_DTYPE)],
        compiler_params=_params("parallel", "parallel"),
        name="project_pack_c",
    )(x, w, coef)


def _proj_ln_kernel(a_ref, w_ref, res_ref, g_ref, b_ref, o_ref):
    h = _mm(a_ref[...].astype(MXU_DTYPE), w_ref[...])
    o_ref[...] = _layer_norm_rows(DEEPNORM_ALPHA * res_ref[...] + h, g_ref[...], b_ref[...])


def project_residual_ln(a, w, res, g, b, tm=512):
    m, k = a.shape
    n = w.shape[1]
    return pl.pallas_call(
        _proj_ln_kernel,
        grid=(m // tm,),
        in_specs=[pl.BlockSpec((tm, k), lambda i: (i, 0)), pl.BlockSpec((k, n), lambda i: (0, 0)),
                  pl.BlockSpec((tm, n), lambda i: (i, 0)),
                  pl.BlockSpec((1, n), lambda i: (0, 0)), pl.BlockSpec((1, n), lambda i: (0, 0))],
        out_specs=pl.BlockSpec((tm, n), lambda i: (i, 0)),
        out_shape=jax.ShapeDtypeStruct((m, n), f32),
        compiler_params=_params("parallel"),
        name="project_residual_ln",
    )(a, w, res, g.reshape(1, n), b.reshape(1, n))


def _swiglu_tile(xb, w1_ref, w3_ref, w2_ref):
    a = _mm(xb, w1_ref[...])
    h = (a * jax.nn.sigmoid(a)) * _mm(xb, w3_ref[...])
    return _mm(h.astype(MXU_DTYPE), w2_ref[...])


def _ffn_ln_kernel(x_ref, w1_ref, w3_ref, w2_ref, g_ref, b_ref, o_ref, xb_ref, acc_ref):
    f = pl.program_id(1)

    @pl.when(f == 0)
    def _():
        xb_ref[...] = x_ref[...].astype(MXU_DTYPE)
        acc_ref[...] = jnp.zeros_like(acc_ref)

    acc_ref[...] += _swiglu_tile(xb_ref[...], w1_ref, w3_ref, w2_ref)

    @pl.when(f == pl.num_programs(1) - 1)
    def _():
        o_ref[...] = _layer_norm_rows(DEEPNORM_ALPHA * x_ref[...] + acc_ref[...], g_ref[...], b_ref[...])


def ffn_residual_ln(x, w1, w3, w2, g, b, tm, tf):
    m, d = x.shape
    ff = w1.shape[1]
    resident = dict(pipeline_mode=pl.Buffered(1)) if tf == ff else {}
    return pl.pallas_call(
        _ffn_ln_kernel,
        grid=(m // tm, ff // tf),
        in_specs=[pl.BlockSpec((tm, d), lambda i, f: (i, 0)),
                  pl.BlockSpec((d, tf), lambda i, f: (0, f), **resident),
                  pl.BlockSpec((d, tf), lambda i, f: (0, f), **resident),
                  pl.BlockSpec((tf, d), lambda i, f: (f, 0), **resident),
                  pl.BlockSpec((1, d), lambda i, f: (0, 0)), pl.BlockSpec((1, d), lambda i, f: (0, 0))],
        out_specs=pl.BlockSpec((tm, d), lambda i, f: (i, 0)),
        out_shape=jax.ShapeDtypeStruct((m, d), f32),
        scratch_shapes=[pltpu.VMEM((tm, d), MXU_DTYPE), pltpu.VMEM((tm, d), f32)],
        compiler_params=_params("parallel", "arbitrary"),
        name="ffn_residual_ln",
    )(x, w1, w3, w2, g.reshape(1, d), b.reshape(1, d))


ROUTE_IDS = N_EXPERTS
ROUTE_GATES = N_EXPERTS + 2


def _router_kernel(x_ref, w_ref, o_ref, *, n_experts):
    logits = jnp.dot(x_ref[...], w_ref[...], preferred_element_type=f32, precision=lax.Precision.HIGHEST)
    lane = lax.broadcasted_iota(i32, logits.shape, 1).astype(f32)
    logits = jnp.where(lane < n_experts, logits, -jnp.inf)
    v1 = jnp.max(logits, axis=1, keepdims=True)
    i1 = jnp.min(jnp.where(logits == v1, lane, float(LANES)), axis=1, keepdims=True)
    rest = jnp.where(lane == i1, -jnp.inf, logits)
    v2 = jnp.max(rest, axis=1, keepdims=True)
    i2 = jnp.min(jnp.where(rest == v2, lane, float(LANES)), axis=1, keepdims=True)
    e2 = jnp.exp(v2 - v1)
    g1 = 1.0 / (1.0 + e2)
    g2 = e2 / (1.0 + e2)
    out = jnp.where(lane == ROUTE_IDS, i1, 0.0) + jnp.where(lane == ROUTE_IDS + 1, i2, 0.0)
    out = out + jnp.where(lane == ROUTE_GATES, g1, 0.0) + jnp.where(lane == ROUTE_GATES + 1, g2, 0.0)
    o_ref[...] = out


def route_top2(x, router_w, tm=512):
    m, d = x.shape
    n_experts = router_w.shape[1]
    w = jnp.zeros((d, LANES), f32).at[:, :n_experts].set(router_w)
    return pl.pallas_call(
        functools.partial(_router_kernel, n_experts=n_experts),
        grid=(m // tm,),
        in_specs=[pl.BlockSpec((tm, d), lambda i: (i, 0)), pl.BlockSpec((d, LANES), lambda i: (0, 0))],
        out_specs=pl.BlockSpec((tm, LANES), lambda i: (i, 0)),
        out_shape=jax.ShapeDtypeStruct((m, LANES), f32),
        compiler_params=_params("parallel"),
        name="route_top2",
    )(x, w)


def _moe_ln_kernel(tok_ref, gs_ref, off_ref, x_ref, w1_ref, w3_ref, w2_ref, g_ref, b_ref, o_ref,
                   xg_ref, xb_ref, y_ref, *, rt):
    c = pl.program_id(0)
    e = pl.program_id(1)
    f = pl.program_id(2)
    last_f = pl.num_programs(2) - 1
    start = off_ref[0, e]
    count = off_ref[0, e + 1] - start
    n_tiles = (count + rt - 1) // rt

    @pl.when((c == 0) & (e == 0) & (f == 0))
    def _():
        xg_ref[...] = jnp.zeros_like(xg_ref)

    @pl.when((e == 0) & (f == 0))
    def _():
        o_ref[...] = jnp.zeros_like(o_ref)

    def row_loop(body):
        def group(j, carry):
            for u in range(ROW_UNROLL):
                body(j * ROW_UNROLL + u)
            return carry

        def single(r, carry):
            body(r)
            return carry
        lax.fori_loop(0, count // ROW_UNROLL, group, 0)
        lax.fori_loop((count // ROW_UNROLL) * ROW_UNROLL, count, single, 0)

    @pl.when(f == 0)
    def _():
        def gather(r):
            t = tok_ref[0, start + r]
            xg_ref[pl.ds(r, 1), :] = x_ref[pl.ds(t, 1), :]
        row_loop(gather)

        def cast(j, carry):
            rows = pl.ds(pl.multiple_of(j * rt, rt), rt)
            xb_ref[rows, :] = xg_ref[rows, :].astype(MXU_DTYPE)
            return carry
        lax.fori_loop(0, n_tiles, cast, 0)

    def tile(j, carry):
        rows = pl.ds(pl.multiple_of(j * rt, rt), rt)
        y = _swiglu_tile(xb_ref[rows, :], w1_ref, w3_ref, w2_ref)

        @pl.when(f == 0)
        def _():
            y_ref[rows, :] = y

        @pl.when(f != 0)
        def _():
            y_ref[rows, :] += y
        return carry
    lax.fori_loop(0, n_tiles, tile, 0)

    @pl.when(f == last_f)
    def _():
        def updated(r):
            t = tok_ref[0, start + r]
            return t, o_ref[pl.ds(t, 1), :] + gs_ref[0, start + r] * y_ref[pl.ds(r, 1), :]

        def group(j, carry):
            rows = [updated(j * ROW_UNROLL + u) for u in range(ROW_UNROLL)]
            for t, row in rows:
                o_ref[pl.ds(t, 1), :] = row
            return carry

        def single(r, carry):
            t, row = updated(r)
            o_ref[pl.ds(t, 1), :] = row
            return carry
        lax.fori_loop(0, count // ROW_UNROLL, group, 0)
        lax.fori_loop((count // ROW_UNROLL) * ROW_UNROLL, count, single, 0)

    @pl.when((e == pl.num_programs(1) - 1) & (f == last_f))
    def _():
        o_ref[...] = _layer_norm_rows(DEEPNORM_ALPHA * x_ref[...] + o_ref[...], g_ref[...], b_ref[...])


def moe_residual_ln(x, routes, w1, w3, w2, g, b, tm, tf, rt=128):
    m, d = x.shape
    n_experts, _, ff = w1.shape
    n_chunks = m // tm
    ids = routes[:, ROUTE_IDS:ROUTE_IDS + TOP_K].astype(i32).reshape(n_chunks, tm * TOP_K)
    gts = routes[:, ROUTE_GATES:ROUTE_GATES + TOP_K].reshape(n_chunks, tm * TOP_K)
    order = jnp.argsort(ids, axis=1, stable=True).astype(i32)
    tok = order // TOP_K
    gs = jnp.take_along_axis(gts, order, axis=1)
    counts = jnp.sum(ids[:, :, None] == jnp.arange(n_experts, dtype=i32)[None, None, :], axis=1, dtype=i32)
    offs = jnp.concatenate([jnp.zeros((n_chunks, 1), i32), jnp.cumsum(counts, axis=1, dtype=i32)], axis=1)
    smem = lambda width: pl.BlockSpec((None, 1, width), lambda c, e, f: (c, 0, 0), memory_space=pltpu.SMEM)
    return pl.pallas_call(
        functools.partial(_moe_ln_kernel, rt=rt),
        grid=(n_chunks, n_experts, ff // tf),
        in_specs=[smem(tm * TOP_K), smem(tm * TOP_K), smem(n_experts + 1),
                  pl.BlockSpec((tm, d), lambda c, e, f: (c, 0), pipeline_mode=pl.Buffered(1)),
                  pl.BlockSpec((None, d, tf), lambda c, e, f: (e, 0, f)),
                  pl.BlockSpec((None, d, tf), lambda c, e, f: (e, 0, f)),
                  pl.BlockSpec((None, tf, d), lambda c, e, f: (e, f, 0)),
                  pl.BlockSpec((1, d), lambda c, e, f: (0, 0)), pl.BlockSpec((1, d), lambda c, e, f: (0, 0))],
        out_specs=pl.BlockSpec((tm, d), lambda c, e, f: (c, 0), pipeline_mode=pl.Buffered(1)),
        out_shape=jax.ShapeDtypeStruct((m, d), f32),
        scratch_shapes=[pltpu.VMEM((tm, d), f32), pltpu.VMEM((tm, d), MXU_DTYPE), pltpu.VMEM((tm, d), f32)],
        compiler_params=_params("arbitrary", "arbitrary", "arbitrary"),
        name="moe_residual_ln",
    )(tok[:, None, :], gs[:, None, :], offs[:, None, :], x, w1, w3, w2, g.reshape(1, d), b.reshape(1, d))


def _flash_reset(m_ref, acc_ref):
    m_ref[...] = jnp.full(m_ref.shape, NEG, f32)
    acc_ref[...] = jnp.zeros(acc_ref.shape, f32)


def _flash_rows(q, kt, bias, m_ref, alpha_ref, p_ref, rows):
    s = _mm(q, kt)
    if bias is not None:
        s = s + bias
    m_prev = m_ref[rows, :]
    m_next = jnp.maximum(m_prev, jnp.max(s, axis=1, keepdims=True))
    alpha_ref[rows, :] = jnp.exp2(m_prev - m_next)
    m_ref[rows, :] = m_next
    p_ref[rows, :s.shape[1]] = jnp.exp2(s - _lane_tile(m_next, s.shape[1] // LANES)).astype(p_ref.dtype)


def _flash_accumulate(v_aug, alpha_ref, p_ref, acc_ref):
    alpha = _lane_tile(alpha_ref[...], acc_ref.shape[-1] // LANES)
    acc_ref[...] = alpha * acc_ref[...] + _mm(p_ref[:, :v_aug.shape[0]], v_aug)


def _skewed_chunks(n, score, fold, carry):
    def pair(j, carry):
        c = 2 * j
        fold(c - 1, 1)
        carry = score(c, 0, carry)
        fold(c, 0)
        return score(c + 1, 1, carry)

    def single(c, carry):
        fold(c - 1, 1)
        return score(c, 0, carry)

    carry = lax.fori_loop(0, n // 2, pair, carry)
    return lax.fori_loop(2 * (n // 2), n, single, carry)


def _fold_last(n, fold):
    for buf in range(2):
        pl.when((n - 1) % 2 == buf)(functools.partial(fold, n - 1, buf))


def _lane_tile(x, reps):
    return x if reps == 1 else jnp.concatenate([x] * reps, axis=1)


def _row_slices(n_rows):
    return [slice(r * Q_BLOCK, (r + 1) * Q_BLOCK) for r in range(n_rows // Q_BLOCK)]


def _dsa_kernel(qi_ref, wi_ref, kit_ref, qa_ref, kat_ref, va_ref, o_ref,
                skey_ref, wrep_ref, m_ref, alpha_ref, acc_ref, p_ref, qh_ref, *, topk, kc):
    qb = Q_BLOCK
    i = pl.program_id(1)
    q0 = i * qb
    n_chunks = (q0 + qb + kc - 1) // kc
    t_row = q0 + lax.broadcasted_iota(i32, (qb, kc), 0)
    col = lax.broadcasted_iota(i32, (qb, kc), 1)
    wi = wi_ref[...]
    qi = qi_ref[...]
    qis = [qi[:, h * IDX_DIM:(h + 1) * IDX_DIM] for h in range(IDX_HEADS)]
    qa = qa_ref[...]
    for h in range(A_HEADS):
        qh_ref[h * qb:(h + 1) * qb, :] = qa[:, h * HEAD_DIM:(h + 1) * HEAD_DIM]
    for h in range(IDX_HEADS):
        wrep_ref[h] = jnp.broadcast_to(wi[:, h:h + 1], (qb, kc))

    def score_body(c, carry):
        off = pl.multiple_of(c * kc, kc)
        kt = kit_ref[:, pl.ds(off, kc)]
        s = jnp.zeros((qb, kc), f32)
        for h in range(IDX_HEADS):
            s = s + wrep_ref[h] * jnp.maximum(_mm(qis[h], kt), 0.0)
        s = jnp.where(col + off <= t_row, s, NEG)
        bits = pltpu.bitcast(s, i32)
        skey_ref[:, pl.ds(off, kc)] = jnp.where(bits < 0, bits ^ 0x7FFFFFFF, bits)
        return carry

    n_strips = (q0 + qb + COUNT_STRIP - 1) // COUNT_STRIP
    lax.fori_loop(0, n_strips * (COUNT_STRIP // kc), score_body, 0)

    def count_ge(cand):
        def body(c, acc):
            off = pl.multiple_of(c * COUNT_STRIP, COUNT_STRIP)
            for j in range(COUNT_STRIP // LANES):
                acc = acc + jnp.where(skey_ref[:, pl.ds(off + j * LANES, LANES)] >= cand, 1.0, 0.0)
            return acc
        acc = lax.fori_loop(0, n_strips, body, jnp.zeros((qb, LANES), f32))
        return jnp.sum(acc, axis=1, keepdims=True)

    thr = jnp.where(count_ge(jnp.zeros((qb, 1), i32)) >= topk, 0, INT_MIN).astype(i32)

    def bisect(b, thr):
        cand = thr + jnp.left_shift(jnp.int32(1), 30 - b)
        return jnp.where(count_ge(cand) >= topk, cand, thr)

    thr = lax.fori_loop(0, 31, bisect, thr)
    quota = topk - count_ge(thr + 1)

    rank_w = min(kc, RANK_PIECE)
    before = (lax.broadcasted_iota(i32, (rank_w, rank_w), 0) < lax.broadcasted_iota(i32, (rank_w, rank_w), 1))
    before = jnp.where(before, 1.0, 0.0).astype(MXU_DTYPE)
    _flash_reset(m_ref, acc_ref)
    alpha_ref[1] = jnp.zeros(alpha_ref.shape[1:], f32)
    p_ref[1] = jnp.zeros(p_ref.shape[1:], p_ref.dtype)
    heads = _row_slices(A_HEADS * qb)

    def fold(c, buf):
        off = pl.multiple_of(jnp.maximum(c, 0) * kc, kc)
        _flash_accumulate(va_ref[pl.ds(off, kc), :], alpha_ref.at[buf], p_ref.at[buf], acc_ref)

    def score(c, buf, ties_seen):
        off = pl.multiple_of(c * kc, kc)
        key = skey_ref[:, pl.ds(off, kc)]
        eq = key == thr
        eqf = jnp.where(eq, 1.0, 0.0)
        ranks = []
        for j in range(kc // rank_w):
            piece = eqf[:, j * rank_w:(j + 1) * rank_w]
            ranks.append(ties_seen + _mm(piece.astype(MXU_DTYPE), before))
            ties_seen = ties_seen + jnp.sum(piece, axis=1, keepdims=True)
        rank = jnp.concatenate(ranks, axis=1)
        sel = ((key > thr) | (eq & (rank < quota))) & (col + off <= t_row)
        bias = jnp.where(sel, 0.0, NEG)
        kt = kat_ref[:, pl.ds(off, kc)]
        for rows in heads:
            _flash_rows(qh_ref[rows, :], kt, bias, m_ref, alpha_ref.at[buf], p_ref.at[buf], rows)
        return ties_seen

    _skewed_chunks(n_chunks, score, fold, jnp.zeros((qb, 1), f32))
    _fold_last(n_chunks, fold)
    acc = acc_ref[...]
    o = acc[:, :HEAD_DIM] / acc[:, HEAD_DIM:]
    for h in range(A_HEADS):
        o_ref[:, h * HEAD_DIM:(h + 1) * HEAD_DIM] = o[h * qb:(h + 1) * qb, :]
    o_ref[:, A_HEADS * HEAD_DIM:] = jnp.zeros((qb, o_ref.shape[1] - A_HEADS * HEAD_DIM), f32)


QPK_QA, QPK_QIDX, QPK_QB = 0, 2, 3
KPK_KA, KPK_KIDX, KPK_KSLC, KPK_KWIN = 0, 1, 2, 4
VPK_VA, VPK_VSLC, VPK_VWIN = 0, 1, 3
OUT_NSA = 2


def dsa_attention(wi, qpk, kpk, vpk, topk, kc=512):
    bsz, seq, _ = qpk.shape
    nb = seq // Q_BLOCK
    rows = A_HEADS * Q_BLOCK
    width = A_HEADS * HEAD_DIM
    kern = functools.partial(_dsa_kernel, topk=topk, kc=kc)
    return pl.pallas_call(
        kern,
        grid=(bsz, nb),
        in_specs=[pl.BlockSpec((None, Q_BLOCK, IDX_HEADS * IDX_DIM), lambda b, i: (b, i, QPK_QIDX)),
                  pl.BlockSpec((None, Q_BLOCK, IDX_HEADS), lambda b, i: (b, i, 0)),
                  pl.BlockSpec((None, IDX_DIM, seq), lambda b, i: (b, KPK_KIDX, 0)),
                  pl.BlockSpec((None, Q_BLOCK, width), lambda b, i: (b, i, QPK_QA)),
                  pl.BlockSpec((None, HEAD_DIM, seq), lambda b, i: (b, KPK_KA, 0)),
                  pl.BlockSpec((None, seq, 2 * HEAD_DIM), lambda b, i: (b, 0, VPK_VA))],
        out_specs=pl.BlockSpec((None, Q_BLOCK, 2 * width), lambda b, i: (b, i, 0)),
        out_shape=jax.ShapeDtypeStruct((bsz, seq, 2 * width), f32),
        scratch_shapes=[pltpu.VMEM((Q_BLOCK, seq), i32), pltpu.VMEM((IDX_HEADS, Q_BLOCK, kc), f32),
                        pltpu.VMEM((rows, LANES), f32), pltpu.VMEM((2, rows, LANES), f32),
                        pltpu.VMEM((rows, 2 * HEAD_DIM), f32), pltpu.VMEM((2, rows, kc), MXU_DTYPE),
                        pltpu.VMEM((rows, HEAD_DIM), MXU_DTYPE)],
        compiler_params=_params("parallel", "arbitrary"),
        name="dsa_attention",
    )(qpk, wi, kpk, qpk, kpk, vpk)


def _compress_kernel(x_ref, pe_ref, w1_ref, w2_ref, o_ref):
    n_rows = o_ref.shape[1]
    parts = CMP_LEN // CMP_STRIDE
    acc = [[jnp.zeros((n_rows, CMP_HIDDEN), f32) for _ in range(parts)] for _ in range(B_KV_GROUPS)]
    for t in range(CMP_STRIDE):
        tok = x_ref[pl.ds(t, n_rows, stride=CMP_STRIDE), :]
        for g in range(B_KV_GROUPS):
            tok_g = tok[:, g * HEAD_DIM:(g + 1) * HEAD_DIM]
            for part in range(parts):
                l = part * CMP_STRIDE + t
                lhs = (tok_g + pe_ref[l:l + 1, :]).astype(MXU_DTYPE)
                acc[g][part] = acc[g][part] + _mm(lhs, w1_ref[l * HEAD_DIM:(l + 1) * HEAD_DIM, :])
    for g in range(B_KV_GROUPS):
        pre = acc[g][0] + pltpu.roll(acc[g][1], shift=n_rows - 1, axis=0)
        hid = pre * jax.nn.sigmoid(pre)
        o_ref[g] = _mm(hid.astype(MXU_DTYPE), w2_ref[...])


def nsa_compress(cmp, pe, w1, w2):
    bsz, seq, _ = cmp.shape
    n_rows = seq // CMP_STRIDE
    width = B_KV_GROUPS * HEAD_DIM
    return pl.pallas_call(
        _compress_kernel,
        grid=(bsz, 2),
        in_specs=[pl.BlockSpec((None, seq, width), lambda b, s: (b, 0, s)),
                  pl.BlockSpec((None, CMP_LEN, HEAD_DIM), lambda b, s: (s, 0, 0)),
                  pl.BlockSpec((None, CMP_LEN * HEAD_DIM, CMP_HIDDEN), lambda b, s: (s, 0, 0)),
                  pl.BlockSpec((None, CMP_HIDDEN, HEAD_DIM), lambda b, s: (s, 0, 0))],
        out_specs=pl.BlockSpec((None, None, B_KV_GROUPS, n_rows, HEAD_DIM), lambda b, s: (b, s, 0, 0, 0)),
        out_shape=jax.ShapeDtypeStruct((bsz, 2, B_KV_GROUPS, n_rows, HEAD_DIM), f32),
        compiler_params=_params("parallel", "parallel"),
        name="nsa_compress",
    )(cmp, pe, w1, w2)


def _nsa_kernel(qin_ref, graw_ref, kct_ref, vc_ref, kst_ref, vs_ref, kwt_ref, vw_ref, ovl_ref, _, o_ref,
                m_ref, alpha_ref, acc_ref, p_ref, pw_ref, q_ref, imp_ref, *, n_slc, n_sel, kc):
    qb = Q_BLOCK
    i = pl.program_id(2)
    q0 = i * qb
    n_cmp = kct_ref.shape[1]
    heads = _row_slices(B_REP * qb)
    qin = qin_ref[...]
    for r in range(B_REP):
        q_ref[r * qb:(r + 1) * qb, :] = qin[:, r * HEAD_DIM:(r + 1) * HEAD_DIM]

    def compressed(width):
        t_c = q0 + lax.broadcasted_iota(i32, (qb, width), 0)
        cmp_end = lax.broadcasted_iota(i32, (qb, width), 1) * CMP_STRIDE + (CMP_LEN - 1)
        vis = cmp_end <= t_c
        kct = kct_ref[:, :width]
        p_sum = jnp.zeros((qb, width), f32)
        for rows in heads:
            lc = jnp.where(vis, _mm(q_ref[rows, :], kct), NEG)
            ec = jnp.where(vis, jnp.exp2(lc - jnp.max(lc, axis=1, keepdims=True)), 0.0)
            den = jnp.sum(ec, axis=1, keepdims=True)
            p_c = ec / jnp.where(den > 0.0, den, 1.0)
            p_sum = p_sum + p_c
            acc_ref[rows, :HEAD_DIM] = _mm(p_c.astype(MXU_DTYPE), vc_ref[:width, :])
        imp_ref[...] = jnp.dot(p_sum, ovl_ref[:width, :], preferred_element_type=f32,
                               precision=lax.Precision.HIGHEST)

    if n_cmp % (2 * LANES) == 0:
        first_half_only = (q0 + qb) * 2 <= n_cmp * CMP_STRIDE
        pl.when(first_half_only)(functools.partial(compressed, n_cmp // 2))
        pl.when(jnp.logical_not(first_half_only))(functools.partial(compressed, n_cmp))
    else:
        compressed(n_cmp)
    o_c = acc_ref[:, :HEAD_DIM]

    imp = imp_ref[...].T
    t_q = q0 + lax.broadcasted_iota(i32, (LANES, qb), 1)
    blk = lax.broadcasted_iota(i32, (LANES, qb), 0)
    blk_t = t_q // SLC_LEN
    forced = (blk == 0) | (blk == blk_t) | (blk == blk_t - 1)
    imp = jnp.where(forced, FORCED_BOOST, imp)
    imp = jnp.where(blk * SLC_LEN <= t_q, imp, NEG)
    imp = jnp.where(blk < n_slc, imp, -jnp.inf)
    blk_f = blk.astype(f32)

    def pick(_, carry):
        imp, selm = carry
        best = jnp.max(imp, axis=0, keepdims=True)
        first = jnp.min(jnp.where(imp == best, blk_f, float(LANES)), axis=0, keepdims=True)
        hit = blk_f == first
        return jnp.where(hit, -jnp.inf, imp), jnp.where(hit, 1.0, selm)

    _, selm = lax.fori_loop(0, n_sel, pick, (imp, jnp.zeros((LANES, qb), f32)))
    selm = selm.T.astype(MXU_DTYPE)

    t_k = q0 + lax.broadcasted_iota(i32, (qb, kc), 0)
    col = lax.broadcasted_iota(i32, (qb, kc), 1)
    exp_row = lax.broadcasted_iota(i32, (LANES, kc), 0)
    exp_col = lax.broadcasted_iota(i32, (LANES, kc), 1)
    _flash_reset(m_ref, acc_ref)
    alpha_ref[1] = jnp.zeros(alpha_ref.shape[1:], f32)
    p_ref[1] = jnp.zeros(p_ref.shape[1:], p_ref.dtype)

    def fold(c, buf):
        off = pl.multiple_of(jnp.maximum(c, 0) * kc, kc)
        _flash_accumulate(vs_ref[pl.ds(off, kc), :], alpha_ref.at[buf], p_ref.at[buf], acc_ref)

    def score(c, buf, carry):
        off = pl.multiple_of(c * kc, kc)
        expand = jnp.where(exp_row == (exp_col + off) // SLC_LEN, 1.0, 0.0).astype(MXU_DTYPE)
        sel = (_mm(selm, expand) > 0.5) & (col + off <= t_k)
        bias = jnp.where(sel, 0.0, NEG)
        kt = kst_ref[:, pl.ds(off, kc)]
        for rows in heads:
            _flash_rows(q_ref[rows, :], kt, bias, m_ref, alpha_ref.at[buf], p_ref.at[buf], rows)
        return carry

    n_slc_chunks = (q0 + qb + kc - 1) // kc
    _skewed_chunks(n_slc_chunks, score, fold, 0)
    _fold_last(n_slc_chunks, fold)
    acc = acc_ref[...]
    o_s = acc[:, :HEAD_DIM] / acc[:, HEAD_DIM:]

    slab = WINDOW + qb
    w0 = pl.multiple_of(jnp.maximum(q0 - WINDOW, 0), qb)
    dist = (q0 + lax.broadcasted_iota(i32, (qb, slab), 0)) - (w0 + lax.broadcasted_iota(i32, (qb, slab), 1))
    bias = jnp.where((dist >= 0) & (dist < WINDOW), 0.0, NEG)
    kt = kwt_ref[:, pl.ds(w0, slab)]
    for rows in heads:
        s = _mm(q_ref[rows, :], kt) + bias
        pw_ref[rows, :] = jnp.exp2(s - jnp.max(s, axis=1, keepdims=True)).astype(pw_ref.dtype)
    acc = _mm(pw_ref[...], vw_ref[pl.ds(w0, slab), :])
    o_w = acc[:, :HEAD_DIM] / acc[:, HEAD_DIM:]

    gates = jax.nn.sigmoid(graw_ref[...])
    for r, rows in enumerate(heads):
        g_c, g_s, g_w = (gates[:, 3 * r + n:3 * r + n + 1] for n in range(3))
        o_ref[:, r * HEAD_DIM:(r + 1) * HEAD_DIM] = g_c * o_c[rows] + g_s * o_s[rows] + g_w * o_w[rows]


def nsa_attention(qpk, graw, kct, vc, kpk, vpk, ovl, out_ab, n_slc, n_sel, kc=512):
    bsz, seq, _ = qpk.shape
    groups = B_KV_GROUPS
    nb = seq // Q_BLOCK
    rows = B_REP * Q_BLOCK
    width = B_REP * HEAD_DIM
    n_cmp = kct.shape[-1]
    kern = functools.partial(_nsa_kernel, n_slc=n_slc, n_sel=n_sel, kc=kc)
    per_bg = lambda b, g, i: (b, g, 0, 0)
    return pl.pallas_call(
        kern,
        grid=(bsz, groups, nb),
        in_specs=[pl.BlockSpec((None, Q_BLOCK, width), lambda b, g, i: (b, i, QPK_QB + g)),
                  pl.BlockSpec((None, None, Q_BLOCK, B_REP * 3), lambda b, g, i: (b, g, i, 0)),
                  pl.BlockSpec((None, None, HEAD_DIM, n_cmp), per_bg),
                  pl.BlockSpec((None, None, n_cmp, HEAD_DIM), per_bg),
                  pl.BlockSpec((None, HEAD_DIM, seq), lambda b, g, i: (b, KPK_KSLC + g, 0)),
                  pl.BlockSpec((None, seq, 2 * HEAD_DIM), lambda b, g, i: (b, 0, VPK_VSLC + g)),
                  pl.BlockSpec((None, HEAD_DIM, seq), lambda b, g, i: (b, KPK_KWIN + g, 0)),
                  pl.BlockSpec((None, seq, 2 * HEAD_DIM), lambda b, g, i: (b, 0, VPK_VWIN + g)),
                  pl.BlockSpec((n_cmp, LANES), lambda b, g, i: (0, 0)),
                  pl.BlockSpec(memory_space=pl.ANY)],
        out_specs=pl.BlockSpec((None, Q_BLOCK, width), lambda b, g, i: (b, i, OUT_NSA + g)),
        out_shape=jax.ShapeDtypeStruct(out_ab.shape, f32),
        input_output_aliases={9: 0},
        scratch_shapes=[pltpu.VMEM((rows, LANES), f32), pltpu.VMEM((2, rows, LANES), f32),
                        pltpu.VMEM((rows, 2 * HEAD_DIM), f32), pltpu.VMEM((2, rows, kc), MXU_DTYPE),
                        pltpu.VMEM((rows, WINDOW + Q_BLOCK), MXU_DTYPE), pltpu.VMEM((rows, HEAD_DIM), MXU_DTYPE),
                        pltpu.VMEM((Q_BLOCK, LANES), f32)],
        compiler_params=_params("parallel", "parallel", "arbitrary"),
        name="nsa_attention",
    )(qpk, graw, kct, vc, kpk, vpk, kpk, vpk, ovl, out_ab)


def _diff_kernel(lam_ref, qin_ref, kt_ref, v_ref, g_ref, o_ref, m_ref, alpha_ref, acc_ref, p_ref, q_ref,
                 *, tq, kc, out_scale):
    i = pl.program_id(2)
    q0 = i * tq
    dv = v_ref.shape[-1] // 2
    groups = _row_slices(tq)
    qin = qin_ref[...]
    for half in range(2):
        q_ref[half] = qin[:, half * HEAD_DIM:(half + 1) * HEAD_DIM]
    for half in range(2):
        _flash_reset(m_ref.at[half], acc_ref.at[half])
    alpha_ref[1] = jnp.zeros(alpha_ref.shape[1:], f32)
    p_ref[1] = jnp.zeros(p_ref.shape[1:], p_ref.dtype)

    def fold(c, buf, sub=(0, kc // tq)):
        off = pl.multiple_of(jnp.maximum(c, 0) * kc + sub[0] * tq, tq)
        v = v_ref[pl.ds(off, sub[1] * tq), :]
        for half in range(2):
            _flash_accumulate(v, alpha_ref.at[buf, half], p_ref.at[buf, half], acc_ref.at[half])

    def score(c, buf, masked, sub=(0, kc // tq)):
        off = pl.multiple_of(c * kc + sub[0] * tq, tq)
        width = sub[1] * tq
        for half in range(2):
            kt = kt_ref[half * HEAD_DIM:(half + 1) * HEAD_DIM, pl.ds(off, width)]
            for r, rows in enumerate(groups):
                bias = None
                if masked:
                    key = off + lax.broadcasted_iota(i32, (Q_BLOCK, width), 1)
                    t = q0 + r * Q_BLOCK + lax.broadcasted_iota(i32, (Q_BLOCK, width), 0)
                    bias = jnp.where(key <= t, 0.0, NEG)
                _flash_rows(q_ref[half, rows, :], kt, bias, m_ref.at[half], alpha_ref.at[buf, half],
                            p_ref.at[buf, half], rows)

    def score_full(c, buf, carry):
        score(c, buf, False)
        return carry

    n_full = q0 // kc
    _skewed_chunks(n_full, score_full, fold, 0)

    own = (q0 - n_full * kc) // tq

    def finish(buf, own_piece):
        fold(n_full - 1, 1 - buf)
        if own_piece:
            score(n_full, buf, False, (0, own_piece))
            fold(n_full, buf, (0, own_piece))
            buf = 1 - buf
        score(n_full, buf, True, (own_piece, 1))
        fold(n_full, buf, (own_piece, 1))

    for buf in range(2):
        for own_piece in range(kc // tq):
            pl.when((n_full % 2 == buf) & (own == own_piece))(functools.partial(finish, buf, own_piece))
    a1 = acc_ref[0]
    a2 = acc_ref[1]
    o = a1[:, :dv] / a1[:, dv:] - lam_ref[0] * (a2[:, :dv] / a2[:, dv:])
    o = o * lax.rsqrt(jnp.mean(o * o, axis=-1, keepdims=True) + LN_EPS) * g_ref[...]
    o_ref[...] = o * out_scale


def diff_attention(lam, q, kt, v, subln_g, out_scale, tq=512, kc=1024):
    bsz, seq, width = q.shape
    dv = 2 * HEAD_DIM
    heads = width // dv
    tq = min(tq, seq)
    kc = min(kc, seq)
    assert kc % tq == 0 and seq % kc == 0
    kern = functools.partial(_diff_kernel, tq=tq, kc=kc, out_scale=out_scale)
    return pl.pallas_call(
        kern,
        grid=(bsz, heads, seq // tq),
        in_specs=[pl.BlockSpec(memory_space=pltpu.SMEM),
                  pl.BlockSpec((None, tq, dv), lambda b, h, i: (b, i, h)),
                  pl.BlockSpec((None, dv, seq), lambda b, h, i: (b, h, 0)),
                  pl.BlockSpec((None, seq, 2 * dv), lambda b, h, i: (b, 0, h)),
                  pl.BlockSpec((1, dv), lambda b, h, i: (0, 0))],
        out_specs=pl.BlockSpec((None, tq, dv), lambda b, h, i: (b, i, h)),
        out_shape=jax.ShapeDtypeStruct((bsz, seq, heads * dv), f32),
        scratch_shapes=[pltpu.VMEM((2, tq, LANES), f32), pltpu.VMEM((2, 2, tq, LANES), f32),
                        pltpu.VMEM((2, tq, 2 * dv), f32), pltpu.VMEM((2, 2, tq, kc), MXU_DTYPE),
                        pltpu.VMEM((2, tq, HEAD_DIM), MXU_DTYPE)],
        compiler_params=_params("parallel", "parallel", "arbitrary"),
        name="diff_attention",
    )(lam, q, kt, v, subln_g.reshape(1, dv))


def _rope_tables(positions):
    inv_freq = ROPE_THETA ** (-jnp.arange(0, ROT_DIM, 2, dtype=f32) / ROT_DIM)
    ang = positions.astype(f32)[..., None] * inv_freq
    return jnp.cos(ang), jnp.sin(ang)


def _apply_rope(x, cos, sin):
    shape = cos.shape[:2] + (1,) * (x.ndim - 3) + cos.shape[-1:]
    c = cos.reshape(shape)
    s = sin.reshape(shape)
    half = ROT_DIM // 2
    x1, x2 = x[..., :half], x[..., half:ROT_DIM]
    return jnp.concatenate([x1 * c - x2 * s, x2 * c + x1 * s, x[..., ROT_DIM:]], axis=-1)


def _rope_coefficients(cos, sin):
    half = ROT_DIM // 2
    spread = np.zeros((ROT_DIM, 3 * LANES), np.float32)
    offset = np.zeros((3 * LANES,), np.float32)
    for lane in range(LANES):
        d = lane % HEAD_DIM
        if d >= ROT_DIM:
            offset[lane] = 1.0
        elif d < half:
            spread[d, lane], spread[half + d, LANES + lane] = 1.0, -1.0
        else:
            spread[d - half, lane], spread[d, 2 * LANES + lane] = 1.0, 1.0
    table = jnp.concatenate([cos, sin], axis=-1)
    return jnp.dot(table, spread, precision=lax.Precision.HIGHEST) + offset


def _ab_weight_columns(w_in):
    widths = dict(AB_LAYOUT)
    starts = dict(zip(widths, np.cumsum([0] + [w for _, w in AB_LAYOUT[:-1]]).tolist()))
    cols = lambda names: np.concatenate([np.arange(starts[n], starts[n] + widths[n]) for n in names])
    first = ("q_a", "q_idx", "q_b", "k_a", "k_idx", "k_slc", "k_win", "v_a", "v_slc", "v_win", "gate_b", "w_idx")
    n_first = sum(widths[n] for n in first)
    pad = AB_CMP_TILE * LANES - n_first
    assert 0 <= pad < HEAD_DIM
    w = jnp.concatenate([w_in[:, cols(first)], jnp.zeros((w_in.shape[0], pad), w_in.dtype),
                         w_in[:, cols(("k_cmp", "v_cmp"))]], axis=1)
    assert w.shape[1] == AB_TILES * LANES
    return w.astype(MXU_DTYPE)


def _overlap_matrix(n_cmp_rows, n_slc):
    c_start = np.arange(n_cmp_rows) * CMP_STRIDE
    s_start = np.arange(LANES) * SLC_LEN
    ovl = (c_start[:, None] < s_start[None, :] + SLC_LEN) & (c_start[:, None] + CMP_LEN > s_start[None, :])
    ovl = ovl & (np.arange(LANES)[None, :] < n_slc)
    return jnp.asarray(ovl.astype(np.float32))


def _ab_mixer(x2, bsz, seq, positions, coef, w_in, pe_k, pe_v, ck1, ck2, cv1, cv2):
    qpk, kpk, vpk, misc, cmp = project_pack_ab(x2.reshape(bsz, seq, -1), _ab_weight_columns(w_in), coef)
    gate_w = B_HEADS * 3
    g_b = misc[..., HEAD_DIM:HEAD_DIM + gate_w]
    w_idx = misc[..., HEAD_DIM + gate_w:HEAD_DIM + gate_w + IDX_HEADS] * (IDX_HEADS * IDX_DIM) ** -0.5

    out_ab = dsa_attention(w_idx, qpk, kpk, vpk, topk=min(DSA_TOPK, seq // 4))

    groups = B_KV_GROUPS
    n_rows = seq // CMP_STRIDE
    n_cmp = (seq - CMP_LEN) // CMP_STRIDE + 1
    assert n_cmp == n_rows - 1
    n_slc = seq // SLC_LEN
    assert n_slc <= LANES

    kv_c = nsa_compress(cmp, jnp.stack([pe_k, pe_v]), jnp.stack([ck1, cv1]).astype(MXU_DTYPE),
                        jnp.stack([ck2, cv2]).astype(MXU_DTYPE))
    k_c, v_c = kv_c[:, 0], kv_c[:, 1]
    cmp_end = jnp.minimum(jnp.arange(n_rows) * CMP_STRIDE + CMP_LEN - 1, seq - 1)
    cos_c, sin_c = _rope_tables(positions[:, cmp_end])
    k_c = _apply_rope(k_c.transpose(0, 2, 1, 3), cos_c, sin_c)
    k_c = k_c.transpose(0, 2, 3, 1).astype(MXU_DTYPE)
    v_c = v_c.astype(MXU_DTYPE)

    graw = g_b.reshape(bsz, seq, groups, B_REP * 3).transpose(0, 2, 1, 3)
    out_ab = nsa_attention(qpk, graw, k_c, v_c, kpk, vpk, _overlap_matrix(n_rows, n_slc), out_ab,
                           n_slc=n_slc, n_sel=min(SLC_TOPN, n_slc))
    return out_ab.reshape(bsz * seq, (A_HEADS + B_HEADS) * HEAD_DIM)


def _diff_mixer(x2, bsz, seq, coef, w_in, lq1, lk1, lq2, lk2, subln_g, lam_init):
    q, kt, v = project_pack_c(x2.reshape(bsz, seq, -1), w_in.astype(MXU_DTYPE), coef)
    lam =(jnp.exp(jnp.sum(lq1 * lk1)) - jnp.exp(jnp.sum(lq2 * lk2)) + lam_init).reshape(1).astype(f32)
    o = diff_attention(lam, q, kt, v, subln_g, 1.0 - lam_init)
    return o.reshape(bsz * seq, C_HEADS * 2 * HEAD_DIM)


def kernel(x, positions, ab_w_in, cmp_pe_k, cmp_pe_v, cmp_k_w1, cmp_k_w2, cmp_v_w1, cmp_v_w2, ab_w_out, ln_ab_g, ln_ab_b, ffn_w1, ffn_w3, ffn_w2, ln_ffn_g, ln_ffn_b, c_w_in, lambda_q1, lambda_k1, lambda_q2, lambda_k2, c_subln_g, c_w_out, ln_c_g, ln_c_b, router_w, moe_w1, moe_w3, moe_w2, ln_moe_g, ln_moe_b):
    bsz, seq, d = x.shape
    assert seq % COUNT_STRIP == 0 and seq >= WINDOW + Q_BLOCK and d == D_MODEL
    coef = _rope_coefficients(*_rope_tables(positions))
    x2 = x.reshape(bsz * seq, d)
    for layer in range(DEPTH):
        i = layer // 2
        if layer % 2 == 0:
            o = _ab_mixer(x2, bsz, seq, positions, coef, ab_w_in[i], cmp_pe_k[i], cmp_pe_v[i],
                          cmp_k_w1[i], cmp_k_w2[i], cmp_v_w1[i], cmp_v_w2[i])
            x2 = project_residual_ln(o, ab_w_out[i].astype(MXU_DTYPE), x2, ln_ab_g[i], ln_ab_b[i])
            x2 = ffn_residual_ln(x2, ffn_w1[i].astype(MXU_DTYPE), ffn_w3[i].astype(MXU_DTYPE),
                                 ffn_w2[i].astype(MXU_DTYPE), ln_ffn_g[i], ln_ffn_b[i], tm=512, tf=ffn_w1.shape[-1])
        else:
            lam_init = 0.8 - 0.6 * math.exp(-0.3 * layer)
            o = _diff_mixer(x2, bsz, seq, coef, c_w_in[i], lambda_q1[i], lambda_k1[i], lambda_q2[i],
                            lambda_k2[i], c_subln_g[i], lam_init)
            x2 = project_residual_ln(o, c_w_out[i].astype(MXU_DTYPE), x2, ln_c_g[i], ln_c_b[i])
            routes = route_top2(x2, router_w[i])
            x2 = moe_residual_ln(x2, routes, moe_w1[i].astype(MXU_DTYPE), moe_w3[i].astype(MXU_DTYPE),
                                 moe_w2[i].astype(MXU_DTYPE), ln_moe_g[i], ln_moe_b[i], tm=min(2048, bsz * seq // 2), tf=896, rt=128)
    return x2.reshape(bsz, seq, d)
```
